```python
import math
import jax, jax.numpy as jnp
from jax import lax
import numpy as np

D_MODEL = 2048
BATCH = 4
SEQ = 4096
DEPTH = 2

MEM_LEN = 256
Q_BLOCK = 128
NORM_EPS = 1e-6
ROPE_THETA = 10000.0

RWKV_HEADS = 16
RWKV_HEAD_DIM = 64
RWKV_DIM = RWKV_HEADS * RWKV_HEAD_DIM
RWKV_W_RANK = 64
RWKV_A_RANK = 64
RWKV_G_RANK = 128
RWKV_V_RANK = 32
RWKV_LN_EPS = 64e-5
RWKV_COLS = 3 * RWKV_DIM + RWKV_W_RANK + RWKV_A_RANK + RWKV_G_RANK

DIFF_HEADS = 4
DIFF_HEAD_DIM = 64
DIFF_V_DIM = 2 * DIFF_HEAD_DIM
DIFF_QK = DIFF_HEADS * 2 * DIFF_HEAD_DIM
DIFF_DIM = DIFF_HEADS * DIFF_V_DIM
DIFF_COLS = 2 * DIFF_QK + DIFF_DIM
DIFF_SUBLN_EPS = 1e-5

MLA_HEADS = 4
MLA_Q_RANK = 384
MLA_KV_RANK = 256
MLA_NOPE_DIM = 128
MLA_ROPE_DIM = 64
MLA_V_DIM = 128
MLA_DIM = MLA_HEADS * MLA_V_DIM
MLA_COLS = MLA_Q_RANK + MLA_KV_RANK + MLA_ROPE_DIM

MIX_DIM = RWKV_DIM + DIFF_DIM + MLA_DIM
IN_COLS = RWKV_COLS + DIFF_COLS + MLA_COLS

REL_BUCKETS = 32
REL_MAX_DIST = 128

CA_HEADS = 4
CA_HEAD_DIM = 128
CA_DIM = CA_HEADS * CA_HEAD_DIM

N_GROUPS = 4
EXPERTS_PER_GROUP = 8
N_EXPERTS = N_GROUPS * EXPERTS_PER_GROUP
D_EXPERT = 512
TOP_K_IN_GROUP = 2

kernel_name = 'hybrid_rwkv7_diffattn_mla_hmoe'


def rmsnorm(x, w, eps=NORM_EPS):
    x32 = x.astype(jnp.float32)
    y = x32 * lax.rsqrt(jnp.mean(x32 * x32, axis=-1, keepdims=True) + eps)
    return (y * w.astype(jnp.float32)).astype(x.dtype)


def token_shift(p, mu):
    prev = jnp.pad(p, ((0, 0), (1, 0), (0, 0)))[:, :-1]
    return p + mu * (prev - p)


def rope(x, positions):
    half = x.shape[-1] // 2
    inv_freq = ROPE_THETA ** (-jnp.arange(half, dtype=jnp.float32) / half)
    ang = positions.astype(jnp.float32)[..., None] * inv_freq
    ang = ang.reshape(ang.shape[:2] + (1,) * (x.ndim - 3) + (half,))
    cos, sin = jnp.cos(ang), jnp.sin(ang)
    x1 = x[..., :half].astype(jnp.float32)
    x2 = x[..., half:].astype(jnp.float32)
    return jnp.concatenate([x1 * cos - x2 * sin, x2 * cos + x1 * sin], axis=-1).astype(x.dtype)


def t5_bucket(rel):
    n = jnp.maximum(rel, 0)
    max_exact = REL_BUCKETS // 2
    large = max_exact + (jnp.log(jnp.maximum(n, max_exact).astype(jnp.float32) / max_exact)
                         / math.log(REL_MAX_DIST / max_exact) * (REL_BUCKETS - max_exact)).astype(jnp.int32)
    large = jnp.minimum(large, REL_BUCKETS - 1)
    return jnp.where(n < max_exact, n, large)


def t5_bias(pos_q, pos_k, table):
    bucket = t5_bucket(pos_q[:, :, None] - pos_k[:, None, :])
    return jnp.moveaxis(table[bucket].astype(jnp.float32), -1, 1)


def rwkv7_mix(p, mu, w0, w2, a0, a2, g2, k_k, k_a, r_k, ln_w, ln_b, vres):
    B, S, _ = p.shape
    f32 = jnp.float32
    xs = token_shift(p, mu)
    c3 = 3 * RWKV_DIM
    r = xs[..., :RWKV_DIM]
    k = xs[..., RWKV_DIM:2 * RWKV_DIM]
    v = xs[..., 2 * RWKV_DIM:c3]
    wl = xs[..., c3:c3 + RWKV_W_RANK]
    al = xs[..., c3 + RWKV_W_RANK:c3 + RWKV_W_RANK + RWKV_A_RANK]
    gl = xs[..., c3 + RWKV_W_RANK + RWKV_A_RANK:]
    w_log = -jax.nn.softplus(-(w0 + jnp.tanh(wl) @ w2).astype(f32)) - 0.5
    decay = jnp.exp(-jnp.exp(w_log))
    a = jax.nn.sigmoid((a0 + al @ a2).astype(f32))
    g = jax.nn.sigmoid(gl) @ g2
    heads = lambda t: t.reshape(B, S, RWKV_HEADS, RWKV_HEAD_DIM)
    kk = heads((k * k_k).astype(f32))
    kk = kk / jnp.maximum(jnp.linalg.norm(kk, axis=-1, keepdims=True), 1e-12)
    k = k * (1.0 + (a.astype(k.dtype) - 1.0) * k_a)
    v_raw = v
    if vres is not None:
        v_first, p_vl, mu_vl, v0, v2 = vres
        v = v + (v_first - v) * jax.nn.sigmoid(v0 + token_shift(p_vl, mu_vl) @ v2)
    seq = lambda t: jnp.moveaxis(heads(t.astype(f32)), 1, 0)

    def step(state, inp):
        r_t, w_t, k_t, v_t, kk_t, a_t = inp
        sa = jnp.einsum('bhvk,bhk->bhv', state, -kk_t)
        state = (state * w_t[:, :, None, :]
                 + sa[..., None] * (kk_t * a_t)[:, :, None, :]
                 + v_t[..., None] * k_t[:, :, None, :])
        return state, jnp.einsum('bhvk,bhk->bhv', state, r_t)

    state0 = jnp.zeros((B, RWKV_HEADS, RWKV_HEAD_DIM, RWKV_HEAD_DIM), f32)
    _, o_seq = lax.scan(step, state0, (seq(r), seq(decay), seq(k), seq(v), jnp.moveaxis(kk, 1, 0), seq(a)))
    y = jnp.moveaxis(o_seq, 0, 1)
    mean = jnp.mean(y, axis=-1, keepdims=True)
    var = jnp.mean(jnp.square(y - mean), axis=-1, keepdims=True)
    y = ((y - mean) * lax.rsqrt(var + RWKV_LN_EPS)).reshape(B, S, RWKV_DIM)
    y = y * ln_w.astype(f32) + ln_b.astype(f32)
    bonus = jnp.sum(heads(r).astype(f32) * heads(k).astype(f32) * r_k.astype(f32), axis=-1, keepdims=True) * heads(v).astype(f32)
    out = (y + bonus.reshape(B, S, RWKV_DIM)).astype(p.dtype) * g
    return out, v_raw


def diff_attention(p, positions, rel_bias, lam, lambda_init, subln_w):
    B, S, _ = p.shape
    f32 = jnp.float32
    q = p[..., :DIFF_QK].reshape(B, S, DIFF_HEADS, 2, DIFF_HEAD_DIM)
    k = p[..., DIFF_QK:2 * DIFF_QK].reshape(B, S, DIFF_HEADS, 2, DIFF_HEAD_DIM)
    v = p[..., 2 * DIFF_QK:].reshape(B, S, DIFF_HEADS, DIFF_V_DIM)
    scale = DIFF_HEAD_DIM ** -0.5
    nb = S // Q_BLOCK
    key_idx = jnp.arange(S)
    q_blk = jnp.moveaxis(q.reshape(B, nb, Q_BLOCK, DIFF_HEADS, 2, DIFF_HEAD_DIM), 1, 0)
    pos_blk = jnp.moveaxis(positions.reshape(B, nb, Q_BLOCK), 1, 0)
    idx_blk = key_idx.reshape(nb, Q_BLOCK)

    def block(args):
        qb, pb, ib = args
        s = jnp.einsum('bqhmd,bkhmd->bhmqk', qb, k).astype(f32) * scale
        s = s + t5_bias(pb, positions, rel_bias)[:, :, None]
        s = jnp.where(ib[:, None] >= key_idx[None, :], s, -jnp.inf)
        pr = jax.nn.softmax(s, axis=-1)
        attn = pr[:, :, 0] - lam * pr[:, :, 1]
        return jnp.einsum('bhqk,bkhd->bqhd', attn.astype(v.dtype), v)

    out = lax.map(block, (q_blk, pos_blk, idx_blk))
    out = jnp.moveaxis(out, 0, 1).reshape(B, S, DIFF_HEADS, DIFF_V_DIM)
    out = rmsnorm(out, subln_w, DIFF_SUBLN_EPS) * (1.0 - lambda_init)
    return out.reshape(B, S, DIFF_DIM)


def mla_attention(p, positions, q_norm, wq_b, kv_norm, wkv_b):
    B, S, _ = p.shape
    f32 = jnp.float32
    q_c = p[..., :MLA_Q_RANK]
    kv_c = p[..., MLA_Q_RANK:MLA_Q_RANK + MLA_KV_RANK]
    k_pe = rope(p[..., MLA_Q_RANK + MLA_KV_RANK:], positions)
    q = (rmsnorm(q_c, q_norm) @ wq_b).reshape(B, S, MLA_HEADS, MLA_NOPE_DIM + MLA_ROPE_DIM)
    q_nope = q[..., :MLA_NOPE_DIM]
    q_pe = rope(q[..., MLA_NOPE_DIM:], positions)
    kv = (rmsnorm(kv_c, kv_norm) @ wkv_b).reshape(B, S, MLA_HEADS, MLA_NOPE_DIM + MLA_V_DIM)
    k_nope = kv[..., :MLA_NOPE_DIM]
    v = kv[..., MLA_NOPE_DIM:]
    scale = (MLA_NOPE_DIM + MLA_ROPE_DIM) ** -0.5
    nb = S // Q_BLOCK
    key_idx = jnp.arange(S)
    qn_blk = jnp.moveaxis(q_nope.reshape(B, nb, Q_BLOCK, MLA_HEADS, MLA_NOPE_DIM), 1, 0)
    qp_blk = jnp.moveaxis(q_pe.reshape(B, nb, Q_BLOCK, MLA_HEADS, MLA_ROPE_DIM), 1, 0)
    idx_blk = key_idx.reshape(nb, Q_BLOCK)

    def block(args):
        qn, qp, ib = args
        s = (jnp.einsum('bqhd,bkhd->bhqk', qn, k_nope)
             + jnp.einsum('bqhr,bkr->bhqk', qp, k_pe)).astype(f32) * scale
        s = jnp.where(ib[:, None] >= key_idx[None, :], s, -jnp.inf)
        pr = jax.nn.softmax(s, axis=-1)
        return jnp.einsum('bhqk,bkhd->bqhd', pr.astype(v.dtype), v)

    out = lax.map(block, (qn_blk, qp_blk, idx_blk))
    return jnp.moveaxis(out, 0, 1).reshape(B, S, MLA_DIM)


def memory_cross_attention(h, m, wq, wkv, wo):
    B, S, _ = h.shape
    M = m.shape[1]
    q = (h @ wq).reshape(B, S, CA_HEADS, CA_HEAD_DIM)
    kv = (m @ wkv).reshape(B, M, 2, CA_HEADS, CA_HEAD_DIM)
    s = jnp.einsum('bqhd,bkhd->bhqk', q, kv[:, :, 0]).astype(jnp.float32) * CA_HEAD_DIM ** -0.5
    pr = jax.nn.softmax(s, axis=-1)
    o = jnp.einsum('bhqk,bkhd->bqhd', pr.astype(h.dtype), kv[:, :, 1]).reshape(B, S, CA_DIM)
    return o @ wo


def hier_moe(h, w_group, b_group, w_expert, b_expert, w_gate, w_up, w_down):
    B, S, D = h.shape
    f32 = jnp.float32
    T = B * S
    t = h.reshape(T, D)
    g_logits = (t @ w_group).astype(f32)
    g_probs = jax.nn.softmax(g_logits, axis=-1)
    g_sel = jnp.argmax(g_logits + b_group.astype(f32), axis=-1)
    p_group = jnp.take_along_axis(g_probs, g_sel[:, None], axis=1)
    e_logits = (t @ w_expert).astype(f32).reshape(T, N_GROUPS, EXPERTS_PER_GROUP)
    e_logits = e_logits[jnp.arange(T), g_sel]
    e_bias = b_expert.astype(f32).reshape(N_GROUPS, EXPERTS_PER_GROUP)[g_sel]
    _, e_idx = lax.top_k(e_logits + e_bias, TOP_K_IN_GROUP)
    e_w = jax.nn.softmax(jnp.take_along_axis(e_logits, e_idx, axis=1), axis=-1)
    within = jnp.sum(jax.nn.one_hot(e_idx, EXPERTS_PER_GROUP, dtype=f32) * e_w[..., None], axis=1)
    gates = (jax.nn.one_hot(g_sel, N_GROUPS, dtype=f32)[:, :, None] * within[:, None, :]).reshape(T, N_EXPERTS) * p_group

    def expert_step(acc, xs):
        wg, wu, wd, gate = xs
        hid = jax.nn.silu(t @ wg) * (t @ wu)
        return acc + gate[:, None].astype(t.dtype) * (hid @ wd), None

    y, _ = lax.scan(expert_step, jnp.zeros_like(t), (w_gate, w_up, w_down, gates.T))
    return y.reshape(B, S, D)


def setup_inputs(seed: int = 0) -> dict:
    key = jax.random.key(seed)
    keys = jax.random.split(key, 64)
    ks = [keys[i] for i in range(64)]
    f32 = jnp.float32

    def nrm(shape, scale):
        return jax.random.normal(ks.pop(), shape, f32) * scale

    def gain(shape):
        return 1.0 + nrm(shape, 0.02)

    def unif(shape):
        return jax.random.uniform(ks.pop(), shape, f32)

    L = DEPTH
    Lv = DEPTH - 1
    D = D_MODEL
    x = nrm((BATCH, SEQ, D), 1.0)
    mem = nrm((BATCH, MEM_LEN, D), 1.0)
    offsets = jax.random.randint(ks.pop(), (BATCH, 1), 0, 1024, dtype=jnp.int32)
    positions = offsets + jnp.arange(SEQ, dtype=jnp.int32)[None, :]
    return {
        'x': x,
        'mem': mem,
        'positions': positions,
        'rel_bias': nrm((REL_BUCKETS, DIFF_HEADS), 0.5),
        'final_norm': gain((D,)),
        'norm_mix': gain((L, D)),
        'w_in': nrm((L, D, IN_COLS), D ** -0.5),
        'w_in_vres': nrm((Lv, D, RWKV_V_RANK), D ** -0.5),
        'w_out': nrm((L, MIX_DIM, D), MIX_DIM ** -0.5),
        'tm_mu': unif((L, RWKV_COLS)),
        'tm_mu_vres': unif((Lv, RWKV_V_RANK)),
        'tm_w0': -1.5 + nrm((L, RWKV_DIM), 0.5),
        'tm_w2': nrm((L, RWKV_W_RANK, RWKV_DIM), 0.1),
        'tm_a0': nrm((L, RWKV_DIM), 0.1),
        'tm_a2': nrm((L, RWKV_A_RANK, RWKV_DIM), RWKV_A_RANK ** -0.5),
        'tm_v0': nrm((Lv, RWKV_DIM), 0.1),
        'tm_v2': nrm((Lv, RWKV_V_RANK, RWKV_DIM), RWKV_V_RANK ** -0.5),
        'tm_g2': nrm((L, RWKV_G_RANK, RWKV_DIM), RWKV_G_RANK ** -0.5),
        'tm_k_k': 0.85 + nrm((L, RWKV_DIM), 0.05),
        'tm_k_a': 1.0 + nrm((L, RWKV_DIM), 0.05),
        'tm_r_k': nrm((L, RWKV_HEADS, RWKV_HEAD_DIM), 0.1),
        'tm_ln_w': gain((L, RWKV_DIM)),
        'tm_ln_b': nrm((L, RWKV_DIM), 0.02),
        'da_lq1': nrm((L, DIFF_HEAD_DIM), 0.1),
        'da_lk1': nrm((L, DIFF_HEAD_DIM), 0.1),
        'da_lq2': nrm((L, DIFF_HEAD_DIM), 0.1),
        'da_lk2': nrm((L, DIFF_HEAD_DIM), 0.1),
        'da_subln': gain((L, DIFF_V_DIM)),
        'mla_q_norm': gain((L, MLA_Q_RANK)),
        'mla_wq_b': nrm((L, MLA_Q_RANK, MLA_HEADS * (MLA_NOPE_DIM + MLA_ROPE_DIM)), MLA_Q_RANK ** -0.5),
        'mla_kv_norm': gain((L, MLA_KV_RANK)),
        'mla_wkv_b': nrm((L, MLA_KV_RANK, MLA_HEADS * (MLA_NOPE_DIM + MLA_V_DIM)), MLA_KV_RANK ** -0.5),
        'norm_cross': gain((L, D)),
        'norm_mem': gain((L, D)),
        'ca_wq': nrm((L, D, CA_DIM), D ** -0.5),
        'ca_wkv': nrm((L, D, 2 * CA_DIM), D ** -0.5),
        'ca_wo': nrm((L, CA_DIM, D), CA_DIM ** -0.5),
        'norm_ffn': gain((L, D)),
        'moe_w_group': nrm((L, D, N_GROUPS), D ** -0.5),
        'moe_b_group': nrm((L, N_GROUPS), 0.01),
        'moe_w_expert': nrm((L, D, N_EXPERTS), D ** -0.5),
        'moe_b_expert': nrm((L, N_EXPERTS), 0.01),
        'moe_w_gate': nrm((L, N_EXPERTS, D, D_EXPERT), D ** -0.5),
        'moe_w_up': nrm((L, N_EXPERTS, D, D_EXPERT), D ** -0.5),
        'moe_w_down': nrm((L, N_EXPERTS, D_EXPERT, D), D_EXPERT ** -0.5),
    }


def reference(x, mem, positions, rel_bias, final_norm, norm_mix, w_in, w_in_vres, w_out,
              tm_mu, tm_mu_vres, tm_w0, tm_w2, tm_a0, tm_a2, tm_v0, tm_v2, tm_g2,
              tm_k_k, tm_k_a, tm_r_k, tm_ln_w, tm_ln_b,
              da_lq1, da_lk1, da_lq2, da_lk2, da_subln,
              mla_q_norm, mla_wq_b, mla_kv_norm, mla_wkv_b,
              norm_cross, norm_mem, ca_wq, ca_wkv, ca_wo,
              norm_ffn, moe_w_group, moe_b_group, moe_w_expert, moe_b_expert,
              moe_w_gate, moe_w_up, moe_w_down):
    f32 = jnp.float32
    v_first = None
    for l in range(DEPTH):
        h = rmsnorm(x, norm_mix[l])
        if l == 0:
            proj = h @ w_in[l]
            vres = None
        else:
            proj = h @ jnp.concatenate([w_in[l], w_in_vres[l - 1]], axis=1)
            vres = (v_first, proj[..., IN_COLS:], tm_mu_vres[l - 1], tm_v0[l - 1], tm_v2[l - 1])
        y_a, v_l = rwkv7_mix(proj[..., :RWKV_COLS], tm_mu[l], tm_w0[l], tm_w2[l], tm_a0[l], tm_a2[l],
                             tm_g2[l], tm_k_k[l], tm_k_a[l], tm_r_k[l], tm_ln_w[l], tm_ln_b[l], vres)
        if l == 0:
            v_first = v_l
        lambda_init = 0.8 - 0.6 * math.exp(-0.3 * l)
        lam = (jnp.exp(jnp.sum(da_lq1[l].astype(f32) * da_lk1[l].astype(f32)))
               - jnp.exp(jnp.sum(da_lq2[l].astype(f32) * da_lk2[l].astype(f32))) + lambda_init)
        y_b = diff_attention(proj[..., RWKV_COLS:RWKV_COLS + DIFF_COLS], positions, rel_bias,
                             lam, lambda_init, da_subln[l])
        y_c = mla_attention(proj[..., RWKV_COLS + DIFF_COLS:IN_COLS], positions,
                            mla_q_norm[l], mla_wq_b[l], mla_kv_norm[l], mla_wkv_b[l])
        x = x + jnp.concatenate([y_a, y_b, y_c], axis=-1) @ w_out[l]
        x = x + memory_cross_attention(rmsnorm(x, norm_cross[l]), rmsnorm(mem, norm_mem[l]),
                                       ca_wq[l], ca_wkv[l], ca_wo[l])
        x = x + hier_moe(rmsnorm(x, norm_ffn[l]), moe_w_group[l], moe_b_group[l], moe_w_expert[l],
                         moe_b_expert[l], moe_w_gate[l], moe_w_up[l], moe_w_down[l])
    return rmsnorm(x, final_norm)
```

```python
import functools
import math

import jax
import jax.numpy as jnp
from jax import lax
from jax.experimental import pallas as pl
from jax.experimental.pallas import tpu as pltpu

F32 = jnp.float32
BF16 = jnp.bfloat16

NORM_EPS = 1e-6
ROPE_THETA = 10000.0

RW_HEADS = 16
RW_HEAD_DIM = 64
RW_DIM = RW_HEADS * RW_HEAD_DIM
RW_W_RANK = 64
RW_A_RANK = 64
RW_G_RANK = 128
RW_V_RANK = 32
RW_LORA = RW_W_RANK + RW_A_RANK + RW_G_RANK
RW_LN_EPS = 64e-5
RW_COLS = 3 * RW_DIM + RW_LORA

DF_HEADS = 4
DF_HEAD_DIM = 64
DF_V_DIM = 2 * DF_HEAD_DIM
DF_QK = DF_HEADS * 2 * DF_HEAD_DIM
DF_DIM = DF_HEADS * DF_V_DIM
DF_COLS = 2 * DF_QK + DF_DIM
DF_SUBLN_EPS = 1e-5

ML_HEADS = 4
ML_Q_RANK = 384
ML_KV_RANK = 256
ML_NOPE = 128
ML_ROPE = 64
ML_V = 128
ML_DIM = ML_HEADS * ML_V
ML_COLS = ML_Q_RANK + ML_KV_RANK + ML_ROPE

REL_BUCKETS = 32
REL_MAX_DIST = 128

CA_HEADS = 4
CA_HEAD_DIM = 128
CA_DIM = CA_HEADS * CA_HEAD_DIM

MOE_GROUPS = 4
MOE_PER_GROUP = 8
MOE_EXPERTS = MOE_GROUPS * MOE_PER_GROUP

LANES = 128
SCAN_CHUNK = 64
SCAN_GROUP = 4
VMEM_LIMIT = 56 * 1024 * 1024
NEG_BIG = -1e30

OFF_R = 0
OFF_K = RW_DIM
OFF_V = 2 * RW_DIM
OFF_LORA = 3 * RW_DIM
OFF_VRES = OFF_LORA + RW_LORA
OFF_KPE = OFF_VRES + LANES
OFF_DQ = OFF_KPE + LANES
OFF_DK = OFF_DQ + DF_QK
OFF_DV = OFF_DK + DF_QK
OFF_MKV = OFF_DV + DF_DIM
OFF_MQ = OFF_MKV + ML_KV_RANK
PROJ_COLS = OFF_MQ + ML_Q_RANK


def _cparams(sem, vmem=VMEM_LIMIT):
    return pltpu.CompilerParams(dimension_semantics=sem, vmem_limit_bytes=vmem)


def _dot(a, b):
    return jnp.dot(a, b, preferred_element_type=F32)


def _dot_t(a, b):
    return lax.dot_general(a, b, (((1,), (1,)), ((), ())), preferred_element_type=F32)


def _split3(x):
    hi = x.astype(BF16)
    r1 = x - hi.astype(F32)
    mid = r1.astype(BF16)
    lo = (r1 - mid.astype(F32)).astype(BF16)
    return hi, mid, lo


def _dot_exact_rhs01(x, ones_bf16):
    h, m, l = _split3(x)
    return _dot(h, ones_bf16) + _dot(m, ones_bf16) + _dot(l, ones_bf16)


def _dot_x3(a, b):
    ah = a.astype(BF16)
    al = (a - ah.astype(F32)).astype(BF16)
    bh = b.astype(BF16)
    bl = (b - bh.astype(F32)).astype(BF16)
    return _dot(ah, bh) + _dot(ah, bl) + _dot(al, bh)


def _rms(x, w, eps):
    ms = jnp.mean(x * x, axis=-1, keepdims=True)
    return x * lax.rsqrt(ms + eps) * w


def _rmsnorm_kernel(x_ref, w_ref, o_ref, *, eps):
    o_ref[...] = _rms(x_ref[...], w_ref[...], eps).astype(o_ref.dtype)


def rmsnorm(x, w, *, eps=NORM_EPS, out_dtype=F32, tm=512):
    m, d = x.shape
    tm = min(tm, m)
    return pl.pallas_call(
        functools.partial(_rmsnorm_kernel, eps=eps),
        grid=(m // tm,),
        in_specs=[pl.BlockSpec((tm, d), lambda i: (i, 0)),
                  pl.BlockSpec((1, d), lambda i: (0, 0))],
        out_specs=pl.BlockSpec((tm, d), lambda i: (i, 0)),
        out_shape=jax.ShapeDtypeStruct((m, d), out_dtype),
        compiler_params=_cparams(("parallel",)),
        name="rmsnorm",
    )(x, w.reshape(1, d))


def _norm_matmul_kernel(x_ref, nw_ref, w_ref, o_ref, xn_ref, *, eps):
    @pl.when(pl.program_id(1) == 0)
    def _():
        xn_ref[...] = _rms(x_ref[...], nw_ref[...], eps).astype(BF16)

    o_ref[...] = _dot(xn_ref[...], w_ref[...]).astype(o_ref.dtype)


def norm_matmul(x, nw, w, *, out_dtype=F32, tm=512, tn=None, eps=NORM_EPS):
    m, d = x.shape
    n = w.shape[1]
    tm = min(tm, m)
    tn = n if tn is None else tn
    return pl.pallas_call(
        functools.partial(_norm_matmul_kernel, eps=eps),
        grid=(m // tm, n // tn),
        in_specs=[pl.BlockSpec((tm, d), lambda i, j: (i, 0)),
                  pl.BlockSpec((1, d), lambda i, j: (0, 0)),
                  pl.BlockSpec((d, tn), lambda i, j: (0, j))],
        out_specs=pl.BlockSpec((tm, tn), lambda i, j: (i, j)),
        out_shape=jax.ShapeDtypeStruct((m, n), out_dtype),
        scratch_shapes=[pltpu.VMEM((tm, d), BF16)],
        compiler_params=_cparams(("parallel", "arbitrary")),
        name="norm_matmul",
    )(x, nw.reshape(1, d), w)


def _matmul_res_kernel(*refs, n_a):
    a_refs, w_refs = refs[:n_a], refs[n_a:2 * n_a]
    res_ref, o_ref = refs[2 * n_a], refs[2 * n_a + 1]
    acc = res_ref[...]
    for a_ref, w_ref in zip(a_refs, w_refs):
        acc = acc + _dot(a_ref[...].astype(BF16), w_ref[...])
    o_ref[...] = acc


def matmul_res(a_list, w_list, res, *, tm=512, tn=1024):
    m, n = res.shape
    tm = min(tm, m)
    tn = min(tn, n)
    n_a = len(a_list)
    in_specs = ([pl.BlockSpec((tm, a.shape[1]), lambda i, j: (i, 0)) for a in a_list]
                + [pl.BlockSpec((w.shape[0], tn), lambda i, j: (0, j)) for w in w_list]
                + [pl.BlockSpec((tm, tn), lambda i, j: (i, j))])
    return pl.pallas_call(
        functools.partial(_matmul_res_kernel, n_a=n_a),
        grid=(m // tm, n // tn),
        in_specs=in_specs,
        out_specs=pl.BlockSpec((tm, tn), lambda i, j: (i, j)),
        out_shape=jax.ShapeDtypeStruct((m, n), F32),
        compiler_params=_cparams(("parallel", "arbitrary")),
        name="matmul_res",
    )(*a_list, *w_list, res)


def _softplus(z):
    return jnp.maximum(z, 0.0) + jnp.log(1.0 + jnp.exp(-jnp.abs(z)))


def _rwkv_prep_kernel(*refs, has_vres):
    if has_vres:
        (pr_ref, pk_ref, pv_ref, pl_ref, pvr_ref, vfirst_ref,
         mu_r, mu_k, mu_v, mu_l, mu_vr, w0, w2, a0, a2, g2, v0, v2,
         k_k, k_a, r_k, seg, seg_t,
         r_o, lw_o, k_o, v_o, kap_o, beta_o, g_o, bonus_o,
         last_r, last_k, last_v, last_l, last_vr) = refs
    else:
        (pr_ref, pk_ref, pv_ref, pl_ref,
         mu_r, mu_k, mu_v, mu_l, w0, w2, a0, a2, g2,
         k_k, k_a, r_k, seg, seg_t,
         r_o, lw_o, k_o, v_o, kap_o, beta_o, g_o, bonus_o,
         last_r, last_k, last_v, last_l) = refs
    t = pl.program_id(1)

    def shifted(p_ref, last_ref, mu_ref):
        p = p_ref[...]
        n = p.shape[0]
        carried = jnp.where(t == 0, 0.0, last_ref[0:1, :])
        row = lax.broadcasted_iota(jnp.int32, p.shape, 0)
        prev = jnp.where(row == 0, carried, pltpu.roll(p, 1, axis=0))
        last_ref[0:1, :] = p[n - 1:n, :]
        return p + mu_ref[...] * (prev - p)

    r = shifted(pr_ref, last_r, mu_r)
    k = shifted(pk_ref, last_k, mu_k)
    v = shifted(pv_ref, last_v, mu_v)
    lora = shifted(pl_ref, last_l, mu_l)
    wl = lora[:, :LANES]
    gl = lora[:, LANES:]

    lane = lax.broadcasted_iota(jnp.int32, wl.shape, 1)
    wl_t = jnp.where(lane < RW_W_RANK, jnp.tanh(wl), 0.0)
    al = jnp.where(lane >= RW_W_RANK, wl, 0.0)
    w_log = -_softplus(-(w0[...] + _dot_x3(wl_t, w2[...]))) - 0.5
    lw_o[...] = -jnp.exp(w_log)
    a = jax.nn.sigmoid(a0[...] + _dot_x3(al, a2[...]))
    g_o[...] = _dot(jax.nn.sigmoid(gl).astype(BF16), g2[...])

    segm, segm_t = seg[...], seg_t[...]

    def head_sum(x):
        return _dot_exact_rhs01(_dot_exact_rhs01(x, segm), segm_t)

    kk = k * k_k[...]
    kk = kk / jnp.maximum(jnp.sqrt(head_sum(kk * kk)), 1e-12)
    k = k * (1.0 + (a - 1.0) * k_a[...])
    if has_vres:
        vr = shifted(pvr_ref, last_vr, mu_vr)
        mix = jax.nn.sigmoid(v0[...] + _dot_x3(vr, v2[...]))
        v = v + (vfirst_ref[...] - v) * mix
    r_o[...] = r
    k_o[...] = k
    v_o[...] = v
    kap_o[...] = kk
    beta_o[...] = kk * a
    bonus_o[...] = head_sum(r * k * r_k[...]) * v


def rwkv_prep(proj, batch, vfirst, prm, *, tt=256):
    tokens = proj.shape[0]
    seq = tokens // batch
    tt = min(tt, seq)
    nt = seq // tt
    has_vres = vfirst is not None
    d = RW_DIM

    def rows(width, col):
        return pl.BlockSpec((tt, width), lambda b, t, col=col: (b * nt + t, col))

    def full(shape):
        return pl.BlockSpec(shape, lambda b, t: (0, 0))

    in_specs = [rows(d, OFF_R // d), rows(d, OFF_K // d), rows(d, OFF_V // d),
                rows(RW_LORA, OFF_LORA // RW_LORA)]
    args = [proj, proj, proj, proj]
    if has_vres:
        in_specs += [rows(LANES, OFF_VRES // LANES), rows(d, 0)]
        args += [proj, vfirst]
    names = ["mu_r", "mu_k", "mu_v", "mu_l"] + (["mu_vr"] if has_vres else []) + ["w0", "w2", "a0", "a2", "g2"]
    names += (["v0", "v2"] if has_vres else []) + ["k_k", "k_a", "r_k", "seg", "seg_t"]
    for nm in names:
        in_specs.append(full(prm[nm].shape))
        args.append(prm[nm])
    out_spec = pl.BlockSpec((tt, d), lambda b, t: (b * nt + t, 0))
    scratch = [pltpu.VMEM((8, d), F32)] * 3 + [pltpu.VMEM((8, RW_LORA), F32)]
    if has_vres:
        scratch.append(pltpu.VMEM((8, LANES), F32))
    return pl.pallas_call(
        functools.partial(_rwkv_prep_kernel, has_vres=has_vres),
        grid=(batch, nt),
        in_specs=in_specs,
        out_specs=[out_spec] * 8,
        out_shape=[jax.ShapeDtypeStruct((tokens, d), F32)] * 8,
        scratch_shapes=scratch,
        compiler_params=_cparams(("arbitrary", "arbitrary")),
        name="rwkv_prep",
    )(*args)


def _rwkv_scan_kernel(r_ref, lw_ref, k_ref, v_ref, kap_ref, beta_ref, tril_ref, bmask_ref,
                      o_ref, h_ref):
    @pl.when(pl.program_id(2) == 0)
    def _():
        h_ref[...] = jnp.zeros_like(h_ref)

    c, w = lw_ref.shape
    g = w // RW_HEAD_DIM
    rr = g * c
    lw = lw_ref[...]
    cum = _dot_exact_lhs01(tril_ref[...], lw)
    total = cum[c - 1:c, :]
    p_in = jnp.exp(cum)
    p_ex = jnp.exp(cum - lw)
    p_inv = jnp.exp(-cum)
    p_rem = jnp.exp(total - cum)
    bmask = bmask_ref[...]

    def stack(x):
        return (jnp.concatenate([x] * g, axis=0) * bmask).astype(BF16)

    kap, beta, k = kap_ref[...], beta_ref[...], k_ref[...]
    a_s = stack(-kap * p_ex)
    r_s = stack(r_ref[...] * p_in)
    b_s = stack(beta * p_inv)
    k_s = stack(k * p_inv)
    v_s = stack(v_ref[...])

    row = lax.broadcasted_iota(jnp.int32, (rr, rr), 0)
    col = lax.broadcasted_iota(jnp.int32, (rr, rr), 1)
    strict = row > col
    incl = row >= col
    ab = jnp.where(strict, _dot_t(a_s, b_s), 0.0)
    ak = jnp.where(strict, _dot_t(a_s, k_s), 0.0)
    rb = jnp.where(incl, _dot_t(r_s, b_s), 0.0)
    rk = jnp.where(incl, _dot_t(r_s, k_s), 0.0)

    h = h_ref[...]
    hb = h.astype(BF16)
    x = _dot(a_s, hb) + _dot(ak.astype(BF16), v_s)
    lp = ab
    n_sq = int(math.log2(c))
    for i in range(n_sq):
        lpb = lp.astype(BF16)
        x = x + _dot(lpb, x.astype(BF16))
        if i < n_sq - 1:
            lp = _dot(lpb, lpb)
    u = x.astype(BF16)
    ow = _dot(r_s, hb) + _dot(rb.astype(BF16), u) + _dot(rk.astype(BF16), v_s)
    o = ow[0:c]
    for i in range(1, g):
        o = o + ow[i * c:(i + 1) * c]
    o_ref[...] = o

    eye = (lax.broadcasted_iota(jnp.int32, (w, w), 0) == lax.broadcasted_iota(jnp.int32, (w, w), 1))
    diag = jnp.where(eye, jnp.exp(total), 0.0)
    p_col = _dot_exact_rhs01(diag, jnp.ones((w, w), BF16))
    z = jnp.concatenate([jnp.concatenate([beta * p_rem] * g, axis=0) * bmask,
                         jnp.concatenate([k * p_rem] * g, axis=0) * bmask], axis=0)
    uv = jnp.concatenate([u, v_s], axis=0)
    h_ref[...] = h * p_col + _dot(z.T.astype(BF16), uv)


def _dot_exact_lhs01(ones_bf16, x):
    h, m, l = _split3(x)
    return _dot(ones_bf16, h) + _dot(ones_bf16, m) + _dot(ones_bf16, l)


def rwkv_scan(r, lw, k, v, kap, beta, batch):
    tokens, d = r.shape
    seq = tokens // batch
    c = min(SCAN_CHUNK, seq)
    nc = seq // c
    gw = SCAN_GROUP * RW_HEAD_DIM
    rr = SCAN_GROUP * c
    tril = (jnp.arange(c)[:, None] >= jnp.arange(c)[None, :]).astype(BF16)
    bmask = (jnp.arange(rr)[:, None] // c == jnp.arange(gw)[None, :] // RW_HEAD_DIM).astype(F32)
    blk = pl.BlockSpec((c, gw), lambda b, g, i: (b * nc + i, g))
    return pl.pallas_call(
        _rwkv_scan_kernel,
        grid=(batch, d // gw, nc),
        in_specs=[blk] * 6 + [pl.BlockSpec((c, c), lambda b, g, i: (0, 0)),
                              pl.BlockSpec((rr, gw), lambda b, g, i: (0, 0))],
        out_specs=blk,
        out_shape=jax.ShapeDtypeStruct((tokens, d), F32),
        scratch_shapes=[pltpu.VMEM((gw, gw), F32)],
        compiler_params=_cparams(("arbitrary", "arbitrary", "arbitrary")),
        name="rwkv_scan",
    )(r, lw, k, v, kap, beta, tril, bmask)


def _rwkv_post_kernel(o_ref, bonus_ref, g_ref, lnw_ref, lnb_ref, seg, seg_t, y_ref):
    segm, segm_t = seg[...], seg_t[...]

    def head_mean(x):
        return _dot_exact_rhs01(_dot_exact_rhs01(x, segm), segm_t) * (1.0 / RW_HEAD_DIM)

    o = o_ref[...]
    dlt = o - head_mean(o)
    var = head_mean(dlt * dlt)
    y = dlt * lax.rsqrt(var + RW_LN_EPS) * lnw_ref[...] + lnb_ref[...]
    y_ref[...] = ((y + bonus_ref[...]) * g_ref[...]).astype(y_ref.dtype)


def rwkv_post(o, bonus, g, ln_w, ln_b, seg, seg_t, *, tm=512):
    tokens, d = o.shape
    tm = min(tm, tokens)
    blk = pl.BlockSpec((tm, d), lambda i: (i, 0))
    vec = pl.BlockSpec((1, d), lambda i: (0, 0))
    return pl.pallas_call(
        _rwkv_post_kernel,
        grid=(tokens // tm,),
        in_specs=[blk, blk, blk, vec, vec,
                  pl.BlockSpec(seg.shape, lambda i: (0, 0)), pl.BlockSpec(seg_t.shape, lambda i: (0, 0))],
        out_specs=blk,
        out_shape=jax.ShapeDtypeStruct((tokens, d), BF16),
        compiler_params=_cparams(("parallel",)),
        name="rwkv_post",
    )(o, bonus, g, ln_w.reshape(1, d), ln_b.reshape(1, d), seg, seg_t)


def _t5_thresholds():
    max_exact = REL_BUCKETS // 2
    thr = list(range(1, max_exact))
    n = max_exact
    for bucket in range(max_exact, REL_BUCKETS):
        while True:
            large = max_exact + int(math.log(max(n, max_exact) / max_exact)
                                    / math.log(REL_MAX_DIST / max_exact) * (REL_BUCKETS - max_exact))
            if min(large, REL_BUCKETS - 1) >= bucket:
                break
            n += 1
        thr.append(n)
    return thr


T5_THRESHOLDS = _t5_thresholds()
T5_FAR = T5_THRESHOLDS[-1]


def _diff_attn_kernel(qfirst_ref, klast_ref, q_ref, k_ref, v_ref, qpos_ref, kpos_ref, subln_ref,
                      table_ref, lam_ref, o_ref, m_ref, l_ref, acc_ref, *, tq, tk, scale, out_scale):
    b, h, i, j = pl.program_id(0), pl.program_id(1), pl.program_id(2), pl.program_id(3)
    nq, nk = pl.num_programs(2), pl.num_programs(3)

    @pl.when(j == 0)
    def _():
        m_ref[...] = jnp.full_like(m_ref, NEG_BIG)
        l_ref[...] = jnp.zeros_like(l_ref)
        acc_ref[...] = jnp.zeros_like(acc_ref)

    def update(bias, masked):
        q = q_ref[...]
        kb = k_ref[...].astype(BF16)
        vb = v_ref[...].astype(BF16)
        lane = lax.broadcasted_iota(jnp.int32, q.shape, 1)
        if masked:
            qi = i * tq + lax.broadcasted_iota(jnp.int32, (tq, tk), 0)
            kj = j * tk + lax.broadcasted_iota(jnp.int32, (tq, tk), 1)
            keep = qi >= kj
        for mi in range(2):
            in_map = (lane >= mi * DF_HEAD_DIM) & (lane < (mi + 1) * DF_HEAD_DIM)
            qm = jnp.where(in_map, q, 0.0).astype(BF16)
            s = _dot_t(qm, kb) * scale + bias
            if masked:
                s = jnp.where(keep, s, NEG_BIG)
            m_old = m_ref[mi]
            m_new = jnp.maximum(m_old, jnp.max(s, axis=-1, keepdims=True))
            alpha = jnp.exp(m_old - m_new)
            p = jnp.exp(s - m_new)
            l_ref[mi] = alpha * l_ref[mi] + jnp.sum(p, axis=-1, keepdims=True)
            acc_ref[mi] = alpha * acc_ref[mi] + _dot(p.astype(BF16), vb)
            m_ref[mi] = m_new

    active = j * tk <= i * tq + (tq - 1)
    far = (qfirst_ref[b * nq + i] - klast_ref[b * nk + j] >= T5_FAR) & (j * tk + (tk - 1) <= i * tq)

    @pl.when(active & far)
    def _():
        update(table_ref[(REL_BUCKETS - 1) * DF_HEADS + h], masked=False)

    @pl.when(active & jnp.logical_not(far))
    def _():
        n = jnp.maximum(qpos_ref[...] - kpos_ref[...], 0)
        bias = jnp.full((tq, tk), table_ref[h], F32)
        for bucket, thr in enumerate(T5_THRESHOLDS, start=1):
            bias = jnp.where(n >= thr, table_ref[bucket * DF_HEADS + h], bias)
        update(bias, masked=True)

    @pl.when(j == nk - 1)
    def _():
        d = acc_ref[0] / l_ref[0] - lam_ref[0] * (acc_ref[1] / l_ref[1])
        o_ref[...] = (_rms(d, subln_ref[...], DF_SUBLN_EPS) * out_scale).astype(o_ref.dtype)


def diff_attention(proj, positions, rel_bias, lam, lambda_init, subln_w, *, tq=256):
    batch, seq = positions.shape
    tokens = batch * seq
    tq = min(tq, seq)
    tk = tq
    nq, nk = seq // tq, seq // tk
    qfirst = positions[:, ::tq].reshape(-1)
    klast = positions[:, tk - 1::tk].reshape(-1)
    qpos = positions.reshape(batch, seq, 1)
    kpos = positions.reshape(batch, 1, seq)
    w = DF_V_DIM

    def q_map(b, h, i, j, *_):
        return (b * nq + i, OFF_DQ // w + h)

    def kv_map(off):
        def f(b, h, i, j, *_):
            return (b * nk + jnp.minimum(j, (i * tq + tq - 1) // tk), off // w + h)
        return f

    grid_spec = pltpu.PrefetchScalarGridSpec(
        num_scalar_prefetch=2,
        grid=(batch, DF_HEADS, nq, nk),
        in_specs=[pl.BlockSpec((tq, w), q_map),
                  pl.BlockSpec((tk, w), kv_map(OFF_DK)),
                  pl.BlockSpec((tk, w), kv_map(OFF_DV)),
                  pl.BlockSpec((None, tq, 1), lambda b, h, i, j, *_: (b, i, 0)),
                  pl.BlockSpec((None, 1, tk),
                               lambda b, h, i, j, *_: (b, 0, jnp.minimum(j, (i * tq + tq - 1) // tk))),
                  pl.BlockSpec((1, w), lambda b, h, i, j, *_: (0, 0)),
                  pl.BlockSpec(memory_space=pltpu.SMEM),
                  pl.BlockSpec(memory_space=pltpu.SMEM)],
        out_specs=pl.BlockSpec((tq, w), lambda b, h, i, j, *_: (b * nq + i, h)),
        scratch_shapes=[pltpu.VMEM((2, tq, 1), F32), pltpu.VMEM((2, tq, 1), F32),
                        pltpu.VMEM((2, tq, w), F32)],
    )
    return pl.pallas_call(
        functools.partial(_diff_attn_kernel, tq=tq, tk=tk, scale=DF_HEAD_DIM ** -0.5,
                          out_scale=1.0 - lambda_init),
        grid_spec=grid_spec,
        out_shape=jax.ShapeDtypeStruct((tokens, DF_DIM), BF16),
        compiler_params=_cparams(("parallel", "parallel", "parallel", "arbitrary")),
        name="diff_attention",
    )(qfirst, klast, proj, proj, proj, qpos, kpos, subln_w.reshape(1, w),
      rel_bias.reshape(-1), lam.reshape(1))


def _rope_wide(x, pos, inv_freq):
    half = ML_ROPE // 2
    width = x.shape[1]
    ang = pos * inv_freq
    cos, sin = jnp.cos(ang), jnp.sin(ang)
    lane = lax.broadcasted_iota(jnp.int32, x.shape, 1)
    first = (lane % ML_ROPE) < half
    rot = jnp.where(first, -pltpu.roll(x, width - half, axis=1), pltpu.roll(x, half, axis=1))
    return x * cos + rot * sin


def _mla_prep_kernel(mq_ref, mkv_ref, kpe_ref, pos_ref, qn_w, kvn_w, wqn_ref, wqp_ref, wkv_ref, freq_ref,
                     qn_o, qp_o, kvb_o, kpe_o):
    pos = pos_ref[...].astype(F32)
    qc = _rms(mq_ref[...], qn_w[...], NORM_EPS).astype(BF16)
    qn_o[...] = _dot(qc, wqn_ref[...]).astype(qn_o.dtype)
    qp = _dot(qc, wqp_ref[...])
    qp_o[...] = _rope_wide(qp, pos, freq_ref[...]).astype(qp_o.dtype)
    kvc = _rms(mkv_ref[...], kvn_w[...], NORM_EPS).astype(BF16)
    kvb_o[...] = _dot(kvc, wkv_ref[...]).astype(kvb_o.dtype)
    kpe = kpe_ref[...]
    kpe2 = kpe + pltpu.roll(kpe, ML_ROPE, axis=1)
    kpe4 = jnp.concatenate([kpe2] * (ML_HEADS * ML_ROPE // LANES), axis=1)
    kpe_o[...] = _rope_wide(kpe4, pos, freq_ref[...]).astype(kpe_o.dtype)


def mla_prep(proj, positions, q_norm, kv_norm, wq_nope, wq_pe, wkv, *, tm=512):
    tokens = proj.shape[0]
    tm = min(tm, tokens)
    half = ML_ROPE // 2
    inv_freq = ROPE_THETA ** (-jnp.arange(half, dtype=F32) / half)
    pe_w = ML_HEADS * ML_ROPE
    freq = jnp.tile(inv_freq, pe_w // half).reshape(1, pe_w)

    def full(a):
        return pl.BlockSpec(a.shape, lambda i: (0, 0))

    qn_w = q_norm.reshape(1, -1)
    kvn_w = kv_norm.reshape(1, -1)
    outs = pl.pallas_call(
        _mla_prep_kernel,
        grid=(tokens // tm,),
        in_specs=[pl.BlockSpec((tm, ML_Q_RANK), lambda i: (i, OFF_MQ // ML_Q_RANK)),
                  pl.BlockSpec((tm, ML_KV_RANK), lambda i: (i, OFF_MKV // ML_KV_RANK)),
                  pl.BlockSpec((tm, LANES), lambda i: (i, OFF_KPE // LANES)),
                  pl.BlockSpec((tm, 1), lambda i: (i, 0)),
                  full(qn_w), full(kvn_w), full(wq_nope), full(wq_pe), full(wkv), full(freq)],
        out_specs=[pl.BlockSpec((tm, ML_HEADS * ML_NOPE), lambda i: (i, 0)),
                   pl.BlockSpec((tm, pe_w), lambda i: (i, 0)),
                   pl.BlockSpec((tm, 2 * ML_HEADS * ML_NOPE), lambda i: (i, 0)),
                   pl.BlockSpec((tm, pe_w), lambda i: (i, 0))],
        out_shape=[jax.ShapeDtypeStruct((tokens, ML_HEADS * ML_NOPE), BF16),
                   jax.ShapeDtypeStruct((tokens, pe_w), BF16),
                   jax.ShapeDtypeStruct((tokens, 2 * ML_HEADS * ML_NOPE), BF16),
                   jax.ShapeDtypeStruct((tokens, pe_w), BF16)],
        compiler_params=_cparams(("parallel",)),
        name="mla_prep",
    )(proj, proj, proj, positions.reshape(tokens, 1), qn_w, kvn_w, wq_nope, wq_pe, wkv, freq)
    return outs


def _mla_attn_kernel(qn_ref, qp_ref, kn_ref, kpe_ref, v_ref, o_ref, m_ref, l_ref, acc_ref, *, tq, tk, scale):
    h, i, j = pl.program_id(1), pl.program_id(2), pl.program_id(3)
    nk = pl.num_programs(3)

    @pl.when(j == 0)
    def _():
        m_ref[...] = jnp.full_like(m_ref, NEG_BIG)
        l_ref[...] = jnp.zeros_like(l_ref)
        acc_ref[...] = jnp.zeros_like(acc_ref)

    def update(masked):
        qp = qp_ref[...]
        lane = lax.broadcasted_iota(jnp.int32, qp.shape, 1)
        qp_h = jnp.where(lane // ML_ROPE == h, qp, jnp.zeros_like(qp))
        s = (_dot_t(qn_ref[...], kn_ref[...]) + _dot_t(qp_h, kpe_ref[...])) * scale
        if masked:
            qi = i * tq + lax.broadcasted_iota(jnp.int32, (tq, tk), 0)
            kj = j * tk + lax.broadcasted_iota(jnp.int32, (tq, tk), 1)
            s = jnp.where(qi >= kj, s, NEG_BIG)
        m_old = m_ref[...]
        m_new = jnp.maximum(m_old, jnp.max(s, axis=-1, keepdims=True))
        alpha = jnp.exp(m_old - m_new)
        p = jnp.exp(s - m_new)
        l_ref[...] = alpha * l_ref[...] + jnp.sum(p, axis=-1, keepdims=True)
        acc_ref[...] = alpha * acc_ref[...] + _dot(p.astype(BF16), v_ref[...])
        m_ref[...] = m_new

    active = j * tk <= i * tq + (tq - 1)
    diagonal = j * tk + (tk - 1) > i * tq

    @pl.when(active & diagonal)
    def _():
        update(True)

    @pl.when(active & jnp.logical_not(diagonal))
    def _():
        update(False)

    @pl.when(j == nk - 1)
    def _():
        o_ref[...] = (acc_ref[...] / l_ref[...]).astype(o_ref.dtype)


def mla_attention(qn, qp, kvb, kpe, batch, *, tq=256):
    tokens = qn.shape[0]
    seq = tokens // batch
    tq = min(tq, seq)
    tk = tq
    nq, nk = seq // tq, seq // tk
    pe_w = ML_HEADS * ML_ROPE

    def kv_row(b, i, j):
        return b * nk + jnp.minimum(j, (i * tq + tq - 1) // tk)

    return pl.pallas_call(
        functools.partial(_mla_attn_kernel, tq=tq, tk=tk, scale=(ML_NOPE + ML_ROPE) ** -0.5),
        grid=(batch, ML_HEADS, nq, nk),
        in_specs=[pl.BlockSpec((tq, ML_NOPE), lambda b, h, i, j: (b * nq + i, h)),
                  pl.BlockSpec((tq, pe_w), lambda b, h, i, j: (b * nq + i, 0)),
                  pl.BlockSpec((tk, ML_NOPE), lambda b, h, i, j: (kv_row(b, i, j), h)),
                  pl.BlockSpec((tk, pe_w), lambda b, h, i, j: (kv_row(b, i, j), 0)),
                  pl.BlockSpec((tk, ML_V), lambda b, h, i, j: (kv_row(b, i, j), ML_HEADS + h))],
        out_specs=pl.BlockSpec((tq, ML_V), lambda b, h, i, j: (b * nq + i, h)),
        out_shape=jax.ShapeDtypeStruct((tokens, ML_DIM), BF16),
        scratch_shapes=[pltpu.VMEM((tq, 1), F32), pltpu.VMEM((tq, 1), F32), pltpu.VMEM((tq, ML_V), F32)],
        compiler_params=_cparams(("parallel", "parallel", "parallel", "arbitrary")),
        name="mla_attention",
    )(qn, qp, kvb, kpe, kvb)


def _cross_kernel(x_ref, nw_ref, wq_ref, kv_ref, wo_ref, o_ref):
    x = x_ref[...]
    q = _dot(_rms(x, nw_ref[...], NORM_EPS).astype(BF16), wq_ref[...])
    kv = kv_ref[...]
    scale = CA_HEAD_DIM ** -0.5
    outs = []
    for hh in range(CA_HEADS):
        sl = slice(hh * CA_HEAD_DIM, (hh + 1) * CA_HEAD_DIM)
        s = _dot_t(q[:, sl].astype(BF16), kv[:, sl]) * scale
        p = jnp.exp(s - jnp.max(s, axis=-1, keepdims=True))
        p = p / jnp.sum(p, axis=-1, keepdims=True)
        outs.append(_dot(p.astype(BF16), kv[:, CA_DIM + hh * CA_HEAD_DIM:CA_DIM + (hh + 1) * CA_HEAD_DIM]))
    o = jnp.concatenate(outs, axis=1).astype(BF16)
    o_ref[...] = x + _dot(o, wo_ref[...])


def cross_block(x, batch, norm_w, wq, kv, wo, *, tq=512):
    tokens, d = x.shape
    seq = tokens // batch
    tq = min(tq, seq)
    nq = seq // tq
    mem_len = kv.shape[0] // batch
    return pl.pallas_call(
        _cross_kernel,
        grid=(batch, nq),
        in_specs=[pl.BlockSpec((tq, d), lambda b, i: (b * nq + i, 0)),
                  pl.BlockSpec((1, d), lambda b, i: (0, 0)),
                  pl.BlockSpec(wq.shape, lambda b, i: (0, 0)),
                  pl.BlockSpec((mem_len, 2 * CA_DIM), lambda b, i: (b, 0)),
                  pl.BlockSpec(wo.shape, lambda b, i: (0, 0))],
        out_specs=pl.BlockSpec((tq, d), lambda b, i: (b * nq + i, 0)),
        out_shape=jax.ShapeDtypeStruct((tokens, d), F32),
        compiler_params=_cparams(("parallel", "parallel")),
        name="cross_block",
    )(x, norm_w.reshape(1, d), wq, kv, wo)


def _router_kernel(x_ref, nw_ref, wr_ref, br_ref, h_ref, gates_ref):
    h = _rms(x_ref[...], nw_ref[...], NORM_EPS)
    h_ref[...] = h.astype(h_ref.dtype)
    logits = _dot_x3(h, wr_ref[...])
    biased = logits + br_ref[...]
    lane = lax.broadcasted_iota(jnp.int32, logits.shape, 1)
    big = jnp.int32(LANES)

    def first_argmax(vals):
        mx = jnp.max(vals, axis=-1, keepdims=True)
        return jnp.min(jnp.where(vals == mx, lane, big), axis=-1, keepdims=True)

    def pick(vals, idx):
        return jnp.sum(jnp.where(lane == idx, vals, 0.0), axis=-1, keepdims=True)

    is_group = (lane >= MOE_EXPERTS) & (lane < MOE_EXPERTS + MOE_GROUPS)
    gl = jnp.where(is_group, logits, NEG_BIG)
    ge = jnp.exp(gl - jnp.max(gl, axis=-1, keepdims=True))
    gp = ge / jnp.sum(ge, axis=-1, keepdims=True)
    g_lane = first_argmax(jnp.where(is_group, biased, NEG_BIG))
    p_group = pick(gp, g_lane)
    lo = (g_lane - MOE_EXPERTS) * MOE_PER_GROUP
    in_group = (lane >= lo) & (lane < lo + MOE_PER_GROUP)
    eb = jnp.where(in_group, biased, NEG_BIG)
    i1 = first_argmax(eb)
    i2 = first_argmax(jnp.where(lane == i1, NEG_BIG, eb))
    l1, l2 = pick(logits, i1), pick(logits, i2)
    mx = jnp.maximum(l1, l2)
    e1, e2 = jnp.exp(l1 - mx), jnp.exp(l2 - mx)
    w1, w2 = e1 / (e1 + e2), e2 / (e1 + e2)
    gates_ref[...] = jnp.where(lane == i1, w1, jnp.where(lane == i2, w2, 0.0)) * p_group


def moe_router(x, norm_w, w_router, b_router, *, tm=512):
    tokens, d = x.shape
    tm = min(tm, tokens)
    return pl.pallas_call(
        _router_kernel,
        grid=(tokens // tm,),
        in_specs=[pl.BlockSpec((tm, d), lambda i: (i, 0)),
                  pl.BlockSpec((1, d), lambda i: (0, 0)),
                  pl.BlockSpec((d, LANES), lambda i: (0, 0)),
                  pl.BlockSpec((1, LANES), lambda i: (0, 0))],
        out_specs=[pl.BlockSpec((tm, d), lambda i: (i, 0)),
                   pl.BlockSpec((tm, LANES), lambda i: (i, 0))],
        out_shape=[jax.ShapeDtypeStruct((tokens, d), BF16),
                   jax.ShapeDtypeStruct((tokens, LANES), F32)],
        compiler_params=_cparams(("parallel",)),
        name="moe_router",
    )(x, norm_w.reshape(1, d), w_router, b_router)


def _moe_dense_kernel(h_ref, gates_ref, x_ref, wg_ref, wu_ref, wd_ref, o_ref):
    e = pl.program_id(1)

    @pl.when(e == 0)
    def _():
        o_ref[...] = x_ref[...]

    h = h_ref[...]
    gate_pre = _dot(h, wg_ref[...])
    hid = (gate_pre * jax.nn.sigmoid(gate_pre)) * _dot(h, wu_ref[...])
    y = _dot(hid.astype(BF16), wd_ref[...])
    gates = gates_ref[...]
    lane = lax.broadcasted_iota(jnp.int32, gates.shape, 1)
    gate = jnp.sum(jnp.where(lane == e, gates, 0.0), axis=-1, keepdims=True)
    o_ref[...] += gate * y


def moe_dense(h, gates, x, wg, wu, wd, *, tm=512):
    tokens, d = x.shape
    tm = min(tm, tokens)
    n_exp, _, de = wg.shape
    return pl.pallas_call(
        _moe_dense_kernel,
        grid=(tokens // tm, n_exp),
        in_specs=[pl.BlockSpec((tm, d), lambda i, e: (i, 0)),
                  pl.BlockSpec((tm, LANES), lambda i, e: (i, 0)),
                  pl.BlockSpec((tm, d), lambda i, e: (i, 0)),
                  pl.BlockSpec((None, d, de), lambda i, e: (e, 0, 0)),
                  pl.BlockSpec((None, d, de), lambda i, e: (e, 0, 0)),
                  pl.BlockSpec((None, de, d), lambda i, e: (e, 0, 0))],
        out_specs=pl.BlockSpec((tm, d), lambda i, e: (i, 0)),
        out_shape=jax.ShapeDtypeStruct((tokens, d), F32),
        compiler_params=_cparams(("parallel", "arbitrary")),
        name="moe_dense",
    )(h, gates, x, wg, wu, wd)


def _proj_weight(w_in_l, w_vres_l):
    d = w_in_l.shape[0]
    mla0 = RW_COLS + DF_COLS
    vres = jnp.zeros((d, RW_V_RANK), F32) if w_vres_l is None else w_vres_l
    parts = [w_in_l[:, :RW_COLS],
             vres, jnp.zeros((d, LANES - RW_V_RANK), F32),
             w_in_l[:, mla0 + ML_Q_RANK + ML_KV_RANK:mla0 + ML_COLS], jnp.zeros((d, LANES - ML_ROPE), F32),
             w_in_l[:, RW_COLS:RW_COLS + DF_COLS],
             w_in_l[:, mla0 + ML_Q_RANK:mla0 + ML_Q_RANK + ML_KV_RANK],
             w_in_l[:, mla0:mla0 + ML_Q_RANK]]
    return jnp.concatenate(parts, axis=1).astype(BF16)


def _pad_rows(w, rows, at=0):
    out = jnp.zeros((rows, w.shape[1]), w.dtype)
    return lax.dynamic_update_slice(out, w, (at, 0))


def kernel(x, mem, positions, rel_bias, final_norm, norm_mix, w_in, w_in_vres, w_out, tm_mu, tm_mu_vres, tm_w0, tm_w2, tm_a0, tm_a2, tm_v0, tm_v2, tm_g2, tm_k_k, tm_k_a, tm_r_k, tm_ln_w, tm_ln_b, da_lq1, da_lk1, da_lq2, da_lk2, da_subln, mla_q_norm, mla_wq_b, mla_kv_norm, mla_wkv_b, norm_cross, norm_mem, ca_wq, ca_wkv, ca_wo, norm_ffn, moe_w_group, moe_b_group, moe_w_expert, moe_b_expert, moe_w_gate, moe_w_up, moe_w_down):
    batch, seq, d = x.shape
    tokens = batch * seq
    depth = norm_mix.shape[0]
    xf = x.reshape(tokens, d)
    memf = mem.reshape(-1, d)
    positions = positions.astype(jnp.int32)

    head_of_lane = jnp.arange(RW_DIM) // RW_HEAD_DIM
    seg = (head_of_lane[:, None] == jnp.arange(LANES)[None, :]).astype(BF16)
    seg_t = seg.T
    row = lambda v: v.reshape(1, -1)

    v_first = None
    for l in range(depth):
        w_cat = _proj_weight(w_in[l], None if l == 0 else w_in_vres[l - 1])
        proj = norm_matmul(xf, norm_mix[l], w_cat, tn=PROJ_COLS // 5)

        mu = tm_mu[l]
        prm = dict(mu_r=row(mu[:RW_DIM]), mu_k=row(mu[RW_DIM:2 * RW_DIM]), mu_v=row(mu[2 * RW_DIM:3 * RW_DIM]),
                   mu_l=row(mu[3 * RW_DIM:]), w0=row(tm_w0[l]), a0=row(tm_a0[l]),
                   w2=_pad_rows(tm_w2[l], LANES, 0), a2=_pad_rows(tm_a2[l], LANES, RW_W_RANK),
                   g2=tm_g2[l].astype(BF16), k_k=row(tm_k_k[l]), k_a=row(tm_k_a[l]), r_k=row(tm_r_k[l]),
                   seg=seg, seg_t=seg_t)
        if l > 0:
            prm.update(mu_vr=jnp.pad(row(tm_mu_vres[l - 1]), ((0, 0), (0, LANES - RW_V_RANK))),
                       v0=row(tm_v0[l - 1]), v2=_pad_rows(tm_v2[l - 1], LANES, 0))
        r, lw, k, v, kap, beta, gate, bonus = rwkv_prep(proj, batch, v_first, prm)
        if l == 0:
            v_first = v
        o = rwkv_scan(r, lw, k, v, kap, beta, batch)
        y_a = rwkv_post(o, bonus, gate, tm_ln_w[l], tm_ln_b[l], seg, seg_t)

        lambda_init = 0.8 - 0.6 * math.exp(-0.3 * l)
        lam = (jnp.exp(jnp.sum(da_lq1[l] * da_lk1[l])) - jnp.exp(jnp.sum(da_lq2[l] * da_lk2[l])) + lambda_init)
        y_b = diff_attention(proj, positions, rel_bias, lam, lambda_init, da_subln[l])

        wq = mla_wq_b[l].reshape(ML_Q_RANK, ML_HEADS, ML_NOPE + ML_ROPE)
        wq_nope = wq[:, :, :ML_NOPE].reshape(ML_Q_RANK, -1).astype(BF16)
        wq_pe = wq[:, :, ML_NOPE:].reshape(ML_Q_RANK, -1).astype(BF16)
        wkv = mla_wkv_b[l].reshape(ML_KV_RANK, ML_HEADS, ML_NOPE + ML_V)
        wkv = jnp.concatenate([wkv[:, :, :ML_NOPE].reshape(ML_KV_RANK, -1),
                               wkv[:, :, ML_NOPE:].reshape(ML_KV_RANK, -1)], axis=1).astype(BF16)
        qn, qp, kvb, kpe = mla_prep(proj, positions, mla_q_norm[l], mla_kv_norm[l], wq_nope, wq_pe, wkv)
        y_c = mla_attention(qn, qp, kvb, kpe, batch)

        wo = w_out[l].astype(BF16)
        xf = matmul_res([y_a, y_b, y_c],
                        [wo[:RW_DIM], wo[RW_DIM:RW_DIM + DF_DIM], wo[RW_DIM + DF_DIM:]], xf)

        kv_mem = norm_matmul(memf, norm_mem[l], ca_wkv[l].astype(BF16), out_dtype=BF16)
        xf = cross_block(xf, batch, norm_cross[l], ca_wq[l].astype(BF16), kv_mem, ca_wo[l].astype(BF16))

        w_router = jnp.concatenate(
            [moe_w_expert[l], moe_w_group[l], jnp.zeros((d, LANES - MOE_EXPERTS - MOE_GROUPS), F32)], axis=1)
        b_router = jnp.concatenate(
            [moe_b_expert[l], moe_b_group[l], jnp.zeros((LANES - MOE_EXPERTS - MOE_GROUPS,), F32)]).reshape(1, LANES)
        h, gates = moe_router(xf, norm_ffn[l], w_router, b_router)
        xf = moe_dense(h, gates, xf, moe_w_gate[l].astype(BF16), moe_w_up[l].astype(BF16),
                       moe_w_down[l].astype(BF16))

    out = rmsnorm(xf, final_norm)
    return out.reshape(batch, seq, d)
```

```python
import functools
import math

import jax
import jax.numpy as jnp
from jax import lax
from jax.experimental import pallas as pl
from jax.experimental.pallas import tpu as pltpu

F32 = jnp.float32
BF16 = jnp.bfloat16

NORM_EPS = 1e-6
ROPE_THETA = 10000.0

RW_HEADS = 16
RW_HEAD_DIM = 64
RW_DIM = RW_HEADS * RW_HEAD_DIM
RW_W_RANK = 64
RW_A_RANK = 64
RW_G_RANK = 128
RW_V_RANK = 32
RW_LORA = RW_W_RANK + RW_A_RANK + RW_G_RANK
RW_LN_EPS = 64e-5
RW_COLS = 3 * RW_DIM + RW_LORA

DF_HEADS = 4
DF_HEAD_DIM = 64
DF_V_DIM = 2 * DF_HEAD_DIM
DF_QK = DF_HEADS * 2 * DF_HEAD_DIM
DF_DIM = DF_HEADS * DF_V_DIM
DF_COLS = 2 * DF_QK + DF_DIM
DF_SUBLN_EPS = 1e-5

ML_HEADS = 4
ML_Q_RANK = 384
ML_KV_RANK = 256
ML_NOPE = 128
ML_ROPE = 64
ML_V = 128
ML_DIM = ML_HEADS * ML_V
ML_COLS = ML_Q_RANK + ML_KV_RANK + ML_ROPE

REL_BUCKETS = 32
REL_MAX_DIST = 128

CA_HEADS = 4
CA_HEAD_DIM = 128
CA_DIM = CA_HEADS * CA_HEAD_DIM

MOE_GROUPS = 4
MOE_PER_GROUP = 8
MOE_EXPERTS = MOE_GROUPS * MOE_PER_GROUP

LANES = 128
SCAN_CHUNK = 64
SCAN_GROUP = 4
ATTN_TILE = 512
VMEM_LIMIT = 56 * 1024 * 1024
NEG_BIG = -1e30

LOG2E = 1.4426950408889634

OFF_R = 0
OFF_K = RW_DIM
OFF_V = 2 * RW_DIM
OFF_LORA = 3 * RW_DIM
OFF_VRES = OFF_LORA + RW_LORA
OFF_KPE = OFF_VRES + LANES
PROJ_F_COLS = OFF_KPE + LANES
OFF_DQ = 0
OFF_DK = OFF_DQ + DF_QK
OFF_DV = OFF_DK + DF_QK
OFF_MKV = OFF_DV + DF_DIM
OFF_MQ = 5 * ML_Q_RANK
PROJ_A_COLS = OFF_MQ + ML_Q_RANK


def _cparams(sem, vmem=VMEM_LIMIT):
    return pltpu.CompilerParams(dimension_semantics=sem, vmem_limit_bytes=vmem)


def _dot(a, b):
    return jnp.dot(a, b, preferred_element_type=F32)


def _dot_t(a, b):
    return lax.dot_general(a, b, (((1,), (1,)), ((), ())), preferred_element_type=F32)


def _split3(x):
    hi = x.astype(BF16)
    r1 = x - hi.astype(F32)
    mid = r1.astype(BF16)
    lo = (r1 - mid.astype(F32)).astype(BF16)
    return hi, mid, lo


def _dot_exact_rhs01(x, ones_bf16):
    h, m, l = _split3(x)
    return _dot(h, ones_bf16) + _dot(m, ones_bf16) + _dot(l, ones_bf16)


def _dot_x3(a, b):
    ah = a.astype(BF16)
    al = (a - ah.astype(F32)).astype(BF16)
    bh = b.astype(BF16)
    bl = (b - bh.astype(F32)).astype(BF16)
    return _dot(ah, bh) + _dot(ah, bl) + _dot(al, bh)


def _rms(x, w, eps):
    ms = jnp.mean(x * x, axis=-1, keepdims=True)
    return x * lax.rsqrt(ms + eps) * w


def _rmsnorm_kernel(x_ref, w_ref, o_ref, *, eps):
    o_ref[...] = _rms(x_ref[...], w_ref[...], eps).astype(o_ref.dtype)


def rmsnorm(x, w, *, eps=NORM_EPS, out_dtype=F32, tm=512):
    m, d = x.shape
    tm = min(tm, m)
    return pl.pallas_call(
        functools.partial(_rmsnorm_kernel, eps=eps),
        grid=(m // tm,),
        in_specs=[pl.BlockSpec((tm, d), lambda i: (i, 0)),
                  pl.BlockSpec((1, d), lambda i: (0, 0))],
        out_specs=pl.BlockSpec((tm, d), lambda i: (i, 0)),
        out_shape=jax.ShapeDtypeStruct((m, d), out_dtype),
        compiler_params=_cparams(("parallel",)),
        name="rmsnorm",
    )(x, w.reshape(1, d))


def _norm_matmul_kernel(x_ref, nw_ref, w_ref, o_ref, xn_ref, *, eps):
    @pl.when(pl.program_id(1) == 0)
    def _():
        xn_ref[...] = _rms(x_ref[...], nw_ref[...], eps).astype(BF16)

    o_ref[...] = _dot(xn_ref[...], w_ref[...]).astype(o_ref.dtype)


def norm_matmul(x, nw, w, *, out_dtype=F32, tm=512, tn=None, eps=NORM_EPS):
    m, d = x.shape
    n = w.shape[1]
    tm = min(tm, m)
    tn = n if tn is None else tn
    return pl.pallas_call(
        functools.partial(_norm_matmul_kernel, eps=eps),
        grid=(m // tm, n // tn),
        in_specs=[pl.BlockSpec((tm, d), lambda i, j: (i, 0)),
                  pl.BlockSpec((1, d), lambda i, j: (0, 0)),
                  pl.BlockSpec((d, tn), lambda i, j: (0, j))],
        out_specs=pl.BlockSpec((tm, tn), lambda i, j: (i, j)),
        out_shape=jax.ShapeDtypeStruct((m, n), out_dtype),
        scratch_shapes=[pltpu.VMEM((tm, d), BF16)],
        compiler_params=_cparams(("parallel", "arbitrary")),
        name="norm_matmul",
    )(x, nw.reshape(1, d), w)


def _matmul_res_kernel(*refs, n_a):
    a_refs, w_refs = refs[:n_a], refs[n_a:2 * n_a]
    res_ref, o_ref = refs[2 * n_a], refs[2 * n_a + 1]
    acc = res_ref[...]
    for a_ref, w_ref in zip(a_refs, w_refs):
        acc = acc + _dot(a_ref[...].astype(BF16), w_ref[...])
    o_ref[...] = acc


def matmul_res(a_list, w_list, res, *, tm=512, tn=1024):
    m, n = res.shape
    tm = min(tm, m)
    tn = min(tn, n)
    n_a = len(a_list)
    in_specs = ([pl.BlockSpec((tm, a.shape[1]), lambda i, j: (i, 0)) for a in a_list]
                + [pl.BlockSpec((w.shape[0], tn), lambda i, j: (0, j)) for w in w_list]
                + [pl.BlockSpec((tm, tn), lambda i, j: (i, j))])
    return pl.pallas_call(
        functools.partial(_matmul_res_kernel, n_a=n_a),
        grid=(m // tm, n // tn),
        in_specs=in_specs,
        out_specs=pl.BlockSpec((tm, tn), lambda i, j: (i, j)),
        out_shape=jax.ShapeDtypeStruct((m, n), F32),
        compiler_params=_cparams(("parallel", "arbitrary")),
        name="matmul_res",
    )(*a_list, *w_list, res)


def _softplus(z):
    return jnp.maximum(z, 0.0) + jnp.log(1.0 + jnp.exp(-jnp.abs(z)))


def _rwkv_prep_kernel(*refs, has_vres):
    if has_vres:
        (pr_ref, pk_ref, pv_ref, pl_ref, pvr_ref, vfirst_ref,
         mu_r, mu_k, mu_v, mu_l, mu_vr, w0, w2, a0, a2, g2, v0, v2,
         k_k, k_a, r_k, seg, seg_t,
         r_o, lw_o, k_o, v_o, kap_o, beta_o, g_o, bonus_o,
         last_r, last_k, last_v, last_l, last_vr) = refs
    else:
        (pr_ref, pk_ref, pv_ref, pl_ref,
         mu_r, mu_k, mu_v, mu_l, w0, w2, a0, a2, g2,
         k_k, k_a, r_k, seg, seg_t,
         r_o, lw_o, k_o, v_o, kap_o, beta_o, g_o, bonus_o,
         last_r, last_k, last_v, last_l) = refs
    t = pl.program_id(1)

    def shifted(p_ref, last_ref, mu_ref):
        p = p_ref[...]
        n = p.shape[0]
        carried = jnp.where(t == 0, 0.0, last_ref[0:1, :])
        row = lax.broadcasted_iota(jnp.int32, p.shape, 0)
        prev = jnp.where(row == 0, carried, pltpu.roll(p, 1, axis=0))
        last_ref[0:1, :] = p[n - 1:n, :]
        return p + mu_ref[...] * (prev - p)

    r = shifted(pr_ref, last_r, mu_r)
    k = shifted(pk_ref, last_k, mu_k)
    v = shifted(pv_ref, last_v, mu_v)
    lora = shifted(pl_ref, last_l, mu_l)
    wl = lora[:, :LANES]
    gl = lora[:, LANES:]

    lane = lax.broadcasted_iota(jnp.int32, wl.shape, 1)
    wl_t = jnp.where(lane < RW_W_RANK, jnp.tanh(wl), 0.0)
    al = jnp.where(lane >= RW_W_RANK, wl, 0.0)
    w_log = -_softplus(-(w0[...] + _dot_x3(wl_t, w2[...]))) - 0.5
    lw_o[...] = -jnp.exp(w_log)
    a = jax.nn.sigmoid(a0[...] + _dot_x3(al, a2[...]))
    g_o[...] = _dot(jax.nn.sigmoid(gl).astype(BF16), g2[...])

    segm, segm_t = seg[...], seg_t[...]

    def head_sum(x):
        return _dot_exact_rhs01(_dot_exact_rhs01(x, segm), segm_t)

    kk = k * k_k[...]
    kk = kk / jnp.maximum(jnp.sqrt(head_sum(kk * kk)), 1e-12)
    k = k * (1.0 + (a - 1.0) * k_a[...])
    if has_vres:
        vr = shifted(pvr_ref, last_vr, mu_vr)
        mix = jax.nn.sigmoid(v0[...] + _dot_x3(vr, v2[...]))
        v = v + (vfirst_ref[...] - v) * mix
    r_o[...] = r
    k_o[...] = k
    v_o[...] = v
    kap_o[...] = kk
    beta_o[...] = kk * a
    bonus_o[...] = head_sum(r * k * r_k[...]) * v


def rwkv_prep(proj, batch, vfirst, prm, *, tt=256):
    tokens = proj.shape[0]
    seq = tokens // batch
    tt = min(tt, seq)
    nt = seq // tt
    has_vres = vfirst is not None
    d = RW_DIM

    def rows(width, col):
        return pl.BlockSpec((tt, width), lambda b, t, col=col: (b * nt + t, col))

    def full(shape):
        return pl.BlockSpec(shape, lambda b, t: (0, 0))

    in_specs = [rows(d, OFF_R // d), rows(d, OFF_K // d), rows(d, OFF_V // d),
                rows(RW_LORA, OFF_LORA // RW_LORA)]
    args = [proj, proj, proj, proj]
    if has_vres:
        in_specs += [rows(LANES, OFF_VRES // LANES), rows(d, 0)]
        args += [proj, vfirst]
    names = ["mu_r", "mu_k", "mu_v", "mu_l"] + (["mu_vr"] if has_vres else []) + ["w0", "w2", "a0", "a2", "g2"]
    names += (["v0", "v2"] if has_vres else []) + ["k_k", "k_a", "r_k", "seg", "seg_t"]
    for nm in names:
        in_specs.append(full(prm[nm].shape))
        args.append(prm[nm])
    out_spec = pl.BlockSpec((tt, d), lambda b, t: (b * nt + t, 0))
    scratch = [pltpu.VMEM((8, d), F32)] * 3 + [pltpu.VMEM((8, RW_LORA), F32)]
    if has_vres:
        scratch.append(pltpu.VMEM((8, LANES), F32))
    return pl.pallas_call(
        functools.partial(_rwkv_prep_kernel, has_vres=has_vres),
        grid=(batch, nt),
        in_specs=in_specs,
        out_specs=[out_spec] * 8,
        out_shape=[jax.ShapeDtypeStruct((tokens, d), F32)] * 8,
        scratch_shapes=scratch,
        compiler_params=_cparams(("arbitrary", "arbitrary")),
        name="rwkv_prep",
    )(*args)


def _rwkv_scan_kernel(r_ref, lw_ref, k_ref, v_ref, kap_ref, beta_ref, tril_ref, bmask_ref,
                      o_ref, h_ref):
    @pl.when(pl.program_id(2) == 0)
    def _():
        h_ref[...] = jnp.zeros_like(h_ref)

    c, w = lw_ref.shape
    g = w // RW_HEAD_DIM
    rr = g * c
    lw = lw_ref[...]
    cum = _dot_exact_lhs01(tril_ref[...], lw)
    total = cum[c - 1:c, :]
    p_in = jnp.exp(cum)
    p_ex = jnp.exp(cum - lw)
    p_inv = jnp.exp(-cum)
    p_rem = jnp.exp(total - cum)
    bmask = bmask_ref[...]

    def stack(x):
        return (jnp.concatenate([x] * g, axis=0) * bmask).astype(BF16)

    kap, beta, k = kap_ref[...], beta_ref[...], k_ref[...]
    a_s = stack(-kap * p_ex)
    r_s = stack(r_ref[...] * p_in)
    b_s = stack(beta * p_inv)
    k_s = stack(k * p_inv)
    v_s = stack(v_ref[...])

    row = lax.broadcasted_iota(jnp.int32, (rr, rr), 0)
    col = lax.broadcasted_iota(jnp.int32, (rr, rr), 1)
    strict = row > col
    incl = row >= col
    ab = jnp.where(strict, _dot_t(a_s, b_s), 0.0)
    ak = jnp.where(strict, _dot_t(a_s, k_s), 0.0)
    rb = jnp.where(incl, _dot_t(r_s, b_s), 0.0)
    rk = jnp.where(incl, _dot_t(r_s, k_s), 0.0)

    h = h_ref[...]
    hb = h.astype(BF16)
    x = _dot(a_s, hb) + _dot(ak.astype(BF16), v_s)
    lp = ab
    n_sq = int(math.log2(c))
    for i in range(n_sq):
        lpb = lp.astype(BF16)
        x = x + _dot(lpb, x.astype(BF16))
        if i < n_sq - 1:
            lp = _dot(lpb, lpb)
    u = x.astype(BF16)
    ow = _dot(r_s, hb) + _dot(rb.astype(BF16), u) + _dot(rk.astype(BF16), v_s)
    o = ow[0:c]
    for i in range(1, g):
        o = o + ow[i * c:(i + 1) * c]
    o_ref[...] = o

    eye = (lax.broadcasted_iota(jnp.int32, (w, w), 0) == lax.broadcasted_iota(jnp.int32, (w, w), 1))
    diag = jnp.where(eye, jnp.exp(total), 0.0)
    p_col = _dot_exact_rhs01(diag, jnp.ones((w, w), BF16))
    z = jnp.concatenate([jnp.concatenate([beta * p_rem] * g, axis=0) * bmask,
                         jnp.concatenate([k * p_rem] * g, axis=0) * bmask], axis=0)
    uv = jnp.concatenate([u, v_s], axis=0)
    h_ref[...] = h * p_col + _dot(z.T.astype(BF16), uv)


def _dot_exact_lhs01(ones_bf16, x):
    h, m, l = _split3(x)
    return _dot(ones_bf16, h) + _dot(ones_bf16, m) + _dot(ones_bf16, l)


def rwkv_scan(r, lw, k, v, kap, beta, batch):
    tokens, d = r.shape
    seq = tokens // batch
    c = min(SCAN_CHUNK, seq)
    nc = seq // c
    gw = SCAN_GROUP * RW_HEAD_DIM
    rr = SCAN_GROUP * c
    tril = (jnp.arange(c)[:, None] >= jnp.arange(c)[None, :]).astype(BF16)
    bmask = (jnp.arange(rr)[:, None] // c == jnp.arange(gw)[None, :] // RW_HEAD_DIM).astype(F32)
    blk = pl.BlockSpec((c, gw), lambda b, g, i: (b * nc + i, g))
    return pl.pallas_call(
        _rwkv_scan_kernel,
        grid=(batch, d // gw, nc),
        in_specs=[blk] * 6 + [pl.BlockSpec((c, c), lambda b, g, i: (0, 0)),
                              pl.BlockSpec((rr, gw), lambda b, g, i: (0, 0))],
        out_specs=blk,
        out_shape=jax.ShapeDtypeStruct((tokens, d), F32),
        scratch_shapes=[pltpu.VMEM((gw, gw), F32)],
        compiler_params=_cparams(("arbitrary", "arbitrary", "arbitrary")),
        name="rwkv_scan",
    )(r, lw, k, v, kap, beta, tril, bmask)


def _rwkv_post_kernel(o_ref, bonus_ref, g_ref, lnw_ref, lnb_ref, seg, seg_t, y_ref):
    segm, segm_t = seg[...], seg_t[...]

    def head_mean(x):
        return _dot_exact_rhs01(_dot_exact_rhs01(x, segm), segm_t) * (1.0 / RW_HEAD_DIM)

    o = o_ref[...]
    dlt = o - head_mean(o)
    var = head_mean(dlt * dlt)
    y = dlt * lax.rsqrt(var + RW_LN_EPS) * lnw_ref[...] + lnb_ref[...]
    y_ref[...] = ((y + bonus_ref[...]) * g_ref[...]).astype(y_ref.dtype)


def rwkv_post(o, bonus, g, ln_w, ln_b, seg, seg_t, *, tm=512):
    tokens, d = o.shape
    tm = min(tm, tokens)
    blk = pl.BlockSpec((tm, d), lambda i: (i, 0))
    vec = pl.BlockSpec((1, d), lambda i: (0, 0))
    return pl.pallas_call(
        _rwkv_post_kernel,
        grid=(tokens // tm,),
        in_specs=[blk, blk, blk, vec, vec,
                  pl.BlockSpec(seg.shape, lambda i: (0, 0)), pl.BlockSpec(seg_t.shape, lambda i: (0, 0))],
        out_specs=blk,
        out_shape=jax.ShapeDtypeStruct((tokens, d), BF16),
        compiler_params=_cparams(("parallel",)),
        name="rwkv_post",
    )(o, bonus, g, ln_w.reshape(1, d), ln_b.reshape(1, d), seg, seg_t)


def _t5_thresholds():
    max_exact = REL_BUCKETS // 2
    thr = list(range(1, max_exact))
    n = max_exact
    for bucket in range(max_exact, REL_BUCKETS):
        while True:
            large = max_exact + int(math.log(max(n, max_exact) / max_exact)
                                    / math.log(REL_MAX_DIST / max_exact) * (REL_BUCKETS - max_exact))
            if min(large, REL_BUCKETS - 1) >= bucket:
                break
            n += 1
        thr.append(n)
    return thr


T5_THRESHOLDS = _t5_thresholds()
T5_FAR = T5_THRESHOLDS[-1]


def _softmax_tile(s_t, c, state, vt_tile):
    m_old, l_old, acc = state
    m_new = jnp.maximum(m_old, jnp.max(s_t, axis=0, keepdims=True) + c)
    alpha = jnp.exp2(m_old - m_new)
    p_t = jnp.exp2(s_t - (m_new - c))
    l_new = alpha * l_old + jnp.sum(p_t, axis=0, keepdims=True)
    acc = alpha * acc + _dot(vt_tile, p_t.astype(BF16))
    return m_new, l_new, acc


def _transpose_into(vt_ref, v_ref, chunk):
    seq = v_ref.shape[0]
    for c in range(seq // chunk):
        vt_ref[:, c * chunk:(c + 1) * chunk] = v_ref[c * chunk:(c + 1) * chunk, :].astype(F32).T.astype(BF16)


def _diff_attn_kernel(qfirst_ref, klast_ref, q_ref, k_ref, v_ref, qpos_ref, kpos_ref, subln_ref, table_ref, lam_ref,
                      o_ref, vt_ref, *, tq, tk, scale2, out_scale):
    b, i = pl.program_id(0), pl.program_id(1)
    nq = pl.num_programs(1)
    seq = k_ref.shape[0]
    nk = seq // tk
    w = DF_V_DIM

    @pl.when(i == 0)
    def _():
        _transpose_into(vt_ref, v_ref, tk)

    n_tiles = (i * tq + tq - 1) // tk + 1
    qf = qfirst_ref[b * nq + i]
    n_far = lax.while_loop(
        lambda j: (j * tk + tk - 1 <= i * tq) & (qf - klast_ref[b * nk + jnp.minimum(j, nk - 1)] >= T5_FAR),
        lambda j: j + 1, jnp.int32(0))

    dist = lax.broadcasted_iota(jnp.int32, (1, LANES), 1)
    qpos = qpos_ref[...]
    q_idx = i * tq + lax.broadcasted_iota(jnp.int32, (tk, tq), 1)
    k_off = lax.broadcasted_iota(jnp.int32, (tk, tq), 0)
    lane = lax.broadcasted_iota(jnp.int32, (tq, w), 1)

    bias_rows, c_far, qm = [], [], []
    for h in range(DF_HEADS):
        bias_vec = jnp.full((1, LANES), table_ref[h], F32)
        for bucket, thr in enumerate(T5_THRESHOLDS, start=1):
            bias_vec = jnp.where(dist >= thr, table_ref[bucket * DF_HEADS + h], bias_vec)
        bias_rows.append(jnp.broadcast_to(bias_vec * LOG2E, (tk, LANES)))
        c_far.append(table_ref[(REL_BUCKETS - 1) * DF_HEADS + h] * LOG2E)
        qh = q_ref[:, h * w:(h + 1) * w].astype(F32) * scale2
        qm.append([jnp.where((lane >= mi * DF_HEAD_DIM) & (lane < (mi + 1) * DF_HEAD_DIM), qh, 0.0).astype(BF16)
                   for mi in range(2)])

    def tiles(j, h):
        off = pl.multiple_of(j * tk, tk)
        return k_ref[pl.ds(off, tk), h * w:(h + 1) * w], vt_ref[h * w:(h + 1) * w, pl.ds(off, tk)], off

    def far_body(j, st):
        out = []
        for h in range(DF_HEADS):
            k_t, vt_t, _ = tiles(j, h)
            out += [_softmax_tile(_dot_t(k_t, qm[h][mi]), c_far[h], st[2 * h + mi], vt_t) for mi in range(2)]
        return tuple(out)

    def near_body(j, st):
        off = pl.multiple_of(j * tk, tk)
        n = jnp.clip(qpos - kpos_ref[pl.ds(off, tk), :], 0, LANES - 1)
        keep = q_idx >= off + k_off
        out = []
        for h in range(DF_HEADS):
            k_t, vt_t, _ = tiles(j, h)
            bias = jnp.concatenate(
                [jnp.take_along_axis(bias_rows[h], n[:, cb * LANES:(cb + 1) * LANES], axis=1)
                 for cb in range(tq // LANES)], axis=1)
            out += [_softmax_tile(jnp.where(keep, _dot_t(k_t, qm[h][mi]) + bias, NEG_BIG), 0.0,
                                  st[2 * h + mi], vt_t) for mi in range(2)]
        return tuple(out)

    init = tuple((jnp.full((1, tq), NEG_BIG, F32), jnp.zeros((1, tq), F32), jnp.zeros((w, tq), F32))
                 for _ in range(2 * DF_HEADS))
    st = lax.fori_loop(0, n_far, far_body, init)
    st = lax.fori_loop(n_far, n_tiles, near_body, st)
    for h in range(DF_HEADS):
        s0, s1 = st[2 * h], st[2 * h + 1]
        d_t = s0[2] / s0[1] - lam_ref[0] * (s1[2] / s1[1])
        ms = jnp.mean(d_t * d_t, axis=0, keepdims=True)
        y_t = d_t * lax.rsqrt(ms + DF_SUBLN_EPS) * (subln_ref[...] * out_scale)
        o_ref[:, h * w:(h + 1) * w] = y_t.T.astype(o_ref.dtype)


def diff_attention(pa, positions, rel_bias, lam, lambda_init, subln_w, *, tq=ATTN_TILE):
    batch, seq = positions.shape
    tokens = batch * seq
    tq = min(tq, seq)
    tk = tq
    nq, nk = seq // tq, seq // tk
    qfirst = positions[:, ::tq].reshape(-1)
    klast = positions[:, tk - 1::tk].reshape(-1)
    qpos = positions.reshape(batch, 1, seq)
    kpos = positions.reshape(batch, seq, 1)
    wd = DF_DIM
    grid_spec = pltpu.PrefetchScalarGridSpec(
        num_scalar_prefetch=2,
        grid=(batch, nq),
        in_specs=[pl.BlockSpec((tq, wd), lambda b, i, *_: (b * nq + i, OFF_DQ // wd)),
                  pl.BlockSpec((seq, wd), lambda b, i, *_: (b, OFF_DK // wd)),
                  pl.BlockSpec((seq, wd), lambda b, i, *_: (b, OFF_DV // wd)),
                  pl.BlockSpec((None, 1, tq), lambda b, i, *_: (b, 0, i)),
                  pl.BlockSpec((None, seq, 1), lambda b, i, *_: (b, 0, 0)),
                  pl.BlockSpec((DF_V_DIM, 1), lambda b, i, *_: (0, 0)),
                  pl.BlockSpec(memory_space=pltpu.SMEM),
                  pl.BlockSpec(memory_space=pltpu.SMEM)],
        out_specs=pl.BlockSpec((tq, wd), lambda b, i, *_: (b * nq + i, 0)),
        scratch_shapes=[pltpu.VMEM((wd, seq), BF16)],
    )
    return pl.pallas_call(
        functools.partial(_diff_attn_kernel, tq=tq, tk=tk, scale2=DF_HEAD_DIM ** -0.5 * LOG2E,
                          out_scale=1.0 - lambda_init),
        grid_spec=grid_spec,
        out_shape=jax.ShapeDtypeStruct((tokens, DF_DIM), BF16),
        compiler_params=_cparams(("arbitrary", "arbitrary")),
        name="diff_attention",
    )(qfirst, klast, pa, pa, pa, qpos, kpos, subln_w.reshape(DF_V_DIM, 1), rel_bias.reshape(-1), lam.reshape(1))


ML_QK_PAD = 2 * LANES


def _rope_block(x, cos, sin):
    half = ML_ROPE // 2
    lane = lax.broadcasted_iota(jnp.int32, x.shape, 1)
    rot = jnp.where(lane < half, -pltpu.roll(x, LANES - half, axis=1),
                    jnp.where(lane < ML_ROPE, pltpu.roll(x, half, axis=1), 0.0))
    return x * cos + rot * sin


def _mla_prep_kernel(mq_ref, mkv_ref, kpe_ref, pos_ref, qn_w, kvn_w, wq_ref, wkv_ref, freq_ref,
                     qf_o, kf_o, v_o, *, qscale):
    ang = pos_ref[...].astype(F32) * freq_ref[...]
    cos, sin = jnp.cos(ang), jnp.sin(ang)
    qc = _rms(mq_ref[...].astype(F32), qn_w[...], NORM_EPS).astype(BF16)
    q_all = _dot(qc, wq_ref[...]) * qscale
    kvc = _rms(mkv_ref[...].astype(F32), kvn_w[...], NORM_EPS).astype(BF16)
    kvb = _dot(kvc, wkv_ref[...])
    kpe = _rope_block(kpe_ref[...], cos, sin).astype(BF16)
    nope_w = ML_HEADS * ML_NOPE
    for h in range(ML_HEADS):
        lo = h * ML_QK_PAD
        qf_o[:, lo:lo + LANES] = q_all[:, h * LANES:(h + 1) * LANES].astype(BF16)
        qf_o[:, lo + LANES:lo + 2 * LANES] = _rope_block(
            q_all[:, nope_w + h * LANES:nope_w + (h + 1) * LANES], cos, sin).astype(BF16)
        kf_o[:, lo:lo + LANES] = kvb[:, h * LANES:(h + 1) * LANES].astype(BF16)
        kf_o[:, lo + LANES:lo + 2 * LANES] = kpe
    v_o[...] = kvb[:, nope_w:].astype(BF16)


def mla_prep(pa, pf, positions, q_norm, kv_norm, wq_all, wkv, *, tm=512):
    tokens = pa.shape[0]
    tm = min(tm, tokens)
    half = ML_ROPE // 2
    inv_freq = ROPE_THETA ** (-jnp.arange(half, dtype=F32) / half)
    freq = jnp.concatenate([inv_freq, inv_freq, jnp.zeros((LANES - ML_ROPE,), F32)]).reshape(1, LANES)

    def full(a):
        return pl.BlockSpec(a.shape, lambda i: (0, 0))

    qn_w = q_norm.reshape(1, -1)
    kvn_w = kv_norm.reshape(1, -1)
    wide = ML_HEADS * ML_QK_PAD
    return pl.pallas_call(
        functools.partial(_mla_prep_kernel, qscale=(ML_NOPE + ML_ROPE) ** -0.5 * LOG2E),
        grid=(tokens // tm,),
        in_specs=[pl.BlockSpec((tm, ML_Q_RANK), lambda i: (i, OFF_MQ // ML_Q_RANK)),
                  pl.BlockSpec((tm, ML_KV_RANK), lambda i: (i, OFF_MKV // ML_KV_RANK)),
                  pl.BlockSpec((tm, LANES), lambda i: (i, OFF_KPE // LANES)),
                  pl.BlockSpec((tm, 1), lambda i: (i, 0)),
                  full(qn_w), full(kvn_w), full(wq_all), full(wkv), full(freq)],
        out_specs=[pl.BlockSpec((tm, wide), lambda i: (i, 0)),
                   pl.BlockSpec((tm, wide), lambda i: (i, 0)),
                   pl.BlockSpec((tm, ML_DIM), lambda i: (i, 0))],
        out_shape=[jax.ShapeDtypeStruct((tokens, wide), BF16),
                   jax.ShapeDtypeStruct((tokens, wide), BF16),
                   jax.ShapeDtypeStruct((tokens, ML_DIM), BF16)],
        compiler_params=_cparams(("parallel",)),
        name="mla_prep",
    )(pa, pa, pf, positions.reshape(tokens, 1), qn_w, kvn_w, wq_all, wkv, freq)


def _mla_attn_kernel(q_ref, k_ref, v_ref, o_ref, vt_ref, *, tq, tk):
    i = pl.program_id(1)
    wq = ML_QK_PAD

    @pl.when(i == 0)
    def _():
        _transpose_into(vt_ref, v_ref, tk)

    n_tiles = (i * tq + tq - 1) // tk + 1
    n_full = (i * tq + 1) // tk
    q_idx = i * tq + lax.broadcasted_iota(jnp.int32, (tk, tq), 1)
    k_off = lax.broadcasted_iota(jnp.int32, (tk, tq), 0)
    qh = [q_ref[:, h * wq:(h + 1) * wq] for h in range(ML_HEADS)]

    def tiles(j, h):
        off = pl.multiple_of(j * tk, tk)
        return (k_ref[pl.ds(off, tk), h * wq:(h + 1) * wq],
                vt_ref[h * ML_V:(h + 1) * ML_V, pl.ds(off, tk)], off)

    def full_body(j, st):
        out = []
        for h in range(ML_HEADS):
            k_t, vt_t, _ = tiles(j, h)
            out.append(_softmax_tile(_dot_t(k_t, qh[h]), 0.0, st[h], vt_t))
        return tuple(out)

    def diag_body(j, st):
        out = []
        for h in range(ML_HEADS):
            k_t, vt_t, off = tiles(j, h)
            s_t = jnp.where(q_idx >= off + k_off, _dot_t(k_t, qh[h]), NEG_BIG)
            out.append(_softmax_tile(s_t, 0.0, st[h], vt_t))
        return tuple(out)

    st = tuple((jnp.full((1, tq), NEG_BIG, F32), jnp.zeros((1, tq), F32), jnp.zeros((ML_V, tq), F32))
               for _ in range(ML_HEADS))
    st = lax.fori_loop(0, n_full, full_body, st)
    st = lax.fori_loop(n_full, n_tiles, diag_body, st)
    for h in range(ML_HEADS):
        o_ref[:, h * ML_V:(h + 1) * ML_V] = (st[h][2] / st[h][1]).T.astype(o_ref.dtype)


def mla_attention(qf, kf, v, batch, *, tq=ATTN_TILE):
    tokens = qf.shape[0]
    seq = tokens // batch
    tq = min(tq, seq)
    tk = tq
    nq = seq // tq
    wide = qf.shape[1]
    return pl.pallas_call(
        functools.partial(_mla_attn_kernel, tq=tq, tk=tk),
        grid=(batch, nq),
        in_specs=[pl.BlockSpec((tq, wide), lambda b, i: (b * nq + i, 0)),
                  pl.BlockSpec((seq, wide), lambda b, i: (b, 0)),
                  pl.BlockSpec((seq, ML_DIM), lambda b, i: (b, 0))],
        out_specs=pl.BlockSpec((tq, ML_DIM), lambda b, i: (b * nq + i, 0)),
        out_shape=jax.ShapeDtypeStruct((tokens, ML_DIM), BF16),
        scratch_shapes=[pltpu.VMEM((ML_DIM, seq), BF16)],
        compiler_params=_cparams(("arbitrary", "arbitrary")),
        name="mla_attention",
    )(qf, kf, v)


def _cross_kernel(x_ref, nw_ref, wq_ref, kv_ref, wo_ref, o_ref):
    x = x_ref[...]
    q = _dot(_rms(x, nw_ref[...], NORM_EPS).astype(BF16), wq_ref[...])
    kv = kv_ref[...]
    scale = CA_HEAD_DIM ** -0.5
    outs = []
    for hh in range(CA_HEADS):
        sl = slice(hh * CA_HEAD_DIM, (hh + 1) * CA_HEAD_DIM)
        s = _dot_t(q[:, sl].astype(BF16), kv[:, sl]) * scale
        p = jnp.exp(s - jnp.max(s, axis=-1, keepdims=True))
        p = p / jnp.sum(p, axis=-1, keepdims=True)
        outs.append(_dot(p.astype(BF16), kv[:, CA_DIM + hh * CA_HEAD_DIM:CA_DIM + (hh + 1) * CA_HEAD_DIM]))
    o = jnp.concatenate(outs, axis=1).astype(BF16)
    o_ref[...] = x + _dot(o, wo_ref[...])


def cross_block(x, batch, norm_w, wq, kv, wo, *, tq=512):
    tokens, d = x.shape
    seq = tokens // batch
    tq = min(tq, seq)
    nq = seq // tq
    mem_len = kv.shape[0] // batch
    return pl.pallas_call(
        _cross_kernel,
        grid=(batch, nq),
        in_specs=[pl.BlockSpec((tq, d), lambda b, i: (b * nq + i, 0)),
                  pl.BlockSpec((1, d), lambda b, i: (0, 0)),
                  pl.BlockSpec(wq.shape, lambda b, i: (0, 0)),
                  pl.BlockSpec((mem_len, 2 * CA_DIM), lambda b, i: (b, 0)),
                  pl.BlockSpec(wo.shape, lambda b, i: (0, 0))],
        out_specs=pl.BlockSpec((tq, d), lambda b, i: (b * nq + i, 0)),
        out_shape=jax.ShapeDtypeStruct((tokens, d), F32),
        compiler_params=_cparams(("parallel", "parallel")),
        name="cross_block",
    )(x, norm_w.reshape(1, d), wq, kv, wo)


def _router_kernel(x_ref, nw_ref, wr_ref, br_ref, h_ref, gates_ref):
    h = _rms(x_ref[...], nw_ref[...], NORM_EPS)
    h_ref[...] = h.astype(h_ref.dtype)
    logits = _dot_x3(h, wr_ref[...])
    biased = logits + br_ref[...]
    lane = lax.broadcasted_iota(jnp.int32, logits.shape, 1)
    big = jnp.int32(LANES)

    def first_argmax(vals):
        mx = jnp.max(vals, axis=-1, keepdims=True)
        return jnp.min(jnp.where(vals == mx, lane, big), axis=-1, keepdims=True)

    def pick(vals, idx):
        return jnp.sum(jnp.where(lane == idx, vals, 0.0), axis=-1, keepdims=True)

    is_group = (lane >= MOE_EXPERTS) & (lane < MOE_EXPERTS + MOE_GROUPS)
    gl = jnp.where(is_group, logits, NEG_BIG)
    ge = jnp.exp(gl - jnp.max(gl, axis=-1, keepdims=True))
    gp = ge / jnp.sum(ge, axis=-1, keepdims=True)
    g_lane = first_argmax(jnp.where(is_group, biased, NEG_BIG))
    p_group = pick(gp, g_lane)
    lo = (g_lane - MOE_EXPERTS) * MOE_PER_GROUP
    in_group = (lane >= lo) & (lane < lo + MOE_PER_GROUP)
    eb = jnp.where(in_group, biased, NEG_BIG)
    i1 = first_argmax(eb)
    i2 = first_argmax(jnp.where(lane == i1, NEG_BIG, eb))
    l1, l2 = pick(logits, i1), pick(logits, i2)
    mx = jnp.maximum(l1, l2)
    e1, e2 = jnp.exp(l1 - mx), jnp.exp(l2 - mx)
    w1, w2 = e1 / (e1 + e2), e2 / (e1 + e2)
    gates_ref[...] = jnp.where(lane == i1, w1, jnp.where(lane == i2, w2, 0.0)) * p_group


def moe_router(x, norm_w, w_router, b_router, *, tm=512):
    tokens, d = x.shape
    tm = min(tm, tokens)
    return pl.pallas_call(
        _router_kernel,
        grid=(tokens // tm,),
        in_specs=[pl.BlockSpec((tm, d), lambda i: (i, 0)),
                  pl.BlockSpec((1, d), lambda i: (0, 0)),
                  pl.BlockSpec((d, LANES), lambda i: (0, 0)),
                  pl.BlockSpec((1, LANES), lambda i: (0, 0))],
        out_specs=[pl.BlockSpec((tm, d), lambda i: (i, 0)),
                   pl.BlockSpec((tm, LANES), lambda i: (i, 0))],
        out_shape=[jax.ShapeDtypeStruct((tokens, d), BF16),
                   jax.ShapeDtypeStruct((tokens, LANES), F32)],
        compiler_params=_cparams(("parallel",)),
        name="moe_router",
    )(x, norm_w.reshape(1, d), w_router, b_router)


def _moe_dense_kernel(h_ref, gates_ref, x_ref, wg_ref, wu_ref, wd_ref, o_ref):
    e = pl.program_id(1)

    @pl.when(e == 0)
    def _():
        o_ref[...] = x_ref[...]

    h = h_ref[...]
    gate_pre = _dot(h, wg_ref[...])
    hid = (gate_pre * jax.nn.sigmoid(gate_pre)) * _dot(h, wu_ref[...])
    y = _dot(hid.astype(BF16), wd_ref[...])
    gates = gates_ref[...]
    lane = lax.broadcasted_iota(jnp.int32, gates.shape, 1)
    gate = jnp.sum(jnp.where(lane == e, gates, 0.0), axis=-1, keepdims=True)
    o_ref[...] += gate * y


def moe_dense(h, gates, x, wg, wu, wd, *, tm=512):
    tokens, d = x.shape
    tm = min(tm, tokens)
    n_exp, _, de = wg.shape
    return pl.pallas_call(
        _moe_dense_kernel,
        grid=(tokens // tm, n_exp),
        in_specs=[pl.BlockSpec((tm, d), lambda i, e: (i, 0)),
                  pl.BlockSpec((tm, LANES), lambda i, e: (i, 0)),
                  pl.BlockSpec((tm, d), lambda i, e: (i, 0)),
                  pl.BlockSpec((None, d, de), lambda i, e: (e, 0, 0)),
                  pl.BlockSpec((None, d, de), lambda i, e: (e, 0, 0)),
                  pl.BlockSpec((None, de, d), lambda i, e: (e, 0, 0))],
        out_specs=pl.BlockSpec((tm, d), lambda i, e: (i, 0)),
        out_shape=jax.ShapeDtypeStruct((tokens, d), F32),
        compiler_params=_cparams(("parallel", "arbitrary")),
        name="moe_dense",
    )(h, gates, x, wg, wu, wd)


def _proj_weights(w_in_l, w_vres_l):
    d = w_in_l.shape[0]
    mla0 = RW_COLS + DF_COLS
    vres = jnp.zeros((d, RW_V_RANK), F32) if w_vres_l is None else w_vres_l
    part_f = [w_in_l[:, :RW_COLS],
              vres, jnp.zeros((d, LANES - RW_V_RANK), F32),
              w_in_l[:, mla0 + ML_Q_RANK + ML_KV_RANK:mla0 + ML_COLS], jnp.zeros((d, LANES - ML_ROPE), F32)]
    part_a = [w_in_l[:, RW_COLS:RW_COLS + DF_COLS],
              w_in_l[:, mla0 + ML_Q_RANK:mla0 + ML_Q_RANK + ML_KV_RANK],
              jnp.zeros((d, OFF_MQ - OFF_MKV - ML_KV_RANK), F32),
              w_in_l[:, mla0:mla0 + ML_Q_RANK]]
    return jnp.concatenate(part_f, axis=1).astype(BF16), jnp.concatenate(part_a, axis=1).astype(BF16)


def _pad_rows(w, rows, at=0):
    out = jnp.zeros((rows, w.shape[1]), w.dtype)
    return lax.dynamic_update_slice(out, w, (at, 0))


def kernel(x, mem, positions, rel_bias, final_norm, norm_mix, w_in, w_in_vres, w_out, tm_mu, tm_mu_vres, tm_w0, tm_w2, tm_a0, tm_a2, tm_v0, tm_v2, tm_g2, tm_k_k, tm_k_a, tm_r_k, tm_ln_w, tm_ln_b, da_lq1, da_lk1, da_lq2, da_lk2, da_subln, mla_q_norm, mla_wq_b, mla_kv_norm, mla_wkv_b, norm_cross, norm_mem, ca_wq, ca_wkv, ca_wo, norm_ffn, moe_w_group, moe_b_group, moe_w_expert, moe_b_expert, moe_w_gate, moe_w_up, moe_w_down):
    batch, seq, d = x.shape
    tokens = batch * seq
    depth = norm_mix.shape[0]
    xf = x.reshape(tokens, d)
    memf = mem.reshape(-1, d)
    positions = positions.astype(jnp.int32)

    head_of_lane = jnp.arange(RW_DIM) // RW_HEAD_DIM
    seg = (head_of_lane[:, None] == jnp.arange(LANES)[None, :]).astype(BF16)
    seg_t = seg.T
    row = lambda v: v.reshape(1, -1)

    v_first = None
    for l in range(depth):
        w_f, w_a = _proj_weights(w_in[l], None if l == 0 else w_in_vres[l - 1])
        proj = norm_matmul(xf, norm_mix[l], w_f, tn=PROJ_F_COLS // 2)
        pa = norm_matmul(xf, norm_mix[l], w_a, out_dtype=BF16, tn=PROJ_A_COLS // 2)

        mu = tm_mu[l]
        prm = dict(mu_r=row(mu[:RW_DIM]), mu_k=row(mu[RW_DIM:2 * RW_DIM]), mu_v=row(mu[2 * RW_DIM:3 * RW_DIM]),
                   mu_l=row(mu[3 * RW_DIM:]), w0=row(tm_w0[l]), a0=row(tm_a0[l]),
                   w2=_pad_rows(tm_w2[l], LANES, 0), a2=_pad_rows(tm_a2[l], LANES, RW_W_RANK),
                   g2=tm_g2[l].astype(BF16), k_k=row(tm_k_k[l]), k_a=row(tm_k_a[l]), r_k=row(tm_r_k[l]),
                   seg=seg, seg_t=seg_t)
        if l > 0:
            prm.update(mu_vr=jnp.pad(row(tm_mu_vres[l - 1]), ((0, 0), (0, LANES - RW_V_RANK))),
                       v0=row(tm_v0[l - 1]), v2=_pad_rows(tm_v2[l - 1], LANES, 0))
        r, lw, k, v, kap, beta, gate, bonus = rwkv_prep(proj, batch, v_first, prm)
        if l == 0:
            v_first = v
        o = rwkv_scan(r, lw, k, v, kap, beta, batch)
        y_a = rwkv_post(o, bonus, gate, tm_ln_w[l], tm_ln_b[l], seg, seg_t)

        lambda_init = 0.8 - 0.6 * math.exp(-0.3 * l)
        lam = (jnp.exp(jnp.sum(da_lq1[l] * da_lk1[l])) - jnp.exp(jnp.sum(da_lq2[l] * da_lk2[l])) + lambda_init)
        y_b = diff_attention(pa, positions, rel_bias, lam, lambda_init, da_subln[l])

        wq = mla_wq_b[l].reshape(ML_Q_RANK, ML_HEADS, ML_NOPE + ML_ROPE)
        wq_pe = jnp.pad(wq[:, :, ML_NOPE:], ((0, 0), (0, 0), (0, LANES - ML_ROPE)))
        wq_all = jnp.concatenate([wq[:, :, :ML_NOPE].reshape(ML_Q_RANK, -1),
                                  wq_pe.reshape(ML_Q_RANK, -1)], axis=1).astype(BF16)
        wkv = mla_wkv_b[l].reshape(ML_KV_RANK, ML_HEADS, ML_NOPE + ML_V)
        wkv = jnp.concatenate([wkv[:, :, :ML_NOPE].reshape(ML_KV_RANK, -1),
                               wkv[:, :, ML_NOPE:].reshape(ML_KV_RANK, -1)], axis=1).astype(BF16)
        qf, kf, v_mla = mla_prep(pa, proj, positions, mla_q_norm[l], mla_kv_norm[l], wq_all, wkv)
        y_c = mla_attention(qf, kf, v_mla, batch)

        wo = w_out[l].astype(BF16)
        xf = matmul_res([y_a, y_b, y_c],
                        [wo[:RW_DIM], wo[RW_DIM:RW_DIM + DF_DIM], wo[RW_DIM + DF_DIM:]], xf)

        kv_mem = norm_matmul(memf, norm_mem[l], ca_wkv[l].astype(BF16), out_dtype=BF16)
        xf = cross_block(xf, batch, norm_cross[l], ca_wq[l].astype(BF16), kv_mem, ca_wo[l].astype(BF16))

        w_router = jnp.concatenate(
            [moe_w_expert[l], moe_w_group[l], jnp.zeros((d, LANES - MOE_EXPERTS - MOE_GROUPS), F32)], axis=1)
        b_router = jnp.concatenate(
            [moe_b_expert[l], moe_b_group[l], jnp.zeros((LANES - MOE_EXPERTS - MOE_GROUPS,), F32)]).reshape(1, LANES)
        h, gates = moe_router(xf, norm_ffn[l], w_router, b_router)
        xf = moe_dense(h, gates, xf, moe_w_gate[l].astype(BF16), moe_w_up[l].astype(BF16),
                       moe_w_down[l].astype(BF16))

    out = rmsnorm(xf, final_norm)
    return out.reshape(batch, seq, d)
```

```python
import functools
import math

import jax
import jax.numpy as jnp
from jax import lax
from jax.experimental import pallas as pl
from jax.experimental.pallas import tpu as pltpu

F32 = jnp.float32
BF16 = jnp.bfloat16

NORM_EPS = 1e-6
ROPE_THETA = 10000.0

RW_HEADS = 16
RW_HEAD_DIM = 64
RW_DIM = RW_HEADS * RW_HEAD_DIM
RW_W_RANK = 64
RW_A_RANK = 64
RW_G_RANK = 128
RW_V_RANK = 32
RW_LORA = RW_W_RANK + RW_A_RANK + RW_G_RANK
RW_LN_EPS = 64e-5
RW_COLS = 3 * RW_DIM + RW_LORA

DF_HEADS = 4
DF_HEAD_DIM = 64
DF_V_DIM = 2 * DF_HEAD_DIM
DF_QK = DF_HEADS * 2 * DF_HEAD_DIM
DF_DIM = DF_HEADS * DF_V_DIM
DF_COLS = 2 * DF_QK + DF_DIM
DF_SUBLN_EPS = 1e-5

ML_HEADS = 4
ML_Q_RANK = 384
ML_KV_RANK = 256
ML_NOPE = 128
ML_ROPE = 64
ML_V = 128
ML_DIM = ML_HEADS * ML_V
ML_COLS = ML_Q_RANK + ML_KV_RANK + ML_ROPE

REL_BUCKETS = 32
REL_MAX_DIST = 128

CA_HEADS = 4
CA_HEAD_DIM = 128
CA_DIM = CA_HEADS * CA_HEAD_DIM

MOE_GROUPS = 4
MOE_PER_GROUP = 8
MOE_EXPERTS = MOE_GROUPS * MOE_PER_GROUP

LANES = 128
SCAN_CHUNK = 64
SCAN_GROUP = 4
ATTN_TILE = 512
MOE_ROW_TILE = 256
VMEM_LIMIT = 56 * 1024 * 1024
NEG_BIG = -1e30

LOG2E = 1.4426950408889634

OFF_R = 0
OFF_K = RW_DIM
OFF_V = 2 * RW_DIM
OFF_LORA = 3 * RW_DIM
OFF_VRES = OFF_LORA + RW_LORA
OFF_KPE = OFF_VRES + LANES
PROJ_F_COLS = OFF_KPE + LANES
OFF_DQ = 0
OFF_DK = OFF_DQ + DF_QK
OFF_DV = OFF_DK + DF_QK
OFF_MKV = OFF_DV + DF_DIM
OFF_MQ = 5 * ML_Q_RANK
PROJ_A_COLS = OFF_MQ + ML_Q_RANK


def _cparams(sem, vmem=VMEM_LIMIT):
    return pltpu.CompilerParams(dimension_semantics=sem, vmem_limit_bytes=vmem)


def _dot(a, b):
    return jnp.dot(a, b, preferred_element_type=F32)


def _dot_t(a, b):
    return lax.dot_general(a, b, (((1,), (1,)), ((), ())), preferred_element_type=F32)


def _split3(x):
    hi = x.astype(BF16)
    r1 = x - hi.astype(F32)
    mid = r1.astype(BF16)
    lo = (r1 - mid.astype(F32)).astype(BF16)
    return hi, mid, lo


def _dot_exact_rhs01(x, ones_bf16):
    h, m, l = _split3(x)
    return _dot(h, ones_bf16) + _dot(m, ones_bf16) + _dot(l, ones_bf16)


def _dot_x3(a, b):
    ah = a.astype(BF16)
    al = (a - ah.astype(F32)).astype(BF16)
    bh = b.astype(BF16)
    bl = (b - bh.astype(F32)).astype(BF16)
    return _dot(ah, bh) + _dot(ah, bl) + _dot(al, bh)


def _rms(x, w, eps):
    ms = jnp.mean(x * x, axis=-1, keepdims=True)
    return x * lax.rsqrt(ms + eps) * w


def _rmsnorm_kernel(x_ref, w_ref, o_ref, *, eps):
    o_ref[...] = _rms(x_ref[...], w_ref[...], eps).astype(o_ref.dtype)


def rmsnorm(x, w, *, eps=NORM_EPS, out_dtype=F32, tm=512):
    m, d = x.shape
    tm = min(tm, m)
    return pl.pallas_call(
        functools.partial(_rmsnorm_kernel, eps=eps),
        grid=(m // tm,),
        in_specs=[pl.BlockSpec((tm, d), lambda i: (i, 0)),
                  pl.BlockSpec((1, d), lambda i: (0, 0))],
        out_specs=pl.BlockSpec((tm, d), lambda i: (i, 0)),
        out_shape=jax.ShapeDtypeStruct((m, d), out_dtype),
        compiler_params=_cparams(("parallel",)),
        name="rmsnorm",
    )(x, w.reshape(1, d))


def _norm_matmul_kernel(x_ref, nw_ref, w_ref, o_ref, xn_ref, *, eps):
    @pl.when(pl.program_id(1) == 0)
    def _():
        xn_ref[...] = _rms(x_ref[...], nw_ref[...], eps).astype(BF16)

    o_ref[...] = _dot(xn_ref[...], w_ref[...]).astype(o_ref.dtype)


def norm_matmul(x, nw, w, *, out_dtype=F32, tm=512, tn=None, eps=NORM_EPS):
    m, d = x.shape
    n = w.shape[1]
    tm = min(tm, m)
    tn = n if tn is None else tn
    return pl.pallas_call(
        functools.partial(_norm_matmul_kernel, eps=eps),
        grid=(m // tm, n // tn),
        in_specs=[pl.BlockSpec((tm, d), lambda i, j: (i, 0)),
                  pl.BlockSpec((1, d), lambda i, j: (0, 0)),
                  pl.BlockSpec((d, tn), lambda i, j: (0, j))],
        out_specs=pl.BlockSpec((tm, tn), lambda i, j: (i, j)),
        out_shape=jax.ShapeDtypeStruct((m, n), out_dtype),
        scratch_shapes=[pltpu.VMEM((tm, d), BF16)],
        compiler_params=_cparams(("parallel", "arbitrary")),
        name="norm_matmul",
    )(x, nw.reshape(1, d), w)


def _matmul_res_kernel(*refs, n_a):
    a_refs, w_refs = refs[:n_a], refs[n_a:2 * n_a]
    res_ref, o_ref = refs[2 * n_a], refs[2 * n_a + 1]
    acc = res_ref[...]
    for a_ref, w_ref in zip(a_refs, w_refs):
        acc = acc + _dot(a_ref[...].astype(BF16), w_ref[...])
    o_ref[...] = acc


def matmul_res(a_list, w_list, res, *, tm=512, tn=1024):
    m, n = res.shape
    tm = min(tm, m)
    tn = min(tn, n)
    n_a = len(a_list)
    in_specs = ([pl.BlockSpec((tm, a.shape[1]), lambda i, j: (i, 0)) for a in a_list]
                + [pl.BlockSpec((w.shape[0], tn), lambda i, j: (0, j)) for w in w_list]
                + [pl.BlockSpec((tm, tn), lambda i, j: (i, j))])
    return pl.pallas_call(
        functools.partial(_matmul_res_kernel, n_a=n_a),
        grid=(m // tm, n // tn),
        in_specs=in_specs,
        out_specs=pl.BlockSpec((tm, tn), lambda i, j: (i, j)),
        out_shape=jax.ShapeDtypeStruct((m, n), F32),
        compiler_params=_cparams(("parallel", "arbitrary")),
        name="matmul_res",
    )(*a_list, *w_list, res)


def _softplus(z):
    return jnp.maximum(z, 0.0) + jnp.log(1.0 + jnp.exp(-jnp.abs(z)))


def _rwkv_prep_kernel(*refs, has_vres):
    if has_vres:
        (pr_ref, pk_ref, pv_ref, pl_ref, pvr_ref, vfirst_ref,
         mu_r, mu_k, mu_v, mu_l, mu_vr, w0, w2, a0, a2, g2, v0, v2,
         k_k, k_a, r_k, seg, seg_t,
         r_o, lw_o, k_o, v_o, kap_o, beta_o, g_o, bonus_o,
         last_r, last_k, last_v, last_l, last_vr) = refs
    else:
        (pr_ref, pk_ref, pv_ref, pl_ref,
         mu_r, mu_k, mu_v, mu_l, w0, w2, a0, a2, g2,
         k_k, k_a, r_k, seg, seg_t,
         r_o, lw_o, k_o, v_o, kap_o, beta_o, g_o, bonus_o,
         last_r, last_k, last_v, last_l) = refs
    t = pl.program_id(1)

    def shifted(p_ref, last_ref, mu_ref):
        p = p_ref[...]
        n = p.shape[0]
        carried = jnp.where(t == 0, 0.0, last_ref[0:1, :])
        row = lax.broadcasted_iota(jnp.int32, p.shape, 0)
        prev = jnp.where(row == 0, carried, pltpu.roll(p, 1, axis=0))
        last_ref[0:1, :] = p[n - 1:n, :]
        return p + mu_ref[...] * (prev - p)

    r = shifted(pr_ref, last_r, mu_r)
    k = shifted(pk_ref, last_k, mu_k)
    v = shifted(pv_ref, last_v, mu_v)
    lora = shifted(pl_ref, last_l, mu_l)
    wl = lora[:, :LANES]
    gl = lora[:, LANES:]

    lane = lax.broadcasted_iota(jnp.int32, wl.shape, 1)
    wl_t = jnp.where(lane < RW_W_RANK, jnp.tanh(wl), 0.0)
    al = jnp.where(lane >= RW_W_RANK, wl, 0.0)
    w_log = -_softplus(-(w0[...] + _dot_x3(wl_t, w2[...]))) - 0.5
    lw_o[...] = -jnp.exp(w_log)
    a = jax.nn.sigmoid(a0[...] + _dot_x3(al, a2[...]))
    g_o[...] = _dot(jax.nn.sigmoid(gl).astype(BF16), g2[...])

    segm, segm_t = seg[...], seg_t[...]

    def head_sum(x):
        return _dot_exact_rhs01(_dot_exact_rhs01(x, segm), segm_t)

    kk = k * k_k[...]
    kk = kk / jnp.maximum(jnp.sqrt(head_sum(kk * kk)), 1e-12)
    k = k * (1.0 + (a - 1.0) * k_a[...])
    if has_vres:
        vr = shifted(pvr_ref, last_vr, mu_vr)
        mix = jax.nn.sigmoid(v0[...] + _dot_x3(vr, v2[...]))
        v = v + (vfirst_ref[...] - v) * mix
    r_o[...] = r
    k_o[...] = k
    v_o[...] = v
    kap_o[...] = kk
    beta_o[...] = kk * a
    bonus_o[...] = head_sum(r * k * r_k[...]) * v


def rwkv_prep(proj, batch, vfirst, prm, *, tt=256):
    tokens = proj.shape[0]
    seq = tokens // batch
    tt = min(tt, seq)
    nt = seq // tt
    has_vres = vfirst is not None
    d = RW_DIM

    def rows(width, col):
        return pl.BlockSpec((tt, width), lambda b, t, col=col: (b * nt + t, col))

    def full(shape):
        return pl.BlockSpec(shape, lambda b, t: (0, 0))

    in_specs = [rows(d, OFF_R // d), rows(d, OFF_K // d), rows(d, OFF_V // d),
                rows(RW_LORA, OFF_LORA // RW_LORA)]
    args = [proj, proj, proj, proj]
    if has_vres:
        in_specs += [rows(LANES, OFF_VRES // LANES), rows(d, 0)]
        args += [proj, vfirst]
    names = ["mu_r", "mu_k", "mu_v", "mu_l"] + (["mu_vr"] if has_vres else []) + ["w0", "w2", "a0", "a2", "g2"]
    names += (["v0", "v2"] if has_vres else []) + ["k_k", "k_a", "r_k", "seg", "seg_t"]
    for nm in names:
        in_specs.append(full(prm[nm].shape))
        args.append(prm[nm])
    out_spec = pl.BlockSpec((tt, d), lambda b, t: (b * nt + t, 0))
    scratch = [pltpu.VMEM((8, d), F32)] * 3 + [pltpu.VMEM((8, RW_LORA), F32)]
    if has_vres:
        scratch.append(pltpu.VMEM((8, LANES), F32))
    return pl.pallas_call(
        functools.partial(_rwkv_prep_kernel, has_vres=has_vres),
        grid=(batch, nt),
        in_specs=in_specs,
        out_specs=[out_spec] * 8,
        out_shape=[jax.ShapeDtypeStruct((tokens, d), F32)] * 8,
        scratch_shapes=scratch,
        compiler_params=_cparams(("arbitrary", "arbitrary")),
        name="rwkv_prep",
    )(*args)


def _rwkv_scan_kernel(r_ref, lw_ref, k_ref, v_ref, kap_ref, beta_ref, tril_ref, bmask_ref,
                      o_ref, h_ref):
    @pl.when(pl.program_id(2) == 0)
    def _():
        h_ref[...] = jnp.zeros_like(h_ref)

    c, w = lw_ref.shape
    g = w // RW_HEAD_DIM
    rr = g * c
    lw = lw_ref[...]
    cum = _dot_exact_lhs01(tril_ref[...], lw)
    total = cum[c - 1:c, :]
    p_in = jnp.exp(cum)
    p_ex = jnp.exp(cum - lw)
    p_inv = jnp.exp(-cum)
    p_rem = jnp.exp(total - cum)
    bmask = bmask_ref[...]

    def stack(x):
        return (jnp.concatenate([x] * g, axis=0) * bmask).astype(BF16)

    kap, beta, k = kap_ref[...], beta_ref[...], k_ref[...]
    a_s = stack(-kap * p_ex)
    r_s = stack(r_ref[...] * p_in)
    b_s = stack(beta * p_inv)
    k_s = stack(k * p_inv)
    v_s = stack(v_ref[...])

    row = lax.broadcasted_iota(jnp.int32, (rr, rr), 0)
    col = lax.broadcasted_iota(jnp.int32, (rr, rr), 1)
    strict = row > col
    incl = row >= col
    ab = jnp.where(strict, _dot_t(a_s, b_s), 0.0)
    ak = jnp.where(strict, _dot_t(a_s, k_s), 0.0)
    rb = jnp.where(incl, _dot_t(r_s, b_s), 0.0)
    rk = jnp.where(incl, _dot_t(r_s, k_s), 0.0)

    h = h_ref[...]
    hb = h.astype(BF16)
    x = _dot(a_s, hb) + _dot(ak.astype(BF16), v_s)
    lp = ab
    n_sq = int(math.log2(c))
    for i in range(n_sq):
        lpb = lp.astype(BF16)
        x = x + _dot(lpb, x.astype(BF16))
        if i < n_sq - 1:
            lp = _dot(lpb, lpb)
    u = x.astype(BF16)
    ow = _dot(r_s, hb) + _dot(rb.astype(BF16), u) + _dot(rk.astype(BF16), v_s)
    o = ow[0:c]
    for i in range(1, g):
        o = o + ow[i * c:(i + 1) * c]
    o_ref[...] = o

    eye = (lax.broadcasted_iota(jnp.int32, (w, w), 0) == lax.broadcasted_iota(jnp.int32, (w, w), 1))
    diag = jnp.where(eye, jnp.exp(total), 0.0)
    p_col = _dot_exact_rhs01(diag, jnp.ones((w, w), BF16))
    z = jnp.concatenate([jnp.concatenate([beta * p_rem] * g, axis=0) * bmask,
                         jnp.concatenate([k * p_rem] * g, axis=0) * bmask], axis=0)
    uv = jnp.concatenate([u, v_s], axis=0)
    h_ref[...] = h * p_col + _dot(z.T.astype(BF16), uv)


def _dot_exact_lhs01(ones_bf16, x):
    h, m, l = _split3(x)
    return _dot(ones_bf16, h) + _dot(ones_bf16, m) + _dot(ones_bf16, l)


def rwkv_scan(r, lw, k, v, kap, beta, batch):
    tokens, d = r.shape
    seq = tokens // batch
    c = min(SCAN_CHUNK, seq)
    nc = seq // c
    gw = SCAN_GROUP * RW_HEAD_DIM
    rr = SCAN_GROUP * c
    tril = (jnp.arange(c)[:, None] >= jnp.arange(c)[None, :]).astype(BF16)
    bmask = (jnp.arange(rr)[:, None] // c == jnp.arange(gw)[None, :] // RW_HEAD_DIM).astype(F32)
    blk = pl.BlockSpec((c, gw), lambda b, g, i: (b * nc + i, g))
    return pl.pallas_call(
        _rwkv_scan_kernel,
        grid=(batch, d // gw, nc),
        in_specs=[blk] * 6 + [pl.BlockSpec((c, c), lambda b, g, i: (0, 0)),
                              pl.BlockSpec((rr, gw), lambda b, g, i: (0, 0))],
        out_specs=blk,
        out_shape=jax.ShapeDtypeStruct((tokens, d), F32),
        scratch_shapes=[pltpu.VMEM((gw, gw), F32)],
        compiler_params=_cparams(("arbitrary", "arbitrary", "arbitrary")),
        name="rwkv_scan",
    )(r, lw, k, v, kap, beta, tril, bmask)


def _rwkv_post_kernel(o_ref, bonus_ref, g_ref, lnw_ref, lnb_ref, seg, seg_t, y_ref):
    segm, segm_t = seg[...], seg_t[...]

    def head_mean(x):
        return _dot_exact_rhs01(_dot_exact_rhs01(x, segm), segm_t) * (1.0 / RW_HEAD_DIM)

    o = o_ref[...]
    dlt = o - head_mean(o)
    var = head_mean(dlt * dlt)
    y = dlt * lax.rsqrt(var + RW_LN_EPS) * lnw_ref[...] + lnb_ref[...]
    y_ref[...] = ((y + bonus_ref[...]) * g_ref[...]).astype(y_ref.dtype)


def rwkv_post(o, bonus, g, ln_w, ln_b, seg, seg_t, *, tm=512):
    tokens, d = o.shape
    tm = min(tm, tokens)
    blk = pl.BlockSpec((tm, d), lambda i: (i, 0))
    vec = pl.BlockSpec((1, d), lambda i: (0, 0))
    return pl.pallas_call(
        _rwkv_post_kernel,
        grid=(tokens // tm,),
        in_specs=[blk, blk, blk, vec, vec,
                  pl.BlockSpec(seg.shape, lambda i: (0, 0)), pl.BlockSpec(seg_t.shape, lambda i: (0, 0))],
        out_specs=blk,
        out_shape=jax.ShapeDtypeStruct((tokens, d), BF16),
        compiler_params=_cparams(("parallel",)),
        name="rwkv_post",
    )(o, bonus, g, ln_w.reshape(1, d), ln_b.reshape(1, d), seg, seg_t)


def _t5_thresholds():
    max_exact = REL_BUCKETS // 2
    thr = list(range(1, max_exact))
    n = max_exact
    for bucket in range(max_exact, REL_BUCKETS):
        while True:
            large = max_exact + int(math.log(max(n, max_exact) / max_exact)
                                    / math.log(REL_MAX_DIST / max_exact) * (REL_BUCKETS - max_exact))
            if min(large, REL_BUCKETS - 1) >= bucket:
                break
            n += 1
        thr.append(n)
    return thr


T5_THRESHOLDS = _t5_thresholds()
T5_FAR = T5_THRESHOLDS[-1]


def _softmax_tile(s_t, c, state, vt_tile):
    m_old, l_old, acc = state
    m_new = jnp.maximum(m_old, jnp.max(s_t, axis=0, keepdims=True) + c)
    alpha = jnp.exp2(m_old - m_new)
    p_t = jnp.exp2(s_t - (m_new - c))
    l_new = alpha * l_old + jnp.sum(p_t, axis=0, keepdims=True)
    acc = alpha * acc + _dot(vt_tile, p_t.astype(BF16))
    return m_new, l_new, acc


def _transpose_into(vt_ref, v_ref, chunk):
    seq = v_ref.shape[0]
    for c in range(seq // chunk):
        vt_ref[:, c * chunk:(c + 1) * chunk] = v_ref[c * chunk:(c + 1) * chunk, :].astype(F32).T.astype(BF16)


def _diff_attn_kernel(qfirst_ref, klast_ref, q_ref, k_ref, v_ref, qpos_ref, kpos_ref, subln_ref, table_ref, lam_ref,
                      o_ref, vt_ref, *, tq, tk, scale2, out_scale):
    b, i = pl.program_id(0), pl.program_id(1)
    nq = pl.num_programs(1)
    seq = k_ref.shape[0]
    nk = seq // tk
    w = DF_V_DIM

    @pl.when(i == 0)
    def _():
        _transpose_into(vt_ref, v_ref, tk)

    n_tiles = (i * tq + tq - 1) // tk + 1
    qf = qfirst_ref[b * nq + i]
    n_far = lax.while_loop(
        lambda j: (j * tk + tk - 1 <= i * tq) & (qf - klast_ref[b * nk + jnp.minimum(j, nk - 1)] >= T5_FAR),
        lambda j: j + 1, jnp.int32(0))

    dist = lax.broadcasted_iota(jnp.int32, (1, LANES), 1)
    qpos = qpos_ref[...]
    q_idx = i * tq + lax.broadcasted_iota(jnp.int32, (tk, tq), 1)
    k_off = lax.broadcasted_iota(jnp.int32, (tk, tq), 0)
    lane = lax.broadcasted_iota(jnp.int32, (tq, w), 1)

    bias_rows, c_far, qm = [], [], []
    for h in range(DF_HEADS):
        bias_vec = jnp.full((1, LANES), table_ref[h], F32)
        for bucket, thr in enumerate(T5_THRESHOLDS, start=1):
            bias_vec = jnp.where(dist >= thr, table_ref[bucket * DF_HEADS + h], bias_vec)
        bias_rows.append(jnp.broadcast_to(bias_vec * LOG2E, (tk, LANES)))
        c_far.append(table_ref[(REL_BUCKETS - 1) * DF_HEADS + h] * LOG2E)
        qh = q_ref[:, h * w:(h + 1) * w].astype(F32) * scale2
        qm.append([jnp.where((lane >= mi * DF_HEAD_DIM) & (lane < (mi + 1) * DF_HEAD_DIM), qh, 0.0).astype(BF16)
                   for mi in range(2)])

    def tiles(j, h):
        off = pl.multiple_of(j * tk, tk)
        return k_ref[pl.ds(off, tk), h * w:(h + 1) * w], vt_ref[h * w:(h + 1) * w, pl.ds(off, tk)], off

    def far_body(j, st):
        out = []
        for h in range(DF_HEADS):
            k_t, vt_t, _ = tiles(j, h)
            out += [_softmax_tile(_dot_t(k_t, qm[h][mi]), c_far[h], st[2 * h + mi], vt_t) for mi in range(2)]
        return tuple(out)

    def near_body(j, st):
        off = pl.multiple_of(j * tk, tk)
        n = jnp.clip(qpos - kpos_ref[pl.ds(off, tk), :], 0, LANES - 1)
        keep = q_idx >= off + k_off
        out = []
        for h in range(DF_HEADS):
            k_t, vt_t, _ = tiles(j, h)
            bias = jnp.concatenate(
                [jnp.take_along_axis(bias_rows[h], n[:, cb * LANES:(cb + 1) * LANES], axis=1)
                 for cb in range(tq // LANES)], axis=1)
            out += [_softmax_tile(jnp.where(keep, _dot_t(k_t, qm[h][mi]) + bias, NEG_BIG), 0.0,
                                  st[2 * h + mi], vt_t) for mi in range(2)]
        return tuple(out)

    init = tuple((jnp.full((1, tq), NEG_BIG, F32), jnp.zeros((1, tq), F32), jnp.zeros((w, tq), F32))
                 for _ in range(2 * DF_HEADS))
    st = lax.fori_loop(0, n_far, far_body, init)
    st = lax.fori_loop(n_far, n_tiles, near_body, st)
    for h in range(DF_HEADS):
        s0, s1 = st[2 * h], st[2 * h + 1]
        d_t = s0[2] / s0[1] - lam_ref[0] * (s1[2] / s1[1])
        ms = jnp.mean(d_t * d_t, axis=0, keepdims=True)
        y_t = d_t * lax.rsqrt(ms + DF_SUBLN_EPS) * (subln_ref[...] * out_scale)
        o_ref[:, h * w:(h + 1) * w] = y_t.T.astype(o_ref.dtype)


def diff_attention(pa, positions, rel_bias, lam, lambda_init, subln_w, *, tq=ATTN_TILE):
    batch, seq = positions.shape
    tokens = batch * seq
    tq = min(tq, seq)
    tk = tq
    nq, nk = seq // tq, seq // tk
    qfirst = positions[:, ::tq].reshape(-1)
    klast = positions[:, tk - 1::tk].reshape(-1)
    qpos = positions.reshape(batch, 1, seq)
    kpos = positions.reshape(batch, seq, 1)
    wd = DF_DIM
    grid_spec = pltpu.PrefetchScalarGridSpec(
        num_scalar_prefetch=2,
        grid=(batch, nq),
        in_specs=[pl.BlockSpec((tq, wd), lambda b, i, *_: (b * nq + i, OFF_DQ // wd)),
                  pl.BlockSpec((seq, wd), lambda b, i, *_: (b, OFF_DK // wd)),
                  pl.BlockSpec((seq, wd), lambda b, i, *_: (b, OFF_DV // wd)),
                  pl.BlockSpec((None, 1, tq), lambda b, i, *_: (b, 0, i)),
                  pl.BlockSpec((None, seq, 1), lambda b, i, *_: (b, 0, 0)),
                  pl.BlockSpec((DF_V_DIM, 1), lambda b, i, *_: (0, 0)),
                  pl.BlockSpec(memory_space=pltpu.SMEM),
                  pl.BlockSpec(memory_space=pltpu.SMEM)],
        out_specs=pl.BlockSpec((tq, wd), lambda b, i, *_: (b * nq + i, 0)),
        scratch_shapes=[pltpu.VMEM((wd, seq), BF16)],
    )
    return pl.pallas_call(
        functools.partial(_diff_attn_kernel, tq=tq, tk=tk, scale2=DF_HEAD_DIM ** -0.5 * LOG2E,
                          out_scale=1.0 - lambda_init),
        grid_spec=grid_spec,
        out_shape=jax.ShapeDtypeStruct((tokens, DF_DIM), BF16),
        compiler_params=_cparams(("arbitrary", "arbitrary")),
        name="diff_attention",
    )(qfirst, klast, pa, pa, pa, qpos, kpos, subln_w.reshape(DF_V_DIM, 1), rel_bias.reshape(-1), lam.reshape(1))


ML_QK_PAD = 2 * LANES


def _rope_block(x, cos, sin):
    half = ML_ROPE // 2
    lane = lax.broadcasted_iota(jnp.int32, x.shape, 1)
    rot = jnp.where(lane < half, -pltpu.roll(x, LANES - half, axis=1),
                    jnp.where(lane < ML_ROPE, pltpu.roll(x, half, axis=1), 0.0))
    return x * cos + rot * sin


def _mla_prep_kernel(mq_ref, mkv_ref, kpe_ref, pos_ref, qn_w, kvn_w, wq_ref, wkv_ref, freq_ref,
                     qf_o, kf_o, v_o, *, qscale):
    ang = pos_ref[...].astype(F32) * freq_ref[...]
    cos, sin = jnp.cos(ang), jnp.sin(ang)
    qc = _rms(mq_ref[...].astype(F32), qn_w[...], NORM_EPS).astype(BF16)
    q_all = _dot(qc, wq_ref[...]) * qscale
    kvc = _rms(mkv_ref[...].astype(F32), kvn_w[...], NORM_EPS).astype(BF16)
    kvb = _dot(kvc, wkv_ref[...])
    kpe = _rope_block(kpe_ref[...], cos, sin).astype(BF16)
    nope_w = ML_HEADS * ML_NOPE
    for h in range(ML_HEADS):
        lo = h * ML_QK_PAD
        qf_o[:, lo:lo + LANES] = q_all[:, h * LANES:(h + 1) * LANES].astype(BF16)
        qf_o[:, lo + LANES:lo + 2 * LANES] = _rope_block(
            q_all[:, nope_w + h * LANES:nope_w + (h + 1) * LANES], cos, sin).astype(BF16)
        kf_o[:, lo:lo + LANES] = kvb[:, h * LANES:(h + 1) * LANES].astype(BF16)
        kf_o[:, lo + LANES:lo + 2 * LANES] = kpe
    v_o[...] = kvb[:, nope_w:].astype(BF16)


def mla_prep(pa, pf, positions, q_norm, kv_norm, wq_all, wkv, *, tm=512):
    tokens = pa.shape[0]
    tm = min(tm, tokens)
    half = ML_ROPE // 2
    inv_freq = ROPE_THETA ** (-jnp.arange(half, dtype=F32) / half)
    freq = jnp.concatenate([inv_freq, inv_freq, jnp.zeros((LANES - ML_ROPE,), F32)]).reshape(1, LANES)

    def full(a):
        return pl.BlockSpec(a.shape, lambda i: (0, 0))

    qn_w = q_norm.reshape(1, -1)
    kvn_w = kv_norm.reshape(1, -1)
    wide = ML_HEADS * ML_QK_PAD
    return pl.pallas_call(
        functools.partial(_mla_prep_kernel, qscale=(ML_NOPE + ML_ROPE) ** -0.5 * LOG2E),
        grid=(tokens // tm,),
        in_specs=[pl.BlockSpec((tm, ML_Q_RANK), lambda i: (i, OFF_MQ // ML_Q_RANK)),
                  pl.BlockSpec((tm, ML_KV_RANK), lambda i: (i, OFF_MKV // ML_KV_RANK)),
                  pl.BlockSpec((tm, LANES), lambda i: (i, OFF_KPE // LANES)),
                  pl.BlockSpec((tm, 1), lambda i: (i, 0)),
                  full(qn_w), full(kvn_w), full(wq_all), full(wkv), full(freq)],
        out_specs=[pl.BlockSpec((tm, wide), lambda i: (i, 0)),
                   pl.BlockSpec((tm, wide), lambda i: (i, 0)),
                   pl.BlockSpec((tm, ML_DIM), lambda i: (i, 0))],
        out_shape=[jax.ShapeDtypeStruct((tokens, wide), BF16),
                   jax.ShapeDtypeStruct((tokens, wide), BF16),
                   jax.ShapeDtypeStruct((tokens, ML_DIM), BF16)],
        compiler_params=_cparams(("parallel",)),
        name="mla_prep",
    )(pa, pa, pf, positions.reshape(tokens, 1), qn_w, kvn_w, wq_all, wkv, freq)


def _mla_attn_kernel(q_ref, k_ref, v_ref, o_ref, vt_ref, *, tq, tk):
    i = pl.program_id(1)
    wq = ML_QK_PAD

    @pl.when(i == 0)
    def _():
        _transpose_into(vt_ref, v_ref, tk)

    n_tiles = (i * tq + tq - 1) // tk + 1
    n_full = (i * tq + 1) // tk
    q_idx = i * tq + lax.broadcasted_iota(jnp.int32, (tk, tq), 1)
    k_off = lax.broadcasted_iota(jnp.int32, (tk, tq), 0)
    qh = [q_ref[:, h * wq:(h + 1) * wq] for h in range(ML_HEADS)]

    def tiles(j, h):
        off = pl.multiple_of(j * tk, tk)
        return (k_ref[pl.ds(off, tk), h * wq:(h + 1) * wq],
                vt_ref[h * ML_V:(h + 1) * ML_V, pl.ds(off, tk)], off)

    def full_body(j, st):
        out = []
        for h in range(ML_HEADS):
            k_t, vt_t, _ = tiles(j, h)
            out.append(_softmax_tile(_dot_t(k_t, qh[h]), 0.0, st[h], vt_t))
        return tuple(out)

    def diag_body(j, st):
        out = []
        for h in range(ML_HEADS):
            k_t, vt_t, off = tiles(j, h)
            s_t = jnp.where(q_idx >= off + k_off, _dot_t(k_t, qh[h]), NEG_BIG)
            out.append(_softmax_tile(s_t, 0.0, st[h], vt_t))
        return tuple(out)

    st = tuple((jnp.full((1, tq), NEG_BIG, F32), jnp.zeros((1, tq), F32), jnp.zeros((ML_V, tq), F32))
               for _ in range(ML_HEADS))
    st = lax.fori_loop(0, n_full, full_body, st)
    st = lax.fori_loop(n_full, n_tiles, diag_body, st)
    for h in range(ML_HEADS):
        o_ref[:, h * ML_V:(h + 1) * ML_V] = (st[h][2] / st[h][1]).T.astype(o_ref.dtype)


def mla_attention(qf, kf, v, batch, *, tq=ATTN_TILE):
    tokens = qf.shape[0]
    seq = tokens // batch
    tq = min(tq, seq)
    tk = tq
    nq = seq // tq
    wide = qf.shape[1]
    return pl.pallas_call(
        functools.partial(_mla_attn_kernel, tq=tq, tk=tk),
        grid=(batch, nq),
        in_specs=[pl.BlockSpec((tq, wide), lambda b, i: (b * nq + i, 0)),
                  pl.BlockSpec((seq, wide), lambda b, i: (b, 0)),
                  pl.BlockSpec((seq, ML_DIM), lambda b, i: (b, 0))],
        out_specs=pl.BlockSpec((tq, ML_DIM), lambda b, i: (b * nq + i, 0)),
        out_shape=jax.ShapeDtypeStruct((tokens, ML_DIM), BF16),
        scratch_shapes=[pltpu.VMEM((ML_DIM, seq), BF16)],
        compiler_params=_cparams(("arbitrary", "arbitrary")),
        name="mla_attention",
    )(qf, kf, v)


def _cross_kernel(x_ref, nw_ref, wq_ref, kv_ref, wo_ref, o_ref):
    x = x_ref[...]
    q = _dot(_rms(x, nw_ref[...], NORM_EPS).astype(BF16), wq_ref[...])
    kv = kv_ref[...]
    scale = CA_HEAD_DIM ** -0.5
    outs = []
    for hh in range(CA_HEADS):
        sl = slice(hh * CA_HEAD_DIM, (hh + 1) * CA_HEAD_DIM)
        s = _dot_t(q[:, sl].astype(BF16), kv[:, sl]) * scale
        p = jnp.exp(s - jnp.max(s, axis=-1, keepdims=True))
        p = p / jnp.sum(p, axis=-1, keepdims=True)
        outs.append(_dot(p.astype(BF16), kv[:, CA_DIM + hh * CA_HEAD_DIM:CA_DIM + (hh + 1) * CA_HEAD_DIM]))
    o = jnp.concatenate(outs, axis=1).astype(BF16)
    o_ref[...] = x + _dot(o, wo_ref[...])


def cross_block(x, batch, norm_w, wq, kv, wo, *, tq=512):
    tokens, d = x.shape
    seq = tokens // batch
    tq = min(tq, seq)
    nq = seq // tq
    mem_len = kv.shape[0] // batch
    return pl.pallas_call(
        _cross_kernel,
        grid=(batch, nq),
        in_specs=[pl.BlockSpec((tq, d), lambda b, i: (b * nq + i, 0)),
                  pl.BlockSpec((1, d), lambda b, i: (0, 0)),
                  pl.BlockSpec(wq.shape, lambda b, i: (0, 0)),
                  pl.BlockSpec((mem_len, 2 * CA_DIM), lambda b, i: (b, 0)),
                  pl.BlockSpec(wo.shape, lambda b, i: (0, 0))],
        out_specs=pl.BlockSpec((tq, d), lambda b, i: (b * nq + i, 0)),
        out_shape=jax.ShapeDtypeStruct((tokens, d), F32),
        compiler_params=_cparams(("parallel", "parallel")),
        name="cross_block",
    )(x, norm_w.reshape(1, d), wq, kv, wo)


def _router_kernel(x_ref, nw_ref, wr_ref, br_ref, h_ref, sel_ref):
    h = _rms(x_ref[...], nw_ref[...], NORM_EPS)
    h_ref[...] = h.astype(h_ref.dtype)
    logits = _dot_x3(h, wr_ref[...])
    biased = logits + br_ref[...]
    lane = lax.broadcasted_iota(jnp.int32, logits.shape, 1)
    big = jnp.int32(LANES)

    def first_argmax(vals):
        mx = jnp.max(vals, axis=-1, keepdims=True)
        return jnp.min(jnp.where(vals == mx, lane, big), axis=-1, keepdims=True)

    def pick(vals, idx):
        return jnp.sum(jnp.where(lane == idx, vals, 0.0), axis=-1, keepdims=True)

    is_group = (lane >= MOE_EXPERTS) & (lane < MOE_EXPERTS + MOE_GROUPS)
    gl = jnp.where(is_group, logits, NEG_BIG)
    ge = jnp.exp(gl - jnp.max(gl, axis=-1, keepdims=True))
    gp = ge / jnp.sum(ge, axis=-1, keepdims=True)
    g_lane = first_argmax(jnp.where(is_group, biased, NEG_BIG))
    p_group = pick(gp, g_lane)
    lo = (g_lane - MOE_EXPERTS) * MOE_PER_GROUP
    in_group = (lane >= lo) & (lane < lo + MOE_PER_GROUP)
    eb = jnp.where(in_group, biased, NEG_BIG)
    i1 = first_argmax(eb)
    i2 = first_argmax(jnp.where(lane == i1, NEG_BIG, eb))
    l1, l2 = pick(logits, i1), pick(logits, i2)
    mx = jnp.maximum(l1, l2)
    e1, e2 = jnp.exp(l1 - mx), jnp.exp(l2 - mx)
    w1, w2 = e1 / (e1 + e2), e2 / (e1 + e2)
    sel_ref[...] = jnp.where(lane == SEL_E1, i1.astype(F32),
                             jnp.where(lane == SEL_E2, i2.astype(F32),
                                       jnp.where(lane == SEL_G1, w1 * p_group,
                                                 jnp.where(lane == SEL_G2, w2 * p_group, 0.0))))


SEL_E1, SEL_E2, SEL_G1, SEL_G2 = 0, 1, 2, 3


def moe_router(x, norm_w, w_router, b_router, *, tm=512):
    tokens, d = x.shape
    tm = min(tm, tokens)
    return pl.pallas_call(
        _router_kernel,
        grid=(tokens // tm,),
        in_specs=[pl.BlockSpec((tm, d), lambda i: (i, 0)),
                  pl.BlockSpec((1, d), lambda i: (0, 0)),
                  pl.BlockSpec((d, LANES), lambda i: (0, 0)),
                  pl.BlockSpec((1, LANES), lambda i: (0, 0))],
        out_specs=[pl.BlockSpec((tm, d), lambda i: (i, 0)),
                   pl.BlockSpec((tm, LANES), lambda i: (i, 0))],
        out_shape=[jax.ShapeDtypeStruct((tokens, d), F32),
                   jax.ShapeDtypeStruct((tokens, LANES), F32)],
        compiler_params=_cparams(("parallel",)),
        name="moe_router",
    )(x, norm_w.reshape(1, d), w_router, b_router)


def _moe_rank_kernel(sel_ref, ltri_ref, rank_ref, counts_ref, carry_ref):
    @pl.when(pl.program_id(0) == 0)
    def _():
        carry_ref[...] = jnp.zeros_like(carry_ref)

    sel = sel_ref[...]
    lane = lax.broadcasted_iota(jnp.int32, sel.shape, 1)
    lane_f = lane.astype(F32)
    oh1 = lane_f == sel[:, SEL_E1:SEL_E1 + 1]
    oh2 = lane_f == sel[:, SEL_E2:SEL_E2 + 1]
    f1, f2 = oh1.astype(F32), oh2.astype(F32)
    ltri = ltri_ref[...]
    before1 = _dot(ltri, f1.astype(BF16))
    before2 = _dot(ltri, f2.astype(BF16))
    c1 = jnp.sum(f1, axis=0, keepdims=True)
    c2 = jnp.sum(f2, axis=0, keepdims=True)
    carry = carry_ref[...]
    r1 = jnp.sum(jnp.where(oh1, before1 + carry, 0.0), axis=1, keepdims=True)
    r2 = jnp.sum(jnp.where(oh2, before2 + carry + c1, 0.0), axis=1, keepdims=True)
    rank_ref[...] = jnp.where(lane == SEL_E1, r1, jnp.where(lane == SEL_E2, r2, 0.0)).astype(jnp.int32)
    total = carry + c1 + c2
    carry_ref[...] = total
    counts_ref[...] = total.astype(jnp.int32)


def moe_rank(sel, *, tm=512):
    tokens = sel.shape[0]
    tm = min(tm, tokens)
    ltri = (jnp.arange(tm)[:, None] > jnp.arange(tm)[None, :]).astype(BF16)
    return pl.pallas_call(
        _moe_rank_kernel,
        grid=(tokens // tm,),
        in_specs=[pl.BlockSpec((tm, LANES), lambda i: (i, 0)),
                  pl.BlockSpec((tm, tm), lambda i: (0, 0))],
        out_specs=[pl.BlockSpec((tm, LANES), lambda i: (i, 0)),
                   pl.BlockSpec((1, LANES), lambda i: (0, 0))],
        out_shape=[jax.ShapeDtypeStruct((tokens, LANES), jnp.int32),
                   jax.ShapeDtypeStruct((1, LANES), jnp.int32)],
        scratch_shapes=[pltpu.VMEM((1, LANES), F32)],
        compiler_params=_cparams(("arbitrary",)),
        name="moe_rank",
    )(sel, ltri)


def _row_copy(src_ref, src_row, dst_ref, dst_row, sem):
    return pltpu.make_async_copy(src_ref.at[pl.ds(src_row, 1)], dst_ref.at[pl.ds(dst_row, 1)], sem)


def _moe_dispatch_kernel(dest_ref, h_ref, init_ref, xs_ref, sem, *, tm):
    del init_ref
    base = pl.program_id(0) * (2 * tm)

    def issue(r, carry):
        for s in range(2):
            _row_copy(h_ref, r, xs_ref, dest_ref[base + 2 * r + s], sem).start()
        return carry

    def drain(r, carry):
        for s in range(2):
            _row_copy(h_ref, 0, xs_ref, 0, sem).wait()
        return carry

    lax.fori_loop(0, tm, issue, 0)
    lax.fori_loop(0, tm, drain, 0)


def moe_dispatch(h, dest, rows, *, tm=256):
    tokens, d = h.shape
    tm = min(tm, tokens)
    grid_spec = pltpu.PrefetchScalarGridSpec(
        num_scalar_prefetch=1,
        grid=(tokens // tm,),
        in_specs=[pl.BlockSpec((tm, d), lambda i, *_: (i, 0)),
                  pl.BlockSpec(memory_space=pl.ANY)],
        out_specs=pl.BlockSpec(memory_space=pl.ANY),
        scratch_shapes=[pltpu.SemaphoreType.DMA(())],
    )
    return pl.pallas_call(
        functools.partial(_moe_dispatch_kernel, tm=tm),
        grid_spec=grid_spec,
        out_shape=jax.ShapeDtypeStruct((rows, d), h.dtype),
        input_output_aliases={2: 0},
        compiler_params=_cparams(("arbitrary",)),
        name="moe_dispatch",
    )(dest, h, jnp.zeros((rows, d), h.dtype))


def _moe_expert_kernel(te_ref, nused_ref, xs_ref, wg_ref, wu_ref, wd_ref, ys_ref, wgb, wub, wdb):
    r = pl.program_id(0)
    used = r < nused_ref[0]
    changed = (r == 0) | (te_ref[r] != te_ref[jnp.maximum(r - 1, 0)])

    @pl.when(used & changed)
    def _():
        wgb[...] = wg_ref[...].astype(BF16)
        wub[...] = wu_ref[...].astype(BF16)
        wdb[...] = wd_ref[...].astype(BF16)

    @pl.when(used)
    def _():
        x = xs_ref[...].astype(BF16)
        gate_pre = _dot(x, wgb[...])
        hid = (gate_pre * jax.nn.sigmoid(gate_pre)) * _dot(x, wub[...])
        ys_ref[...] = _dot(hid.astype(BF16), wdb[...])

    @pl.when(jnp.logical_not(used))
    def _():
        ys_ref[...] = jnp.zeros_like(ys_ref)


def moe_experts(xs, tile_expert, n_used, wg, wu, wd, *, tm):
    rows, d = xs.shape
    _, _, de = wg.shape
    grid_spec = pltpu.PrefetchScalarGridSpec(
        num_scalar_prefetch=2,
        grid=(rows // tm,),
        in_specs=[pl.BlockSpec((tm, d), lambda r, te, nu: (r, 0)),
                  pl.BlockSpec((None, d, de), lambda r, te, nu: (te[r], 0, 0)),
                  pl.BlockSpec((None, d, de), lambda r, te, nu: (te[r], 0, 0)),
                  pl.BlockSpec((None, de, d), lambda r, te, nu: (te[r], 0, 0))],
        out_specs=pl.BlockSpec((tm, d), lambda r, te, nu: (r, 0)),
        scratch_shapes=[pltpu.VMEM((d, de), BF16), pltpu.VMEM((d, de), BF16), pltpu.VMEM((de, d), BF16)],
    )
    return pl.pallas_call(
        _moe_expert_kernel,
        grid_spec=grid_spec,
        out_shape=jax.ShapeDtypeStruct((rows, d), F32),
        compiler_params=_cparams(("arbitrary",)),
        name="moe_experts",
    )(tile_expert, n_used, xs, wg, wu, wd)


def _moe_combine_kernel(dest_ref, x_ref, sel_ref, ys_ref, o_ref, buf_ref, sem, *, tm):
    base = pl.program_id(0) * (2 * tm)

    def issue(r, carry):
        for s in range(2):
            _row_copy(ys_ref, dest_ref[base + 2 * r + s], buf_ref.at[s], r, sem).start()
        return carry

    def drain(r, carry):
        for s in range(2):
            _row_copy(ys_ref, 0, buf_ref.at[s], 0, sem).wait()
        return carry

    lax.fori_loop(0, tm, issue, 0)
    lax.fori_loop(0, tm, drain, 0)
    sel = sel_ref[...]
    o_ref[...] = (x_ref[...] + sel[:, SEL_G1:SEL_G1 + 1] * buf_ref[0] + sel[:, SEL_G2:SEL_G2 + 1] * buf_ref[1])


def moe_combine(x, sel, ys, dest, *, tm=256):
    tokens, d = x.shape
    tm = min(tm, tokens)
    grid_spec = pltpu.PrefetchScalarGridSpec(
        num_scalar_prefetch=1,
        grid=(tokens // tm,),
        in_specs=[pl.BlockSpec((tm, d), lambda i, *_: (i, 0)),
                  pl.BlockSpec((tm, LANES), lambda i, *_: (i, 0)),
                  pl.BlockSpec(memory_space=pl.ANY)],
        out_specs=pl.BlockSpec((tm, d), lambda i, *_: (i, 0)),
        scratch_shapes=[pltpu.VMEM((2, tm, d), F32), pltpu.SemaphoreType.DMA(())],
    )
    return pl.pallas_call(
        functools.partial(_moe_combine_kernel, tm=tm),
        grid_spec=grid_spec,
        out_shape=jax.ShapeDtypeStruct((tokens, d), F32),
        compiler_params=_cparams(("arbitrary",)),
        name="moe_combine",
    )(dest, x, sel, ys)


def moe_block(x, norm_w, w_router, b_router, wg, wu, wd, *, tm=MOE_ROW_TILE):
    tokens, d = x.shape
    n_exp = wg.shape[0]
    h, sel = moe_router(x, norm_w, w_router, b_router)
    rank, counts = moe_rank(sel)
    padded = (counts[0, :n_exp] + (tm - 1)) // tm * tm
    ends = jnp.cumsum(padded)
    starts = ends - padded
    experts = sel[:, SEL_E1:SEL_E2 + 1].astype(jnp.int32)
    dest = (starts[experts] + rank[:, SEL_E1:SEL_E2 + 1]).reshape(-1)
    rows = 2 * tokens + n_exp * tm
    tile_start = jnp.arange(rows // tm, dtype=jnp.int32) * tm
    tile_expert = jnp.minimum(jnp.searchsorted(ends, tile_start, side="right"), n_exp - 1).astype(jnp.int32)
    n_used = (ends[-1] // tm).astype(jnp.int32).reshape(1)
    xs = moe_dispatch(h, dest, rows)
    ys = moe_experts(xs, tile_expert, n_used, wg, wu, wd, tm=tm)
    return moe_combine(x, sel, ys, dest)


def _proj_weights(w_in_l, w_vres_l):
    d = w_in_l.shape[0]
    mla0 = RW_COLS + DF_COLS
    vres = jnp.zeros((d, RW_V_RANK), F32) if w_vres_l is None else w_vres_l
    part_f = [w_in_l[:, :RW_COLS],
              vres, jnp.zeros((d, LANES - RW_V_RANK), F32),
              w_in_l[:, mla0 + ML_Q_RANK + ML_KV_RANK:mla0 + ML_COLS], jnp.zeros((d, LANES - ML_ROPE), F32)]
    part_a = [w_in_l[:, RW_COLS:RW_COLS + DF_COLS],
              w_in_l[:, mla0 + ML_Q_RANK:mla0 + ML_Q_RANK + ML_KV_RANK],
              jnp.zeros((d, OFF_MQ - OFF_MKV - ML_KV_RANK), F32),
              w_in_l[:, mla0:mla0 + ML_Q_RANK]]
    return jnp.concatenate(part_f, axis=1).astype(BF16), jnp.concatenate(part_a, axis=1).astype(BF16)


def _pad_rows(w, rows, at=0):
    out = jnp.zeros((rows, w.shape[1]), w.dtype)
    return lax.dynamic_update_slice(out, w, (at, 0))


def kernel(x, mem, positions, rel_bias, final_norm, norm_mix, w_in, w_in_vres, w_out, tm_mu, tm_mu_vres, tm_w0, tm_w2, tm_a0, tm_a2, tm_v0, tm_v2, tm_g2, tm_k_k, tm_k_a, tm_r_k, tm_ln_w, tm_ln_b, da_lq1, da_lk1, da_lq2, da_lk2, da_subln, mla_q_norm, mla_wq_b, mla_kv_norm, mla_wkv_b, norm_cross, norm_mem, ca_wq, ca_wkv, ca_wo, norm_ffn, moe_w_group, moe_b_group, moe_w_expert, moe_b_expert, moe_w_gate, moe_w_up, moe_w_down):
    batch, seq, d = x.shape
    tokens = batch * seq
    depth = norm_mix.shape[0]
    xf = x.reshape(tokens, d)
    memf = mem.reshape(-1, d)
    positions = positions.astype(jnp.int32)

    head_of_lane = jnp.arange(RW_DIM) // RW_HEAD_DIM
    seg = (head_of_lane[:, None] == jnp.arange(LANES)[None, :]).astype(BF16)
    seg_t = seg.T
    row = lambda v: v.reshape(1, -1)

    v_first = None
    for l in range(depth):
        w_f, w_a = _proj_weights(w_in[l], None if l == 0 else w_in_vres[l - 1])
        proj = norm_matmul(xf, norm_mix[l], w_f, tn=PROJ_F_COLS // 2)
        pa = norm_matmul(xf, norm_mix[l], w_a, out_dtype=BF16, tn=PROJ_A_COLS // 2)

        mu = tm_mu[l]
        prm = dict(mu_r=row(mu[:RW_DIM]), mu_k=row(mu[RW_DIM:2 * RW_DIM]), mu_v=row(mu[2 * RW_DIM:3 * RW_DIM]),
                   mu_l=row(mu[3 * RW_DIM:]), w0=row(tm_w0[l]), a0=row(tm_a0[l]),
                   w2=_pad_rows(tm_w2[l], LANES, 0), a2=_pad_rows(tm_a2[l], LANES, RW_W_RANK),
                   g2=tm_g2[l].astype(BF16), k_k=row(tm_k_k[l]), k_a=row(tm_k_a[l]), r_k=row(tm_r_k[l]),
                   seg=seg, seg_t=seg_t)
        if l > 0:
            prm.update(mu_vr=jnp.pad(row(tm_mu_vres[l - 1]), ((0, 0), (0, LANES - RW_V_RANK))),
                       v0=row(tm_v0[l - 1]), v2=_pad_rows(tm_v2[l - 1], LANES, 0))
        r, lw, k, v, kap, beta, gate, bonus = rwkv_prep(proj, batch, v_first, prm)
        if l == 0:
            v_first = v
        o = rwkv_scan(r, lw, k, v, kap, beta, batch)
        y_a = rwkv_post(o, bonus, gate, tm_ln_w[l], tm_ln_b[l], seg, seg_t)

        lambda_init = 0.8 - 0.6 * math.exp(-0.3 * l)
        lam = (jnp.exp(jnp.sum(da_lq1[l] * da_lk1[l])) - jnp.exp(jnp.sum(da_lq2[l] * da_lk2[l])) + lambda_init)
        y_b = diff_attention(pa, positions, rel_bias, lam, lambda_init, da_subln[l])

        wq = mla_wq_b[l].reshape(ML_Q_RANK, ML_HEADS, ML_NOPE + ML_ROPE)
        wq_pe = jnp.pad(wq[:, :, ML_NOPE:], ((0, 0), (0, 0), (0, LANES - ML_ROPE)))
        wq_all = jnp.concatenate([wq[:, :, :ML_NOPE].reshape(ML_Q_RANK, -1),
                                  wq_pe.reshape(ML_Q_RANK, -1)], axis=1).astype(BF16)
        wkv = mla_wkv_b[l].reshape(ML_KV_RANK, ML_HEADS, ML_NOPE + ML_V)
        wkv = jnp.concatenate([wkv[:, :, :ML_NOPE].reshape(ML_KV_RANK, -1),
                               wkv[:, :, ML_NOPE:].reshape(ML_KV_RANK, -1)], axis=1).astype(BF16)
        qf, kf, v_mla = mla_prep(pa, proj, positions, mla_q_norm[l], mla_kv_norm[l], wq_all, wkv)
        y_c = mla_attention(qf, kf, v_mla, batch)

        wo = w_out[l].astype(BF16)
        xf = matmul_res([y_a, y_b, y_c],
                        [wo[:RW_DIM], wo[RW_DIM:RW_DIM + DF_DIM], wo[RW_DIM + DF_DIM:]], xf)

        kv_mem = norm_matmul(memf, norm_mem[l], ca_wkv[l].astype(BF16), out_dtype=BF16)
        xf = cross_block(xf, batch, norm_cross[l], ca_wq[l].astype(BF16), kv_mem, ca_wo[l].astype(BF16))

        w_router = jnp.concatenate(
            [moe_w_expert[l], moe_w_group[l], jnp.zeros((d, LANES - MOE_EXPERTS - MOE_GROUPS), F32)], axis=1)
        b_router = jnp.concatenate(
            [moe_b_expert[l], moe_b_group[l], jnp.zeros((LANES - MOE_EXPERTS - MOE_GROUPS,), F32)]).reshape(1, LANES)
        xf = moe_block(xf, norm_ffn[l], w_router, b_router, moe_w_gate[l], moe_w_up[l], moe_w_down[l])

    out = rmsnorm(xf, final_norm)
    return out.reshape(batch, seq, d)
```

```python
import functools
import math

import jax
import jax.numpy as jnp
from jax import lax
from jax.experimental import pallas as pl
from jax.experimental.pallas import tpu as pltpu

F32 = jnp.float32
BF16 = jnp.bfloat16

NORM_EPS = 1e-6
ROPE_THETA = 10000.0

RW_HEADS = 16
RW_HEAD_DIM = 64
RW_DIM = RW_HEADS * RW_HEAD_DIM
RW_W_RANK = 64
RW_A_RANK = 64
RW_G_RANK = 128
RW_V_RANK = 32
RW_LORA = RW_W_RANK + RW_A_RANK + RW_G_RANK
RW_LN_EPS = 64e-5
RW_COLS = 3 * RW_DIM + RW_LORA

DF_HEADS = 4
DF_HEAD_DIM = 64
DF_V_DIM = 2 * DF_HEAD_DIM
DF_QK = DF_HEADS * 2 * DF_HEAD_DIM
DF_DIM = DF_HEADS * DF_V_DIM
DF_COLS = 2 * DF_QK + DF_DIM
DF_SUBLN_EPS = 1e-5

ML_HEADS = 4
ML_Q_RANK = 384
ML_KV_RANK = 256
ML_NOPE = 128
ML_ROPE = 64
ML_V = 128
ML_DIM = ML_HEADS * ML_V
ML_COLS = ML_Q_RANK + ML_KV_RANK + ML_ROPE

REL_BUCKETS = 32
REL_MAX_DIST = 128

CA_HEADS = 4
CA_HEAD_DIM = 128
CA_DIM = CA_HEADS * CA_HEAD_DIM

MOE_GROUPS = 4
MOE_PER_GROUP = 8
MOE_EXPERTS = MOE_GROUPS * MOE_PER_GROUP

LANES = 128
SCAN_CHUNK = 64
SCAN_GROUP = 4
ATTN_TILE = 512
MOE_ROW_TILE = 256
VMEM_LIMIT = 56 * 1024 * 1024
NEG_BIG = -1e30

LOG2E = 1.4426950408889634

OFF_R = 0
OFF_K = RW_DIM
OFF_V = 2 * RW_DIM
OFF_LORA = 3 * RW_DIM
OFF_VRES = OFF_LORA + RW_LORA
OFF_KPE = OFF_VRES + LANES
PROJ_F_COLS = OFF_KPE + LANES
OFF_DQ = 0
OFF_DK = OFF_DQ + DF_QK
OFF_DV = OFF_DK + DF_QK
OFF_MKV = OFF_DV + DF_DIM
OFF_MQ = 5 * ML_Q_RANK
PROJ_A_COLS = OFF_MQ + ML_Q_RANK


def _cparams(sem, vmem=VMEM_LIMIT, flags=None):
    return pltpu.CompilerParams(dimension_semantics=sem, vmem_limit_bytes=vmem, flags=flags)


def _dot(a, b):
    return jnp.dot(a, b, preferred_element_type=F32)


def _dot_t(a, b):
    return lax.dot_general(a, b, (((1,), (1,)), ((), ())), preferred_element_type=F32)


def _split3(x):
    hi = x.astype(BF16)
    r1 = x - hi.astype(F32)
    mid = r1.astype(BF16)
    lo = (r1 - mid.astype(F32)).astype(BF16)
    return hi, mid, lo


def _dot_exact_rhs01(x, ones_bf16):
    h, m, l = _split3(x)
    return _dot(h, ones_bf16) + _dot(m, ones_bf16) + _dot(l, ones_bf16)


def _dot_x3(a, b):
    ah = a.astype(BF16)
    al = (a - ah.astype(F32)).astype(BF16)
    bh = b.astype(BF16)
    bl = (b - bh.astype(F32)).astype(BF16)
    return _dot(ah, bh) + _dot(ah, bl) + _dot(al, bh)


def _rms(x, w, eps):
    ms = jnp.mean(x * x, axis=-1, keepdims=True)
    return x * lax.rsqrt(ms + eps) * w


def _rmsnorm_kernel(x_ref, w_ref, o_ref, *, eps):
    o_ref[...] = _rms(x_ref[...], w_ref[...], eps).astype(o_ref.dtype)


def rmsnorm(x, w, *, eps=NORM_EPS, out_dtype=F32, tm=512):
    m, d = x.shape
    tm = min(tm, m)
    return pl.pallas_call(
        functools.partial(_rmsnorm_kernel, eps=eps),
        grid=(m // tm,),
        in_specs=[pl.BlockSpec((tm, d), lambda i: (i, 0)),
                  pl.BlockSpec((1, d), lambda i: (0, 0))],
        out_specs=pl.BlockSpec((tm, d), lambda i: (i, 0)),
        out_shape=jax.ShapeDtypeStruct((m, d), out_dtype),
        compiler_params=_cparams(("parallel",)),
        name="rmsnorm",
    )(x, w.reshape(1, d))


def _norm_matmul_kernel(x_ref, nw_ref, w_ref, o_ref, xn_ref, *, eps):
    @pl.when(pl.program_id(1) == 0)
    def _():
        xn_ref[...] = _rms(x_ref[...], nw_ref[...], eps).astype(BF16)

    o_ref[...] = _dot(xn_ref[...], w_ref[...]).astype(o_ref.dtype)


def norm_matmul(x, nw, w, *, out_dtype=F32, tm=512, tn=None, eps=NORM_EPS):
    m, d = x.shape
    n = w.shape[1]
    tm = min(tm, m)
    tn = n if tn is None else tn
    return pl.pallas_call(
        functools.partial(_norm_matmul_kernel, eps=eps),
        grid=(m // tm, n // tn),
        in_specs=[pl.BlockSpec((tm, d), lambda i, j: (i, 0)),
                  pl.BlockSpec((1, d), lambda i, j: (0, 0)),
                  pl.BlockSpec((d, tn), lambda i, j: (0, j))],
        out_specs=pl.BlockSpec((tm, tn), lambda i, j: (i, j)),
        out_shape=jax.ShapeDtypeStruct((m, n), out_dtype),
        scratch_shapes=[pltpu.VMEM((tm, d), BF16)],
        compiler_params=_cparams(("parallel", "arbitrary")),
        name="norm_matmul",
    )(x, nw.reshape(1, d), w)


def _matmul_res_kernel(*refs, n_a):
    a_refs, w_refs = refs[:n_a], refs[n_a:2 * n_a]
    res_ref, o_ref = refs[2 * n_a], refs[2 * n_a + 1]
    acc = res_ref[...]
    for a_ref, w_ref in zip(a_refs, w_refs):
        acc = acc + _dot(a_ref[...].astype(BF16), w_ref[...])
    o_ref[...] = acc


def matmul_res(a_list, w_list, res, *, tm=512, tn=1024):
    m, n = res.shape
    tm = min(tm, m)
    tn = min(tn, n)
    n_a = len(a_list)
    in_specs = ([pl.BlockSpec((tm, a.shape[1]), lambda i, j: (i, 0)) for a in a_list]
                + [pl.BlockSpec((w.shape[0], tn), lambda i, j: (0, j)) for w in w_list]
                + [pl.BlockSpec((tm, tn), lambda i, j: (i, j))])
    return pl.pallas_call(
        functools.partial(_matmul_res_kernel, n_a=n_a),
        grid=(m // tm, n // tn),
        in_specs=in_specs,
        out_specs=pl.BlockSpec((tm, tn), lambda i, j: (i, j)),
        out_shape=jax.ShapeDtypeStruct((m, n), F32),
        compiler_params=_cparams(("parallel", "arbitrary")),
        name="matmul_res",
    )(*a_list, *w_list, res)


def _softplus(z):
    return jnp.maximum(z, 0.0) + jnp.log(1.0 + jnp.exp(-jnp.abs(z)))


def _rwkv_prep_kernel(*refs, has_vres):
    if has_vres:
        (pr_ref, pk_ref, pv_ref, pl_ref, pvr_ref, vfirst_ref,
         mu_r, mu_k, mu_v, mu_l, mu_vr, w0, w2, a0, a2, g2, v0, v2,
         k_k, k_a, r_k, seg, seg_t,
         r_o, lw_o, k_o, v_o, kap_o, beta_o, g_o, bonus_o,
         last_r, last_k, last_v, last_l, last_vr) = refs
    else:
        (pr_ref, pk_ref, pv_ref, pl_ref,
         mu_r, mu_k, mu_v, mu_l, w0, w2, a0, a2, g2,
         k_k, k_a, r_k, seg, seg_t,
         r_o, lw_o, k_o, v_o, kap_o, beta_o, g_o, bonus_o,
         last_r, last_k, last_v, last_l) = refs
    t = pl.program_id(1)

    def shifted(p_ref, last_ref, mu_ref):
        p = p_ref[...]
        n = p.shape[0]
        carried = jnp.where(t == 0, 0.0, last_ref[0:1, :])
        row = lax.broadcasted_iota(jnp.int32, p.shape, 0)
        prev = jnp.where(row == 0, carried, pltpu.roll(p, 1, axis=0))
        last_ref[0:1, :] = p[n - 1:n, :]
        return p + mu_ref[...] * (prev - p)

    r = shifted(pr_ref, last_r, mu_r)
    k = shifted(pk_ref, last_k, mu_k)
    v = shifted(pv_ref, last_v, mu_v)
    lora = shifted(pl_ref, last_l, mu_l)
    wl = lora[:, :LANES]
    gl = lora[:, LANES:]

    lane = lax.broadcasted_iota(jnp.int32, wl.shape, 1)
    wl_t = jnp.where(lane < RW_W_RANK, jnp.tanh(wl), 0.0)
    al = jnp.where(lane >= RW_W_RANK, wl, 0.0)
    w_log = -_softplus(-(w0[...] + _dot_x3(wl_t, w2[...]))) - 0.5
    lw_o[...] = -jnp.exp(w_log)
    a = jax.nn.sigmoid(a0[...] + _dot_x3(al, a2[...]))
    g_o[...] = _dot(jax.nn.sigmoid(gl).astype(BF16), g2[...])

    segm, segm_t = seg[...], seg_t[...]

    def head_sum(x):
        return _dot_exact_rhs01(_dot_exact_rhs01(x, segm), segm_t)

    kk = k * k_k[...]
    kk = kk / jnp.maximum(jnp.sqrt(head_sum(kk * kk)), 1e-12)
    k = k * (1.0 + (a - 1.0) * k_a[...])
    if has_vres:
        vr = shifted(pvr_ref, last_vr, mu_vr)
        mix = jax.nn.sigmoid(v0[...] + _dot_x3(vr, v2[...]))
        v = v + (vfirst_ref[...] - v) * mix
    r_o[...] = r
    k_o[...] = k
    v_o[...] = v
    kap_o[...] = kk
    beta_o[...] = kk * a
    bonus_o[...] = head_sum(r * k * r_k[...]) * v


def rwkv_prep(proj, batch, vfirst, prm, *, tt=256):
    tokens = proj.shape[0]
    seq = tokens // batch
    tt = min(tt, seq)
    nt = seq // tt
    has_vres = vfirst is not None
    d = RW_DIM

    def rows(width, col):
        return pl.BlockSpec((tt, width), lambda b, t, col=col: (b * nt + t, col))

    def full(shape):
        return pl.BlockSpec(shape, lambda b, t: (0, 0))

    in_specs = [rows(d, OFF_R // d), rows(d, OFF_K // d), rows(d, OFF_V // d),
                rows(RW_LORA, OFF_LORA // RW_LORA)]
    args = [proj, proj, proj, proj]
    if has_vres:
        in_specs += [rows(LANES, OFF_VRES // LANES), rows(d, 0)]
        args += [proj, vfirst]
    names = ["mu_r", "mu_k", "mu_v", "mu_l"] + (["mu_vr"] if has_vres else []) + ["w0", "w2", "a0", "a2", "g2"]
    names += (["v0", "v2"] if has_vres else []) + ["k_k", "k_a", "r_k", "seg", "seg_t"]
    for nm in names:
        in_specs.append(full(prm[nm].shape))
        args.append(prm[nm])
    out_spec = pl.BlockSpec((tt, d), lambda b, t: (b * nt + t, 0))
    scratch = [pltpu.VMEM((8, d), F32)] * 3 + [pltpu.VMEM((8, RW_LORA), F32)]
    if has_vres:
        scratch.append(pltpu.VMEM((8, LANES), F32))
    return pl.pallas_call(
        functools.partial(_rwkv_prep_kernel, has_vres=has_vres),
        grid=(batch, nt),
        in_specs=in_specs,
        out_specs=[out_spec] * 8,
        out_shape=[jax.ShapeDtypeStruct((tokens, d), F32)] * 8,
        scratch_shapes=scratch,
        compiler_params=_cparams(("arbitrary", "arbitrary")),
        name="rwkv_prep",
    )(*args)


def _rwkv_scan_kernel(r_ref, lw_ref, k_ref, v_ref, kap_ref, beta_ref, tril_ref, bmask_ref,
                      o_ref, ht_ref):
    @pl.when(pl.program_id(1) == 0)
    def _():
        ht_ref[...] = jnp.zeros_like(ht_ref)

    c = lw_ref.shape[0]
    n_groups = ht_ref.shape[0]
    w = ht_ref.shape[1]
    g = w // RW_HEAD_DIM
    rr = g * c
    bmask = bmask_ref[...]
    tril = tril_ref[...]
    row = lax.broadcasted_iota(jnp.int32, (rr, rr), 0)
    col = lax.broadcasted_iota(jnp.int32, (rr, rr), 1)
    strict = row > col
    incl = row >= col
    n_sq = int(math.log2(c))

    def stack_f32(x):
        return jnp.concatenate([x] * g, axis=0) * bmask

    groups = range(n_groups)
    sls = [slice(gi * w, (gi + 1) * w) for gi in groups]
    lw = [lw_ref[:, sl] for sl in sls]
    cum = [_dot_exact_lhs01(tril, x) for x in lw]
    total = [x[c - 1:c, :] for x in cum]
    a_s = [stack_f32(-kap_ref[:, sls[gi]] * jnp.exp(cum[gi] - lw[gi])).astype(BF16) for gi in groups]
    r_s = [stack_f32(r_ref[:, sls[gi]] * jnp.exp(cum[gi])).astype(BF16) for gi in groups]
    p_inv = [jnp.exp(-x) for x in cum]
    b_s = [stack_f32(beta_ref[:, sls[gi]] * p_inv[gi]).astype(BF16) for gi in groups]
    k_s = [stack_f32(k_ref[:, sls[gi]] * p_inv[gi]).astype(BF16) for gi in groups]
    v_f = [stack_f32(v_ref[:, sl]) for sl in sls]
    v_s = [x.astype(BF16) for x in v_f]

    ab = [jnp.where(strict, _dot_t(a_s[gi], b_s[gi]), 0.0) for gi in groups]
    ak = [jnp.where(strict, _dot_t(a_s[gi], k_s[gi]), 0.0).astype(BF16) for gi in groups]
    rb = [jnp.where(incl, _dot_t(r_s[gi], b_s[gi]), 0.0).astype(BF16) for gi in groups]
    rk = [jnp.where(incl, _dot_t(r_s[gi], k_s[gi]), 0.0).astype(BF16) for gi in groups]

    ht = [ht_ref[gi] for gi in groups]
    htb = [x.astype(BF16) for x in ht]
    x = [_dot_t(a_s[gi], htb[gi]) + _dot(ak[gi], v_s[gi]) for gi in groups]
    lp = ab
    for i in range(n_sq):
        lpb = [m.astype(BF16) for m in lp]
        x = [x[gi] + _dot(lpb[gi], x[gi].astype(BF16)) for gi in groups]
        if i < n_sq - 1:
            lp = [_dot(m, m) for m in lpb]
    u = [m.astype(BF16) for m in x]
    ow = [_dot_t(r_s[gi], htb[gi]) + _dot(rb[gi], u[gi]) + _dot(rk[gi], v_s[gi]) for gi in groups]
    for gi in groups:
        o = ow[gi][0:c]
        for i in range(1, g):
            o = o + ow[gi][i * c:(i + 1) * c]
        o_ref[:, sls[gi]] = o

    for gi in groups:
        p_rem = jnp.exp(total[gi] - cum[gi])
        z = jnp.concatenate([stack_f32(beta_ref[:, sls[gi]] * p_rem), stack_f32(k_ref[:, sls[gi]] * p_rem)],
                            axis=0).astype(BF16)
        uv_t = jnp.concatenate([x[gi].T, v_f[gi].T], axis=1).astype(BF16)
        ht_ref[gi] = ht[gi] * jnp.exp(total[gi]) + _dot(uv_t, z)


def _dot_exact_lhs01(ones_bf16, x):
    h, m, l = _split3(x)
    return _dot(ones_bf16, h) + _dot(ones_bf16, m) + _dot(ones_bf16, l)


def rwkv_scan(r, lw, k, v, kap, beta, batch):
    tokens, d = r.shape
    seq = tokens // batch
    c = min(SCAN_CHUNK, seq)
    nc = seq // c
    gw = SCAN_GROUP * RW_HEAD_DIM
    rr = SCAN_GROUP * c
    tril = (jnp.arange(c)[:, None] >= jnp.arange(c)[None, :]).astype(BF16)
    bmask = (jnp.arange(rr)[:, None] // c == jnp.arange(gw)[None, :] // RW_HEAD_DIM).astype(F32)
    blk = pl.BlockSpec((c, d), lambda b, i: (b * nc + i, 0))
    return pl.pallas_call(
        _rwkv_scan_kernel,
        grid=(batch, nc),
        in_specs=[blk] * 6 + [pl.BlockSpec((c, c), lambda b, i: (0, 0)),
                              pl.BlockSpec((rr, gw), lambda b, i: (0, 0))],
        out_specs=blk,
        out_shape=jax.ShapeDtypeStruct((tokens, d), F32),
        scratch_shapes=[pltpu.VMEM((d // gw, gw, gw), F32)],
        compiler_params=_cparams(("arbitrary", "arbitrary")),
        name="rwkv_scan",
    )(r, lw, k, v, kap, beta, tril, bmask)


def _rwkv_post_kernel(o_ref, bonus_ref, g_ref, lnw_ref, lnb_ref, seg, seg_t, y_ref):
    segm, segm_t = seg[...], seg_t[...]

    def head_mean(x):
        return _dot_exact_rhs01(_dot_exact_rhs01(x, segm), segm_t) * (1.0 / RW_HEAD_DIM)

    o = o_ref[...]
    dlt = o - head_mean(o)
    var = head_mean(dlt * dlt)
    y = dlt * lax.rsqrt(var + RW_LN_EPS) * lnw_ref[...] + lnb_ref[...]
    y_ref[...] = ((y + bonus_ref[...]) * g_ref[...]).astype(y_ref.dtype)


def rwkv_post(o, bonus, g, ln_w, ln_b, seg, seg_t, *, tm=512):
    tokens, d = o.shape
    tm = min(tm, tokens)
    blk = pl.BlockSpec((tm, d), lambda i: (i, 0))
    vec = pl.BlockSpec((1, d), lambda i: (0, 0))
    return pl.pallas_call(
        _rwkv_post_kernel,
        grid=(tokens // tm,),
        in_specs=[blk, blk, blk, vec, vec,
                  pl.BlockSpec(seg.shape, lambda i: (0, 0)), pl.BlockSpec(seg_t.shape, lambda i: (0, 0))],
        out_specs=blk,
        out_shape=jax.ShapeDtypeStruct((tokens, d), BF16),
        compiler_params=_cparams(("parallel",)),
        name="rwkv_post",
    )(o, bonus, g, ln_w.reshape(1, d), ln_b.reshape(1, d), seg, seg_t)


def _t5_thresholds():
    max_exact = REL_BUCKETS // 2
    thr = list(range(1, max_exact))
    n = max_exact
    for bucket in range(max_exact, REL_BUCKETS):
        while True:
            large = max_exact + int(math.log(max(n, max_exact) / max_exact)
                                    / math.log(REL_MAX_DIST / max_exact) * (REL_BUCKETS - max_exact))
            if min(large, REL_BUCKETS - 1) >= bucket:
                break
            n += 1
        thr.append(n)
    return thr


T5_THRESHOLDS = _t5_thresholds()
T5_FAR = T5_THRESHOLDS[-1]


def _softmax_tile(s_t, c, state, vt_tile):
    m_old, l_old, acc = state
    m_new = jnp.maximum(m_old, jnp.max(s_t, axis=0, keepdims=True) + c)
    alpha = jnp.exp2(m_old - m_new)
    p_t = jnp.exp2(s_t - (m_new - c))
    l_new = alpha * l_old + jnp.sum(p_t, axis=0, keepdims=True)
    acc = alpha * acc + _dot(vt_tile, p_t.astype(BF16))
    return m_new, l_new, acc


def _transpose_into(vt_ref, v_ref, chunk):
    seq = v_ref.shape[0]
    for c in range(seq // chunk):
        vt_ref[:, c * chunk:(c + 1) * chunk] = v_ref[c * chunk:(c + 1) * chunk, :].astype(F32).T.astype(BF16)


def _diff_attn_kernel(qfirst_ref, klast_ref, q_ref, k_ref, v_ref, qpos_ref, kpos_ref, subln_ref, table_ref, lam_ref,
                      o_ref, vt_ref, *, tq, tk, scale2, out_scale):
    b, i = pl.program_id(0), pl.program_id(1)
    nq = pl.num_programs(1)
    seq = k_ref.shape[0]
    nk = seq // tk
    w = DF_V_DIM

    @pl.when(i == 0)
    def _():
        _transpose_into(vt_ref, v_ref, tk)

    n_tiles = (i * tq + tq - 1) // tk + 1
    qf = qfirst_ref[b * nq + i]
    n_far = lax.while_loop(
        lambda j: (j * tk + tk - 1 <= i * tq) & (qf - klast_ref[b * nk + jnp.minimum(j, nk - 1)] >= T5_FAR),
        lambda j: j + 1, jnp.int32(0))

    dist = lax.broadcasted_iota(jnp.int32, (1, LANES), 1)
    qpos = qpos_ref[...]
    q_idx = i * tq + lax.broadcasted_iota(jnp.int32, (tk, tq), 1)
    k_off = lax.broadcasted_iota(jnp.int32, (tk, tq), 0)
    lane = lax.broadcasted_iota(jnp.int32, (tq, w), 1)

    bias_rows, c_far, qm = [], [], []
    for h in range(DF_HEADS):
        bias_vec = jnp.full((1, LANES), table_ref[h], F32)
        for bucket, thr in enumerate(T5_THRESHOLDS, start=1):
            bias_vec = jnp.where(dist >= thr, table_ref[bucket * DF_HEADS + h], bias_vec)
        bias_rows.append(jnp.broadcast_to(bias_vec * LOG2E, (tk, LANES)))
        c_far.append(table_ref[(REL_BUCKETS - 1) * DF_HEADS + h] * LOG2E)
        qh = q_ref[:, h * w:(h + 1) * w].astype(F32) * scale2
        qm.append([jnp.where((lane >= mi * DF_HEAD_DIM) & (lane < (mi + 1) * DF_HEAD_DIM), qh, 0.0).astype(BF16)
                   for mi in range(2)])

    def tiles(j, h):
        off = pl.multiple_of(j * tk, tk)
        return k_ref[pl.ds(off, tk), h * w:(h + 1) * w], vt_ref[h * w:(h + 1) * w, pl.ds(off, tk)], off

    def far_body(j, st):
        out = []
        for h in range(DF_HEADS):
            k_t, vt_t, _ = tiles(j, h)
            out += [_softmax_tile(_dot_t(k_t, qm[h][mi]), c_far[h], st[2 * h + mi], vt_t) for mi in range(2)]
        return tuple(out)

    def near_body(j, st):
        off = pl.multiple_of(j * tk, tk)
        n = jnp.clip(qpos - kpos_ref[pl.ds(off, tk), :], 0, LANES - 1)
        keep = q_idx >= off + k_off
        out = []
        for h in range(DF_HEADS):
            k_t, vt_t, _ = tiles(j, h)
            bias = jnp.concatenate(
                [jnp.take_along_axis(bias_rows[h], n[:, cb * LANES:(cb + 1) * LANES], axis=1)
                 for cb in range(tq // LANES)], axis=1)
            out += [_softmax_tile(jnp.where(keep, _dot_t(k_t, qm[h][mi]) + bias, NEG_BIG), 0.0,
                                  st[2 * h + mi], vt_t) for mi in range(2)]
        return tuple(out)

    init = tuple((jnp.full((1, tq), NEG_BIG, F32), jnp.zeros((1, tq), F32), jnp.zeros((w, tq), F32))
                 for _ in range(2 * DF_HEADS))
    st = lax.fori_loop(0, n_far, far_body, init)
    st = lax.fori_loop(n_far, n_tiles, near_body, st)
    for h in range(DF_HEADS):
        s0, s1 = st[2 * h], st[2 * h + 1]
        d_t = s0[2] / s0[1] - lam_ref[0] * (s1[2] / s1[1])
        ms = jnp.mean(d_t * d_t, axis=0, keepdims=True)
        y_t = d_t * lax.rsqrt(ms + DF_SUBLN_EPS) * (subln_ref[...] * out_scale)
        o_ref[:, h * w:(h + 1) * w] = y_t.T.astype(o_ref.dtype)


def diff_attention(pa, positions, rel_bias, lam, lambda_init, subln_w, *, tq=ATTN_TILE):
    batch, seq = positions.shape
    tokens = batch * seq
    tq = min(tq, seq)
    tk = tq
    nq, nk = seq // tq, seq // tk
    qfirst = positions[:, ::tq].reshape(-1)
    klast = positions[:, tk - 1::tk].reshape(-1)
    qpos = positions.reshape(batch, 1, seq)
    kpos = positions.reshape(batch, seq, 1)
    wd = DF_DIM
    grid_spec = pltpu.PrefetchScalarGridSpec(
        num_scalar_prefetch=2,
        grid=(batch, nq),
        in_specs=[pl.BlockSpec((tq, wd), lambda b, i, *_: (b * nq + i, OFF_DQ // wd)),
                  pl.BlockSpec((seq, wd), lambda b, i, *_: (b, OFF_DK // wd)),
                  pl.BlockSpec((seq, wd), lambda b, i, *_: (b, OFF_DV // wd)),
                  pl.BlockSpec((None, 1, tq), lambda b, i, *_: (b, 0, i)),
                  pl.BlockSpec((None, seq, 1), lambda b, i, *_: (b, 0, 0)),
                  pl.BlockSpec((DF_V_DIM, 1), lambda b, i, *_: (0, 0)),
                  pl.BlockSpec(memory_space=pltpu.SMEM),
                  pl.BlockSpec(memory_space=pltpu.SMEM)],
        out_specs=pl.BlockSpec((tq, wd), lambda b, i, *_: (b * nq + i, 0)),
        scratch_shapes=[pltpu.VMEM((wd, seq), BF16)],
    )
    return pl.pallas_call(
        functools.partial(_diff_attn_kernel, tq=tq, tk=tk, scale2=DF_HEAD_DIM ** -0.5 * LOG2E,
                          out_scale=1.0 - lambda_init),
        grid_spec=grid_spec,
        out_shape=jax.ShapeDtypeStruct((tokens, DF_DIM), BF16),
        compiler_params=_cparams(("arbitrary", "arbitrary")),
        name="diff_attention",
    )(qfirst, klast, pa, pa, pa, qpos, kpos, subln_w.reshape(DF_V_DIM, 1), rel_bias.reshape(-1), lam.reshape(1))


ML_QK_PAD = 2 * LANES


def _rope_block(x, cos, sin):
    half = ML_ROPE // 2
    lane = lax.broadcasted_iota(jnp.int32, x.shape, 1)
    rot = jnp.where(lane < half, -pltpu.roll(x, LANES - half, axis=1),
                    jnp.where(lane < ML_ROPE, pltpu.roll(x, half, axis=1), 0.0))
    return x * cos + rot * sin


def _mla_prep_kernel(mq_ref, mkv_ref, kpe_ref, pos_ref, qn_w, kvn_w, wq_ref, wkv_ref, freq_ref,
                     qf_o, kf_o, v_o, *, qscale):
    ang = pos_ref[...].astype(F32) * freq_ref[...]
    cos, sin = jnp.cos(ang), jnp.sin(ang)
    qc = _rms(mq_ref[...].astype(F32), qn_w[...], NORM_EPS).astype(BF16)
    q_all = _dot(qc, wq_ref[...]) * qscale
    kvc = _rms(mkv_ref[...].astype(F32), kvn_w[...], NORM_EPS).astype(BF16)
    kvb = _dot(kvc, wkv_ref[...])
    kpe = _rope_block(kpe_ref[...], cos, sin).astype(BF16)
    nope_w = ML_HEADS * ML_NOPE
    for h in range(ML_HEADS):
        lo = h * ML_QK_PAD
        qf_o[:, lo:lo + LANES] = q_all[:, h * LANES:(h + 1) * LANES].astype(BF16)
        qf_o[:, lo + LANES:lo + 2 * LANES] = _rope_block(
            q_all[:, nope_w + h * LANES:nope_w + (h + 1) * LANES], cos, sin).astype(BF16)
        kf_o[:, lo:lo + LANES] = kvb[:, h * LANES:(h + 1) * LANES].astype(BF16)
        kf_o[:, lo + LANES:lo + 2 * LANES] = kpe
    v_o[...] = kvb[:, nope_w:].astype(BF16)


def mla_prep(pa, pf, positions, q_norm, kv_norm, wq_all, wkv, *, tm=512):
    tokens = pa.shape[0]
    tm = min(tm, tokens)
    half = ML_ROPE // 2
    inv_freq = ROPE_THETA ** (-jnp.arange(half, dtype=F32) / half)
    freq = jnp.concatenate([inv_freq, inv_freq, jnp.zeros((LANES - ML_ROPE,), F32)]).reshape(1, LANES)

    def full(a):
        return pl.BlockSpec(a.shape, lambda i: (0, 0))

    qn_w = q_norm.reshape(1, -1)
    kvn_w = kv_norm.reshape(1, -1)
    wide = ML_HEADS * ML_QK_PAD
    return pl.pallas_call(
        functools.partial(_mla_prep_kernel, qscale=(ML_NOPE + ML_ROPE) ** -0.5 * LOG2E),
        grid=(tokens // tm,),
        in_specs=[pl.BlockSpec((tm, ML_Q_RANK), lambda i: (i, OFF_MQ // ML_Q_RANK)),
                  pl.BlockSpec((tm, ML_KV_RANK), lambda i: (i, OFF_MKV // ML_KV_RANK)),
                  pl.BlockSpec((tm, LANES), lambda i: (i, OFF_KPE // LANES)),
                  pl.BlockSpec((tm, 1), lambda i: (i, 0)),
                  full(qn_w), full(kvn_w), full(wq_all), full(wkv), full(freq)],
        out_specs=[pl.BlockSpec((tm, wide), lambda i: (i, 0)),
                   pl.BlockSpec((tm, wide), lambda i: (i, 0)),
                   pl.BlockSpec((tm, ML_DIM), lambda i: (i, 0))],
        out_shape=[jax.ShapeDtypeStruct((tokens, wide), BF16),
                   jax.ShapeDtypeStruct((tokens, wide), BF16),
                   jax.ShapeDtypeStruct((tokens, ML_DIM), BF16)],
        compiler_params=_cparams(("parallel",)),
        name="mla_prep",
    )(pa, pa, pf, positions.reshape(tokens, 1), qn_w, kvn_w, wq_all, wkv, freq)


def _mla_attn_kernel(q_ref, k_ref, v_ref, o_ref, vt_ref, *, tq, tk):
    i = pl.program_id(1)
    wq = ML_QK_PAD

    @pl.when(i == 0)
    def _():
        _transpose_into(vt_ref, v_ref, tk)

    n_tiles = (i * tq + tq - 1) // tk + 1
    n_full = (i * tq + 1) // tk
    q_idx = i * tq + lax.broadcasted_iota(jnp.int32, (tk, tq), 1)
    k_off = lax.broadcasted_iota(jnp.int32, (tk, tq), 0)
    qh = [q_ref[:, h * wq:(h + 1) * wq] for h in range(ML_HEADS)]

    def tiles(j, h):
        off = pl.multiple_of(j * tk, tk)
        return (k_ref[pl.ds(off, tk), h * wq:(h + 1) * wq],
                vt_ref[h * ML_V:(h + 1) * ML_V, pl.ds(off, tk)], off)

    def full_body(j, st):
        out = []
        for h in range(ML_HEADS):
            k_t, vt_t, _ = tiles(j, h)
            out.append(_softmax_tile(_dot_t(k_t, qh[h]), 0.0, st[h], vt_t))
        return tuple(out)

    def diag_body(j, st):
        out = []
        for h in range(ML_HEADS):
            k_t, vt_t, off = tiles(j, h)
            s_t = jnp.where(q_idx >= off + k_off, _dot_t(k_t, qh[h]), NEG_BIG)
            out.append(_softmax_tile(s_t, 0.0, st[h], vt_t))
        return tuple(out)

    st = tuple((jnp.full((1, tq), NEG_BIG, F32), jnp.zeros((1, tq), F32), jnp.zeros((ML_V, tq), F32))
               for _ in range(ML_HEADS))
    st = lax.fori_loop(0, n_full, full_body, st)
    st = lax.fori_loop(n_full, n_tiles, diag_body, st)
    for h in range(ML_HEADS):
        o_ref[:, h * ML_V:(h + 1) * ML_V] = (st[h][2] / st[h][1]).T.astype(o_ref.dtype)


def mla_attention(qf, kf, v, batch, *, tq=ATTN_TILE):
    tokens = qf.shape[0]
    seq = tokens // batch
    tq = min(tq, seq)
    tk = tq
    nq = seq // tq
    wide = qf.shape[1]
    return pl.pallas_call(
        functools.partial(_mla_attn_kernel, tq=tq, tk=tk),
        grid=(batch, nq),
        in_specs=[pl.BlockSpec((tq, wide), lambda b, i: (b * nq + i, 0)),
                  pl.BlockSpec((seq, wide), lambda b, i: (b, 0)),
                  pl.BlockSpec((seq, ML_DIM), lambda b, i: (b, 0))],
        out_specs=pl.BlockSpec((tq, ML_DIM), lambda b, i: (b * nq + i, 0)),
        out_shape=jax.ShapeDtypeStruct((tokens, ML_DIM), BF16),
        scratch_shapes=[pltpu.VMEM((ML_DIM, seq), BF16)],
        compiler_params=_cparams(("arbitrary", "arbitrary")),
        name="mla_attention",
    )(qf, kf, v)


def _cross_kernel(x_ref, nw_ref, wq_ref, kv_ref, wo_ref, o_ref):
    x = x_ref[...]
    q = _dot(_rms(x, nw_ref[...], NORM_EPS).astype(BF16), wq_ref[...])
    kv = kv_ref[...]
    scale = CA_HEAD_DIM ** -0.5
    outs = []
    for hh in range(CA_HEADS):
        sl = slice(hh * CA_HEAD_DIM, (hh + 1) * CA_HEAD_DIM)
        s = _dot_t(q[:, sl].astype(BF16), kv[:, sl]) * scale
        p = jnp.exp(s - jnp.max(s, axis=-1, keepdims=True))
        p = p / jnp.sum(p, axis=-1, keepdims=True)
        outs.append(_dot(p.astype(BF16), kv[:, CA_DIM + hh * CA_HEAD_DIM:CA_DIM + (hh + 1) * CA_HEAD_DIM]))
    o = jnp.concatenate(outs, axis=1).astype(BF16)
    o_ref[...] = x + _dot(o, wo_ref[...])


def cross_block(x, batch, norm_w, wq, kv, wo, *, tq=512):
    tokens, d = x.shape
    seq = tokens // batch
    tq = min(tq, seq)
    nq = seq // tq
    mem_len = kv.shape[0] // batch
    return pl.pallas_call(
        _cross_kernel,
        grid=(batch, nq),
        in_specs=[pl.BlockSpec((tq, d), lambda b, i: (b * nq + i, 0)),
                  pl.BlockSpec((1, d), lambda b, i: (0, 0)),
                  pl.BlockSpec(wq.shape, lambda b, i: (0, 0)),
                  pl.BlockSpec((mem_len, 2 * CA_DIM), lambda b, i: (b, 0)),
                  pl.BlockSpec(wo.shape, lambda b, i: (0, 0))],
        out_specs=pl.BlockSpec((tq, d), lambda b, i: (b * nq + i, 0)),
        out_shape=jax.ShapeDtypeStruct((tokens, d), F32),
        compiler_params=_cparams(("parallel", "parallel")),
        name="cross_block",
    )(x, norm_w.reshape(1, d), wq, kv, wo)


def _router_kernel(x_ref, nw_ref, wr_ref, br_ref, h_ref, sel_ref):
    h = _rms(x_ref[...], nw_ref[...], NORM_EPS)
    h_ref[...] = h.astype(h_ref.dtype)
    logits = _dot_x3(h, wr_ref[...])
    biased = logits + br_ref[...]
    lane = lax.broadcasted_iota(jnp.int32, logits.shape, 1)
    big = jnp.int32(LANES)

    def first_argmax(vals):
        mx = jnp.max(vals, axis=-1, keepdims=True)
        return jnp.min(jnp.where(vals == mx, lane, big), axis=-1, keepdims=True)

    def pick(vals, idx):
        return jnp.sum(jnp.where(lane == idx, vals, 0.0), axis=-1, keepdims=True)

    is_group = (lane >= MOE_EXPERTS) & (lane < MOE_EXPERTS + MOE_GROUPS)
    gl = jnp.where(is_group, logits, NEG_BIG)
    ge = jnp.exp(gl - jnp.max(gl, axis=-1, keepdims=True))
    gp = ge / jnp.sum(ge, axis=-1, keepdims=True)
    g_lane = first_argmax(jnp.where(is_group, biased, NEG_BIG))
    p_group = pick(gp, g_lane)
    lo = (g_lane - MOE_EXPERTS) * MOE_PER_GROUP
    in_group = (lane >= lo) & (lane < lo + MOE_PER_GROUP)
    eb = jnp.where(in_group, biased, NEG_BIG)
    i1 = first_argmax(eb)
    i2 = first_argmax(jnp.where(lane == i1, NEG_BIG, eb))
    l1, l2 = pick(logits, i1), pick(logits, i2)
    mx = jnp.maximum(l1, l2)
    e1, e2 = jnp.exp(l1 - mx), jnp.exp(l2 - mx)
    w1, w2 = e1 / (e1 + e2), e2 / (e1 + e2)
    sel_ref[...] = jnp.where(lane == SEL_E1, i1.astype(F32),
                             jnp.where(lane == SEL_E2, i2.astype(F32),
                                       jnp.where(lane == SEL_G1, w1 * p_group,
                                                 jnp.where(lane == SEL_G2, w2 * p_group, 0.0))))


SEL_E1, SEL_E2, SEL_G1, SEL_G2 = 0, 1, 2, 3


def moe_router(x, norm_w, w_router, b_router, *, tm=512):
    tokens, d = x.shape
    tm = min(tm, tokens)
    return pl.pallas_call(
        _router_kernel,
        grid=(tokens // tm,),
        in_specs=[pl.BlockSpec((tm, d), lambda i: (i, 0)),
                  pl.BlockSpec((1, d), lambda i: (0, 0)),
                  pl.BlockSpec((d, LANES), lambda i: (0, 0)),
                  pl.BlockSpec((1, LANES), lambda i: (0, 0))],
        out_specs=[pl.BlockSpec((tm, d), lambda i: (i, 0)),
                   pl.BlockSpec((tm, LANES), lambda i: (i, 0))],
        out_shape=[jax.ShapeDtypeStruct((tokens, d), F32),
                   jax.ShapeDtypeStruct((tokens, LANES), F32)],
        compiler_params=_cparams(("parallel",)),
        name="moe_router",
    )(x, norm_w.reshape(1, d), w_router, b_router)


def _moe_rank_kernel(sel_ref, ltri_ref, rank_ref, counts_ref, carry_ref):
    @pl.when(pl.program_id(0) == 0)
    def _():
        carry_ref[...] = jnp.zeros_like(carry_ref)

    sel = sel_ref[...]
    lane = lax.broadcasted_iota(jnp.int32, sel.shape, 1)
    lane_f = lane.astype(F32)
    oh1 = lane_f == sel[:, SEL_E1:SEL_E1 + 1]
    oh2 = lane_f == sel[:, SEL_E2:SEL_E2 + 1]
    f1, f2 = oh1.astype(F32), oh2.astype(F32)
    ltri = ltri_ref[...]
    before1 = _dot(ltri, f1.astype(BF16))
    before2 = _dot(ltri, f2.astype(BF16))
    c1 = jnp.sum(f1, axis=0, keepdims=True)
    c2 = jnp.sum(f2, axis=0, keepdims=True)
    carry = carry_ref[...]
    r1 = jnp.sum(jnp.where(oh1, before1 + carry, 0.0), axis=1, keepdims=True)
    r2 = jnp.sum(jnp.where(oh2, before2 + carry + c1, 0.0), axis=1, keepdims=True)
    rank_ref[...] = jnp.where(lane == SEL_E1, r1, jnp.where(lane == SEL_E2, r2, 0.0)).astype(jnp.int32)
    total = carry + c1 + c2
    carry_ref[...] = total
    counts_ref[...] = total.astype(jnp.int32)


def moe_rank(sel, *, tm=512):
    tokens = sel.shape[0]
    tm = min(tm, tokens)
    ltri = (jnp.arange(tm)[:, None] > jnp.arange(tm)[None, :]).astype(BF16)
    return pl.pallas_call(
        _moe_rank_kernel,
        grid=(tokens // tm,),
        in_specs=[pl.BlockSpec((tm, LANES), lambda i: (i, 0)),
                  pl.BlockSpec((tm, tm), lambda i: (0, 0))],
        out_specs=[pl.BlockSpec((tm, LANES), lambda i: (i, 0)),
                   pl.BlockSpec((1, LANES), lambda i: (0, 0))],
        out_shape=[jax.ShapeDtypeStruct((tokens, LANES), jnp.int32),
                   jax.ShapeDtypeStruct((1, LANES), jnp.int32)],
        scratch_shapes=[pltpu.VMEM((1, LANES), F32)],
        compiler_params=_cparams(("arbitrary",)),
        name="moe_rank",
    )(sel, ltri)


def _row_copy(src_ref, src_row, dst_ref, dst_row, sem):
    return pltpu.make_async_copy(src_ref.at[pl.ds(src_row, 1)], dst_ref.at[pl.ds(dst_row, 1)], sem)


def _moe_dispatch_kernel(dest_ref, h_ref, init_ref, xs_ref, sem, *, tm):
    del init_ref
    base = pl.program_id(0) * (2 * tm)

    def issue(r, carry):
        for s in range(2):
            _row_copy(h_ref, r, xs_ref, dest_ref[base + 2 * r + s], sem).start()
        return carry

    def drain(r, carry):
        for s in range(2):
            _row_copy(h_ref, 0, xs_ref, 0, sem).wait()
        return carry

    lax.fori_loop(0, tm, issue, 0)
    lax.fori_loop(0, tm, drain, 0)


def moe_dispatch(h, dest, rows, *, tm=256):
    tokens, d = h.shape
    tm = min(tm, tokens)
    grid_spec = pltpu.PrefetchScalarGridSpec(
        num_scalar_prefetch=1,
        grid=(tokens // tm,),
        in_specs=[pl.BlockSpec((tm, d), lambda i, *_: (i, 0)),
                  pl.BlockSpec(memory_space=pl.ANY)],
        out_specs=pl.BlockSpec(memory_space=pl.ANY),
        scratch_shapes=[pltpu.SemaphoreType.DMA(())],
    )
    return pl.pallas_call(
        functools.partial(_moe_dispatch_kernel, tm=tm),
        grid_spec=grid_spec,
        out_shape=jax.ShapeDtypeStruct((rows, d), h.dtype),
        input_output_aliases={2: 0},
        compiler_params=_cparams(("arbitrary",)),
        name="moe_dispatch",
    )(dest, h, jnp.zeros((rows, d), h.dtype))


def _moe_expert_kernel(te_ref, nused_ref, xs_ref, wg_ref, wu_ref, wd_ref, ys_ref, wgb, wub, wdb):
    r = pl.program_id(0)
    used = r < nused_ref[0]
    changed = (r == 0) | (te_ref[r] != te_ref[jnp.maximum(r - 1, 0)])

    @pl.when(used & changed)
    def _():
        wgb[...] = wg_ref[...].astype(BF16)
        wub[...] = wu_ref[...].astype(BF16)
        wdb[...] = wd_ref[...].astype(BF16)

    @pl.when(used)
    def _():
        x = xs_ref[...].astype(BF16)
        gate_pre = _dot(x, wgb[...])
        hid = (gate_pre * jax.nn.sigmoid(gate_pre)) * _dot(x, wub[...])
        ys_ref[...] = _dot(hid.astype(BF16), wdb[...])

    @pl.when(jnp.logical_not(used))
    def _():
        ys_ref[...] = jnp.zeros_like(ys_ref)


def moe_experts(xs, tile_expert, n_used, wg, wu, wd, layer, *, tm):
    rows, d = xs.shape
    de = wg.shape[-1]
    grid_spec = pltpu.PrefetchScalarGridSpec(
        num_scalar_prefetch=2,
        grid=(rows // tm,),
        in_specs=[pl.BlockSpec((tm, d), lambda r, te, nu: (r, 0)),
                  pl.BlockSpec((None, None, d, de), lambda r, te, nu: (layer, te[r], 0, 0)),
                  pl.BlockSpec((None, None, d, de), lambda r, te, nu: (layer, te[r], 0, 0)),
                  pl.BlockSpec((None, None, de, d), lambda r, te, nu: (layer, te[r], 0, 0))],
        out_specs=pl.BlockSpec((tm, d), lambda r, te, nu: (r, 0)),
        scratch_shapes=[pltpu.VMEM((d, de), BF16), pltpu.VMEM((d, de), BF16), pltpu.VMEM((de, d), BF16)],
    )
    return pl.pallas_call(
        _moe_expert_kernel,
        grid_spec=grid_spec,
        out_shape=jax.ShapeDtypeStruct((rows, d), F32),
        compiler_params=_cparams(("arbitrary",)),
        name="moe_experts",
    )(tile_expert, n_used, xs, wg, wu, wd)


def _moe_combine_kernel(dest_ref, x_ref, sel_ref, ys_ref, o_ref, buf_ref, sem, *, tm):
    base = pl.program_id(0) * (2 * tm)

    def issue(r, carry):
        for s in range(2):
            _row_copy(ys_ref, dest_ref[base + 2 * r + s], buf_ref.at[s], r, sem).start()
        return carry

    def drain(r, carry):
        for s in range(2):
            _row_copy(ys_ref, 0, buf_ref.at[s], 0, sem).wait()
        return carry

    lax.fori_loop(0, tm, issue, 0)
    lax.fori_loop(0, tm, drain, 0)
    sel = sel_ref[...]
    o_ref[...] = (x_ref[...] + sel[:, SEL_G1:SEL_G1 + 1] * buf_ref[0] + sel[:, SEL_G2:SEL_G2 + 1] * buf_ref[1])


def moe_combine(x, sel, ys, dest, *, tm=256):
    tokens, d = x.shape
    tm = min(tm, tokens)
    grid_spec = pltpu.PrefetchScalarGridSpec(
        num_scalar_prefetch=1,
        grid=(tokens // tm,),
        in_specs=[pl.BlockSpec((tm, d), lambda i, *_: (i, 0)),
                  pl.BlockSpec((tm, LANES), lambda i, *_: (i, 0)),
                  pl.BlockSpec(memory_space=pl.ANY)],
        out_specs=pl.BlockSpec((tm, d), lambda i, *_: (i, 0)),
        scratch_shapes=[pltpu.VMEM((2, tm, d), F32), pltpu.SemaphoreType.DMA(())],
    )
    return pl.pallas_call(
        functools.partial(_moe_combine_kernel, tm=tm),
        grid_spec=grid_spec,
        out_shape=jax.ShapeDtypeStruct((tokens, d), F32),
        compiler_params=_cparams(("arbitrary",)),
        name="moe_combine",
    )(dest, x, sel, ys)


def moe_block(x, norm_w, w_router, b_router, wg, wu, wd, layer, *, tm=MOE_ROW_TILE):
    tokens, d = x.shape
    n_exp = wg.shape[1]
    h, sel = moe_router(x, norm_w, w_router, b_router)
    rank, counts = moe_rank(sel)
    padded = (counts[0, :n_exp] + (tm - 1)) // tm * tm
    ends = jnp.cumsum(padded)
    starts = ends - padded
    experts = sel[:, SEL_E1:SEL_E2 + 1].astype(jnp.int32)
    dest = (starts[experts] + rank[:, SEL_E1:SEL_E2 + 1]).reshape(-1)
    rows = 2 * tokens + n_exp * tm
    tile_start = jnp.arange(rows // tm, dtype=jnp.int32) * tm
    tile_expert = jnp.minimum(jnp.sum(tile_start[:, None] >= ends[None, :], axis=1), n_exp - 1).astype(jnp.int32)
    n_used = (ends[-1] // tm).astype(jnp.int32).reshape(1)
    xs = moe_dispatch(h, dest, rows)
    ys = moe_experts(xs, tile_expert, n_used, wg, wu, wd, layer, tm=tm)
    return moe_combine(x, sel, ys, dest)


def _proj_weights(w_in_l, w_vres_l):
    d = w_in_l.shape[0]
    mla0 = RW_COLS + DF_COLS
    vres = jnp.zeros((d, RW_V_RANK), F32) if w_vres_l is None else w_vres_l
    part_f = [w_in_l[:, :RW_COLS],
              vres, jnp.zeros((d, LANES - RW_V_RANK), F32),
              w_in_l[:, mla0 + ML_Q_RANK + ML_KV_RANK:mla0 + ML_COLS], jnp.zeros((d, LANES - ML_ROPE), F32)]
    part_a = [w_in_l[:, RW_COLS:RW_COLS + DF_COLS],
              w_in_l[:, mla0 + ML_Q_RANK:mla0 + ML_Q_RANK + ML_KV_RANK],
              jnp.zeros((d, OFF_MQ - OFF_MKV - ML_KV_RANK), F32),
              w_in_l[:, mla0:mla0 + ML_Q_RANK]]
    return jnp.concatenate(part_f, axis=1).astype(BF16), jnp.concatenate(part_a, axis=1).astype(BF16)


def _pad_rows(w, rows, at=0):
    out = jnp.zeros((rows, w.shape[1]), w.dtype)
    return lax.dynamic_update_slice(out, w, (at, 0))


def kernel(x, mem, positions, rel_bias, final_norm, norm_mix, w_in, w_in_vres, w_out, tm_mu, tm_mu_vres, tm_w0, tm_w2, tm_a0, tm_a2, tm_v0, tm_v2, tm_g2, tm_k_k, tm_k_a, tm_r_k, tm_ln_w, tm_ln_b, da_lq1, da_lk1, da_lq2, da_lk2, da_subln, mla_q_norm, mla_wq_b, mla_kv_norm, mla_wkv_b, norm_cross, norm_mem, ca_wq, ca_wkv, ca_wo, norm_ffn, moe_w_group, moe_b_group, moe_w_expert, moe_b_expert, moe_w_gate, moe_w_up, moe_w_down):
    batch, seq, d = x.shape
    tokens = batch * seq
    depth = norm_mix.shape[0]
    xf = x.reshape(tokens, d)
    memf = mem.reshape(-1, d)
    positions = positions.astype(jnp.int32)

    head_of_lane = jnp.arange(RW_DIM) // RW_HEAD_DIM
    seg = (head_of_lane[:, None] == jnp.arange(LANES)[None, :]).astype(BF16)
    seg_t = seg.T
    row = lambda v: v.reshape(1, -1)

    v_first = None
    for l in range(depth):
        w_f, w_a = _proj_weights(w_in[l], None if l == 0 else w_in_vres[l - 1])
        proj = norm_matmul(xf, norm_mix[l], w_f, tn=PROJ_F_COLS // 2)
        pa = norm_matmul(xf, norm_mix[l], w_a, out_dtype=BF16, tn=PROJ_A_COLS // 2)

        mu = tm_mu[l]
        prm = dict(mu_r=row(mu[:RW_DIM]), mu_k=row(mu[RW_DIM:2 * RW_DIM]), mu_v=row(mu[2 * RW_DIM:3 * RW_DIM]),
                   mu_l=row(mu[3 * RW_DIM:]), w0=row(tm_w0[l]), a0=row(tm_a0[l]),
                   w2=_pad_rows(tm_w2[l], LANES, 0), a2=_pad_rows(tm_a2[l], LANES, RW_W_RANK),
                   g2=tm_g2[l].astype(BF16), k_k=row(tm_k_k[l]), k_a=row(tm_k_a[l]), r_k=row(tm_r_k[l]),
                   seg=seg, seg_t=seg_t)
        if l > 0:
            prm.update(mu_vr=jnp.pad(row(tm_mu_vres[l - 1]), ((0, 0), (0, LANES - RW_V_RANK))),
                       v0=row(tm_v0[l - 1]), v2=_pad_rows(tm_v2[l - 1], LANES, 0))
        r, lw, k, v, kap, beta, gate, bonus = rwkv_prep(proj, batch, v_first, prm)
        if l == 0:
            v_first = v
        o = rwkv_scan(r, lw, k, v, kap, beta, batch)
        y_a = rwkv_post(o, bonus, gate, tm_ln_w[l], tm_ln_b[l], seg, seg_t)

        lambda_init = 0.8 - 0.6 * math.exp(-0.3 * l)
        lam = (jnp.exp(jnp.sum(da_lq1[l] * da_lk1[l])) - jnp.exp(jnp.sum(da_lq2[l] * da_lk2[l])) + lambda_init)
        y_b = diff_attention(pa, positions, rel_bias, lam, lambda_init, da_subln[l])

        wq = mla_wq_b[l].reshape(ML_Q_RANK, ML_HEADS, ML_NOPE + ML_ROPE)
        wq_pe = jnp.pad(wq[:, :, ML_NOPE:], ((0, 0), (0, 0), (0, LANES - ML_ROPE)))
        wq_all = jnp.concatenate([wq[:, :, :ML_NOPE].reshape(ML_Q_RANK, -1),
                                  wq_pe.reshape(ML_Q_RANK, -1)], axis=1).astype(BF16)
        wkv = mla_wkv_b[l].reshape(ML_KV_RANK, ML_HEADS, ML_NOPE + ML_V)
        wkv = jnp.concatenate([wkv[:, :, :ML_NOPE].reshape(ML_KV_RANK, -1),
                               wkv[:, :, ML_NOPE:].reshape(ML_KV_RANK, -1)], axis=1).astype(BF16)
        qf, kf, v_mla = mla_prep(pa, proj, positions, mla_q_norm[l], mla_kv_norm[l], wq_all, wkv)
        y_c = mla_attention(qf, kf, v_mla, batch)

        wo = w_out[l].astype(BF16)
        xf = matmul_res([y_a, y_b, y_c],
                        [wo[:RW_DIM], wo[RW_DIM:RW_DIM + DF_DIM], wo[RW_DIM + DF_DIM:]], xf)

        kv_mem = norm_matmul(memf, norm_mem[l], ca_wkv[l].astype(BF16), out_dtype=BF16)
        xf = cross_block(xf, batch, norm_cross[l], ca_wq[l].astype(BF16), kv_mem, ca_wo[l].astype(BF16))

        w_router = jnp.concatenate(
            [moe_w_expert[l], moe_w_group[l], jnp.zeros((d, LANES - MOE_EXPERTS - MOE_GROUPS), F32)], axis=1)
        b_router = jnp.concatenate(
            [moe_b_expert[l], moe_b_group[l], jnp.zeros((LANES - MOE_EXPERTS - MOE_GROUPS,), F32)]).reshape(1, LANES)
        xf = moe_block(xf, norm_ffn[l], w_router, b_router, moe_w_gate, moe_w_up, moe_w_down, l)

    out = rmsnorm(xf, final_norm)
    return out.reshape(batch, seq, d)
```

```python
import functools
import math

import jax
import jax.numpy as jnp
from jax import lax
from jax.experimental import pallas as pl
from jax.experimental.pallas import tpu as pltpu

F32 = jnp.float32
BF16 = jnp.bfloat16

NORM_EPS = 1e-6
ROPE_THETA = 10000.0

RW_HEADS = 16
RW_HEAD_DIM = 64
RW_DIM = RW_HEADS * RW_HEAD_DIM
RW_W_RANK = 64
RW_A_RANK = 64
RW_G_RANK = 128
RW_V_RANK = 32
RW_LORA = RW_W_RANK + RW_A_RANK + RW_G_RANK
RW_LN_EPS = 64e-5
RW_COLS = 3 * RW_DIM + RW_LORA

DF_HEADS = 4
DF_HEAD_DIM = 64
DF_V_DIM = 2 * DF_HEAD_DIM
DF_QK = DF_HEADS * 2 * DF_HEAD_DIM
DF_DIM = DF_HEADS * DF_V_DIM
DF_COLS = 2 * DF_QK + DF_DIM
DF_SUBLN_EPS = 1e-5

ML_HEADS = 4
ML_Q_RANK = 384
ML_KV_RANK = 256
ML_NOPE = 128
ML_ROPE = 64
ML_V = 128
ML_DIM = ML_HEADS * ML_V
ML_COLS = ML_Q_RANK + ML_KV_RANK + ML_ROPE

REL_BUCKETS = 32
REL_MAX_DIST = 128

CA_HEADS = 4
CA_HEAD_DIM = 128
CA_DIM = CA_HEADS * CA_HEAD_DIM

MOE_GROUPS = 4
MOE_PER_GROUP = 8
MOE_EXPERTS = MOE_GROUPS * MOE_PER_GROUP

LANES = 128
SCAN_CHUNK = 64
SCAN_GROUP = 4
ATTN_TILE = 512
MOE_ROW_TILE = 512
VMEM_LIMIT = 56 * 1024 * 1024
NEG_BIG = -1e30

LOG2E = 1.4426950408889634

OFF_R = 0
OFF_K = RW_DIM
OFF_V = 2 * RW_DIM
OFF_LORA = 3 * RW_DIM
OFF_VRES = OFF_LORA + RW_LORA
OFF_KPE = OFF_VRES + LANES
PROJ_F_COLS = OFF_KPE + LANES
OFF_DQ = 0
OFF_DK = OFF_DQ + DF_QK
OFF_DV = OFF_DK + DF_QK
OFF_MKV = OFF_DV + DF_DIM
OFF_MQ = 5 * ML_Q_RANK
PROJ_A_COLS = OFF_MQ + ML_Q_RANK


def _cparams(sem, vmem=VMEM_LIMIT, flags=None):
    return pltpu.CompilerParams(dimension_semantics=sem, vmem_limit_bytes=vmem, flags=flags)


def _dot(a, b):
    return jnp.dot(a, b, preferred_element_type=F32)


def _dot_t(a, b):
    return lax.dot_general(a, b, (((1,), (1,)), ((), ())), preferred_element_type=F32)


def _split3(x):
    hi = x.astype(BF16)
    r1 = x - hi.astype(F32)
    mid = r1.astype(BF16)
    lo = (r1 - mid.astype(F32)).astype(BF16)
    return hi, mid, lo


def _dot_exact_rhs01(x, ones_bf16):
    h, m, l = _split3(x)
    return _dot(h, ones_bf16) + _dot(m, ones_bf16) + _dot(l, ones_bf16)


def _dot_x3(a, b):
    ah = a.astype(BF16)
    al = (a - ah.astype(F32)).astype(BF16)
    bh = b.astype(BF16)
    bl = (b - bh.astype(F32)).astype(BF16)
    return _dot(ah, bh) + _dot(ah, bl) + _dot(al, bh)


def _rms(x, w, eps):
    ms = jnp.mean(x * x, axis=-1, keepdims=True)
    return x * lax.rsqrt(ms + eps) * w


def _rmsnorm_kernel(x_ref, w_ref, o_ref, *, eps):
    o_ref[...] = _rms(x_ref[...], w_ref[...], eps).astype(o_ref.dtype)


def rmsnorm(x, w, *, eps=NORM_EPS, out_dtype=F32, tm=512):
    m, d = x.shape
    tm = min(tm, m)
    return pl.pallas_call(
        functools.partial(_rmsnorm_kernel, eps=eps),
        grid=(m // tm,),
        in_specs=[pl.BlockSpec((tm, d), lambda i: (i, 0)),
                  pl.BlockSpec((1, d), lambda i: (0, 0))],
        out_specs=pl.BlockSpec((tm, d), lambda i: (i, 0)),
        out_shape=jax.ShapeDtypeStruct((m, d), out_dtype),
        compiler_params=_cparams(("parallel",)),
        name="rmsnorm",
    )(x, w.reshape(1, d))


def _norm_matmul_kernel(x_ref, nw_ref, w_ref, o_ref, xn_ref, *, eps):
    @pl.when(pl.program_id(1) == 0)
    def _():
        xn_ref[...] = _rms(x_ref[...], nw_ref[...], eps).astype(BF16)

    o_ref[...] = _dot(xn_ref[...], w_ref[...]).astype(o_ref.dtype)


def norm_matmul(x, nw, w, *, out_dtype=F32, tm=512, tn=None, eps=NORM_EPS):
    m, d = x.shape
    n = w.shape[1]
    tm = min(tm, m)
    tn = n if tn is None else tn
    return pl.pallas_call(
        functools.partial(_norm_matmul_kernel, eps=eps),
        grid=(m // tm, n // tn),
        in_specs=[pl.BlockSpec((tm, d), lambda i, j: (i, 0)),
                  pl.BlockSpec((1, d), lambda i, j: (0, 0)),
                  pl.BlockSpec((d, tn), lambda i, j: (0, j))],
        out_specs=pl.BlockSpec((tm, tn), lambda i, j: (i, j)),
        out_shape=jax.ShapeDtypeStruct((m, n), out_dtype),
        scratch_shapes=[pltpu.VMEM((tm, d), BF16)],
        compiler_params=_cparams(("parallel", "arbitrary")),
        name="norm_matmul",
    )(x, nw.reshape(1, d), w)


def _matmul_res_kernel(*refs, n_a):
    a_refs, w_refs = refs[:n_a], refs[n_a:2 * n_a]
    res_ref, o_ref = refs[2 * n_a], refs[2 * n_a + 1]
    acc = res_ref[...]
    for a_ref, w_ref in zip(a_refs, w_refs):
        acc = acc + _dot(a_ref[...].astype(BF16), w_ref[...])
    o_ref[...] = acc


def matmul_res(a_list, w_list, res, *, tm=512, tn=1024):
    m, n = res.shape
    tm = min(tm, m)
    tn = min(tn, n)
    n_a = len(a_list)
    in_specs = ([pl.BlockSpec((tm, a.shape[1]), lambda i, j: (i, 0)) for a in a_list]
                + [pl.BlockSpec((w.shape[0], tn), lambda i, j: (0, j)) for w in w_list]
                + [pl.BlockSpec((tm, tn), lambda i, j: (i, j))])
    return pl.pallas_call(
        functools.partial(_matmul_res_kernel, n_a=n_a),
        grid=(m // tm, n // tn),
        in_specs=in_specs,
        out_specs=pl.BlockSpec((tm, tn), lambda i, j: (i, j)),
        out_shape=jax.ShapeDtypeStruct((m, n), F32),
        compiler_params=_cparams(("parallel", "arbitrary")),
        name="matmul_res",
    )(*a_list, *w_list, res)


def _softplus(z):
    return jnp.maximum(z, 0.0) + jnp.log(1.0 + jnp.exp(-jnp.abs(z)))


def _rwkv_prep_kernel(*refs, has_vres):
    if has_vres:
        (pr_ref, pk_ref, pv_ref, pl_ref, pvr_ref, vfirst_ref,
         mu_r, mu_k, mu_v, mu_l, mu_vr, w0, w2, a0, a2, g2, v0, v2,
         k_k, k_a, r_k, seg, seg_t,
         r_o, lw_o, k_o, v_o, kap_o, beta_o, g_o, bonus_o,
         last_r, last_k, last_v, last_l, last_vr) = refs
    else:
        (pr_ref, pk_ref, pv_ref, pl_ref,
         mu_r, mu_k, mu_v, mu_l, w0, w2, a0, a2, g2,
         k_k, k_a, r_k, seg, seg_t,
         r_o, lw_o, k_o, v_o, kap_o, beta_o, g_o, bonus_o,
         last_r, last_k, last_v, last_l) = refs
    t = pl.program_id(1)

    def shifted(p_ref, last_ref, mu_ref):
        p = p_ref[...]
        n = p.shape[0]
        carried = jnp.where(t == 0, 0.0, last_ref[0:1, :])
        row = lax.broadcasted_iota(jnp.int32, p.shape, 0)
        prev = jnp.where(row == 0, carried, pltpu.roll(p, 1, axis=0))
        last_ref[0:1, :] = p[n - 1:n, :]
        return p + mu_ref[...] * (prev - p)

    r = shifted(pr_ref, last_r, mu_r)
    k = shifted(pk_ref, last_k, mu_k)
    v = shifted(pv_ref, last_v, mu_v)
    lora = shifted(pl_ref, last_l, mu_l)
    wl = lora[:, :LANES]
    gl = lora[:, LANES:]

    lane = lax.broadcasted_iota(jnp.int32, wl.shape, 1)
    wl_t = jnp.where(lane < RW_W_RANK, jnp.tanh(wl), 0.0)
    al = jnp.where(lane >= RW_W_RANK, wl, 0.0)
    w_log = -_softplus(-(w0[...] + _dot_x3(wl_t, w2[...]))) - 0.5
    lw_o[...] = -jnp.exp(w_log)
    a = jax.nn.sigmoid(a0[...] + _dot_x3(al, a2[...]))
    g_o[...] = _dot(jax.nn.sigmoid(gl).astype(BF16), g2[...])

    segm, segm_t = seg[...], seg_t[...]

    def head_sum(x):
        return _dot_exact_rhs01(_dot_exact_rhs01(x, segm), segm_t)

    kk = k * k_k[...]
    kk = kk / jnp.maximum(jnp.sqrt(head_sum(kk * kk)), 1e-12)
    k = k * (1.0 + (a - 1.0) * k_a[...])
    if has_vres:
        vr = shifted(pvr_ref, last_vr, mu_vr)
        mix = jax.nn.sigmoid(v0[...] + _dot_x3(vr, v2[...]))
        v = v + (vfirst_ref[...] - v) * mix
    r_o[...] = r
    k_o[...] = k
    v_o[...] = v
    kap_o[...] = kk
    beta_o[...] = kk * a
    bonus_o[...] = head_sum(r * k * r_k[...]) * v


def rwkv_prep(proj, batch, vfirst, prm, *, tt=256):
    tokens = proj.shape[0]
    seq = tokens // batch
    tt = min(tt, seq)
    nt = seq // tt
    has_vres = vfirst is not None
    d = RW_DIM

    def rows(width, col):
        return pl.BlockSpec((tt, width), lambda b, t, col=col: (b * nt + t, col))

    def full(shape):
        return pl.BlockSpec(shape, lambda b, t: (0, 0))

    in_specs = [rows(d, OFF_R // d), rows(d, OFF_K // d), rows(d, OFF_V // d),
                rows(RW_LORA, OFF_LORA // RW_LORA)]
    args = [proj, proj, proj, proj]
    if has_vres:
        in_specs += [rows(LANES, OFF_VRES // LANES), rows(d, 0)]
        args += [proj, vfirst]
    names = ["mu_r", "mu_k", "mu_v", "mu_l"] + (["mu_vr"] if has_vres else []) + ["w0", "w2", "a0", "a2", "g2"]
    names += (["v0", "v2"] if has_vres else []) + ["k_k", "k_a", "r_k", "seg", "seg_t"]
    for nm in names:
        in_specs.append(full(prm[nm].shape))
        args.append(prm[nm])
    out_spec = pl.BlockSpec((tt, d), lambda b, t: (b * nt + t, 0))
    scratch = [pltpu.VMEM((8, d), F32)] * 3 + [pltpu.VMEM((8, RW_LORA), F32)]
    if has_vres:
        scratch.append(pltpu.VMEM((8, LANES), F32))
    return pl.pallas_call(
        functools.partial(_rwkv_prep_kernel, has_vres=has_vres),
        grid=(batch, nt),
        in_specs=in_specs,
        out_specs=[out_spec] * 8,
        out_shape=[jax.ShapeDtypeStruct((tokens, d), F32)] * 8,
        scratch_shapes=scratch,
        compiler_params=_cparams(("arbitrary", "arbitrary")),
        name="rwkv_prep",
    )(*args)


def _rwkv_scan_kernel(r_ref, lw_ref, k_ref, v_ref, kap_ref, beta_ref, tril_ref, bmask_ref,
                      o_ref, ht_ref):
    @pl.when(pl.program_id(1) == 0)
    def _():
        ht_ref[...] = jnp.zeros_like(ht_ref)

    c = lw_ref.shape[0]
    n_groups = ht_ref.shape[0]
    w = ht_ref.shape[1]
    g = w // RW_HEAD_DIM
    rr = g * c
    bmask = bmask_ref[...]
    tril = tril_ref[...]
    row = lax.broadcasted_iota(jnp.int32, (rr, rr), 0)
    col = lax.broadcasted_iota(jnp.int32, (rr, rr), 1)
    strict = row > col
    incl = row >= col
    n_sq = int(math.log2(c))

    def stack_f32(x):
        return jnp.concatenate([x] * g, axis=0) * bmask

    groups = range(n_groups)
    sls = [slice(gi * w, (gi + 1) * w) for gi in groups]
    lw = [lw_ref[:, sl] for sl in sls]
    cum = [_dot_exact_lhs01(tril, x) for x in lw]
    total = [x[c - 1:c, :] for x in cum]
    a_s = [stack_f32(-kap_ref[:, sls[gi]] * jnp.exp(cum[gi] - lw[gi])).astype(BF16) for gi in groups]
    r_s = [stack_f32(r_ref[:, sls[gi]] * jnp.exp(cum[gi])).astype(BF16) for gi in groups]
    p_inv = [jnp.exp(-x) for x in cum]
    b_s = [stack_f32(beta_ref[:, sls[gi]] * p_inv[gi]).astype(BF16) for gi in groups]
    k_s = [stack_f32(k_ref[:, sls[gi]] * p_inv[gi]).astype(BF16) for gi in groups]
    v_f = [stack_f32(v_ref[:, sl]) for sl in sls]
    v_s = [x.astype(BF16) for x in v_f]

    ab = [jnp.where(strict, _dot_t(a_s[gi], b_s[gi]), 0.0) for gi in groups]
    ak = [jnp.where(strict, _dot_t(a_s[gi], k_s[gi]), 0.0).astype(BF16) for gi in groups]
    rb = [jnp.where(incl, _dot_t(r_s[gi], b_s[gi]), 0.0).astype(BF16) for gi in groups]
    rk = [jnp.where(incl, _dot_t(r_s[gi], k_s[gi]), 0.0).astype(BF16) for gi in groups]

    ht = [ht_ref[gi] for gi in groups]
    htb = [x.astype(BF16) for x in ht]
    x = [_dot_t(a_s[gi], htb[gi]) + _dot(ak[gi], v_s[gi]) for gi in groups]
    lp = ab
    for i in range(n_sq):
        lpb = [m.astype(BF16) for m in lp]
        x = [x[gi] + _dot(lpb[gi], x[gi].astype(BF16)) for gi in groups]
        if i < n_sq - 1:
            lp = [_dot(m, m) for m in lpb]
    u = [m.astype(BF16) for m in x]
    ow = [_dot_t(r_s[gi], htb[gi]) + _dot(rb[gi], u[gi]) + _dot(rk[gi], v_s[gi]) for gi in groups]
    for gi in groups:
        o = ow[gi][0:c]
        for i in range(1, g):
            o = o + ow[gi][i * c:(i + 1) * c]
        o_ref[:, sls[gi]] = o

    for gi in groups:
        p_rem = jnp.exp(total[gi] - cum[gi])
        z = jnp.concatenate([stack_f32(beta_ref[:, sls[gi]] * p_rem), stack_f32(k_ref[:, sls[gi]] * p_rem)],
                            axis=0).astype(BF16)
        uv_t = jnp.concatenate([x[gi].T, v_f[gi].T], axis=1).astype(BF16)
        ht_ref[gi] = ht[gi] * jnp.exp(total[gi]) + _dot(uv_t, z)


def _dot_exact_lhs01(ones_bf16, x):
    h, m, l = _split3(x)
    return _dot(ones_bf16, h) + _dot(ones_bf16, m) + _dot(ones_bf16, l)


def rwkv_scan(r, lw, k, v, kap, beta, batch):
    tokens, d = r.shape
    seq = tokens // batch
    c = min(SCAN_CHUNK, seq)
    nc = seq // c
    gw = SCAN_GROUP * RW_HEAD_DIM
    rr = SCAN_GROUP * c
    tril = (jnp.arange(c)[:, None] >= jnp.arange(c)[None, :]).astype(BF16)
    bmask = (jnp.arange(rr)[:, None] // c == jnp.arange(gw)[None, :] // RW_HEAD_DIM).astype(F32)
    blk = pl.BlockSpec((c, d), lambda b, i: (b * nc + i, 0))
    return pl.pallas_call(
        _rwkv_scan_kernel,
        grid=(batch, nc),
        in_specs=[blk] * 6 + [pl.BlockSpec((c, c), lambda b, i: (0, 0)),
                              pl.BlockSpec((rr, gw), lambda b, i: (0, 0))],
        out_specs=blk,
        out_shape=jax.ShapeDtypeStruct((tokens, d), F32),
        scratch_shapes=[pltpu.VMEM((d // gw, gw, gw), F32)],
        compiler_params=_cparams(("arbitrary", "arbitrary")),
        name="rwkv_scan",
    )(r, lw, k, v, kap, beta, tril, bmask)


def _rwkv_post_kernel(o_ref, bonus_ref, g_ref, lnw_ref, lnb_ref, seg, seg_t, y_ref):
    segm, segm_t = seg[...], seg_t[...]

    def head_mean(x):
        return _dot_exact_rhs01(_dot_exact_rhs01(x, segm), segm_t) * (1.0 / RW_HEAD_DIM)

    o = o_ref[...]
    dlt = o - head_mean(o)
    var = head_mean(dlt * dlt)
    y = dlt * lax.rsqrt(var + RW_LN_EPS) * lnw_ref[...] + lnb_ref[...]
    y_ref[...] = ((y + bonus_ref[...]) * g_ref[...]).astype(y_ref.dtype)


def rwkv_post(o, bonus, g, ln_w, ln_b, seg, seg_t, *, tm=512):
    tokens, d = o.shape
    tm = min(tm, tokens)
    blk = pl.BlockSpec((tm, d), lambda i: (i, 0))
    vec = pl.BlockSpec((1, d), lambda i: (0, 0))
    return pl.pallas_call(
        _rwkv_post_kernel,
        grid=(tokens // tm,),
        in_specs=[blk, blk, blk, vec, vec,
                  pl.BlockSpec(seg.shape, lambda i: (0, 0)), pl.BlockSpec(seg_t.shape, lambda i: (0, 0))],
        out_specs=blk,
        out_shape=jax.ShapeDtypeStruct((tokens, d), BF16),
        compiler_params=_cparams(("parallel",)),
        name="rwkv_post",
    )(o, bonus, g, ln_w.reshape(1, d), ln_b.reshape(1, d), seg, seg_t)


def _t5_thresholds():
    max_exact = REL_BUCKETS // 2
    thr = list(range(1, max_exact))
    n = max_exact
    for bucket in range(max_exact, REL_BUCKETS):
        while True:
            large = max_exact + int(math.log(max(n, max_exact) / max_exact)
                                    / math.log(REL_MAX_DIST / max_exact) * (REL_BUCKETS - max_exact))
            if min(large, REL_BUCKETS - 1) >= bucket:
                break
            n += 1
        thr.append(n)
    return thr


T5_THRESHOLDS = _t5_thresholds()
T5_FAR = T5_THRESHOLDS[-1]


def _softmax_tiles(s_list, c_list, states, vt_list):
    stats = []
    for s_t, c, (m_old, l_old, _) in zip(s_list, c_list, states):
        m_new = jnp.maximum(m_old, jnp.max(s_t, axis=0, keepdims=True) + c)
        alpha = jnp.exp2(m_old - m_new)
        p_t = jnp.exp2(s_t - (m_new - c))
        stats.append((m_new, alpha, alpha * l_old + jnp.sum(p_t, axis=0, keepdims=True), p_t.astype(BF16)))
    return tuple((m_new, l_new, alpha * acc + _dot(vt, p_t))
                 for (m_new, alpha, l_new, p_t), (_, _, acc), vt in zip(stats, states, vt_list))


def _transpose_into(vt_ref, v_ref, chunk):
    seq = v_ref.shape[0]
    for c in range(seq // chunk):
        vt_ref[:, c * chunk:(c + 1) * chunk] = v_ref[c * chunk:(c + 1) * chunk, :].astype(F32).T.astype(BF16)


def _diff_attn_kernel(qfirst_ref, klast_ref, q_ref, k_ref, v_ref, qpos_ref, kpos_ref, subln_ref, table_ref, lam_ref,
                      o_ref, vt_ref, *, tq, tk, scale2, out_scale):
    b, i = pl.program_id(0), pl.program_id(1)
    nq = pl.num_programs(1)
    seq = k_ref.shape[0]
    nk = seq // tk
    w = DF_V_DIM

    @pl.when(i == 0)
    def _():
        _transpose_into(vt_ref, v_ref, tk)

    n_tiles = (i * tq + tq - 1) // tk + 1
    qf = qfirst_ref[b * nq + i]
    n_far = lax.while_loop(
        lambda j: (j * tk + tk - 1 <= i * tq) & (qf - klast_ref[b * nk + jnp.minimum(j, nk - 1)] >= T5_FAR),
        lambda j: j + 1, jnp.int32(0))

    dist = lax.broadcasted_iota(jnp.int32, (1, LANES), 1)
    qpos = qpos_ref[...]
    q_idx = i * tq + lax.broadcasted_iota(jnp.int32, (tk, tq), 1)
    k_off = lax.broadcasted_iota(jnp.int32, (tk, tq), 0)
    lane = lax.broadcasted_iota(jnp.int32, (tq, w), 1)

    bias_rows, c_far, qm = [], [], []
    for h in range(DF_HEADS):
        bias_vec = jnp.full((1, LANES), table_ref[h], F32)
        for bucket, thr in enumerate(T5_THRESHOLDS, start=1):
            bias_vec = jnp.where(dist >= thr, table_ref[bucket * DF_HEADS + h], bias_vec)
        bias_rows.append(jnp.broadcast_to(bias_vec * LOG2E, (tk, LANES)))
        c_far.append(table_ref[(REL_BUCKETS - 1) * DF_HEADS + h] * LOG2E)
        qh = q_ref[:, h * w:(h + 1) * w].astype(F32) * scale2
        qm.append([jnp.where((lane >= mi * DF_HEAD_DIM) & (lane < (mi + 1) * DF_HEAD_DIM), qh, 0.0).astype(BF16)
                   for mi in range(2)])

    def tiles(j, h):
        off = pl.multiple_of(j * tk, tk)
        return k_ref[pl.ds(off, tk), h * w:(h + 1) * w], vt_ref[h * w:(h + 1) * w, pl.ds(off, tk)], off

    chains = [(h, mi) for h in range(DF_HEADS) for mi in range(2)]

    def far_body(j, st):
        kv = [tiles(j, h) for h in range(DF_HEADS)]
        s = [_dot_t(kv[h][0], qm[h][mi]) for h, mi in chains]
        return _softmax_tiles(s, [c_far[h] for h, _ in chains], st, [kv[h][1] for h, _ in chains])

    def near_body(j, st):
        off = pl.multiple_of(j * tk, tk)
        n = jnp.clip(qpos - kpos_ref[pl.ds(off, tk), :], 0, LANES - 1)
        keep = q_idx >= off + k_off
        kv = [tiles(j, h) for h in range(DF_HEADS)]
        bias = [jnp.concatenate(
            [jnp.take_along_axis(bias_rows[h], n[:, cb * LANES:(cb + 1) * LANES], axis=1)
             for cb in range(tq // LANES)], axis=1) for h in range(DF_HEADS)]
        s = [jnp.where(keep, _dot_t(kv[h][0], qm[h][mi]) + bias[h], NEG_BIG) for h, mi in chains]
        return _softmax_tiles(s, [0.0] * len(chains), st, [kv[h][1] for h, _ in chains])

    init = tuple((jnp.full((1, tq), NEG_BIG, F32), jnp.zeros((1, tq), F32), jnp.zeros((w, tq), F32))
                 for _ in range(2 * DF_HEADS))
    st = lax.fori_loop(0, n_far, far_body, init)
    st = lax.fori_loop(n_far, n_tiles, near_body, st)
    for h in range(DF_HEADS):
        s0, s1 = st[2 * h], st[2 * h + 1]
        d_t = s0[2] / s0[1] - lam_ref[0] * (s1[2] / s1[1])
        ms = jnp.mean(d_t * d_t, axis=0, keepdims=True)
        y_t = d_t * lax.rsqrt(ms + DF_SUBLN_EPS) * (subln_ref[...] * out_scale)
        o_ref[:, h * w:(h + 1) * w] = y_t.T.astype(o_ref.dtype)


def diff_attention(pa, positions, rel_bias, lam, lambda_init, subln_w, *, tq=ATTN_TILE):
    batch, seq = positions.shape
    tokens = batch * seq
    tq = min(tq, seq)
    tk = tq
    nq, nk = seq // tq, seq // tk
    qfirst = positions[:, ::tq].reshape(-1)
    klast = positions[:, tk - 1::tk].reshape(-1)
    qpos = positions.reshape(batch, 1, seq)
    kpos = positions.reshape(batch, seq, 1)
    wd = DF_DIM
    grid_spec = pltpu.PrefetchScalarGridSpec(
        num_scalar_prefetch=2,
        grid=(batch, nq),
        in_specs=[pl.BlockSpec((tq, wd), lambda b, i, *_: (b * nq + i, OFF_DQ // wd)),
                  pl.BlockSpec((seq, wd), lambda b, i, *_: (b, OFF_DK // wd)),
                  pl.BlockSpec((seq, wd), lambda b, i, *_: (b, OFF_DV // wd)),
                  pl.BlockSpec((None, 1, tq), lambda b, i, *_: (b, 0, i)),
                  pl.BlockSpec((None, seq, 1), lambda b, i, *_: (b, 0, 0)),
                  pl.BlockSpec((DF_V_DIM, 1), lambda b, i, *_: (0, 0)),
                  pl.BlockSpec(memory_space=pltpu.SMEM),
                  pl.BlockSpec(memory_space=pltpu.SMEM)],
        out_specs=pl.BlockSpec((tq, wd), lambda b, i, *_: (b * nq + i, 0)),
        scratch_shapes=[pltpu.VMEM((wd, seq), BF16)],
    )
    return pl.pallas_call(
        functools.partial(_diff_attn_kernel, tq=tq, tk=tk, scale2=DF_HEAD_DIM ** -0.5 * LOG2E,
                          out_scale=1.0 - lambda_init),
        grid_spec=grid_spec,
        out_shape=jax.ShapeDtypeStruct((tokens, DF_DIM), BF16),
        compiler_params=_cparams(("arbitrary", "arbitrary")),
        name="diff_attention",
    )(qfirst, klast, pa, pa, pa, qpos, kpos, subln_w.reshape(DF_V_DIM, 1), rel_bias.reshape(-1), lam.reshape(1))


ML_QK_PAD = 2 * LANES


def _rope_block(x, cos, sin):
    half = ML_ROPE // 2
    lane = lax.broadcasted_iota(jnp.int32, x.shape, 1)
    rot = jnp.where(lane < half, -pltpu.roll(x, LANES - half, axis=1),
                    jnp.where(lane < ML_ROPE, pltpu.roll(x, half, axis=1), 0.0))
    return x * cos + rot * sin


def _mla_prep_kernel(mq_ref, mkv_ref, kpe_ref, pos_ref, qn_w, kvn_w, wq_ref, wkv_ref, freq_ref,
                     qf_o, kf_o, v_o, *, qscale):
    ang = pos_ref[...].astype(F32) * freq_ref[...]
    cos, sin = jnp.cos(ang), jnp.sin(ang)
    qc = _rms(mq_ref[...].astype(F32), qn_w[...], NORM_EPS).astype(BF16)
    q_all = _dot(qc, wq_ref[...]) * qscale
    kvc = _rms(mkv_ref[...].astype(F32), kvn_w[...], NORM_EPS).astype(BF16)
    kvb = _dot(kvc, wkv_ref[...])
    kpe = _rope_block(kpe_ref[...], cos, sin).astype(BF16)
    nope_w = ML_HEADS * ML_NOPE
    for h in range(ML_HEADS):
        lo = h * ML_QK_PAD
        qf_o[:, lo:lo + LANES] = q_all[:, h * LANES:(h + 1) * LANES].astype(BF16)
        qf_o[:, lo + LANES:lo + 2 * LANES] = _rope_block(
            q_all[:, nope_w + h * LANES:nope_w + (h + 1) * LANES], cos, sin).astype(BF16)
        kf_o[:, lo:lo + LANES] = kvb[:, h * LANES:(h + 1) * LANES].astype(BF16)
        kf_o[:, lo + LANES:lo + 2 * LANES] = kpe
    v_o[...] = kvb[:, nope_w:].astype(BF16)


def mla_prep(pa, pf, positions, q_norm, kv_norm, wq_all, wkv, *, tm=512):
    tokens = pa.shape[0]
    tm = min(tm, tokens)
    half = ML_ROPE // 2
    inv_freq = ROPE_THETA ** (-jnp.arange(half, dtype=F32) / half)
    freq = jnp.concatenate([inv_freq, inv_freq, jnp.zeros((LANES - ML_ROPE,), F32)]).reshape(1, LANES)

    def full(a):
        return pl.BlockSpec(a.shape, lambda i: (0, 0))

    qn_w = q_norm.reshape(1, -1)
    kvn_w = kv_norm.reshape(1, -1)
    wide = ML_HEADS * ML_QK_PAD
    return pl.pallas_call(
        functools.partial(_mla_prep_kernel, qscale=(ML_NOPE + ML_ROPE) ** -0.5 * LOG2E),
        grid=(tokens // tm,),
        in_specs=[pl.BlockSpec((tm, ML_Q_RANK), lambda i: (i, OFF_MQ // ML_Q_RANK)),
                  pl.BlockSpec((tm, ML_KV_RANK), lambda i: (i, OFF_MKV // ML_KV_RANK)),
                  pl.BlockSpec((tm, LANES), lambda i: (i, OFF_KPE // LANES)),
                  pl.BlockSpec((tm, 1), lambda i: (i, 0)),
                  full(qn_w), full(kvn_w), full(wq_all), full(wkv), full(freq)],
        out_specs=[pl.BlockSpec((tm, wide), lambda i: (i, 0)),
                   pl.BlockSpec((tm, wide), lambda i: (i, 0)),
                   pl.BlockSpec((tm, ML_DIM), lambda i: (i, 0))],
        out_shape=[jax.ShapeDtypeStruct((tokens, wide), BF16),
                   jax.ShapeDtypeStruct((tokens, wide), BF16),
                   jax.ShapeDtypeStruct((tokens, ML_DIM), BF16)],
        compiler_params=_cparams(("parallel",)),
        name="mla_prep",
    )(pa, pa, pf, positions.reshape(tokens, 1), qn_w, kvn_w, wq_all, wkv, freq)


def _mla_attn_kernel(q_ref, k_ref, v_ref, o_ref, vt_ref, *, tq, tk):
    i = pl.program_id(1)
    wq = ML_QK_PAD

    @pl.when(i == 0)
    def _():
        _transpose_into(vt_ref, v_ref, tk)

    n_tiles = (i * tq + tq - 1) // tk + 1
    n_full = (i * tq + 1) // tk
    q_idx = i * tq + lax.broadcasted_iota(jnp.int32, (tk, tq), 1)
    k_off = lax.broadcasted_iota(jnp.int32, (tk, tq), 0)
    qh = [q_ref[:, h * wq:(h + 1) * wq] for h in range(ML_HEADS)]

    def tiles(j, h):
        off = pl.multiple_of(j * tk, tk)
        return (k_ref[pl.ds(off, tk), h * wq:(h + 1) * wq],
                vt_ref[h * ML_V:(h + 1) * ML_V, pl.ds(off, tk)], off)

    heads = range(ML_HEADS)

    def full_body(j, st):
        kv = [tiles(j, h) for h in heads]
        s = [_dot_t(kv[h][0], qh[h]) for h in heads]
        return _softmax_tiles(s, [0.0] * ML_HEADS, st, [kv[h][1] for h in heads])

    def diag_body(j, st):
        kv = [tiles(j, h) for h in heads]
        keep = q_idx >= kv[0][2] + k_off
        s = [jnp.where(keep, _dot_t(kv[h][0], qh[h]), NEG_BIG) for h in heads]
        return _softmax_tiles(s, [0.0] * ML_HEADS, st, [kv[h][1] for h in heads])

    st = tuple((jnp.full((1, tq), NEG_BIG, F32), jnp.zeros((1, tq), F32), jnp.zeros((ML_V, tq), F32))
               for _ in range(ML_HEADS))
    st = lax.fori_loop(0, n_full, full_body, st)
    st = lax.fori_loop(n_full, n_tiles, diag_body, st)
    for h in range(ML_HEADS):
        o_ref[:, h * ML_V:(h + 1) * ML_V] = (st[h][2] / st[h][1]).T.astype(o_ref.dtype)


def mla_attention(qf, kf, v, batch, *, tq=ATTN_TILE):
    tokens = qf.shape[0]
    seq = tokens // batch
    tq = min(tq, seq)
    tk = tq
    nq = seq // tq
    wide = qf.shape[1]
    return pl.pallas_call(
        functools.partial(_mla_attn_kernel, tq=tq, tk=tk),
        grid=(batch, nq),
        in_specs=[pl.BlockSpec((tq, wide), lambda b, i: (b * nq + i, 0)),
                  pl.BlockSpec((seq, wide), lambda b, i: (b, 0)),
                  pl.BlockSpec((seq, ML_DIM), lambda b, i: (b, 0))],
        out_specs=pl.BlockSpec((tq, ML_DIM), lambda b, i: (b * nq + i, 0)),
        out_shape=jax.ShapeDtypeStruct((tokens, ML_DIM), BF16),
        scratch_shapes=[pltpu.VMEM((ML_DIM, seq), BF16)],
        compiler_params=_cparams(("arbitrary", "arbitrary")),
        name="mla_attention",
    )(qf, kf, v)


def _cross_kernel(x_ref, nw_ref, wq_ref, kv_ref, wo_ref, o_ref):
    x = x_ref[...]
    q = _dot(_rms(x, nw_ref[...], NORM_EPS).astype(BF16), wq_ref[...])
    kv = kv_ref[...]
    scale = CA_HEAD_DIM ** -0.5
    outs = []
    for hh in range(CA_HEADS):
        sl = slice(hh * CA_HEAD_DIM, (hh + 1) * CA_HEAD_DIM)
        s = _dot_t(q[:, sl].astype(BF16), kv[:, sl]) * scale
        p = jnp.exp(s - jnp.max(s, axis=-1, keepdims=True))
        p = p / jnp.sum(p, axis=-1, keepdims=True)
        outs.append(_dot(p.astype(BF16), kv[:, CA_DIM + hh * CA_HEAD_DIM:CA_DIM + (hh + 1) * CA_HEAD_DIM]))
    o = jnp.concatenate(outs, axis=1).astype(BF16)
    o_ref[...] = x + _dot(o, wo_ref[...])


def cross_block(x, batch, norm_w, wq, kv, wo, *, tq=512):
    tokens, d = x.shape
    seq = tokens // batch
    tq = min(tq, seq)
    nq = seq // tq
    mem_len = kv.shape[0] // batch
    return pl.pallas_call(
        _cross_kernel,
        grid=(batch, nq),
        in_specs=[pl.BlockSpec((tq, d), lambda b, i: (b * nq + i, 0)),
                  pl.BlockSpec((1, d), lambda b, i: (0, 0)),
                  pl.BlockSpec(wq.shape, lambda b, i: (0, 0)),
                  pl.BlockSpec((mem_len, 2 * CA_DIM), lambda b, i: (b, 0)),
                  pl.BlockSpec(wo.shape, lambda b, i: (0, 0))],
        out_specs=pl.BlockSpec((tq, d), lambda b, i: (b * nq + i, 0)),
        out_shape=jax.ShapeDtypeStruct((tokens, d), F32),
        compiler_params=_cparams(("parallel", "parallel")),
        name="cross_block",
    )(x, norm_w.reshape(1, d), wq, kv, wo)


def _router_kernel(x_ref, nw_ref, wr_ref, br_ref, h_ref, sel_ref):
    h = _rms(x_ref[...], nw_ref[...], NORM_EPS)
    h_ref[...] = h.astype(h_ref.dtype)
    logits = _dot_x3(h, wr_ref[...])
    biased = logits + br_ref[...]
    lane = lax.broadcasted_iota(jnp.int32, logits.shape, 1)
    big = jnp.int32(LANES)

    def first_argmax(vals):
        mx = jnp.max(vals, axis=-1, keepdims=True)
        return jnp.min(jnp.where(vals == mx, lane, big), axis=-1, keepdims=True)

    def pick(vals, idx):
        return jnp.sum(jnp.where(lane == idx, vals, 0.0), axis=-1, keepdims=True)

    is_group = (lane >= MOE_EXPERTS) & (lane < MOE_EXPERTS + MOE_GROUPS)
    gl = jnp.where(is_group, logits, NEG_BIG)
    ge = jnp.exp(gl - jnp.max(gl, axis=-1, keepdims=True))
    gp = ge / jnp.sum(ge, axis=-1, keepdims=True)
    g_lane = first_argmax(jnp.where(is_group, biased, NEG_BIG))
    p_group = pick(gp, g_lane)
    lo = (g_lane - MOE_EXPERTS) * MOE_PER_GROUP
    in_group = (lane >= lo) & (lane < lo + MOE_PER_GROUP)
    eb = jnp.where(in_group, biased, NEG_BIG)
    i1 = first_argmax(eb)
    i2 = first_argmax(jnp.where(lane == i1, NEG_BIG, eb))
    l1, l2 = pick(logits, i1), pick(logits, i2)
    mx = jnp.maximum(l1, l2)
    e1, e2 = jnp.exp(l1 - mx), jnp.exp(l2 - mx)
    w1, w2 = e1 / (e1 + e2), e2 / (e1 + e2)
    sel_ref[...] = jnp.where(lane == SEL_E1, i1.astype(F32),
                             jnp.where(lane == SEL_E2, i2.astype(F32),
                                       jnp.where(lane == SEL_G1, w1 * p_group,
                                                 jnp.where(lane == SEL_G2, w2 * p_group, 0.0))))


SEL_E1, SEL_E2, SEL_G1, SEL_G2 = 0, 1, 2, 3


def moe_router(x, norm_w, w_router, b_router, *, tm=512):
    tokens, d = x.shape
    tm = min(tm, tokens)
    return pl.pallas_call(
        _router_kernel,
        grid=(tokens // tm,),
        in_specs=[pl.BlockSpec((tm, d), lambda i: (i, 0)),
                  pl.BlockSpec((1, d), lambda i: (0, 0)),
                  pl.BlockSpec((d, LANES), lambda i: (0, 0)),
                  pl.BlockSpec((1, LANES), lambda i: (0, 0))],
        out_specs=[pl.BlockSpec((tm, d), lambda i: (i, 0)),
                   pl.BlockSpec((tm, LANES), lambda i: (i, 0))],
        out_shape=[jax.ShapeDtypeStruct((tokens, d), F32),
                   jax.ShapeDtypeStruct((tokens, LANES), F32)],
        compiler_params=_cparams(("parallel",)),
        name="moe_router",
    )(x, norm_w.reshape(1, d), w_router, b_router)


def _moe_rank_kernel(sel_ref, ltri_ref, rank_ref, counts_ref, carry_ref):
    @pl.when(pl.program_id(0) == 0)
    def _():
        carry_ref[...] = jnp.zeros_like(carry_ref)

    sel = sel_ref[...]
    lane = lax.broadcasted_iota(jnp.int32, sel.shape, 1)
    lane_f = lane.astype(F32)
    oh1 = lane_f == sel[:, SEL_E1:SEL_E1 + 1]
    oh2 = lane_f == sel[:, SEL_E2:SEL_E2 + 1]
    f1, f2 = oh1.astype(F32), oh2.astype(F32)
    ltri = ltri_ref[...]
    before1 = _dot(ltri, f1.astype(BF16))
    before2 = _dot(ltri, f2.astype(BF16))
    c1 = jnp.sum(f1, axis=0, keepdims=True)
    c2 = jnp.sum(f2, axis=0, keepdims=True)
    carry = carry_ref[...]
    r1 = jnp.sum(jnp.where(oh1, before1 + carry, 0.0), axis=1, keepdims=True)
    r2 = jnp.sum(jnp.where(oh2, before2 + carry + c1, 0.0), axis=1, keepdims=True)
    rank_ref[...] = jnp.where(lane == SEL_E1, r1, jnp.where(lane == SEL_E2, r2, 0.0)).astype(jnp.int32)
    total = carry + c1 + c2
    carry_ref[...] = total
    counts_ref[...] = total.astype(jnp.int32)


def moe_rank(sel, *, tm=512):
    tokens = sel.shape[0]
    tm = min(tm, tokens)
    ltri = (jnp.arange(tm)[:, None] > jnp.arange(tm)[None, :]).astype(BF16)
    return pl.pallas_call(
        _moe_rank_kernel,
        grid=(tokens // tm,),
        in_specs=[pl.BlockSpec((tm, LANES), lambda i: (i, 0)),
                  pl.BlockSpec((tm, tm), lambda i: (0, 0))],
        out_specs=[pl.BlockSpec((tm, LANES), lambda i: (i, 0)),
                   pl.BlockSpec((1, LANES), lambda i: (0, 0))],
        out_shape=[jax.ShapeDtypeStruct((tokens, LANES), jnp.int32),
                   jax.ShapeDtypeStruct((1, LANES), jnp.int32)],
        scratch_shapes=[pltpu.VMEM((1, LANES), F32)],
        compiler_params=_cparams(("arbitrary",)),
        name="moe_rank",
    )(sel, ltri)


def _row_copy(src_ref, src_row, dst_ref, dst_row, sem):
    return pltpu.make_async_copy(src_ref.at[pl.ds(src_row, 1)], dst_ref.at[pl.ds(dst_row, 1)], sem)


def _moe_dispatch_kernel(dest_ref, h_ref, init_ref, xs_ref, sem, *, tm):
    del init_ref
    base = pl.program_id(0) * (2 * tm)

    def issue(r, carry):
        for s in range(2):
            _row_copy(h_ref, r, xs_ref, dest_ref[base + 2 * r + s], sem).start(priority=s)
        return carry

    def drain(r, carry):
        for s in range(2):
            _row_copy(h_ref, 0, xs_ref, 0, sem).wait()
        return carry

    lax.fori_loop(0, tm, issue, 0)
    lax.fori_loop(0, tm, drain, 0)


def moe_dispatch(h, dest, rows, *, tm=256):
    tokens, d = h.shape
    tm = min(tm, tokens)
    grid_spec = pltpu.PrefetchScalarGridSpec(
        num_scalar_prefetch=1,
        grid=(tokens // tm,),
        in_specs=[pl.BlockSpec((tm, d), lambda i, *_: (i, 0)),
                  pl.BlockSpec(memory_space=pl.ANY)],
        out_specs=pl.BlockSpec(memory_space=pl.ANY),
        scratch_shapes=[pltpu.SemaphoreType.DMA(())],
    )
    return pl.pallas_call(
        functools.partial(_moe_dispatch_kernel, tm=tm),
        grid_spec=grid_spec,
        out_shape=jax.ShapeDtypeStruct((rows, d), h.dtype),
        input_output_aliases={2: 0},
        compiler_params=_cparams(("arbitrary",)),
        name="moe_dispatch",
    )(dest, h, jnp.zeros((rows, d), h.dtype))


def _moe_expert_kernel(te_ref, nused_ref, xs_ref, wg_ref, wu_ref, wd_ref, ys_ref, wgb, wub, wdb):
    r = pl.program_id(0)
    used = r < nused_ref[0]
    changed = (r == 0) | (te_ref[r] != te_ref[jnp.maximum(r - 1, 0)])

    @pl.when(used & changed)
    def _():
        wgb[...] = wg_ref[...].astype(BF16)
        wub[...] = wu_ref[...].astype(BF16)
        wdb[...] = wd_ref[...].astype(BF16)

    @pl.when(used)
    def _():
        x = xs_ref[...].astype(BF16)
        gate_pre = _dot(x, wgb[...])
        hid = (gate_pre * jax.nn.sigmoid(gate_pre)) * _dot(x, wub[...])
        ys_ref[...] = _dot(hid.astype(BF16), wdb[...])

    @pl.when(jnp.logical_not(used))
    def _():
        ys_ref[...] = jnp.zeros_like(ys_ref)


def moe_experts(xs, tile_expert, n_used, wg, wu, wd, layer, *, tm):
    rows, d = xs.shape
    de = wg.shape[-1]
    grid_spec = pltpu.PrefetchScalarGridSpec(
        num_scalar_prefetch=2,
        grid=(rows // tm,),
        in_specs=[pl.BlockSpec((tm, d), lambda r, te, nu: (r, 0)),
                  pl.BlockSpec((None, None, d, de), lambda r, te, nu: (layer, te[r], 0, 0)),
                  pl.BlockSpec((None, None, d, de), lambda r, te, nu: (layer, te[r], 0, 0)),
                  pl.BlockSpec((None, None, de, d), lambda r, te, nu: (layer, te[r], 0, 0))],
        out_specs=pl.BlockSpec((tm, d), lambda r, te, nu: (r, 0)),
        scratch_shapes=[pltpu.VMEM((d, de), BF16), pltpu.VMEM((d, de), BF16), pltpu.VMEM((de, d), BF16)],
    )
    return pl.pallas_call(
        _moe_expert_kernel,
        grid_spec=grid_spec,
        out_shape=jax.ShapeDtypeStruct((rows, d), F32),
        compiler_params=_cparams(("arbitrary",)),
        name="moe_experts",
    )(tile_expert, n_used, xs, wg, wu, wd)


def _moe_combine_kernel(dest_ref, x_ref, sel_ref, ys_ref, o_ref, buf_ref, sem, *, tm):
    base = pl.program_id(0) * (2 * tm)

    def issue(r, carry):
        for s in range(2):
            _row_copy(ys_ref, dest_ref[base + 2 * r + s], buf_ref.at[s], r, sem).start(priority=s)
        return carry

    def drain(r, carry):
        for s in range(2):
            _row_copy(ys_ref, 0, buf_ref.at[s], 0, sem).wait()
        return carry

    lax.fori_loop(0, tm, issue, 0)
    lax.fori_loop(0, tm, drain, 0)
    sel = sel_ref[...]
    o_ref[...] = (x_ref[...] + sel[:, SEL_G1:SEL_G1 + 1] * buf_ref[0] + sel[:, SEL_G2:SEL_G2 + 1] * buf_ref[1])


def moe_combine(x, sel, ys, dest, *, tm=256):
    tokens, d = x.shape
    tm = min(tm, tokens)
    grid_spec = pltpu.PrefetchScalarGridSpec(
        num_scalar_prefetch=1,
        grid=(tokens // tm,),
        in_specs=[pl.BlockSpec((tm, d), lambda i, *_: (i, 0)),
                  pl.BlockSpec((tm, LANES), lambda i, *_: (i, 0)),
                  pl.BlockSpec(memory_space=pl.ANY)],
        out_specs=pl.BlockSpec((tm, d), lambda i, *_: (i, 0)),
        scratch_shapes=[pltpu.VMEM((2, tm, d), F32), pltpu.SemaphoreType.DMA(())],
    )
    return pl.pallas_call(
        functools.partial(_moe_combine_kernel, tm=tm),
        grid_spec=grid_spec,
        out_shape=jax.ShapeDtypeStruct((tokens, d), F32),
        compiler_params=_cparams(("arbitrary",)),
        name="moe_combine",
    )(dest, x, sel, ys)


def moe_block(x, norm_w, w_router, b_router, wg, wu, wd, layer, *, tm=MOE_ROW_TILE):
    tokens, d = x.shape
    n_exp = wg.shape[1]
    h, sel = moe_router(x, norm_w, w_router, b_router)
    rank, counts = moe_rank(sel)
    padded = (counts[0, :n_exp] + (tm - 1)) // tm * tm
    ends = jnp.cumsum(padded)
    starts = ends - padded
    experts = sel[:, SEL_E1:SEL_E2 + 1].astype(jnp.int32)
    dest = (starts[experts] + rank[:, SEL_E1:SEL_E2 + 1]).reshape(-1)
    rows = 2 * tokens + n_exp * tm
    tile_start = jnp.arange(rows // tm, dtype=jnp.int32) * tm
    tile_expert = jnp.minimum(jnp.sum(tile_start[:, None] >= ends[None, :], axis=1), n_exp - 1).astype(jnp.int32)
    n_used = (ends[-1] // tm).astype(jnp.int32).reshape(1)
    xs = moe_dispatch(h, dest, rows)
    ys = moe_experts(xs, tile_expert, n_used, wg, wu, wd, layer, tm=tm)
    return moe_combine(x, sel, ys, dest)


def _proj_weights(w_in_l, w_vres_l):
    d = w_in_l.shape[0]
    mla0 = RW_COLS + DF_COLS
    vres = jnp.zeros((d, RW_V_RANK), F32) if w_vres_l is None else w_vres_l
    part_f = [w_in_l[:, :RW_COLS],
              vres, jnp.zeros((d, LANES - RW_V_RANK), F32),
              w_in_l[:, mla0 + ML_Q_RANK + ML_KV_RANK:mla0 + ML_COLS], jnp.zeros((d, LANES - ML_ROPE), F32)]
    part_a = [w_in_l[:, RW_COLS:RW_COLS + DF_COLS],
              w_in_l[:, mla0 + ML_Q_RANK:mla0 + ML_Q_RANK + ML_KV_RANK],
              jnp.zeros((d, OFF_MQ - OFF_MKV - ML_KV_RANK), F32),
              w_in_l[:, mla0:mla0 + ML_Q_RANK]]
    return jnp.concatenate(part_f, axis=1).astype(BF16), jnp.concatenate(part_a, axis=1).astype(BF16)


def _pad_rows(w, rows, at=0):
    out = jnp.zeros((rows, w.shape[1]), w.dtype)
    return lax.dynamic_update_slice(out, w, (at, 0))


def kernel(x, mem, positions, rel_bias, final_norm, norm_mix, w_in, w_in_vres, w_out, tm_mu, tm_mu_vres, tm_w0, tm_w2, tm_a0, tm_a2, tm_v0, tm_v2, tm_g2, tm_k_k, tm_k_a, tm_r_k, tm_ln_w, tm_ln_b, da_lq1, da_lk1, da_lq2, da_lk2, da_subln, mla_q_norm, mla_wq_b, mla_kv_norm, mla_wkv_b, norm_cross, norm_mem, ca_wq, ca_wkv, ca_wo, norm_ffn, moe_w_group, moe_b_group, moe_w_expert, moe_b_expert, moe_w_gate, moe_w_up, moe_w_down):
    batch, seq, d = x.shape
    tokens = batch * seq
    depth = norm_mix.shape[0]
    xf = x.reshape(tokens, d)
    memf = mem.reshape(-1, d)
    positions = positions.astype(jnp.int32)

    head_of_lane = jnp.arange(RW_DIM) // RW_HEAD_DIM
    seg = (head_of_lane[:, None] == jnp.arange(LANES)[None, :]).astype(BF16)
    seg_t = seg.T
    row = lambda v: v.reshape(1, -1)

    v_first = None
    for l in range(depth):
        w_f, w_a = _proj_weights(w_in[l], None if l == 0 else w_in_vres[l - 1])
        proj = norm_matmul(xf, norm_mix[l], w_f, tn=PROJ_F_COLS // 2)
        pa = norm_matmul(xf, norm_mix[l], w_a, out_dtype=BF16, tn=PROJ_A_COLS // 2)

        mu = tm_mu[l]
        prm = dict(mu_r=row(mu[:RW_DIM]), mu_k=row(mu[RW_DIM:2 * RW_DIM]), mu_v=row(mu[2 * RW_DIM:3 * RW_DIM]),
                   mu_l=row(mu[3 * RW_DIM:]), w0=row(tm_w0[l]), a0=row(tm_a0[l]),
                   w2=_pad_rows(tm_w2[l], LANES, 0), a2=_pad_rows(tm_a2[l], LANES, RW_W_RANK),
                   g2=tm_g2[l].astype(BF16), k_k=row(tm_k_k[l]), k_a=row(tm_k_a[l]), r_k=row(tm_r_k[l]),
                   seg=seg, seg_t=seg_t)
        if l > 0:
            prm.update(mu_vr=jnp.pad(row(tm_mu_vres[l - 1]), ((0, 0), (0, LANES - RW_V_RANK))),
                       v0=row(tm_v0[l - 1]), v2=_pad_rows(tm_v2[l - 1], LANES, 0))
        r, lw, k, v, kap, beta, gate, bonus = rwkv_prep(proj, batch, v_first, prm)
        if l == 0:
            v_first = v
        o = rwkv_scan(r, lw, k, v, kap, beta, batch)
        y_a = rwkv_post(o, bonus, gate, tm_ln_w[l], tm_ln_b[l], seg, seg_t)

        lambda_init = 0.8 - 0.6 * math.exp(-0.3 * l)
        lam = (jnp.exp(jnp.sum(da_lq1[l] * da_lk1[l])) - jnp.exp(jnp.sum(da_lq2[l] * da_lk2[l])) + lambda_init)
        y_b = diff_attention(pa, positions, rel_bias, lam, lambda_init, da_subln[l])

        wq = mla_wq_b[l].reshape(ML_Q_RANK, ML_HEADS, ML_NOPE + ML_ROPE)
        wq_pe = jnp.pad(wq[:, :, ML_NOPE:], ((0, 0), (0, 0), (0, LANES - ML_ROPE)))
        wq_all = jnp.concatenate([wq[:, :, :ML_NOPE].reshape(ML_Q_RANK, -1),
                                  wq_pe.reshape(ML_Q_RANK, -1)], axis=1).astype(BF16)
        wkv = mla_wkv_b[l].reshape(ML_KV_RANK, ML_HEADS, ML_NOPE + ML_V)
        wkv = jnp.concatenate([wkv[:, :, :ML_NOPE].reshape(ML_KV_RANK, -1),
                               wkv[:, :, ML_NOPE:].reshape(ML_KV_RANK, -1)], axis=1).astype(BF16)
        qf, kf, v_mla = mla_prep(pa, proj, positions, mla_q_norm[l], mla_kv_norm[l], wq_all, wkv)
        y_c = mla_attention(qf, kf, v_mla, batch)

        wo = w_out[l].astype(BF16)
        xf = matmul_res([y_a, y_b, y_c],
                        [wo[:RW_DIM], wo[RW_DIM:RW_DIM + DF_DIM], wo[RW_DIM + DF_DIM:]], xf)

        kv_mem = norm_matmul(memf, norm_mem[l], ca_wkv[l].astype(BF16), out_dtype=BF16)
        xf = cross_block(xf, batch, norm_cross[l], ca_wq[l].astype(BF16), kv_mem, ca_wo[l].astype(BF16))

        w_router = jnp.concatenate(
            [moe_w_expert[l], moe_w_group[l], jnp.zeros((d, LANES - MOE_EXPERTS - MOE_GROUPS), F32)], axis=1)
        b_router = jnp.concatenate(
            [moe_b_expert[l], moe_b_group[l], jnp.zeros((LANES - MOE_EXPERTS - MOE_GROUPS,), F32)]).reshape(1, LANES)
        xf = moe_block(xf, norm_ffn[l], w_router, b_router, moe_w_gate, moe_w_up, moe_w_down, l)

    out = rmsnorm(xf, final_norm)
    return out.reshape(batch, seq, d)
```

```python
import functools
import math

import jax
import jax.numpy as jnp
from jax import lax
from jax.experimental import pallas as pl
from jax.experimental.pallas import tpu as pltpu

F32 = jnp.float32
BF16 = jnp.bfloat16

NORM_EPS = 1e-6
ROPE_THETA = 10000.0

RW_HEADS = 16
RW_HEAD_DIM = 64
RW_DIM = RW_HEADS * RW_HEAD_DIM
RW_W_RANK = 64
RW_A_RANK = 64
RW_G_RANK = 128
RW_V_RANK = 32
RW_LORA = RW_W_RANK + RW_A_RANK + RW_G_RANK
RW_LN_EPS = 64e-5
RW_COLS = 3 * RW_DIM + RW_LORA

DF_HEADS = 4
DF_HEAD_DIM = 64
DF_V_DIM = 2 * DF_HEAD_DIM
DF_QK = DF_HEADS * 2 * DF_HEAD_DIM
DF_DIM = DF_HEADS * DF_V_DIM
DF_COLS = 2 * DF_QK + DF_DIM
DF_SUBLN_EPS = 1e-5

ML_HEADS = 4
ML_Q_RANK = 384
ML_KV_RANK = 256
ML_NOPE = 128
ML_ROPE = 64
ML_V = 128
ML_DIM = ML_HEADS * ML_V
ML_COLS = ML_Q_RANK + ML_KV_RANK + ML_ROPE

REL_BUCKETS = 32
REL_MAX_DIST = 128

CA_HEADS = 4
CA_HEAD_DIM = 128
CA_DIM = CA_HEADS * CA_HEAD_DIM

MOE_GROUPS = 4
MOE_PER_GROUP = 8
MOE_EXPERTS = MOE_GROUPS * MOE_PER_GROUP

LANES = 128
SCAN_CHUNK = 64
SCAN_GROUP = 4
SCAN_BATCHES = 2
ATTN_TILE = 512
DMA_ISSUE_UNROLL = 8
MOE_ROW_TILE = 512
VMEM_LIMIT = 56 * 1024 * 1024
NEG_BIG = -1e30

LOG2E = 1.4426950408889634

OFF_R = 0
OFF_K = RW_DIM
OFF_V = 2 * RW_DIM
OFF_LORA = 3 * RW_DIM
OFF_VRES = OFF_LORA + RW_LORA
OFF_KPE = OFF_VRES + LANES
PROJ_F_COLS = OFF_KPE + LANES
OFF_DQ = 0
OFF_DK = OFF_DQ + DF_QK
OFF_DV = OFF_DK + DF_QK
OFF_MKV = OFF_DV + DF_DIM
OFF_MQ = 5 * ML_Q_RANK
PROJ_A_COLS = OFF_MQ + ML_Q_RANK


def _cparams(sem, vmem=VMEM_LIMIT, flags=None):
    return pltpu.CompilerParams(dimension_semantics=sem, vmem_limit_bytes=vmem, flags=flags)


def _dot(a, b):
    return jnp.dot(a, b, preferred_element_type=F32)


def _dot_t(a, b):
    return lax.dot_general(a, b, (((1,), (1,)), ((), ())), preferred_element_type=F32)


def _split3(x):
    hi = x.astype(BF16)
    r1 = x - hi.astype(F32)
    mid = r1.astype(BF16)
    lo = (r1 - mid.astype(F32)).astype(BF16)
    return hi, mid, lo


def _dot_rhs01(x, ones_bf16):
    hi = x.astype(BF16)
    lo = (x - hi.astype(F32)).astype(BF16)
    return _dot(hi, ones_bf16) + _dot(lo, ones_bf16)


def _dot_x3(a, b):
    ah = a.astype(BF16)
    al = (a - ah.astype(F32)).astype(BF16)
    bh = b.astype(BF16)
    bl = (b - bh.astype(F32)).astype(BF16)
    return _dot(ah, bh) + _dot(ah, bl) + _dot(al, bh)


def _rms(x, w, eps):
    ms = jnp.mean(x * x, axis=-1, keepdims=True)
    return x * lax.rsqrt(ms + eps) * w


def _rmsnorm_kernel(x_ref, w_ref, o_ref, *, eps):
    o_ref[...] = _rms(x_ref[...], w_ref[...], eps).astype(o_ref.dtype)


def rmsnorm(x, w, *, eps=NORM_EPS, out_dtype=F32, tm=512):
    m, d = x.shape
    tm = min(tm, m)
    return pl.pallas_call(
        functools.partial(_rmsnorm_kernel, eps=eps),
        grid=(m // tm,),
        in_specs=[pl.BlockSpec((tm, d), lambda i: (i, 0)),
                  pl.BlockSpec((1, d), lambda i: (0, 0))],
        out_specs=pl.BlockSpec((tm, d), lambda i: (i, 0)),
        out_shape=jax.ShapeDtypeStruct((m, d), out_dtype),
        compiler_params=_cparams(("parallel",)),
        name="rmsnorm",
    )(x, w.reshape(1, d))


def _norm_matmul_kernel(x_ref, nw_ref, w_ref, o_ref, xn_ref, *, eps):
    @pl.when(pl.program_id(1) == 0)
    def _():
        xn_ref[...] = _rms(x_ref[...], nw_ref[...], eps).astype(BF16)

    o_ref[...] = _dot(xn_ref[...], w_ref[...]).astype(o_ref.dtype)


def norm_matmul(x, nw, w, *, out_dtype=F32, tm=512, tn=None, eps=NORM_EPS):
    m, d = x.shape
    n = w.shape[1]
    tm = min(tm, m)
    tn = n if tn is None else tn
    return pl.pallas_call(
        functools.partial(_norm_matmul_kernel, eps=eps),
        grid=(m // tm, n // tn),
        in_specs=[pl.BlockSpec((tm, d), lambda i, j: (i, 0)),
                  pl.BlockSpec((1, d), lambda i, j: (0, 0)),
                  pl.BlockSpec((d, tn), lambda i, j: (0, j))],
        out_specs=pl.BlockSpec((tm, tn), lambda i, j: (i, j)),
        out_shape=jax.ShapeDtypeStruct((m, n), out_dtype),
        scratch_shapes=[pltpu.VMEM((tm, d), BF16)],
        compiler_params=_cparams(("parallel", "arbitrary")),
        name="norm_matmul",
    )(x, nw.reshape(1, d), w)


def _matmul_res_kernel(*refs, n_a):
    a_refs, w_refs = refs[:n_a], refs[n_a:2 * n_a]
    res_ref, o_ref = refs[2 * n_a], refs[2 * n_a + 1]
    acc = res_ref[...]
    for a_ref, w_ref in zip(a_refs, w_refs):
        acc = acc + _dot(a_ref[...].astype(BF16), w_ref[...])
    o_ref[...] = acc


def matmul_res(a_list, w_list, res, *, tm=1024, tn=1024):
    m, n = res.shape
    tm = min(tm, m)
    tn = min(tn, n)
    n_a = len(a_list)
    in_specs = ([pl.BlockSpec((tm, a.shape[1]), lambda i, j: (i, 0)) for a in a_list]
                + [pl.BlockSpec((w.shape[0], tn), lambda i, j: (0, j)) for w in w_list]
                + [pl.BlockSpec((tm, tn), lambda i, j: (i, j))])
    return pl.pallas_call(
        functools.partial(_matmul_res_kernel, n_a=n_a),
        grid=(m // tm, n // tn),
        in_specs=in_specs,
        out_specs=pl.BlockSpec((tm, tn), lambda i, j: (i, j)),
        out_shape=jax.ShapeDtypeStruct((m, n), F32),
        compiler_params=_cparams(("parallel", "arbitrary")),
        name="matmul_res",
    )(*a_list, *w_list, res)


def _softplus(z):
    return jnp.maximum(z, 0.0) + jnp.log(1.0 + jnp.exp(-jnp.abs(z)))


def _rwkv_prep_kernel(*refs, has_vres):
    if has_vres:
        (pr_ref, pk_ref, pv_ref, pl_ref, pvr_ref, vfirst_ref,
         mu_r, mu_k, mu_v, mu_l, mu_vr, w0, w2, a0, a2, g2, v0, v2,
         k_k, k_a, r_k, seg, seg_t,
         r_o, lw_o, k_o, v_o, kap_o, beta_o, g_o, bonus_o,
         last_r, last_k, last_v, last_l, last_vr) = refs
    else:
        (pr_ref, pk_ref, pv_ref, pl_ref,
         mu_r, mu_k, mu_v, mu_l, w0, w2, a0, a2, g2,
         k_k, k_a, r_k, seg, seg_t,
         r_o, lw_o, k_o, v_o, kap_o, beta_o, g_o, bonus_o,
         last_r, last_k, last_v, last_l) = refs
    t = pl.program_id(1)

    def shifted(p_ref, last_ref, mu_ref):
        p = p_ref[...]
        n = p.shape[0]
        carried = jnp.where(t == 0, 0.0, last_ref[0:1, :])
        row = lax.broadcasted_iota(jnp.int32, p.shape, 0)
        prev = jnp.where(row == 0, carried, pltpu.roll(p, 1, axis=0))
        last_ref[0:1, :] = p[n - 1:n, :]
        return p + mu_ref[...] * (prev - p)

    r = shifted(pr_ref, last_r, mu_r)
    k = shifted(pk_ref, last_k, mu_k)
    v = shifted(pv_ref, last_v, mu_v)
    lora = shifted(pl_ref, last_l, mu_l)
    wl = lora[:, :LANES]
    gl = lora[:, LANES:]

    lane = lax.broadcasted_iota(jnp.int32, wl.shape, 1)
    wl_t = jnp.where(lane < RW_W_RANK, jnp.tanh(wl), 0.0)
    al = jnp.where(lane >= RW_W_RANK, wl, 0.0)
    w_log = -_softplus(-(w0[...] + _dot_x3(wl_t, w2[...]))) - 0.5
    lw_o[...] = -jnp.exp(w_log)
    a = jax.nn.sigmoid(a0[...] + _dot_x3(al, a2[...]))
    g_o[...] = _dot(jax.nn.sigmoid(gl).astype(BF16), g2[...])

    segm, segm_t = seg[...], seg_t[...]

    def head_sum(x):
        return _dot_rhs01(_dot_rhs01(x, segm), segm_t)

    kk = k * k_k[...]
    kk = kk / jnp.maximum(jnp.sqrt(head_sum(kk * kk)), 1e-12)
    k = k * (1.0 + (a - 1.0) * k_a[...])
    if has_vres:
        vr = shifted(pvr_ref, last_vr, mu_vr)
        mix = jax.nn.sigmoid(v0[...] + _dot_x3(vr, v2[...]))
        v = v + (vfirst_ref[...] - v) * mix
    r_o[...] = r
    k_o[...] = k
    v_o[...] = v
    kap_o[...] = kk
    beta_o[...] = kk * a
    bonus_o[...] = head_sum(r * k * r_k[...]) * v


def rwkv_prep(proj, batch, vfirst, prm, *, tt=256):
    tokens = proj.shape[0]
    seq = tokens // batch
    tt = min(tt, seq)
    nt = seq // tt
    has_vres = vfirst is not None
    d = RW_DIM

    def rows(width, col):
        return pl.BlockSpec((tt, width), lambda b, t, col=col: (b * nt + t, col))

    def full(shape):
        return pl.BlockSpec(shape, lambda b, t: (0, 0))

    in_specs = [rows(d, OFF_R // d), rows(d, OFF_K // d), rows(d, OFF_V // d),
                rows(RW_LORA, OFF_LORA // RW_LORA)]
    args = [proj, proj, proj, proj]
    if has_vres:
        in_specs += [rows(LANES, OFF_VRES // LANES), rows(d, 0)]
        args += [proj, vfirst]
    names = ["mu_r", "mu_k", "mu_v", "mu_l"] + (["mu_vr"] if has_vres else []) + ["w0", "w2", "a0", "a2", "g2"]
    names += (["v0", "v2"] if has_vres else []) + ["k_k", "k_a", "r_k", "seg", "seg_t"]
    for nm in names:
        in_specs.append(full(prm[nm].shape))
        args.append(prm[nm])
    out_spec = pl.BlockSpec((tt, d), lambda b, t: (b * nt + t, 0))
    scratch = [pltpu.VMEM((8, d), F32)] * 3 + [pltpu.VMEM((8, RW_LORA), F32)]
    if has_vres:
        scratch.append(pltpu.VMEM((8, LANES), F32))
    return pl.pallas_call(
        functools.partial(_rwkv_prep_kernel, has_vres=has_vres),
        grid=(batch, nt),
        in_specs=in_specs,
        out_specs=[out_spec] * 8,
        out_shape=[jax.ShapeDtypeStruct((tokens, d), F32)] * 8,
        scratch_shapes=scratch,
        compiler_params=_cparams(("arbitrary", "arbitrary")),
        name="rwkv_prep",
    )(*args)


def _rwkv_scan_kernel(r_ref, lw_ref, k_ref, v_ref, kap_ref, beta_ref, tril_ref, bmask_ref,
                      o_ref, ht_ref):
    @pl.when(pl.program_id(1) == 0)
    def _():
        ht_ref[...] = jnp.zeros_like(ht_ref)

    n_batch, c, d = lw_ref.shape
    w = ht_ref.shape[1]
    g = w // RW_HEAD_DIM
    bmask = bmask_ref[...]
    bmask_b = bmask.astype(BF16)
    tril3 = tril_ref[...]
    t_idx = lax.broadcasted_iota(jnp.int32, (c, w), 0)
    s_idx = lax.broadcasted_iota(jnp.int32, (c, w), 1) % c
    strict = t_idx > s_idx
    incl = t_idx >= s_idx
    n_sq = int(math.log2(c))

    def stack(x):
        return jnp.concatenate([x.astype(BF16)] * g, axis=0) * bmask_b

    sls = [(bi, slice(None), slice(lo, lo + w)) for bi in range(n_batch) for lo in range(0, d, w)]
    groups = range(len(sls))
    lw = [lw_ref[sl] for sl in sls]
    cum = [_dot(tril3, jnp.concatenate(_split3(x), axis=0)) for x in lw]
    total = [x[c - 1:c, :] for x in cum]
    ar = [jnp.concatenate([-kap_ref[sls[gi]] * jnp.exp(cum[gi] - lw[gi]), r_ref[sls[gi]] * jnp.exp(cum[gi])],
                          axis=0).astype(BF16) for gi in groups]
    p_inv = [jnp.exp(-x) for x in cum]
    b_s = [stack(beta_ref[sls[gi]] * p_inv[gi]) for gi in groups]
    k_s = [stack(k_ref[sls[gi]] * p_inv[gi]) for gi in groups]
    v_n = [v_ref[sl] for sl in sls]
    v_s = [stack(x) for x in v_n]

    arb = [_dot_t(ar[gi], b_s[gi]) for gi in groups]
    ark = [_dot_t(ar[gi], k_s[gi]) for gi in groups]
    ab = [jnp.where(strict, m[:c], 0.0) for m in arb]
    rb = [jnp.where(incl, m[c:], 0.0).astype(BF16) for m in arb]
    akrk = [jnp.concatenate([jnp.where(strict, m[:c], 0.0), jnp.where(incl, m[c:], 0.0)], axis=0).astype(BF16)
            for m in ark]

    ht = [ht_ref[gi] for gi in groups]
    base = [_dot_t(ar[gi], ht[gi].astype(BF16)) + _dot(akrk[gi], v_s[gi]) for gi in groups]
    x = [m[:c] for m in base]
    lp = ab
    for i in range(n_sq):
        lpb = [m.astype(BF16) for m in lp]
        x = [x[gi] + _dot(lpb[gi], stack(x[gi])) for gi in groups]
        if i < n_sq - 1:
            lp = [_dot(lpb[gi], stack(lp[gi])) for gi in groups]
    for gi in groups:
        o_ref[sls[gi]] = base[gi][c:] + _dot(rb[gi], stack(x[gi]))

    for gi in groups:
        p_rem = jnp.exp(total[gi] - cum[gi])
        z = jnp.concatenate([beta_ref[sls[gi]] * p_rem, k_ref[sls[gi]] * p_rem], axis=0).astype(BF16)
        uv_t = jnp.concatenate([x[gi], v_n[gi]], axis=0).T.astype(BF16)
        ht_ref[gi] = ht[gi] * jnp.exp(total[gi]) + bmask * _dot(uv_t, z)


def rwkv_scan(r, lw, k, v, kap, beta, batch):
    tokens, d = r.shape
    seq = tokens // batch
    c = min(SCAN_CHUNK, seq)
    nc = seq // c
    gw = SCAN_GROUP * RW_HEAD_DIM
    rr = SCAN_GROUP * c
    assert c == RW_HEAD_DIM, "the stacking mask doubles as the head-block mask of the state"
    bb = math.gcd(batch, SCAN_BATCHES)
    tril = jnp.tile((jnp.arange(c)[:, None] >= jnp.arange(c)[None, :]).astype(BF16), (1, 3))
    bmask = (jnp.arange(rr)[:, None] // c == jnp.arange(gw)[None, :] // RW_HEAD_DIM).astype(F32)
    blk = pl.BlockSpec((bb, c, d), lambda b, i: (b, i, 0))
    as3d = lambda a: a.reshape(batch, seq, d)
    out = pl.pallas_call(
        _rwkv_scan_kernel,
        grid=(batch // bb, nc),
        in_specs=[blk] * 6 + [pl.BlockSpec((c, 3 * c), lambda b, i: (0, 0)),
                              pl.BlockSpec((rr, gw), lambda b, i: (0, 0))],
        out_specs=blk,
        out_shape=jax.ShapeDtypeStruct((batch, seq, d), F32),
        scratch_shapes=[pltpu.VMEM((bb * (d // gw), gw, gw), F32)],
        compiler_params=_cparams(("arbitrary", "arbitrary")),
        name="rwkv_scan",
    )(as3d(r), as3d(lw), as3d(k), as3d(v), as3d(kap), as3d(beta), tril, bmask)
    return out.reshape(tokens, d)


def _rwkv_post_kernel(o_ref, bonus_ref, g_ref, lnw_ref, lnb_ref, seg, seg_t, y_ref):
    segm, segm_t = seg[...], seg_t[...]

    def head_mean(x):
        return _dot_rhs01(_dot_rhs01(x, segm), segm_t) * (1.0 / RW_HEAD_DIM)

    o = o_ref[...]
    dlt = o - head_mean(o)
    var = head_mean(dlt * dlt)
    y = dlt * lax.rsqrt(var + RW_LN_EPS) * lnw_ref[...] + lnb_ref[...]
    y_ref[...] = ((y + bonus_ref[...]) * g_ref[...]).astype(y_ref.dtype)


def rwkv_post(o, bonus, g, ln_w, ln_b, seg, seg_t, *, tm=512):
    tokens, d = o.shape
    tm = min(tm, tokens)
    blk = pl.BlockSpec((tm, d), lambda i: (i, 0))
    vec = pl.BlockSpec((1, d), lambda i: (0, 0))
    return pl.pallas_call(
        _rwkv_post_kernel,
        grid=(tokens // tm,),
        in_specs=[blk, blk, blk, vec, vec,
                  pl.BlockSpec(seg.shape, lambda i: (0, 0)), pl.BlockSpec(seg_t.shape, lambda i: (0, 0))],
        out_specs=blk,
        out_shape=jax.ShapeDtypeStruct((tokens, d), BF16),
        compiler_params=_cparams(("parallel",)),
        name="rwkv_post",
    )(o, bonus, g, ln_w.reshape(1, d), ln_b.reshape(1, d), seg, seg_t)


def _t5_thresholds():
    max_exact = REL_BUCKETS // 2
    thr = list(range(1, max_exact))
    n = max_exact
    for bucket in range(max_exact, REL_BUCKETS):
        while True:
            large = max_exact + int(math.log(max(n, max_exact) / max_exact)
                                    / math.log(REL_MAX_DIST / max_exact) * (REL_BUCKETS - max_exact))
            if min(large, REL_BUCKETS - 1) >= bucket:
                break
            n += 1
        thr.append(n)
    return thr


T5_THRESHOLDS = _t5_thresholds()
T5_FAR = T5_THRESHOLDS[-1]


def _softmax_tiles(s_list, c_list, states, vt_list):
    stats = []
    for s_t, c, (m_old, l_old, _) in zip(s_list, c_list, states):
        m_new = jnp.maximum(m_old, jnp.max(s_t, axis=0, keepdims=True) + c)
        alpha = jnp.exp2(m_old - m_new)
        p_t = jnp.exp2(s_t - (m_new - c))
        stats.append((m_new, alpha, alpha * l_old + jnp.sum(p_t, axis=0, keepdims=True), p_t.astype(BF16)))
    return tuple((m_new, l_new, alpha * acc + _dot(vt, p_t))
                 for (m_new, alpha, l_new, p_t), (_, _, acc), vt in zip(stats, states, vt_list))


def _transpose_into(vt_ref, v_ref, chunk):
    seq = v_ref.shape[0]
    for c in range(seq // chunk):
        vt_ref[:, c * chunk:(c + 1) * chunk] = v_ref[c * chunk:(c + 1) * chunk, :].astype(F32).T.astype(BF16)


def _diff_attn_kernel(qfirst_ref, klast_ref, q_ref, k_ref, v_ref, qpos_ref, kpos_ref, subln_ref, table_ref, lam_ref,
                      o_ref, vt_ref, *, tq, tk, scale2, out_scale):
    b, i = pl.program_id(0), pl.program_id(1)
    nq = pl.num_programs(1)
    seq = k_ref.shape[0]
    nk = seq // tk
    w = DF_V_DIM

    @pl.when(i == 0)
    def _():
        _transpose_into(vt_ref, v_ref, tk)

    n_tiles = (i * tq + tq - 1) // tk + 1
    qf = qfirst_ref[b * nq + i]
    n_far = lax.while_loop(
        lambda j: (j * tk + tk - 1 <= i * tq) & (qf - klast_ref[b * nk + jnp.minimum(j, nk - 1)] >= T5_FAR),
        lambda j: j + 1, jnp.int32(0))

    dist = lax.broadcasted_iota(jnp.int32, (1, LANES), 1)
    qpos = qpos_ref[...]
    q_idx = i * tq + lax.broadcasted_iota(jnp.int32, (tk, tq), 1)
    k_off = lax.broadcasted_iota(jnp.int32, (tk, tq), 0)
    lane = lax.broadcasted_iota(jnp.int32, (tq, w), 1)

    bias_rows, c_far, qm = [], [], []
    for h in range(DF_HEADS):
        bias_vec = jnp.full((1, LANES), table_ref[h], F32)
        for bucket, thr in enumerate(T5_THRESHOLDS, start=1):
            bias_vec = jnp.where(dist >= thr, table_ref[bucket * DF_HEADS + h], bias_vec)
        bias_rows.append(jnp.broadcast_to(bias_vec * LOG2E, (tk, LANES)))
        c_far.append(table_ref[(REL_BUCKETS - 1) * DF_HEADS + h] * LOG2E)
        qh = q_ref[:, h * w:(h + 1) * w].astype(F32) * scale2
        qm.append([jnp.where((lane >= mi * DF_HEAD_DIM) & (lane < (mi + 1) * DF_HEAD_DIM), qh, 0.0).astype(BF16)
                   for mi in range(2)])

    def tiles(j, h):
        off = pl.multiple_of(j * tk, tk)
        return k_ref[pl.ds(off, tk), h * w:(h + 1) * w], vt_ref[h * w:(h + 1) * w, pl.ds(off, tk)], off

    chains = [(h, mi) for h in range(DF_HEADS) for mi in range(2)]

    def far_body(j, st):
        kv = [tiles(j, h) for h in range(DF_HEADS)]
        s = [_dot_t(kv[h][0], qm[h][mi]) for h, mi in chains]
        return _softmax_tiles(s, [c_far[h] for h, _ in chains], st, [kv[h][1] for h, _ in chains])

    def near_body(j, st):
        off = pl.multiple_of(j * tk, tk)
        n = jnp.clip(qpos - kpos_ref[pl.ds(off, tk), :], 0, LANES - 1)
        keep = q_idx >= off + k_off
        kv = [tiles(j, h) for h in range(DF_HEADS)]
        bias = [jnp.concatenate(
            [jnp.take_along_axis(bias_rows[h], n[:, cb * LANES:(cb + 1) * LANES], axis=1)
             for cb in range(tq // LANES)], axis=1) for h in range(DF_HEADS)]
        s = [jnp.where(keep, _dot_t(kv[h][0], qm[h][mi]) + bias[h], NEG_BIG) for h, mi in chains]
        return _softmax_tiles(s, [0.0] * len(chains), st, [kv[h][1] for h, _ in chains])

    init = tuple((jnp.full((1, tq), NEG_BIG, F32), jnp.zeros((1, tq), F32), jnp.zeros((w, tq), F32))
                 for _ in range(2 * DF_HEADS))
    st = lax.fori_loop(0, n_far, far_body, init)
    st = lax.fori_loop(n_far, n_tiles, near_body, st)
    for h in range(DF_HEADS):
        s0, s1 = st[2 * h], st[2 * h + 1]
        d_t = s0[2] / s0[1] - lam_ref[0] * (s1[2] / s1[1])
        ms = jnp.mean(d_t * d_t, axis=0, keepdims=True)
        y_t = d_t * lax.rsqrt(ms + DF_SUBLN_EPS) * (subln_ref[...] * out_scale)
        o_ref[:, h * w:(h + 1) * w] = y_t.T.astype(o_ref.dtype)


def diff_attention(pa, positions, rel_bias, lam, lambda_init, subln_w, *, tq=ATTN_TILE):
    batch, seq = positions.shape
    tokens = batch * seq
    tq = min(tq, seq)
    tk = tq
    nq, nk = seq // tq, seq // tk
    qfirst = positions[:, ::tq].reshape(-1)
    klast = positions[:, tk - 1::tk].reshape(-1)
    qpos = positions.reshape(batch, 1, seq)
    kpos = positions.reshape(batch, seq, 1)
    wd = DF_DIM
    grid_spec = pltpu.PrefetchScalarGridSpec(
        num_scalar_prefetch=2,
        grid=(batch, nq),
        in_specs=[pl.BlockSpec((tq, wd), lambda b, i, *_: (b * nq + i, OFF_DQ // wd)),
                  pl.BlockSpec((seq, wd), lambda b, i, *_: (b, OFF_DK // wd)),
                  pl.BlockSpec((seq, wd), lambda b, i, *_: (b, OFF_DV // wd)),
                  pl.BlockSpec((None, 1, tq), lambda b, i, *_: (b, 0, i)),
                  pl.BlockSpec((None, seq, 1), lambda b, i, *_: (b, 0, 0)),
                  pl.BlockSpec((DF_V_DIM, 1), lambda b, i, *_: (0, 0)),
                  pl.BlockSpec(memory_space=pltpu.SMEM),
                  pl.BlockSpec(memory_space=pltpu.SMEM)],
        out_specs=pl.BlockSpec((tq, wd), lambda b, i, *_: (b * nq + i, 0)),
        scratch_shapes=[pltpu.VMEM((wd, seq), BF16)],
    )
    return pl.pallas_call(
        functools.partial(_diff_attn_kernel, tq=tq, tk=tk, scale2=DF_HEAD_DIM ** -0.5 * LOG2E,
                          out_scale=1.0 - lambda_init),
        grid_spec=grid_spec,
        out_shape=jax.ShapeDtypeStruct((tokens, DF_DIM), BF16),
        compiler_params=_cparams(("arbitrary", "arbitrary")),
        name="diff_attention",
    )(qfirst, klast, pa, pa, pa, qpos, kpos, subln_w.reshape(DF_V_DIM, 1), rel_bias.reshape(-1), lam.reshape(1))


ML_QK_PAD = 2 * LANES


def _rope_block(x, cos, sin):
    half = ML_ROPE // 2
    lane = lax.broadcasted_iota(jnp.int32, x.shape, 1)
    rot = jnp.where(lane < half, -pltpu.roll(x, LANES - half, axis=1),
                    jnp.where(lane < ML_ROPE, pltpu.roll(x, half, axis=1), 0.0))
    return x * cos + rot * sin


def _mla_prep_kernel(mq_ref, mkv_ref, kpe_ref, pos_ref, qn_w, kvn_w, wq_ref, wkv_ref, freq_ref,
                     qf_o, kf_o, v_o, *, qscale):
    ang = pos_ref[...].astype(F32) * freq_ref[...]
    cos, sin = jnp.cos(ang), jnp.sin(ang)
    qc = _rms(mq_ref[...].astype(F32), qn_w[...], NORM_EPS).astype(BF16)
    q_all = _dot(qc, wq_ref[...]) * qscale
    kvc = _rms(mkv_ref[...].astype(F32), kvn_w[...], NORM_EPS).astype(BF16)
    kvb = _dot(kvc, wkv_ref[...])
    kpe = _rope_block(kpe_ref[...], cos, sin).astype(BF16)
    nope_w = ML_HEADS * ML_NOPE
    for h in range(ML_HEADS):
        lo = h * ML_QK_PAD
        qf_o[:, lo:lo + LANES] = q_all[:, h * LANES:(h + 1) * LANES].astype(BF16)
        qf_o[:, lo + LANES:lo + 2 * LANES] = _rope_block(
            q_all[:, nope_w + h * LANES:nope_w + (h + 1) * LANES], cos, sin).astype(BF16)
        kf_o[:, lo:lo + LANES] = kvb[:, h * LANES:(h + 1) * LANES].astype(BF16)
        kf_o[:, lo + LANES:lo + 2 * LANES] = kpe
    v_o[...] = kvb[:, nope_w:].astype(BF16)


def mla_prep(pa, pf, positions, q_norm, kv_norm, wq_all, wkv, *, tm=512):
    tokens = pa.shape[0]
    tm = min(tm, tokens)
    half = ML_ROPE // 2
    inv_freq = ROPE_THETA ** (-jnp.arange(half, dtype=F32) / half)
    freq = jnp.concatenate([inv_freq, inv_freq, jnp.zeros((LANES - ML_ROPE,), F32)]).reshape(1, LANES)

    def full(a):
        return pl.BlockSpec(a.shape, lambda i: (0, 0))

    qn_w = q_norm.reshape(1, -1)
    kvn_w = kv_norm.reshape(1, -1)
    wide = ML_HEADS * ML_QK_PAD
    return pl.pallas_call(
        functools.partial(_mla_prep_kernel, qscale=(ML_NOPE + ML_ROPE) ** -0.5 * LOG2E),
        grid=(tokens // tm,),
        in_specs=[pl.BlockSpec((tm, ML_Q_RANK), lambda i: (i, OFF_MQ // ML_Q_RANK)),
                  pl.BlockSpec((tm, ML_KV_RANK), lambda i: (i, OFF_MKV // ML_KV_RANK)),
                  pl.BlockSpec((tm, LANES), lambda i: (i, OFF_KPE // LANES)),
                  pl.BlockSpec((tm, 1), lambda i: (i, 0)),
                  full(qn_w), full(kvn_w), full(wq_all), full(wkv), full(freq)],
        out_specs=[pl.BlockSpec((tm, wide), lambda i: (i, 0)),
                   pl.BlockSpec((tm, wide), lambda i: (i, 0)),
                   pl.BlockSpec((tm, ML_DIM), lambda i: (i, 0))],
        out_shape=[jax.ShapeDtypeStruct((tokens, wide), BF16),
                   jax.ShapeDtypeStruct((tokens, wide), BF16),
                   jax.ShapeDtypeStruct((tokens, ML_DIM), BF16)],
        compiler_params=_cparams(("parallel",)),
        name="mla_prep",
    )(pa, pa, pf, positions.reshape(tokens, 1), qn_w, kvn_w, wq_all, wkv, freq)


def _mla_attn_kernel(q_ref, k_ref, v_ref, o_ref, vt_ref, *, tq, tk):
    i = pl.program_id(1)
    wq = ML_QK_PAD

    @pl.when(i == 0)
    def _():
        _transpose_into(vt_ref, v_ref, tk)

    n_tiles = (i * tq + tq - 1) // tk + 1
    n_full = (i * tq + 1) // tk
    q_idx = i * tq + lax.broadcasted_iota(jnp.int32, (tk, tq), 1)
    k_off = lax.broadcasted_iota(jnp.int32, (tk, tq), 0)
    qh = [q_ref[:, h * wq:(h + 1) * wq] for h in range(ML_HEADS)]

    def tiles(j, h):
        off = pl.multiple_of(j * tk, tk)
        return (k_ref[pl.ds(off, tk), h * wq:(h + 1) * wq],
                vt_ref[h * ML_V:(h + 1) * ML_V, pl.ds(off, tk)], off)

    heads = range(ML_HEADS)

    def full_body(j, st):
        kv = [tiles(j, h) for h in heads]
        s = [_dot_t(kv[h][0], qh[h]) for h in heads]
        return _softmax_tiles(s, [0.0] * ML_HEADS, st, [kv[h][1] for h in heads])

    def diag_body(j, st):
        kv = [tiles(j, h) for h in heads]
        keep = q_idx >= kv[0][2] + k_off
        s = [jnp.where(keep, _dot_t(kv[h][0], qh[h]), NEG_BIG) for h in heads]
        return _softmax_tiles(s, [0.0] * ML_HEADS, st, [kv[h][1] for h in heads])

    st = tuple((jnp.full((1, tq), NEG_BIG, F32), jnp.zeros((1, tq), F32), jnp.zeros((ML_V, tq), F32))
               for _ in range(ML_HEADS))
    st = lax.fori_loop(0, n_full, full_body, st)
    st = lax.fori_loop(n_full, n_tiles, diag_body, st)
    for h in range(ML_HEADS):
        o_ref[:, h * ML_V:(h + 1) * ML_V] = (st[h][2] / st[h][1]).T.astype(o_ref.dtype)


def mla_attention(qf, kf, v, batch, *, tq=ATTN_TILE):
    tokens = qf.shape[0]
    seq = tokens // batch
    tq = min(tq, seq)
    tk = tq
    nq = seq // tq
    wide = qf.shape[1]
    return pl.pallas_call(
        functools.partial(_mla_attn_kernel, tq=tq, tk=tk),
        grid=(batch, nq),
        in_specs=[pl.BlockSpec((tq, wide), lambda b, i: (b * nq + i, 0)),
                  pl.BlockSpec((seq, wide), lambda b, i: (b, 0)),
                  pl.BlockSpec((seq, ML_DIM), lambda b, i: (b, 0))],
        out_specs=pl.BlockSpec((tq, ML_DIM), lambda b, i: (b * nq + i, 0)),
        out_shape=jax.ShapeDtypeStruct((tokens, ML_DIM), BF16),
        scratch_shapes=[pltpu.VMEM((ML_DIM, seq), BF16)],
        compiler_params=_cparams(("arbitrary", "arbitrary")),
        name="mla_attention",
    )(qf, kf, v)


def _cross_kernel(x_ref, nw_ref, wq_ref, kv_ref, wo_ref, o_ref):
    x = x_ref[...]
    q = _dot(_rms(x, nw_ref[...], NORM_EPS).astype(BF16), wq_ref[...])
    kv = kv_ref[...]
    scale = CA_HEAD_DIM ** -0.5
    outs = []
    for hh in range(CA_HEADS):
        sl = slice(hh * CA_HEAD_DIM, (hh + 1) * CA_HEAD_DIM)
        s = _dot_t(q[:, sl].astype(BF16), kv[:, sl]) * scale
        p = jnp.exp(s - jnp.max(s, axis=-1, keepdims=True))
        p = p / jnp.sum(p, axis=-1, keepdims=True)
        outs.append(_dot(p.astype(BF16), kv[:, CA_DIM + hh * CA_HEAD_DIM:CA_DIM + (hh + 1) * CA_HEAD_DIM]))
    o = jnp.concatenate(outs, axis=1).astype(BF16)
    o_ref[...] = x + _dot(o, wo_ref[...])


def cross_block(x, batch, norm_w, wq, kv, wo, *, tq=512):
    tokens, d = x.shape
    seq = tokens // batch
    tq = min(tq, seq)
    nq = seq // tq
    mem_len = kv.shape[0] // batch
    return pl.pallas_call(
        _cross_kernel,
        grid=(batch, nq),
        in_specs=[pl.BlockSpec((tq, d), lambda b, i: (b * nq + i, 0)),
                  pl.BlockSpec((1, d), lambda b, i: (0, 0)),
                  pl.BlockSpec(wq.shape, lambda b, i: (0, 0)),
                  pl.BlockSpec((mem_len, 2 * CA_DIM), lambda b, i: (b, 0)),
                  pl.BlockSpec(wo.shape, lambda b, i: (0, 0))],
        out_specs=pl.BlockSpec((tq, d), lambda b, i: (b * nq + i, 0)),
        out_shape=jax.ShapeDtypeStruct((tokens, d), F32),
        compiler_params=_cparams(("parallel", "parallel")),
        name="cross_block",
    )(x, norm_w.reshape(1, d), wq, kv, wo)


def _router_kernel(x_ref, nw_ref, wr_ref, br_ref, h_ref, sel_ref):
    h = _rms(x_ref[...], nw_ref[...], NORM_EPS)
    h_ref[...] = h.astype(h_ref.dtype)
    logits = _dot_x3(h, wr_ref[...])
    biased = logits + br_ref[...]
    lane = lax.broadcasted_iota(jnp.int32, logits.shape, 1)
    big = jnp.int32(LANES)

    def first_argmax(vals):
        mx = jnp.max(vals, axis=-1, keepdims=True)
        return jnp.min(jnp.where(vals == mx, lane, big), axis=-1, keepdims=True)

    def pick(vals, idx):
        return jnp.sum(jnp.where(lane == idx, vals, 0.0), axis=-1, keepdims=True)

    is_group = (lane >= MOE_EXPERTS) & (lane < MOE_EXPERTS + MOE_GROUPS)
    gl = jnp.where(is_group, logits, NEG_BIG)
    ge = jnp.exp(gl - jnp.max(gl, axis=-1, keepdims=True))
    gp = ge / jnp.sum(ge, axis=-1, keepdims=True)
    g_lane = first_argmax(jnp.where(is_group, biased, NEG_BIG))
    p_group = pick(gp, g_lane)
    lo = (g_lane - MOE_EXPERTS) * MOE_PER_GROUP
    in_group = (lane >= lo) & (lane < lo + MOE_PER_GROUP)
    eb = jnp.where(in_group, biased, NEG_BIG)
    i1 = first_argmax(eb)
    i2 = first_argmax(jnp.where(lane == i1, NEG_BIG, eb))
    l1, l2 = pick(logits, i1), pick(logits, i2)
    mx = jnp.maximum(l1, l2)
    e1, e2 = jnp.exp(l1 - mx), jnp.exp(l2 - mx)
    w1, w2 = e1 / (e1 + e2), e2 / (e1 + e2)
    sel_ref[...] = jnp.where(lane == SEL_E1, i1.astype(F32),
                             jnp.where(lane == SEL_E2, i2.astype(F32),
                                       jnp.where(lane == SEL_G1, w1 * p_group,
                                                 jnp.where(lane == SEL_G2, w2 * p_group, 0.0))))


SEL_E1, SEL_E2, SEL_G1, SEL_G2 = 0, 1, 2, 3


def moe_router(x, norm_w, w_router, b_router, *, tm=512):
    tokens, d = x.shape
    tm = min(tm, tokens)
    return pl.pallas_call(
        _router_kernel,
        grid=(tokens // tm,),
        in_specs=[pl.BlockSpec((tm, d), lambda i: (i, 0)),
                  pl.BlockSpec((1, d), lambda i: (0, 0)),
                  pl.BlockSpec((d, LANES), lambda i: (0, 0)),
                  pl.BlockSpec((1, LANES), lambda i: (0, 0))],
        out_specs=[pl.BlockSpec((tm, d), lambda i: (i, 0)),
                   pl.BlockSpec((tm, LANES), lambda i: (i, 0))],
        out_shape=[jax.ShapeDtypeStruct((tokens, d), F32),
                   jax.ShapeDtypeStruct((tokens, LANES), F32)],
        compiler_params=_cparams(("parallel",)),
        name="moe_router",
    )(x, norm_w.reshape(1, d), w_router, b_router)


def _moe_rank_kernel(sel_ref, ltri_ref, rank_ref, counts_ref, carry_ref):
    @pl.when(pl.program_id(0) == 0)
    def _():
        carry_ref[...] = jnp.zeros_like(carry_ref)

    sel = sel_ref[...]
    lane = lax.broadcasted_iota(jnp.int32, sel.shape, 1)
    lane_f = lane.astype(F32)
    oh1 = lane_f == sel[:, SEL_E1:SEL_E1 + 1]
    oh2 = lane_f == sel[:, SEL_E2:SEL_E2 + 1]
    f1, f2 = oh1.astype(F32), oh2.astype(F32)
    ltri = ltri_ref[...]
    before1 = _dot(ltri, f1.astype(BF16))
    before2 = _dot(ltri, f2.astype(BF16))
    c1 = jnp.sum(f1, axis=0, keepdims=True)
    c2 = jnp.sum(f2, axis=0, keepdims=True)
    carry = carry_ref[...]
    r1 = jnp.sum(jnp.where(oh1, before1 + carry, 0.0), axis=1, keepdims=True)
    r2 = jnp.sum(jnp.where(oh2, before2 + carry + c1, 0.0), axis=1, keepdims=True)
    rank_ref[...] = jnp.where(lane == SEL_E1, r1, jnp.where(lane == SEL_E2, r2, 0.0)).astype(jnp.int32)
    total = carry + c1 + c2
    carry_ref[...] = total
    counts_ref[...] = total.astype(jnp.int32)


def moe_rank(sel, *, tm=512):
    tokens = sel.shape[0]
    tm = min(tm, tokens)
    ltri = (jnp.arange(tm)[:, None] > jnp.arange(tm)[None, :]).astype(BF16)
    return pl.pallas_call(
        _moe_rank_kernel,
        grid=(tokens // tm,),
        in_specs=[pl.BlockSpec((tm, LANES), lambda i: (i, 0)),
                  pl.BlockSpec((tm, tm), lambda i: (0, 0))],
        out_specs=[pl.BlockSpec((tm, LANES), lambda i: (i, 0)),
                   pl.BlockSpec((1, LANES), lambda i: (0, 0))],
        out_shape=[jax.ShapeDtypeStruct((tokens, LANES), jnp.int32),
                   jax.ShapeDtypeStruct((1, LANES), jnp.int32)],
        scratch_shapes=[pltpu.VMEM((1, LANES), F32)],
        compiler_params=_cparams(("arbitrary",)),
        name="moe_rank",
    )(sel, ltri)


def _row_copy(src_ref, src_row, dst_ref, dst_row, sem):
    return pltpu.make_async_copy(src_ref.at[pl.ds(src_row, 1)], dst_ref.at[pl.ds(dst_row, 1)], sem)


def _moe_dispatch_kernel(dest_ref, h_ref, init_ref, xs_ref, sem, *, tm):
    del init_ref
    base = pl.program_id(0) * (2 * tm)

    def issue(r, carry):
        for s in range(2):
            _row_copy(h_ref, r, xs_ref, dest_ref[base + 2 * r + s], sem).start()
        return carry

    lax.fori_loop(0, tm, issue, 0, unroll=DMA_ISSUE_UNROLL)
    for s in range(2):
        pltpu.make_async_copy(h_ref, xs_ref.at[pl.ds(0, tm)], sem).wait()


def moe_dispatch(h, dest, rows, *, tm=256):
    tokens, d = h.shape
    tm = min(tm, tokens)
    grid_spec = pltpu.PrefetchScalarGridSpec(
        num_scalar_prefetch=1,
        grid=(tokens // tm,),
        in_specs=[pl.BlockSpec((tm, d), lambda i, *_: (i, 0)),
                  pl.BlockSpec(memory_space=pl.ANY)],
        out_specs=pl.BlockSpec(memory_space=pl.ANY),
        scratch_shapes=[pltpu.SemaphoreType.DMA(())],
    )
    return pl.pallas_call(
        functools.partial(_moe_dispatch_kernel, tm=tm),
        grid_spec=grid_spec,
        out_shape=jax.ShapeDtypeStruct((rows, d), h.dtype),
        input_output_aliases={2: 0},
        compiler_params=_cparams(("arbitrary",)),
        name="moe_dispatch",
    )(dest, h, jnp.zeros((rows, d), h.dtype))


def _moe_expert_kernel(te_ref, nused_ref, xs_ref, wg_ref, wu_ref, wd_ref, ys_ref, wgb, wub, wdb):
    r = pl.program_id(0)
    used = r < nused_ref[0]
    changed = (r == 0) | (te_ref[r] != te_ref[jnp.maximum(r - 1, 0)])

    @pl.when(used & changed)
    def _():
        wgb[...] = wg_ref[...].astype(BF16)
        wub[...] = wu_ref[...].astype(BF16)
        wdb[...] = wd_ref[...].astype(BF16)

    @pl.when(used)
    def _():
        x = xs_ref[...].astype(BF16)
        gate_pre = _dot(x, wgb[...])
        hid = (gate_pre * jax.nn.sigmoid(gate_pre)) * _dot(x, wub[...])
        ys_ref[...] = _dot(hid.astype(BF16), wdb[...])

    @pl.when(jnp.logical_not(used))
    def _():
        ys_ref[...] = jnp.zeros_like(ys_ref)


def moe_experts(xs, tile_expert, n_used, wg, wu, wd, layer, *, tm):
    rows, d = xs.shape
    de = wg.shape[-1]
    grid_spec = pltpu.PrefetchScalarGridSpec(
        num_scalar_prefetch=2,
        grid=(rows // tm,),
        in_specs=[pl.BlockSpec((tm, d), lambda r, te, nu: (r, 0)),
                  pl.BlockSpec((None, None, d, de), lambda r, te, nu: (layer, te[r], 0, 0)),
                  pl.BlockSpec((None, None, d, de), lambda r, te, nu: (layer, te[r], 0, 0)),
                  pl.BlockSpec((None, None, de, d), lambda r, te, nu: (layer, te[r], 0, 0))],
        out_specs=pl.BlockSpec((tm, d), lambda r, te, nu: (r, 0)),
        scratch_shapes=[pltpu.VMEM((d, de), BF16), pltpu.VMEM((d, de), BF16), pltpu.VMEM((de, d), BF16)],
    )
    return pl.pallas_call(
        _moe_expert_kernel,
        grid_spec=grid_spec,
        out_shape=jax.ShapeDtypeStruct((rows, d), F32),
        compiler_params=_cparams(("arbitrary",)),
        name="moe_experts",
    )(tile_expert, n_used, xs, wg, wu, wd)


def _moe_combine_kernel(dest_ref, x_ref, sel_ref, ys_ref, o_ref, buf_ref, sem, *, tm):
    base = pl.program_id(0) * (2 * tm)

    def issue(r, carry):
        for s in range(2):
            _row_copy(ys_ref, dest_ref[base + 2 * r + s], buf_ref.at[s], r, sem).start()
        return carry

    lax.fori_loop(0, tm, issue, 0, unroll=DMA_ISSUE_UNROLL)
    for s in range(2):
        pltpu.make_async_copy(ys_ref.at[pl.ds(0, tm)], buf_ref.at[s], sem).wait()
    sel = sel_ref[...]
    o_ref[...] = (x_ref[...] + sel[:, SEL_G1:SEL_G1 + 1] * buf_ref[0] + sel[:, SEL_G2:SEL_G2 + 1] * buf_ref[1])


def moe_combine(x, sel, ys, dest, *, tm=256):
    tokens, d = x.shape
    tm = min(tm, tokens)
    grid_spec = pltpu.PrefetchScalarGridSpec(
        num_scalar_prefetch=1,
        grid=(tokens // tm,),
        in_specs=[pl.BlockSpec((tm, d), lambda i, *_: (i, 0)),
                  pl.BlockSpec((tm, LANES), lambda i, *_: (i, 0)),
                  pl.BlockSpec(memory_space=pl.ANY)],
        out_specs=pl.BlockSpec((tm, d), lambda i, *_: (i, 0)),
        scratch_shapes=[pltpu.VMEM((2, tm, d), F32), pltpu.SemaphoreType.DMA(())],
    )
    return pl.pallas_call(
        functools.partial(_moe_combine_kernel, tm=tm),
        grid_spec=grid_spec,
        out_shape=jax.ShapeDtypeStruct((tokens, d), F32),
        compiler_params=_cparams(("arbitrary",)),
        name="moe_combine",
    )(dest, x, sel, ys)


def moe_block(x, norm_w, w_router, b_router, wg, wu, wd, layer, *, tm=MOE_ROW_TILE):
    tokens, d = x.shape
    n_exp = wg.shape[1]
    h, sel = moe_router(x, norm_w, w_router, b_router)
    rank, counts = moe_rank(sel)
    padded = (counts[0, :n_exp] + (tm - 1)) // tm * tm
    ends = jnp.cumsum(padded)
    starts = ends - padded
    experts = sel[:, SEL_E1:SEL_E2 + 1].astype(jnp.int32)
    dest = (starts[experts] + rank[:, SEL_E1:SEL_E2 + 1]).reshape(-1)
    rows = 2 * tokens + n_exp * tm
    tile_start = jnp.arange(rows // tm, dtype=jnp.int32) * tm
    tile_expert = jnp.minimum(jnp.sum(tile_start[:, None] >= ends[None, :], axis=1), n_exp - 1).astype(jnp.int32)
    n_used = (ends[-1] // tm).astype(jnp.int32).reshape(1)
    xs = moe_dispatch(h, dest, rows)
    ys = moe_experts(xs, tile_expert, n_used, wg, wu, wd, layer, tm=tm)
    return moe_combine(x, sel, ys, dest)


def _proj_weights(w_in_l, w_vres_l):
    d = w_in_l.shape[0]
    mla0 = RW_COLS + DF_COLS
    vres = jnp.zeros((d, RW_V_RANK), F32) if w_vres_l is None else w_vres_l
    part_f = [w_in_l[:, :RW_COLS],
              vres, jnp.zeros((d, LANES - RW_V_RANK), F32),
              w_in_l[:, mla0 + ML_Q_RANK + ML_KV_RANK:mla0 + ML_COLS], jnp.zeros((d, LANES - ML_ROPE), F32)]
    part_a = [w_in_l[:, RW_COLS:RW_COLS + DF_COLS],
              w_in_l[:, mla0 + ML_Q_RANK:mla0 + ML_Q_RANK + ML_KV_RANK],
              jnp.zeros((d, OFF_MQ - OFF_MKV - ML_KV_RANK), F32),
              w_in_l[:, mla0:mla0 + ML_Q_RANK]]
    return jnp.concatenate(part_f, axis=1).astype(BF16), jnp.concatenate(part_a, axis=1).astype(BF16)


def _pad_rows(w, rows, at=0):
    out = jnp.zeros((rows, w.shape[1]), w.dtype)
    return lax.dynamic_update_slice(out, w, (at, 0))


def kernel(x, mem, positions, rel_bias, final_norm, norm_mix, w_in, w_in_vres, w_out, tm_mu, tm_mu_vres, tm_w0, tm_w2, tm_a0, tm_a2, tm_v0, tm_v2, tm_g2, tm_k_k, tm_k_a, tm_r_k, tm_ln_w, tm_ln_b, da_lq1, da_lk1, da_lq2, da_lk2, da_subln, mla_q_norm, mla_wq_b, mla_kv_norm, mla_wkv_b, norm_cross, norm_mem, ca_wq, ca_wkv, ca_wo, norm_ffn, moe_w_group, moe_b_group, moe_w_expert, moe_b_expert, moe_w_gate, moe_w_up, moe_w_down):
    batch, seq, d = x.shape
    tokens = batch * seq
    depth = norm_mix.shape[0]
    xf = x.reshape(tokens, d)
    memf = mem.reshape(-1, d)
    positions = positions.astype(jnp.int32)

    head_of_lane = jnp.arange(RW_DIM) // RW_HEAD_DIM
    seg = (head_of_lane[:, None] == jnp.arange(LANES)[None, :]).astype(BF16)
    seg_t = seg.T
    row = lambda v: v.reshape(1, -1)

    v_first = None
    for l in range(depth):
        w_f, w_a = _proj_weights(w_in[l], None if l == 0 else w_in_vres[l - 1])
        proj = norm_matmul(xf, norm_mix[l], w_f, tn=PROJ_F_COLS // 2)
        pa = norm_matmul(xf, norm_mix[l], w_a, out_dtype=BF16, tn=PROJ_A_COLS // 2)

        mu = tm_mu[l]
        prm = dict(mu_r=row(mu[:RW_DIM]), mu_k=row(mu[RW_DIM:2 * RW_DIM]), mu_v=row(mu[2 * RW_DIM:3 * RW_DIM]),
                   mu_l=row(mu[3 * RW_DIM:]), w0=row(tm_w0[l]), a0=row(tm_a0[l]),
                   w2=_pad_rows(tm_w2[l], LANES, 0), a2=_pad_rows(tm_a2[l], LANES, RW_W_RANK),
                   g2=tm_g2[l].astype(BF16), k_k=row(tm_k_k[l]), k_a=row(tm_k_a[l]), r_k=row(tm_r_k[l]),
                   seg=seg, seg_t=seg_t)
        if l > 0:
            prm.update(mu_vr=jnp.pad(row(tm_mu_vres[l - 1]), ((0, 0), (0, LANES - RW_V_RANK))),
                       v0=row(tm_v0[l - 1]), v2=_pad_rows(tm_v2[l - 1], LANES, 0))
        r, lw, k, v, kap, beta, gate, bonus = rwkv_prep(proj, batch, v_first, prm)
        if l == 0:
            v_first = v
        o = rwkv_scan(r, lw, k, v, kap, beta, batch)
        y_a = rwkv_post(o, bonus, gate, tm_ln_w[l], tm_ln_b[l], seg, seg_t)

        lambda_init = 0.8 - 0.6 * math.exp(-0.3 * l)
        lam = (jnp.exp(jnp.sum(da_lq1[l] * da_lk1[l])) - jnp.exp(jnp.sum(da_lq2[l] * da_lk2[l])) + lambda_init)
        y_b = diff_attention(pa, positions, rel_bias, lam, lambda_init, da_subln[l])

        wq = mla_wq_b[l].reshape(ML_Q_RANK, ML_HEADS, ML_NOPE + ML_ROPE)
        wq_pe = jnp.pad(wq[:, :, ML_NOPE:], ((0, 0), (0, 0), (0, LANES - ML_ROPE)))
        wq_all = jnp.concatenate([wq[:, :, :ML_NOPE].reshape(ML_Q_RANK, -1),
                                  wq_pe.reshape(ML_Q_RANK, -1)], axis=1).astype(BF16)
        wkv = mla_wkv_b[l].reshape(ML_KV_RANK, ML_HEADS, ML_NOPE + ML_V)
        wkv = jnp.concatenate([wkv[:, :, :ML_NOPE].reshape(ML_KV_RANK, -1),
                               wkv[:, :, ML_NOPE:].reshape(ML_KV_RANK, -1)], axis=1).astype(BF16)
        qf, kf, v_mla = mla_prep(pa, proj, positions, mla_q_norm[l], mla_kv_norm[l], wq_all, wkv)
        y_c = mla_attention(qf, kf, v_mla, batch)

        wo = w_out[l].astype(BF16)
        xf = matmul_res([y_a, y_b, y_c],
                        [wo[:RW_DIM], wo[RW_DIM:RW_DIM + DF_DIM], wo[RW_DIM + DF_DIM:]], xf)

        kv_mem = norm_matmul(memf, norm_mem[l], ca_wkv[l].astype(BF16), out_dtype=BF16)
        xf = cross_block(xf, batch, norm_cross[l], ca_wq[l].astype(BF16), kv_mem, ca_wo[l].astype(BF16))

        w_router = jnp.concatenate(
            [moe_w_expert[l], moe_w_group[l], jnp.zeros((d, LANES - MOE_EXPERTS - MOE_GROUPS), F32)], axis=1)
        b_router = jnp.concatenate(
            [moe_b_expert[l], moe_b_group[l], jnp.zeros((LANES - MOE_EXPERTS - MOE_GROUPS,), F32)]).reshape(1, LANES)
        xf = moe_block(xf, norm_ffn[l], w_router, b_router, moe_w_gate, moe_w_up, moe_w_down, l)

    out = rmsnorm(xf, final_norm)
    return out.reshape(batch, seq, d)
```

```python
import functools
import math

import jax
import jax.numpy as jnp
from jax import lax
from jax.experimental import pallas as pl
from jax.experimental.pallas import tpu as pltpu

F32 = jnp.float32
BF16 = jnp.bfloat16

NORM_EPS = 1e-6
ROPE_THETA = 10000.0

RW_HEADS = 16
RW_HEAD_DIM = 64
RW_DIM = RW_HEADS * RW_HEAD_DIM
RW_W_RANK = 64
RW_A_RANK = 64
RW_G_RANK = 128
RW_V_RANK = 32
RW_LORA = RW_W_RANK + RW_A_RANK + RW_G_RANK
RW_LN_EPS = 64e-5
RW_COLS = 3 * RW_DIM + RW_LORA

DF_HEADS = 4
DF_HEAD_DIM = 64
DF_V_DIM = 2 * DF_HEAD_DIM
DF_QK = DF_HEADS * 2 * DF_HEAD_DIM
DF_DIM = DF_HEADS * DF_V_DIM
DF_COLS = 2 * DF_QK + DF_DIM
DF_SUBLN_EPS = 1e-5

ML_HEADS = 4
ML_Q_RANK = 384
ML_KV_RANK = 256
ML_NOPE = 128
ML_ROPE = 64
ML_V = 128
ML_DIM = ML_HEADS * ML_V
ML_COLS = ML_Q_RANK + ML_KV_RANK + ML_ROPE

REL_BUCKETS = 32
REL_MAX_DIST = 128

CA_HEADS = 4
CA_HEAD_DIM = 128
CA_DIM = CA_HEADS * CA_HEAD_DIM

MOE_GROUPS = 4
MOE_PER_GROUP = 8
MOE_EXPERTS = MOE_GROUPS * MOE_PER_GROUP

LANES = 128
SCAN_CHUNK = 64
SCAN_GROUP = 4
SCAN_BATCHES = 2
ATTN_TILE = 512
DMA_ISSUE_UNROLL = 8
MOE_ROW_TILE = 256
VMEM_LIMIT = 56 * 1024 * 1024
NEG_BIG = -1e30

LOG2E = 1.4426950408889634

OFF_R = 0
OFF_K = RW_DIM
OFF_V = 2 * RW_DIM
OFF_LORA = 3 * RW_DIM
OFF_VRES = OFF_LORA + RW_LORA
OFF_KPE = OFF_VRES + LANES
PROJ_F_COLS = OFF_KPE + LANES
OFF_DQ = 0
OFF_DK = OFF_DQ + DF_QK
OFF_DV = OFF_DK + DF_QK
OFF_MKV = OFF_DV + DF_DIM
OFF_MQ = 5 * ML_Q_RANK
PROJ_A_COLS = OFF_MQ + ML_Q_RANK


def _cparams(sem, vmem=VMEM_LIMIT, flags=None):
    return pltpu.CompilerParams(dimension_semantics=sem, vmem_limit_bytes=vmem, flags=flags)


def _dot(a, b):
    return jnp.dot(a, b, preferred_element_type=F32)


def _dot_t(a, b):
    return lax.dot_general(a, b, (((1,), (1,)), ((), ())), preferred_element_type=F32)


def _split3(x):
    hi = x.astype(BF16)
    r1 = x - hi.astype(F32)
    mid = r1.astype(BF16)
    lo = (r1 - mid.astype(F32)).astype(BF16)
    return hi, mid, lo


def _dot_rhs01(x, ones_bf16):
    hi = x.astype(BF16)
    lo = (x - hi.astype(F32)).astype(BF16)
    return _dot(hi, ones_bf16) + _dot(lo, ones_bf16)


def _dot_x3(a, b):
    ah = a.astype(BF16)
    al = (a - ah.astype(F32)).astype(BF16)
    bh = b.astype(BF16)
    bl = (b - bh.astype(F32)).astype(BF16)
    return _dot(ah, bh) + _dot(ah, bl) + _dot(al, bh)


def _rms(x, w, eps):
    ms = jnp.mean(x * x, axis=-1, keepdims=True)
    return x * lax.rsqrt(ms + eps) * w


def _rmsnorm_kernel(x_ref, w_ref, o_ref, *, eps):
    o_ref[...] = _rms(x_ref[...], w_ref[...], eps).astype(o_ref.dtype)


def rmsnorm(x, w, *, eps=NORM_EPS, out_dtype=F32, tm=512):
    m, d = x.shape
    tm = min(tm, m)
    return pl.pallas_call(
        functools.partial(_rmsnorm_kernel, eps=eps),
        grid=(m // tm,),
        in_specs=[pl.BlockSpec((tm, d), lambda i: (i, 0)),
                  pl.BlockSpec((1, d), lambda i: (0, 0))],
        out_specs=pl.BlockSpec((tm, d), lambda i: (i, 0)),
        out_shape=jax.ShapeDtypeStruct((m, d), out_dtype),
        compiler_params=_cparams(("parallel",)),
        name="rmsnorm",
    )(x, w.reshape(1, d))


def _norm_matmul_kernel(x_ref, nw_ref, w_ref, o_ref, xn_ref, *, eps):
    @pl.when(pl.program_id(1) == 0)
    def _():
        xn_ref[...] = _rms(x_ref[...], nw_ref[...], eps).astype(BF16)

    o_ref[...] = _dot(xn_ref[...], w_ref[...]).astype(o_ref.dtype)


def norm_matmul(x, nw, w, *, out_dtype=F32, tm=512, tn=None, eps=NORM_EPS):
    m, d = x.shape
    n = w.shape[1]
    tm = min(tm, m)
    tn = n if tn is None else tn
    return pl.pallas_call(
        functools.partial(_norm_matmul_kernel, eps=eps),
        grid=(m // tm, n // tn),
        in_specs=[pl.BlockSpec((tm, d), lambda i, j: (i, 0)),
                  pl.BlockSpec((1, d), lambda i, j: (0, 0)),
                  pl.BlockSpec((d, tn), lambda i, j: (0, j))],
        out_specs=pl.BlockSpec((tm, tn), lambda i, j: (i, j)),
        out_shape=jax.ShapeDtypeStruct((m, n), out_dtype),
        scratch_shapes=[pltpu.VMEM((tm, d), BF16)],
        compiler_params=_cparams(("parallel", "arbitrary")),
        name="norm_matmul",
    )(x, nw.reshape(1, d), w)


def _matmul_res_kernel(*refs, n_a):
    a_refs, w_refs = refs[:n_a], refs[n_a:2 * n_a]
    res_ref, o_ref = refs[2 * n_a], refs[2 * n_a + 1]
    acc = res_ref[...]
    for a_ref, w_ref in zip(a_refs, w_refs):
        acc = acc + _dot(a_ref[...].astype(BF16), w_ref[...])
    o_ref[...] = acc


def matmul_res(a_list, w_list, res, *, tm=1024, tn=1024):
    m, n = res.shape
    tm = min(tm, m)
    tn = min(tn, n)
    n_a = len(a_list)
    in_specs = ([pl.BlockSpec((tm, a.shape[1]), lambda i, j: (i, 0)) for a in a_list]
                + [pl.BlockSpec((w.shape[0], tn), lambda i, j: (0, j)) for w in w_list]
                + [pl.BlockSpec((tm, tn), lambda i, j: (i, j))])
    return pl.pallas_call(
        functools.partial(_matmul_res_kernel, n_a=n_a),
        grid=(m // tm, n // tn),
        in_specs=in_specs,
        out_specs=pl.BlockSpec((tm, tn), lambda i, j: (i, j)),
        out_shape=jax.ShapeDtypeStruct((m, n), F32),
        compiler_params=_cparams(("parallel", "arbitrary")),
        name="matmul_res",
    )(*a_list, *w_list, res)


def _softplus(z):
    return jnp.maximum(z, 0.0) + jnp.log(1.0 + jnp.exp(-jnp.abs(z)))


def _rwkv_prep_kernel(*refs, has_vres):
    if has_vres:
        (pr_ref, pk_ref, pv_ref, pl_ref, pvr_ref, vfirst_ref,
         mu_r, mu_k, mu_v, mu_l, mu_vr, w0, w2, a0, a2, g2, v0, v2,
         k_k, k_a, r_k, seg, seg_t,
         r_o, lw_o, k_o, v_o, kap_o, beta_o, g_o, bonus_o,
         last_r, last_k, last_v, last_l, last_vr) = refs
    else:
        (pr_ref, pk_ref, pv_ref, pl_ref,
         mu_r, mu_k, mu_v, mu_l, w0, w2, a0, a2, g2,
         k_k, k_a, r_k, seg, seg_t,
         r_o, lw_o, k_o, v_o, kap_o, beta_o, g_o, bonus_o,
         last_r, last_k, last_v, last_l) = refs
    t = pl.program_id(1)

    def shifted(p_ref, last_ref, mu_ref):
        p = p_ref[...]
        n = p.shape[0]
        carried = jnp.where(t == 0, 0.0, last_ref[0:1, :])
        row = lax.broadcasted_iota(jnp.int32, p.shape, 0)
        prev = jnp.where(row == 0, carried, pltpu.roll(p, 1, axis=0))
        last_ref[0:1, :] = p[n - 1:n, :]
        return p + mu_ref[...] * (prev - p)

    r = shifted(pr_ref, last_r, mu_r)
    k = shifted(pk_ref, last_k, mu_k)
    v = shifted(pv_ref, last_v, mu_v)
    lora = shifted(pl_ref, last_l, mu_l)
    wl = lora[:, :LANES]
    gl = lora[:, LANES:]

    lane = lax.broadcasted_iota(jnp.int32, wl.shape, 1)
    wl_t = jnp.where(lane < RW_W_RANK, jnp.tanh(wl), 0.0)
    al = jnp.where(lane >= RW_W_RANK, wl, 0.0)
    w_log = -_softplus(-(w0[...] + _dot_x3(wl_t, w2[...]))) - 0.5
    lw_o[...] = -jnp.exp(w_log)
    a = jax.nn.sigmoid(a0[...] + _dot_x3(al, a2[...]))
    g_o[...] = _dot(jax.nn.sigmoid(gl).astype(BF16), g2[...]).astype(g_o.dtype)

    segm, segm_t = seg[...], seg_t[...]

    def head_sum(x):
        return _dot_rhs01(_dot_rhs01(x, segm), segm_t)

    kk = k * k_k[...]
    kk = kk / jnp.maximum(jnp.sqrt(head_sum(kk * kk)), 1e-12)
    k = k * (1.0 + (a - 1.0) * k_a[...])
    if has_vres:
        vr = shifted(pvr_ref, last_vr, mu_vr)
        mix = jax.nn.sigmoid(v0[...] + _dot_x3(vr, v2[...]))
        v = v + (vfirst_ref[...] - v) * mix
    r_o[...] = r.astype(r_o.dtype)
    k_o[...] = k.astype(k_o.dtype)
    v_o[...] = v.astype(v_o.dtype)
    kap_o[...] = kk.astype(kap_o.dtype)
    beta_o[...] = (kk * a).astype(beta_o.dtype)
    bonus_o[...] = (head_sum(r * k * r_k[...]) * v).astype(bonus_o.dtype)


def rwkv_prep(proj, batch, vfirst, prm, *, tt=256):
    tokens = proj.shape[0]
    seq = tokens // batch
    tt = min(tt, seq)
    nt = seq // tt
    has_vres = vfirst is not None
    d = RW_DIM

    def rows(width, col):
        return pl.BlockSpec((tt, width), lambda b, t, col=col: (b * nt + t, col))

    def full(shape):
        return pl.BlockSpec(shape, lambda b, t: (0, 0))

    in_specs = [rows(d, OFF_R // d), rows(d, OFF_K // d), rows(d, OFF_V // d),
                rows(RW_LORA, OFF_LORA // RW_LORA)]
    args = [proj, proj, proj, proj]
    if has_vres:
        in_specs += [rows(LANES, OFF_VRES // LANES), rows(d, 0)]
        args += [proj, vfirst]
    names = ["mu_r", "mu_k", "mu_v", "mu_l"] + (["mu_vr"] if has_vres else []) + ["w0", "w2", "a0", "a2", "g2"]
    names += (["v0", "v2"] if has_vres else []) + ["k_k", "k_a", "r_k", "seg", "seg_t"]
    for nm in names:
        in_specs.append(full(prm[nm].shape))
        args.append(prm[nm])
    out_spec = pl.BlockSpec((tt, d), lambda b, t: (b * nt + t, 0))
    scratch = [pltpu.VMEM((8, d), F32)] * 3 + [pltpu.VMEM((8, RW_LORA), F32)]
    if has_vres:
        scratch.append(pltpu.VMEM((8, LANES), F32))
    return pl.pallas_call(
        functools.partial(_rwkv_prep_kernel, has_vres=has_vres),
        grid=(batch, nt),
        in_specs=in_specs,
        out_specs=[out_spec] * 8,
        out_shape=[jax.ShapeDtypeStruct((tokens, d), F32 if i == 1 else BF16) for i in range(8)],
        scratch_shapes=scratch,
        compiler_params=_cparams(("arbitrary", "arbitrary")),
        name="rwkv_prep",
    )(*args)


def _rwkv_scan_kernel(r_ref, lw_ref, k_ref, v_ref, kap_ref, beta_ref, tril_ref, bmask_ref,
                      o_ref, ht_ref):
    @pl.when(pl.program_id(1) == 0)
    def _():
        ht_ref[...] = jnp.zeros_like(ht_ref)

    n_batch, c, d = lw_ref.shape
    w = ht_ref.shape[1]
    g = w // RW_HEAD_DIM
    bmask = bmask_ref[...]
    bmask_b = bmask.astype(BF16)
    tril3 = tril_ref[...]
    t_idx = lax.broadcasted_iota(jnp.int32, (c, w), 0)
    s_idx = lax.broadcasted_iota(jnp.int32, (c, w), 1) % c
    strict = t_idx > s_idx
    incl = t_idx >= s_idx
    n_sq = int(math.log2(c))

    def stack(x):
        return jnp.concatenate([x.astype(BF16)] * g, axis=0) * bmask_b

    sls = [(bi, slice(None), slice(lo, lo + w)) for bi in range(n_batch) for lo in range(0, d, w)]
    groups = range(len(sls))
    lw = [lw_ref[sl] for sl in sls]
    cum = [_dot(tril3, jnp.concatenate(_split3(x), axis=0)) for x in lw]
    total = [x[c - 1:c, :] for x in cum]
    ar = [jnp.concatenate([-kap_ref[sls[gi]] * jnp.exp(cum[gi] - lw[gi]), r_ref[sls[gi]] * jnp.exp(cum[gi])],
                          axis=0).astype(BF16) for gi in groups]
    p_inv = [jnp.exp(-x) for x in cum]
    b_s = [stack(beta_ref[sls[gi]] * p_inv[gi]) for gi in groups]
    k_s = [stack(k_ref[sls[gi]] * p_inv[gi]) for gi in groups]
    v_n = [v_ref[sl] for sl in sls]
    v_s = [stack(x) for x in v_n]

    arb = [_dot_t(ar[gi], b_s[gi]) for gi in groups]
    ark = [_dot_t(ar[gi], k_s[gi]) for gi in groups]
    ab = [jnp.where(strict, m[:c], 0.0) for m in arb]
    rb = [jnp.where(incl, m[c:], 0.0).astype(BF16) for m in arb]
    akrk = [jnp.concatenate([jnp.where(strict, m[:c], 0.0), jnp.where(incl, m[c:], 0.0)], axis=0).astype(BF16)
            for m in ark]

    ht = [ht_ref[gi] for gi in groups]
    base = [_dot_t(ar[gi], ht[gi].astype(BF16)) + _dot(akrk[gi], v_s[gi]) for gi in groups]
    x = [m[:c] for m in base]
    lp = ab
    for i in range(n_sq):
        lpb = [m.astype(BF16) for m in lp]
        x = [x[gi] + _dot(lpb[gi], stack(x[gi])) for gi in groups]
        if i < n_sq - 1:
            lp = [_dot(lpb[gi], stack(lp[gi])) for gi in groups]
    for gi in groups:
        o_ref[sls[gi]] = base[gi][c:] + _dot(rb[gi], stack(x[gi]))

    for gi in groups:
        p_rem = jnp.exp(total[gi] - cum[gi])
        z = jnp.concatenate([beta_ref[sls[gi]] * p_rem, k_ref[sls[gi]] * p_rem], axis=0).astype(BF16)
        uv_t = jnp.concatenate([x[gi], v_n[gi].astype(F32)], axis=0).T.astype(BF16)
        ht_ref[gi] = ht[gi] * jnp.exp(total[gi]) + bmask * _dot(uv_t, z)


def rwkv_scan(r, lw, k, v, kap, beta, batch):
    tokens, d = r.shape
    seq = tokens // batch
    c = min(SCAN_CHUNK, seq)
    nc = seq // c
    gw = SCAN_GROUP * RW_HEAD_DIM
    rr = SCAN_GROUP * c
    assert c == RW_HEAD_DIM, "the stacking mask doubles as the head-block mask of the state"
    bb = math.gcd(batch, SCAN_BATCHES)
    tril = jnp.tile((jnp.arange(c)[:, None] >= jnp.arange(c)[None, :]).astype(BF16), (1, 3))
    bmask = (jnp.arange(rr)[:, None] // c == jnp.arange(gw)[None, :] // RW_HEAD_DIM).astype(F32)
    blk = pl.BlockSpec((bb, c, d), lambda b, i: (b, i, 0))
    as3d = lambda a: a.reshape(batch, seq, d)
    out = pl.pallas_call(
        _rwkv_scan_kernel,
        grid=(batch // bb, nc),
        in_specs=[blk] * 6 + [pl.BlockSpec((c, 3 * c), lambda b, i: (0, 0)),
                              pl.BlockSpec((rr, gw), lambda b, i: (0, 0))],
        out_specs=blk,
        out_shape=jax.ShapeDtypeStruct((batch, seq, d), F32),
        scratch_shapes=[pltpu.VMEM((bb * (d // gw), gw, gw), F32)],
        compiler_params=_cparams(("arbitrary", "arbitrary")),
        name="rwkv_scan",
    )(as3d(r), as3d(lw), as3d(k), as3d(v), as3d(kap), as3d(beta), tril, bmask)
    return out.reshape(tokens, d)


def _rwkv_post_kernel(o_ref, bonus_ref, g_ref, lnw_ref, lnb_ref, seg, seg_t, y_ref):
    segm, segm_t = seg[...], seg_t[...]

    def head_mean(x):
        return _dot_rhs01(_dot_rhs01(x, segm), segm_t) * (1.0 / RW_HEAD_DIM)

    o = o_ref[...]
    dlt = o - head_mean(o)
    var = head_mean(dlt * dlt)
    y = dlt * lax.rsqrt(var + RW_LN_EPS) * lnw_ref[...] + lnb_ref[...]
    y_ref[...] = ((y + bonus_ref[...]) * g_ref[...]).astype(y_ref.dtype)


def rwkv_post(o, bonus, g, ln_w, ln_b, seg, seg_t, *, tm=512):
    tokens, d = o.shape
    tm = min(tm, tokens)
    blk = pl.BlockSpec((tm, d), lambda i: (i, 0))
    vec = pl.BlockSpec((1, d), lambda i: (0, 0))
    return pl.pallas_call(
        _rwkv_post_kernel,
        grid=(tokens // tm,),
        in_specs=[blk, blk, blk, vec, vec,
                  pl.BlockSpec(seg.shape, lambda i: (0, 0)), pl.BlockSpec(seg_t.shape, lambda i: (0, 0))],
        out_specs=blk,
        out_shape=jax.ShapeDtypeStruct((tokens, d), BF16),
        compiler_params=_cparams(("parallel",)),
        name="rwkv_post",
    )(o, bonus, g, ln_w.reshape(1, d), ln_b.reshape(1, d), seg, seg_t)


def _t5_thresholds():
    max_exact = REL_BUCKETS // 2
    thr = list(range(1, max_exact))
    n = max_exact
    for bucket in range(max_exact, REL_BUCKETS):
        while True:
            large = max_exact + int(math.log(max(n, max_exact) / max_exact)
                                    / math.log(REL_MAX_DIST / max_exact) * (REL_BUCKETS - max_exact))
            if min(large, REL_BUCKETS - 1) >= bucket:
                break
            n += 1
        thr.append(n)
    return thr


T5_THRESHOLDS = _t5_thresholds()
T5_FAR = T5_THRESHOLDS[-1]


def _softmax_tiles(s_list, c_list, states, vt_list):
    stats = []
    for s_t, c, (m_old, l_old, _) in zip(s_list, c_list, states):
        m_new = jnp.maximum(m_old, jnp.max(s_t, axis=0, keepdims=True) + c)
        alpha = jnp.exp2(m_old - m_new)
        p_t = jnp.exp2(s_t - (m_new - c))
        stats.append((m_new, alpha, alpha * l_old + jnp.sum(p_t, axis=0, keepdims=True), p_t.astype(BF16)))
    return tuple((m_new, l_new, alpha * acc + _dot(vt, p_t))
                 for (m_new, alpha, l_new, p_t), (_, _, acc), vt in zip(stats, states, vt_list))


def _transpose_into(vt_ref, v_ref, chunk):
    seq = v_ref.shape[0]
    for c in range(seq // chunk):
        vt_ref[:, c * chunk:(c + 1) * chunk] = v_ref[c * chunk:(c + 1) * chunk, :].astype(F32).T.astype(BF16)


def _diff_attn_kernel(qfirst_ref, klast_ref, q_ref, k_ref, v_ref, qpos_ref, kpos_ref, subln_ref, table_ref, lam_ref,
                      o_ref, vt_ref, *, tq, tk, scale2, out_scale):
    b, i = pl.program_id(0), pl.program_id(1)
    nq = pl.num_programs(1)
    seq = k_ref.shape[0]
    nk = seq // tk
    w = DF_V_DIM

    @pl.when(i == 0)
    def _():
        _transpose_into(vt_ref, v_ref, tk)

    n_tiles = (i * tq + tq - 1) // tk + 1
    qf = qfirst_ref[b * nq + i]
    n_far = lax.while_loop(
        lambda j: (j * tk + tk - 1 <= i * tq) & (qf - klast_ref[b * nk + jnp.minimum(j, nk - 1)] >= T5_FAR),
        lambda j: j + 1, jnp.int32(0))

    dist = lax.broadcasted_iota(jnp.int32, (1, LANES), 1)
    qpos = qpos_ref[...]
    q_idx = i * tq + lax.broadcasted_iota(jnp.int32, (tk, tq), 1)
    k_off = lax.broadcasted_iota(jnp.int32, (tk, tq), 0)
    lane = lax.broadcasted_iota(jnp.int32, (tq, w), 1)

    bias_rows, c_far, qm = [], [], []
    for h in range(DF_HEADS):
        bias_vec = jnp.full((1, LANES), table_ref[h], F32)
        for bucket, thr in enumerate(T5_THRESHOLDS, start=1):
            bias_vec = jnp.where(dist >= thr, table_ref[bucket * DF_HEADS + h], bias_vec)
        bias_rows.append(jnp.broadcast_to(bias_vec * LOG2E, (tk, LANES)))
        c_far.append(table_ref[(REL_BUCKETS - 1) * DF_HEADS + h] * LOG2E)
        qh = q_ref[:, h * w:(h + 1) * w].astype(F32) * scale2
        qm.append([jnp.where((lane >= mi * DF_HEAD_DIM) & (lane < (mi + 1) * DF_HEAD_DIM), qh, 0.0).astype(BF16)
                   for mi in range(2)])

    def tiles(j, h):
        off = pl.multiple_of(j * tk, tk)
        return k_ref[pl.ds(off, tk), h * w:(h + 1) * w], vt_ref[h * w:(h + 1) * w, pl.ds(off, tk)], off

    chains = [(h, mi) for h in range(DF_HEADS) for mi in range(2)]

    def far_body(j, st):
        kv = [tiles(j, h) for h in range(DF_HEADS)]
        s = [_dot_t(kv[h][0], qm[h][mi]) for h, mi in chains]
        return _softmax_tiles(s, [c_far[h] for h, _ in chains], st, [kv[h][1] for h, _ in chains])

    def near_body(j, st):
        off = pl.multiple_of(j * tk, tk)
        n = jnp.clip(qpos - kpos_ref[pl.ds(off, tk), :], 0, LANES - 1)
        keep = q_idx >= off + k_off
        kv = [tiles(j, h) for h in range(DF_HEADS)]
        bias = [jnp.concatenate(
            [jnp.take_along_axis(bias_rows[h], n[:, cb * LANES:(cb + 1) * LANES], axis=1)
             for cb in range(tq // LANES)], axis=1) for h in range(DF_HEADS)]
        s = [jnp.where(keep, _dot_t(kv[h][0], qm[h][mi]) + bias[h], NEG_BIG) for h, mi in chains]
        return _softmax_tiles(s, [0.0] * len(chains), st, [kv[h][1] for h, _ in chains])

    init = tuple((jnp.full((1, tq), NEG_BIG, F32), jnp.zeros((1, tq), F32), jnp.zeros((w, tq), F32))
                 for _ in range(2 * DF_HEADS))
    st = lax.fori_loop(0, n_far, far_body, init)
    st = lax.fori_loop(n_far, n_tiles, near_body, st)
    for h in range(DF_HEADS):
        s0, s1 = st[2 * h], st[2 * h + 1]
        d_t = s0[2] / s0[1] - lam_ref[0] * (s1[2] / s1[1])
        ms = jnp.mean(d_t * d_t, axis=0, keepdims=True)
        y_t = d_t * lax.rsqrt(ms + DF_SUBLN_EPS) * (subln_ref[...] * out_scale)
        o_ref[:, h * w:(h + 1) * w] = y_t.T.astype(o_ref.dtype)


def diff_attention(pa, positions, rel_bias, lam, lambda_init, subln_w, *, tq=ATTN_TILE):
    batch, seq = positions.shape
    tokens = batch * seq
    tq = min(tq, seq)
    tk = tq
    nq, nk = seq // tq, seq // tk
    qfirst = positions[:, ::tq].reshape(-1)
    klast = positions[:, tk - 1::tk].reshape(-1)
    qpos = positions.reshape(batch, 1, seq)
    kpos = positions.reshape(batch, seq, 1)
    wd = DF_DIM
    grid_spec = pltpu.PrefetchScalarGridSpec(
        num_scalar_prefetch=2,
        grid=(batch, nq),
        in_specs=[pl.BlockSpec((tq, wd), lambda b, i, *_: (b * nq + i, OFF_DQ // wd)),
                  pl.BlockSpec((seq, wd), lambda b, i, *_: (b, OFF_DK // wd)),
                  pl.BlockSpec((seq, wd), lambda b, i, *_: (b, OFF_DV // wd)),
                  pl.BlockSpec((None, 1, tq), lambda b, i, *_: (b, 0, i)),
                  pl.BlockSpec((None, seq, 1), lambda b, i, *_: (b, 0, 0)),
                  pl.BlockSpec((DF_V_DIM, 1), lambda b, i, *_: (0, 0)),
                  pl.BlockSpec(memory_space=pltpu.SMEM),
                  pl.BlockSpec(memory_space=pltpu.SMEM)],
        out_specs=pl.BlockSpec((tq, wd), lambda b, i, *_: (b * nq + i, 0)),
        scratch_shapes=[pltpu.VMEM((wd, seq), BF16)],
    )
    return pl.pallas_call(
        functools.partial(_diff_attn_kernel, tq=tq, tk=tk, scale2=DF_HEAD_DIM ** -0.5 * LOG2E,
                          out_scale=1.0 - lambda_init),
        grid_spec=grid_spec,
        out_shape=jax.ShapeDtypeStruct((tokens, DF_DIM), BF16),
        compiler_params=_cparams(("arbitrary", "arbitrary")),
        name="diff_attention",
    )(qfirst, klast, pa, pa, pa, qpos, kpos, subln_w.reshape(DF_V_DIM, 1), rel_bias.reshape(-1), lam.reshape(1))


ML_QK_PAD = 2 * LANES


def _rope_block(x, cos, sin):
    half = ML_ROPE // 2
    lane = lax.broadcasted_iota(jnp.int32, x.shape, 1)
    rot = jnp.where(lane < half, -pltpu.roll(x, LANES - half, axis=1),
                    jnp.where(lane < ML_ROPE, pltpu.roll(x, half, axis=1), 0.0))
    return x * cos + rot * sin


def _mla_prep_kernel(mq_ref, mkv_ref, kpe_ref, pos_ref, qn_w, kvn_w, wq_ref, wkv_ref, freq_ref,
                     qf_o, kf_o, v_o, *, qscale):
    ang = pos_ref[...].astype(F32) * freq_ref[...]
    cos, sin = jnp.cos(ang), jnp.sin(ang)
    qc = _rms(mq_ref[...].astype(F32), qn_w[...], NORM_EPS).astype(BF16)
    q_all = _dot(qc, wq_ref[...]) * qscale
    kvc = _rms(mkv_ref[...].astype(F32), kvn_w[...], NORM_EPS).astype(BF16)
    kvb = _dot(kvc, wkv_ref[...])
    kpe = _rope_block(kpe_ref[...], cos, sin).astype(BF16)
    nope_w = ML_HEADS * ML_NOPE
    for h in range(ML_HEADS):
        lo = h * ML_QK_PAD
        qf_o[:, lo:lo + LANES] = q_all[:, h * LANES:(h + 1) * LANES].astype(BF16)
        qf_o[:, lo + LANES:lo + 2 * LANES] = _rope_block(
            q_all[:, nope_w + h * LANES:nope_w + (h + 1) * LANES], cos, sin).astype(BF16)
        kf_o[:, lo:lo + LANES] = kvb[:, h * LANES:(h + 1) * LANES].astype(BF16)
        kf_o[:, lo + LANES:lo + 2 * LANES] = kpe
    v_o[...] = kvb[:, nope_w:].astype(BF16)


def mla_prep(pa, pf, positions, q_norm, kv_norm, wq_all, wkv, *, tm=512):
    tokens = pa.shape[0]
    tm = min(tm, tokens)
    half = ML_ROPE // 2
    inv_freq = ROPE_THETA ** (-jnp.arange(half, dtype=F32) / half)
    freq = jnp.concatenate([inv_freq, inv_freq, jnp.zeros((LANES - ML_ROPE,), F32)]).reshape(1, LANES)

    def full(a):
        return pl.BlockSpec(a.shape, lambda i: (0, 0))

    qn_w = q_norm.reshape(1, -1)
    kvn_w = kv_norm.reshape(1, -1)
    wide = ML_HEADS * ML_QK_PAD
    return pl.pallas_call(
        functools.partial(_mla_prep_kernel, qscale=(ML_NOPE + ML_ROPE) ** -0.5 * LOG2E),
        grid=(tokens // tm,),
        in_specs=[pl.BlockSpec((tm, ML_Q_RANK), lambda i: (i, OFF_MQ // ML_Q_RANK)),
                  pl.BlockSpec((tm, ML_KV_RANK), lambda i: (i, OFF_MKV // ML_KV_RANK)),
                  pl.BlockSpec((tm, LANES), lambda i: (i, OFF_KPE // LANES)),
                  pl.BlockSpec((tm, 1), lambda i: (i, 0)),
                  full(qn_w), full(kvn_w), full(wq_all), full(wkv), full(freq)],
        out_specs=[pl.BlockSpec((tm, wide), lambda i: (i, 0)),
                   pl.BlockSpec((tm, wide), lambda i: (i, 0)),
                   pl.BlockSpec((tm, ML_DIM), lambda i: (i, 0))],
        out_shape=[jax.ShapeDtypeStruct((tokens, wide), BF16),
                   jax.ShapeDtypeStruct((tokens, wide), BF16),
                   jax.ShapeDtypeStruct((tokens, ML_DIM), BF16)],
        compiler_params=_cparams(("parallel",)),
        name="mla_prep",
    )(pa, pa, pf, positions.reshape(tokens, 1), qn_w, kvn_w, wq_all, wkv, freq)


def _mla_attn_kernel(q_ref, k_ref, v_ref, o_ref, vt_ref, *, tq, tk):
    i = pl.program_id(1)
    wq = ML_QK_PAD

    @pl.when(i == 0)
    def _():
        _transpose_into(vt_ref, v_ref, tk)

    n_tiles = (i * tq + tq - 1) // tk + 1
    n_full = (i * tq + 1) // tk
    q_idx = i * tq + lax.broadcasted_iota(jnp.int32, (tk, tq), 1)
    k_off = lax.broadcasted_iota(jnp.int32, (tk, tq), 0)
    qh = [q_ref[:, h * wq:(h + 1) * wq] for h in range(ML_HEADS)]

    def tiles(j, h):
        off = pl.multiple_of(j * tk, tk)
        return (k_ref[pl.ds(off, tk), h * wq:(h + 1) * wq],
                vt_ref[h * ML_V:(h + 1) * ML_V, pl.ds(off, tk)], off)

    heads = range(ML_HEADS)

    def full_body(j, st):
        kv = [tiles(j, h) for h in heads]
        s = [_dot_t(kv[h][0], qh[h]) for h in heads]
        return _softmax_tiles(s, [0.0] * ML_HEADS, st, [kv[h][1] for h in heads])

    def diag_body(j, st):
        kv = [tiles(j, h) for h in heads]
        keep = q_idx >= kv[0][2] + k_off
        s = [jnp.where(keep, _dot_t(kv[h][0], qh[h]), NEG_BIG) for h in heads]
        return _softmax_tiles(s, [0.0] * ML_HEADS, st, [kv[h][1] for h in heads])

    st = tuple((jnp.full((1, tq), NEG_BIG, F32), jnp.zeros((1, tq), F32), jnp.zeros((ML_V, tq), F32))
               for _ in range(ML_HEADS))
    st = lax.fori_loop(0, n_full, full_body, st)
    st = lax.fori_loop(n_full, n_tiles, diag_body, st)
    for h in range(ML_HEADS):
        o_ref[:, h * ML_V:(h + 1) * ML_V] = (st[h][2] / st[h][1]).T.astype(o_ref.dtype)


def mla_attention(qf, kf, v, batch, *, tq=ATTN_TILE):
    tokens = qf.shape[0]
    seq = tokens // batch
    tq = min(tq, seq)
    tk = tq
    nq = seq // tq
    wide = qf.shape[1]
    return pl.pallas_call(
        functools.partial(_mla_attn_kernel, tq=tq, tk=tk),
        grid=(batch, nq),
        in_specs=[pl.BlockSpec((tq, wide), lambda b, i: (b * nq + i, 0)),
                  pl.BlockSpec((seq, wide), lambda b, i: (b, 0)),
                  pl.BlockSpec((seq, ML_DIM), lambda b, i: (b, 0))],
        out_specs=pl.BlockSpec((tq, ML_DIM), lambda b, i: (b * nq + i, 0)),
        out_shape=jax.ShapeDtypeStruct((tokens, ML_DIM), BF16),
        scratch_shapes=[pltpu.VMEM((ML_DIM, seq), BF16)],
        compiler_params=_cparams(("arbitrary", "arbitrary")),
        name="mla_attention",
    )(qf, kf, v)


def _cross_kernel(x_ref, nw_ref, wq_ref, kv_ref, wo_ref, o_ref):
    x = x_ref[...]
    q = _dot(_rms(x, nw_ref[...], NORM_EPS).astype(BF16), wq_ref[...])
    kv = kv_ref[...]
    scale = CA_HEAD_DIM ** -0.5
    outs = []
    for hh in range(CA_HEADS):
        sl = slice(hh * CA_HEAD_DIM, (hh + 1) * CA_HEAD_DIM)
        s = _dot_t(q[:, sl].astype(BF16), kv[:, sl]) * scale
        p = jnp.exp(s - jnp.max(s, axis=-1, keepdims=True))
        p = p / jnp.sum(p, axis=-1, keepdims=True)
        outs.append(_dot(p.astype(BF16), kv[:, CA_DIM + hh * CA_HEAD_DIM:CA_DIM + (hh + 1) * CA_HEAD_DIM]))
    o = jnp.concatenate(outs, axis=1).astype(BF16)
    o_ref[...] = x + _dot(o, wo_ref[...])


def cross_block(x, batch, norm_w, wq, kv, wo, *, tq=512):
    tokens, d = x.shape
    seq = tokens // batch
    tq = min(tq, seq)
    nq = seq // tq
    mem_len = kv.shape[0] // batch
    return pl.pallas_call(
        _cross_kernel,
        grid=(batch, nq),
        in_specs=[pl.BlockSpec((tq, d), lambda b, i: (b * nq + i, 0)),
                  pl.BlockSpec((1, d), lambda b, i: (0, 0)),
                  pl.BlockSpec(wq.shape, lambda b, i: (0, 0)),
                  pl.BlockSpec((mem_len, 2 * CA_DIM), lambda b, i: (b, 0)),
                  pl.BlockSpec(wo.shape, lambda b, i: (0, 0))],
        out_specs=pl.BlockSpec((tq, d), lambda b, i: (b * nq + i, 0)),
        out_shape=jax.ShapeDtypeStruct((tokens, d), F32),
        compiler_params=_cparams(("parallel", "parallel")),
        name="cross_block",
    )(x, norm_w.reshape(1, d), wq, kv, wo)


def _router_kernel(x_ref, nw_ref, wr_ref, br_ref, h_ref, sel_ref):
    h = _rms(x_ref[...], nw_ref[...], NORM_EPS)
    h_ref[...] = h.astype(h_ref.dtype)
    logits = _dot_x3(h, wr_ref[...])
    biased = logits + br_ref[...]
    lane = lax.broadcasted_iota(jnp.int32, logits.shape, 1)
    big = jnp.int32(LANES)

    def first_argmax(vals):
        mx = jnp.max(vals, axis=-1, keepdims=True)
        return jnp.min(jnp.where(vals == mx, lane, big), axis=-1, keepdims=True)

    def pick(vals, idx):
        return jnp.sum(jnp.where(lane == idx, vals, 0.0), axis=-1, keepdims=True)

    is_group = (lane >= MOE_EXPERTS) & (lane < MOE_EXPERTS + MOE_GROUPS)
    gl = jnp.where(is_group, logits, NEG_BIG)
    ge = jnp.exp(gl - jnp.max(gl, axis=-1, keepdims=True))
    gp = ge / jnp.sum(ge, axis=-1, keepdims=True)
    g_lane = first_argmax(jnp.where(is_group, biased, NEG_BIG))
    p_group = pick(gp, g_lane)
    lo = (g_lane - MOE_EXPERTS) * MOE_PER_GROUP
    in_group = (lane >= lo) & (lane < lo + MOE_PER_GROUP)
    eb = jnp.where(in_group, biased, NEG_BIG)
    i1 = first_argmax(eb)
    i2 = first_argmax(jnp.where(lane == i1, NEG_BIG, eb))
    l1, l2 = pick(logits, i1), pick(logits, i2)
    mx = jnp.maximum(l1, l2)
    e1, e2 = jnp.exp(l1 - mx), jnp.exp(l2 - mx)
    w1, w2 = e1 / (e1 + e2), e2 / (e1 + e2)
    sel_ref[...] = jnp.where(lane == SEL_E1, i1.astype(F32),
                             jnp.where(lane == SEL_E2, i2.astype(F32),
                                       jnp.where(lane == SEL_G1, w1 * p_group,
                                                 jnp.where(lane == SEL_G2, w2 * p_group, 0.0))))


SEL_E1, SEL_E2, SEL_G1, SEL_G2 = 0, 1, 2, 3


def moe_router(x, norm_w, w_router, b_router, *, tm=512):
    tokens, d = x.shape
    tm = min(tm, tokens)
    return pl.pallas_call(
        _router_kernel,
        grid=(tokens // tm,),
        in_specs=[pl.BlockSpec((tm, d), lambda i: (i, 0)),
                  pl.BlockSpec((1, d), lambda i: (0, 0)),
                  pl.BlockSpec((d, LANES), lambda i: (0, 0)),
                  pl.BlockSpec((1, LANES), lambda i: (0, 0))],
        out_specs=[pl.BlockSpec((tm, d), lambda i: (i, 0)),
                   pl.BlockSpec((tm, LANES), lambda i: (i, 0))],
        out_shape=[jax.ShapeDtypeStruct((tokens, d), F32),
                   jax.ShapeDtypeStruct((tokens, LANES), F32)],
        compiler_params=_cparams(("parallel",)),
        name="moe_router",
    )(x, norm_w.reshape(1, d), w_router, b_router)


def _moe_rank_kernel(sel_ref, ltri_ref, rank_ref, counts_ref, carry_ref):
    @pl.when(pl.program_id(0) == 0)
    def _():
        carry_ref[...] = jnp.zeros_like(carry_ref)

    sel = sel_ref[...]
    lane = lax.broadcasted_iota(jnp.int32, sel.shape, 1)
    lane_f = lane.astype(F32)
    oh1 = lane_f == sel[:, SEL_E1:SEL_E1 + 1]
    oh2 = lane_f == sel[:, SEL_E2:SEL_E2 + 1]
    f1, f2 = oh1.astype(F32), oh2.astype(F32)
    ltri = ltri_ref[...]
    before1 = _dot(ltri, f1.astype(BF16))
    before2 = _dot(ltri, f2.astype(BF16))
    c1 = jnp.sum(f1, axis=0, keepdims=True)
    c2 = jnp.sum(f2, axis=0, keepdims=True)
    carry = carry_ref[...]
    r1 = jnp.sum(jnp.where(oh1, before1 + carry, 0.0), axis=1, keepdims=True)
    r2 = jnp.sum(jnp.where(oh2, before2 + carry + c1, 0.0), axis=1, keepdims=True)
    rank_ref[...] = jnp.where(lane == SEL_E1, r1, jnp.where(lane == SEL_E2, r2, 0.0)).astype(jnp.int32)
    total = carry + c1 + c2
    carry_ref[...] = total
    counts_ref[...] = total.astype(jnp.int32)


def moe_rank(sel, *, tm=512):
    tokens = sel.shape[0]
    tm = min(tm, tokens)
    ltri = (jnp.arange(tm)[:, None] > jnp.arange(tm)[None, :]).astype(BF16)
    return pl.pallas_call(
        _moe_rank_kernel,
        grid=(tokens // tm,),
        in_specs=[pl.BlockSpec((tm, LANES), lambda i: (i, 0)),
                  pl.BlockSpec((tm, tm), lambda i: (0, 0))],
        out_specs=[pl.BlockSpec((tm, LANES), lambda i: (i, 0)),
                   pl.BlockSpec((1, LANES), lambda i: (0, 0))],
        out_shape=[jax.ShapeDtypeStruct((tokens, LANES), jnp.int32),
                   jax.ShapeDtypeStruct((1, LANES), jnp.int32)],
        scratch_shapes=[pltpu.VMEM((1, LANES), F32)],
        compiler_params=_cparams(("arbitrary",)),
        name="moe_rank",
    )(sel, ltri)


def _row_copy(src_ref, src_row, dst_ref, dst_row, sem):
    return pltpu.make_async_copy(src_ref.at[pl.ds(src_row, 1)], dst_ref.at[pl.ds(dst_row, 1)], sem)


def _moe_expert_kernel(te_ref, tv_ref, nused_ref, src_ref, src_next_ref, dst_ref, h_ref, wg_ref, wu_ref, wd_ref,
                       z_ref, xbuf, ybuf, wgb, wub, wdb, sem_in, sem_out, *, tm):
    r = pl.program_id(0)
    last = nused_ref[0] - 1
    slot = r % 2
    unroll = DMA_ISSUE_UNROLL

    def gather(idx_ref, buf_slot):
        def issue(i, carry):
            _row_copy(h_ref, idx_ref[0, i], xbuf.at[buf_slot], i, sem_in.at[buf_slot]).start()
            return carry
        lax.fori_loop(0, tm, issue, 0, unroll=unroll)

    def wait_gather(buf_slot):
        pltpu.make_async_copy(h_ref.at[pl.ds(0, tm)], xbuf.at[buf_slot], sem_in.at[buf_slot]).wait()

    def scatter(n):
        def issue(i):
            _row_copy(ybuf.at[slot], i, z_ref, dst_ref[0, i], sem_out.at[slot]).start()

        def group(gidx, carry):
            for k in range(unroll):
                issue(gidx * unroll + k)
            return carry

        def single(i, carry):
            issue(i)
            return carry
        full = lax.shift_right_logical(n, int(math.log2(unroll)))
        lax.fori_loop(0, full, group, 0)
        lax.fori_loop(full * unroll, n, single, 0)

    def wait_scatter(buf_slot, n):
        def group(gidx, carry):
            pltpu.make_async_copy(ybuf.at[buf_slot, pl.ds(0, unroll)], z_ref.at[pl.ds(0, unroll)],
                                  sem_out.at[buf_slot]).wait()
            return carry

        def single(i, carry):
            _row_copy(ybuf.at[buf_slot], 0, z_ref, 0, sem_out.at[buf_slot]).wait()
            return carry
        full = lax.shift_right_logical(n, int(math.log2(unroll)))
        lax.fori_loop(0, full, group, 0)
        lax.fori_loop(full * unroll, n, single, 0)

    @pl.when(r == 0)
    def _():
        gather(src_ref, 0)

    @pl.when(r <= last)
    def _():
        @pl.when(r >= 2)
        def _():
            wait_scatter(slot, tv_ref[jnp.maximum(r - 2, 0)])

        wait_gather(slot)

        @pl.when(r < last)
        def _():
            gather(src_next_ref, 1 - slot)

        @pl.when((r == 0) | (te_ref[r] != te_ref[jnp.maximum(r - 1, 0)]))
        def _():
            wgb[...] = wg_ref[...].astype(BF16)
            wub[...] = wu_ref[...].astype(BF16)
            wdb[...] = wd_ref[...].astype(BF16)

        x = xbuf[slot].astype(BF16)
        gate_pre = _dot(x, wgb[...])
        hid = (gate_pre * jax.nn.sigmoid(gate_pre)) * _dot(x, wub[...])
        ybuf[slot] = _dot(hid.astype(BF16), wdb[...])
        scatter(tv_ref[r])

        @pl.when(r == last)
        def _():
            @pl.when(r >= 1)
            def _():
                wait_scatter(1 - slot, tv_ref[jnp.maximum(r - 1, 0)])
            wait_scatter(slot, tv_ref[r])


def moe_experts(h, tile_expert, tile_valid, n_used, src, dst, wg, wu, wd, layer, *, tm):
    tokens, d = h.shape
    de = wg.shape[-1]
    n_tiles = src.shape[0]
    idx_spec = lambda f: pl.BlockSpec((None, 1, tm), f, memory_space=pltpu.SMEM)
    grid_spec = pltpu.PrefetchScalarGridSpec(
        num_scalar_prefetch=3,
        grid=(n_tiles,),
        in_specs=[idx_spec(lambda r, te, tv, nu: (r, 0, 0)),
                  idx_spec(lambda r, te, tv, nu: (jnp.minimum(r + 1, n_tiles - 1), 0, 0)),
                  idx_spec(lambda r, te, tv, nu: (r, 0, 0)),
                  pl.BlockSpec(memory_space=pl.ANY),
                  pl.BlockSpec((None, None, d, de), lambda r, te, tv, nu: (layer, te[r], 0, 0)),
                  pl.BlockSpec((None, None, d, de), lambda r, te, tv, nu: (layer, te[r], 0, 0)),
                  pl.BlockSpec((None, None, de, d), lambda r, te, tv, nu: (layer, te[r], 0, 0))],
        out_specs=pl.BlockSpec(memory_space=pl.ANY),
        scratch_shapes=[pltpu.VMEM((2, tm, d), F32), pltpu.VMEM((2, tm, d), F32),
                        pltpu.VMEM((d, de), BF16), pltpu.VMEM((d, de), BF16), pltpu.VMEM((de, d), BF16),
                        pltpu.SemaphoreType.DMA((2,)), pltpu.SemaphoreType.DMA((2,))],
    )
    return pl.pallas_call(
        functools.partial(_moe_expert_kernel, tm=tm),
        grid_spec=grid_spec,
        out_shape=jax.ShapeDtypeStruct((2 * tokens, d), F32),
        compiler_params=_cparams(("arbitrary",)),
        name="moe_experts",
    )(tile_expert, tile_valid, n_used, src, src, dst, h, wg, wu, wd)


def _moe_combine_kernel(x_ref, sel_ref, z0_ref, z1_ref, o_ref):
    sel = sel_ref[...]
    o_ref[...] = x_ref[...] + sel[:, SEL_G1:SEL_G1 + 1] * z0_ref[...] + sel[:, SEL_G2:SEL_G2 + 1] * z1_ref[...]


def moe_combine(x, sel, z, *, tm=512):
    tokens, d = x.shape
    tm = min(tm, tokens)
    nt = tokens // tm
    return pl.pallas_call(
        _moe_combine_kernel,
        grid=(nt,),
        in_specs=[pl.BlockSpec((tm, d), lambda i: (i, 0)),
                  pl.BlockSpec((tm, LANES), lambda i: (i, 0)),
                  pl.BlockSpec((tm, d), lambda i: (i, 0)),
                  pl.BlockSpec((tm, d), lambda i: (nt + i, 0))],
        out_specs=pl.BlockSpec((tm, d), lambda i: (i, 0)),
        out_shape=jax.ShapeDtypeStruct((tokens, d), F32),
        compiler_params=_cparams(("parallel",)),
        name="moe_combine",
    )(x, sel, z, z)


def moe_block(x, norm_w, w_router, b_router, wg, wu, wd, layer, *, tm=MOE_ROW_TILE):
    tokens, d = x.shape
    n_exp = wg.shape[1]
    h, sel = moe_router(x, norm_w, w_router, b_router)
    rank, counts = moe_rank(sel)
    padded = (counts[0, :n_exp] + (tm - 1)) // tm * tm
    ends = jnp.cumsum(padded)
    starts = ends - padded
    experts = sel[:, SEL_E1:SEL_E2 + 1].astype(jnp.int32)
    dest = starts[experts] + rank[:, SEL_E1:SEL_E2 + 1]
    rows = 2 * tokens + n_exp * tm
    n_tiles = rows // tm
    tile_start = jnp.arange(n_tiles, dtype=jnp.int32) * tm
    tile_expert = jnp.minimum(jnp.sum(tile_start[:, None] >= ends[None, :], axis=1), n_exp - 1).astype(jnp.int32)
    n_used = (ends[-1] // tm).astype(jnp.int32).reshape(1)
    real_end = (starts + counts[0, :n_exp])[tile_expert]
    tile_valid = jnp.clip(real_end - tile_start, 0, tm).astype(jnp.int32)
    token = jnp.broadcast_to(jnp.arange(tokens, dtype=jnp.int32)[:, None], (tokens, 2))
    z_row = token + jnp.array([0, tokens], jnp.int32)[None, :]
    src = jnp.zeros((rows,), jnp.int32).at[dest.reshape(-1)].set(token.reshape(-1))
    dst = jnp.zeros((rows,), jnp.int32).at[dest.reshape(-1)].set(z_row.reshape(-1))
    z = moe_experts(h, tile_expert, tile_valid, n_used, src.reshape(n_tiles, 1, tm), dst.reshape(n_tiles, 1, tm),
                    wg, wu, wd, layer, tm=tm)
    return moe_combine(x, sel, z)


def _proj_weights(w_in_l, w_vres_l):
    d = w_in_l.shape[0]
    mla0 = RW_COLS + DF_COLS
    vres = jnp.zeros((d, RW_V_RANK), F32) if w_vres_l is None else w_vres_l
    part_f = [w_in_l[:, :RW_COLS],
              vres, jnp.zeros((d, LANES - RW_V_RANK), F32),
              w_in_l[:, mla0 + ML_Q_RANK + ML_KV_RANK:mla0 + ML_COLS], jnp.zeros((d, LANES - ML_ROPE), F32)]
    part_a = [w_in_l[:, RW_COLS:RW_COLS + DF_COLS],
              w_in_l[:, mla0 + ML_Q_RANK:mla0 + ML_Q_RANK + ML_KV_RANK],
              jnp.zeros((d, OFF_MQ - OFF_MKV - ML_KV_RANK), F32),
              w_in_l[:, mla0:mla0 + ML_Q_RANK]]
    return jnp.concatenate(part_f, axis=1).astype(BF16), jnp.concatenate(part_a, axis=1).astype(BF16)


def _pad_rows(w, rows, at=0):
    out = jnp.zeros((rows, w.shape[1]), w.dtype)
    return lax.dynamic_update_slice(out, w, (at, 0))


def kernel(x, mem, positions, rel_bias, final_norm, norm_mix, w_in, w_in_vres, w_out, tm_mu, tm_mu_vres, tm_w0, tm_w2, tm_a0, tm_a2, tm_v0, tm_v2, tm_g2, tm_k_k, tm_k_a, tm_r_k, tm_ln_w, tm_ln_b, da_lq1, da_lk1, da_lq2, da_lk2, da_subln, mla_q_norm, mla_wq_b, mla_kv_norm, mla_wkv_b, norm_cross, norm_mem, ca_wq, ca_wkv, ca_wo, norm_ffn, moe_w_group, moe_b_group, moe_w_expert, moe_b_expert, moe_w_gate, moe_w_up, moe_w_down):
    batch, seq, d = x.shape
    tokens = batch * seq
    depth = norm_mix.shape[0]
    xf = x.reshape(tokens, d)
    memf = mem.reshape(-1, d)
    positions = positions.astype(jnp.int32)

    head_of_lane = jnp.arange(RW_DIM) // RW_HEAD_DIM
    seg = (head_of_lane[:, None] == jnp.arange(LANES)[None, :]).astype(BF16)
    seg_t = seg.T
    row = lambda v: v.reshape(1, -1)

    v_first = None
    for l in range(depth):
        w_f, w_a = _proj_weights(w_in[l], None if l == 0 else w_in_vres[l - 1])
        proj = norm_matmul(xf, norm_mix[l], w_f, tn=PROJ_F_COLS // 2)
        pa = norm_matmul(xf, norm_mix[l], w_a, out_dtype=BF16, tn=PROJ_A_COLS // 2)

        mu = tm_mu[l]
        prm = dict(mu_r=row(mu[:RW_DIM]), mu_k=row(mu[RW_DIM:2 * RW_DIM]), mu_v=row(mu[2 * RW_DIM:3 * RW_DIM]),
                   mu_l=row(mu[3 * RW_DIM:]), w0=row(tm_w0[l]), a0=row(tm_a0[l]),
                   w2=_pad_rows(tm_w2[l], LANES, 0), a2=_pad_rows(tm_a2[l], LANES, RW_W_RANK),
                   g2=tm_g2[l].astype(BF16), k_k=row(tm_k_k[l]), k_a=row(tm_k_a[l]), r_k=row(tm_r_k[l]),
                   seg=seg, seg_t=seg_t)
        if l > 0:
            prm.update(mu_vr=jnp.pad(row(tm_mu_vres[l - 1]), ((0, 0), (0, LANES - RW_V_RANK))),
                       v0=row(tm_v0[l - 1]), v2=_pad_rows(tm_v2[l - 1], LANES, 0))
        r, lw, k, v, kap, beta, gate, bonus = rwkv_prep(proj, batch, v_first, prm)
        if l == 0:
            v_first = v
        o = rwkv_scan(r, lw, k, v, kap, beta, batch)
        y_a = rwkv_post(o, bonus, gate, tm_ln_w[l], tm_ln_b[l], seg, seg_t)

        lambda_init = 0.8 - 0.6 * math.exp(-0.3 * l)
        lam = (jnp.exp(jnp.sum(da_lq1[l] * da_lk1[l])) - jnp.exp(jnp.sum(da_lq2[l] * da_lk2[l])) + lambda_init)
        y_b = diff_attention(pa, positions, rel_bias, lam, lambda_init, da_subln[l])

        wq = mla_wq_b[l].reshape(ML_Q_RANK, ML_HEADS, ML_NOPE + ML_ROPE)
        wq_pe = jnp.pad(wq[:, :, ML_NOPE:], ((0, 0), (0, 0), (0, LANES - ML_ROPE)))
        wq_all = jnp.concatenate([wq[:, :, :ML_NOPE].reshape(ML_Q_RANK, -1),
                                  wq_pe.reshape(ML_Q_RANK, -1)], axis=1).astype(BF16)
        wkv = mla_wkv_b[l].reshape(ML_KV_RANK, ML_HEADS, ML_NOPE + ML_V)
        wkv = jnp.concatenate([wkv[:, :, :ML_NOPE].reshape(ML_KV_RANK, -1),
                               wkv[:, :, ML_NOPE:].reshape(ML_KV_RANK, -1)], axis=1).astype(BF16)
        qf, kf, v_mla = mla_prep(pa, proj, positions, mla_q_norm[l], mla_kv_norm[l], wq_all, wkv)
        y_c = mla_attention(qf, kf, v_mla, batch)

        wo = w_out[l].astype(BF16)
        xf = matmul_res([y_a, y_b, y_c],
                        [wo[:RW_DIM], wo[RW_DIM:RW_DIM + DF_DIM], wo[RW_DIM + DF_DIM:]], xf)

        kv_mem = norm_matmul(memf, norm_mem[l], ca_wkv[l].astype(BF16), out_dtype=BF16)
        xf = cross_block(xf, batch, norm_cross[l], ca_wq[l].astype(BF16), kv_mem, ca_wo[l].astype(BF16))

        w_router = jnp.concatenate(
            [moe_w_expert[l], moe_w_group[l], jnp.zeros((d, LANES - MOE_EXPERTS - MOE_GROUPS), F32)], axis=1)
        b_router = jnp.concatenate(
            [moe_b_expert[l], moe_b_group[l], jnp.zeros((LANES - MOE_EXPERTS - MOE_GROUPS,), F32)]).reshape(1, LANES)
        xf = moe_block(xf, norm_ffn[l], w_router, b_router, moe_w_gate, moe_w_up, moe_w_down, l)

    out = rmsnorm(xf, final_norm)
    return out.reshape(batch, seq, d)
```

```python
import functools
import math

import jax
import jax.numpy as jnp
from jax import lax
from jax.experimental import pallas as pl
from jax.experimental.pallas import tpu as pltpu

F32 = jnp.float32
BF16 = jnp.bfloat16

NORM_EPS = 1e-6
ROPE_THETA = 10000.0

RW_HEADS = 16
RW_HEAD_DIM = 64
RW_DIM = RW_HEADS * RW_HEAD_DIM
RW_W_RANK = 64
RW_A_RANK = 64
RW_G_RANK = 128
RW_V_RANK = 32
RW_LORA = RW_W_RANK + RW_A_RANK + RW_G_RANK
RW_LN_EPS = 64e-5
RW_COLS = 3 * RW_DIM + RW_LORA

DF_HEADS = 4
DF_HEAD_DIM = 64
DF_V_DIM = 2 * DF_HEAD_DIM
DF_QK = DF_HEADS * 2 * DF_HEAD_DIM
DF_DIM = DF_HEADS * DF_V_DIM
DF_COLS = 2 * DF_QK + DF_DIM
DF_SUBLN_EPS = 1e-5

ML_HEADS = 4
ML_Q_RANK = 384
ML_KV_RANK = 256
ML_NOPE = 128
ML_ROPE = 64
ML_V = 128
ML_DIM = ML_HEADS * ML_V
ML_COLS = ML_Q_RANK + ML_KV_RANK + ML_ROPE

REL_BUCKETS = 32
REL_MAX_DIST = 128

CA_HEADS = 4
CA_HEAD_DIM = 128
CA_DIM = CA_HEADS * CA_HEAD_DIM

MOE_GROUPS = 4
MOE_PER_GROUP = 8
MOE_EXPERTS = MOE_GROUPS * MOE_PER_GROUP

LANES = 128
SCAN_CHUNK = 64
SCAN_GROUP = 4
SCAN_BATCHES = 2
ATTN_TILE = 512
DMA_ISSUE_UNROLL = 8
MOE_ROW_TILE = 256
VMEM_LIMIT = 56 * 1024 * 1024
NEG_BIG = -1e30

LOG2E = 1.4426950408889634

OFF_R = 0
OFF_K = RW_DIM
OFF_V = 2 * RW_DIM
OFF_LORA = 3 * RW_DIM
OFF_VRES = OFF_LORA + RW_LORA
OFF_KPE = OFF_VRES + LANES
PROJ_F_COLS = OFF_KPE + LANES
OFF_DQ = 0
OFF_DK = OFF_DQ + DF_QK
OFF_DV = OFF_DK + DF_QK
OFF_MKV = OFF_DV + DF_DIM
OFF_MQ = 5 * ML_Q_RANK
PROJ_A_COLS = OFF_MQ + ML_Q_RANK


def _cparams(sem, vmem=VMEM_LIMIT, flags=None):
    return pltpu.CompilerParams(dimension_semantics=sem, vmem_limit_bytes=vmem, flags=flags)


def _dot(a, b):
    return jnp.dot(a, b, preferred_element_type=F32)


def _dot_t(a, b):
    return lax.dot_general(a, b, (((1,), (1,)), ((), ())), preferred_element_type=F32)


def _split3(x):
    hi = x.astype(BF16)
    r1 = x - hi.astype(F32)
    mid = r1.astype(BF16)
    lo = (r1 - mid.astype(F32)).astype(BF16)
    return hi, mid, lo


def _dot_rhs01(x, ones_bf16):
    hi = x.astype(BF16)
    lo = (x - hi.astype(F32)).astype(BF16)
    return _dot(hi, ones_bf16) + _dot(lo, ones_bf16)


def _dot_x3(a, b):
    ah = a.astype(BF16)
    al = (a - ah.astype(F32)).astype(BF16)
    bh = b.astype(BF16)
    bl = (b - bh.astype(F32)).astype(BF16)
    return _dot(ah, bh) + _dot(ah, bl) + _dot(al, bh)


def _rms(x, w, eps):
    ms = jnp.mean(x * x, axis=-1, keepdims=True)
    return x * lax.rsqrt(ms + eps) * w


def _norm_matmul_kernel(x_ref, nw_ref, w_ref, o_ref, xn_ref, *, eps):
    @pl.when(pl.program_id(1) == 0)
    def _():
        xn_ref[...] = _rms(x_ref[...], nw_ref[...], eps).astype(BF16)

    o_ref[...] = _dot(xn_ref[...], w_ref[...]).astype(o_ref.dtype)


def norm_matmul(x, nw, w, *, out_dtype=F32, tm=512, tn=None, eps=NORM_EPS):
    m, d = x.shape
    n = w.shape[1]
    tm = min(tm, m)
    tn = n if tn is None else tn
    return pl.pallas_call(
        functools.partial(_norm_matmul_kernel, eps=eps),
        grid=(m // tm, n // tn),
        in_specs=[pl.BlockSpec((tm, d), lambda i, j: (i, 0)),
                  pl.BlockSpec((1, d), lambda i, j: (0, 0)),
                  pl.BlockSpec((d, tn), lambda i, j: (0, j))],
        out_specs=pl.BlockSpec((tm, tn), lambda i, j: (i, j)),
        out_shape=jax.ShapeDtypeStruct((m, n), out_dtype),
        scratch_shapes=[pltpu.VMEM((tm, d), BF16)],
        compiler_params=_cparams(("parallel", "arbitrary")),
        name="norm_matmul",
    )(x, nw.reshape(1, d), w)


def _matmul_res_kernel(*refs, n_a):
    a_refs, w_refs = refs[:n_a], refs[n_a:2 * n_a]
    res_ref, o_ref = refs[2 * n_a], refs[2 * n_a + 1]
    acc = res_ref[...]
    for a_ref, w_ref in zip(a_refs, w_refs):
        acc = acc + _dot(a_ref[...].astype(BF16), w_ref[...])
    o_ref[...] = acc


def matmul_res(a_list, w_list, res, *, tm=1024, tn=1024):
    m, n = res.shape
    tm = min(tm, m)
    tn = min(tn, n)
    n_a = len(a_list)
    in_specs = ([pl.BlockSpec((tm, a.shape[1]), lambda i, j: (i, 0)) for a in a_list]
                + [pl.BlockSpec((w.shape[0], tn), lambda i, j: (0, j)) for w in w_list]
                + [pl.BlockSpec((tm, tn), lambda i, j: (i, j))])
    return pl.pallas_call(
        functools.partial(_matmul_res_kernel, n_a=n_a),
        grid=(m // tm, n // tn),
        in_specs=in_specs,
        out_specs=pl.BlockSpec((tm, tn), lambda i, j: (i, j)),
        out_shape=jax.ShapeDtypeStruct((m, n), F32),
        compiler_params=_cparams(("parallel", "arbitrary")),
        name="matmul_res",
    )(*a_list, *w_list, res)


def _softplus(z):
    return jnp.maximum(z, 0.0) + jnp.log(1.0 + jnp.exp(-jnp.abs(z)))


def _rwkv_prep_kernel(*refs, has_vres):
    if has_vres:
        (pr_ref, pk_ref, pv_ref, pl_ref, pvr_ref, vfirst_ref,
         mu_r, mu_k, mu_v, mu_l, mu_vr, w0, w2, a0, a2, g2, v0, v2,
         k_k, k_a, r_k, seg, seg_t,
         r_o, lw_o, k_o, v_o, kap_o, beta_o, g_o, bonus_o,
         last_r, last_k, last_v, last_l, last_vr) = refs
    else:
        (pr_ref, pk_ref, pv_ref, pl_ref,
         mu_r, mu_k, mu_v, mu_l, w0, w2, a0, a2, g2,
         k_k, k_a, r_k, seg, seg_t,
         r_o, lw_o, k_o, v_o, kap_o, beta_o, g_o, bonus_o,
         last_r, last_k, last_v, last_l) = refs
    t = pl.program_id(1)

    def shifted(p_ref, last_ref, mu_ref):
        p = p_ref[...]
        n = p.shape[0]
        carried = jnp.where(t == 0, 0.0, last_ref[0:1, :])
        row = lax.broadcasted_iota(jnp.int32, p.shape, 0)
        prev = jnp.where(row == 0, carried, pltpu.roll(p, 1, axis=0))
        last_ref[0:1, :] = p[n - 1:n, :]
        return p + mu_ref[...] * (prev - p)

    r = shifted(pr_ref, last_r, mu_r)
    k = shifted(pk_ref, last_k, mu_k)
    v = shifted(pv_ref, last_v, mu_v)
    lora = shifted(pl_ref, last_l, mu_l)
    wl = lora[:, :LANES]
    gl = lora[:, LANES:]

    lane = lax.broadcasted_iota(jnp.int32, wl.shape, 1)
    wl_t = jnp.where(lane < RW_W_RANK, jnp.tanh(wl), 0.0)
    al = jnp.where(lane >= RW_W_RANK, wl, 0.0)
    w_log = -_softplus(-(w0[...] + _dot_x3(wl_t, w2[...]))) - 0.5
    lw_o[...] = -jnp.exp(w_log)
    a = jax.nn.sigmoid(a0[...] + _dot_x3(al, a2[...]))
    g_o[...] = _dot(jax.nn.sigmoid(gl).astype(BF16), g2[...]).astype(g_o.dtype)

    segm, segm_t = seg[...], seg_t[...]

    def head_sum(x):
        return _dot_rhs01(_dot_rhs01(x, segm), segm_t)

    kk = k * k_k[...]
    kk = kk / jnp.maximum(jnp.sqrt(head_sum(kk * kk)), 1e-12)
    k = k * (1.0 + (a - 1.0) * k_a[...])
    if has_vres:
        vr = shifted(pvr_ref, last_vr, mu_vr)
        mix = jax.nn.sigmoid(v0[...] + _dot_x3(vr, v2[...]))
        v = v + (vfirst_ref[...] - v) * mix
    r_o[...] = r.astype(r_o.dtype)
    k_o[...] = k.astype(k_o.dtype)
    v_o[...] = v.astype(v_o.dtype)
    kap_o[...] = kk.astype(kap_o.dtype)
    beta_o[...] = (kk * a).astype(beta_o.dtype)
    bonus_o[...] = (head_sum(r * k * r_k[...]) * v).astype(bonus_o.dtype)


def rwkv_prep(proj, batch, vfirst, prm, *, tt=256):
    tokens = proj.shape[0]
    seq = tokens // batch
    tt = min(tt, seq)
    nt = seq // tt
    has_vres = vfirst is not None
    d = RW_DIM

    def rows(width, col):
        return pl.BlockSpec((tt, width), lambda b, t, col=col: (b * nt + t, col))

    def full(shape):
        return pl.BlockSpec(shape, lambda b, t: (0, 0))

    in_specs = [rows(d, OFF_R // d), rows(d, OFF_K // d), rows(d, OFF_V // d),
                rows(RW_LORA, OFF_LORA // RW_LORA)]
    args = [proj, proj, proj, proj]
    if has_vres:
        in_specs += [rows(LANES, OFF_VRES // LANES), rows(d, 0)]
        args += [proj, vfirst]
    names = ["mu_r", "mu_k", "mu_v", "mu_l"] + (["mu_vr"] if has_vres else []) + ["w0", "w2", "a0", "a2", "g2"]
    names += (["v0", "v2"] if has_vres else []) + ["k_k", "k_a", "r_k", "seg", "seg_t"]
    for nm in names:
        in_specs.append(full(prm[nm].shape))
        args.append(prm[nm])
    out_spec = pl.BlockSpec((tt, d), lambda b, t: (b * nt + t, 0))
    scratch = [pltpu.VMEM((8, d), F32)] * 3 + [pltpu.VMEM((8, RW_LORA), F32)]
    if has_vres:
        scratch.append(pltpu.VMEM((8, LANES), F32))
    return pl.pallas_call(
        functools.partial(_rwkv_prep_kernel, has_vres=has_vres),
        grid=(batch, nt),
        in_specs=in_specs,
        out_specs=[out_spec] * 8,
        out_shape=[jax.ShapeDtypeStruct((tokens, d), F32 if i == 1 else BF16) for i in range(8)],
        scratch_shapes=scratch,
        compiler_params=_cparams(("arbitrary", "arbitrary")),
        name="rwkv_prep",
    )(*args)


def _rwkv_scan_kernel(r_ref, lw_ref, k_ref, v_ref, kap_ref, beta_ref, tril_ref, bmask_ref,
                      o_ref, ht_ref):
    @pl.when(pl.program_id(1) == 0)
    def _():
        ht_ref[...] = jnp.zeros_like(ht_ref)

    n_batch, c, d = lw_ref.shape
    w = ht_ref.shape[1]
    g = w // RW_HEAD_DIM
    bmask = bmask_ref[...]
    bmask_b = bmask.astype(BF16)
    tril3 = tril_ref[...]
    t_idx = lax.broadcasted_iota(jnp.int32, (c, w), 0)
    s_idx = lax.broadcasted_iota(jnp.int32, (c, w), 1) % c
    strict = t_idx > s_idx
    incl = t_idx >= s_idx
    n_sq = int(math.log2(c))

    def stack(x):
        return jnp.concatenate([x.astype(BF16)] * g, axis=0) * bmask_b

    sls = [(bi, slice(None), slice(lo, lo + w)) for bi in range(n_batch) for lo in range(0, d, w)]
    groups = range(len(sls))
    lw = [lw_ref[sl] for sl in sls]
    cum = [_dot(tril3, jnp.concatenate(_split3(x), axis=0)) for x in lw]
    total = [x[c - 1:c, :] for x in cum]
    ar = [jnp.concatenate([-kap_ref[sls[gi]] * jnp.exp(cum[gi] - lw[gi]), r_ref[sls[gi]] * jnp.exp(cum[gi])],
                          axis=0).astype(BF16) for gi in groups]
    p_inv = [jnp.exp(-x) for x in cum]
    b_s = [stack(beta_ref[sls[gi]] * p_inv[gi]) for gi in groups]
    k_s = [stack(k_ref[sls[gi]] * p_inv[gi]) for gi in groups]
    v_n = [v_ref[sl] for sl in sls]
    v_s = [stack(x) for x in v_n]

    arb = [_dot_t(ar[gi], b_s[gi]) for gi in groups]
    ark = [_dot_t(ar[gi], k_s[gi]) for gi in groups]
    ab = [jnp.where(strict, m[:c], 0.0) for m in arb]
    rb = [jnp.where(incl, m[c:], 0.0).astype(BF16) for m in arb]
    akrk = [jnp.concatenate([jnp.where(strict, m[:c], 0.0), jnp.where(incl, m[c:], 0.0)], axis=0).astype(BF16)
            for m in ark]

    ht = [ht_ref[gi] for gi in groups]
    base = [_dot_t(ar[gi], ht[gi].astype(BF16)) + _dot(akrk[gi], v_s[gi]) for gi in groups]
    x = [m[:c] for m in base]
    lp = ab
    for i in range(n_sq):
        lpb = [m.astype(BF16) for m in lp]
        x = [x[gi] + _dot(lpb[gi], stack(x[gi])) for gi in groups]
        if i < n_sq - 1:
            lp = [_dot(lpb[gi], stack(lp[gi])) for gi in groups]
    for gi in groups:
        o_ref[sls[gi]] = base[gi][c:] + _dot(rb[gi], stack(x[gi]))

    for gi in groups:
        p_rem = jnp.exp(total[gi] - cum[gi])
        z = jnp.concatenate([beta_ref[sls[gi]] * p_rem, k_ref[sls[gi]] * p_rem], axis=0).astype(BF16)
        uv_t = jnp.concatenate([x[gi], v_n[gi].astype(F32)], axis=0).T.astype(BF16)
        ht_ref[gi] = ht[gi] * jnp.exp(total[gi]) + bmask * _dot(uv_t, z)


def rwkv_scan(r, lw, k, v, kap, beta, batch):
    tokens, d = r.shape
    seq = tokens // batch
    c = min(SCAN_CHUNK, seq)
    nc = seq // c
    gw = SCAN_GROUP * RW_HEAD_DIM
    rr = SCAN_GROUP * c
    assert c == RW_HEAD_DIM, "the stacking mask doubles as the head-block mask of the state"
    bb = math.gcd(batch, SCAN_BATCHES)
    tril = jnp.tile((jnp.arange(c)[:, None] >= jnp.arange(c)[None, :]).astype(BF16), (1, 3))
    bmask = (jnp.arange(rr)[:, None] // c == jnp.arange(gw)[None, :] // RW_HEAD_DIM).astype(F32)
    blk = pl.BlockSpec((bb, c, d), lambda b, i: (b, i, 0))
    as3d = lambda a: a.reshape(batch, seq, d)
    out = pl.pallas_call(
        _rwkv_scan_kernel,
        grid=(batch // bb, nc),
        in_specs=[blk] * 6 + [pl.BlockSpec((c, 3 * c), lambda b, i: (0, 0)),
                              pl.BlockSpec((rr, gw), lambda b, i: (0, 0))],
        out_specs=blk,
        out_shape=jax.ShapeDtypeStruct((batch, seq, d), F32),
        scratch_shapes=[pltpu.VMEM((bb * (d // gw), gw, gw), F32)],
        compiler_params=_cparams(("arbitrary", "arbitrary")),
        name="rwkv_scan",
    )(as3d(r), as3d(lw), as3d(k), as3d(v), as3d(kap), as3d(beta), tril, bmask)
    return out.reshape(tokens, d)


def _rwkv_post_kernel(o_ref, bonus_ref, g_ref, lnw_ref, lnb_ref, seg, seg_t, y_ref):
    segm, segm_t = seg[...], seg_t[...]

    def head_mean(x):
        return _dot_rhs01(_dot_rhs01(x, segm), segm_t) * (1.0 / RW_HEAD_DIM)

    o = o_ref[...]
    dlt = o - head_mean(o)
    var = head_mean(dlt * dlt)
    y = dlt * lax.rsqrt(var + RW_LN_EPS) * lnw_ref[...] + lnb_ref[...]
    y_ref[...] = ((y + bonus_ref[...]) * g_ref[...]).astype(y_ref.dtype)


def rwkv_post(o, bonus, g, ln_w, ln_b, seg, seg_t, *, tm=512):
    tokens, d = o.shape
    tm = min(tm, tokens)
    blk = pl.BlockSpec((tm, d), lambda i: (i, 0))
    vec = pl.BlockSpec((1, d), lambda i: (0, 0))
    return pl.pallas_call(
        _rwkv_post_kernel,
        grid=(tokens // tm,),
        in_specs=[blk, blk, blk, vec, vec,
                  pl.BlockSpec(seg.shape, lambda i: (0, 0)), pl.BlockSpec(seg_t.shape, lambda i: (0, 0))],
        out_specs=blk,
        out_shape=jax.ShapeDtypeStruct((tokens, d), BF16),
        compiler_params=_cparams(("parallel",)),
        name="rwkv_post",
    )(o, bonus, g, ln_w.reshape(1, d), ln_b.reshape(1, d), seg, seg_t)


def _t5_thresholds():
    max_exact = REL_BUCKETS // 2
    thr = list(range(1, max_exact))
    n = max_exact
    for bucket in range(max_exact, REL_BUCKETS):
        while True:
            large = max_exact + int(math.log(max(n, max_exact) / max_exact)
                                    / math.log(REL_MAX_DIST / max_exact) * (REL_BUCKETS - max_exact))
            if min(large, REL_BUCKETS - 1) >= bucket:
                break
            n += 1
        thr.append(n)
    return thr


T5_THRESHOLDS = _t5_thresholds()
T5_FAR = T5_THRESHOLDS[-1]


def _softmax_tiles(s_list, c_list, states, vt_list):
    stats = []
    for s_t, c, (m_old, l_old, _) in zip(s_list, c_list, states):
        m_new = jnp.maximum(m_old, jnp.max(s_t, axis=0, keepdims=True) + c)
        alpha = jnp.exp2(m_old - m_new)
        p_t = jnp.exp2(s_t - (m_new - c))
        stats.append((m_new, alpha, alpha * l_old + jnp.sum(p_t, axis=0, keepdims=True), p_t.astype(BF16)))
    return tuple((m_new, l_new, alpha * acc + _dot(vt, p_t))
                 for (m_new, alpha, l_new, p_t), (_, _, acc), vt in zip(stats, states, vt_list))


def _transpose_into(vt_ref, v_ref, chunk):
    seq = v_ref.shape[0]
    for c in range(seq // chunk):
        vt_ref[:, c * chunk:(c + 1) * chunk] = v_ref[c * chunk:(c + 1) * chunk, :].astype(F32).T.astype(BF16)


def _diff_attn_kernel(qfirst_ref, klast_ref, q_ref, k_ref, v_ref, qpos_ref, kpos_ref, subln_ref, table_ref, lam_ref,
                      o_ref, vt_ref, *, tq, tk, scale2, out_scale):
    b, i = pl.program_id(0), pl.program_id(1)
    nq = pl.num_programs(1)
    seq = k_ref.shape[0]
    nk = seq // tk
    w = DF_V_DIM

    @pl.when(i == 0)
    def _():
        _transpose_into(vt_ref, v_ref, tk)

    n_tiles = (i * tq + tq - 1) // tk + 1
    qf = qfirst_ref[b * nq + i]
    n_far = lax.while_loop(
        lambda j: (j * tk + tk - 1 <= i * tq) & (qf - klast_ref[b * nk + jnp.minimum(j, nk - 1)] >= T5_FAR),
        lambda j: j + 1, jnp.int32(0))

    dist = lax.broadcasted_iota(jnp.int32, (1, LANES), 1)
    qpos = qpos_ref[...]
    q_idx = i * tq + lax.broadcasted_iota(jnp.int32, (tk, tq), 1)
    k_off = lax.broadcasted_iota(jnp.int32, (tk, tq), 0)
    lane = lax.broadcasted_iota(jnp.int32, (tq, w), 1)

    bias_rows, c_far, qm = [], [], []
    for h in range(DF_HEADS):
        bias_vec = jnp.full((1, LANES), table_ref[h], F32)
        for bucket, thr in enumerate(T5_THRESHOLDS, start=1):
            bias_vec = jnp.where(dist >= thr, table_ref[bucket * DF_HEADS + h], bias_vec)
        bias_rows.append(jnp.broadcast_to(bias_vec * LOG2E, (tk, LANES)))
        c_far.append(table_ref[(REL_BUCKETS - 1) * DF_HEADS + h] * LOG2E)
        qh = q_ref[:, h * w:(h + 1) * w].astype(F32) * scale2
        qm.append([jnp.where((lane >= mi * DF_HEAD_DIM) & (lane < (mi + 1) * DF_HEAD_DIM), qh, 0.0).astype(BF16)
                   for mi in range(2)])

    def tiles(j, h):
        off = pl.multiple_of(j * tk, tk)
        return k_ref[pl.ds(off, tk), h * w:(h + 1) * w], vt_ref[h * w:(h + 1) * w, pl.ds(off, tk)], off

    chains = [(h, mi) for h in range(DF_HEADS) for mi in range(2)]

    def far_body(j, st):
        kv = [tiles(j, h) for h in range(DF_HEADS)]
        s = [_dot_t(kv[h][0], qm[h][mi]) for h, mi in chains]
        return _softmax_tiles(s, [c_far[h] for h, _ in chains], st, [kv[h][1] for h, _ in chains])

    def near_body(j, st):
        off = pl.multiple_of(j * tk, tk)
        n = jnp.clip(qpos - kpos_ref[pl.ds(off, tk), :], 0, LANES - 1)
        keep = q_idx >= off + k_off
        kv = [tiles(j, h) for h in range(DF_HEADS)]
        bias = [jnp.concatenate(
            [jnp.take_along_axis(bias_rows[h], n[:, cb * LANES:(cb + 1) * LANES], axis=1)
             for cb in range(tq // LANES)], axis=1) for h in range(DF_HEADS)]
        s = [jnp.where(keep, _dot_t(kv[h][0], qm[h][mi]) + bias[h], NEG_BIG) for h, mi in chains]
        return _softmax_tiles(s, [0.0] * len(chains), st, [kv[h][1] for h, _ in chains])

    init = tuple((jnp.full((1, tq), NEG_BIG, F32), jnp.zeros((1, tq), F32), jnp.zeros((w, tq), F32))
                 for _ in range(2 * DF_HEADS))
    st = lax.fori_loop(0, n_far, far_body, init)
    st = lax.fori_loop(n_far, n_tiles, near_body, st)
    for h in range(DF_HEADS):
        s0, s1 = st[2 * h], st[2 * h + 1]
        d_t = s0[2] / s0[1] - lam_ref[0] * (s1[2] / s1[1])
        ms = jnp.mean(d_t * d_t, axis=0, keepdims=True)
        y_t = d_t * lax.rsqrt(ms + DF_SUBLN_EPS) * (subln_ref[...] * out_scale)
        o_ref[:, h * w:(h + 1) * w] = y_t.T.astype(o_ref.dtype)


def diff_attention(pa, positions, rel_bias, lam, lambda_init, subln_w, *, tq=ATTN_TILE):
    batch, seq = positions.shape
    tokens = batch * seq
    tq = min(tq, seq)
    tk = tq
    nq, nk = seq // tq, seq // tk
    qfirst = positions[:, ::tq].reshape(-1)
    klast = positions[:, tk - 1::tk].reshape(-1)
    qpos = positions.reshape(batch, 1, seq)
    kpos = positions.reshape(batch, seq, 1)
    wd = DF_DIM
    grid_spec = pltpu.PrefetchScalarGridSpec(
        num_scalar_prefetch=2,
        grid=(batch, nq),
        in_specs=[pl.BlockSpec((tq, wd), lambda b, i, *_: (b * nq + i, OFF_DQ // wd)),
                  pl.BlockSpec((seq, wd), lambda b, i, *_: (b, OFF_DK // wd)),
                  pl.BlockSpec((seq, wd), lambda b, i, *_: (b, OFF_DV // wd)),
                  pl.BlockSpec((None, 1, tq), lambda b, i, *_: (b, 0, i)),
                  pl.BlockSpec((None, seq, 1), lambda b, i, *_: (b, 0, 0)),
                  pl.BlockSpec((DF_V_DIM, 1), lambda b, i, *_: (0, 0)),
                  pl.BlockSpec(memory_space=pltpu.SMEM),
                  pl.BlockSpec(memory_space=pltpu.SMEM)],
        out_specs=pl.BlockSpec((tq, wd), lambda b, i, *_: (b * nq + i, 0)),
        scratch_shapes=[pltpu.VMEM((wd, seq), BF16)],
    )
    return pl.pallas_call(
        functools.partial(_diff_attn_kernel, tq=tq, tk=tk, scale2=DF_HEAD_DIM ** -0.5 * LOG2E,
                          out_scale=1.0 - lambda_init),
        grid_spec=grid_spec,
        out_shape=jax.ShapeDtypeStruct((tokens, DF_DIM), BF16),
        compiler_params=_cparams(("arbitrary", "arbitrary")),
        name="diff_attention",
    )(qfirst, klast, pa, pa, pa, qpos, kpos, subln_w.reshape(DF_V_DIM, 1), rel_bias.reshape(-1), lam.reshape(1))


ML_QK_PAD = 2 * LANES


def _rope_block(x, cos, sin):
    half = ML_ROPE // 2
    lane = lax.broadcasted_iota(jnp.int32, x.shape, 1)
    rot = jnp.where(lane < half, -pltpu.roll(x, LANES - half, axis=1),
                    jnp.where(lane < ML_ROPE, pltpu.roll(x, half, axis=1), 0.0))
    return x * cos + rot * sin


def _mla_prep_kernel(mq_ref, mkv_ref, kpe_ref, pos_ref, qn_w, kvn_w, wq_ref, wkv_ref, freq_ref,
                     qf_o, kf_o, v_o, *, qscale):
    ang = pos_ref[...].astype(F32) * freq_ref[...]
    cos, sin = jnp.cos(ang), jnp.sin(ang)
    qc = _rms(mq_ref[...].astype(F32), qn_w[...], NORM_EPS).astype(BF16)
    q_all = _dot(qc, wq_ref[...]) * qscale
    kvc = _rms(mkv_ref[...].astype(F32), kvn_w[...], NORM_EPS).astype(BF16)
    kvb = _dot(kvc, wkv_ref[...])
    kpe = _rope_block(kpe_ref[...], cos, sin).astype(BF16)
    nope_w = ML_HEADS * ML_NOPE
    for h in range(ML_HEADS):
        lo = h * ML_QK_PAD
        qf_o[:, lo:lo + LANES] = q_all[:, h * LANES:(h + 1) * LANES].astype(BF16)
        qf_o[:, lo + LANES:lo + 2 * LANES] = _rope_block(
            q_all[:, nope_w + h * LANES:nope_w + (h + 1) * LANES], cos, sin).astype(BF16)
        kf_o[:, lo:lo + LANES] = kvb[:, h * LANES:(h + 1) * LANES].astype(BF16)
        kf_o[:, lo + LANES:lo + 2 * LANES] = kpe
    v_o[...] = kvb[:, nope_w:].astype(BF16)


def mla_prep(pa, pf, positions, q_norm, kv_norm, wq_all, wkv, *, tm=512):
    tokens = pa.shape[0]
    tm = min(tm, tokens)
    half = ML_ROPE // 2
    inv_freq = ROPE_THETA ** (-jnp.arange(half, dtype=F32) / half)
    freq = jnp.concatenate([inv_freq, inv_freq, jnp.zeros((LANES - ML_ROPE,), F32)]).reshape(1, LANES)

    def full(a):
        return pl.BlockSpec(a.shape, lambda i: (0, 0))

    qn_w = q_norm.reshape(1, -1)
    kvn_w = kv_norm.reshape(1, -1)
    wide = ML_HEADS * ML_QK_PAD
    return pl.pallas_call(
        functools.partial(_mla_prep_kernel, qscale=(ML_NOPE + ML_ROPE) ** -0.5 * LOG2E),
        grid=(tokens // tm,),
        in_specs=[pl.BlockSpec((tm, ML_Q_RANK), lambda i: (i, OFF_MQ // ML_Q_RANK)),
                  pl.BlockSpec((tm, ML_KV_RANK), lambda i: (i, OFF_MKV // ML_KV_RANK)),
                  pl.BlockSpec((tm, LANES), lambda i: (i, OFF_KPE // LANES)),
                  pl.BlockSpec((tm, 1), lambda i: (i, 0)),
                  full(qn_w), full(kvn_w), full(wq_all), full(wkv), full(freq)],
        out_specs=[pl.BlockSpec((tm, wide), lambda i: (i, 0)),
                   pl.BlockSpec((tm, wide), lambda i: (i, 0)),
                   pl.BlockSpec((tm, ML_DIM), lambda i: (i, 0))],
        out_shape=[jax.ShapeDtypeStruct((tokens, wide), BF16),
                   jax.ShapeDtypeStruct((tokens, wide), BF16),
                   jax.ShapeDtypeStruct((tokens, ML_DIM), BF16)],
        compiler_params=_cparams(("parallel",)),
        name="mla_prep",
    )(pa, pa, pf, positions.reshape(tokens, 1), qn_w, kvn_w, wq_all, wkv, freq)


def _mla_attn_kernel(q_ref, k_ref, v_ref, o_ref, vt_ref, *, tq, tk):
    i = pl.program_id(1)
    wq = ML_QK_PAD

    @pl.when(i == 0)
    def _():
        _transpose_into(vt_ref, v_ref, tk)

    n_tiles = (i * tq + tq - 1) // tk + 1
    n_full = (i * tq + 1) // tk
    q_idx = i * tq + lax.broadcasted_iota(jnp.int32, (tk, tq), 1)
    k_off = lax.broadcasted_iota(jnp.int32, (tk, tq), 0)
    qh = [q_ref[:, h * wq:(h + 1) * wq] for h in range(ML_HEADS)]

    def tiles(j, h):
        off = pl.multiple_of(j * tk, tk)
        return (k_ref[pl.ds(off, tk), h * wq:(h + 1) * wq],
                vt_ref[h * ML_V:(h + 1) * ML_V, pl.ds(off, tk)], off)

    heads = range(ML_HEADS)

    def full_body(j, st):
        kv = [tiles(j, h) for h in heads]
        s = [_dot_t(kv[h][0], qh[h]) for h in heads]
        return _softmax_tiles(s, [0.0] * ML_HEADS, st, [kv[h][1] for h in heads])

    def diag_body(j, st):
        kv = [tiles(j, h) for h in heads]
        keep = q_idx >= kv[0][2] + k_off
        s = [jnp.where(keep, _dot_t(kv[h][0], qh[h]), NEG_BIG) for h in heads]
        return _softmax_tiles(s, [0.0] * ML_HEADS, st, [kv[h][1] for h in heads])

    st = tuple((jnp.full((1, tq), NEG_BIG, F32), jnp.zeros((1, tq), F32), jnp.zeros((ML_V, tq), F32))
               for _ in range(ML_HEADS))
    st = lax.fori_loop(0, n_full, full_body, st)
    st = lax.fori_loop(n_full, n_tiles, diag_body, st)
    for h in range(ML_HEADS):
        o_ref[:, h * ML_V:(h + 1) * ML_V] = (st[h][2] / st[h][1]).T.astype(o_ref.dtype)


def mla_attention(qf, kf, v, batch, *, tq=ATTN_TILE):
    tokens = qf.shape[0]
    seq = tokens // batch
    tq = min(tq, seq)
    tk = tq
    nq = seq // tq
    wide = qf.shape[1]
    return pl.pallas_call(
        functools.partial(_mla_attn_kernel, tq=tq, tk=tk),
        grid=(batch, nq),
        in_specs=[pl.BlockSpec((tq, wide), lambda b, i: (b * nq + i, 0)),
                  pl.BlockSpec((seq, wide), lambda b, i: (b, 0)),
                  pl.BlockSpec((seq, ML_DIM), lambda b, i: (b, 0))],
        out_specs=pl.BlockSpec((tq, ML_DIM), lambda b, i: (b * nq + i, 0)),
        out_shape=jax.ShapeDtypeStruct((tokens, ML_DIM), BF16),
        scratch_shapes=[pltpu.VMEM((ML_DIM, seq), BF16)],
        compiler_params=_cparams(("arbitrary", "arbitrary")),
        name="mla_attention",
    )(qf, kf, v)


def _cross_kernel(x_ref, nw_ref, wq_ref, kv_ref, wo_ref, o_ref):
    x = x_ref[...]
    q = _dot(_rms(x, nw_ref[...], NORM_EPS).astype(BF16), wq_ref[...])
    kv = kv_ref[...]
    scale = CA_HEAD_DIM ** -0.5
    outs = []
    for hh in range(CA_HEADS):
        sl = slice(hh * CA_HEAD_DIM, (hh + 1) * CA_HEAD_DIM)
        s = _dot_t(q[:, sl].astype(BF16), kv[:, sl]) * scale
        p = jnp.exp(s - jnp.max(s, axis=-1, keepdims=True))
        p = p / jnp.sum(p, axis=-1, keepdims=True)
        outs.append(_dot(p.astype(BF16), kv[:, CA_DIM + hh * CA_HEAD_DIM:CA_DIM + (hh + 1) * CA_HEAD_DIM]))
    o = jnp.concatenate(outs, axis=1).astype(BF16)
    o_ref[...] = x + _dot(o, wo_ref[...])


def cross_block(x, batch, norm_w, wq, kv, wo, *, tq=512):
    tokens, d = x.shape
    seq = tokens // batch
    tq = min(tq, seq)
    nq = seq // tq
    mem_len = kv.shape[0] // batch
    return pl.pallas_call(
        _cross_kernel,
        grid=(batch, nq),
        in_specs=[pl.BlockSpec((tq, d), lambda b, i: (b * nq + i, 0)),
                  pl.BlockSpec((1, d), lambda b, i: (0, 0)),
                  pl.BlockSpec(wq.shape, lambda b, i: (0, 0)),
                  pl.BlockSpec((mem_len, 2 * CA_DIM), lambda b, i: (b, 0)),
                  pl.BlockSpec(wo.shape, lambda b, i: (0, 0))],
        out_specs=pl.BlockSpec((tq, d), lambda b, i: (b * nq + i, 0)),
        out_shape=jax.ShapeDtypeStruct((tokens, d), F32),
        compiler_params=_cparams(("parallel", "parallel")),
        name="cross_block",
    )(x, norm_w.reshape(1, d), wq, kv, wo)


def _router_kernel(x_ref, nw_ref, wr_ref, br_ref, h_ref, sel_ref):
    h = _rms(x_ref[...], nw_ref[...], NORM_EPS)
    h_ref[...] = h.astype(h_ref.dtype)
    logits = _dot_x3(h, wr_ref[...])
    biased = logits + br_ref[...]
    lane = lax.broadcasted_iota(jnp.int32, logits.shape, 1)
    big = jnp.int32(LANES)

    def first_argmax(vals):
        mx = jnp.max(vals, axis=-1, keepdims=True)
        return jnp.min(jnp.where(vals == mx, lane, big), axis=-1, keepdims=True)

    def pick(vals, idx):
        return jnp.sum(jnp.where(lane == idx, vals, 0.0), axis=-1, keepdims=True)

    is_group = (lane >= MOE_EXPERTS) & (lane < MOE_EXPERTS + MOE_GROUPS)
    gl = jnp.where(is_group, logits, NEG_BIG)
    ge = jnp.exp(gl - jnp.max(gl, axis=-1, keepdims=True))
    gp = ge / jnp.sum(ge, axis=-1, keepdims=True)
    g_lane = first_argmax(jnp.where(is_group, biased, NEG_BIG))
    p_group = pick(gp, g_lane)
    lo = (g_lane - MOE_EXPERTS) * MOE_PER_GROUP
    in_group = (lane >= lo) & (lane < lo + MOE_PER_GROUP)
    eb = jnp.where(in_group, biased, NEG_BIG)
    i1 = first_argmax(eb)
    i2 = first_argmax(jnp.where(lane == i1, NEG_BIG, eb))
    l1, l2 = pick(logits, i1), pick(logits, i2)
    mx = jnp.maximum(l1, l2)
    e1, e2 = jnp.exp(l1 - mx), jnp.exp(l2 - mx)
    w1, w2 = e1 / (e1 + e2), e2 / (e1 + e2)
    sel_ref[...] = jnp.where(lane == SEL_E1, i1.astype(F32),
                             jnp.where(lane == SEL_E2, i2.astype(F32),
                                       jnp.where(lane == SEL_G1, w1 * p_group,
                                                 jnp.where(lane == SEL_G2, w2 * p_group, 0.0))))


SEL_E1, SEL_E2, SEL_G1, SEL_G2 = 0, 1, 2, 3


def moe_router(x, norm_w, w_router, b_router, *, tm=512):
    tokens, d = x.shape
    tm = min(tm, tokens)
    return pl.pallas_call(
        _router_kernel,
        grid=(tokens // tm,),
        in_specs=[pl.BlockSpec((tm, d), lambda i: (i, 0)),
                  pl.BlockSpec((1, d), lambda i: (0, 0)),
                  pl.BlockSpec((d, LANES), lambda i: (0, 0)),
                  pl.BlockSpec((1, LANES), lambda i: (0, 0))],
        out_specs=[pl.BlockSpec((tm, d), lambda i: (i, 0)),
                   pl.BlockSpec((tm, LANES), lambda i: (i, 0))],
        out_shape=[jax.ShapeDtypeStruct((tokens, d), F32),
                   jax.ShapeDtypeStruct((tokens, LANES), F32)],
        compiler_params=_cparams(("parallel",)),
        name="moe_router",
    )(x, norm_w.reshape(1, d), w_router, b_router)


def _moe_rank_kernel(sel_ref, ltri_ref, rank_ref, counts_ref, carry_ref):
    @pl.when(pl.program_id(0) == 0)
    def _():
        carry_ref[...] = jnp.zeros_like(carry_ref)

    sel = sel_ref[...]
    lane = lax.broadcasted_iota(jnp.int32, sel.shape, 1)
    lane_f = lane.astype(F32)
    oh1 = lane_f == sel[:, SEL_E1:SEL_E1 + 1]
    oh2 = lane_f == sel[:, SEL_E2:SEL_E2 + 1]
    f1, f2 = oh1.astype(F32), oh2.astype(F32)
    ltri = ltri_ref[...]
    before1 = _dot(ltri, f1.astype(BF16))
    before2 = _dot(ltri, f2.astype(BF16))
    c1 = jnp.sum(f1, axis=0, keepdims=True)
    c2 = jnp.sum(f2, axis=0, keepdims=True)
    carry = carry_ref[...]
    r1 = jnp.sum(jnp.where(oh1, before1 + carry, 0.0), axis=1, keepdims=True)
    r2 = jnp.sum(jnp.where(oh2, before2 + carry + c1, 0.0), axis=1, keepdims=True)
    rank_ref[...] = jnp.where(lane == SEL_E1, r1, jnp.where(lane == SEL_E2, r2, 0.0)).astype(jnp.int32)
    total = carry + c1 + c2
    carry_ref[...] = total
    counts_ref[...] = total.astype(jnp.int32)


def moe_rank(sel, *, tm=512):
    tokens = sel.shape[0]
    tm = min(tm, tokens)
    ltri = (jnp.arange(tm)[:, None] > jnp.arange(tm)[None, :]).astype(BF16)
    return pl.pallas_call(
        _moe_rank_kernel,
        grid=(tokens // tm,),
        in_specs=[pl.BlockSpec((tm, LANES), lambda i: (i, 0)),
                  pl.BlockSpec((tm, tm), lambda i: (0, 0))],
        out_specs=[pl.BlockSpec((tm, LANES), lambda i: (i, 0)),
                   pl.BlockSpec((1, LANES), lambda i: (0, 0))],
        out_shape=[jax.ShapeDtypeStruct((tokens, LANES), jnp.int32),
                   jax.ShapeDtypeStruct((1, LANES), jnp.int32)],
        scratch_shapes=[pltpu.VMEM((1, LANES), F32)],
        compiler_params=_cparams(("arbitrary",)),
        name="moe_rank",
    )(sel, ltri)


def _row_copy(src_ref, src_row, dst_ref, dst_row, sem):
    return pltpu.make_async_copy(src_ref.at[pl.ds(src_row, 1)], dst_ref.at[pl.ds(dst_row, 1)], sem)


def _moe_dispatch_kernel(dest_ref, tail_ref, h_ref, xs_ref, zero_ref, sem, zsem, *, tm):
    base = pl.program_id(0) * (2 * tm)

    @pl.when(pl.program_id(0) == 0)
    def _():
        zero_ref[...] = jnp.zeros_like(zero_ref)

        def fill(tail):
            return pltpu.make_async_copy(zero_ref, xs_ref.at[pl.ds(pl.multiple_of(tail, 8), zero_ref.shape[0])], zsem)

        def start(e, carry):
            @pl.when(tail_ref[e] >= 0)
            def _():
                fill(tail_ref[e]).start()
            return carry

        def wait(e, carry):
            @pl.when(tail_ref[e] >= 0)
            def _():
                fill(tail_ref[e]).wait()
            return carry

        lax.fori_loop(0, tail_ref.shape[0], start, 0)
        lax.fori_loop(0, tail_ref.shape[0], wait, 0)

    def issue(r, carry):
        for s in range(2):
            _row_copy(h_ref, r, xs_ref, dest_ref[base + 2 * r + s], sem).start()
        return carry

    lax.fori_loop(0, tm, issue, 0, unroll=DMA_ISSUE_UNROLL)
    for s in range(2):
        pltpu.make_async_copy(h_ref, xs_ref.at[pl.ds(0, tm)], sem).wait()


def moe_dispatch(h, dest, tails, rows, row_tile, *, tm=256):
    tokens, d = h.shape
    tm = min(tm, tokens)
    grid_spec = pltpu.PrefetchScalarGridSpec(
        num_scalar_prefetch=2,
        grid=(tokens // tm,),
        in_specs=[pl.BlockSpec((tm, d), lambda i, *_: (i, 0))],
        out_specs=pl.BlockSpec(memory_space=pl.ANY),
        scratch_shapes=[pltpu.VMEM((row_tile, d), h.dtype), pltpu.SemaphoreType.DMA(()),
                        pltpu.SemaphoreType.DMA(())],
    )
    return pl.pallas_call(
        functools.partial(_moe_dispatch_kernel, tm=tm),
        grid_spec=grid_spec,
        out_shape=jax.ShapeDtypeStruct((rows, d), h.dtype),
        compiler_params=_cparams(("arbitrary",)),
        name="moe_dispatch",
    )(dest, tails, h)


def _moe_expert_kernel(te_ref, nused_ref, xs_ref, wg_ref, wu_ref, wd_ref, ys_ref, wgb, wub, wdb):
    r = pl.program_id(0)
    used = r < nused_ref[0]
    changed = (r == 0) | (te_ref[r] != te_ref[jnp.maximum(r - 1, 0)])

    @pl.when(used & changed)
    def _():
        wgb[...] = wg_ref[...].astype(BF16)
        wub[...] = wu_ref[...].astype(BF16)
        wdb[...] = wd_ref[...].astype(BF16)

    @pl.when(used)
    def _():
        x = xs_ref[...].astype(BF16)
        gate_pre = _dot(x, wgb[...])
        hid = (gate_pre * jax.nn.sigmoid(gate_pre)) * _dot(x, wub[...])
        ys_ref[...] = _dot(hid.astype(BF16), wdb[...])

    @pl.when(jnp.logical_not(used))
    def _():
        ys_ref[...] = jnp.zeros_like(ys_ref)


def moe_experts(xs, tile_expert, n_used, wg, wu, wd, layer, *, tm):
    rows, d = xs.shape
    de = wg.shape[-1]
    grid_spec = pltpu.PrefetchScalarGridSpec(
        num_scalar_prefetch=2,
        grid=(rows // tm,),
        in_specs=[pl.BlockSpec((tm, d), lambda r, te, nu: (jnp.minimum(r, nu[0] - 1), 0)),
                  pl.BlockSpec((None, None, d, de), lambda r, te, nu: (layer, te[r], 0, 0)),
                  pl.BlockSpec((None, None, d, de), lambda r, te, nu: (layer, te[r], 0, 0)),
                  pl.BlockSpec((None, None, de, d), lambda r, te, nu: (layer, te[r], 0, 0))],
        out_specs=pl.BlockSpec((tm, d), lambda r, te, nu: (r, 0)),
        scratch_shapes=[pltpu.VMEM((d, de), BF16), pltpu.VMEM((d, de), BF16), pltpu.VMEM((de, d), BF16)],
    )
    return pl.pallas_call(
        _moe_expert_kernel,
        grid_spec=grid_spec,
        out_shape=jax.ShapeDtypeStruct((rows, d), F32),
        compiler_params=_cparams(("arbitrary",)),
        name="moe_experts",
    )(tile_expert, n_used, xs, wg, wu, wd)


def _moe_combine_kernel(dest_ref, x_ref, sel_ref, nw_ref, ys_ref, o_ref, buf_ref, sem, *, tm, normalize):
    base = pl.program_id(0) * (2 * tm)

    def issue(r, carry):
        for s in range(2):
            _row_copy(ys_ref, dest_ref[base + 2 * r + s], buf_ref.at[s], r, sem).start()
        return carry

    lax.fori_loop(0, tm, issue, 0, unroll=DMA_ISSUE_UNROLL)
    for s in range(2):
        pltpu.make_async_copy(ys_ref.at[pl.ds(0, tm)], buf_ref.at[s], sem).wait()
    sel = sel_ref[...]
    out = x_ref[...] + sel[:, SEL_G1:SEL_G1 + 1] * buf_ref[0] + sel[:, SEL_G2:SEL_G2 + 1] * buf_ref[1]
    o_ref[...] = _rms(out, nw_ref[...], NORM_EPS) if normalize else out


def moe_combine(x, sel, ys, dest, final_norm=None, *, tm=256):
    tokens, d = x.shape
    tm = min(tm, tokens)
    normalize = final_norm is not None
    nw = (final_norm if normalize else jnp.ones((d,), F32)).reshape(1, d)
    grid_spec = pltpu.PrefetchScalarGridSpec(
        num_scalar_prefetch=1,
        grid=(tokens // tm,),
        in_specs=[pl.BlockSpec((tm, d), lambda i, *_: (i, 0)),
                  pl.BlockSpec((tm, LANES), lambda i, *_: (i, 0)),
                  pl.BlockSpec((1, d), lambda i, *_: (0, 0)),
                  pl.BlockSpec(memory_space=pl.ANY)],
        out_specs=pl.BlockSpec((tm, d), lambda i, *_: (i, 0)),
        scratch_shapes=[pltpu.VMEM((2, tm, d), F32), pltpu.SemaphoreType.DMA(())],
    )
    return pl.pallas_call(
        functools.partial(_moe_combine_kernel, tm=tm, normalize=normalize),
        grid_spec=grid_spec,
        out_shape=jax.ShapeDtypeStruct((tokens, d), F32),
        compiler_params=_cparams(("arbitrary",)),
        name="moe_combine",
    )(dest, x, sel, nw, ys)


def moe_block(x, norm_w, w_router, b_router, wg, wu, wd, layer, final_norm=None, *, tm=MOE_ROW_TILE):
    tokens, d = x.shape
    n_exp = wg.shape[1]
    h, sel = moe_router(x, norm_w, w_router, b_router)
    rank, counts = moe_rank(sel)
    padded = (counts[0, :n_exp] + (tm - 1)) // tm * tm
    ends = jnp.cumsum(padded)
    starts = ends - padded
    experts = sel[:, SEL_E1:SEL_E2 + 1].astype(jnp.int32)
    dest = (starts[experts] + rank[:, SEL_E1:SEL_E2 + 1]).reshape(-1)
    rows = 2 * tokens + n_exp * tm
    tile_start = jnp.arange(rows // tm, dtype=jnp.int32) * tm
    tile_expert = jnp.minimum(jnp.sum(tile_start[:, None] >= ends[None, :], axis=1), n_exp - 1).astype(jnp.int32)
    n_used = (ends[-1] // tm).astype(jnp.int32).reshape(1)
    tails = jnp.concatenate([jnp.where(padded > 0, ends - tm, -1),
                             jnp.where(tile_start >= ends[-1], tile_start, -1)]).astype(jnp.int32)
    xs = moe_dispatch(h, dest, tails, rows, tm)
    ys = moe_experts(xs, tile_expert, n_used, wg, wu, wd, layer, tm=tm)
    return moe_combine(x, sel, ys, dest, final_norm)


def _proj_weights(w_in_l, w_vres_l):
    d = w_in_l.shape[0]
    mla0 = RW_COLS + DF_COLS
    vres = jnp.zeros((d, RW_V_RANK), F32) if w_vres_l is None else w_vres_l
    part_f = [w_in_l[:, :RW_COLS],
              vres, jnp.zeros((d, LANES - RW_V_RANK), F32),
              w_in_l[:, mla0 + ML_Q_RANK + ML_KV_RANK:mla0 + ML_COLS], jnp.zeros((d, LANES - ML_ROPE), F32)]
    part_a = [w_in_l[:, RW_COLS:RW_COLS + DF_COLS],
              w_in_l[:, mla0 + ML_Q_RANK:mla0 + ML_Q_RANK + ML_KV_RANK],
              jnp.zeros((d, OFF_MQ - OFF_MKV - ML_KV_RANK), F32),
              w_in_l[:, mla0:mla0 + ML_Q_RANK]]
    return jnp.concatenate(part_f, axis=1).astype(BF16), jnp.concatenate(part_a, axis=1).astype(BF16)


def _pad_rows(w, rows, at=0):
    out = jnp.zeros((rows, w.shape[1]), w.dtype)
    return lax.dynamic_update_slice(out, w, (at, 0))


def kernel(x, mem, positions, rel_bias, final_norm, norm_mix, w_in, w_in_vres, w_out, tm_mu, tm_mu_vres, tm_w0, tm_w2, tm_a0, tm_a2, tm_v0, tm_v2, tm_g2, tm_k_k, tm_k_a, tm_r_k, tm_ln_w, tm_ln_b, da_lq1, da_lk1, da_lq2, da_lk2, da_subln, mla_q_norm, mla_wq_b, mla_kv_norm, mla_wkv_b, norm_cross, norm_mem, ca_wq, ca_wkv, ca_wo, norm_ffn, moe_w_group, moe_b_group, moe_w_expert, moe_b_expert, moe_w_gate, moe_w_up, moe_w_down):
    batch, seq, d = x.shape
    tokens = batch * seq
    depth = norm_mix.shape[0]
    xf = x.reshape(tokens, d)
    memf = mem.reshape(-1, d)
    positions = positions.astype(jnp.int32)

    head_of_lane = jnp.arange(RW_DIM) // RW_HEAD_DIM
    seg = (head_of_lane[:, None] == jnp.arange(LANES)[None, :]).astype(BF16)
    seg_t = seg.T
    row = lambda v: v.reshape(1, -1)

    v_first = None
    for l in range(depth):
        w_f, w_a = _proj_weights(w_in[l], None if l == 0 else w_in_vres[l - 1])
        proj = norm_matmul(xf, norm_mix[l], w_f, tn=PROJ_F_COLS // 2)
        pa = norm_matmul(xf, norm_mix[l], w_a, out_dtype=BF16, tn=PROJ_A_COLS // 2)

        mu = tm_mu[l]
        prm = dict(mu_r=row(mu[:RW_DIM]), mu_k=row(mu[RW_DIM:2 * RW_DIM]), mu_v=row(mu[2 * RW_DIM:3 * RW_DIM]),
                   mu_l=row(mu[3 * RW_DIM:]), w0=row(tm_w0[l]), a0=row(tm_a0[l]),
                   w2=_pad_rows(tm_w2[l], LANES, 0), a2=_pad_rows(tm_a2[l], LANES, RW_W_RANK),
                   g2=tm_g2[l].astype(BF16), k_k=row(tm_k_k[l]), k_a=row(tm_k_a[l]), r_k=row(tm_r_k[l]),
                   seg=seg, seg_t=seg_t)
        if l > 0:
            prm.update(mu_vr=jnp.pad(row(tm_mu_vres[l - 1]), ((0, 0), (0, LANES - RW_V_RANK))),
                       v0=row(tm_v0[l - 1]), v2=_pad_rows(tm_v2[l - 1], LANES, 0))
        r, lw, k, v, kap, beta, gate, bonus = rwkv_prep(proj, batch, v_first, prm)
        if l == 0:
            v_first = v
        o = rwkv_scan(r, lw, k, v, kap, beta, batch)
        y_a = rwkv_post(o, bonus, gate, tm_ln_w[l], tm_ln_b[l], seg, seg_t)

        lambda_init = 0.8 - 0.6 * math.exp(-0.3 * l)
        lam = (jnp.exp(jnp.sum(da_lq1[l] * da_lk1[l])) - jnp.exp(jnp.sum(da_lq2[l] * da_lk2[l])) + lambda_init)
        y_b = diff_attention(pa, positions, rel_bias, lam, lambda_init, da_subln[l])

        wq = mla_wq_b[l].reshape(ML_Q_RANK, ML_HEADS, ML_NOPE + ML_ROPE)
        wq_pe = jnp.pad(wq[:, :, ML_NOPE:], ((0, 0), (0, 0), (0, LANES - ML_ROPE)))
        wq_all = jnp.concatenate([wq[:, :, :ML_NOPE].reshape(ML_Q_RANK, -1),
                                  wq_pe.reshape(ML_Q_RANK, -1)], axis=1).astype(BF16)
        wkv = mla_wkv_b[l].reshape(ML_KV_RANK, ML_HEADS, ML_NOPE + ML_V)
        wkv = jnp.concatenate([wkv[:, :, :ML_NOPE].reshape(ML_KV_RANK, -1),
                               wkv[:, :, ML_NOPE:].reshape(ML_KV_RANK, -1)], axis=1).astype(BF16)
        qf, kf, v_mla = mla_prep(pa, proj, positions, mla_q_norm[l], mla_kv_norm[l], wq_all, wkv)
        y_c = mla_attention(qf, kf, v_mla, batch)

        wo = w_out[l].astype(BF16)
        xf = matmul_res([y_a, y_b, y_c],
                        [wo[:RW_DIM], wo[RW_DIM:RW_DIM + DF_DIM], wo[RW_DIM + DF_DIM:]], xf)

        kv_mem = norm_matmul(memf, norm_mem[l], ca_wkv[l].astype(BF16), out_dtype=BF16)
        xf = cross_block(xf, batch, norm_cross[l], ca_wq[l].astype(BF16), kv_mem, ca_wo[l].astype(BF16))

        w_router = jnp.concatenate(
            [moe_w_expert[l], moe_w_group[l], jnp.zeros((d, LANES - MOE_EXPERTS - MOE_GROUPS), F32)], axis=1)
        b_router = jnp.concatenate(
            [moe_b_expert[l], moe_b_group[l], jnp.zeros((LANES - MOE_EXPERTS - MOE_GROUPS,), F32)]).reshape(1, LANES)
        xf = moe_block(xf, norm_ffn[l], w_router, b_router, moe_w_gate, moe_w_up, moe_w_down, l,
                       final_norm if l == depth - 1 else None)

    return xf.reshape(batch, seq, d)
```

```python
import functools
import math

import jax
import jax.numpy as jnp
from jax import lax
from jax.experimental import pallas as pl
from jax.experimental.pallas import tpu as pltpu

F32 = jnp.float32
BF16 = jnp.bfloat16

NORM_EPS = 1e-6
ROPE_THETA = 10000.0

RW_HEADS = 16
RW_HEAD_DIM = 64
RW_DIM = RW_HEADS * RW_HEAD_DIM
RW_W_RANK = 64
RW_A_RANK = 64
RW_G_RANK = 128
RW_V_RANK = 32
RW_LORA = RW_W_RANK + RW_A_RANK + RW_G_RANK
RW_LN_EPS = 64e-5
RW_COLS = 3 * RW_DIM + RW_LORA

DF_HEADS = 4
DF_HEAD_DIM = 64
DF_V_DIM = 2 * DF_HEAD_DIM
DF_QK = DF_HEADS * 2 * DF_HEAD_DIM
DF_DIM = DF_HEADS * DF_V_DIM
DF_COLS = 2 * DF_QK + DF_DIM
DF_SUBLN_EPS = 1e-5

ML_HEADS = 4
ML_Q_RANK = 384
ML_KV_RANK = 256
ML_NOPE = 128
ML_ROPE = 64
ML_V = 128
ML_DIM = ML_HEADS * ML_V
ML_COLS = ML_Q_RANK + ML_KV_RANK + ML_ROPE

REL_BUCKETS = 32
REL_MAX_DIST = 128

CA_HEADS = 4
CA_HEAD_DIM = 128
CA_DIM = CA_HEADS * CA_HEAD_DIM

MOE_GROUPS = 4
MOE_PER_GROUP = 8
MOE_EXPERTS = MOE_GROUPS * MOE_PER_GROUP

LANES = 128
SCAN_CHUNK = 64
SCAN_GROUP = 4
SCAN_BATCHES = 2
ATTN_TILE = 512
DMA_ISSUE_UNROLL = 8
MOE_ROW_TILE = 256
VMEM_LIMIT = 56 * 1024 * 1024
NEG_BIG = -1e30

LOG2E = 1.4426950408889634

OFF_R = 0
OFF_K = RW_DIM
OFF_V = 2 * RW_DIM
OFF_LORA = 3 * RW_DIM
OFF_VRES = OFF_LORA + RW_LORA
OFF_KPE = OFF_VRES + LANES
PROJ_F_COLS = OFF_KPE + LANES
OFF_DQ = 0
OFF_DK = OFF_DQ + DF_QK
OFF_DV = OFF_DK + DF_QK
OFF_MKV = OFF_DV + DF_DIM
OFF_MQ = 5 * ML_Q_RANK
PROJ_A_COLS = OFF_MQ + ML_Q_RANK


def _cparams(sem, vmem=VMEM_LIMIT, flags=None):
    return pltpu.CompilerParams(dimension_semantics=sem, vmem_limit_bytes=vmem, flags=flags)


def _dot(a, b):
    return jnp.dot(a, b, preferred_element_type=F32)


def _dot_t(a, b):
    return lax.dot_general(a, b, (((1,), (1,)), ((), ())), preferred_element_type=F32)


def _split3(x):
    hi = x.astype(BF16)
    r1 = x - hi.astype(F32)
    mid = r1.astype(BF16)
    lo = (r1 - mid.astype(F32)).astype(BF16)
    return hi, mid, lo


def _dot_rhs01(x, ones_bf16):
    hi = x.astype(BF16)
    lo = (x - hi.astype(F32)).astype(BF16)
    return _dot(hi, ones_bf16) + _dot(lo, ones_bf16)


def _dot_x3(a, b):
    ah = a.astype(BF16)
    al = (a - ah.astype(F32)).astype(BF16)
    bh = b.astype(BF16)
    bl = (b - bh.astype(F32)).astype(BF16)
    return _dot(ah, bh) + _dot(ah, bl) + _dot(al, bh)


def _rms(x, w, eps):
    ms = jnp.mean(x * x, axis=-1, keepdims=True)
    return x * lax.rsqrt(ms + eps) * w


def _norm_matmul_kernel(x_ref, nw_ref, w_ref, o_ref, xn_ref, *, eps):
    @pl.when(pl.program_id(1) == 0)
    def _():
        xn_ref[...] = _rms(x_ref[...], nw_ref[...], eps).astype(BF16)

    o_ref[...] = _dot(xn_ref[...], w_ref[...]).astype(o_ref.dtype)


def norm_matmul(x, nw, w, *, out_dtype=F32, tm=512, tn=None, eps=NORM_EPS):
    m, d = x.shape
    n = w.shape[1]
    tm = min(tm, m)
    tn = n if tn is None else tn
    return pl.pallas_call(
        functools.partial(_norm_matmul_kernel, eps=eps),
        grid=(m // tm, n // tn),
        in_specs=[pl.BlockSpec((tm, d), lambda i, j: (i, 0)),
                  pl.BlockSpec((1, d), lambda i, j: (0, 0)),
                  pl.BlockSpec((d, tn), lambda i, j: (0, j))],
        out_specs=pl.BlockSpec((tm, tn), lambda i, j: (i, j)),
        out_shape=jax.ShapeDtypeStruct((m, n), out_dtype),
        scratch_shapes=[pltpu.VMEM((tm, d), BF16)],
        compiler_params=_cparams(("parallel", "arbitrary")),
        name="norm_matmul",
    )(x, nw.reshape(1, d), w)


def _matmul_res_kernel(*refs, n_a):
    a_refs, w_refs = refs[:n_a], refs[n_a:2 * n_a]
    res_ref, o_ref = refs[2 * n_a], refs[2 * n_a + 1]
    acc = res_ref[...]
    for a_ref, w_ref in zip(a_refs, w_refs):
        acc = acc + _dot(a_ref[...].astype(BF16), w_ref[...])
    o_ref[...] = acc


def matmul_res(a_list, w_list, res, *, tm=1024, tn=1024):
    m, n = res.shape
    tm = min(tm, m)
    tn = min(tn, n)
    n_a = len(a_list)
    in_specs = ([pl.BlockSpec((tm, a.shape[1]), lambda i, j: (i, 0)) for a in a_list]
                + [pl.BlockSpec((w.shape[0], tn), lambda i, j: (0, j)) for w in w_list]
                + [pl.BlockSpec((tm, tn), lambda i, j: (i, j))])
    return pl.pallas_call(
        functools.partial(_matmul_res_kernel, n_a=n_a),
        grid=(m // tm, n // tn),
        in_specs=in_specs,
        out_specs=pl.BlockSpec((tm, tn), lambda i, j: (i, j)),
        out_shape=jax.ShapeDtypeStruct((m, n), F32),
        compiler_params=_cparams(("parallel", "arbitrary")),
        name="matmul_res",
    )(*a_list, *w_list, res)


def _softplus(z):
    return jnp.maximum(z, 0.0) + jnp.log(1.0 + jnp.exp(-jnp.abs(z)))


def _rwkv_prep_kernel(*refs, has_vres):
    if has_vres:
        (pr_ref, pk_ref, pv_ref, pl_ref, pvr_ref, vfirst_ref,
         mu_r, mu_k, mu_v, mu_l, mu_vr, w0, w2, a0, a2, g2, v0, v2,
         k_k, k_a, r_k, seg, seg_t,
         r_o, lw_o, k_o, v_o, kap_o, beta_o, g_o, bonus_o,
         last_r, last_k, last_v, last_l, last_vr) = refs
    else:
        (pr_ref, pk_ref, pv_ref, pl_ref,
         mu_r, mu_k, mu_v, mu_l, w0, w2, a0, a2, g2,
         k_k, k_a, r_k, seg, seg_t,
         r_o, lw_o, k_o, v_o, kap_o, beta_o, g_o, bonus_o,
         last_r, last_k, last_v, last_l) = refs
    t = pl.program_id(1)

    def shifted(p_ref, last_ref, mu_ref):
        p = p_ref[...]
        n = p.shape[0]
        carried = jnp.where(t == 0, 0.0, last_ref[0:1, :])
        row = lax.broadcasted_iota(jnp.int32, p.shape, 0)
        prev = jnp.where(row == 0, carried, pltpu.roll(p, 1, axis=0))
        last_ref[0:1, :] = p[n - 1:n, :]
        return p + mu_ref[...] * (prev - p)

    r = shifted(pr_ref, last_r, mu_r)
    k = shifted(pk_ref, last_k, mu_k)
    v = shifted(pv_ref, last_v, mu_v)
    lora = shifted(pl_ref, last_l, mu_l)
    wl = lora[:, :LANES]
    gl = lora[:, LANES:]

    lane = lax.broadcasted_iota(jnp.int32, wl.shape, 1)
    wl_t = jnp.where(lane < RW_W_RANK, jnp.tanh(wl), 0.0)
    al = jnp.where(lane >= RW_W_RANK, wl, 0.0)
    w_log = -_softplus(-(w0[...] + _dot_x3(wl_t, w2[...]))) - 0.5
    lw_o[...] = -jnp.exp(w_log)
    a = jax.nn.sigmoid(a0[...] + _dot_x3(al, a2[...]))
    g_o[...] = _dot(jax.nn.sigmoid(gl).astype(BF16), g2[...]).astype(g_o.dtype)

    segm, segm_t = seg[...], seg_t[...]

    def head_sum(x):
        return _dot_rhs01(_dot_rhs01(x, segm), segm_t)

    kk = k * k_k[...]
    kk = kk / jnp.maximum(jnp.sqrt(head_sum(kk * kk)), 1e-12)
    k = k * (1.0 + (a - 1.0) * k_a[...])
    if has_vres:
        vr = shifted(pvr_ref, last_vr, mu_vr)
        mix = jax.nn.sigmoid(v0[...] + _dot_x3(vr, v2[...]))
        v = v + (vfirst_ref[...] - v) * mix
    r_o[...] = r.astype(r_o.dtype)
    k_o[...] = k.astype(k_o.dtype)
    v_o[...] = v.astype(v_o.dtype)
    kap_o[...] = kk.astype(kap_o.dtype)
    beta_o[...] = (kk * a).astype(beta_o.dtype)
    bonus_o[...] = (head_sum(r * k * r_k[...]) * v).astype(bonus_o.dtype)


def rwkv_prep(proj, batch, vfirst, prm, *, tt=256):
    tokens = proj.shape[0]
    seq = tokens // batch
    tt = min(tt, seq)
    nt = seq // tt
    has_vres = vfirst is not None
    d = RW_DIM

    def rows(width, col):
        return pl.BlockSpec((tt, width), lambda b, t, col=col: (b * nt + t, col))

    def full(shape):
        return pl.BlockSpec(shape, lambda b, t: (0, 0))

    in_specs = [rows(d, OFF_R // d), rows(d, OFF_K // d), rows(d, OFF_V // d),
                rows(RW_LORA, OFF_LORA // RW_LORA)]
    args = [proj, proj, proj, proj]
    if has_vres:
        in_specs += [rows(LANES, OFF_VRES // LANES), rows(d, 0)]
        args += [proj, vfirst]
    names = ["mu_r", "mu_k", "mu_v", "mu_l"] + (["mu_vr"] if has_vres else []) + ["w0", "w2", "a0", "a2", "g2"]
    names += (["v0", "v2"] if has_vres else []) + ["k_k", "k_a", "r_k", "seg", "seg_t"]
    for nm in names:
        in_specs.append(full(prm[nm].shape))
        args.append(prm[nm])
    out_spec = pl.BlockSpec((tt, d), lambda b, t: (b * nt + t, 0))
    scratch = [pltpu.VMEM((8, d), F32)] * 3 + [pltpu.VMEM((8, RW_LORA), F32)]
    if has_vres:
        scratch.append(pltpu.VMEM((8, LANES), F32))
    return pl.pallas_call(
        functools.partial(_rwkv_prep_kernel, has_vres=has_vres),
        grid=(batch, nt),
        in_specs=in_specs,
        out_specs=[out_spec] * 8,
        out_shape=[jax.ShapeDtypeStruct((tokens, d), F32 if i == 1 else BF16) for i in range(8)],
        scratch_shapes=scratch,
        compiler_params=_cparams(("arbitrary", "arbitrary")),
        name="rwkv_prep",
    )(*args)


def _rwkv_scan_kernel(r_ref, lw_ref, k_ref, v_ref, kap_ref, beta_ref, tril_ref, bmask_ref,
                      o_ref, ht_ref):
    @pl.when(pl.program_id(1) == 0)
    def _():
        ht_ref[...] = jnp.zeros_like(ht_ref)

    n_batch, c, d = lw_ref.shape
    w = ht_ref.shape[1]
    g = w // RW_HEAD_DIM
    bmask = bmask_ref[...]
    bmask_b = bmask.astype(BF16)
    tril3 = tril_ref[...]
    t_idx = lax.broadcasted_iota(jnp.int32, (c, w), 0)
    s_idx = lax.broadcasted_iota(jnp.int32, (c, w), 1) % c
    strict = t_idx > s_idx
    incl = t_idx >= s_idx
    n_sq = int(math.log2(c))

    def stack(x):
        return jnp.concatenate([x.astype(BF16)] * g, axis=0) * bmask_b

    sls = [(bi, slice(None), slice(lo, lo + w)) for bi in range(n_batch) for lo in range(0, d, w)]
    groups = range(len(sls))
    lw = [lw_ref[sl] for sl in sls]
    cum = [_dot(tril3, jnp.concatenate(_split3(x), axis=0)) for x in lw]
    total = [x[c - 1:c, :] for x in cum]
    ar = [jnp.concatenate([-kap_ref[sls[gi]] * jnp.exp(cum[gi] - lw[gi]), r_ref[sls[gi]] * jnp.exp(cum[gi])],
                          axis=0).astype(BF16) for gi in groups]
    p_inv = [jnp.exp(-x) for x in cum]
    b_s = [stack(beta_ref[sls[gi]] * p_inv[gi]) for gi in groups]
    k_s = [stack(k_ref[sls[gi]] * p_inv[gi]) for gi in groups]
    v_n = [v_ref[sl] for sl in sls]
    v_s = [stack(x) for x in v_n]

    arb = [_dot_t(ar[gi], b_s[gi]) for gi in groups]
    ark = [_dot_t(ar[gi], k_s[gi]) for gi in groups]
    ab = [jnp.where(strict, m[:c], 0.0) for m in arb]
    rb = [jnp.where(incl, m[c:], 0.0).astype(BF16) for m in arb]
    akrk = [jnp.concatenate([jnp.where(strict, m[:c], 0.0), jnp.where(incl, m[c:], 0.0)], axis=0).astype(BF16)
            for m in ark]

    ht = [ht_ref[gi] for gi in groups]
    base = [_dot_t(ar[gi], ht[gi].astype(BF16)) + _dot(akrk[gi], v_s[gi]) for gi in groups]
    x = [m[:c] for m in base]
    lp = ab
    for i in range(n_sq):
        lpb = [m.astype(BF16) for m in lp]
        x = [x[gi] + _dot(lpb[gi], stack(x[gi])) for gi in groups]
        if i < n_sq - 1:
            lp = [_dot(lpb[gi], stack(lp[gi])) for gi in groups]
    for gi in groups:
        o_ref[sls[gi]] = base[gi][c:] + _dot(rb[gi], stack(x[gi]))

    for gi in groups:
        p_rem = jnp.exp(total[gi] - cum[gi])
        z = jnp.concatenate([beta_ref[sls[gi]] * p_rem, k_ref[sls[gi]] * p_rem], axis=0).astype(BF16)
        uv_t = jnp.concatenate([x[gi], v_n[gi].astype(F32)], axis=0).T.astype(BF16)
        ht_ref[gi] = ht[gi] * jnp.exp(total[gi]) + bmask * _dot(uv_t, z)


def rwkv_scan(r, lw, k, v, kap, beta, batch):
    tokens, d = r.shape
    seq = tokens // batch
    c = min(SCAN_CHUNK, seq)
    nc = seq // c
    gw = SCAN_GROUP * RW_HEAD_DIM
    rr = SCAN_GROUP * c
    assert c == RW_HEAD_DIM, "the stacking mask doubles as the head-block mask of the state"
    bb = math.gcd(batch, SCAN_BATCHES)
    tril = jnp.tile((jnp.arange(c)[:, None] >= jnp.arange(c)[None, :]).astype(BF16), (1, 3))
    bmask = (jnp.arange(rr)[:, None] // c == jnp.arange(gw)[None, :] // RW_HEAD_DIM).astype(F32)
    blk = pl.BlockSpec((bb, c, d), lambda b, i: (b, i, 0))
    as3d = lambda a: a.reshape(batch, seq, d)
    out = pl.pallas_call(
        _rwkv_scan_kernel,
        grid=(batch // bb, nc),
        in_specs=[blk] * 6 + [pl.BlockSpec((c, 3 * c), lambda b, i: (0, 0)),
                              pl.BlockSpec((rr, gw), lambda b, i: (0, 0))],
        out_specs=blk,
        out_shape=jax.ShapeDtypeStruct((batch, seq, d), F32),
        scratch_shapes=[pltpu.VMEM((bb * (d // gw), gw, gw), F32)],
        compiler_params=_cparams(("arbitrary", "arbitrary")),
        name="rwkv_scan",
    )(as3d(r), as3d(lw), as3d(k), as3d(v), as3d(kap), as3d(beta), tril, bmask)
    return out.reshape(tokens, d)


def _rwkv_post_kernel(o_ref, bonus_ref, g_ref, lnw_ref, lnb_ref, seg, seg_t, y_ref):
    segm, segm_t = seg[...], seg_t[...]

    def head_mean(x):
        return _dot_rhs01(_dot_rhs01(x, segm), segm_t) * (1.0 / RW_HEAD_DIM)

    o = o_ref[...]
    dlt = o - head_mean(o)
    var = head_mean(dlt * dlt)
    y = dlt * lax.rsqrt(var + RW_LN_EPS) * lnw_ref[...] + lnb_ref[...]
    y_ref[...] = ((y + bonus_ref[...]) * g_ref[...]).astype(y_ref.dtype)


def rwkv_post(o, bonus, g, ln_w, ln_b, seg, seg_t, *, tm=512):
    tokens, d = o.shape
    tm = min(tm, tokens)
    blk = pl.BlockSpec((tm, d), lambda i: (i, 0))
    vec = pl.BlockSpec((1, d), lambda i: (0, 0))
    return pl.pallas_call(
        _rwkv_post_kernel,
        grid=(tokens // tm,),
        in_specs=[blk, blk, blk, vec, vec,
                  pl.BlockSpec(seg.shape, lambda i: (0, 0)), pl.BlockSpec(seg_t.shape, lambda i: (0, 0))],
        out_specs=blk,
        out_shape=jax.ShapeDtypeStruct((tokens, d), BF16),
        compiler_params=_cparams(("parallel",)),
        name="rwkv_post",
    )(o, bonus, g, ln_w.reshape(1, d), ln_b.reshape(1, d), seg, seg_t)


def _t5_thresholds():
    max_exact = REL_BUCKETS // 2
    thr = list(range(1, max_exact))
    n = max_exact
    for bucket in range(max_exact, REL_BUCKETS):
        while True:
            large = max_exact + int(math.log(max(n, max_exact) / max_exact)
                                    / math.log(REL_MAX_DIST / max_exact) * (REL_BUCKETS - max_exact))
            if min(large, REL_BUCKETS - 1) >= bucket:
                break
            n += 1
        thr.append(n)
    return thr


T5_THRESHOLDS = _t5_thresholds()
T5_FAR = T5_THRESHOLDS[-1]


def _softmax_tiles(s_list, c_list, states, vt_list):
    stats = []
    for s_t, c, (m_old, l_old, _) in zip(s_list, c_list, states):
        m_new = jnp.maximum(m_old, jnp.max(s_t, axis=0, keepdims=True) + c)
        alpha = jnp.exp2(m_old - m_new)
        p_t = jnp.exp2(s_t - (m_new - c))
        stats.append((m_new, alpha, alpha * l_old + jnp.sum(p_t, axis=0, keepdims=True), p_t.astype(BF16)))
    return tuple((m_new, l_new, alpha * acc + _dot(vt, p_t))
                 for (m_new, alpha, l_new, p_t), (_, _, acc), vt in zip(stats, states, vt_list))


def _transpose_into(vt_ref, v_ref, chunk):
    seq = v_ref.shape[0]
    for c in range(seq // chunk):
        vt_ref[:, c * chunk:(c + 1) * chunk] = v_ref[c * chunk:(c + 1) * chunk, :].astype(F32).T.astype(BF16)


def _diff_attn_kernel(qfirst_ref, klast_ref, q_ref, k_ref, v_ref, qpos_ref, kpos_ref, subln_ref, table_ref, lam_ref,
                      o_ref, vt_ref, *, tq, tk, scale2, out_scale):
    b, i = pl.program_id(0), pl.program_id(1)
    nq = pl.num_programs(1)
    seq = k_ref.shape[0]
    nk = seq // tk
    w = DF_V_DIM

    @pl.when(i == 0)
    def _():
        _transpose_into(vt_ref, v_ref, tk)

    n_tiles = (i * tq + tq - 1) // tk + 1
    qf = qfirst_ref[b * nq + i]
    n_far = lax.while_loop(
        lambda j: (j * tk + tk - 1 <= i * tq) & (qf - klast_ref[b * nk + jnp.minimum(j, nk - 1)] >= T5_FAR),
        lambda j: j + 1, jnp.int32(0))

    dist = lax.broadcasted_iota(jnp.int32, (1, LANES), 1)
    qpos = qpos_ref[...]
    q_idx = i * tq + lax.broadcasted_iota(jnp.int32, (tk, tq), 1)
    k_off = lax.broadcasted_iota(jnp.int32, (tk, tq), 0)
    lane = lax.broadcasted_iota(jnp.int32, (tq, w), 1)

    bias_rows, c_far, qm = [], [], []
    for h in range(DF_HEADS):
        bias_vec = jnp.full((1, LANES), table_ref[h], F32)
        for bucket, thr in enumerate(T5_THRESHOLDS, start=1):
            bias_vec = jnp.where(dist >= thr, table_ref[bucket * DF_HEADS + h], bias_vec)
        bias_rows.append(jnp.broadcast_to(bias_vec * LOG2E, (tk, LANES)))
        c_far.append(table_ref[(REL_BUCKETS - 1) * DF_HEADS + h] * LOG2E)
        qh = q_ref[:, h * w:(h + 1) * w].astype(F32) * scale2
        qm.append([jnp.where((lane >= mi * DF_HEAD_DIM) & (lane < (mi + 1) * DF_HEAD_DIM), qh, 0.0).astype(BF16)
                   for mi in range(2)])

    def tiles(j, h):
        off = pl.multiple_of(j * tk, tk)
        return k_ref[pl.ds(off, tk), h * w:(h + 1) * w], vt_ref[h * w:(h + 1) * w, pl.ds(off, tk)], off

    chains = [(h, mi) for h in range(DF_HEADS) for mi in range(2)]

    def far_body(j, st):
        kv = [tiles(j, h) for h in range(DF_HEADS)]
        s = [_dot_t(kv[h][0], qm[h][mi]) for h, mi in chains]
        return _softmax_tiles(s, [c_far[h] for h, _ in chains], st, [kv[h][1] for h, _ in chains])

    def near_body(j, st):
        off = pl.multiple_of(j * tk, tk)
        n = jnp.clip(qpos - kpos_ref[pl.ds(off, tk), :], 0, LANES - 1)
        keep = q_idx >= off + k_off
        kv = [tiles(j, h) for h in range(DF_HEADS)]
        bias = [jnp.concatenate(
            [jnp.take_along_axis(bias_rows[h], n[:, cb * LANES:(cb + 1) * LANES], axis=1)
             for cb in range(tq // LANES)], axis=1) for h in range(DF_HEADS)]
        s = [jnp.where(keep, _dot_t(kv[h][0], qm[h][mi]) + bias[h], NEG_BIG) for h, mi in chains]
        return _softmax_tiles(s, [0.0] * len(chains), st, [kv[h][1] for h, _ in chains])

    init = tuple((jnp.full((1, tq), NEG_BIG, F32), jnp.zeros((1, tq), F32), jnp.zeros((w, tq), F32))
                 for _ in range(2 * DF_HEADS))
    st = lax.fori_loop(0, n_far, far_body, init)
    st = lax.fori_loop(n_far, n_tiles, near_body, st)
    for h in range(DF_HEADS):
        s0, s1 = st[2 * h], st[2 * h + 1]
        d_t = s0[2] / s0[1] - lam_ref[0] * (s1[2] / s1[1])
        ms = jnp.mean(d_t * d_t, axis=0, keepdims=True)
        y_t = d_t * lax.rsqrt(ms + DF_SUBLN_EPS) * (subln_ref[...] * out_scale)
        o_ref[:, h * w:(h + 1) * w] = y_t.T.astype(o_ref.dtype)


def diff_attention(pa, positions, rel_bias, lam, lambda_init, subln_w, *, tq=ATTN_TILE):
    batch, seq = positions.shape
    tokens = batch * seq
    tq = min(tq, seq)
    tk = tq
    nq, nk = seq // tq, seq // tk
    qfirst = positions[:, ::tq].reshape(-1)
    klast = positions[:, tk - 1::tk].reshape(-1)
    qpos = positions.reshape(batch, 1, seq)
    kpos = positions.reshape(batch, seq, 1)
    wd = DF_DIM
    grid_spec = pltpu.PrefetchScalarGridSpec(
        num_scalar_prefetch=2,
        grid=(batch, nq),
        in_specs=[pl.BlockSpec((tq, wd), lambda b, i, *_: (b * nq + i, OFF_DQ // wd)),
                  pl.BlockSpec((seq, wd), lambda b, i, *_: (b, OFF_DK // wd)),
                  pl.BlockSpec((seq, wd), lambda b, i, *_: (b, OFF_DV // wd)),
                  pl.BlockSpec((None, 1, tq), lambda b, i, *_: (b, 0, i)),
                  pl.BlockSpec((None, seq, 1), lambda b, i, *_: (b, 0, 0)),
                  pl.BlockSpec((DF_V_DIM, 1), lambda b, i, *_: (0, 0)),
                  pl.BlockSpec(memory_space=pltpu.SMEM),
                  pl.BlockSpec(memory_space=pltpu.SMEM)],
        out_specs=pl.BlockSpec((tq, wd), lambda b, i, *_: (b * nq + i, 0)),
        scratch_shapes=[pltpu.VMEM((wd, seq), BF16)],
    )
    return pl.pallas_call(
        functools.partial(_diff_attn_kernel, tq=tq, tk=tk, scale2=DF_HEAD_DIM ** -0.5 * LOG2E,
                          out_scale=1.0 - lambda_init),
        grid_spec=grid_spec,
        out_shape=jax.ShapeDtypeStruct((tokens, DF_DIM), BF16),
        compiler_params=_cparams(("arbitrary", "arbitrary")),
        name="diff_attention",
    )(qfirst, klast, pa, pa, pa, qpos, kpos, subln_w.reshape(DF_V_DIM, 1), rel_bias.reshape(-1), lam.reshape(1))


ML_QK_PAD = 2 * LANES


def _rope_block(x, cos, sin):
    half = ML_ROPE // 2
    lane = lax.broadcasted_iota(jnp.int32, x.shape, 1)
    rot = jnp.where(lane < half, -pltpu.roll(x, LANES - half, axis=1),
                    jnp.where(lane < ML_ROPE, pltpu.roll(x, half, axis=1), 0.0))
    return x * cos + rot * sin


def _mla_prep_kernel(mq_ref, mkv_ref, kpe_ref, pos_ref, qn_w, kvn_w, wq_ref, wkv_ref, freq_ref,
                     qf_o, kf_o, v_o, *, qscale):
    ang = pos_ref[...].astype(F32) * freq_ref[...]
    cos, sin = jnp.cos(ang), jnp.sin(ang)
    qc = _rms(mq_ref[...].astype(F32), qn_w[...], NORM_EPS).astype(BF16)
    q_all = _dot(qc, wq_ref[...]) * qscale
    kvc = _rms(mkv_ref[...].astype(F32), kvn_w[...], NORM_EPS).astype(BF16)
    kvb = _dot(kvc, wkv_ref[...])
    kpe = _rope_block(kpe_ref[...], cos, sin).astype(BF16)
    nope_w = ML_HEADS * ML_NOPE
    for h in range(ML_HEADS):
        lo = h * ML_QK_PAD
        qf_o[:, lo:lo + LANES] = q_all[:, h * LANES:(h + 1) * LANES].astype(BF16)
        qf_o[:, lo + LANES:lo + 2 * LANES] = _rope_block(
            q_all[:, nope_w + h * LANES:nope_w + (h + 1) * LANES], cos, sin).astype(BF16)
        kf_o[:, lo:lo + LANES] = kvb[:, h * LANES:(h + 1) * LANES].astype(BF16)
        kf_o[:, lo + LANES:lo + 2 * LANES] = kpe
    v_o[...] = kvb[:, nope_w:].astype(BF16)


def mla_prep(pa, pf, positions, q_norm, kv_norm, wq_all, wkv, *, tm=512):
    tokens = pa.shape[0]
    tm = min(tm, tokens)
    half = ML_ROPE // 2
    inv_freq = ROPE_THETA ** (-jnp.arange(half, dtype=F32) / half)
    freq = jnp.concatenate([inv_freq, inv_freq, jnp.zeros((LANES - ML_ROPE,), F32)]).reshape(1, LANES)

    def full(a):
        return pl.BlockSpec(a.shape, lambda i: (0, 0))

    qn_w = q_norm.reshape(1, -1)
    kvn_w = kv_norm.reshape(1, -1)
    wide = ML_HEADS * ML_QK_PAD
    return pl.pallas_call(
        functools.partial(_mla_prep_kernel, qscale=(ML_NOPE + ML_ROPE) ** -0.5 * LOG2E),
        grid=(tokens // tm,),
        in_specs=[pl.BlockSpec((tm, ML_Q_RANK), lambda i: (i, OFF_MQ // ML_Q_RANK)),
                  pl.BlockSpec((tm, ML_KV_RANK), lambda i: (i, OFF_MKV // ML_KV_RANK)),
                  pl.BlockSpec((tm, LANES), lambda i: (i, OFF_KPE // LANES)),
                  pl.BlockSpec((tm, 1), lambda i: (i, 0)),
                  full(qn_w), full(kvn_w), full(wq_all), full(wkv), full(freq)],
        out_specs=[pl.BlockSpec((tm, wide), lambda i: (i, 0)),
                   pl.BlockSpec((tm, wide), lambda i: (i, 0)),
                   pl.BlockSpec((tm, ML_DIM), lambda i: (i, 0))],
        out_shape=[jax.ShapeDtypeStruct((tokens, wide), BF16),
                   jax.ShapeDtypeStruct((tokens, wide), BF16),
                   jax.ShapeDtypeStruct((tokens, ML_DIM), BF16)],
        compiler_params=_cparams(("parallel",)),
        name="mla_prep",
    )(pa, pa, pf, positions.reshape(tokens, 1), qn_w, kvn_w, wq_all, wkv, freq)


def _mla_attn_kernel(q_ref, k_ref, v_ref, o_ref, vt_ref, *, tq, tk):
    i = pl.program_id(1)
    wq = ML_QK_PAD

    @pl.when(i == 0)
    def _():
        _transpose_into(vt_ref, v_ref, tk)

    n_tiles = (i * tq + tq - 1) // tk + 1
    n_full = (i * tq + 1) // tk
    q_idx = i * tq + lax.broadcasted_iota(jnp.int32, (tk, tq), 1)
    k_off = lax.broadcasted_iota(jnp.int32, (tk, tq), 0)
    qh = [q_ref[:, h * wq:(h + 1) * wq] for h in range(ML_HEADS)]

    def tiles(j, h):
        off = pl.multiple_of(j * tk, tk)
        return (k_ref[pl.ds(off, tk), h * wq:(h + 1) * wq],
                vt_ref[h * ML_V:(h + 1) * ML_V, pl.ds(off, tk)], off)

    heads = range(ML_HEADS)

    def full_body(j, st):
        kv = [tiles(j, h) for h in heads]
        s = [_dot_t(kv[h][0], qh[h]) for h in heads]
        return _softmax_tiles(s, [0.0] * ML_HEADS, st, [kv[h][1] for h in heads])

    def diag_body(j, st):
        kv = [tiles(j, h) for h in heads]
        keep = q_idx >= kv[0][2] + k_off
        s = [jnp.where(keep, _dot_t(kv[h][0], qh[h]), NEG_BIG) for h in heads]
        return _softmax_tiles(s, [0.0] * ML_HEADS, st, [kv[h][1] for h in heads])

    st = tuple((jnp.full((1, tq), NEG_BIG, F32), jnp.zeros((1, tq), F32), jnp.zeros((ML_V, tq), F32))
               for _ in range(ML_HEADS))
    st = lax.fori_loop(0, n_full, full_body, st)
    st = lax.fori_loop(n_full, n_tiles, diag_body, st)
    for h in range(ML_HEADS):
        o_ref[:, h * ML_V:(h + 1) * ML_V] = (st[h][2] / st[h][1]).T.astype(o_ref.dtype)


def mla_attention(qf, kf, v, batch, *, tq=ATTN_TILE):
    tokens = qf.shape[0]
    seq = tokens // batch
    tq = min(tq, seq)
    tk = tq
    nq = seq // tq
    wide = qf.shape[1]
    return pl.pallas_call(
        functools.partial(_mla_attn_kernel, tq=tq, tk=tk),
        grid=(batch, nq),
        in_specs=[pl.BlockSpec((tq, wide), lambda b, i: (b * nq + i, 0)),
                  pl.BlockSpec((seq, wide), lambda b, i: (b, 0)),
                  pl.BlockSpec((seq, ML_DIM), lambda b, i: (b, 0))],
        out_specs=pl.BlockSpec((tq, ML_DIM), lambda b, i: (b * nq + i, 0)),
        out_shape=jax.ShapeDtypeStruct((tokens, ML_DIM), BF16),
        scratch_shapes=[pltpu.VMEM((ML_DIM, seq), BF16)],
        compiler_params=_cparams(("arbitrary", "arbitrary")),
        name="mla_attention",
    )(qf, kf, v)


def _cross_kernel(x_ref, nw_ref, wq_ref, kv_ref, wo_ref, o_ref):
    x = x_ref[...]
    q = _dot(_rms(x, nw_ref[...], NORM_EPS).astype(BF16), wq_ref[...])
    kv = kv_ref[...]
    scale = CA_HEAD_DIM ** -0.5
    outs = []
    for hh in range(CA_HEADS):
        sl = slice(hh * CA_HEAD_DIM, (hh + 1) * CA_HEAD_DIM)
        s = _dot_t(q[:, sl].astype(BF16), kv[:, sl]) * scale
        p = jnp.exp(s - jnp.max(s, axis=-1, keepdims=True))
        p = p / jnp.sum(p, axis=-1, keepdims=True)
        outs.append(_dot(p.astype(BF16), kv[:, CA_DIM + hh * CA_HEAD_DIM:CA_DIM + (hh + 1) * CA_HEAD_DIM]))
    o = jnp.concatenate(outs, axis=1).astype(BF16)
    o_ref[...] = x + _dot(o, wo_ref[...])


def cross_block(x, batch, norm_w, wq, kv, wo, *, tq=512):
    tokens, d = x.shape
    seq = tokens // batch
    tq = min(tq, seq)
    nq = seq // tq
    mem_len = kv.shape[0] // batch
    return pl.pallas_call(
        _cross_kernel,
        grid=(batch, nq),
        in_specs=[pl.BlockSpec((tq, d), lambda b, i: (b * nq + i, 0)),
                  pl.BlockSpec((1, d), lambda b, i: (0, 0)),
                  pl.BlockSpec(wq.shape, lambda b, i: (0, 0)),
                  pl.BlockSpec((mem_len, 2 * CA_DIM), lambda b, i: (b, 0)),
                  pl.BlockSpec(wo.shape, lambda b, i: (0, 0))],
        out_specs=pl.BlockSpec((tq, d), lambda b, i: (b * nq + i, 0)),
        out_shape=jax.ShapeDtypeStruct((tokens, d), F32),
        compiler_params=_cparams(("parallel", "parallel")),
        name="cross_block",
    )(x, norm_w.reshape(1, d), wq, kv, wo)


def _router_kernel(x_ref, nw_ref, wr_ref, br_ref, h_ref, sel_ref):
    h = _rms(x_ref[...], nw_ref[...], NORM_EPS)
    h_ref[...] = h.astype(h_ref.dtype)
    logits = _dot_x3(h, wr_ref[...])
    biased = logits + br_ref[...]
    lane = lax.broadcasted_iota(jnp.int32, logits.shape, 1)
    big = jnp.int32(LANES)

    def first_argmax(vals):
        mx = jnp.max(vals, axis=-1, keepdims=True)
        return jnp.min(jnp.where(vals == mx, lane, big), axis=-1, keepdims=True)

    def pick(vals, idx):
        return jnp.sum(jnp.where(lane == idx, vals, 0.0), axis=-1, keepdims=True)

    is_group = (lane >= MOE_EXPERTS) & (lane < MOE_EXPERTS + MOE_GROUPS)
    gl = jnp.where(is_group, logits, NEG_BIG)
    ge = jnp.exp(gl - jnp.max(gl, axis=-1, keepdims=True))
    gp = ge / jnp.sum(ge, axis=-1, keepdims=True)
    g_lane = first_argmax(jnp.where(is_group, biased, NEG_BIG))
    p_group = pick(gp, g_lane)
    lo = (g_lane - MOE_EXPERTS) * MOE_PER_GROUP
    in_group = (lane >= lo) & (lane < lo + MOE_PER_GROUP)
    eb = jnp.where(in_group, biased, NEG_BIG)
    i1 = first_argmax(eb)
    i2 = first_argmax(jnp.where(lane == i1, NEG_BIG, eb))
    l1, l2 = pick(logits, i1), pick(logits, i2)
    mx = jnp.maximum(l1, l2)
    e1, e2 = jnp.exp(l1 - mx), jnp.exp(l2 - mx)
    w1, w2 = e1 / (e1 + e2), e2 / (e1 + e2)
    sel_ref[...] = jnp.where(lane == SEL_E1, i1.astype(F32),
                             jnp.where(lane == SEL_E2, i2.astype(F32),
                                       jnp.where(lane == SEL_G1, w1 * p_group,
                                                 jnp.where(lane == SEL_G2, w2 * p_group, 0.0))))


SEL_E1, SEL_E2, SEL_G1, SEL_G2 = 0, 1, 2, 3


def moe_router(x, norm_w, w_router, b_router, *, tm=512):
    tokens, d = x.shape
    tm = min(tm, tokens)
    return pl.pallas_call(
        _router_kernel,
        grid=(tokens // tm,),
        in_specs=[pl.BlockSpec((tm, d), lambda i: (i, 0)),
                  pl.BlockSpec((1, d), lambda i: (0, 0)),
                  pl.BlockSpec((d, LANES), lambda i: (0, 0)),
                  pl.BlockSpec((1, LANES), lambda i: (0, 0))],
        out_specs=[pl.BlockSpec((tm, d), lambda i: (i, 0)),
                   pl.BlockSpec((tm, LANES), lambda i: (i, 0))],
        out_shape=[jax.ShapeDtypeStruct((tokens, d), F32),
                   jax.ShapeDtypeStruct((tokens, LANES), F32)],
        compiler_params=_cparams(("parallel",)),
        name="moe_router",
    )(x, norm_w.reshape(1, d), w_router, b_router)


def _moe_rank_kernel(sel_ref, ltri_ref, rank_ref, counts_ref, carry_ref):
    @pl.when(pl.program_id(0) == 0)
    def _():
        carry_ref[...] = jnp.zeros_like(carry_ref)

    sel = sel_ref[...]
    lane = lax.broadcasted_iota(jnp.int32, sel.shape, 1)
    lane_f = lane.astype(F32)
    oh1 = lane_f == sel[:, SEL_E1:SEL_E1 + 1]
    oh2 = lane_f == sel[:, SEL_E2:SEL_E2 + 1]
    f1, f2 = oh1.astype(F32), oh2.astype(F32)
    ltri = ltri_ref[...]
    before1 = _dot(ltri, f1.astype(BF16))
    before2 = _dot(ltri, f2.astype(BF16))
    c1 = jnp.sum(f1, axis=0, keepdims=True)
    c2 = jnp.sum(f2, axis=0, keepdims=True)
    carry = carry_ref[...]
    r1 = jnp.sum(jnp.where(oh1, before1 + carry, 0.0), axis=1, keepdims=True)
    r2 = jnp.sum(jnp.where(oh2, before2 + carry + c1, 0.0), axis=1, keepdims=True)
    rank_ref[...] = jnp.where(lane == SEL_E1, r1, jnp.where(lane == SEL_E2, r2, 0.0)).astype(jnp.int32)
    total = carry + c1 + c2
    carry_ref[...] = total
    counts_ref[...] = total.astype(jnp.int32)


def moe_rank(sel, *, tm=512):
    tokens = sel.shape[0]
    tm = min(tm, tokens)
    ltri = (jnp.arange(tm)[:, None] > jnp.arange(tm)[None, :]).astype(BF16)
    return pl.pallas_call(
        _moe_rank_kernel,
        grid=(tokens // tm,),
        in_specs=[pl.BlockSpec((tm, LANES), lambda i: (i, 0)),
                  pl.BlockSpec((tm, tm), lambda i: (0, 0))],
        out_specs=[pl.BlockSpec((tm, LANES), lambda i: (i, 0)),
                   pl.BlockSpec((1, LANES), lambda i: (0, 0))],
        out_shape=[jax.ShapeDtypeStruct((tokens, LANES), jnp.int32),
                   jax.ShapeDtypeStruct((1, LANES), jnp.int32)],
        scratch_shapes=[pltpu.VMEM((1, LANES), F32)],
        compiler_params=_cparams(("arbitrary",)),
        name="moe_rank",
    )(sel, ltri)


def _row_copy(src_ref, src_row, dst_ref, dst_row, sem):
    return pltpu.make_async_copy(src_ref.at[pl.ds(src_row, 1)], dst_ref.at[pl.ds(dst_row, 1)], sem)


def _moe_dispatch_kernel(dest_ref, tail_ref, h_ref, xs_ref, zero_ref, sem, zsem, *, tm):
    base = pl.program_id(0) * (2 * tm)

    @pl.when(pl.program_id(0) == 0)
    def _():
        zero_ref[...] = jnp.zeros_like(zero_ref)

        def fill(tail):
            return pltpu.make_async_copy(zero_ref, xs_ref.at[pl.ds(pl.multiple_of(tail, 8), zero_ref.shape[0])], zsem)

        def start(e, carry):
            @pl.when(tail_ref[e] >= 0)
            def _():
                fill(tail_ref[e]).start()
            return carry

        def wait(e, carry):
            @pl.when(tail_ref[e] >= 0)
            def _():
                fill(tail_ref[e]).wait()
            return carry

        lax.fori_loop(0, tail_ref.shape[0], start, 0)
        lax.fori_loop(0, tail_ref.shape[0], wait, 0)

    def issue(r, carry):
        for s in range(2):
            _row_copy(h_ref, r, xs_ref, dest_ref[base + 2 * r + s], sem).start()
        return carry

    lax.fori_loop(0, tm, issue, 0, unroll=DMA_ISSUE_UNROLL)
    for s in range(2):
        pltpu.make_async_copy(h_ref, xs_ref.at[pl.ds(0, tm)], sem).wait()


def moe_dispatch(h, dest, tails, rows, row_tile, *, tm=256):
    tokens, d = h.shape
    tm = min(tm, tokens)
    grid_spec = pltpu.PrefetchScalarGridSpec(
        num_scalar_prefetch=2,
        grid=(tokens // tm,),
        in_specs=[pl.BlockSpec((tm, d), lambda i, *_: (i, 0))],
        out_specs=pl.BlockSpec(memory_space=pl.ANY),
        scratch_shapes=[pltpu.VMEM((row_tile, d), h.dtype), pltpu.SemaphoreType.DMA(()),
                        pltpu.SemaphoreType.DMA(())],
    )
    return pl.pallas_call(
        functools.partial(_moe_dispatch_kernel, tm=tm),
        grid_spec=grid_spec,
        out_shape=jax.ShapeDtypeStruct((rows, d), h.dtype),
        compiler_params=_cparams(("arbitrary",)),
        name="moe_dispatch",
    )(dest, tails, h)


def _moe_expert_kernel(te_ref, nused_ref, xs_ref, wg_ref, wu_ref, wd_ref, ys_ref, wgb, wub, wdb):
    r = pl.program_id(0)
    used = r < nused_ref[0]
    changed = (r == 0) | (te_ref[r] != te_ref[jnp.maximum(r - 1, 0)])

    @pl.when(used & changed)
    def _():
        wgb[...] = wg_ref[...].astype(BF16)
        wub[...] = wu_ref[...].astype(BF16)
        wdb[...] = wd_ref[...].astype(BF16)

    @pl.when(used)
    def _():
        x = xs_ref[...].astype(BF16)
        gate_pre = _dot(x, wgb[...])
        hid = (gate_pre * jax.nn.sigmoid(gate_pre)) * _dot(x, wub[...])
        ys_ref[...] = _dot(hid.astype(BF16), wdb[...])

    @pl.when(jnp.logical_not(used))
    def _():
        ys_ref[...] = jnp.zeros_like(ys_ref)


def moe_experts(xs, tile_expert, n_used, wg, wu, wd, layer, *, tm):
    rows, d = xs.shape
    de = wg.shape[-1]
    grid_spec = pltpu.PrefetchScalarGridSpec(
        num_scalar_prefetch=2,
        grid=(rows // tm,),
        in_specs=[pl.BlockSpec((tm, d), lambda r, te, nu: (jnp.minimum(r, nu[0] - 1), 0)),
                  pl.BlockSpec((None, None, d, de), lambda r, te, nu: (layer, te[r], 0, 0)),
                  pl.BlockSpec((None, None, d, de), lambda r, te, nu: (layer, te[r], 0, 0)),
                  pl.BlockSpec((None, None, de, d), lambda r, te, nu: (layer, te[r], 0, 0))],
        out_specs=pl.BlockSpec((tm, d), lambda r, te, nu: (r, 0)),
        scratch_shapes=[pltpu.VMEM((d, de), BF16), pltpu.VMEM((d, de), BF16), pltpu.VMEM((de, d), BF16)],
    )
    return pl.pallas_call(
        _moe_expert_kernel,
        grid_spec=grid_spec,
        out_shape=jax.ShapeDtypeStruct((rows, d), F32),
        compiler_params=_cparams(("arbitrary",)),
        name="moe_experts",
    )(tile_expert, n_used, xs, wg, wu, wd)


def _moe_combine_kernel(dest_ref, x_ref, sel_ref, nw_ref, ys_ref, o_ref, buf_ref, sem, *, tm, normalize):
    step = pl.program_id(0)
    slot = step % 2

    def gather(tile, buf_slot):
        base = tile * (2 * tm)

        def issue(r, carry):
            for s in range(2):
                _row_copy(ys_ref, dest_ref[base + 2 * r + s], buf_ref.at[buf_slot, s], r, sem.at[buf_slot]).start()
            return carry

        lax.fori_loop(0, tm, issue, 0, unroll=DMA_ISSUE_UNROLL)

    @pl.when(step == 0)
    def _():
        gather(0, 0)

    @pl.when(step + 1 < pl.num_programs(0))
    def _():
        gather(step + 1, 1 - slot)

    for s in range(2):
        pltpu.make_async_copy(ys_ref.at[pl.ds(0, tm)], buf_ref.at[slot, s], sem.at[slot]).wait()
    sel = sel_ref[...]
    out = x_ref[...] + sel[:, SEL_G1:SEL_G1 + 1] * buf_ref[slot, 0] + sel[:, SEL_G2:SEL_G2 + 1] * buf_ref[slot, 1]
    o_ref[...] = _rms(out, nw_ref[...], NORM_EPS) if normalize else out


def moe_combine(x, sel, ys, dest, final_norm=None, *, tm=256):
    tokens, d = x.shape
    tm = min(tm, tokens)
    normalize = final_norm is not None
    nw = (final_norm if normalize else jnp.ones((d,), F32)).reshape(1, d)
    grid_spec = pltpu.PrefetchScalarGridSpec(
        num_scalar_prefetch=1,
        grid=(tokens // tm,),
        in_specs=[pl.BlockSpec((tm, d), lambda i, *_: (i, 0)),
                  pl.BlockSpec((tm, LANES), lambda i, *_: (i, 0)),
                  pl.BlockSpec((1, d), lambda i, *_: (0, 0)),
                  pl.BlockSpec(memory_space=pl.ANY)],
        out_specs=pl.BlockSpec((tm, d), lambda i, *_: (i, 0)),
        scratch_shapes=[pltpu.VMEM((2, 2, tm, d), F32), pltpu.SemaphoreType.DMA((2,))],
    )
    return pl.pallas_call(
        functools.partial(_moe_combine_kernel, tm=tm, normalize=normalize),
        grid_spec=grid_spec,
        out_shape=jax.ShapeDtypeStruct((tokens, d), F32),
        compiler_params=_cparams(("arbitrary",)),
        name="moe_combine",
    )(dest, x, sel, nw, ys)


def moe_block(x, norm_w, w_router, b_router, wg, wu, wd, layer, final_norm=None, *, tm=MOE_ROW_TILE):
    tokens, d = x.shape
    n_exp = wg.shape[1]
    h, sel = moe_router(x, norm_w, w_router, b_router)
    rank, counts = moe_rank(sel)
    padded = (counts[0, :n_exp] + (tm - 1)) // tm * tm
    ends = jnp.cumsum(padded)
    starts = ends - padded
    experts = sel[:, SEL_E1:SEL_E2 + 1].astype(jnp.int32)
    dest = (starts[experts] + rank[:, SEL_E1:SEL_E2 + 1]).reshape(-1)
    rows = 2 * tokens + n_exp * tm
    tile_start = jnp.arange(rows // tm, dtype=jnp.int32) * tm
    tile_expert = jnp.minimum(jnp.sum(tile_start[:, None] >= ends[None, :], axis=1), n_exp - 1).astype(jnp.int32)
    n_used = (ends[-1] // tm).astype(jnp.int32).reshape(1)
    tails = jnp.concatenate([jnp.where(padded > 0, ends - tm, -1),
                             jnp.where(tile_start >= ends[-1], tile_start, -1)]).astype(jnp.int32)
    xs = moe_dispatch(h, dest, tails, rows, tm)
    ys = moe_experts(xs, tile_expert, n_used, wg, wu, wd, layer, tm=tm)
    return moe_combine(x, sel, ys, dest, final_norm)


def _proj_weights(w_in_l, w_vres_l):
    d = w_in_l.shape[0]
    mla0 = RW_COLS + DF_COLS
    vres = jnp.zeros((d, RW_V_RANK), F32) if w_vres_l is None else w_vres_l
    part_f = [w_in_l[:, :RW_COLS],
              vres, jnp.zeros((d, LANES - RW_V_RANK), F32),
              w_in_l[:, mla0 + ML_Q_RANK + ML_KV_RANK:mla0 + ML_COLS], jnp.zeros((d, LANES - ML_ROPE), F32)]
    part_a = [w_in_l[:, RW_COLS:RW_COLS + DF_COLS],
              w_in_l[:, mla0 + ML_Q_RANK:mla0 + ML_Q_RANK + ML_KV_RANK],
              jnp.zeros((d, OFF_MQ - OFF_MKV - ML_KV_RANK), F32),
              w_in_l[:, mla0:mla0 + ML_Q_RANK]]
    return jnp.concatenate(part_f, axis=1).astype(BF16), jnp.concatenate(part_a, axis=1).astype(BF16)


def _pad_rows(w, rows, at=0):
    out = jnp.zeros((rows, w.shape[1]), w.dtype)
    return lax.dynamic_update_slice(out, w, (at, 0))


def kernel(x, mem, positions, rel_bias, final_norm, norm_mix, w_in, w_in_vres, w_out, tm_mu, tm_mu_vres, tm_w0, tm_w2, tm_a0, tm_a2, tm_v0, tm_v2, tm_g2, tm_k_k, tm_k_a, tm_r_k, tm_ln_w, tm_ln_b, da_lq1, da_lk1, da_lq2, da_lk2, da_subln, mla_q_norm, mla_wq_b, mla_kv_norm, mla_wkv_b, norm_cross, norm_mem, ca_wq, ca_wkv, ca_wo, norm_ffn, moe_w_group, moe_b_group, moe_w_expert, moe_b_expert, moe_w_gate, moe_w_up, moe_w_down):
    batch, seq, d = x.shape
    tokens = batch * seq
    depth = norm_mix.shape[0]
    xf = x.reshape(tokens, d)
    memf = mem.reshape(-1, d)
    positions = positions.astype(jnp.int32)

    head_of_lane = jnp.arange(RW_DIM) // RW_HEAD_DIM
    seg = (head_of_lane[:, None] == jnp.arange(LANES)[None, :]).astype(BF16)
    seg_t = seg.T
    row = lambda v: v.reshape(1, -1)

    v_first = None
    for l in range(depth):
        w_f, w_a = _proj_weights(w_in[l], None if l == 0 else w_in_vres[l - 1])
        proj = norm_matmul(xf, norm_mix[l], w_f, tm=1024, tn=PROJ_F_COLS // 4)
        pa = norm_matmul(xf, norm_mix[l], w_a, out_dtype=BF16, tm=1024, tn=PROJ_A_COLS // 2)

        mu = tm_mu[l]
        prm = dict(mu_r=row(mu[:RW_DIM]), mu_k=row(mu[RW_DIM:2 * RW_DIM]), mu_v=row(mu[2 * RW_DIM:3 * RW_DIM]),
                   mu_l=row(mu[3 * RW_DIM:]), w0=row(tm_w0[l]), a0=row(tm_a0[l]),
                   w2=_pad_rows(tm_w2[l], LANES, 0), a2=_pad_rows(tm_a2[l], LANES, RW_W_RANK),
                   g2=tm_g2[l].astype(BF16), k_k=row(tm_k_k[l]), k_a=row(tm_k_a[l]), r_k=row(tm_r_k[l]),
                   seg=seg, seg_t=seg_t)
        if l > 0:
            prm.update(mu_vr=jnp.pad(row(tm_mu_vres[l - 1]), ((0, 0), (0, LANES - RW_V_RANK))),
                       v0=row(tm_v0[l - 1]), v2=_pad_rows(tm_v2[l - 1], LANES, 0))
        r, lw, k, v, kap, beta, gate, bonus = rwkv_prep(proj, batch, v_first, prm)
        if l == 0:
            v_first = v
        o = rwkv_scan(r, lw, k, v, kap, beta, batch)
        y_a = rwkv_post(o, bonus, gate, tm_ln_w[l], tm_ln_b[l], seg, seg_t)

        lambda_init = 0.8 - 0.6 * math.exp(-0.3 * l)
        lam = (jnp.exp(jnp.sum(da_lq1[l] * da_lk1[l])) - jnp.exp(jnp.sum(da_lq2[l] * da_lk2[l])) + lambda_init)
        y_b = diff_attention(pa, positions, rel_bias, lam, lambda_init, da_subln[l])

        wq = mla_wq_b[l].reshape(ML_Q_RANK, ML_HEADS, ML_NOPE + ML_ROPE)
        wq_pe = jnp.pad(wq[:, :, ML_NOPE:], ((0, 0), (0, 0), (0, LANES - ML_ROPE)))
        wq_all = jnp.concatenate([wq[:, :, :ML_NOPE].reshape(ML_Q_RANK, -1),
                                  wq_pe.reshape(ML_Q_RANK, -1)], axis=1).astype(BF16)
        wkv = mla_wkv_b[l].reshape(ML_KV_RANK, ML_HEADS, ML_NOPE + ML_V)
        wkv = jnp.concatenate([wkv[:, :, :ML_NOPE].reshape(ML_KV_RANK, -1),
                               wkv[:, :, ML_NOPE:].reshape(ML_KV_RANK, -1)], axis=1).astype(BF16)
        qf, kf, v_mla = mla_prep(pa, proj, positions, mla_q_norm[l], mla_kv_norm[l], wq_all, wkv)
        y_c = mla_attention(qf, kf, v_mla, batch)

        wo = w_out[l].astype(BF16)
        xf = matmul_res([y_a, y_b, y_c],
                        [wo[:RW_DIM], wo[RW_DIM:RW_DIM + DF_DIM], wo[RW_DIM + DF_DIM:]], xf)

        kv_mem = norm_matmul(memf, norm_mem[l], ca_wkv[l].astype(BF16), out_dtype=BF16)
        xf = cross_block(xf, batch, norm_cross[l], ca_wq[l].astype(BF16), kv_mem, ca_wo[l].astype(BF16))

        w_router = jnp.concatenate(
            [moe_w_expert[l], moe_w_group[l], jnp.zeros((d, LANES - MOE_EXPERTS - MOE_GROUPS), F32)], axis=1)
        b_router = jnp.concatenate(
            [moe_b_expert[l], moe_b_group[l], jnp.zeros((LANES - MOE_EXPERTS - MOE_GROUPS,), F32)]).reshape(1, LANES)
        xf = moe_block(xf, norm_ffn[l], w_router, b_router, moe_w_gate, moe_w_up, moe_w_down, l,
                       final_norm if l == depth - 1 else None)

    return xf.reshape(batch, seq, d)
```

```python
import functools
import math

import jax
import jax.numpy as jnp
from jax import lax
from jax.experimental import pallas as pl
from jax.experimental.pallas import tpu as pltpu

F32 = jnp.float32
BF16 = jnp.bfloat16

NORM_EPS = 1e-6
ROPE_THETA = 10000.0

RW_HEADS = 16
RW_HEAD_DIM = 64
RW_DIM = RW_HEADS * RW_HEAD_DIM
RW_W_RANK = 64
RW_A_RANK = 64
RW_G_RANK = 128
RW_V_RANK = 32
RW_LORA = RW_W_RANK + RW_A_RANK + RW_G_RANK
RW_LN_EPS = 64e-5
RW_COLS = 3 * RW_DIM + RW_LORA

DF_HEADS = 4
DF_HEAD_DIM = 64
DF_V_DIM = 2 * DF_HEAD_DIM
DF_QK = DF_HEADS * 2 * DF_HEAD_DIM
DF_DIM = DF_HEADS * DF_V_DIM
DF_COLS = 2 * DF_QK + DF_DIM
DF_SUBLN_EPS = 1e-5

ML_HEADS = 4
ML_Q_RANK = 384
ML_KV_RANK = 256
ML_NOPE = 128
ML_ROPE = 64
ML_V = 128
ML_DIM = ML_HEADS * ML_V
ML_COLS = ML_Q_RANK + ML_KV_RANK + ML_ROPE

REL_BUCKETS = 32
REL_MAX_DIST = 128

CA_HEADS = 4
CA_HEAD_DIM = 128
CA_DIM = CA_HEADS * CA_HEAD_DIM

MOE_GROUPS = 4
MOE_PER_GROUP = 8
MOE_EXPERTS = MOE_GROUPS * MOE_PER_GROUP

LANES = 128
SCAN_CHUNK = 64
SCAN_GROUP = 4
SCAN_BATCHES = 2
ATTN_TILE = 512
DMA_ISSUE_UNROLL = 8
MOE_ROW_TILE = 256
VMEM_LIMIT = 56 * 1024 * 1024
NEG_BIG = -1e30

LOG2E = 1.4426950408889634

OFF_LORA = 0
OFF_VRES = OFF_LORA + RW_LORA
OFF_KPE = OFF_VRES + LANES
PROJ_F_COLS = OFF_KPE + LANES
OFF_R = 0
OFF_K = RW_DIM
OFF_V = 2 * RW_DIM
OFF_DQ = 3 * RW_DIM
OFF_DK = OFF_DQ + DF_QK
OFF_DV = OFF_DK + DF_QK
OFF_MKV = OFF_DV + DF_DIM
OFF_MQ = 13 * ML_Q_RANK
PROJ_A_COLS = OFF_MQ + ML_Q_RANK


def _cparams(sem, vmem=VMEM_LIMIT, flags=None):
    return pltpu.CompilerParams(dimension_semantics=sem, vmem_limit_bytes=vmem, flags=flags)


def _dot(a, b):
    return jnp.dot(a, b, preferred_element_type=F32)


def _dot_t(a, b):
    return lax.dot_general(a, b, (((1,), (1,)), ((), ())), preferred_element_type=F32)


def _split3(x):
    hi = x.astype(BF16)
    r1 = x - hi.astype(F32)
    mid = r1.astype(BF16)
    lo = (r1 - mid.astype(F32)).astype(BF16)
    return hi, mid, lo


def _dot_rhs01(x, ones_bf16):
    hi = x.astype(BF16)
    lo = (x - hi.astype(F32)).astype(BF16)
    return _dot(hi, ones_bf16) + _dot(lo, ones_bf16)


def _dot_x3(a, b):
    ah = a.astype(BF16)
    al = (a - ah.astype(F32)).astype(BF16)
    bh = b.astype(BF16)
    bl = (b - bh.astype(F32)).astype(BF16)
    return _dot(ah, bh) + _dot(ah, bl) + _dot(al, bh)


def _rms(x, w, eps):
    ms = jnp.mean(x * x, axis=-1, keepdims=True)
    return x * lax.rsqrt(ms + eps) * w


def _norm_matmul_kernel(x_ref, nw_ref, w_ref, o_ref, xn_ref, *, eps):
    @pl.when(pl.program_id(1) == 0)
    def _():
        xn_ref[...] = _rms(x_ref[...], nw_ref[...], eps).astype(BF16)

    o_ref[...] = _dot(xn_ref[...], w_ref[...]).astype(o_ref.dtype)


def norm_matmul(x, nw, w, *, out_dtype=F32, tm=512, tn=None, eps=NORM_EPS):
    m, d = x.shape
    n = w.shape[1]
    tm = min(tm, m)
    tn = n if tn is None else tn
    return pl.pallas_call(
        functools.partial(_norm_matmul_kernel, eps=eps),
        grid=(m // tm, n // tn),
        in_specs=[pl.BlockSpec((tm, d), lambda i, j: (i, 0)),
                  pl.BlockSpec((1, d), lambda i, j: (0, 0)),
                  pl.BlockSpec((d, tn), lambda i, j: (0, j))],
        out_specs=pl.BlockSpec((tm, tn), lambda i, j: (i, j)),
        out_shape=jax.ShapeDtypeStruct((m, n), out_dtype),
        scratch_shapes=[pltpu.VMEM((tm, d), BF16)],
        compiler_params=_cparams(("parallel", "arbitrary")),
        name="norm_matmul",
    )(x, nw.reshape(1, d), w)


def _matmul_res_kernel(*refs, n_a):
    a_refs, w_refs = refs[:n_a], refs[n_a:2 * n_a]
    res_ref, o_ref = refs[2 * n_a], refs[2 * n_a + 1]
    acc = res_ref[...]
    for a_ref, w_ref in zip(a_refs, w_refs):
        acc = acc + _dot(a_ref[...].astype(BF16), w_ref[...])
    o_ref[...] = acc


def matmul_res(a_list, w_list, res, *, tm=1024, tn=1024):
    m, n = res.shape
    tm = min(tm, m)
    tn = min(tn, n)
    n_a = len(a_list)
    in_specs = ([pl.BlockSpec((tm, a.shape[1]), lambda i, j: (i, 0)) for a in a_list]
                + [pl.BlockSpec((w.shape[0], tn), lambda i, j: (0, j)) for w in w_list]
                + [pl.BlockSpec((tm, tn), lambda i, j: (i, j))])
    return pl.pallas_call(
        functools.partial(_matmul_res_kernel, n_a=n_a),
        grid=(m // tm, n // tn),
        in_specs=in_specs,
        out_specs=pl.BlockSpec((tm, tn), lambda i, j: (i, j)),
        out_shape=jax.ShapeDtypeStruct((m, n), F32),
        compiler_params=_cparams(("parallel", "arbitrary")),
        name="matmul_res",
    )(*a_list, *w_list, res)


def _softplus(z):
    return jnp.maximum(z, 0.0) + jnp.log(1.0 + jnp.exp(-jnp.abs(z)))


def _rwkv_prep_kernel(*refs, has_vres):
    if has_vres:
        (pr_ref, pk_ref, pv_ref, pl_ref, pvr_ref, vfirst_ref,
         mu_r, mu_k, mu_v, mu_l, mu_vr, w0, w2, a0, a2, g2, v0, v2,
         k_k, k_a, r_k, seg, seg_t,
         r_o, lw_o, k_o, v_o, kap_o, beta_o, g_o, bonus_o,
         last_r, last_k, last_v, last_l, last_vr) = refs
    else:
        (pr_ref, pk_ref, pv_ref, pl_ref,
         mu_r, mu_k, mu_v, mu_l, w0, w2, a0, a2, g2,
         k_k, k_a, r_k, seg, seg_t,
         r_o, lw_o, k_o, v_o, kap_o, beta_o, g_o, bonus_o,
         last_r, last_k, last_v, last_l) = refs
    t = pl.program_id(1)

    def shifted(p_ref, last_ref, mu_ref):
        p = p_ref[...].astype(F32)
        n = p.shape[0]
        carried = jnp.where(t == 0, 0.0, last_ref[0:1, :])
        row = lax.broadcasted_iota(jnp.int32, p.shape, 0)
        prev = jnp.where(row == 0, carried, pltpu.roll(p, 1, axis=0))
        last_ref[0:1, :] = p[n - 1:n, :]
        return p + mu_ref[...] * (prev - p)

    r = shifted(pr_ref, last_r, mu_r)
    k = shifted(pk_ref, last_k, mu_k)
    v = shifted(pv_ref, last_v, mu_v)
    lora = shifted(pl_ref, last_l, mu_l)
    wl = lora[:, :LANES]
    gl = lora[:, LANES:]

    lane = lax.broadcasted_iota(jnp.int32, wl.shape, 1)
    wl_t = jnp.where(lane < RW_W_RANK, jnp.tanh(wl), 0.0)
    al = jnp.where(lane >= RW_W_RANK, wl, 0.0)
    w_log = -_softplus(-(w0[...] + _dot_x3(wl_t, w2[...]))) - 0.5
    lw_o[...] = -jnp.exp(w_log)
    a = jax.nn.sigmoid(a0[...] + _dot_x3(al, a2[...]))
    g_o[...] = _dot(jax.nn.sigmoid(gl).astype(BF16), g2[...]).astype(g_o.dtype)

    segm, segm_t = seg[...], seg_t[...]

    def head_sum(x):
        return _dot_rhs01(_dot_rhs01(x, segm), segm_t)

    kk = k * k_k[...]
    kk = kk * lax.rsqrt(jnp.maximum(head_sum(kk * kk), 1e-24))
    k = k * (1.0 + (a - 1.0) * k_a[...])
    if has_vres:
        vr = shifted(pvr_ref, last_vr, mu_vr)
        mix = jax.nn.sigmoid(v0[...] + _dot_x3(vr, v2[...]))
        v = v + (vfirst_ref[...] - v) * mix
    r_o[...] = r.astype(r_o.dtype)
    k_o[...] = k.astype(k_o.dtype)
    v_o[...] = v.astype(v_o.dtype)
    kap_o[...] = kk.astype(kap_o.dtype)
    beta_o[...] = (kk * a).astype(beta_o.dtype)
    bonus_o[...] = (head_sum(r * k * r_k[...]) * v).astype(bonus_o.dtype)


def rwkv_prep(pa, proj, batch, vfirst, prm, *, tt=256):
    tokens = proj.shape[0]
    seq = tokens // batch
    tt = min(tt, seq)
    nt = seq // tt
    has_vres = vfirst is not None
    d = RW_DIM

    def rows(width, col):
        return pl.BlockSpec((tt, width), lambda b, t, col=col: (b * nt + t, col))

    def full(shape):
        return pl.BlockSpec(shape, lambda b, t: (0, 0))

    in_specs = [rows(d, OFF_R // d), rows(d, OFF_K // d), rows(d, OFF_V // d),
                rows(RW_LORA, OFF_LORA // RW_LORA)]
    args = [pa, pa, pa, proj]
    if has_vres:
        in_specs += [rows(LANES, OFF_VRES // LANES), rows(d, 0)]
        args += [proj, vfirst]
    names = ["mu_r", "mu_k", "mu_v", "mu_l"] + (["mu_vr"] if has_vres else []) + ["w0", "w2", "a0", "a2", "g2"]
    names += (["v0", "v2"] if has_vres else []) + ["k_k", "k_a", "r_k", "seg", "seg_t"]
    for nm in names:
        in_specs.append(full(prm[nm].shape))
        args.append(prm[nm])
    out_spec = pl.BlockSpec((tt, d), lambda b, t: (b * nt + t, 0))
    scratch = [pltpu.VMEM((8, d), F32)] * 3 + [pltpu.VMEM((8, RW_LORA), F32)]
    if has_vres:
        scratch.append(pltpu.VMEM((8, LANES), F32))
    return pl.pallas_call(
        functools.partial(_rwkv_prep_kernel, has_vres=has_vres),
        grid=(batch, nt),
        in_specs=in_specs,
        out_specs=[out_spec] * 8,
        out_shape=[jax.ShapeDtypeStruct((tokens, d), F32 if i == 1 else BF16) for i in range(8)],
        scratch_shapes=scratch,
        compiler_params=_cparams(("arbitrary", "arbitrary")),
        name="rwkv_prep",
    )(*args)


def _rwkv_scan_kernel(r_ref, lw_ref, k_ref, v_ref, kap_ref, beta_ref, tril_ref, bmask_ref,
                      o_ref, ht_ref):
    @pl.when(pl.program_id(1) == 0)
    def _():
        ht_ref[...] = jnp.zeros_like(ht_ref)

    n_batch, c, d = lw_ref.shape
    w = ht_ref.shape[1]
    g = w // RW_HEAD_DIM
    bmask = bmask_ref[...]
    bmask_b = bmask.astype(BF16)
    tril3 = tril_ref[...]
    t_idx = lax.broadcasted_iota(jnp.int32, (c, w), 0)
    s_idx = lax.broadcasted_iota(jnp.int32, (c, w), 1) % c
    strict = t_idx > s_idx
    incl = t_idx >= s_idx
    n_sq = int(math.log2(c))

    def stack(x):
        return jnp.concatenate([x.astype(BF16)] * g, axis=0) * bmask_b

    sls = [(bi, slice(None), slice(lo, lo + w)) for bi in range(n_batch) for lo in range(0, d, w)]
    groups = range(len(sls))
    lw = [lw_ref[sl] for sl in sls]
    cum = [_dot(tril3, jnp.concatenate(_split3(x), axis=0)) for x in lw]
    total = [x[c - 1:c, :] for x in cum]
    ar = [jnp.concatenate([-kap_ref[sls[gi]] * jnp.exp(cum[gi] - lw[gi]), r_ref[sls[gi]] * jnp.exp(cum[gi])],
                          axis=0).astype(BF16) for gi in groups]
    p_inv = [jnp.exp(-x) for x in cum]
    b_s = [stack(beta_ref[sls[gi]] * p_inv[gi]) for gi in groups]
    k_s = [stack(k_ref[sls[gi]] * p_inv[gi]) for gi in groups]
    v_n = [v_ref[sl] for sl in sls]
    v_s = [stack(x) for x in v_n]

    arb = [_dot_t(ar[gi], b_s[gi]) for gi in groups]
    ark = [_dot_t(ar[gi], k_s[gi]) for gi in groups]
    ab = [jnp.where(strict, m[:c], 0.0) for m in arb]
    rb = [jnp.where(incl, m[c:], 0.0).astype(BF16) for m in arb]
    akrk = [jnp.concatenate([jnp.where(strict, m[:c], 0.0), jnp.where(incl, m[c:], 0.0)], axis=0).astype(BF16)
            for m in ark]

    ht = [ht_ref[gi] for gi in groups]
    base = [_dot_t(ar[gi], ht[gi].astype(BF16)) + _dot(akrk[gi], v_s[gi]) for gi in groups]
    x = [m[:c] for m in base]
    lp = ab
    for i in range(n_sq):
        lpb = [m.astype(BF16) for m in lp]
        x = [x[gi] + _dot(lpb[gi], stack(x[gi])) for gi in groups]
        if i < n_sq - 1:
            lp = [_dot(lpb[gi], stack(lp[gi])) for gi in groups]
    for gi in groups:
        o_ref[sls[gi]] = base[gi][c:] + _dot(rb[gi], stack(x[gi]))

    for gi in groups:
        p_rem = jnp.exp(total[gi] - cum[gi])
        z = jnp.concatenate([beta_ref[sls[gi]] * p_rem, k_ref[sls[gi]] * p_rem], axis=0).astype(BF16)
        uv_t = jnp.concatenate([x[gi], v_n[gi].astype(F32)], axis=0).T.astype(BF16)
        ht_ref[gi] = ht[gi] * jnp.exp(total[gi]) + bmask * _dot(uv_t, z)


def rwkv_scan(r, lw, k, v, kap, beta, batch):
    tokens, d = r.shape
    seq = tokens // batch
    c = min(SCAN_CHUNK, seq)
    nc = seq // c
    gw = SCAN_GROUP * RW_HEAD_DIM
    rr = SCAN_GROUP * c
    assert c == RW_HEAD_DIM, "the stacking mask doubles as the head-block mask of the state"
    bb = math.gcd(batch, SCAN_BATCHES)
    tril = jnp.tile((jnp.arange(c)[:, None] >= jnp.arange(c)[None, :]).astype(BF16), (1, 3))
    bmask = (jnp.arange(rr)[:, None] // c == jnp.arange(gw)[None, :] // RW_HEAD_DIM).astype(F32)
    blk = pl.BlockSpec((bb, c, d), lambda b, i: (b, i, 0))
    as3d = lambda a: a.reshape(batch, seq, d)
    out = pl.pallas_call(
        _rwkv_scan_kernel,
        grid=(batch // bb, nc),
        in_specs=[blk] * 6 + [pl.BlockSpec((c, 3 * c), lambda b, i: (0, 0)),
                              pl.BlockSpec((rr, gw), lambda b, i: (0, 0))],
        out_specs=blk,
        out_shape=jax.ShapeDtypeStruct((batch, seq, d), F32),
        scratch_shapes=[pltpu.VMEM((bb * (d // gw), gw, gw), F32)],
        compiler_params=_cparams(("arbitrary", "arbitrary")),
        name="rwkv_scan",
    )(as3d(r), as3d(lw), as3d(k), as3d(v), as3d(kap), as3d(beta), tril, bmask)
    return out.reshape(tokens, d)


def _rwkv_post_kernel(o_ref, bonus_ref, g_ref, lnw_ref, lnb_ref, seg, seg_t, y_ref):
    segm, segm_t = seg[...], seg_t[...]

    def head_mean(x):
        return _dot_rhs01(_dot_rhs01(x, segm), segm_t) * (1.0 / RW_HEAD_DIM)

    o = o_ref[...]
    dlt = o - head_mean(o)
    var = head_mean(dlt * dlt)
    y = dlt * lax.rsqrt(var + RW_LN_EPS) * lnw_ref[...] + lnb_ref[...]
    y_ref[...] = ((y + bonus_ref[...]) * g_ref[...]).astype(y_ref.dtype)


def rwkv_post(o, bonus, g, ln_w, ln_b, seg, seg_t, *, tm=512):
    tokens, d = o.shape
    tm = min(tm, tokens)
    blk = pl.BlockSpec((tm, d), lambda i: (i, 0))
    vec = pl.BlockSpec((1, d), lambda i: (0, 0))
    return pl.pallas_call(
        _rwkv_post_kernel,
        grid=(tokens // tm,),
        in_specs=[blk, blk, blk, vec, vec,
                  pl.BlockSpec(seg.shape, lambda i: (0, 0)), pl.BlockSpec(seg_t.shape, lambda i: (0, 0))],
        out_specs=blk,
        out_shape=jax.ShapeDtypeStruct((tokens, d), BF16),
        compiler_params=_cparams(("parallel",)),
        name="rwkv_post",
    )(o, bonus, g, ln_w.reshape(1, d), ln_b.reshape(1, d), seg, seg_t)


def _t5_thresholds():
    max_exact = REL_BUCKETS // 2
    thr = list(range(1, max_exact))
    n = max_exact
    for bucket in range(max_exact, REL_BUCKETS):
        while True:
            large = max_exact + int(math.log(max(n, max_exact) / max_exact)
                                    / math.log(REL_MAX_DIST / max_exact) * (REL_BUCKETS - max_exact))
            if min(large, REL_BUCKETS - 1) >= bucket:
                break
            n += 1
        thr.append(n)
    return thr


T5_THRESHOLDS = _t5_thresholds()
T5_FAR = T5_THRESHOLDS[-1]


def _softmax_tiles(s_list, c_list, states, vt_list):
    stats = []
    for s_t, c, (m_old, l_old, _) in zip(s_list, c_list, states):
        m_new = jnp.maximum(m_old, jnp.max(s_t, axis=0, keepdims=True) + c)
        alpha = jnp.exp2(m_old - m_new)
        p_t = jnp.exp2(s_t - (m_new - c))
        stats.append((m_new, alpha, alpha * l_old + jnp.sum(p_t, axis=0, keepdims=True), p_t.astype(BF16)))
    return tuple((m_new, l_new, alpha * acc + _dot(vt, p_t))
                 for (m_new, alpha, l_new, p_t), (_, _, acc), vt in zip(stats, states, vt_list))


def _transpose_into(vt_ref, v_ref, chunk):
    seq = v_ref.shape[0]
    for c in range(seq // chunk):
        vt_ref[:, c * chunk:(c + 1) * chunk] = v_ref[c * chunk:(c + 1) * chunk, :].astype(F32).T.astype(BF16)


def _diff_attn_kernel(qfirst_ref, klast_ref, q_ref, k_ref, v_ref, qpos_ref, kpos_ref, subln_ref, table_ref, lam_ref,
                      o_ref, vt_ref, *, tq, tk, scale2, out_scale):
    b, i = pl.program_id(0), pl.program_id(1)
    nq = pl.num_programs(1)
    seq = k_ref.shape[0]
    nk = seq // tk
    w = DF_V_DIM

    @pl.when(i == 0)
    def _():
        _transpose_into(vt_ref, v_ref, tk)

    n_tiles = (i * tq + tq - 1) // tk + 1
    qf = qfirst_ref[b * nq + i]
    n_far = lax.while_loop(
        lambda j: (j * tk + tk - 1 <= i * tq) & (qf - klast_ref[b * nk + jnp.minimum(j, nk - 1)] >= T5_FAR),
        lambda j: j + 1, jnp.int32(0))

    dist = lax.broadcasted_iota(jnp.int32, (1, LANES), 1)
    qpos = qpos_ref[...]
    q_idx = i * tq + lax.broadcasted_iota(jnp.int32, (tk, tq), 1)
    k_off = lax.broadcasted_iota(jnp.int32, (tk, tq), 0)
    lane = lax.broadcasted_iota(jnp.int32, (tq, w), 1)

    bias_rows, c_far, qm = [], [], []
    for h in range(DF_HEADS):
        bias_vec = jnp.full((1, LANES), table_ref[h], F32)
        for bucket, thr in enumerate(T5_THRESHOLDS, start=1):
            bias_vec = jnp.where(dist >= thr, table_ref[bucket * DF_HEADS + h], bias_vec)
        bias_rows.append(jnp.broadcast_to(bias_vec * LOG2E, (tk, LANES)))
        c_far.append(table_ref[(REL_BUCKETS - 1) * DF_HEADS + h] * LOG2E)
        qh = q_ref[:, h * w:(h + 1) * w].astype(F32) * scale2
        qm.append([jnp.where((lane >= mi * DF_HEAD_DIM) & (lane < (mi + 1) * DF_HEAD_DIM), qh, 0.0).astype(BF16)
                   for mi in range(2)])

    def tiles(j, h):
        off = pl.multiple_of(j * tk, tk)
        return k_ref[pl.ds(off, tk), h * w:(h + 1) * w], vt_ref[h * w:(h + 1) * w, pl.ds(off, tk)], off

    chains = [(h, mi) for h in range(DF_HEADS) for mi in range(2)]

    def far_body(j, st):
        kv = [tiles(j, h) for h in range(DF_HEADS)]
        s = [_dot_t(kv[h][0], qm[h][mi]) for h, mi in chains]
        return _softmax_tiles(s, [c_far[h] for h, _ in chains], st, [kv[h][1] for h, _ in chains])

    def near_body(j, st):
        off = pl.multiple_of(j * tk, tk)
        n = jnp.clip(qpos - kpos_ref[pl.ds(off, tk), :], 0, LANES - 1)
        keep = q_idx >= off + k_off
        kv = [tiles(j, h) for h in range(DF_HEADS)]
        bias = [jnp.concatenate(
            [jnp.take_along_axis(bias_rows[h], n[:, cb * LANES:(cb + 1) * LANES], axis=1)
             for cb in range(tq // LANES)], axis=1) for h in range(DF_HEADS)]
        s = [jnp.where(keep, _dot_t(kv[h][0], qm[h][mi]) + bias[h], NEG_BIG) for h, mi in chains]
        return _softmax_tiles(s, [0.0] * len(chains), st, [kv[h][1] for h, _ in chains])

    init = tuple((jnp.full((1, tq), NEG_BIG, F32), jnp.zeros((1, tq), F32), jnp.zeros((w, tq), F32))
                 for _ in range(2 * DF_HEADS))
    st = lax.fori_loop(0, n_far, far_body, init)
    st = lax.fori_loop(n_far, n_tiles, near_body, st)
    for h in range(DF_HEADS):
        s0, s1 = st[2 * h], st[2 * h + 1]
        d_t = s0[2] / s0[1] - lam_ref[0] * (s1[2] / s1[1])
        ms = jnp.mean(d_t * d_t, axis=0, keepdims=True)
        y_t = d_t * lax.rsqrt(ms + DF_SUBLN_EPS) * (subln_ref[...] * out_scale)
        o_ref[:, h * w:(h + 1) * w] = y_t.T.astype(o_ref.dtype)


def diff_attention(pa, positions, rel_bias, lam, lambda_init, subln_w, *, tq=ATTN_TILE):
    batch, seq = positions.shape
    tokens = batch * seq
    tq = min(tq, seq)
    tk = tq
    nq, nk = seq // tq, seq // tk
    qfirst = positions[:, ::tq].reshape(-1)
    klast = positions[:, tk - 1::tk].reshape(-1)
    qpos = positions.reshape(batch, 1, seq)
    kpos = positions.reshape(batch, seq, 1)
    wd = DF_DIM
    grid_spec = pltpu.PrefetchScalarGridSpec(
        num_scalar_prefetch=2,
        grid=(batch, nq),
        in_specs=[pl.BlockSpec((tq, wd), lambda b, i, *_: (b * nq + i, OFF_DQ // wd)),
                  pl.BlockSpec((seq, wd), lambda b, i, *_: (b, OFF_DK // wd)),
                  pl.BlockSpec((seq, wd), lambda b, i, *_: (b, OFF_DV // wd)),
                  pl.BlockSpec((None, 1, tq), lambda b, i, *_: (b, 0, i)),
                  pl.BlockSpec((None, seq, 1), lambda b, i, *_: (b, 0, 0)),
                  pl.BlockSpec((DF_V_DIM, 1), lambda b, i, *_: (0, 0)),
                  pl.BlockSpec(memory_space=pltpu.SMEM),
                  pl.BlockSpec(memory_space=pltpu.SMEM)],
        out_specs=pl.BlockSpec((tq, wd), lambda b, i, *_: (b * nq + i, 0)),
        scratch_shapes=[pltpu.VMEM((wd, seq), BF16)],
    )
    return pl.pallas_call(
        functools.partial(_diff_attn_kernel, tq=tq, tk=tk, scale2=DF_HEAD_DIM ** -0.5 * LOG2E,
                          out_scale=1.0 - lambda_init),
        grid_spec=grid_spec,
        out_shape=jax.ShapeDtypeStruct((tokens, DF_DIM), BF16),
        compiler_params=_cparams(("arbitrary", "arbitrary")),
        name="diff_attention",
    )(qfirst, klast, pa, pa, pa, qpos, kpos, subln_w.reshape(DF_V_DIM, 1), rel_bias.reshape(-1), lam.reshape(1))


ML_QK_PAD = 2 * LANES


def _rope_block(x, cos, sin):
    half = ML_ROPE // 2
    lane = lax.broadcasted_iota(jnp.int32, x.shape, 1)
    rot = jnp.where(lane < half, -pltpu.roll(x, LANES - half, axis=1),
                    jnp.where(lane < ML_ROPE, pltpu.roll(x, half, axis=1), 0.0))
    return x * cos + rot * sin


def _mla_prep_kernel(mq_ref, mkv_ref, kpe_ref, pos_ref, qn_w, kvn_w, wq_ref, wkv_ref, freq_ref,
                     qf_o, kf_o, v_o, *, qscale):
    ang = pos_ref[...].astype(F32) * freq_ref[...]
    cos, sin = jnp.cos(ang), jnp.sin(ang)
    qc = _rms(mq_ref[...].astype(F32), qn_w[...], NORM_EPS).astype(BF16)
    q_all = _dot(qc, wq_ref[...]) * qscale
    kvc = _rms(mkv_ref[...].astype(F32), kvn_w[...], NORM_EPS).astype(BF16)
    kvb = _dot(kvc, wkv_ref[...])
    kpe = _rope_block(kpe_ref[...], cos, sin).astype(BF16)
    nope_w = ML_HEADS * ML_NOPE
    for h in range(ML_HEADS):
        lo = h * ML_QK_PAD
        qf_o[:, lo:lo + LANES] = q_all[:, h * LANES:(h + 1) * LANES].astype(BF16)
        qf_o[:, lo + LANES:lo + 2 * LANES] = _rope_block(
            q_all[:, nope_w + h * LANES:nope_w + (h + 1) * LANES], cos, sin).astype(BF16)
        kf_o[:, lo:lo + LANES] = kvb[:, h * LANES:(h + 1) * LANES].astype(BF16)
        kf_o[:, lo + LANES:lo + 2 * LANES] = kpe
    v_o[...] = kvb[:, nope_w:].astype(BF16)


def mla_prep(pa, pf, positions, q_norm, kv_norm, wq_all, wkv, *, tm=512):
    tokens = pa.shape[0]
    tm = min(tm, tokens)
    half = ML_ROPE // 2
    inv_freq = ROPE_THETA ** (-jnp.arange(half, dtype=F32) / half)
    freq = jnp.concatenate([inv_freq, inv_freq, jnp.zeros((LANES - ML_ROPE,), F32)]).reshape(1, LANES)

    def full(a):
        return pl.BlockSpec(a.shape, lambda i: (0, 0))

    qn_w = q_norm.reshape(1, -1)
    kvn_w = kv_norm.reshape(1, -1)
    wide = ML_HEADS * ML_QK_PAD
    return pl.pallas_call(
        functools.partial(_mla_prep_kernel, qscale=(ML_NOPE + ML_ROPE) ** -0.5 * LOG2E),
        grid=(tokens // tm,),
        in_specs=[pl.BlockSpec((tm, ML_Q_RANK), lambda i: (i, OFF_MQ // ML_Q_RANK)),
                  pl.BlockSpec((tm, ML_KV_RANK), lambda i: (i, OFF_MKV // ML_KV_RANK)),
                  pl.BlockSpec((tm, LANES), lambda i: (i, OFF_KPE // LANES)),
                  pl.BlockSpec((tm, 1), lambda i: (i, 0)),
                  full(qn_w), full(kvn_w), full(wq_all), full(wkv), full(freq)],
        out_specs=[pl.BlockSpec((tm, wide), lambda i: (i, 0)),
                   pl.BlockSpec((tm, wide), lambda i: (i, 0)),
                   pl.BlockSpec((tm, ML_DIM), lambda i: (i, 0))],
        out_shape=[jax.ShapeDtypeStruct((tokens, wide), BF16),
                   jax.ShapeDtypeStruct((tokens, wide), BF16),
                   jax.ShapeDtypeStruct((tokens, ML_DIM), BF16)],
        compiler_params=_cparams(("parallel",)),
        name="mla_prep",
    )(pa, pa, pf, positions.reshape(tokens, 1), qn_w, kvn_w, wq_all, wkv, freq)


def _mla_attn_kernel(q_ref, k_ref, v_ref, o_ref, vt_ref, *, tq, tk):
    i = pl.program_id(1)
    wq = ML_QK_PAD

    @pl.when(i == 0)
    def _():
        _transpose_into(vt_ref, v_ref, tk)

    n_tiles = (i * tq + tq - 1) // tk + 1
    n_full = (i * tq + 1) // tk
    q_idx = i * tq + lax.broadcasted_iota(jnp.int32, (tk, tq), 1)
    k_off = lax.broadcasted_iota(jnp.int32, (tk, tq), 0)
    qh = [q_ref[:, h * wq:(h + 1) * wq] for h in range(ML_HEADS)]

    def tiles(j, h):
        off = pl.multiple_of(j * tk, tk)
        return (k_ref[pl.ds(off, tk), h * wq:(h + 1) * wq],
                vt_ref[h * ML_V:(h + 1) * ML_V, pl.ds(off, tk)], off)

    heads = range(ML_HEADS)

    def full_body(j, st):
        kv = [tiles(j, h) for h in heads]
        s = [_dot_t(kv[h][0], qh[h]) for h in heads]
        return _softmax_tiles(s, [0.0] * ML_HEADS, st, [kv[h][1] for h in heads])

    def diag_body(j, st):
        kv = [tiles(j, h) for h in heads]
        keep = q_idx >= kv[0][2] + k_off
        s = [jnp.where(keep, _dot_t(kv[h][0], qh[h]), NEG_BIG) for h in heads]
        return _softmax_tiles(s, [0.0] * ML_HEADS, st, [kv[h][1] for h in heads])

    st = tuple((jnp.full((1, tq), NEG_BIG, F32), jnp.zeros((1, tq), F32), jnp.zeros((ML_V, tq), F32))
               for _ in range(ML_HEADS))
    st = lax.fori_loop(0, n_full, full_body, st)
    st = lax.fori_loop(n_full, n_tiles, diag_body, st)
    for h in range(ML_HEADS):
        o_ref[:, h * ML_V:(h + 1) * ML_V] = (st[h][2] / st[h][1]).T.astype(o_ref.dtype)


def mla_attention(qf, kf, v, batch, *, tq=ATTN_TILE):
    tokens = qf.shape[0]
    seq = tokens // batch
    tq = min(tq, seq)
    tk = tq
    nq = seq // tq
    wide = qf.shape[1]
    return pl.pallas_call(
        functools.partial(_mla_attn_kernel, tq=tq, tk=tk),
        grid=(batch, nq),
        in_specs=[pl.BlockSpec((tq, wide), lambda b, i: (b * nq + i, 0)),
                  pl.BlockSpec((seq, wide), lambda b, i: (b, 0)),
                  pl.BlockSpec((seq, ML_DIM), lambda b, i: (b, 0))],
        out_specs=pl.BlockSpec((tq, ML_DIM), lambda b, i: (b * nq + i, 0)),
        out_shape=jax.ShapeDtypeStruct((tokens, ML_DIM), BF16),
        scratch_shapes=[pltpu.VMEM((ML_DIM, seq), BF16)],
        compiler_params=_cparams(("arbitrary", "arbitrary")),
        name="mla_attention",
    )(qf, kf, v)


def _cross_kernel(x_ref, nw_ref, wq_ref, kv_ref, wo_ref, o_ref):
    x = x_ref[...]
    q = _dot(_rms(x, nw_ref[...], NORM_EPS).astype(BF16), wq_ref[...])
    kv = kv_ref[...]
    scale = CA_HEAD_DIM ** -0.5
    outs = []
    for hh in range(CA_HEADS):
        sl = slice(hh * CA_HEAD_DIM, (hh + 1) * CA_HEAD_DIM)
        s = _dot_t(q[:, sl].astype(BF16), kv[:, sl]) * scale
        p = jnp.exp(s - jnp.max(s, axis=-1, keepdims=True))
        p = p / jnp.sum(p, axis=-1, keepdims=True)
        outs.append(_dot(p.astype(BF16), kv[:, CA_DIM + hh * CA_HEAD_DIM:CA_DIM + (hh + 1) * CA_HEAD_DIM]))
    o = jnp.concatenate(outs, axis=1).astype(BF16)
    o_ref[...] = x + _dot(o, wo_ref[...])


def cross_block(x, batch, norm_w, wq, kv, wo, *, tq=512):
    tokens, d = x.shape
    seq = tokens // batch
    tq = min(tq, seq)
    nq = seq // tq
    mem_len = kv.shape[0] // batch
    return pl.pallas_call(
        _cross_kernel,
        grid=(batch, nq),
        in_specs=[pl.BlockSpec((tq, d), lambda b, i: (b * nq + i, 0)),
                  pl.BlockSpec((1, d), lambda b, i: (0, 0)),
                  pl.BlockSpec(wq.shape, lambda b, i: (0, 0)),
                  pl.BlockSpec((mem_len, 2 * CA_DIM), lambda b, i: (b, 0)),
                  pl.BlockSpec(wo.shape, lambda b, i: (0, 0))],
        out_specs=pl.BlockSpec((tq, d), lambda b, i: (b * nq + i, 0)),
        out_shape=jax.ShapeDtypeStruct((tokens, d), F32),
        compiler_params=_cparams(("parallel", "parallel")),
        name="cross_block",
    )(x, norm_w.reshape(1, d), wq, kv, wo)


def _router_kernel(x_ref, nw_ref, wr_ref, br_ref, h_ref, sel_ref):
    h = _rms(x_ref[...], nw_ref[...], NORM_EPS)
    h_ref[...] = h.astype(h_ref.dtype)
    logits = _dot_x3(h, wr_ref[...])
    biased = logits + br_ref[...]
    lane = lax.broadcasted_iota(jnp.int32, logits.shape, 1)
    big = jnp.int32(LANES)

    def first_argmax(vals):
        mx = jnp.max(vals, axis=-1, keepdims=True)
        return jnp.min(jnp.where(vals == mx, lane, big), axis=-1, keepdims=True)

    def pick(vals, idx):
        return jnp.sum(jnp.where(lane == idx, vals, 0.0), axis=-1, keepdims=True)

    is_group = (lane >= MOE_EXPERTS) & (lane < MOE_EXPERTS + MOE_GROUPS)
    gl = jnp.where(is_group, logits, NEG_BIG)
    ge = jnp.exp(gl - jnp.max(gl, axis=-1, keepdims=True))
    gp = ge / jnp.sum(ge, axis=-1, keepdims=True)
    g_lane = first_argmax(jnp.where(is_group, biased, NEG_BIG))
    p_group = pick(gp, g_lane)
    lo = (g_lane - MOE_EXPERTS) * MOE_PER_GROUP
    in_group = (lane >= lo) & (lane < lo + MOE_PER_GROUP)
    eb = jnp.where(in_group, biased, NEG_BIG)
    i1 = first_argmax(eb)
    i2 = first_argmax(jnp.where(lane == i1, NEG_BIG, eb))
    l1, l2 = pick(logits, i1), pick(logits, i2)
    mx = jnp.maximum(l1, l2)
    e1, e2 = jnp.exp(l1 - mx), jnp.exp(l2 - mx)
    w1, w2 = e1 / (e1 + e2), e2 / (e1 + e2)
    sel_ref[...] = jnp.where(lane == SEL_E1, i1.astype(F32),
                             jnp.where(lane == SEL_E2, i2.astype(F32),
                                       jnp.where(lane == SEL_G1, w1 * p_group,
                                                 jnp.where(lane == SEL_G2, w2 * p_group, 0.0))))


SEL_E1, SEL_E2, SEL_G1, SEL_G2 = 0, 1, 2, 3


def moe_router(x, norm_w, w_router, b_router, *, tm=512):
    tokens, d = x.shape
    tm = min(tm, tokens)
    return pl.pallas_call(
        _router_kernel,
        grid=(tokens // tm,),
        in_specs=[pl.BlockSpec((tm, d), lambda i: (i, 0)),
                  pl.BlockSpec((1, d), lambda i: (0, 0)),
                  pl.BlockSpec((d, LANES), lambda i: (0, 0)),
                  pl.BlockSpec((1, LANES), lambda i: (0, 0))],
        out_specs=[pl.BlockSpec((tm, d), lambda i: (i, 0)),
                   pl.BlockSpec((tm, LANES), lambda i: (i, 0))],
        out_shape=[jax.ShapeDtypeStruct((tokens, d), F32),
                   jax.ShapeDtypeStruct((tokens, LANES), F32)],
        compiler_params=_cparams(("parallel",)),
        name="moe_router",
    )(x, norm_w.reshape(1, d), w_router, b_router)


def _moe_rank_kernel(sel_ref, ltri_ref, rank_ref, counts_ref, carry_ref):
    @pl.when(pl.program_id(0) == 0)
    def _():
        carry_ref[...] = jnp.zeros_like(carry_ref)

    sel = sel_ref[...]
    lane = lax.broadcasted_iota(jnp.int32, sel.shape, 1)
    lane_f = lane.astype(F32)
    oh1 = lane_f == sel[:, SEL_E1:SEL_E1 + 1]
    oh2 = lane_f == sel[:, SEL_E2:SEL_E2 + 1]
    f1, f2 = oh1.astype(F32), oh2.astype(F32)
    ltri = ltri_ref[...]
    before1 = _dot(ltri, f1.astype(BF16))
    before2 = _dot(ltri, f2.astype(BF16))
    c1 = jnp.sum(f1, axis=0, keepdims=True)
    c2 = jnp.sum(f2, axis=0, keepdims=True)
    carry = carry_ref[...]
    r1 = jnp.sum(jnp.where(oh1, before1 + carry, 0.0), axis=1, keepdims=True)
    r2 = jnp.sum(jnp.where(oh2, before2 + carry + c1, 0.0), axis=1, keepdims=True)
    rank_ref[...] = jnp.where(lane == SEL_E1, r1, jnp.where(lane == SEL_E2, r2, 0.0)).astype(jnp.int32)
    total = carry + c1 + c2
    carry_ref[...] = total
    counts_ref[...] = total.astype(jnp.int32)


def moe_rank(sel, *, tm=512):
    tokens = sel.shape[0]
    tm = min(tm, tokens)
    ltri = (jnp.arange(tm)[:, None] > jnp.arange(tm)[None, :]).astype(BF16)
    return pl.pallas_call(
        _moe_rank_kernel,
        grid=(tokens // tm,),
        in_specs=[pl.BlockSpec((tm, LANES), lambda i: (i, 0)),
                  pl.BlockSpec((tm, tm), lambda i: (0, 0))],
        out_specs=[pl.BlockSpec((tm, LANES), lambda i: (i, 0)),
                   pl.BlockSpec((1, LANES), lambda i: (0, 0))],
        out_shape=[jax.ShapeDtypeStruct((tokens, LANES), jnp.int32),
                   jax.ShapeDtypeStruct((1, LANES), jnp.int32)],
        scratch_shapes=[pltpu.VMEM((1, LANES), F32)],
        compiler_params=_cparams(("arbitrary",)),
        name="moe_rank",
    )(sel, ltri)


def _row_copy(src_ref, src_row, dst_ref, dst_row, sem):
    return pltpu.make_async_copy(src_ref.at[pl.ds(src_row, 1)], dst_ref.at[pl.ds(dst_row, 1)], sem)


def _moe_dispatch_kernel(dest_ref, tail_ref, h_ref, xs_ref, hbuf, zero_ref, lsem, ssem, zsem, *, tm):
    step = pl.program_id(0)
    n_steps = pl.num_programs(0)
    base = step * (2 * tm)
    slot = step % 3

    def load(tile, buf_slot):
        return pltpu.make_async_copy(h_ref.at[pl.ds(pl.multiple_of(tile * tm, tm), tm)], hbuf.at[buf_slot],
                                     lsem.at[buf_slot])

    def wait_scatter(buf_slot):
        for _ in range(2):
            pltpu.make_async_copy(hbuf.at[buf_slot], xs_ref.at[pl.ds(0, tm)], ssem.at[buf_slot]).wait()

    @pl.when(step == 0)
    def _():
        load(0, 0).start()

        @pl.when(n_steps > 1)
        def _():
            load(1, 1).start()

    @pl.when(step == 0)
    def _():
        zero_ref[...] = jnp.zeros_like(zero_ref)

        def fill(tail):
            return pltpu.make_async_copy(zero_ref, xs_ref.at[pl.ds(pl.multiple_of(tail, 8), zero_ref.shape[0])], zsem)

        def start(e, carry):
            @pl.when(tail_ref[e] >= 0)
            def _():
                fill(tail_ref[e]).start()
            return carry

        def wait(e, carry):
            @pl.when(tail_ref[e] >= 0)
            def _():
                fill(tail_ref[e]).wait()
            return carry

        lax.fori_loop(0, tail_ref.shape[0], start, 0)
        lax.fori_loop(0, tail_ref.shape[0], wait, 0)

    load(step, slot).wait()

    def issue(r, carry):
        for s in range(2):
            _row_copy(hbuf.at[slot], r, xs_ref, dest_ref[base + 2 * r + s], ssem.at[slot]).start()
        return carry

    lax.fori_loop(0, tm, issue, 0, unroll=DMA_ISSUE_UNROLL)

    @pl.when(step >= 1)
    def _():
        wait_scatter((step + 2) % 3)

    @pl.when(step + 2 < n_steps)
    def _():
        load(step + 2, (step + 2) % 3).start()

    @pl.when(step == n_steps - 1)
    def _():
        wait_scatter(slot)


def moe_dispatch(h, dest, tails, rows, row_tile, *, tm=256):
    tokens, d = h.shape
    tm = min(tm, tokens)
    grid_spec = pltpu.PrefetchScalarGridSpec(
        num_scalar_prefetch=2,
        grid=(tokens // tm,),
        in_specs=[pl.BlockSpec(memory_space=pl.ANY)],
        out_specs=pl.BlockSpec(memory_space=pl.ANY),
        scratch_shapes=[pltpu.VMEM((3, tm, d), h.dtype), pltpu.VMEM((row_tile, d), h.dtype),
                        pltpu.SemaphoreType.DMA((3,)), pltpu.SemaphoreType.DMA((3,)),
                        pltpu.SemaphoreType.DMA(())],
    )
    return pl.pallas_call(
        functools.partial(_moe_dispatch_kernel, tm=tm),
        grid_spec=grid_spec,
        out_shape=jax.ShapeDtypeStruct((rows, d), h.dtype),
        compiler_params=_cparams(("arbitrary",)),
        name="moe_dispatch",
    )(dest, tails, h)


def _moe_expert_kernel(te_ref, nused_ref, xs_ref, wg_ref, wu_ref, wd_ref, ys_ref, wgb, wub, wdb):
    r = pl.program_id(0)
    used = r < nused_ref[0]
    changed = (r == 0) | (te_ref[r] != te_ref[jnp.maximum(r - 1, 0)])

    @pl.when(used & changed)
    def _():
        wgb[...] = wg_ref[...].astype(BF16)
        wub[...] = wu_ref[...].astype(BF16)
        wdb[...] = wd_ref[...].astype(BF16)

    @pl.when(used)
    def _():
        x = xs_ref[...].astype(BF16)
        gate_pre = _dot(x, wgb[...])
        hid = (gate_pre * jax.nn.sigmoid(gate_pre)) * _dot(x, wub[...])
        ys_ref[...] = _dot(hid.astype(BF16), wdb[...])

    @pl.when(jnp.logical_not(used))
    def _():
        ys_ref[...] = jnp.zeros_like(ys_ref)


def moe_experts(xs, tile_expert, n_used, wg, wu, wd, layer, *, tm):
    rows, d = xs.shape
    de = wg.shape[-1]
    grid_spec = pltpu.PrefetchScalarGridSpec(
        num_scalar_prefetch=2,
        grid=(rows // tm,),
        in_specs=[pl.BlockSpec((tm, d), lambda r, te, nu: (jnp.minimum(r, nu[0] - 1), 0)),
                  pl.BlockSpec((None, None, d, de), lambda r, te, nu: (layer, te[r], 0, 0)),
                  pl.BlockSpec((None, None, d, de), lambda r, te, nu: (layer, te[r], 0, 0)),
                  pl.BlockSpec((None, None, de, d), lambda r, te, nu: (layer, te[r], 0, 0))],
        out_specs=pl.BlockSpec((tm, d), lambda r, te, nu: (r, 0)),
        scratch_shapes=[pltpu.VMEM((d, de), BF16), pltpu.VMEM((d, de), BF16), pltpu.VMEM((de, d), BF16)],
    )
    return pl.pallas_call(
        _moe_expert_kernel,
        grid_spec=grid_spec,
        out_shape=jax.ShapeDtypeStruct((rows, d), F32),
        compiler_params=_cparams(("arbitrary",)),
        name="moe_experts",
    )(tile_expert, n_used, xs, wg, wu, wd)


def _moe_combine_kernel(dest_ref, x_ref, sel_ref, nw_ref, ys_ref, o_ref, buf_ref, sem, *, tm, normalize):
    step = pl.program_id(0)
    slot = step % 2

    def gather(tile, buf_slot):
        base = tile * (2 * tm)

        def issue(r, carry):
            for s in range(2):
                _row_copy(ys_ref, dest_ref[base + 2 * r + s], buf_ref.at[buf_slot, s], r, sem.at[buf_slot]).start()
            return carry

        lax.fori_loop(0, tm, issue, 0, unroll=DMA_ISSUE_UNROLL)

    @pl.when(step == 0)
    def _():
        gather(0, 0)

    @pl.when(step + 1 < pl.num_programs(0))
    def _():
        gather(step + 1, 1 - slot)

    for s in range(2):
        pltpu.make_async_copy(ys_ref.at[pl.ds(0, tm)], buf_ref.at[slot, s], sem.at[slot]).wait()
    sel = sel_ref[...]
    out = x_ref[...] + sel[:, SEL_G1:SEL_G1 + 1] * buf_ref[slot, 0] + sel[:, SEL_G2:SEL_G2 + 1] * buf_ref[slot, 1]
    o_ref[...] = _rms(out, nw_ref[...], NORM_EPS) if normalize else out


def moe_combine(x, sel, ys, dest, final_norm=None, *, tm=256):
    tokens, d = x.shape
    tm = min(tm, tokens)
    normalize = final_norm is not None
    nw = (final_norm if normalize else jnp.ones((d,), F32)).reshape(1, d)
    grid_spec = pltpu.PrefetchScalarGridSpec(
        num_scalar_prefetch=1,
        grid=(tokens // tm,),
        in_specs=[pl.BlockSpec((tm, d), lambda i, *_: (i, 0)),
                  pl.BlockSpec((tm, LANES), lambda i, *_: (i, 0)),
                  pl.BlockSpec((1, d), lambda i, *_: (0, 0)),
                  pl.BlockSpec(memory_space=pl.ANY)],
        out_specs=pl.BlockSpec((tm, d), lambda i, *_: (i, 0)),
        scratch_shapes=[pltpu.VMEM((2, 2, tm, d), F32), pltpu.SemaphoreType.DMA((2,))],
    )
    return pl.pallas_call(
        functools.partial(_moe_combine_kernel, tm=tm, normalize=normalize),
        grid_spec=grid_spec,
        out_shape=jax.ShapeDtypeStruct((tokens, d), F32),
        compiler_params=_cparams(("arbitrary",)),
        name="moe_combine",
    )(dest, x, sel, nw, ys)


def moe_block(x, norm_w, w_router, b_router, wg, wu, wd, layer, final_norm=None, *, tm=MOE_ROW_TILE):
    tokens, d = x.shape
    n_exp = wg.shape[1]
    h, sel = moe_router(x, norm_w, w_router, b_router)
    rank, counts = moe_rank(sel)
    padded = (counts[0, :n_exp] + (tm - 1)) // tm * tm
    ends = jnp.cumsum(padded)
    starts = ends - padded
    experts = sel[:, SEL_E1:SEL_E2 + 1].astype(jnp.int32)
    start_of = jnp.sum(jnp.where(experts[..., None] == jnp.arange(n_exp, dtype=jnp.int32), starts, 0), axis=-1)
    dest = (start_of + rank[:, SEL_E1:SEL_E2 + 1]).reshape(-1)
    rows = 2 * tokens + n_exp * tm
    tile_start = jnp.arange(rows // tm, dtype=jnp.int32) * tm
    tile_expert = jnp.minimum(jnp.sum(tile_start[:, None] >= ends[None, :], axis=1), n_exp - 1).astype(jnp.int32)
    n_used = (ends[-1] // tm).astype(jnp.int32).reshape(1)
    tails = jnp.concatenate([jnp.where(padded > 0, ends - tm, -1),
                             jnp.where(tile_start >= ends[-1], tile_start, -1)]).astype(jnp.int32)
    xs = moe_dispatch(h, dest, tails, rows, tm)
    ys = moe_experts(xs, tile_expert, n_used, wg, wu, wd, layer, tm=tm)
    return moe_combine(x, sel, ys, dest, final_norm)


def _proj_weights(w_in_l, w_vres_l):
    d = w_in_l.shape[0]
    mla0 = RW_COLS + DF_COLS
    vres = jnp.zeros((d, RW_V_RANK), F32) if w_vres_l is None else w_vres_l
    part_f = [w_in_l[:, 3 * RW_DIM:RW_COLS],
              vres, jnp.zeros((d, LANES - RW_V_RANK), F32),
              w_in_l[:, mla0 + ML_Q_RANK + ML_KV_RANK:mla0 + ML_COLS], jnp.zeros((d, LANES - ML_ROPE), F32)]
    part_a = [w_in_l[:, :3 * RW_DIM],
              w_in_l[:, RW_COLS:RW_COLS + DF_COLS],
              w_in_l[:, mla0 + ML_Q_RANK:mla0 + ML_Q_RANK + ML_KV_RANK],
              jnp.zeros((d, OFF_MQ - OFF_MKV - ML_KV_RANK), F32),
              w_in_l[:, mla0:mla0 + ML_Q_RANK]]
    return jnp.concatenate(part_f, axis=1).astype(BF16), jnp.concatenate(part_a, axis=1).astype(BF16)


def _pad_rows(w, rows, at=0):
    out = jnp.zeros((rows, w.shape[1]), w.dtype)
    return lax.dynamic_update_slice(out, w, (at, 0))


def kernel(x, mem, positions, rel_bias, final_norm, norm_mix, w_in, w_in_vres, w_out, tm_mu, tm_mu_vres, tm_w0, tm_w2, tm_a0, tm_a2, tm_v0, tm_v2, tm_g2, tm_k_k, tm_k_a, tm_r_k, tm_ln_w, tm_ln_b, da_lq1, da_lk1, da_lq2, da_lk2, da_subln, mla_q_norm, mla_wq_b, mla_kv_norm, mla_wkv_b, norm_cross, norm_mem, ca_wq, ca_wkv, ca_wo, norm_ffn, moe_w_group, moe_b_group, moe_w_expert, moe_b_expert, moe_w_gate, moe_w_up, moe_w_down):
    batch, seq, d = x.shape
    tokens = batch * seq
    depth = norm_mix.shape[0]
    xf = x.reshape(tokens, d)
    memf = mem.reshape(-1, d)
    positions = positions.astype(jnp.int32)

    head_of_lane = jnp.arange(RW_DIM) // RW_HEAD_DIM
    seg = (head_of_lane[:, None] == jnp.arange(LANES)[None, :]).astype(BF16)
    seg_t = seg.T
    row = lambda v: v.reshape(1, -1)

    v_first = None
    for l in range(depth):
        w_f, w_a = _proj_weights(w_in[l], None if l == 0 else w_in_vres[l - 1])
        proj = norm_matmul(xf, norm_mix[l], w_f, tm=1024, tn=PROJ_F_COLS)
        pa = norm_matmul(xf, norm_mix[l], w_a, out_dtype=BF16, tm=1024, tn=PROJ_A_COLS // 3)

        mu = tm_mu[l]
        prm = dict(mu_r=row(mu[:RW_DIM]), mu_k=row(mu[RW_DIM:2 * RW_DIM]), mu_v=row(mu[2 * RW_DIM:3 * RW_DIM]),
                   mu_l=row(mu[3 * RW_DIM:]), w0=row(tm_w0[l]), a0=row(tm_a0[l]),
                   w2=_pad_rows(tm_w2[l], LANES, 0), a2=_pad_rows(tm_a2[l], LANES, RW_W_RANK),
                   g2=tm_g2[l].astype(BF16), k_k=row(tm_k_k[l]), k_a=row(tm_k_a[l]), r_k=row(tm_r_k[l]),
                   seg=seg, seg_t=seg_t)
        if l > 0:
            prm.update(mu_vr=jnp.pad(row(tm_mu_vres[l - 1]), ((0, 0), (0, LANES - RW_V_RANK))),
                       v0=row(tm_v0[l - 1]), v2=_pad_rows(tm_v2[l - 1], LANES, 0))
        r, lw, k, v, kap, beta, gate, bonus = rwkv_prep(pa, proj, batch, v_first, prm)
        if l == 0:
            v_first = v
        o = rwkv_scan(r, lw, k, v, kap, beta, batch)
        y_a = rwkv_post(o, bonus, gate, tm_ln_w[l], tm_ln_b[l], seg, seg_t)

        lambda_init = 0.8 - 0.6 * math.exp(-0.3 * l)
        lam = (jnp.exp(jnp.sum(da_lq1[l] * da_lk1[l])) - jnp.exp(jnp.sum(da_lq2[l] * da_lk2[l])) + lambda_init)
        y_b = diff_attention(pa, positions, rel_bias, lam, lambda_init, da_subln[l])

        wq = mla_wq_b[l].reshape(ML_Q_RANK, ML_HEADS, ML_NOPE + ML_ROPE)
        wq_pe = jnp.pad(wq[:, :, ML_NOPE:], ((0, 0), (0, 0), (0, LANES - ML_ROPE)))
        wq_all = jnp.concatenate([wq[:, :, :ML_NOPE].reshape(ML_Q_RANK, -1),
                                  wq_pe.reshape(ML_Q_RANK, -1)], axis=1).astype(BF16)
        wkv = mla_wkv_b[l].reshape(ML_KV_RANK, ML_HEADS, ML_NOPE + ML_V)
        wkv = jnp.concatenate([wkv[:, :, :ML_NOPE].reshape(ML_KV_RANK, -1),
                               wkv[:, :, ML_NOPE:].reshape(ML_KV_RANK, -1)], axis=1).astype(BF16)
        qf, kf, v_mla = mla_prep(pa, proj, positions, mla_q_norm[l], mla_kv_norm[l], wq_all, wkv)
        y_c = mla_attention(qf, kf, v_mla, batch)

        wo = w_out[l].astype(BF16)
        xf = matmul_res([y_a, y_b, y_c],
                        [wo[:RW_DIM], wo[RW_DIM:RW_DIM + DF_DIM], wo[RW_DIM + DF_DIM:]], xf)

        kv_mem = norm_matmul(memf, norm_mem[l], ca_wkv[l].astype(BF16), out_dtype=BF16)
        xf = cross_block(xf, batch, norm_cross[l], ca_wq[l].astype(BF16), kv_mem, ca_wo[l].astype(BF16))

        w_router = jnp.concatenate(
            [moe_w_expert[l], moe_w_group[l], jnp.zeros((d, LANES - MOE_EXPERTS - MOE_GROUPS), F32)], axis=1)
        b_router = jnp.concatenate(
            [moe_b_expert[l], moe_b_group[l], jnp.zeros((LANES - MOE_EXPERTS - MOE_GROUPS,), F32)]).reshape(1, LANES)
        xf = moe_block(xf, norm_ffn[l], w_router, b_router, moe_w_gate, moe_w_up, moe_w_down, l,
                       final_norm if l == depth - 1 else None)

    return xf.reshape(batch, seq, d)
```

```python
import functools
import math

import jax
import jax.numpy as jnp
from jax import lax
from jax.experimental import pallas as pl
from jax.experimental.pallas import tpu as pltpu

F32 = jnp.float32
BF16 = jnp.bfloat16

NORM_EPS = 1e-6
ROPE_THETA = 10000.0

RW_HEADS = 16
RW_HEAD_DIM = 64
RW_DIM = RW_HEADS * RW_HEAD_DIM
RW_W_RANK = 64
RW_A_RANK = 64
RW_G_RANK = 128
RW_V_RANK = 32
RW_LORA = RW_W_RANK + RW_A_RANK + RW_G_RANK
RW_LN_EPS = 64e-5
RW_COLS = 3 * RW_DIM + RW_LORA

DF_HEADS = 4
DF_HEAD_DIM = 64
DF_V_DIM = 2 * DF_HEAD_DIM
DF_QK = DF_HEADS * 2 * DF_HEAD_DIM
DF_DIM = DF_HEADS * DF_V_DIM
DF_COLS = 2 * DF_QK + DF_DIM
DF_SUBLN_EPS = 1e-5

ML_HEADS = 4
ML_Q_RANK = 384
ML_KV_RANK = 256
ML_NOPE = 128
ML_ROPE = 64
ML_V = 128
ML_DIM = ML_HEADS * ML_V
ML_COLS = ML_Q_RANK + ML_KV_RANK + ML_ROPE

REL_BUCKETS = 32
REL_MAX_DIST = 128

CA_HEADS = 4
CA_HEAD_DIM = 128
CA_DIM = CA_HEADS * CA_HEAD_DIM

MOE_GROUPS = 4
MOE_PER_GROUP = 8
MOE_EXPERTS = MOE_GROUPS * MOE_PER_GROUP

LANES = 128
SCAN_CHUNK = 64
SCAN_GROUP = 4
SCAN_BATCHES = 2
ATTN_TILE = 512
DMA_ISSUE_UNROLL = 8
MOE_ROW_TILE = 256
VMEM_LIMIT = 56 * 1024 * 1024
NEG_BIG = -1e30

LOG2E = 1.4426950408889634

OFF_LORA = 0
OFF_VRES = OFF_LORA + RW_LORA
OFF_KPE = OFF_VRES + LANES
PROJ_F_COLS = OFF_KPE + LANES
OFF_R = 0
OFF_K = RW_DIM
OFF_V = 2 * RW_DIM
OFF_DQ = 3 * RW_DIM
OFF_DK = OFF_DQ + DF_QK
OFF_DV = OFF_DK + DF_QK
OFF_MKV = OFF_DV + DF_DIM
OFF_MQ = 13 * ML_Q_RANK
PROJ_A_COLS = OFF_MQ + ML_Q_RANK


def _cparams(sem, vmem=VMEM_LIMIT, flags=None):
    return pltpu.CompilerParams(dimension_semantics=sem, vmem_limit_bytes=vmem, flags=flags)


def _dot(a, b):
    return jnp.dot(a, b, preferred_element_type=F32)


def _dot_t(a, b):
    return lax.dot_general(a, b, (((1,), (1,)), ((), ())), preferred_element_type=F32)


def _split3(x):
    hi = x.astype(BF16)
    r1 = x - hi.astype(F32)
    mid = r1.astype(BF16)
    lo = (r1 - mid.astype(F32)).astype(BF16)
    return hi, mid, lo


def _dot_rhs01(x, ones_bf16):
    hi = x.astype(BF16)
    lo = (x - hi.astype(F32)).astype(BF16)
    return _dot(hi, ones_bf16) + _dot(lo, ones_bf16)


def _dot_x3(a, b):
    ah = a.astype(BF16)
    al = (a - ah.astype(F32)).astype(BF16)
    bh = b.astype(BF16)
    bl = (b - bh.astype(F32)).astype(BF16)
    return _dot(ah, bh) + _dot(ah, bl) + _dot(al, bh)


def _rms(x, w, eps):
    ms = jnp.mean(x * x, axis=-1, keepdims=True)
    return x * lax.rsqrt(ms + eps) * w


def _norm_matmul_kernel(*refs, eps, has_side):
    if has_side:
        x_ref, nw_ref, w_ref, ws_ref, o_ref, os_ref, xn_ref = refs
    else:
        x_ref, nw_ref, w_ref, o_ref, xn_ref = refs

    @pl.when(pl.program_id(1) == 0)
    def _():
        xn_ref[...] = _rms(x_ref[...], nw_ref[...], eps).astype(BF16)
        if has_side:
            os_ref[...] = _dot(xn_ref[...], ws_ref[...])

    o_ref[...] = _dot(xn_ref[...], w_ref[...]).astype(o_ref.dtype)


def norm_matmul(x, nw, w, w_side=None, *, out_dtype=F32, tm=512, tn=None, eps=NORM_EPS):
    m, d = x.shape
    n = w.shape[1]
    tm = min(tm, m)
    tn = n if tn is None else tn
    has_side = w_side is not None
    in_specs = [pl.BlockSpec((tm, d), lambda i, j: (i, 0)),
                pl.BlockSpec((1, d), lambda i, j: (0, 0)),
                pl.BlockSpec((d, tn), lambda i, j: (0, j))]
    out_specs = [pl.BlockSpec((tm, tn), lambda i, j: (i, j))]
    out_shape = [jax.ShapeDtypeStruct((m, n), out_dtype)]
    args = [x, nw.reshape(1, d), w]
    if has_side:
        ns = w_side.shape[1]
        in_specs.append(pl.BlockSpec((d, ns), lambda i, j: (0, 0)))
        out_specs.append(pl.BlockSpec((tm, ns), lambda i, j: (i, 0)))
        out_shape.append(jax.ShapeDtypeStruct((m, ns), F32))
        args.append(w_side)
    outs = pl.pallas_call(
        functools.partial(_norm_matmul_kernel, eps=eps, has_side=has_side),
        grid=(m // tm, n // tn),
        in_specs=in_specs,
        out_specs=out_specs,
        out_shape=out_shape,
        scratch_shapes=[pltpu.VMEM((tm, d), BF16)],
        compiler_params=_cparams(("parallel", "arbitrary")),
        name="norm_matmul",
    )(*args)
    return outs if has_side else outs[0]


def _matmul_res_kernel(*refs, n_a):
    a_refs, w_refs = refs[:n_a], refs[n_a:2 * n_a]
    res_ref, o_ref = refs[2 * n_a], refs[2 * n_a + 1]
    acc = res_ref[...]
    for a_ref, w_ref in zip(a_refs, w_refs):
        acc = acc + _dot(a_ref[...].astype(BF16), w_ref[...])
    o_ref[...] = acc


def matmul_res(a_list, w_list, res, *, tm=1024, tn=1024):
    m, n = res.shape
    tm = min(tm, m)
    tn = min(tn, n)
    n_a = len(a_list)
    in_specs = ([pl.BlockSpec((tm, a.shape[1]), lambda i, j: (i, 0)) for a in a_list]
                + [pl.BlockSpec((w.shape[0], tn), lambda i, j: (0, j)) for w in w_list]
                + [pl.BlockSpec((tm, tn), lambda i, j: (i, j))])
    return pl.pallas_call(
        functools.partial(_matmul_res_kernel, n_a=n_a),
        grid=(m // tm, n // tn),
        in_specs=in_specs,
        out_specs=pl.BlockSpec((tm, tn), lambda i, j: (i, j)),
        out_shape=jax.ShapeDtypeStruct((m, n), F32),
        compiler_params=_cparams(("parallel", "arbitrary")),
        name="matmul_res",
    )(*a_list, *w_list, res)


def _softplus(z):
    return jnp.maximum(z, 0.0) + jnp.log(1.0 + jnp.exp(-jnp.abs(z)))


def _rwkv_prep_kernel(*refs, has_vres):
    if has_vres:
        (pr_ref, pk_ref, pv_ref, pl_ref, pvr_ref, vfirst_ref,
         mu_r, mu_k, mu_v, mu_l, mu_vr, w0, w2, a0, a2, g2, v0, v2,
         k_k, k_a, r_k, seg, seg_t,
         r_o, lw_o, k_o, v_o, kap_o, beta_o, g_o, bonus_o,
         last_r, last_k, last_v, last_l, last_vr) = refs
    else:
        (pr_ref, pk_ref, pv_ref, pl_ref,
         mu_r, mu_k, mu_v, mu_l, w0, w2, a0, a2, g2,
         k_k, k_a, r_k, seg, seg_t,
         r_o, lw_o, k_o, v_o, kap_o, beta_o, g_o, bonus_o,
         last_r, last_k, last_v, last_l) = refs
    t = pl.program_id(1)

    def shifted(p_ref, last_ref, mu_ref):
        p = p_ref[...].astype(F32)
        n = p.shape[0]
        carried = jnp.where(t == 0, 0.0, last_ref[0:1, :])
        row = lax.broadcasted_iota(jnp.int32, p.shape, 0)
        prev = jnp.where(row == 0, carried, pltpu.roll(p, 1, axis=0))
        last_ref[0:1, :] = p[n - 1:n, :]
        return p + mu_ref[...] * (prev - p)

    r = shifted(pr_ref, last_r, mu_r)
    k = shifted(pk_ref, last_k, mu_k)
    v = shifted(pv_ref, last_v, mu_v)
    lora = shifted(pl_ref, last_l, mu_l)
    wl = lora[:, :LANES]
    gl = lora[:, LANES:]

    lane = lax.broadcasted_iota(jnp.int32, wl.shape, 1)
    wl_t = jnp.where(lane < RW_W_RANK, jnp.tanh(wl), 0.0)
    al = jnp.where(lane >= RW_W_RANK, wl, 0.0)
    w_log = -_softplus(-(w0[...] + _dot_x3(wl_t, w2[...]))) - 0.5
    lw_o[...] = -jnp.exp(w_log)
    a = jax.nn.sigmoid(a0[...] + _dot_x3(al, a2[...]))
    g_o[...] = _dot(jax.nn.sigmoid(gl).astype(BF16), g2[...]).astype(g_o.dtype)

    segm, segm_t = seg[...], seg_t[...]

    def head_sum(x):
        return _dot_rhs01(_dot_rhs01(x, segm), segm_t)

    kk = k * k_k[...]
    kk = kk * lax.rsqrt(jnp.maximum(head_sum(kk * kk), 1e-24))
    k = k * (1.0 + (a - 1.0) * k_a[...])
    if has_vres:
        vr = shifted(pvr_ref, last_vr, mu_vr)
        mix = jax.nn.sigmoid(v0[...] + _dot_x3(vr, v2[...]))
        v = v + (vfirst_ref[...] - v) * mix
    r_o[...] = r.astype(r_o.dtype)
    k_o[...] = k.astype(k_o.dtype)
    v_o[...] = v.astype(v_o.dtype)
    kap_o[...] = kk.astype(kap_o.dtype)
    beta_o[...] = (kk * a).astype(beta_o.dtype)
    bonus_o[...] = (head_sum(r * k * r_k[...]) * v).astype(bonus_o.dtype)


def rwkv_prep(pa, proj, batch, vfirst, prm, *, tt=512):
    tokens = proj.shape[0]
    seq = tokens // batch
    tt = min(tt, seq)
    nt = seq // tt
    has_vres = vfirst is not None
    d = RW_DIM

    def rows(width, col):
        return pl.BlockSpec((tt, width), lambda b, t, col=col: (b * nt + t, col))

    def full(shape):
        return pl.BlockSpec(shape, lambda b, t: (0, 0))

    in_specs = [rows(d, OFF_R // d), rows(d, OFF_K // d), rows(d, OFF_V // d),
                rows(RW_LORA, OFF_LORA // RW_LORA)]
    args = [pa, pa, pa, proj]
    if has_vres:
        in_specs += [rows(LANES, OFF_VRES // LANES), rows(d, 0)]
        args += [proj, vfirst]
    names = ["mu_r", "mu_k", "mu_v", "mu_l"] + (["mu_vr"] if has_vres else []) + ["w0", "w2", "a0", "a2", "g2"]
    names += (["v0", "v2"] if has_vres else []) + ["k_k", "k_a", "r_k", "seg", "seg_t"]
    for nm in names:
        in_specs.append(full(prm[nm].shape))
        args.append(prm[nm])
    out_spec = pl.BlockSpec((tt, d), lambda b, t: (b * nt + t, 0))
    scratch = [pltpu.VMEM((8, d), F32)] * 3 + [pltpu.VMEM((8, RW_LORA), F32)]
    if has_vres:
        scratch.append(pltpu.VMEM((8, LANES), F32))
    return pl.pallas_call(
        functools.partial(_rwkv_prep_kernel, has_vres=has_vres),
        grid=(batch, nt),
        in_specs=in_specs,
        out_specs=[out_spec] * 8,
        out_shape=[jax.ShapeDtypeStruct((tokens, d), F32 if i == 1 else BF16) for i in range(8)],
        scratch_shapes=scratch,
        compiler_params=_cparams(("arbitrary", "arbitrary")),
        name="rwkv_prep",
    )(*args)


def _rwkv_scan_kernel(r_ref, lw_ref, k_ref, v_ref, kap_ref, beta_ref, tril_ref, bmask_ref,
                      o_ref, ht_ref):
    @pl.when(pl.program_id(1) == 0)
    def _():
        ht_ref[...] = jnp.zeros_like(ht_ref)

    n_batch, c, d = lw_ref.shape
    w = ht_ref.shape[1]
    g = w // RW_HEAD_DIM
    bmask = bmask_ref[...]
    bmask_b = bmask.astype(BF16)
    tril3 = tril_ref[...]
    t_idx = lax.broadcasted_iota(jnp.int32, (c, w), 0)
    s_idx = lax.broadcasted_iota(jnp.int32, (c, w), 1) % c
    strict = t_idx > s_idx
    incl = t_idx >= s_idx
    n_sq = int(math.log2(c))

    def stack(x):
        return jnp.concatenate([x.astype(BF16)] * g, axis=0) * bmask_b

    sls = [(bi, slice(None), slice(lo, lo + w)) for bi in range(n_batch) for lo in range(0, d, w)]
    groups = range(len(sls))
    lw = [lw_ref[sl] for sl in sls]
    cum = [_dot(tril3, jnp.concatenate(_split3(x), axis=0)) for x in lw]
    total = [x[c - 1:c, :] for x in cum]
    ar = [jnp.concatenate([-kap_ref[sls[gi]] * jnp.exp(cum[gi] - lw[gi]), r_ref[sls[gi]] * jnp.exp(cum[gi])],
                          axis=0).astype(BF16) for gi in groups]
    p_inv = [jnp.exp(-x) for x in cum]
    b_s = [stack(beta_ref[sls[gi]] * p_inv[gi]) for gi in groups]
    k_s = [stack(k_ref[sls[gi]] * p_inv[gi]) for gi in groups]
    v_n = [v_ref[sl] for sl in sls]
    v_s = [stack(x) for x in v_n]

    arb = [_dot_t(ar[gi], b_s[gi]) for gi in groups]
    ark = [_dot_t(ar[gi], k_s[gi]) for gi in groups]
    ab = [jnp.where(strict, m[:c], 0.0) for m in arb]
    rb = [jnp.where(incl, m[c:], 0.0).astype(BF16) for m in arb]
    akrk = [jnp.concatenate([jnp.where(strict, m[:c], 0.0), jnp.where(incl, m[c:], 0.0)], axis=0).astype(BF16)
            for m in ark]

    ht = [ht_ref[gi] for gi in groups]
    base = [_dot_t(ar[gi], ht[gi].astype(BF16)) + _dot(akrk[gi], v_s[gi]) for gi in groups]
    x = [m[:c] for m in base]
    lp = ab
    for i in range(n_sq):
        lpb = [m.astype(BF16) for m in lp]
        x = [x[gi] + _dot(lpb[gi], stack(x[gi])) for gi in groups]
        if i < n_sq - 1:
            lp = [_dot(lpb[gi], stack(lp[gi])) for gi in groups]
    for gi in groups:
        o_ref[sls[gi]] = base[gi][c:] + _dot(rb[gi], stack(x[gi]))

    for gi in groups:
        p_rem = jnp.exp(total[gi] - cum[gi])
        z = jnp.concatenate([beta_ref[sls[gi]] * p_rem, k_ref[sls[gi]] * p_rem], axis=0).astype(BF16)
        uv_t = jnp.concatenate([x[gi], v_n[gi].astype(F32)], axis=0).T.astype(BF16)
        ht_ref[gi] = ht[gi] * jnp.exp(total[gi]) + bmask * _dot(uv_t, z)


def rwkv_scan(r, lw, k, v, kap, beta, batch):
    tokens, d = r.shape
    seq = tokens // batch
    c = min(SCAN_CHUNK, seq)
    nc = seq // c
    gw = SCAN_GROUP * RW_HEAD_DIM
    rr = SCAN_GROUP * c
    assert c == RW_HEAD_DIM, "the stacking mask doubles as the head-block mask of the state"
    bb = math.gcd(batch, SCAN_BATCHES)
    tril = jnp.tile((jnp.arange(c)[:, None] >= jnp.arange(c)[None, :]).astype(BF16), (1, 3))
    bmask = (jnp.arange(rr)[:, None] // c == jnp.arange(gw)[None, :] // RW_HEAD_DIM).astype(F32)
    blk = pl.BlockSpec((bb, c, d), lambda b, i: (b, i, 0))
    as3d = lambda a: a.reshape(batch, seq, d)
    out = pl.pallas_call(
        _rwkv_scan_kernel,
        grid=(batch // bb, nc),
        in_specs=[blk] * 6 + [pl.BlockSpec((c, 3 * c), lambda b, i: (0, 0)),
                              pl.BlockSpec((rr, gw), lambda b, i: (0, 0))],
        out_specs=blk,
        out_shape=jax.ShapeDtypeStruct((batch, seq, d), F32),
        scratch_shapes=[pltpu.VMEM((bb * (d // gw), gw, gw), F32)],
        compiler_params=_cparams(("arbitrary", "arbitrary")),
        name="rwkv_scan",
    )(as3d(r), as3d(lw), as3d(k), as3d(v), as3d(kap), as3d(beta), tril, bmask)
    return out.reshape(tokens, d)


def _rwkv_post_kernel(o_ref, bonus_ref, g_ref, lnw_ref, lnb_ref, seg, seg_t, y_ref):
    segm, segm_t = seg[...], seg_t[...]

    def head_mean(x):
        return _dot_rhs01(_dot_rhs01(x, segm), segm_t) * (1.0 / RW_HEAD_DIM)

    o = o_ref[...]
    dlt = o - head_mean(o)
    var = head_mean(dlt * dlt)
    y = dlt * lax.rsqrt(var + RW_LN_EPS) * lnw_ref[...] + lnb_ref[...]
    y_ref[...] = ((y + bonus_ref[...]) * g_ref[...]).astype(y_ref.dtype)


def rwkv_post(o, bonus, g, ln_w, ln_b, seg, seg_t, *, tm=512):
    tokens, d = o.shape
    tm = min(tm, tokens)
    blk = pl.BlockSpec((tm, d), lambda i: (i, 0))
    vec = pl.BlockSpec((1, d), lambda i: (0, 0))
    return pl.pallas_call(
        _rwkv_post_kernel,
        grid=(tokens // tm,),
        in_specs=[blk, blk, blk, vec, vec,
                  pl.BlockSpec(seg.shape, lambda i: (0, 0)), pl.BlockSpec(seg_t.shape, lambda i: (0, 0))],
        out_specs=blk,
        out_shape=jax.ShapeDtypeStruct((tokens, d), BF16),
        compiler_params=_cparams(("parallel",)),
        name="rwkv_post",
    )(o, bonus, g, ln_w.reshape(1, d), ln_b.reshape(1, d), seg, seg_t)


def _t5_thresholds():
    max_exact = REL_BUCKETS // 2
    thr = list(range(1, max_exact))
    n = max_exact
    for bucket in range(max_exact, REL_BUCKETS):
        while True:
            large = max_exact + int(math.log(max(n, max_exact) / max_exact)
                                    / math.log(REL_MAX_DIST / max_exact) * (REL_BUCKETS - max_exact))
            if min(large, REL_BUCKETS - 1) >= bucket:
                break
            n += 1
        thr.append(n)
    return thr


T5_THRESHOLDS = _t5_thresholds()
T5_FAR = T5_THRESHOLDS[-1]


def _softmax_tiles(s_list, c_list, states, vt_list):
    stats = []
    for s_t, c, (m_old, l_old, _) in zip(s_list, c_list, states):
        m_new = jnp.maximum(m_old, jnp.max(s_t, axis=0, keepdims=True) + c)
        alpha = jnp.exp2(m_old - m_new)
        p_t = jnp.exp2(s_t - (m_new - c))
        stats.append((m_new, alpha, alpha * l_old + jnp.sum(p_t, axis=0, keepdims=True), p_t.astype(BF16)))
    return tuple((m_new, l_new, alpha * acc + _dot(vt, p_t))
                 for (m_new, alpha, l_new, p_t), (_, _, acc), vt in zip(stats, states, vt_list))


def _transpose_into(vt_ref, v_ref, chunk):
    seq = v_ref.shape[0]
    for c in range(seq // chunk):
        vt_ref[:, c * chunk:(c + 1) * chunk] = v_ref[c * chunk:(c + 1) * chunk, :].astype(F32).T.astype(BF16)


def _diff_attn_kernel(qfirst_ref, klast_ref, q_ref, k_ref, v_ref, qpos_ref, kpos_ref, subln_ref, table_ref, lam_ref,
                      o_ref, vt_ref, *, tq, tk, scale2, out_scale):
    b, i = pl.program_id(0), pl.program_id(1)
    nq = pl.num_programs(1)
    seq = k_ref.shape[0]
    nk = seq // tk
    w = DF_V_DIM

    @pl.when(i == 0)
    def _():
        _transpose_into(vt_ref, v_ref, tk)

    n_tiles = (i * tq + tq - 1) // tk + 1
    qf = qfirst_ref[b * nq + i]
    n_far = lax.while_loop(
        lambda j: (j * tk + tk - 1 <= i * tq) & (qf - klast_ref[b * nk + jnp.minimum(j, nk - 1)] >= T5_FAR),
        lambda j: j + 1, jnp.int32(0))

    dist = lax.broadcasted_iota(jnp.int32, (1, LANES), 1)
    qpos = qpos_ref[...]
    q_idx = i * tq + lax.broadcasted_iota(jnp.int32, (tk, tq), 1)
    k_off = lax.broadcasted_iota(jnp.int32, (tk, tq), 0)
    lane = lax.broadcasted_iota(jnp.int32, (tq, w), 1)

    bias_rows, c_far, qm = [], [], []
    for h in range(DF_HEADS):
        bias_vec = jnp.full((1, LANES), table_ref[h], F32)
        for bucket, thr in enumerate(T5_THRESHOLDS, start=1):
            bias_vec = jnp.where(dist >= thr, table_ref[bucket * DF_HEADS + h], bias_vec)
        bias_rows.append(jnp.broadcast_to(bias_vec * LOG2E, (tk, LANES)))
        c_far.append(table_ref[(REL_BUCKETS - 1) * DF_HEADS + h] * LOG2E)
        qh = q_ref[:, h * w:(h + 1) * w].astype(F32) * scale2
        qm.append([jnp.where((lane >= mi * DF_HEAD_DIM) & (lane < (mi + 1) * DF_HEAD_DIM), qh, 0.0).astype(BF16)
                   for mi in range(2)])

    def tiles(j, h):
        off = pl.multiple_of(j * tk, tk)
        return k_ref[pl.ds(off, tk), h * w:(h + 1) * w], vt_ref[h * w:(h + 1) * w, pl.ds(off, tk)], off

    chains = [(h, mi) for h in range(DF_HEADS) for mi in range(2)]

    def far_body(j, st):
        kv = [tiles(j, h) for h in range(DF_HEADS)]
        s = [_dot_t(kv[h][0], qm[h][mi]) for h, mi in chains]
        return _softmax_tiles(s, [c_far[h] for h, _ in chains], st, [kv[h][1] for h, _ in chains])

    def near_body(j, st):
        off = pl.multiple_of(j * tk, tk)
        n = jnp.clip(qpos - kpos_ref[pl.ds(off, tk), :], 0, LANES - 1)
        keep = q_idx >= off + k_off
        kv = [tiles(j, h) for h in range(DF_HEADS)]
        bias = [jnp.concatenate(
            [jnp.take_along_axis(bias_rows[h], n[:, cb * LANES:(cb + 1) * LANES], axis=1)
             for cb in range(tq // LANES)], axis=1) for h in range(DF_HEADS)]
        s = [jnp.where(keep, _dot_t(kv[h][0], qm[h][mi]) + bias[h], NEG_BIG) for h, mi in chains]
        return _softmax_tiles(s, [0.0] * len(chains), st, [kv[h][1] for h, _ in chains])

    init = tuple((jnp.full((1, tq), NEG_BIG, F32), jnp.zeros((1, tq), F32), jnp.zeros((w, tq), F32))
                 for _ in range(2 * DF_HEADS))
    st = lax.fori_loop(0, n_far, far_body, init)
    st = lax.fori_loop(n_far, n_tiles, near_body, st)
    for h in range(DF_HEADS):
        s0, s1 = st[2 * h], st[2 * h + 1]
        d_t = s0[2] / s0[1] - lam_ref[0] * (s1[2] / s1[1])
        ms = jnp.mean(d_t * d_t, axis=0, keepdims=True)
        y_t = d_t * lax.rsqrt(ms + DF_SUBLN_EPS) * (subln_ref[...] * out_scale)
        o_ref[:, h * w:(h + 1) * w] = y_t.T.astype(o_ref.dtype)


def diff_attention(pa, positions, rel_bias, lam, lambda_init, subln_w, *, tq=ATTN_TILE):
    batch, seq = positions.shape
    tokens = batch * seq
    tq = min(tq, seq)
    tk = tq
    nq, nk = seq // tq, seq // tk
    qfirst = positions[:, ::tq].reshape(-1)
    klast = positions[:, tk - 1::tk].reshape(-1)
    qpos = positions.reshape(batch, 1, seq)
    kpos = positions.reshape(batch, seq, 1)
    wd = DF_DIM
    grid_spec = pltpu.PrefetchScalarGridSpec(
        num_scalar_prefetch=2,
        grid=(batch, nq),
        in_specs=[pl.BlockSpec((tq, wd), lambda b, i, *_: (b * nq + i, OFF_DQ // wd)),
                  pl.BlockSpec((seq, wd), lambda b, i, *_: (b, OFF_DK // wd)),
                  pl.BlockSpec((seq, wd), lambda b, i, *_: (b, OFF_DV // wd)),
                  pl.BlockSpec((None, 1, tq), lambda b, i, *_: (b, 0, i)),
                  pl.BlockSpec((None, seq, 1), lambda b, i, *_: (b, 0, 0)),
                  pl.BlockSpec((DF_V_DIM, 1), lambda b, i, *_: (0, 0)),
                  pl.BlockSpec(memory_space=pltpu.SMEM),
                  pl.BlockSpec(memory_space=pltpu.SMEM)],
        out_specs=pl.BlockSpec((tq, wd), lambda b, i, *_: (b * nq + i, 0)),
        scratch_shapes=[pltpu.VMEM((wd, seq), BF16)],
    )
    return pl.pallas_call(
        functools.partial(_diff_attn_kernel, tq=tq, tk=tk, scale2=DF_HEAD_DIM ** -0.5 * LOG2E,
                          out_scale=1.0 - lambda_init),
        grid_spec=grid_spec,
        out_shape=jax.ShapeDtypeStruct((tokens, DF_DIM), BF16),
        compiler_params=_cparams(("arbitrary", "arbitrary")),
        name="diff_attention",
    )(qfirst, klast, pa, pa, pa, qpos, kpos, subln_w.reshape(DF_V_DIM, 1), rel_bias.reshape(-1), lam.reshape(1))


ML_QK_PAD = 2 * LANES


def _rope_block(x, cos, sin):
    half = ML_ROPE // 2
    lane = lax.broadcasted_iota(jnp.int32, x.shape, 1)
    rot = jnp.where(lane < half, -pltpu.roll(x, LANES - half, axis=1),
                    jnp.where(lane < ML_ROPE, pltpu.roll(x, half, axis=1), 0.0))
    return x * cos + rot * sin


def _mla_prep_kernel(mq_ref, mkv_ref, kpe_ref, pos_ref, qn_w, kvn_w, wq_ref, wkv_ref, freq_ref,
                     qf_o, kf_o, v_o, *, qscale):
    ang = pos_ref[...].astype(F32) * freq_ref[...]
    cos, sin = jnp.cos(ang), jnp.sin(ang)
    qc = _rms(mq_ref[...].astype(F32), qn_w[...], NORM_EPS).astype(BF16)
    q_all = _dot(qc, wq_ref[...]) * qscale
    kvc = _rms(mkv_ref[...].astype(F32), kvn_w[...], NORM_EPS).astype(BF16)
    kvb = _dot(kvc, wkv_ref[...])
    kpe = _rope_block(kpe_ref[...], cos, sin).astype(BF16)
    nope_w = ML_HEADS * ML_NOPE
    for h in range(ML_HEADS):
        lo = h * ML_QK_PAD
        qf_o[:, lo:lo + LANES] = q_all[:, h * LANES:(h + 1) * LANES].astype(BF16)
        qf_o[:, lo + LANES:lo + 2 * LANES] = _rope_block(
            q_all[:, nope_w + h * LANES:nope_w + (h + 1) * LANES], cos, sin).astype(BF16)
        kf_o[:, lo:lo + LANES] = kvb[:, h * LANES:(h + 1) * LANES].astype(BF16)
        kf_o[:, lo + LANES:lo + 2 * LANES] = kpe
    v_o[...] = kvb[:, nope_w:].astype(BF16)


def mla_prep(pa, pf, positions, q_norm, kv_norm, wq_all, wkv, *, tm=512):
    tokens = pa.shape[0]
    tm = min(tm, tokens)
    half = ML_ROPE // 2
    inv_freq = ROPE_THETA ** (-jnp.arange(half, dtype=F32) / half)
    freq = jnp.concatenate([inv_freq, inv_freq, jnp.zeros((LANES - ML_ROPE,), F32)]).reshape(1, LANES)

    def full(a):
        return pl.BlockSpec(a.shape, lambda i: (0, 0))

    qn_w = q_norm.reshape(1, -1)
    kvn_w = kv_norm.reshape(1, -1)
    wide = ML_HEADS * ML_QK_PAD
    return pl.pallas_call(
        functools.partial(_mla_prep_kernel, qscale=(ML_NOPE + ML_ROPE) ** -0.5 * LOG2E),
        grid=(tokens // tm,),
        in_specs=[pl.BlockSpec((tm, ML_Q_RANK), lambda i: (i, OFF_MQ // ML_Q_RANK)),
                  pl.BlockSpec((tm, ML_KV_RANK), lambda i: (i, OFF_MKV // ML_KV_RANK)),
                  pl.BlockSpec((tm, LANES), lambda i: (i, OFF_KPE // LANES)),
                  pl.BlockSpec((tm, 1), lambda i: (i, 0)),
                  full(qn_w), full(kvn_w), full(wq_all), full(wkv), full(freq)],
        out_specs=[pl.BlockSpec((tm, wide), lambda i: (i, 0)),
                   pl.BlockSpec((tm, wide), lambda i: (i, 0)),
                   pl.BlockSpec((tm, ML_DIM), lambda i: (i, 0))],
        out_shape=[jax.ShapeDtypeStruct((tokens, wide), BF16),
                   jax.ShapeDtypeStruct((tokens, wide), BF16),
                   jax.ShapeDtypeStruct((tokens, ML_DIM), BF16)],
        compiler_params=_cparams(("parallel",)),
        name="mla_prep",
    )(pa, pa, pf, positions.reshape(tokens, 1), qn_w, kvn_w, wq_all, wkv, freq)


def _mla_attn_kernel(q_ref, k_ref, v_ref, o_ref, vt_ref, *, tq, tk):
    i = pl.program_id(1)
    wq = ML_QK_PAD

    @pl.when(i == 0)
    def _():
        _transpose_into(vt_ref, v_ref, tk)

    n_tiles = (i * tq + tq - 1) // tk + 1
    n_full = (i * tq + 1) // tk
    q_idx = i * tq + lax.broadcasted_iota(jnp.int32, (tk, tq), 1)
    k_off = lax.broadcasted_iota(jnp.int32, (tk, tq), 0)
    qh = [q_ref[:, h * wq:(h + 1) * wq] for h in range(ML_HEADS)]

    def tiles(j, h):
        off = pl.multiple_of(j * tk, tk)
        return (k_ref[pl.ds(off, tk), h * wq:(h + 1) * wq],
                vt_ref[h * ML_V:(h + 1) * ML_V, pl.ds(off, tk)], off)

    heads = range(ML_HEADS)

    def full_body(j, st):
        kv = [tiles(j, h) for h in heads]
        s = [_dot_t(kv[h][0], qh[h]) for h in heads]
        return _softmax_tiles(s, [0.0] * ML_HEADS, st, [kv[h][1] for h in heads])

    def diag_body(j, st):
        kv = [tiles(j, h) for h in heads]
        keep = q_idx >= kv[0][2] + k_off
        s = [jnp.where(keep, _dot_t(kv[h][0], qh[h]), NEG_BIG) for h in heads]
        return _softmax_tiles(s, [0.0] * ML_HEADS, st, [kv[h][1] for h in heads])

    st = tuple((jnp.full((1, tq), NEG_BIG, F32), jnp.zeros((1, tq), F32), jnp.zeros((ML_V, tq), F32))
               for _ in range(ML_HEADS))
    st = lax.fori_loop(0, n_full, full_body, st)
    st = lax.fori_loop(n_full, n_tiles, diag_body, st)
    for h in range(ML_HEADS):
        o_ref[:, h * ML_V:(h + 1) * ML_V] = (st[h][2] / st[h][1]).T.astype(o_ref.dtype)


def mla_attention(qf, kf, v, batch, *, tq=ATTN_TILE):
    tokens = qf.shape[0]
    seq = tokens // batch
    tq = min(tq, seq)
    tk = tq
    nq = seq // tq
    wide = qf.shape[1]
    return pl.pallas_call(
        functools.partial(_mla_attn_kernel, tq=tq, tk=tk),
        grid=(batch, nq),
        in_specs=[pl.BlockSpec((tq, wide), lambda b, i: (b * nq + i, 0)),
                  pl.BlockSpec((seq, wide), lambda b, i: (b, 0)),
                  pl.BlockSpec((seq, ML_DIM), lambda b, i: (b, 0))],
        out_specs=pl.BlockSpec((tq, ML_DIM), lambda b, i: (b * nq + i, 0)),
        out_shape=jax.ShapeDtypeStruct((tokens, ML_DIM), BF16),
        scratch_shapes=[pltpu.VMEM((ML_DIM, seq), BF16)],
        compiler_params=_cparams(("arbitrary", "arbitrary")),
        name="mla_attention",
    )(qf, kf, v)


def _cross_kernel(x_ref, nw_ref, wq_ref, kv_ref, wo_ref, o_ref):
    x = x_ref[...]
    q = _dot(_rms(x, nw_ref[...], NORM_EPS).astype(BF16), wq_ref[...])
    kv = kv_ref[...]
    scale = CA_HEAD_DIM ** -0.5
    outs = []
    for hh in range(CA_HEADS):
        sl = slice(hh * CA_HEAD_DIM, (hh + 1) * CA_HEAD_DIM)
        s = _dot_t(q[:, sl].astype(BF16), kv[:, sl]) * scale
        p = jnp.exp(s - jnp.max(s, axis=-1, keepdims=True))
        p = p / jnp.sum(p, axis=-1, keepdims=True)
        outs.append(_dot(p.astype(BF16), kv[:, CA_DIM + hh * CA_HEAD_DIM:CA_DIM + (hh + 1) * CA_HEAD_DIM]))
    o = jnp.concatenate(outs, axis=1).astype(BF16)
    o_ref[...] = x + _dot(o, wo_ref[...])


def cross_block(x, batch, norm_w, wq, kv, wo, *, tq=512):
    tokens, d = x.shape
    seq = tokens // batch
    tq = min(tq, seq)
    nq = seq // tq
    mem_len = kv.shape[0] // batch
    return pl.pallas_call(
        _cross_kernel,
        grid=(batch, nq),
        in_specs=[pl.BlockSpec((tq, d), lambda b, i: (b * nq + i, 0)),
                  pl.BlockSpec((1, d), lambda b, i: (0, 0)),
                  pl.BlockSpec(wq.shape, lambda b, i: (0, 0)),
                  pl.BlockSpec((mem_len, 2 * CA_DIM), lambda b, i: (b, 0)),
                  pl.BlockSpec(wo.shape, lambda b, i: (0, 0))],
        out_specs=pl.BlockSpec((tq, d), lambda b, i: (b * nq + i, 0)),
        out_shape=jax.ShapeDtypeStruct((tokens, d), F32),
        compiler_params=_cparams(("parallel", "parallel")),
        name="cross_block",
    )(x, norm_w.reshape(1, d), wq, kv, wo)


def _router_kernel(x_ref, nw_ref, wr_ref, br_ref, h_ref, sel_ref):
    h = _rms(x_ref[...], nw_ref[...], NORM_EPS)
    h_ref[...] = h.astype(h_ref.dtype)
    logits = _dot_x3(h, wr_ref[...])
    biased = logits + br_ref[...]
    lane = lax.broadcasted_iota(jnp.int32, logits.shape, 1)
    big = jnp.int32(LANES)

    def first_argmax(vals):
        mx = jnp.max(vals, axis=-1, keepdims=True)
        return jnp.min(jnp.where(vals == mx, lane, big), axis=-1, keepdims=True)

    def pick(vals, idx):
        return jnp.sum(jnp.where(lane == idx, vals, 0.0), axis=-1, keepdims=True)

    is_group = (lane >= MOE_EXPERTS) & (lane < MOE_EXPERTS + MOE_GROUPS)
    gl = jnp.where(is_group, logits, NEG_BIG)
    ge = jnp.exp(gl - jnp.max(gl, axis=-1, keepdims=True))
    gp = ge / jnp.sum(ge, axis=-1, keepdims=True)
    g_lane = first_argmax(jnp.where(is_group, biased, NEG_BIG))
    p_group = pick(gp, g_lane)
    lo = (g_lane - MOE_EXPERTS) * MOE_PER_GROUP
    in_group = (lane >= lo) & (lane < lo + MOE_PER_GROUP)
    eb = jnp.where(in_group, biased, NEG_BIG)
    i1 = first_argmax(eb)
    i2 = first_argmax(jnp.where(lane == i1, NEG_BIG, eb))
    l1, l2 = pick(logits, i1), pick(logits, i2)
    mx = jnp.maximum(l1, l2)
    e1, e2 = jnp.exp(l1 - mx), jnp.exp(l2 - mx)
    w1, w2 = e1 / (e1 + e2), e2 / (e1 + e2)
    sel_ref[...] = jnp.where(lane == SEL_E1, i1.astype(F32),
                             jnp.where(lane == SEL_E2, i2.astype(F32),
                                       jnp.where(lane == SEL_G1, w1 * p_group,
                                                 jnp.where(lane == SEL_G2, w2 * p_group, 0.0))))


SEL_E1, SEL_E2, SEL_G1, SEL_G2 = 0, 1, 2, 3


def moe_router(x, norm_w, w_router, b_router, *, tm=512):
    tokens, d = x.shape
    tm = min(tm, tokens)
    return pl.pallas_call(
        _router_kernel,
        grid=(tokens // tm,),
        in_specs=[pl.BlockSpec((tm, d), lambda i: (i, 0)),
                  pl.BlockSpec((1, d), lambda i: (0, 0)),
                  pl.BlockSpec((d, LANES), lambda i: (0, 0)),
                  pl.BlockSpec((1, LANES), lambda i: (0, 0))],
        out_specs=[pl.BlockSpec((tm, d), lambda i: (i, 0)),
                   pl.BlockSpec((tm, LANES), lambda i: (i, 0))],
        out_shape=[jax.ShapeDtypeStruct((tokens, d), F32),
                   jax.ShapeDtypeStruct((tokens, LANES), F32)],
        compiler_params=_cparams(("parallel",)),
        name="moe_router",
    )(x, norm_w.reshape(1, d), w_router, b_router)


def _moe_rank_kernel(sel_ref, ltri_ref, rank_ref, counts_ref, carry_ref):
    @pl.when(pl.program_id(0) == 0)
    def _():
        carry_ref[...] = jnp.zeros_like(carry_ref)

    sel = sel_ref[...]
    lane = lax.broadcasted_iota(jnp.int32, sel.shape, 1)
    lane_f = lane.astype(F32)
    oh1 = lane_f == sel[:, SEL_E1:SEL_E1 + 1]
    oh2 = lane_f == sel[:, SEL_E2:SEL_E2 + 1]
    f1, f2 = oh1.astype(F32), oh2.astype(F32)
    ltri = ltri_ref[...]
    before1 = _dot(ltri, f1.astype(BF16))
    before2 = _dot(ltri, f2.astype(BF16))
    c1 = jnp.sum(f1, axis=0, keepdims=True)
    c2 = jnp.sum(f2, axis=0, keepdims=True)
    carry = carry_ref[...]
    r1 = jnp.sum(jnp.where(oh1, before1 + carry, 0.0), axis=1, keepdims=True)
    r2 = jnp.sum(jnp.where(oh2, before2 + carry + c1, 0.0), axis=1, keepdims=True)
    rank_ref[...] = jnp.where(lane == SEL_E1, r1, jnp.where(lane == SEL_E2, r2, 0.0)).astype(jnp.int32)
    total = carry + c1 + c2
    carry_ref[...] = total
    counts_ref[...] = total.astype(jnp.int32)


def moe_rank(sel, *, tm=512):
    tokens = sel.shape[0]
    tm = min(tm, tokens)
    ltri = (jnp.arange(tm)[:, None] > jnp.arange(tm)[None, :]).astype(BF16)
    return pl.pallas_call(
        _moe_rank_kernel,
        grid=(tokens // tm,),
        in_specs=[pl.BlockSpec((tm, LANES), lambda i: (i, 0)),
                  pl.BlockSpec((tm, tm), lambda i: (0, 0))],
        out_specs=[pl.BlockSpec((tm, LANES), lambda i: (i, 0)),
                   pl.BlockSpec((1, LANES), lambda i: (0, 0))],
        out_shape=[jax.ShapeDtypeStruct((tokens, LANES), jnp.int32),
                   jax.ShapeDtypeStruct((1, LANES), jnp.int32)],
        scratch_shapes=[pltpu.VMEM((1, LANES), F32)],
        compiler_params=_cparams(("arbitrary",)),
        name="moe_rank",
    )(sel, ltri)


def _row_copy(src_ref, src_row, dst_ref, dst_row, sem):
    return pltpu.make_async_copy(src_ref.at[pl.ds(src_row, 1)], dst_ref.at[pl.ds(dst_row, 1)], sem)


def _moe_dispatch_kernel(dest_ref, tail_ref, h_ref, xs_ref, hbuf, zero_ref, lsem, ssem, zsem, *, tm):
    step = pl.program_id(0)
    n_steps = pl.num_programs(0)
    base = step * (2 * tm)
    slot = step % 3

    def load(tile, buf_slot):
        return pltpu.make_async_copy(h_ref.at[pl.ds(pl.multiple_of(tile * tm, tm), tm)], hbuf.at[buf_slot],
                                     lsem.at[buf_slot])

    def wait_scatter(buf_slot):
        for _ in range(2):
            pltpu.make_async_copy(hbuf.at[buf_slot], xs_ref.at[pl.ds(0, tm)], ssem.at[buf_slot]).wait()

    @pl.when(step == 0)
    def _():
        load(0, 0).start()

        @pl.when(n_steps > 1)
        def _():
            load(1, 1).start()

    @pl.when(step == 0)
    def _():
        zero_ref[...] = jnp.zeros_like(zero_ref)

        def fill(tail):
            return pltpu.make_async_copy(zero_ref, xs_ref.at[pl.ds(pl.multiple_of(tail, 8), zero_ref.shape[0])], zsem)

        def start(e, carry):
            @pl.when(tail_ref[e] >= 0)
            def _():
                fill(tail_ref[e]).start()
            return carry

        def wait(e, carry):
            @pl.when(tail_ref[e] >= 0)
            def _():
                fill(tail_ref[e]).wait()
            return carry

        lax.fori_loop(0, tail_ref.shape[0], start, 0)
        lax.fori_loop(0, tail_ref.shape[0], wait, 0)

    load(step, slot).wait()

    def issue(r, carry):
        for s in range(2):
            _row_copy(hbuf.at[slot], r, xs_ref, dest_ref[base + 2 * r + s], ssem.at[slot]).start()
        return carry

    lax.fori_loop(0, tm, issue, 0, unroll=DMA_ISSUE_UNROLL)

    @pl.when(step >= 1)
    def _():
        wait_scatter((step + 2) % 3)

    @pl.when(step + 2 < n_steps)
    def _():
        load(step + 2, (step + 2) % 3).start()

    @pl.when(step == n_steps - 1)
    def _():
        wait_scatter(slot)


def moe_dispatch(h, dest, tails, rows, row_tile, *, tm=256):
    tokens, d = h.shape
    tm = min(tm, tokens)
    grid_spec = pltpu.PrefetchScalarGridSpec(
        num_scalar_prefetch=2,
        grid=(tokens // tm,),
        in_specs=[pl.BlockSpec(memory_space=pl.ANY)],
        out_specs=pl.BlockSpec(memory_space=pl.ANY),
        scratch_shapes=[pltpu.VMEM((3, tm, d), h.dtype), pltpu.VMEM((row_tile, d), h.dtype),
                        pltpu.SemaphoreType.DMA((3,)), pltpu.SemaphoreType.DMA((3,)),
                        pltpu.SemaphoreType.DMA(())],
    )
    return pl.pallas_call(
        functools.partial(_moe_dispatch_kernel, tm=tm),
        grid_spec=grid_spec,
        out_shape=jax.ShapeDtypeStruct((rows, d), h.dtype),
        compiler_params=_cparams(("arbitrary",)),
        name="moe_dispatch",
    )(dest, tails, h)


def _moe_expert_kernel(te_ref, nused_ref, xs_ref, wg_ref, wu_ref, wd_ref, ys_ref, wgb, wub, wdb):
    r = pl.program_id(0)
    used = r < nused_ref[0]
    changed = (r == 0) | (te_ref[r] != te_ref[jnp.maximum(r - 1, 0)])

    @pl.when(used & changed)
    def _():
        wgb[...] = wg_ref[...].astype(BF16)
        wub[...] = wu_ref[...].astype(BF16)
        wdb[...] = wd_ref[...].astype(BF16)

    @pl.when(used)
    def _():
        x = xs_ref[...].astype(BF16)
        gate_pre = _dot(x, wgb[...])
        hid = (gate_pre * jax.nn.sigmoid(gate_pre)) * _dot(x, wub[...])
        ys_ref[...] = _dot(hid.astype(BF16), wdb[...])

    @pl.when(jnp.logical_not(used))
    def _():
        ys_ref[...] = jnp.zeros_like(ys_ref)


def moe_experts(xs, tile_expert, n_used, wg, wu, wd, layer, *, tm):
    rows, d = xs.shape
    de = wg.shape[-1]
    grid_spec = pltpu.PrefetchScalarGridSpec(
        num_scalar_prefetch=2,
        grid=(rows // tm,),
        in_specs=[pl.BlockSpec((tm, d), lambda r, te, nu: (jnp.minimum(r, nu[0] - 1), 0)),
                  pl.BlockSpec((None, None, d, de), lambda r, te, nu: (layer, te[r], 0, 0)),
                  pl.BlockSpec((None, None, d, de), lambda r, te, nu: (layer, te[r], 0, 0)),
                  pl.BlockSpec((None, None, de, d), lambda r, te, nu: (layer, te[r], 0, 0))],
        out_specs=pl.BlockSpec((tm, d), lambda r, te, nu: (r, 0)),
        scratch_shapes=[pltpu.VMEM((d, de), BF16), pltpu.VMEM((d, de), BF16), pltpu.VMEM((de, d), BF16)],
    )
    return pl.pallas_call(
        _moe_expert_kernel,
        grid_spec=grid_spec,
        out_shape=jax.ShapeDtypeStruct((rows, d), F32),
        compiler_params=_cparams(("arbitrary",)),
        name="moe_experts",
    )(tile_expert, n_used, xs, wg, wu, wd)


def _moe_combine_kernel(dest_ref, x_ref, sel_ref, nw_ref, ys_ref, o_ref, buf_ref, sem, *, tm, normalize):
    step = pl.program_id(0)
    slot = step % 2

    def gather(tile, buf_slot):
        base = tile * (2 * tm)

        def issue(r, carry):
            for s in range(2):
                _row_copy(ys_ref, dest_ref[base + 2 * r + s], buf_ref.at[buf_slot, s], r, sem.at[buf_slot]).start()
            return carry

        lax.fori_loop(0, tm, issue, 0, unroll=DMA_ISSUE_UNROLL)

    @pl.when(step == 0)
    def _():
        gather(0, 0)

    @pl.when(step + 1 < pl.num_programs(0))
    def _():
        gather(step + 1, 1 - slot)

    for s in range(2):
        pltpu.make_async_copy(ys_ref.at[pl.ds(0, tm)], buf_ref.at[slot, s], sem.at[slot]).wait()
    sel = sel_ref[...]
    out = x_ref[...] + sel[:, SEL_G1:SEL_G1 + 1] * buf_ref[slot, 0] + sel[:, SEL_G2:SEL_G2 + 1] * buf_ref[slot, 1]
    o_ref[...] = _rms(out, nw_ref[...], NORM_EPS) if normalize else out


def moe_combine(x, sel, ys, dest, final_norm=None, *, tm=256):
    tokens, d = x.shape
    tm = min(tm, tokens)
    normalize = final_norm is not None
    nw = (final_norm if normalize else jnp.ones((d,), F32)).reshape(1, d)
    grid_spec = pltpu.PrefetchScalarGridSpec(
        num_scalar_prefetch=1,
        grid=(tokens // tm,),
        in_specs=[pl.BlockSpec((tm, d), lambda i, *_: (i, 0)),
                  pl.BlockSpec((tm, LANES), lambda i, *_: (i, 0)),
                  pl.BlockSpec((1, d), lambda i, *_: (0, 0)),
                  pl.BlockSpec(memory_space=pl.ANY)],
        out_specs=pl.BlockSpec((tm, d), lambda i, *_: (i, 0)),
        scratch_shapes=[pltpu.VMEM((2, 2, tm, d), F32), pltpu.SemaphoreType.DMA((2,))],
    )
    return pl.pallas_call(
        functools.partial(_moe_combine_kernel, tm=tm, normalize=normalize),
        grid_spec=grid_spec,
        out_shape=jax.ShapeDtypeStruct((tokens, d), F32),
        compiler_params=_cparams(("arbitrary",)),
        name="moe_combine",
    )(dest, x, sel, nw, ys)


def moe_block(x, norm_w, w_router, b_router, wg, wu, wd, layer, final_norm=None, *, tm=MOE_ROW_TILE):
    tokens, d = x.shape
    n_exp = wg.shape[1]
    h, sel = moe_router(x, norm_w, w_router, b_router)
    rank, counts = moe_rank(sel)
    padded = (counts[0, :n_exp] + (tm - 1)) // tm * tm
    ends = jnp.cumsum(padded)
    starts = ends - padded
    experts = sel[:, SEL_E1:SEL_E2 + 1].astype(jnp.int32)
    start_of = jnp.sum(jnp.where(experts[..., None] == jnp.arange(n_exp, dtype=jnp.int32), starts, 0), axis=-1)
    dest = (start_of + rank[:, SEL_E1:SEL_E2 + 1]).reshape(-1)
    rows = 2 * tokens + n_exp * tm
    tile_start = jnp.arange(rows // tm, dtype=jnp.int32) * tm
    tile_expert = jnp.minimum(jnp.sum(tile_start[:, None] >= ends[None, :], axis=1), n_exp - 1).astype(jnp.int32)
    n_used = (ends[-1] // tm).astype(jnp.int32).reshape(1)
    tails = jnp.concatenate([jnp.where(padded > 0, ends - tm, -1),
                             jnp.where(tile_start >= ends[-1], tile_start, -1)]).astype(jnp.int32)
    xs = moe_dispatch(h, dest, tails, rows, tm)
    ys = moe_experts(xs, tile_expert, n_used, wg, wu, wd, layer, tm=tm)
    return moe_combine(x, sel, ys, dest, final_norm)


def _proj_weights(w_in_l, w_vres_l):
    d = w_in_l.shape[0]
    mla0 = RW_COLS + DF_COLS
    vres = jnp.zeros((d, RW_V_RANK), F32) if w_vres_l is None else w_vres_l
    part_f = [w_in_l[:, 3 * RW_DIM:RW_COLS],
              vres, jnp.zeros((d, LANES - RW_V_RANK), F32),
              w_in_l[:, mla0 + ML_Q_RANK + ML_KV_RANK:mla0 + ML_COLS], jnp.zeros((d, LANES - ML_ROPE), F32)]
    part_a = [w_in_l[:, :3 * RW_DIM],
              w_in_l[:, RW_COLS:RW_COLS + DF_COLS],
              w_in_l[:, mla0 + ML_Q_RANK:mla0 + ML_Q_RANK + ML_KV_RANK],
              jnp.zeros((d, OFF_MQ - OFF_MKV - ML_KV_RANK), F32),
              w_in_l[:, mla0:mla0 + ML_Q_RANK]]
    return jnp.concatenate(part_f, axis=1).astype(BF16), jnp.concatenate(part_a, axis=1).astype(BF16)


def _pad_rows(w, rows, at=0):
    out = jnp.zeros((rows, w.shape[1]), w.dtype)
    return lax.dynamic_update_slice(out, w, (at, 0))


def kernel(x, mem, positions, rel_bias, final_norm, norm_mix, w_in, w_in_vres, w_out, tm_mu, tm_mu_vres, tm_w0, tm_w2, tm_a0, tm_a2, tm_v0, tm_v2, tm_g2, tm_k_k, tm_k_a, tm_r_k, tm_ln_w, tm_ln_b, da_lq1, da_lk1, da_lq2, da_lk2, da_subln, mla_q_norm, mla_wq_b, mla_kv_norm, mla_wkv_b, norm_cross, norm_mem, ca_wq, ca_wkv, ca_wo, norm_ffn, moe_w_group, moe_b_group, moe_w_expert, moe_b_expert, moe_w_gate, moe_w_up, moe_w_down):
    batch, seq, d = x.shape
    tokens = batch * seq
    depth = norm_mix.shape[0]
    xf = x.reshape(tokens, d)
    memf = mem.reshape(-1, d)
    positions = positions.astype(jnp.int32)

    head_of_lane = jnp.arange(RW_DIM) // RW_HEAD_DIM
    seg = (head_of_lane[:, None] == jnp.arange(LANES)[None, :]).astype(BF16)
    seg_t = seg.T
    row = lambda v: v.reshape(1, -1)

    v_first = None
    for l in range(depth):
        w_f, w_a = _proj_weights(w_in[l], None if l == 0 else w_in_vres[l - 1])
        pa, proj = norm_matmul(xf, norm_mix[l], w_a, w_f, out_dtype=BF16, tm=1024, tn=PROJ_A_COLS // 3)

        mu = tm_mu[l]
        prm = dict(mu_r=row(mu[:RW_DIM]), mu_k=row(mu[RW_DIM:2 * RW_DIM]), mu_v=row(mu[2 * RW_DIM:3 * RW_DIM]),
                   mu_l=row(mu[3 * RW_DIM:]), w0=row(tm_w0[l]), a0=row(tm_a0[l]),
                   w2=_pad_rows(tm_w2[l], LANES, 0), a2=_pad_rows(tm_a2[l], LANES, RW_W_RANK),
                   g2=tm_g2[l].astype(BF16), k_k=row(tm_k_k[l]), k_a=row(tm_k_a[l]), r_k=row(tm_r_k[l]),
                   seg=seg, seg_t=seg_t)
        if l > 0:
            prm.update(mu_vr=jnp.pad(row(tm_mu_vres[l - 1]), ((0, 0), (0, LANES - RW_V_RANK))),
                       v0=row(tm_v0[l - 1]), v2=_pad_rows(tm_v2[l - 1], LANES, 0))
        r, lw, k, v, kap, beta, gate, bonus = rwkv_prep(pa, proj, batch, v_first, prm)
        if l == 0:
            v_first = v
        o = rwkv_scan(r, lw, k, v, kap, beta, batch)
        y_a = rwkv_post(o, bonus, gate, tm_ln_w[l], tm_ln_b[l], seg, seg_t)

        lambda_init = 0.8 - 0.6 * math.exp(-0.3 * l)
        lam = (jnp.exp(jnp.sum(da_lq1[l] * da_lk1[l])) - jnp.exp(jnp.sum(da_lq2[l] * da_lk2[l])) + lambda_init)
        y_b = diff_attention(pa, positions, rel_bias, lam, lambda_init, da_subln[l])

        wq = mla_wq_b[l].reshape(ML_Q_RANK, ML_HEADS, ML_NOPE + ML_ROPE)
        wq_pe = jnp.pad(wq[:, :, ML_NOPE:], ((0, 0), (0, 0), (0, LANES - ML_ROPE)))
        wq_all = jnp.concatenate([wq[:, :, :ML_NOPE].reshape(ML_Q_RANK, -1),
                                  wq_pe.reshape(ML_Q_RANK, -1)], axis=1).astype(BF16)
        wkv = mla_wkv_b[l].reshape(ML_KV_RANK, ML_HEADS, ML_NOPE + ML_V)
        wkv = jnp.concatenate([wkv[:, :, :ML_NOPE].reshape(ML_KV_RANK, -1),
                               wkv[:, :, ML_NOPE:].reshape(ML_KV_RANK, -1)], axis=1).astype(BF16)
        qf, kf, v_mla = mla_prep(pa, proj, positions, mla_q_norm[l], mla_kv_norm[l], wq_all, wkv)
        y_c = mla_attention(qf, kf, v_mla, batch)

        wo = w_out[l].astype(BF16)
        xf = matmul_res([y_a, y_b, y_c],
                        [wo[:RW_DIM], wo[RW_DIM:RW_DIM + DF_DIM], wo[RW_DIM + DF_DIM:]], xf)

        kv_mem = norm_matmul(memf, norm_mem[l], ca_wkv[l].astype(BF16), out_dtype=BF16)
        xf = cross_block(xf, batch, norm_cross[l], ca_wq[l].astype(BF16), kv_mem, ca_wo[l].astype(BF16))

        w_router = jnp.concatenate(
            [moe_w_expert[l], moe_w_group[l], jnp.zeros((d, LANES - MOE_EXPERTS - MOE_GROUPS), F32)], axis=1)
        b_router = jnp.concatenate(
            [moe_b_expert[l], moe_b_group[l], jnp.zeros((LANES - MOE_EXPERTS - MOE_GROUPS,), F32)]).reshape(1, LANES)
        xf = moe_block(xf, norm_ffn[l], w_router, b_router, moe_w_gate, moe_w_up, moe_w_down, l,
                       final_norm if l == depth - 1 else None)

    return xf.reshape(batch, seq, d)
```

```python
import functools
import math

import jax
import jax.numpy as jnp
from jax import lax
from jax.experimental import pallas as pl
from jax.experimental.pallas import tpu as pltpu

F32 = jnp.float32
BF16 = jnp.bfloat16

NORM_EPS = 1e-6
ROPE_THETA = 10000.0

RW_HEADS = 16
RW_HEAD_DIM = 64
RW_DIM = RW_HEADS * RW_HEAD_DIM
RW_W_RANK = 64
RW_A_RANK = 64
RW_G_RANK = 128
RW_V_RANK = 32
RW_LORA = RW_W_RANK + RW_A_RANK + RW_G_RANK
RW_LN_EPS = 64e-5
RW_COLS = 3 * RW_DIM + RW_LORA

DF_HEADS = 4
DF_HEAD_DIM = 64
DF_V_DIM = 2 * DF_HEAD_DIM
DF_QK = DF_HEADS * 2 * DF_HEAD_DIM
DF_DIM = DF_HEADS * DF_V_DIM
DF_COLS = 2 * DF_QK + DF_DIM
DF_SUBLN_EPS = 1e-5

ML_HEADS = 4
ML_Q_RANK = 384
ML_KV_RANK = 256
ML_NOPE = 128
ML_ROPE = 64
ML_V = 128
ML_DIM = ML_HEADS * ML_V
ML_COLS = ML_Q_RANK + ML_KV_RANK + ML_ROPE

REL_BUCKETS = 32
REL_MAX_DIST = 128

CA_HEADS = 4
CA_HEAD_DIM = 128
CA_DIM = CA_HEADS * CA_HEAD_DIM

MOE_GROUPS = 4
MOE_PER_GROUP = 8
MOE_EXPERTS = MOE_GROUPS * MOE_PER_GROUP

LANES = 128
SCAN_CHUNK = 64
SCAN_GROUP = 4
SCAN_BATCHES = 2
ATTN_TILE = 512
DMA_ISSUE_UNROLL = 8
MOE_ROW_TILE = 256
VMEM_LIMIT = 56 * 1024 * 1024
NEG_BIG = -1e30

LOG2E = 1.4426950408889634

OFF_LORA = 0
OFF_VRES = OFF_LORA + RW_LORA
OFF_KPE = OFF_VRES + LANES
PROJ_F_COLS = OFF_KPE + LANES
OFF_R = 0
OFF_K = RW_DIM
OFF_V = 2 * RW_DIM
OFF_DQ = 3 * RW_DIM
OFF_DK = OFF_DQ + DF_QK
OFF_DV = OFF_DK + DF_QK
OFF_MKV = OFF_DV + DF_DIM
OFF_MQ = 13 * ML_Q_RANK
PROJ_A_COLS = OFF_MQ + ML_Q_RANK


def _cparams(sem, vmem=VMEM_LIMIT, flags=None):
    return pltpu.CompilerParams(dimension_semantics=sem, vmem_limit_bytes=vmem, flags=flags)


def _dot(a, b):
    return jnp.dot(a, b, preferred_element_type=F32)


def _dot_t(a, b):
    return lax.dot_general(a, b, (((1,), (1,)), ((), ())), preferred_element_type=F32)


def _split3(x):
    hi = x.astype(BF16)
    r1 = x - hi.astype(F32)
    mid = r1.astype(BF16)
    lo = (r1 - mid.astype(F32)).astype(BF16)
    return hi, mid, lo


def _dot_rhs01(x, ones_bf16):
    hi = x.astype(BF16)
    lo = (x - hi.astype(F32)).astype(BF16)
    return _dot(hi, ones_bf16) + _dot(lo, ones_bf16)


def _dot_x3(a, b):
    ah = a.astype(BF16)
    al = (a - ah.astype(F32)).astype(BF16)
    bh = b.astype(BF16)
    bl = (b - bh.astype(F32)).astype(BF16)
    return _dot(ah, bh) + _dot(ah, bl) + _dot(al, bh)


def _rms(x, w, eps):
    ms = jnp.mean(x * x, axis=-1, keepdims=True)
    return x * lax.rsqrt(ms + eps) * w


def _norm_matmul_kernel(*refs, eps, has_side):
    if has_side:
        x_ref, nw_ref, w_ref, ws_ref, o_ref, os_ref, xn_ref = refs
    else:
        x_ref, nw_ref, w_ref, o_ref, xn_ref = refs

    @pl.when(pl.program_id(1) == 0)
    def _():
        xn_ref[...] = _rms(x_ref[...], nw_ref[...], eps).astype(BF16)
        if has_side:
            os_ref[...] = _dot(xn_ref[...], ws_ref[...])

    o_ref[...] = _dot(xn_ref[...], w_ref[...]).astype(o_ref.dtype)


def norm_matmul(x, nw, w, w_side=None, *, out_dtype=F32, tm=512, tn=None, eps=NORM_EPS):
    m, d = x.shape
    n = w.shape[1]
    tm = min(tm, m)
    tn = n if tn is None else tn
    has_side = w_side is not None
    in_specs = [pl.BlockSpec((tm, d), lambda i, j: (i, 0)),
                pl.BlockSpec((1, d), lambda i, j: (0, 0)),
                pl.BlockSpec((d, tn), lambda i, j: (0, j))]
    out_specs = [pl.BlockSpec((tm, tn), lambda i, j: (i, j))]
    out_shape = [jax.ShapeDtypeStruct((m, n), out_dtype)]
    args = [x, nw.reshape(1, d), w]
    if has_side:
        ns = w_side.shape[1]
        in_specs.append(pl.BlockSpec((d, ns), lambda i, j: (0, 0)))
        out_specs.append(pl.BlockSpec((tm, ns), lambda i, j: (i, 0)))
        out_shape.append(jax.ShapeDtypeStruct((m, ns), F32))
        args.append(w_side)
    outs = pl.pallas_call(
        functools.partial(_norm_matmul_kernel, eps=eps, has_side=has_side),
        grid=(m // tm, n // tn),
        in_specs=in_specs,
        out_specs=out_specs,
        out_shape=out_shape,
        scratch_shapes=[pltpu.VMEM((tm, d), BF16)],
        compiler_params=_cparams(("parallel", "arbitrary")),
        name="norm_matmul",
    )(*args)
    return outs if has_side else outs[0]


def _matmul_res_kernel(*refs, n_a):
    a_refs, w_refs = refs[:n_a], refs[n_a:2 * n_a]
    res_ref, o_ref = refs[2 * n_a], refs[2 * n_a + 1]
    acc = res_ref[...]
    for a_ref, w_ref in zip(a_refs, w_refs):
        acc = acc + _dot(a_ref[...].astype(BF16), w_ref[...])
    o_ref[...] = acc


def matmul_res(a_list, w_list, res, *, tm=512, tn=2048):
    m, n = res.shape
    tm = min(tm, m)
    tn = min(tn, n)
    n_a = len(a_list)
    in_specs = ([pl.BlockSpec((tm, a.shape[1]), lambda i, j: (i, 0)) for a in a_list]
                + [pl.BlockSpec((w.shape[0], tn), lambda i, j: (0, j)) for w in w_list]
                + [pl.BlockSpec((tm, tn), lambda i, j: (i, j))])
    return pl.pallas_call(
        functools.partial(_matmul_res_kernel, n_a=n_a),
        grid=(m // tm, n // tn),
        in_specs=in_specs,
        out_specs=pl.BlockSpec((tm, tn), lambda i, j: (i, j)),
        out_shape=jax.ShapeDtypeStruct((m, n), F32),
        compiler_params=_cparams(("parallel", "arbitrary")),
        name="matmul_res",
    )(*a_list, *w_list, res)


def _softplus(z):
    return jnp.maximum(z, 0.0) + jnp.log(1.0 + jnp.exp(-jnp.abs(z)))


def _rwkv_prep_kernel(*refs, has_vres):
    if has_vres:
        (pr_ref, pk_ref, pv_ref, pl_ref, pvr_ref, vfirst_ref,
         mu_r, mu_k, mu_v, mu_l, mu_vr, w0, w2, a0, a2, g2, v0, v2,
         k_k, k_a, r_k, seg, seg_t,
         r_o, lw_o, k_o, v_o, kap_o, beta_o, g_o, bonus_o,
         last_r, last_k, last_v, last_l, last_vr) = refs
    else:
        (pr_ref, pk_ref, pv_ref, pl_ref,
         mu_r, mu_k, mu_v, mu_l, w0, w2, a0, a2, g2,
         k_k, k_a, r_k, seg, seg_t,
         r_o, lw_o, k_o, v_o, kap_o, beta_o, g_o, bonus_o,
         last_r, last_k, last_v, last_l) = refs
    t = pl.program_id(1)

    def shifted(p_ref, last_ref, mu_ref):
        p = p_ref[...].astype(F32)
        n = p.shape[0]
        carried = jnp.where(t == 0, 0.0, last_ref[0:1, :])
        row = lax.broadcasted_iota(jnp.int32, p.shape, 0)
        prev = jnp.where(row == 0, carried, pltpu.roll(p, 1, axis=0))
        last_ref[0:1, :] = p[n - 1:n, :]
        return p + mu_ref[...] * (prev - p)

    r = shifted(pr_ref, last_r, mu_r)
    k = shifted(pk_ref, last_k, mu_k)
    v = shifted(pv_ref, last_v, mu_v)
    lora = shifted(pl_ref, last_l, mu_l)
    wl = lora[:, :LANES]
    gl = lora[:, LANES:]

    lane = lax.broadcasted_iota(jnp.int32, wl.shape, 1)
    wl_t = jnp.where(lane < RW_W_RANK, jnp.tanh(wl), 0.0)
    al = jnp.where(lane >= RW_W_RANK, wl, 0.0)
    w_log = -_softplus(-(w0[...] + _dot_x3(wl_t, w2[...]))) - 0.5
    lw_o[...] = -jnp.exp(w_log)
    a = jax.nn.sigmoid(a0[...] + _dot_x3(al, a2[...]))
    g_o[...] = _dot(jax.nn.sigmoid(gl).astype(BF16), g2[...]).astype(g_o.dtype)

    segm, segm_t = seg[...], seg_t[...]

    def head_sum(x):
        return _dot_rhs01(_dot_rhs01(x, segm), segm_t)

    kk = k * k_k[...]
    kk = kk * lax.rsqrt(jnp.maximum(head_sum(kk * kk), 1e-24))
    k = k * (1.0 + (a - 1.0) * k_a[...])
    if has_vres:
        vr = shifted(pvr_ref, last_vr, mu_vr)
        mix = jax.nn.sigmoid(v0[...] + _dot_x3(vr, v2[...]))
        v = v + (vfirst_ref[...] - v) * mix
    r_o[...] = r.astype(r_o.dtype)
    k_o[...] = k.astype(k_o.dtype)
    v_o[...] = v.astype(v_o.dtype)
    kap_o[...] = kk.astype(kap_o.dtype)
    beta_o[...] = (kk * a).astype(beta_o.dtype)
    bonus_o[...] = (head_sum(r * k * r_k[...]) * v).astype(bonus_o.dtype)


def rwkv_prep(pa, proj, batch, vfirst, prm, *, tt=512):
    tokens = proj.shape[0]
    seq = tokens // batch
    tt = min(tt, seq)
    nt = seq // tt
    has_vres = vfirst is not None
    d = RW_DIM

    def rows(width, col):
        return pl.BlockSpec((tt, width), lambda b, t, col=col: (b * nt + t, col))

    def full(shape):
        return pl.BlockSpec(shape, lambda b, t: (0, 0))

    in_specs = [rows(d, OFF_R // d), rows(d, OFF_K // d), rows(d, OFF_V // d),
                rows(RW_LORA, OFF_LORA // RW_LORA)]
    args = [pa, pa, pa, proj]
    if has_vres:
        in_specs += [rows(LANES, OFF_VRES // LANES), rows(d, 0)]
        args += [proj, vfirst]
    names = ["mu_r", "mu_k", "mu_v", "mu_l"] + (["mu_vr"] if has_vres else []) + ["w0", "w2", "a0", "a2", "g2"]
    names += (["v0", "v2"] if has_vres else []) + ["k_k", "k_a", "r_k", "seg", "seg_t"]
    for nm in names:
        in_specs.append(full(prm[nm].shape))
        args.append(prm[nm])
    out_spec = pl.BlockSpec((tt, d), lambda b, t: (b * nt + t, 0))
    scratch = [pltpu.VMEM((8, d), F32)] * 3 + [pltpu.VMEM((8, RW_LORA), F32)]
    if has_vres:
        scratch.append(pltpu.VMEM((8, LANES), F32))
    return pl.pallas_call(
        functools.partial(_rwkv_prep_kernel, has_vres=has_vres),
        grid=(batch, nt),
        in_specs=in_specs,
        out_specs=[out_spec] * 8,
        out_shape=[jax.ShapeDtypeStruct((tokens, d), F32 if i == 1 else BF16) for i in range(8)],
        scratch_shapes=scratch,
        compiler_params=_cparams(("arbitrary", "arbitrary")),
        name="rwkv_prep",
    )(*args)


def _rwkv_scan_kernel(r_ref, lw_ref, k_ref, v_ref, kap_ref, beta_ref, tril_ref, bmask_ref,
                      o_ref, ht_ref):
    @pl.when(pl.program_id(1) == 0)
    def _():
        ht_ref[...] = jnp.zeros_like(ht_ref)

    n_batch, c, d = lw_ref.shape
    w = ht_ref.shape[1]
    g = w // RW_HEAD_DIM
    bmask = bmask_ref[...]
    bmask_b = bmask.astype(BF16)
    tril3 = tril_ref[...]
    t_idx = lax.broadcasted_iota(jnp.int32, (c, w), 0)
    s_idx = lax.broadcasted_iota(jnp.int32, (c, w), 1) % c
    strict = t_idx > s_idx
    incl = t_idx >= s_idx
    n_sq = int(math.log2(c))

    def stack(x):
        return jnp.concatenate([x.astype(BF16)] * g, axis=0) * bmask_b

    sls = [(bi, slice(None), slice(lo, lo + w)) for bi in range(n_batch) for lo in range(0, d, w)]
    groups = range(len(sls))
    lw = [lw_ref[sl] for sl in sls]
    cum = [_dot(tril3, jnp.concatenate(_split3(x), axis=0)) for x in lw]
    total = [x[c - 1:c, :] for x in cum]
    ar = [jnp.concatenate([-kap_ref[sls[gi]] * jnp.exp(cum[gi] - lw[gi]), r_ref[sls[gi]] * jnp.exp(cum[gi])],
                          axis=0).astype(BF16) for gi in groups]
    p_inv = [jnp.exp(-x) for x in cum]
    b_s = [stack(beta_ref[sls[gi]] * p_inv[gi]) for gi in groups]
    k_s = [stack(k_ref[sls[gi]] * p_inv[gi]) for gi in groups]
    v_n = [v_ref[sl] for sl in sls]
    v_s = [stack(x) for x in v_n]

    arb = [_dot_t(ar[gi], b_s[gi]) for gi in groups]
    ark = [_dot_t(ar[gi], k_s[gi]) for gi in groups]
    ab = [jnp.where(strict, m[:c], 0.0) for m in arb]
    rb = [jnp.where(incl, m[c:], 0.0).astype(BF16) for m in arb]
    akrk = [jnp.concatenate([jnp.where(strict, m[:c], 0.0), jnp.where(incl, m[c:], 0.0)], axis=0).astype(BF16)
            for m in ark]

    ht = [ht_ref[gi] for gi in groups]
    base = [_dot_t(ar[gi], ht[gi].astype(BF16)) + _dot(akrk[gi], v_s[gi]) for gi in groups]
    x = [m[:c] for m in base]
    lp = ab
    for i in range(n_sq):
        lpb = [m.astype(BF16) for m in lp]
        x = [x[gi] + _dot(lpb[gi], stack(x[gi])) for gi in groups]
        if i < n_sq - 1:
            lp = [_dot(lpb[gi], stack(lp[gi])) for gi in groups]
    for gi in groups:
        o_ref[sls[gi]] = base[gi][c:] + _dot(rb[gi], stack(x[gi]))

    for gi in groups:
        p_rem = jnp.exp(total[gi] - cum[gi])
        z = jnp.concatenate([beta_ref[sls[gi]] * p_rem, k_ref[sls[gi]] * p_rem], axis=0).astype(BF16)
        uv_t = jnp.concatenate([x[gi], v_n[gi].astype(F32)], axis=0).T.astype(BF16)
        ht_ref[gi] = ht[gi] * jnp.exp(total[gi]) + bmask * _dot(uv_t, z)


def rwkv_scan(r, lw, k, v, kap, beta, batch):
    tokens, d = r.shape
    seq = tokens // batch
    c = min(SCAN_CHUNK, seq)
    nc = seq // c
    gw = SCAN_GROUP * RW_HEAD_DIM
    rr = SCAN_GROUP * c
    assert c == RW_HEAD_DIM, "the stacking mask doubles as the head-block mask of the state"
    bb = math.gcd(batch, SCAN_BATCHES)
    tril = jnp.tile((jnp.arange(c)[:, None] >= jnp.arange(c)[None, :]).astype(BF16), (1, 3))
    bmask = (jnp.arange(rr)[:, None] // c == jnp.arange(gw)[None, :] // RW_HEAD_DIM).astype(F32)
    blk = pl.BlockSpec((bb, c, d), lambda b, i: (b, i, 0))
    as3d = lambda a: a.reshape(batch, seq, d)
    out = pl.pallas_call(
        _rwkv_scan_kernel,
        grid=(batch // bb, nc),
        in_specs=[blk] * 6 + [pl.BlockSpec((c, 3 * c), lambda b, i: (0, 0)),
                              pl.BlockSpec((rr, gw), lambda b, i: (0, 0))],
        out_specs=blk,
        out_shape=jax.ShapeDtypeStruct((batch, seq, d), F32),
        scratch_shapes=[pltpu.VMEM((bb * (d // gw), gw, gw), F32)],
        compiler_params=_cparams(("arbitrary", "arbitrary")),
        name="rwkv_scan",
    )(as3d(r), as3d(lw), as3d(k), as3d(v), as3d(kap), as3d(beta), tril, bmask)
    return out.reshape(tokens, d)


def _rwkv_post_kernel(o_ref, bonus_ref, g_ref, lnw_ref, lnb_ref, seg, seg_t, y_ref):
    segm, segm_t = seg[...], seg_t[...]

    def head_mean(x):
        return _dot_rhs01(_dot_rhs01(x, segm), segm_t) * (1.0 / RW_HEAD_DIM)

    o = o_ref[...]
    dlt = o - head_mean(o)
    var = head_mean(dlt * dlt)
    y = dlt * lax.rsqrt(var + RW_LN_EPS) * lnw_ref[...] + lnb_ref[...]
    y_ref[...] = ((y + bonus_ref[...]) * g_ref[...]).astype(y_ref.dtype)


def rwkv_post(o, bonus, g, ln_w, ln_b, seg, seg_t, *, tm=512):
    tokens, d = o.shape
    tm = min(tm, tokens)
    blk = pl.BlockSpec((tm, d), lambda i: (i, 0))
    vec = pl.BlockSpec((1, d), lambda i: (0, 0))
    return pl.pallas_call(
        _rwkv_post_kernel,
        grid=(tokens // tm,),
        in_specs=[blk, blk, blk, vec, vec,
                  pl.BlockSpec(seg.shape, lambda i: (0, 0)), pl.BlockSpec(seg_t.shape, lambda i: (0, 0))],
        out_specs=blk,
        out_shape=jax.ShapeDtypeStruct((tokens, d), BF16),
        compiler_params=_cparams(("parallel",)),
        name="rwkv_post",
    )(o, bonus, g, ln_w.reshape(1, d), ln_b.reshape(1, d), seg, seg_t)


def _t5_thresholds():
    max_exact = REL_BUCKETS // 2
    thr = list(range(1, max_exact))
    n = max_exact
    for bucket in range(max_exact, REL_BUCKETS):
        while True:
            large = max_exact + int(math.log(max(n, max_exact) / max_exact)
                                    / math.log(REL_MAX_DIST / max_exact) * (REL_BUCKETS - max_exact))
            if min(large, REL_BUCKETS - 1) >= bucket:
                break
            n += 1
        thr.append(n)
    return thr


T5_THRESHOLDS = _t5_thresholds()
T5_FAR = T5_THRESHOLDS[-1]


def _softmax_tiles(s_list, c_list, states, vt_list):
    stats = []
    for s_t, c, (m_old, l_old, _) in zip(s_list, c_list, states):
        m_new = jnp.maximum(m_old, jnp.max(s_t, axis=0, keepdims=True) + c)
        alpha = jnp.exp2(m_old - m_new)
        p_t = jnp.exp2(s_t - (m_new - c))
        stats.append((m_new, alpha, alpha * l_old + jnp.sum(p_t, axis=0, keepdims=True), p_t.astype(BF16)))
    return tuple((m_new, l_new, alpha * acc + _dot(vt, p_t))
                 for (m_new, alpha, l_new, p_t), (_, _, acc), vt in zip(stats, states, vt_list))


def _transpose_into(vt_ref, v_ref, chunk):
    seq = v_ref.shape[0]
    for c in range(seq // chunk):
        vt_ref[:, c * chunk:(c + 1) * chunk] = v_ref[c * chunk:(c + 1) * chunk, :].astype(F32).T.astype(BF16)


def _diff_attn_kernel(qfirst_ref, klast_ref, q_ref, k_ref, v_ref, qpos_ref, kpos_ref, subln_ref, table_ref, lam_ref,
                      o_ref, vt_ref, *, tq, tk, scale2, out_scale):
    b, i = pl.program_id(0), pl.program_id(1)
    nq = pl.num_programs(1)
    seq = k_ref.shape[0]
    nk = seq // tk
    w = DF_V_DIM

    @pl.when(i == 0)
    def _():
        _transpose_into(vt_ref, v_ref, tk)

    n_tiles = (i * tq + tq - 1) // tk + 1
    qf = qfirst_ref[b * nq + i]
    n_far = lax.while_loop(
        lambda j: (j * tk + tk - 1 <= i * tq) & (qf - klast_ref[b * nk + jnp.minimum(j, nk - 1)] >= T5_FAR),
        lambda j: j + 1, jnp.int32(0))

    dist = lax.broadcasted_iota(jnp.int32, (1, LANES), 1)
    qpos = qpos_ref[...]
    q_idx = i * tq + lax.broadcasted_iota(jnp.int32, (tk, tq), 1)
    k_off = lax.broadcasted_iota(jnp.int32, (tk, tq), 0)
    lane = lax.broadcasted_iota(jnp.int32, (tq, w), 1)

    bias_rows, c_far, qm = [], [], []
    for h in range(DF_HEADS):
        bias_vec = jnp.full((1, LANES), table_ref[h], F32)
        for bucket, thr in enumerate(T5_THRESHOLDS, start=1):
            bias_vec = jnp.where(dist >= thr, table_ref[bucket * DF_HEADS + h], bias_vec)
        bias_rows.append(jnp.broadcast_to(bias_vec * LOG2E, (tk, LANES)))
        c_far.append(table_ref[(REL_BUCKETS - 1) * DF_HEADS + h] * LOG2E)
        qh = q_ref[:, h * w:(h + 1) * w].astype(F32) * scale2
        qm.append([jnp.where((lane >= mi * DF_HEAD_DIM) & (lane < (mi + 1) * DF_HEAD_DIM), qh, 0.0).astype(BF16)
                   for mi in range(2)])

    def tiles(j, h):
        off = pl.multiple_of(j * tk, tk)
        return k_ref[pl.ds(off, tk), h * w:(h + 1) * w], vt_ref[h * w:(h + 1) * w, pl.ds(off, tk)], off

    chains = [(h, mi) for h in range(DF_HEADS) for mi in range(2)]

    def far_body(j, st):
        kv = [tiles(j, h) for h in range(DF_HEADS)]
        s = [_dot_t(kv[h][0], qm[h][mi]) for h, mi in chains]
        return _softmax_tiles(s, [c_far[h] for h, _ in chains], st, [kv[h][1] for h, _ in chains])

    def near_body(j, st):
        off = pl.multiple_of(j * tk, tk)
        n = jnp.clip(qpos - kpos_ref[pl.ds(off, tk), :], 0, LANES - 1)
        keep = q_idx >= off + k_off
        kv = [tiles(j, h) for h in range(DF_HEADS)]
        bias = [jnp.concatenate(
            [jnp.take_along_axis(bias_rows[h], n[:, cb * LANES:(cb + 1) * LANES], axis=1)
             for cb in range(tq // LANES)], axis=1) for h in range(DF_HEADS)]
        s = [jnp.where(keep, _dot_t(kv[h][0], qm[h][mi]) + bias[h], NEG_BIG) for h, mi in chains]
        return _softmax_tiles(s, [0.0] * len(chains), st, [kv[h][1] for h, _ in chains])

    init = tuple((jnp.full((1, tq), NEG_BIG, F32), jnp.zeros((1, tq), F32), jnp.zeros((w, tq), F32))
                 for _ in range(2 * DF_HEADS))
    st = lax.fori_loop(0, n_far, far_body, init)
    st = lax.fori_loop(n_far, n_tiles, near_body, st)
    for h in range(DF_HEADS):
        s0, s1 = st[2 * h], st[2 * h + 1]
        d_t = s0[2] / s0[1] - lam_ref[0] * (s1[2] / s1[1])
        ms = jnp.mean(d_t * d_t, axis=0, keepdims=True)
        y_t = d_t * lax.rsqrt(ms + DF_SUBLN_EPS) * (subln_ref[...] * out_scale)
        o_ref[:, h * w:(h + 1) * w] = y_t.T.astype(o_ref.dtype)


def diff_attention(pa, positions, rel_bias, lam, lambda_init, subln_w, *, tq=ATTN_TILE):
    batch, seq = positions.shape
    tokens = batch * seq
    tq = min(tq, seq)
    tk = tq
    nq, nk = seq // tq, seq // tk
    qfirst = positions[:, ::tq].reshape(-1)
    klast = positions[:, tk - 1::tk].reshape(-1)
    qpos = positions.reshape(batch, 1, seq)
    kpos = positions.reshape(batch, seq, 1)
    wd = DF_DIM
    grid_spec = pltpu.PrefetchScalarGridSpec(
        num_scalar_prefetch=2,
        grid=(batch, nq),
        in_specs=[pl.BlockSpec((tq, wd), lambda b, i, *_: (b * nq + i, OFF_DQ // wd)),
                  pl.BlockSpec((seq, wd), lambda b, i, *_: (b, OFF_DK // wd)),
                  pl.BlockSpec((seq, wd), lambda b, i, *_: (b, OFF_DV // wd)),
                  pl.BlockSpec((None, 1, tq), lambda b, i, *_: (b, 0, i)),
                  pl.BlockSpec((None, seq, 1), lambda b, i, *_: (b, 0, 0)),
                  pl.BlockSpec((DF_V_DIM, 1), lambda b, i, *_: (0, 0)),
                  pl.BlockSpec(memory_space=pltpu.SMEM),
                  pl.BlockSpec(memory_space=pltpu.SMEM)],
        out_specs=pl.BlockSpec((tq, wd), lambda b, i, *_: (b * nq + i, 0)),
        scratch_shapes=[pltpu.VMEM((wd, seq), BF16)],
    )
    return pl.pallas_call(
        functools.partial(_diff_attn_kernel, tq=tq, tk=tk, scale2=DF_HEAD_DIM ** -0.5 * LOG2E,
                          out_scale=1.0 - lambda_init),
        grid_spec=grid_spec,
        out_shape=jax.ShapeDtypeStruct((tokens, DF_DIM), BF16),
        compiler_params=_cparams(("arbitrary", "arbitrary")),
        name="diff_attention",
    )(qfirst, klast, pa, pa, pa, qpos, kpos, subln_w.reshape(DF_V_DIM, 1), rel_bias.reshape(-1), lam.reshape(1))


ML_QK_PAD = 2 * LANES


def _rope_block(x, cos, sin):
    half = ML_ROPE // 2
    lane = lax.broadcasted_iota(jnp.int32, x.shape, 1)
    rot = jnp.where(lane < half, -pltpu.roll(x, LANES - half, axis=1),
                    jnp.where(lane < ML_ROPE, pltpu.roll(x, half, axis=1), 0.0))
    return x * cos + rot * sin


def _mla_prep_kernel(mq_ref, mkv_ref, kpe_ref, pos_ref, qn_w, kvn_w, wq_ref, wkv_ref, freq_ref,
                     qf_o, kf_o, v_o, *, qscale):
    ang = pos_ref[...].astype(F32) * freq_ref[...]
    cos, sin = jnp.cos(ang), jnp.sin(ang)
    qc = _rms(mq_ref[...].astype(F32), qn_w[...], NORM_EPS).astype(BF16)
    q_all = _dot(qc, wq_ref[...]) * qscale
    kvc = _rms(mkv_ref[...].astype(F32), kvn_w[...], NORM_EPS).astype(BF16)
    kvb = _dot(kvc, wkv_ref[...])
    kpe = _rope_block(kpe_ref[...], cos, sin).astype(BF16)
    nope_w = ML_HEADS * ML_NOPE
    for h in range(ML_HEADS):
        lo = h * ML_QK_PAD
        qf_o[:, lo:lo + LANES] = q_all[:, h * LANES:(h + 1) * LANES].astype(BF16)
        qf_o[:, lo + LANES:lo + 2 * LANES] = _rope_block(
            q_all[:, nope_w + h * LANES:nope_w + (h + 1) * LANES], cos, sin).astype(BF16)
        kf_o[:, lo:lo + LANES] = kvb[:, h * LANES:(h + 1) * LANES].astype(BF16)
        kf_o[:, lo + LANES:lo + 2 * LANES] = kpe
    v_o[...] = kvb[:, nope_w:].astype(BF16)


def mla_prep(pa, pf, positions, q_norm, kv_norm, wq_all, wkv, *, tm=512):
    tokens = pa.shape[0]
    tm = min(tm, tokens)
    half = ML_ROPE // 2
    inv_freq = ROPE_THETA ** (-jnp.arange(half, dtype=F32) / half)
    freq = jnp.concatenate([inv_freq, inv_freq, jnp.zeros((LANES - ML_ROPE,), F32)]).reshape(1, LANES)

    def full(a):
        return pl.BlockSpec(a.shape, lambda i: (0, 0))

    qn_w = q_norm.reshape(1, -1)
    kvn_w = kv_norm.reshape(1, -1)
    wide = ML_HEADS * ML_QK_PAD
    return pl.pallas_call(
        functools.partial(_mla_prep_kernel, qscale=(ML_NOPE + ML_ROPE) ** -0.5 * LOG2E),
        grid=(tokens // tm,),
        in_specs=[pl.BlockSpec((tm, ML_Q_RANK), lambda i: (i, OFF_MQ // ML_Q_RANK)),
                  pl.BlockSpec((tm, ML_KV_RANK), lambda i: (i, OFF_MKV // ML_KV_RANK)),
                  pl.BlockSpec((tm, LANES), lambda i: (i, OFF_KPE // LANES)),
                  pl.BlockSpec((tm, 1), lambda i: (i, 0)),
                  full(qn_w), full(kvn_w), full(wq_all), full(wkv), full(freq)],
        out_specs=[pl.BlockSpec((tm, wide), lambda i: (i, 0)),
                   pl.BlockSpec((tm, wide), lambda i: (i, 0)),
                   pl.BlockSpec((tm, ML_DIM), lambda i: (i, 0))],
        out_shape=[jax.ShapeDtypeStruct((tokens, wide), BF16),
                   jax.ShapeDtypeStruct((tokens, wide), BF16),
                   jax.ShapeDtypeStruct((tokens, ML_DIM), BF16)],
        compiler_params=_cparams(("parallel",)),
        name="mla_prep",
    )(pa, pa, pf, positions.reshape(tokens, 1), qn_w, kvn_w, wq_all, wkv, freq)


def _mla_attn_kernel(q_ref, k_ref, v_ref, o_ref, vt_ref, *, tq, tk):
    i = pl.program_id(1)
    wq = ML_QK_PAD

    @pl.when(i == 0)
    def _():
        _transpose_into(vt_ref, v_ref, tk)

    n_tiles = (i * tq + tq - 1) // tk + 1
    n_full = (i * tq + 1) // tk
    q_idx = i * tq + lax.broadcasted_iota(jnp.int32, (tk, tq), 1)
    k_off = lax.broadcasted_iota(jnp.int32, (tk, tq), 0)
    qh = [q_ref[:, h * wq:(h + 1) * wq] for h in range(ML_HEADS)]

    def tiles(j, h):
        off = pl.multiple_of(j * tk, tk)
        return (k_ref[pl.ds(off, tk), h * wq:(h + 1) * wq],
                vt_ref[h * ML_V:(h + 1) * ML_V, pl.ds(off, tk)], off)

    heads = range(ML_HEADS)

    def full_body(j, st):
        kv = [tiles(j, h) for h in heads]
        s = [_dot_t(kv[h][0], qh[h]) for h in heads]
        return _softmax_tiles(s, [0.0] * ML_HEADS, st, [kv[h][1] for h in heads])

    def diag_body(j, st):
        kv = [tiles(j, h) for h in heads]
        keep = q_idx >= kv[0][2] + k_off
        s = [jnp.where(keep, _dot_t(kv[h][0], qh[h]), NEG_BIG) for h in heads]
        return _softmax_tiles(s, [0.0] * ML_HEADS, st, [kv[h][1] for h in heads])

    st = tuple((jnp.full((1, tq), NEG_BIG, F32), jnp.zeros((1, tq), F32), jnp.zeros((ML_V, tq), F32))
               for _ in range(ML_HEADS))
    st = lax.fori_loop(0, n_full, full_body, st)
    st = lax.fori_loop(n_full, n_tiles, diag_body, st)
    for h in range(ML_HEADS):
        o_ref[:, h * ML_V:(h + 1) * ML_V] = (st[h][2] / st[h][1]).T.astype(o_ref.dtype)


def mla_attention(qf, kf, v, batch, *, tq=ATTN_TILE):
    tokens = qf.shape[0]
    seq = tokens // batch
    tq = min(tq, seq)
    tk = tq
    nq = seq // tq
    wide = qf.shape[1]
    return pl.pallas_call(
        functools.partial(_mla_attn_kernel, tq=tq, tk=tk),
        grid=(batch, nq),
        in_specs=[pl.BlockSpec((tq, wide), lambda b, i: (b * nq + i, 0)),
                  pl.BlockSpec((seq, wide), lambda b, i: (b, 0)),
                  pl.BlockSpec((seq, ML_DIM), lambda b, i: (b, 0))],
        out_specs=pl.BlockSpec((tq, ML_DIM), lambda b, i: (b * nq + i, 0)),
        out_shape=jax.ShapeDtypeStruct((tokens, ML_DIM), BF16),
        scratch_shapes=[pltpu.VMEM((ML_DIM, seq), BF16)],
        compiler_params=_cparams(("arbitrary", "arbitrary")),
        name="mla_attention",
    )(qf, kf, v)


def _cross_kernel(x_ref, nw_ref, wq_ref, kv_ref, wo_ref, o_ref):
    x = x_ref[...]
    q = _dot(_rms(x, nw_ref[...], NORM_EPS).astype(BF16), wq_ref[...])
    kv = kv_ref[...]
    scale = CA_HEAD_DIM ** -0.5
    outs = []
    for hh in range(CA_HEADS):
        sl = slice(hh * CA_HEAD_DIM, (hh + 1) * CA_HEAD_DIM)
        s = _dot_t(q[:, sl].astype(BF16), kv[:, sl]) * scale
        p = jnp.exp(s - jnp.max(s, axis=-1, keepdims=True))
        p = p / jnp.sum(p, axis=-1, keepdims=True)
        outs.append(_dot(p.astype(BF16), kv[:, CA_DIM + hh * CA_HEAD_DIM:CA_DIM + (hh + 1) * CA_HEAD_DIM]))
    o = jnp.concatenate(outs, axis=1).astype(BF16)
    o_ref[...] = x + _dot(o, wo_ref[...])


def cross_block(x, batch, norm_w, wq, kv, wo, *, tq=512):
    tokens, d = x.shape
    seq = tokens // batch
    tq = min(tq, seq)
    nq = seq // tq
    mem_len = kv.shape[0] // batch
    return pl.pallas_call(
        _cross_kernel,
        grid=(batch, nq),
        in_specs=[pl.BlockSpec((tq, d), lambda b, i: (b * nq + i, 0)),
                  pl.BlockSpec((1, d), lambda b, i: (0, 0)),
                  pl.BlockSpec(wq.shape, lambda b, i: (0, 0)),
                  pl.BlockSpec((mem_len, 2 * CA_DIM), lambda b, i: (b, 0)),
                  pl.BlockSpec(wo.shape, lambda b, i: (0, 0))],
        out_specs=pl.BlockSpec((tq, d), lambda b, i: (b * nq + i, 0)),
        out_shape=jax.ShapeDtypeStruct((tokens, d), F32),
        compiler_params=_cparams(("parallel", "parallel")),
        name="cross_block",
    )(x, norm_w.reshape(1, d), wq, kv, wo)


SEL_E1, SEL_E2, SEL_G1, SEL_G2 = 0, 1, 2, 3


def _route(h, w_router, b_router):
    logits = _dot_x3(h, w_router)
    biased = logits + b_router
    lane = lax.broadcasted_iota(jnp.int32, logits.shape, 1)
    big = jnp.int32(LANES)

    def first_argmax(vals):
        mx = jnp.max(vals, axis=-1, keepdims=True)
        return jnp.min(jnp.where(vals == mx, lane, big), axis=-1, keepdims=True)

    def pick(vals, idx):
        return jnp.sum(jnp.where(lane == idx, vals, 0.0), axis=-1, keepdims=True)

    is_group = (lane >= MOE_EXPERTS) & (lane < MOE_EXPERTS + MOE_GROUPS)
    gl = jnp.where(is_group, logits, NEG_BIG)
    ge = jnp.exp(gl - jnp.max(gl, axis=-1, keepdims=True))
    gp = ge / jnp.sum(ge, axis=-1, keepdims=True)
    g_lane = first_argmax(jnp.where(is_group, biased, NEG_BIG))
    p_group = pick(gp, g_lane)
    lo = (g_lane - MOE_EXPERTS) * MOE_PER_GROUP
    in_group = (lane >= lo) & (lane < lo + MOE_PER_GROUP)
    eb = jnp.where(in_group, biased, NEG_BIG)
    i1 = first_argmax(eb)
    i2 = first_argmax(jnp.where(lane == i1, NEG_BIG, eb))
    l1, l2 = pick(logits, i1), pick(logits, i2)
    mx = jnp.maximum(l1, l2)
    e1, e2 = jnp.exp(l1 - mx), jnp.exp(l2 - mx)
    w1, w2 = e1 / (e1 + e2), e2 / (e1 + e2)
    return jnp.where(lane == SEL_E1, i1.astype(F32),
                     jnp.where(lane == SEL_E2, i2.astype(F32),
                               jnp.where(lane == SEL_G1, w1 * p_group,
                                         jnp.where(lane == SEL_G2, w2 * p_group, 0.0))))


def _router_kernel(x_ref, nw_ref, wr_ref, br_ref, h_ref, sel_ref):
    h = _rms(x_ref[...], nw_ref[...], NORM_EPS)
    h_ref[...] = h
    sel_ref[...] = _route(h, wr_ref[...], br_ref[...])


def moe_router(x, norm_w, w_router, b_router, *, tm=512):
    tokens, d = x.shape
    tm = min(tm, tokens)
    return pl.pallas_call(
        _router_kernel,
        grid=(tokens // tm,),
        in_specs=[pl.BlockSpec((tm, d), lambda i: (i, 0)),
                  pl.BlockSpec((1, d), lambda i: (0, 0)),
                  pl.BlockSpec((d, LANES), lambda i: (0, 0)),
                  pl.BlockSpec((1, LANES), lambda i: (0, 0))],
        out_specs=[pl.BlockSpec((tm, d), lambda i: (i, 0)),
                   pl.BlockSpec((tm, LANES), lambda i: (i, 0))],
        out_shape=[jax.ShapeDtypeStruct((tokens, d), F32),
                   jax.ShapeDtypeStruct((tokens, LANES), F32)],
        compiler_params=_cparams(("parallel",)),
        name="moe_router",
    )(x, norm_w.reshape(1, d), w_router, b_router)


def _moe_rank_kernel(sel_ref, ltri_ref, rank_ref, counts_ref, carry_ref):
    @pl.when(pl.program_id(0) == 0)
    def _():
        carry_ref[...] = jnp.zeros_like(carry_ref)

    sel = sel_ref[...]
    lane = lax.broadcasted_iota(jnp.int32, sel.shape, 1)
    lane_f = lane.astype(F32)
    oh1 = lane_f == sel[:, SEL_E1:SEL_E1 + 1]
    oh2 = lane_f == sel[:, SEL_E2:SEL_E2 + 1]
    f1, f2 = oh1.astype(F32), oh2.astype(F32)
    ltri = ltri_ref[...]
    before1 = _dot(ltri, f1.astype(BF16))
    before2 = _dot(ltri, f2.astype(BF16))
    c1 = jnp.sum(f1, axis=0, keepdims=True)
    c2 = jnp.sum(f2, axis=0, keepdims=True)
    carry = carry_ref[...]
    r1 = jnp.sum(jnp.where(oh1, before1 + carry, 0.0), axis=1, keepdims=True)
    r2 = jnp.sum(jnp.where(oh2, before2 + carry + c1, 0.0), axis=1, keepdims=True)
    rank_ref[...] = jnp.where(lane == SEL_E1, r1, jnp.where(lane == SEL_E2, r2, 0.0)).astype(jnp.int32)
    total = carry + c1 + c2
    carry_ref[...] = total
    counts_ref[...] = total.astype(jnp.int32)


def moe_rank(sel, *, tm=512):
    tokens = sel.shape[0]
    tm = min(tm, tokens)
    ltri = (jnp.arange(tm)[:, None] > jnp.arange(tm)[None, :]).astype(BF16)
    return pl.pallas_call(
        _moe_rank_kernel,
        grid=(tokens // tm,),
        in_specs=[pl.BlockSpec((tm, LANES), lambda i: (i, 0)),
                  pl.BlockSpec((tm, tm), lambda i: (0, 0))],
        out_specs=[pl.BlockSpec((tm, LANES), lambda i: (i, 0)),
                   pl.BlockSpec((1, LANES), lambda i: (0, 0))],
        out_shape=[jax.ShapeDtypeStruct((tokens, LANES), jnp.int32),
                   jax.ShapeDtypeStruct((1, LANES), jnp.int32)],
        scratch_shapes=[pltpu.VMEM((1, LANES), F32)],
        compiler_params=_cparams(("arbitrary",)),
        name="moe_rank",
    )(sel, ltri)


def _row_copy(src_ref, src_row, dst_ref, dst_row, sem):
    return pltpu.make_async_copy(src_ref.at[pl.ds(src_row, 1)], dst_ref.at[pl.ds(dst_row, 1)], sem)


def _moe_dispatch_kernel(dest_ref, tail_ref, h_ref, xs_ref, hbuf, zero_ref, lsem, ssem, zsem, *, tm):
    step = pl.program_id(0)
    n_steps = pl.num_programs(0)
    base = step * (2 * tm)
    slot = step % 3

    def load(tile, buf_slot):
        return pltpu.make_async_copy(h_ref.at[pl.ds(pl.multiple_of(tile * tm, tm), tm)], hbuf.at[buf_slot],
                                     lsem.at[buf_slot])

    def wait_scatter(buf_slot):
        for _ in range(2):
            pltpu.make_async_copy(hbuf.at[buf_slot], xs_ref.at[pl.ds(0, tm)], ssem.at[buf_slot]).wait()

    @pl.when(step == 0)
    def _():
        load(0, 0).start()

        @pl.when(n_steps > 1)
        def _():
            load(1, 1).start()

    @pl.when(step == 0)
    def _():
        zero_ref[...] = jnp.zeros_like(zero_ref)

        def fill(tail):
            return pltpu.make_async_copy(zero_ref, xs_ref.at[pl.ds(pl.multiple_of(tail, 8), zero_ref.shape[0])], zsem)

        def start(e, carry):
            @pl.when(tail_ref[e] >= 0)
            def _():
                fill(tail_ref[e]).start()
            return carry

        def wait(e, carry):
            @pl.when(tail_ref[e] >= 0)
            def _():
                fill(tail_ref[e]).wait()
            return carry

        lax.fori_loop(0, tail_ref.shape[0], start, 0)
        lax.fori_loop(0, tail_ref.shape[0], wait, 0)

    load(step, slot).wait()

    def issue(r, carry):
        for s in range(2):
            _row_copy(hbuf.at[slot], r, xs_ref, dest_ref[base + 2 * r + s], ssem.at[slot]).start()
        return carry

    lax.fori_loop(0, tm, issue, 0, unroll=DMA_ISSUE_UNROLL)

    @pl.when(step >= 1)
    def _():
        wait_scatter((step + 2) % 3)

    @pl.when(step + 2 < n_steps)
    def _():
        load(step + 2, (step + 2) % 3).start()

    @pl.when(step == n_steps - 1)
    def _():
        wait_scatter(slot)


def moe_dispatch(h, dest, tails, rows, row_tile, *, tm=256):
    tokens, d = h.shape
    tm = min(tm, tokens)
    grid_spec = pltpu.PrefetchScalarGridSpec(
        num_scalar_prefetch=2,
        grid=(tokens // tm,),
        in_specs=[pl.BlockSpec(memory_space=pl.ANY)],
        out_specs=pl.BlockSpec(memory_space=pl.ANY),
        scratch_shapes=[pltpu.VMEM((3, tm, d), h.dtype), pltpu.VMEM((row_tile, d), h.dtype),
                        pltpu.SemaphoreType.DMA((3,)), pltpu.SemaphoreType.DMA((3,)),
                        pltpu.SemaphoreType.DMA(())],
    )
    return pl.pallas_call(
        functools.partial(_moe_dispatch_kernel, tm=tm),
        grid_spec=grid_spec,
        out_shape=jax.ShapeDtypeStruct((rows, d), h.dtype),
        compiler_params=_cparams(("arbitrary",)),
        name="moe_dispatch",
    )(dest, tails, h)


def _moe_expert_kernel(te_ref, nused_ref, xs_ref, wg_ref, wu_ref, wd_ref, ys_ref, wgb, wub, wdb):
    r = pl.program_id(0)
    used = r < nused_ref[0]
    changed = (r == 0) | (te_ref[r] != te_ref[jnp.maximum(r - 1, 0)])

    @pl.when(used & changed)
    def _():
        wgb[...] = wg_ref[...].astype(BF16)
        wub[...] = wu_ref[...].astype(BF16)
        wdb[...] = wd_ref[...].astype(BF16)

    @pl.when(used)
    def _():
        x = xs_ref[...].astype(BF16)
        gate_pre = _dot(x, wgb[...])
        hid = (gate_pre * jax.nn.sigmoid(gate_pre)) * _dot(x, wub[...])
        ys_ref[...] = _dot(hid.astype(BF16), wdb[...])

    @pl.when(jnp.logical_not(used))
    def _():
        ys_ref[...] = jnp.zeros_like(ys_ref)


def moe_experts(xs, tile_expert, n_used, wg, wu, wd, layer, *, tm):
    rows, d = xs.shape
    de = wg.shape[-1]
    grid_spec = pltpu.PrefetchScalarGridSpec(
        num_scalar_prefetch=2,
        grid=(rows // tm,),
        in_specs=[pl.BlockSpec((tm, d), lambda r, te, nu: (jnp.minimum(r, nu[0] - 1), 0)),
                  pl.BlockSpec((None, None, d, de), lambda r, te, nu: (layer, te[r], 0, 0)),
                  pl.BlockSpec((None, None, d, de), lambda r, te, nu: (layer, te[r], 0, 0)),
                  pl.BlockSpec((None, None, de, d), lambda r, te, nu: (layer, te[r], 0, 0))],
        out_specs=pl.BlockSpec((tm, d), lambda r, te, nu: (r, 0)),
        scratch_shapes=[pltpu.VMEM((d, de), BF16), pltpu.VMEM((d, de), BF16), pltpu.VMEM((de, d), BF16)],
    )
    return pl.pallas_call(
        _moe_expert_kernel,
        grid_spec=grid_spec,
        out_shape=jax.ShapeDtypeStruct((rows, d), F32),
        compiler_params=_cparams(("arbitrary",)),
        name="moe_experts",
    )(tile_expert, n_used, xs, wg, wu, wd)


def _moe_combine_kernel(dest_ref, x_ref, sel_ref, nw_ref, ys_ref, o_ref, buf_ref, sem, *, tm, normalize):
    step = pl.program_id(0)
    slot = step % 2

    def gather(tile, buf_slot):
        base = tile * (2 * tm)

        def issue(r, carry):
            for s in range(2):
                _row_copy(ys_ref, dest_ref[base + 2 * r + s], buf_ref.at[buf_slot, s], r, sem.at[buf_slot]).start()
            return carry

        lax.fori_loop(0, tm, issue, 0, unroll=DMA_ISSUE_UNROLL)

    @pl.when(step == 0)
    def _():
        gather(0, 0)

    @pl.when(step + 1 < pl.num_programs(0))
    def _():
        gather(step + 1, 1 - slot)

    for s in range(2):
        pltpu.make_async_copy(ys_ref.at[pl.ds(0, tm)], buf_ref.at[slot, s], sem.at[slot]).wait()
    sel = sel_ref[...]
    out = x_ref[...] + sel[:, SEL_G1:SEL_G1 + 1] * buf_ref[slot, 0] + sel[:, SEL_G2:SEL_G2 + 1] * buf_ref[slot, 1]
    o_ref[...] = _rms(out, nw_ref[...], NORM_EPS) if normalize else out


def moe_combine(x, sel, ys, dest, final_norm=None, *, tm=256):
    tokens, d = x.shape
    tm = min(tm, tokens)
    normalize = final_norm is not None
    nw = (final_norm if normalize else jnp.ones((d,), F32)).reshape(1, d)
    grid_spec = pltpu.PrefetchScalarGridSpec(
        num_scalar_prefetch=1,
        grid=(tokens // tm,),
        in_specs=[pl.BlockSpec((tm, d), lambda i, *_: (i, 0)),
                  pl.BlockSpec((tm, LANES), lambda i, *_: (i, 0)),
                  pl.BlockSpec((1, d), lambda i, *_: (0, 0)),
                  pl.BlockSpec(memory_space=pl.ANY)],
        out_specs=pl.BlockSpec((tm, d), lambda i, *_: (i, 0)),
        scratch_shapes=[pltpu.VMEM((2, 2, tm, d), F32), pltpu.SemaphoreType.DMA((2,))],
    )
    return pl.pallas_call(
        functools.partial(_moe_combine_kernel, tm=tm, normalize=normalize),
        grid_spec=grid_spec,
        out_shape=jax.ShapeDtypeStruct((tokens, d), F32),
        compiler_params=_cparams(("arbitrary",)),
        name="moe_combine",
    )(dest, x, sel, nw, ys)


def moe_block(x, norm_w, w_router, b_router, wg, wu, wd, layer, final_norm=None, *, tm=MOE_ROW_TILE):
    tokens, d = x.shape
    n_exp = wg.shape[1]
    h, sel = moe_router(x, norm_w, w_router, b_router)
    rank, counts = moe_rank(sel)
    padded = (counts[0, :n_exp] + (tm - 1)) // tm * tm
    ends = jnp.cumsum(padded)
    starts = ends - padded
    experts = sel[:, SEL_E1:SEL_E2 + 1].astype(jnp.int32)
    start_of = jnp.sum(jnp.where(experts[..., None] == jnp.arange(n_exp, dtype=jnp.int32), starts, 0), axis=-1)
    dest = (start_of + rank[:, SEL_E1:SEL_E2 + 1]).reshape(-1)
    rows = 2 * tokens + n_exp * tm
    tile_start = jnp.arange(rows // tm, dtype=jnp.int32) * tm
    tile_expert = jnp.minimum(jnp.sum(tile_start[:, None] >= ends[None, :], axis=1), n_exp - 1).astype(jnp.int32)
    n_used = (ends[-1] // tm).astype(jnp.int32).reshape(1)
    tails = jnp.concatenate([jnp.where(padded > 0, ends - tm, -1),
                             jnp.where(tile_start >= ends[-1], tile_start, -1)]).astype(jnp.int32)
    xs = moe_dispatch(h, dest, tails, rows, tm)
    ys = moe_experts(xs, tile_expert, n_used, wg, wu, wd, layer, tm=tm)
    return moe_combine(x, sel, ys, dest, final_norm)


def _proj_weights(w_in_l, w_vres_l):
    d = w_in_l.shape[0]
    mla0 = RW_COLS + DF_COLS
    vres = jnp.zeros((d, RW_V_RANK), F32) if w_vres_l is None else w_vres_l
    part_f = [w_in_l[:, 3 * RW_DIM:RW_COLS],
              vres, jnp.zeros((d, LANES - RW_V_RANK), F32),
              w_in_l[:, mla0 + ML_Q_RANK + ML_KV_RANK:mla0 + ML_COLS], jnp.zeros((d, LANES - ML_ROPE), F32)]
    part_a = [w_in_l[:, :3 * RW_DIM],
              w_in_l[:, RW_COLS:RW_COLS + DF_COLS],
              w_in_l[:, mla0 + ML_Q_RANK:mla0 + ML_Q_RANK + ML_KV_RANK],
              jnp.zeros((d, OFF_MQ - OFF_MKV - ML_KV_RANK), F32),
              w_in_l[:, mla0:mla0 + ML_Q_RANK]]
    return jnp.concatenate(part_f, axis=1).astype(BF16), jnp.concatenate(part_a, axis=1).astype(BF16)


def _pad_rows(w, rows, at=0):
    out = jnp.zeros((rows, w.shape[1]), w.dtype)
    return lax.dynamic_update_slice(out, w, (at, 0))


def kernel(x, mem, positions, rel_bias, final_norm, norm_mix, w_in, w_in_vres, w_out, tm_mu, tm_mu_vres, tm_w0, tm_w2, tm_a0, tm_a2, tm_v0, tm_v2, tm_g2, tm_k_k, tm_k_a, tm_r_k, tm_ln_w, tm_ln_b, da_lq1, da_lk1, da_lq2, da_lk2, da_subln, mla_q_norm, mla_wq_b, mla_kv_norm, mla_wkv_b, norm_cross, norm_mem, ca_wq, ca_wkv, ca_wo, norm_ffn, moe_w_group, moe_b_group, moe_w_expert, moe_b_expert, moe_w_gate, moe_w_up, moe_w_down):
    batch, seq, d = x.shape
    tokens = batch * seq
    depth = norm_mix.shape[0]
    xf = x.reshape(tokens, d)
    memf = mem.reshape(-1, d)
    positions = positions.astype(jnp.int32)

    head_of_lane = jnp.arange(RW_DIM) // RW_HEAD_DIM
    seg = (head_of_lane[:, None] == jnp.arange(LANES)[None, :]).astype(BF16)
    seg_t = seg.T
    row = lambda v: v.reshape(1, -1)

    v_first = None
    for l in range(depth):
        w_f, w_a = _proj_weights(w_in[l], None if l == 0 else w_in_vres[l - 1])
        pa, proj = norm_matmul(xf, norm_mix[l], w_a, w_f, out_dtype=BF16, tm=1024, tn=PROJ_A_COLS // 3)

        mu = tm_mu[l]
        prm = dict(mu_r=row(mu[:RW_DIM]), mu_k=row(mu[RW_DIM:2 * RW_DIM]), mu_v=row(mu[2 * RW_DIM:3 * RW_DIM]),
                   mu_l=row(mu[3 * RW_DIM:]), w0=row(tm_w0[l]), a0=row(tm_a0[l]),
                   w2=_pad_rows(tm_w2[l], LANES, 0), a2=_pad_rows(tm_a2[l], LANES, RW_W_RANK),
                   g2=tm_g2[l].astype(BF16), k_k=row(tm_k_k[l]), k_a=row(tm_k_a[l]), r_k=row(tm_r_k[l]),
                   seg=seg, seg_t=seg_t)
        if l > 0:
            prm.update(mu_vr=jnp.pad(row(tm_mu_vres[l - 1]), ((0, 0), (0, LANES - RW_V_RANK))),
                       v0=row(tm_v0[l - 1]), v2=_pad_rows(tm_v2[l - 1], LANES, 0))
        r, lw, k, v, kap, beta, gate, bonus = rwkv_prep(pa, proj, batch, v_first, prm)
        if l == 0:
            v_first = v
        o = rwkv_scan(r, lw, k, v, kap, beta, batch)
        y_a = rwkv_post(o, bonus, gate, tm_ln_w[l], tm_ln_b[l], seg, seg_t)

        lambda_init = 0.8 - 0.6 * math.exp(-0.3 * l)
        lam = (jnp.exp(jnp.sum(da_lq1[l] * da_lk1[l])) - jnp.exp(jnp.sum(da_lq2[l] * da_lk2[l])) + lambda_init)
        y_b = diff_attention(pa, positions, rel_bias, lam, lambda_init, da_subln[l])

        wq = mla_wq_b[l].reshape(ML_Q_RANK, ML_HEADS, ML_NOPE + ML_ROPE)
        wq_pe = jnp.pad(wq[:, :, ML_NOPE:], ((0, 0), (0, 0), (0, LANES - ML_ROPE)))
        wq_all = jnp.concatenate([wq[:, :, :ML_NOPE].reshape(ML_Q_RANK, -1),
                                  wq_pe.reshape(ML_Q_RANK, -1)], axis=1).astype(BF16)
        wkv = mla_wkv_b[l].reshape(ML_KV_RANK, ML_HEADS, ML_NOPE + ML_V)
        wkv = jnp.concatenate([wkv[:, :, :ML_NOPE].reshape(ML_KV_RANK, -1),
                               wkv[:, :, ML_NOPE:].reshape(ML_KV_RANK, -1)], axis=1).astype(BF16)
        qf, kf, v_mla = mla_prep(pa, proj, positions, mla_q_norm[l], mla_kv_norm[l], wq_all, wkv)
        y_c = mla_attention(qf, kf, v_mla, batch)

        wo = w_out[l].astype(BF16)
        xf = matmul_res([y_a, y_b, y_c],
                        [wo[:RW_DIM], wo[RW_DIM:RW_DIM + DF_DIM], wo[RW_DIM + DF_DIM:]], xf)

        kv_mem = norm_matmul(memf, norm_mem[l], ca_wkv[l].astype(BF16), out_dtype=BF16)
        xf = cross_block(xf, batch, norm_cross[l], ca_wq[l].astype(BF16), kv_mem, ca_wo[l].astype(BF16))

        w_router = jnp.concatenate(
            [moe_w_expert[l], moe_w_group[l], jnp.zeros((d, LANES - MOE_EXPERTS - MOE_GROUPS), F32)], axis=1)
        b_router = jnp.concatenate(
            [moe_b_expert[l], moe_b_group[l], jnp.zeros((LANES - MOE_EXPERTS - MOE_GROUPS,), F32)]).reshape(1, LANES)
        xf = moe_block(xf, norm_ffn[l], w_router, b_router, moe_w_gate, moe_w_up, moe_w_down, l,
                       final_norm if l == depth - 1 else None)

    return xf.reshape(batch, seq, d)
```

```python
import functools
import math

import jax
import jax.numpy as jnp
from jax import lax
from jax.experimental import pallas as pl
from jax.experimental.pallas import tpu as pltpu

F32 = jnp.float32
BF16 = jnp.bfloat16

NORM_EPS = 1e-6
ROPE_THETA = 10000.0

RW_HEADS = 16
RW_HEAD_DIM = 64
RW_DIM = RW_HEADS * RW_HEAD_DIM
RW_W_RANK = 64
RW_A_RANK = 64
RW_G_RANK = 128
RW_V_RANK = 32
RW_LORA = RW_W_RANK + RW_A_RANK + RW_G_RANK
RW_LN_EPS = 64e-5
RW_COLS = 3 * RW_DIM + RW_LORA

DF_HEADS = 4
DF_HEAD_DIM = 64
DF_V_DIM = 2 * DF_HEAD_DIM
DF_QK = DF_HEADS * 2 * DF_HEAD_DIM
DF_DIM = DF_HEADS * DF_V_DIM
DF_COLS = 2 * DF_QK + DF_DIM
DF_SUBLN_EPS = 1e-5

ML_HEADS = 4
ML_Q_RANK = 384
ML_KV_RANK = 256
ML_NOPE = 128
ML_ROPE = 64
ML_V = 128
ML_DIM = ML_HEADS * ML_V
ML_COLS = ML_Q_RANK + ML_KV_RANK + ML_ROPE

REL_BUCKETS = 32
REL_MAX_DIST = 128

CA_HEADS = 4
CA_HEAD_DIM = 128
CA_DIM = CA_HEADS * CA_HEAD_DIM

MOE_GROUPS = 4
MOE_PER_GROUP = 8
MOE_EXPERTS = MOE_GROUPS * MOE_PER_GROUP

LANES = 128
SCAN_CHUNK = 64
SCAN_GROUP = 4
SCAN_BATCHES = 2
ATTN_TILE = 512
POST_ROW_BLOCKS = 4
DMA_ISSUE_UNROLL = 8
MOE_ROW_TILE = 256
VMEM_LIMIT = 56 * 1024 * 1024
NEG_BIG = -1e30

LOG2E = 1.4426950408889634

OFF_LORA = 0
OFF_VRES = OFF_LORA + RW_LORA
OFF_KPE = OFF_VRES + LANES
PROJ_F_COLS = OFF_KPE + LANES
OFF_R = 0
OFF_K = RW_DIM
OFF_V = 2 * RW_DIM
OFF_DQ = 3 * RW_DIM
OFF_DK = OFF_DQ + DF_QK
OFF_DV = OFF_DK + DF_QK
OFF_MKV = OFF_DV + DF_DIM
OFF_MQ = 13 * ML_Q_RANK
PROJ_A_COLS = OFF_MQ + ML_Q_RANK


def _cparams(sem, vmem=VMEM_LIMIT, flags=None):
    return pltpu.CompilerParams(dimension_semantics=sem, vmem_limit_bytes=vmem, flags=flags)


def _dot(a, b):
    return jnp.dot(a, b, preferred_element_type=F32)


def _dot_t(a, b):
    return lax.dot_general(a, b, (((1,), (1,)), ((), ())), preferred_element_type=F32)


def _split3(x):
    hi = x.astype(BF16)
    r1 = x - hi.astype(F32)
    mid = r1.astype(BF16)
    lo = (r1 - mid.astype(F32)).astype(BF16)
    return hi, mid, lo


def _dot_rhs01(x, ones_bf16):
    hi = x.astype(BF16)
    lo = (x - hi.astype(F32)).astype(BF16)
    return _dot(hi, ones_bf16) + _dot(lo, ones_bf16)


def _dot_x3(a, b):
    ah = a.astype(BF16)
    al = (a - ah.astype(F32)).astype(BF16)
    bh = b.astype(BF16)
    bl = (b - bh.astype(F32)).astype(BF16)
    return _dot(ah, bh) + _dot(ah, bl) + _dot(al, bh)


def _rms(x, w, eps):
    ms = jnp.mean(x * x, axis=-1, keepdims=True)
    return x * lax.rsqrt(ms + eps) * w


def _norm_matmul_kernel(*refs, eps, has_side):
    if has_side:
        x_ref, nw_ref, w_ref, ws_ref, o_ref, os_ref, xn_ref = refs
    else:
        x_ref, nw_ref, w_ref, o_ref, xn_ref = refs

    @pl.when(pl.program_id(1) == 0)
    def _():
        xn_ref[...] = _rms(x_ref[...], nw_ref[...], eps).astype(BF16)
        if has_side:
            os_ref[...] = _dot(xn_ref[...], ws_ref[...])

    o_ref[...] = _dot(xn_ref[...], w_ref[...]).astype(o_ref.dtype)


def norm_matmul(x, nw, w, w_side=None, *, out_dtype=F32, tm=512, tn=None, eps=NORM_EPS):
    m, d = x.shape
    n = w.shape[1]
    tm = min(tm, m)
    tn = n if tn is None else tn
    has_side = w_side is not None
    in_specs = [pl.BlockSpec((tm, d), lambda i, j: (i, 0)),
                pl.BlockSpec((1, d), lambda i, j: (0, 0)),
                pl.BlockSpec((d, tn), lambda i, j: (0, j))]
    out_specs = [pl.BlockSpec((tm, tn), lambda i, j: (i, j))]
    out_shape = [jax.ShapeDtypeStruct((m, n), out_dtype)]
    args = [x, nw.reshape(1, d), w]
    if has_side:
        ns = w_side.shape[1]
        in_specs.append(pl.BlockSpec((d, ns), lambda i, j: (0, 0)))
        out_specs.append(pl.BlockSpec((tm, ns), lambda i, j: (i, 0)))
        out_shape.append(jax.ShapeDtypeStruct((m, ns), F32))
        args.append(w_side)
    outs = pl.pallas_call(
        functools.partial(_norm_matmul_kernel, eps=eps, has_side=has_side),
        grid=(m // tm, n // tn),
        in_specs=in_specs,
        out_specs=out_specs,
        out_shape=out_shape,
        scratch_shapes=[pltpu.VMEM((tm, d), BF16)],
        compiler_params=_cparams(("parallel", "arbitrary")),
        name="norm_matmul",
    )(*args)
    return outs if has_side else outs[0]


def _matmul_res_kernel(*refs, n_a):
    a_refs, w_refs = refs[:n_a], refs[n_a:2 * n_a]
    res_ref, o_ref = refs[2 * n_a], refs[2 * n_a + 1]
    acc = res_ref[...]
    for a_ref, w_ref in zip(a_refs, w_refs):
        acc = acc + _dot(a_ref[...].astype(BF16), w_ref[...])
    o_ref[...] = acc


def matmul_res(a_list, w_list, res, *, tm=512, tn=2048):
    m, n = res.shape
    tm = min(tm, m)
    tn = min(tn, n)
    n_a = len(a_list)
    in_specs = ([pl.BlockSpec((tm, a.shape[1]), lambda i, j: (i, 0)) for a in a_list]
                + [pl.BlockSpec((w.shape[0], tn), lambda i, j: (0, j)) for w in w_list]
                + [pl.BlockSpec((tm, tn), lambda i, j: (i, j))])
    return pl.pallas_call(
        functools.partial(_matmul_res_kernel, n_a=n_a),
        grid=(m // tm, n // tn),
        in_specs=in_specs,
        out_specs=pl.BlockSpec((tm, tn), lambda i, j: (i, j)),
        out_shape=jax.ShapeDtypeStruct((m, n), F32),
        compiler_params=_cparams(("parallel", "arbitrary")),
        name="matmul_res",
    )(*a_list, *w_list, res)


def _softplus(z):
    return jnp.maximum(z, 0.0) + jnp.log(1.0 + jnp.exp(-jnp.abs(z)))


def _rwkv_prep_kernel(*refs, has_vres):
    if has_vres:
        (pr_ref, pk_ref, pv_ref, pl_ref, pvr_ref, vfirst_ref,
         mu_r, mu_k, mu_v, mu_l, mu_vr, w0, w2, a0, a2, g2, v0, v2,
         k_k, k_a, r_k, seg, seg_t,
         r_o, lw_o, k_o, v_o, kap_o, beta_o, g_o, bonus_o,
         last_r, last_k, last_v, last_l, last_vr) = refs
    else:
        (pr_ref, pk_ref, pv_ref, pl_ref,
         mu_r, mu_k, mu_v, mu_l, w0, w2, a0, a2, g2,
         k_k, k_a, r_k, seg, seg_t,
         r_o, lw_o, k_o, v_o, kap_o, beta_o, g_o, bonus_o,
         last_r, last_k, last_v, last_l) = refs
    t = pl.program_id(1)

    def shifted(p_ref, last_ref, mu_ref):
        p = p_ref[...].astype(F32)
        n = p.shape[0]
        carried = jnp.where(t == 0, 0.0, last_ref[0:1, :])
        row = lax.broadcasted_iota(jnp.int32, p.shape, 0)
        prev = jnp.where(row == 0, carried, pltpu.roll(p, 1, axis=0))
        last_ref[0:1, :] = p[n - 1:n, :]
        return p + mu_ref[...] * (prev - p)

    r = shifted(pr_ref, last_r, mu_r)
    k = shifted(pk_ref, last_k, mu_k)
    v = shifted(pv_ref, last_v, mu_v)
    lora = shifted(pl_ref, last_l, mu_l)
    wl = lora[:, :LANES]
    gl = lora[:, LANES:]

    lane = lax.broadcasted_iota(jnp.int32, wl.shape, 1)
    wl_t = jnp.where(lane < RW_W_RANK, jnp.tanh(wl), 0.0)
    al = jnp.where(lane >= RW_W_RANK, wl, 0.0)
    w_log = -_softplus(-(w0[...] + _dot_x3(wl_t, w2[...]))) - 0.5
    lw_o[...] = -jnp.exp(w_log)
    a = jax.nn.sigmoid(a0[...] + _dot_x3(al, a2[...]))
    g_o[...] = _dot(jax.nn.sigmoid(gl).astype(BF16), g2[...]).astype(g_o.dtype)

    segm, segm_t = seg[...], seg_t[...]

    def head_sum(x):
        return _dot_rhs01(_dot_rhs01(x, segm), segm_t)

    kk = k * k_k[...]
    kk = kk * lax.rsqrt(jnp.maximum(head_sum(kk * kk), 1e-24))
    k = k * (1.0 + (a - 1.0) * k_a[...])
    if has_vres:
        vr = shifted(pvr_ref, last_vr, mu_vr)
        mix = jax.nn.sigmoid(v0[...] + _dot_x3(vr, v2[...]))
        v = v + (vfirst_ref[...] - v) * mix
    r_o[...] = r.astype(r_o.dtype)
    k_o[...] = k.astype(k_o.dtype)
    v_o[...] = v.astype(v_o.dtype)
    kap_o[...] = kk.astype(kap_o.dtype)
    beta_o[...] = (kk * a).astype(beta_o.dtype)
    bonus_o[...] = (head_sum(r * k * r_k[...]) * v).astype(bonus_o.dtype)


def rwkv_prep(pa, proj, batch, vfirst, prm, *, tt=512):
    tokens = proj.shape[0]
    seq = tokens // batch
    tt = min(tt, seq)
    nt = seq // tt
    has_vres = vfirst is not None
    d = RW_DIM

    def rows(width, col):
        return pl.BlockSpec((tt, width), lambda b, t, col=col: (b * nt + t, col))

    def full(shape):
        return pl.BlockSpec(shape, lambda b, t: (0, 0))

    in_specs = [rows(d, OFF_R // d), rows(d, OFF_K // d), rows(d, OFF_V // d),
                rows(RW_LORA, OFF_LORA // RW_LORA)]
    args = [pa, pa, pa, proj]
    if has_vres:
        in_specs += [rows(LANES, OFF_VRES // LANES), rows(d, 0)]
        args += [proj, vfirst]
    names = ["mu_r", "mu_k", "mu_v", "mu_l"] + (["mu_vr"] if has_vres else []) + ["w0", "w2", "a0", "a2", "g2"]
    names += (["v0", "v2"] if has_vres else []) + ["k_k", "k_a", "r_k", "seg", "seg_t"]
    for nm in names:
        in_specs.append(full(prm[nm].shape))
        args.append(prm[nm])
    out_spec = pl.BlockSpec((tt, d), lambda b, t: (b * nt + t, 0))
    scratch = [pltpu.VMEM((8, d), F32)] * 3 + [pltpu.VMEM((8, RW_LORA), F32)]
    if has_vres:
        scratch.append(pltpu.VMEM((8, LANES), F32))
    return pl.pallas_call(
        functools.partial(_rwkv_prep_kernel, has_vres=has_vres),
        grid=(batch, nt),
        in_specs=in_specs,
        out_specs=[out_spec] * 8,
        out_shape=[jax.ShapeDtypeStruct((tokens, d), F32 if i == 1 else BF16) for i in range(8)],
        scratch_shapes=scratch,
        compiler_params=_cparams(("arbitrary", "arbitrary")),
        name="rwkv_prep",
    )(*args)


def _rwkv_scan_kernel(r_ref, lw_ref, k_ref, v_ref, kap_ref, beta_ref, tril_ref, bmask_ref,
                      o_ref, ht_ref):
    @pl.when(pl.program_id(1) == 0)
    def _():
        ht_ref[...] = jnp.zeros_like(ht_ref)

    n_batch, c, d = lw_ref.shape
    w = ht_ref.shape[1]
    g = w // RW_HEAD_DIM
    bmask = bmask_ref[...]
    bmask_b = bmask.astype(BF16)
    tril3 = tril_ref[...]
    t_idx = lax.broadcasted_iota(jnp.int32, (c, w), 0)
    s_idx = lax.broadcasted_iota(jnp.int32, (c, w), 1) % c
    strict = t_idx > s_idx
    incl = t_idx >= s_idx
    n_sq = int(math.log2(c))

    def stack(x):
        return jnp.concatenate([x.astype(BF16)] * g, axis=0) * bmask_b

    sls = [(bi, slice(None), slice(lo, lo + w)) for bi in range(n_batch) for lo in range(0, d, w)]
    groups = range(len(sls))
    lw = [lw_ref[sl] for sl in sls]
    cum = [_dot(tril3, jnp.concatenate(_split3(x), axis=0)) for x in lw]
    total = [x[c - 1:c, :] for x in cum]
    ar = [jnp.concatenate([-kap_ref[sls[gi]] * jnp.exp(cum[gi] - lw[gi]), r_ref[sls[gi]] * jnp.exp(cum[gi])],
                          axis=0).astype(BF16) for gi in groups]
    p_inv = [jnp.exp(-x) for x in cum]
    b_s = [stack(beta_ref[sls[gi]] * p_inv[gi]) for gi in groups]
    k_s = [stack(k_ref[sls[gi]] * p_inv[gi]) for gi in groups]
    v_n = [v_ref[sl] for sl in sls]
    v_s = [stack(x) for x in v_n]

    arb = [_dot_t(ar[gi], b_s[gi]) for gi in groups]
    ark = [_dot_t(ar[gi], k_s[gi]) for gi in groups]
    ab = [jnp.where(strict, m[:c], 0.0) for m in arb]
    rb = [jnp.where(incl, m[c:], 0.0).astype(BF16) for m in arb]
    akrk = [jnp.concatenate([jnp.where(strict, m[:c], 0.0), jnp.where(incl, m[c:], 0.0)], axis=0).astype(BF16)
            for m in ark]

    ht = [ht_ref[gi] for gi in groups]
    base = [_dot_t(ar[gi], ht[gi].astype(BF16)) + _dot(akrk[gi], v_s[gi]) for gi in groups]
    x = [m[:c] for m in base]
    lp = ab
    for i in range(n_sq):
        lpb = [m.astype(BF16) for m in lp]
        x = [x[gi] + _dot(lpb[gi], stack(x[gi])) for gi in groups]
        if i < n_sq - 1:
            lp = [_dot(lpb[gi], stack(lp[gi])) for gi in groups]
    for gi in groups:
        o_ref[sls[gi]] = base[gi][c:] + _dot(rb[gi], stack(x[gi]))

    for gi in groups:
        p_rem = jnp.exp(total[gi] - cum[gi])
        z = jnp.concatenate([beta_ref[sls[gi]] * p_rem, k_ref[sls[gi]] * p_rem], axis=0).astype(BF16)
        uv_t = jnp.concatenate([x[gi], v_n[gi].astype(F32)], axis=0).T.astype(BF16)
        ht_ref[gi] = ht[gi] * jnp.exp(total[gi]) + bmask * _dot(uv_t, z)


def rwkv_scan(r, lw, k, v, kap, beta, batch):
    tokens, d = r.shape
    seq = tokens // batch
    c = min(SCAN_CHUNK, seq)
    nc = seq // c
    gw = SCAN_GROUP * RW_HEAD_DIM
    rr = SCAN_GROUP * c
    assert c == RW_HEAD_DIM, "the stacking mask doubles as the head-block mask of the state"
    bb = math.gcd(batch, SCAN_BATCHES)
    tril = jnp.tile((jnp.arange(c)[:, None] >= jnp.arange(c)[None, :]).astype(BF16), (1, 3))
    bmask = (jnp.arange(rr)[:, None] // c == jnp.arange(gw)[None, :] // RW_HEAD_DIM).astype(F32)
    blk = pl.BlockSpec((bb, c, d), lambda b, i: (b, i, 0))
    as3d = lambda a: a.reshape(batch, seq, d)
    out = pl.pallas_call(
        _rwkv_scan_kernel,
        grid=(batch // bb, nc),
        in_specs=[blk] * 6 + [pl.BlockSpec((c, 3 * c), lambda b, i: (0, 0)),
                              pl.BlockSpec((rr, gw), lambda b, i: (0, 0))],
        out_specs=blk,
        out_shape=jax.ShapeDtypeStruct((batch, seq, d), F32),
        scratch_shapes=[pltpu.VMEM((bb * (d // gw), gw, gw), F32)],
        compiler_params=_cparams(("arbitrary", "arbitrary")),
        name="rwkv_scan",
    )(as3d(r), as3d(lw), as3d(k), as3d(v), as3d(kap), as3d(beta), tril, bmask)
    return out.reshape(tokens, d)


def _rwkv_post_kernel(o_ref, bonus_ref, g_ref, lnw_ref, lnb_ref, seg, seg_t, y_ref):
    segm, segm_t = seg[...], seg_t[...]

    def head_means(xs):
        sums = [_dot_rhs01(x, segm) for x in xs]
        return [_dot_rhs01(s, segm_t) * (1.0 / RW_HEAD_DIM) for s in sums]

    rows = o_ref.shape[0] // POST_ROW_BLOCKS
    blocks = [slice(i * rows, (i + 1) * rows) for i in range(POST_ROW_BLOCKS)]
    o = [o_ref[b, :] for b in blocks]
    dlt = [x - m for x, m in zip(o, head_means(o))]
    var = head_means([x * x for x in dlt])
    for b, x, v in zip(blocks, dlt, var):
        y = x * lax.rsqrt(v + RW_LN_EPS) * lnw_ref[...] + lnb_ref[...]
        y_ref[b, :] = ((y + bonus_ref[b, :]) * g_ref[b, :]).astype(y_ref.dtype)


def rwkv_post(o, bonus, g, ln_w, ln_b, seg, seg_t, *, tm=512):
    tokens, d = o.shape
    tm = min(tm, tokens)
    blk = pl.BlockSpec((tm, d), lambda i: (i, 0))
    vec = pl.BlockSpec((1, d), lambda i: (0, 0))
    return pl.pallas_call(
        _rwkv_post_kernel,
        grid=(tokens // tm,),
        in_specs=[blk, blk, blk, vec, vec,
                  pl.BlockSpec(seg.shape, lambda i: (0, 0)), pl.BlockSpec(seg_t.shape, lambda i: (0, 0))],
        out_specs=blk,
        out_shape=jax.ShapeDtypeStruct((tokens, d), BF16),
        compiler_params=_cparams(("parallel",)),
        name="rwkv_post",
    )(o, bonus, g, ln_w.reshape(1, d), ln_b.reshape(1, d), seg, seg_t)


def _t5_thresholds():
    max_exact = REL_BUCKETS // 2
    thr = list(range(1, max_exact))
    n = max_exact
    for bucket in range(max_exact, REL_BUCKETS):
        while True:
            large = max_exact + int(math.log(max(n, max_exact) / max_exact)
                                    / math.log(REL_MAX_DIST / max_exact) * (REL_BUCKETS - max_exact))
            if min(large, REL_BUCKETS - 1) >= bucket:
                break
            n += 1
        thr.append(n)
    return thr


T5_THRESHOLDS = _t5_thresholds()
T5_FAR = T5_THRESHOLDS[-1]


def _softmax_tiles(s_list, c_list, states, vt_list):
    stats = []
    for s_t, c, (m_old, l_old, _) in zip(s_list, c_list, states):
        m_new = jnp.maximum(m_old, jnp.max(s_t, axis=0, keepdims=True) + c)
        alpha = jnp.exp2(m_old - m_new)
        p_t = jnp.exp2(s_t - (m_new - c))
        stats.append((m_new, alpha, alpha * l_old + jnp.sum(p_t, axis=0, keepdims=True), p_t.astype(BF16)))
    return tuple((m_new, l_new, alpha * acc + _dot(vt, p_t))
                 for (m_new, alpha, l_new, p_t), (_, _, acc), vt in zip(stats, states, vt_list))


def _transpose_into(vt_ref, v_ref, chunk):
    seq = v_ref.shape[0]
    for c in range(seq // chunk):
        vt_ref[:, c * chunk:(c + 1) * chunk] = v_ref[c * chunk:(c + 1) * chunk, :].astype(F32).T.astype(BF16)


def _diff_attn_kernel(qfirst_ref, klast_ref, q_ref, k_ref, v_ref, qpos_ref, kpos_ref, subln_ref, table_ref, lam_ref,
                      o_ref, vt_ref, *, tq, tk, scale2, out_scale):
    b, i = pl.program_id(0), pl.program_id(1)
    nq = pl.num_programs(1)
    seq = k_ref.shape[0]
    nk = seq // tk
    w = DF_V_DIM

    @pl.when(i == 0)
    def _():
        _transpose_into(vt_ref, v_ref, tk)

    n_tiles = (i * tq + tq - 1) // tk + 1
    qf = qfirst_ref[b * nq + i]
    n_far = lax.while_loop(
        lambda j: (j * tk + tk - 1 <= i * tq) & (qf - klast_ref[b * nk + jnp.minimum(j, nk - 1)] >= T5_FAR),
        lambda j: j + 1, jnp.int32(0))

    dist = lax.broadcasted_iota(jnp.int32, (1, LANES), 1)
    qpos = qpos_ref[...]
    q_idx = i * tq + lax.broadcasted_iota(jnp.int32, (tk, tq), 1)
    k_off = lax.broadcasted_iota(jnp.int32, (tk, tq), 0)
    lane = lax.broadcasted_iota(jnp.int32, (tq, w), 1)

    bias_rows, c_far, qm = [], [], []
    for h in range(DF_HEADS):
        bias_vec = jnp.full((1, LANES), table_ref[h], F32)
        for bucket, thr in enumerate(T5_THRESHOLDS, start=1):
            bias_vec = jnp.where(dist >= thr, table_ref[bucket * DF_HEADS + h], bias_vec)
        bias_rows.append(jnp.broadcast_to(bias_vec * LOG2E, (tk, LANES)))
        c_far.append(table_ref[(REL_BUCKETS - 1) * DF_HEADS + h] * LOG2E)
        qh = q_ref[:, h * w:(h + 1) * w].astype(F32) * scale2
        qm.append([jnp.where((lane >= mi * DF_HEAD_DIM) & (lane < (mi + 1) * DF_HEAD_DIM), qh, 0.0).astype(BF16)
                   for mi in range(2)])

    def tiles(j, h):
        off = pl.multiple_of(j * tk, tk)
        return k_ref[pl.ds(off, tk), h * w:(h + 1) * w], vt_ref[h * w:(h + 1) * w, pl.ds(off, tk)], off

    chains = [(h, mi) for h in range(DF_HEADS) for mi in range(2)]

    def far_body(j, st):
        kv = [tiles(j, h) for h in range(DF_HEADS)]
        s = [_dot_t(kv[h][0], qm[h][mi]) for h, mi in chains]
        return _softmax_tiles(s, [c_far[h] for h, _ in chains], st, [kv[h][1] for h, _ in chains])

    def near_body(j, st):
        off = pl.multiple_of(j * tk, tk)
        n = jnp.clip(qpos - kpos_ref[pl.ds(off, tk), :], 0, LANES - 1)
        keep = q_idx >= off + k_off
        kv = [tiles(j, h) for h in range(DF_HEADS)]
        bias = [jnp.concatenate(
            [jnp.take_along_axis(bias_rows[h], n[:, cb * LANES:(cb + 1) * LANES], axis=1)
             for cb in range(tq // LANES)], axis=1) for h in range(DF_HEADS)]
        s = [jnp.where(keep, _dot_t(kv[h][0], qm[h][mi]) + bias[h], NEG_BIG) for h, mi in chains]
        return _softmax_tiles(s, [0.0] * len(chains), st, [kv[h][1] for h, _ in chains])

    init = tuple((jnp.full((1, tq), NEG_BIG, F32), jnp.zeros((1, tq), F32), jnp.zeros((w, tq), F32))
                 for _ in range(2 * DF_HEADS))
    st = lax.fori_loop(0, n_far, far_body, init)
    st = lax.fori_loop(n_far, n_tiles, near_body, st)
    for h in range(DF_HEADS):
        s0, s1 = st[2 * h], st[2 * h + 1]
        d_t = s0[2] / s0[1] - lam_ref[0] * (s1[2] / s1[1])
        ms = jnp.mean(d_t * d_t, axis=0, keepdims=True)
        y_t = d_t * lax.rsqrt(ms + DF_SUBLN_EPS) * (subln_ref[...] * out_scale)
        o_ref[:, h * w:(h + 1) * w] = y_t.T.astype(o_ref.dtype)


def diff_attention(pa, positions, rel_bias, lam, lambda_init, subln_w, *, tq=ATTN_TILE):
    batch, seq = positions.shape
    tokens = batch * seq
    tq = min(tq, seq)
    tk = tq
    nq, nk = seq // tq, seq // tk
    qfirst = positions[:, ::tq].reshape(-1)
    klast = positions[:, tk - 1::tk].reshape(-1)
    qpos = positions.reshape(batch, 1, seq)
    kpos = positions.reshape(batch, seq, 1)
    wd = DF_DIM
    grid_spec = pltpu.PrefetchScalarGridSpec(
        num_scalar_prefetch=2,
        grid=(batch, nq),
        in_specs=[pl.BlockSpec((tq, wd), lambda b, i, *_: (b * nq + i, OFF_DQ // wd)),
                  pl.BlockSpec((seq, wd), lambda b, i, *_: (b, OFF_DK // wd)),
                  pl.BlockSpec((seq, wd), lambda b, i, *_: (b, OFF_DV // wd)),
                  pl.BlockSpec((None, 1, tq), lambda b, i, *_: (b, 0, i)),
                  pl.BlockSpec((None, seq, 1), lambda b, i, *_: (b, 0, 0)),
                  pl.BlockSpec((DF_V_DIM, 1), lambda b, i, *_: (0, 0)),
                  pl.BlockSpec(memory_space=pltpu.SMEM),
                  pl.BlockSpec(memory_space=pltpu.SMEM)],
        out_specs=pl.BlockSpec((tq, wd), lambda b, i, *_: (b * nq + i, 0)),
        scratch_shapes=[pltpu.VMEM((wd, seq), BF16)],
    )
    return pl.pallas_call(
        functools.partial(_diff_attn_kernel, tq=tq, tk=tk, scale2=DF_HEAD_DIM ** -0.5 * LOG2E,
                          out_scale=1.0 - lambda_init),
        grid_spec=grid_spec,
        out_shape=jax.ShapeDtypeStruct((tokens, DF_DIM), BF16),
        compiler_params=_cparams(("arbitrary", "arbitrary")),
        name="diff_attention",
    )(qfirst, klast, pa, pa, pa, qpos, kpos, subln_w.reshape(DF_V_DIM, 1), rel_bias.reshape(-1), lam.reshape(1))


ML_QK_PAD = 2 * LANES


def _rope_block(x, cos, sin):
    half = ML_ROPE // 2
    lane = lax.broadcasted_iota(jnp.int32, x.shape, 1)
    rot = jnp.where(lane < half, -pltpu.roll(x, LANES - half, axis=1),
                    jnp.where(lane < ML_ROPE, pltpu.roll(x, half, axis=1), 0.0))
    return x * cos + rot * sin


def _mla_prep_kernel(mq_ref, mkv_ref, kpe_ref, pos_ref, qn_w, kvn_w, wq_ref, wkv_ref, freq_ref,
                     qf_o, kf_o, v_o, *, qscale):
    ang = pos_ref[...].astype(F32) * freq_ref[...]
    cos, sin = jnp.cos(ang), jnp.sin(ang)
    qc = _rms(mq_ref[...].astype(F32), qn_w[...], NORM_EPS).astype(BF16)
    q_all = _dot(qc, wq_ref[...]) * qscale
    kvc = _rms(mkv_ref[...].astype(F32), kvn_w[...], NORM_EPS).astype(BF16)
    kvb = _dot(kvc, wkv_ref[...])
    kpe = _rope_block(kpe_ref[...], cos, sin).astype(BF16)
    nope_w = ML_HEADS * ML_NOPE
    for h in range(ML_HEADS):
        lo = h * ML_QK_PAD
        qf_o[:, lo:lo + LANES] = q_all[:, h * LANES:(h + 1) * LANES].astype(BF16)
        qf_o[:, lo + LANES:lo + 2 * LANES] = _rope_block(
            q_all[:, nope_w + h * LANES:nope_w + (h + 1) * LANES], cos, sin).astype(BF16)
        kf_o[:, lo:lo + LANES] = kvb[:, h * LANES:(h + 1) * LANES].astype(BF16)
        kf_o[:, lo + LANES:lo + 2 * LANES] = kpe
    v_o[...] = kvb[:, nope_w:].astype(BF16)


def mla_prep(pa, pf, positions, q_norm, kv_norm, wq_all, wkv, *, tm=512):
    tokens = pa.shape[0]
    tm = min(tm, tokens)
    half = ML_ROPE // 2
    inv_freq = ROPE_THETA ** (-jnp.arange(half, dtype=F32) / half)
    freq = jnp.concatenate([inv_freq, inv_freq, jnp.zeros((LANES - ML_ROPE,), F32)]).reshape(1, LANES)

    def full(a):
        return pl.BlockSpec(a.shape, lambda i: (0, 0))

    qn_w = q_norm.reshape(1, -1)
    kvn_w = kv_norm.reshape(1, -1)
    wide = ML_HEADS * ML_QK_PAD
    return pl.pallas_call(
        functools.partial(_mla_prep_kernel, qscale=(ML_NOPE + ML_ROPE) ** -0.5 * LOG2E),
        grid=(tokens // tm,),
        in_specs=[pl.BlockSpec((tm, ML_Q_RANK), lambda i: (i, OFF_MQ // ML_Q_RANK)),
                  pl.BlockSpec((tm, ML_KV_RANK), lambda i: (i, OFF_MKV // ML_KV_RANK)),
                  pl.BlockSpec((tm, LANES), lambda i: (i, OFF_KPE // LANES)),
                  pl.BlockSpec((tm, 1), lambda i: (i, 0)),
                  full(qn_w), full(kvn_w), full(wq_all), full(wkv), full(freq)],
        out_specs=[pl.BlockSpec((tm, wide), lambda i: (i, 0)),
                   pl.BlockSpec((tm, wide), lambda i: (i, 0)),
                   pl.BlockSpec((tm, ML_DIM), lambda i: (i, 0))],
        out_shape=[jax.ShapeDtypeStruct((tokens, wide), BF16),
                   jax.ShapeDtypeStruct((tokens, wide), BF16),
                   jax.ShapeDtypeStruct((tokens, ML_DIM), BF16)],
        compiler_params=_cparams(("parallel",)),
        name="mla_prep",
    )(pa, pa, pf, positions.reshape(tokens, 1), qn_w, kvn_w, wq_all, wkv, freq)


def _mla_attn_kernel(q_ref, k_ref, v_ref, o_ref, vt_ref, *, tq, tk):
    i = pl.program_id(1)
    wq = ML_QK_PAD

    @pl.when(i == 0)
    def _():
        _transpose_into(vt_ref, v_ref, tk)

    n_tiles = (i * tq + tq - 1) // tk + 1
    n_full = (i * tq + 1) // tk
    q_idx = i * tq + lax.broadcasted_iota(jnp.int32, (tk, tq), 1)
    k_off = lax.broadcasted_iota(jnp.int32, (tk, tq), 0)
    qh = [q_ref[:, h * wq:(h + 1) * wq] for h in range(ML_HEADS)]

    def tiles(j, h):
        off = pl.multiple_of(j * tk, tk)
        return (k_ref[pl.ds(off, tk), h * wq:(h + 1) * wq],
                vt_ref[h * ML_V:(h + 1) * ML_V, pl.ds(off, tk)], off)

    heads = range(ML_HEADS)

    def full_body(j, st):
        kv = [tiles(j, h) for h in heads]
        s = [_dot_t(kv[h][0], qh[h]) for h in heads]
        return _softmax_tiles(s, [0.0] * ML_HEADS, st, [kv[h][1] for h in heads])

    def diag_body(j, st):
        kv = [tiles(j, h) for h in heads]
        keep = q_idx >= kv[0][2] + k_off
        s = [jnp.where(keep, _dot_t(kv[h][0], qh[h]), NEG_BIG) for h in heads]
        return _softmax_tiles(s, [0.0] * ML_HEADS, st, [kv[h][1] for h in heads])

    st = tuple((jnp.full((1, tq), NEG_BIG, F32), jnp.zeros((1, tq), F32), jnp.zeros((ML_V, tq), F32))
               for _ in range(ML_HEADS))
    st = lax.fori_loop(0, n_full, full_body, st)
    st = lax.fori_loop(n_full, n_tiles, diag_body, st)
    for h in range(ML_HEADS):
        o_ref[:, h * ML_V:(h + 1) * ML_V] = (st[h][2] / st[h][1]).T.astype(o_ref.dtype)


def mla_attention(qf, kf, v, batch, *, tq=ATTN_TILE):
    tokens = qf.shape[0]
    seq = tokens // batch
    tq = min(tq, seq)
    tk = tq
    nq = seq // tq
    wide = qf.shape[1]
    return pl.pallas_call(
        functools.partial(_mla_attn_kernel, tq=tq, tk=tk),
        grid=(batch, nq),
        in_specs=[pl.BlockSpec((tq, wide), lambda b, i: (b * nq + i, 0)),
                  pl.BlockSpec((seq, wide), lambda b, i: (b, 0)),
                  pl.BlockSpec((seq, ML_DIM), lambda b, i: (b, 0))],
        out_specs=pl.BlockSpec((tq, ML_DIM), lambda b, i: (b * nq + i, 0)),
        out_shape=jax.ShapeDtypeStruct((tokens, ML_DIM), BF16),
        scratch_shapes=[pltpu.VMEM((ML_DIM, seq), BF16)],
        compiler_params=_cparams(("arbitrary", "arbitrary")),
        name="mla_attention",
    )(qf, kf, v)


def _cross_kernel(x_ref, nw_ref, wq_ref, kv_ref, wo_ref, o_ref):
    x = x_ref[...]
    q = _dot(_rms(x, nw_ref[...], NORM_EPS).astype(BF16), wq_ref[...])
    kv = kv_ref[...]
    scale = CA_HEAD_DIM ** -0.5
    outs = []
    for hh in range(CA_HEADS):
        sl = slice(hh * CA_HEAD_DIM, (hh + 1) * CA_HEAD_DIM)
        s = _dot_t(q[:, sl].astype(BF16), kv[:, sl]) * scale
        p = jnp.exp(s - jnp.max(s, axis=-1, keepdims=True))
        p = p / jnp.sum(p, axis=-1, keepdims=True)
        outs.append(_dot(p.astype(BF16), kv[:, CA_DIM + hh * CA_HEAD_DIM:CA_DIM + (hh + 1) * CA_HEAD_DIM]))
    o = jnp.concatenate(outs, axis=1).astype(BF16)
    o_ref[...] = x + _dot(o, wo_ref[...])


def cross_block(x, batch, norm_w, wq, kv, wo, *, tq=512):
    tokens, d = x.shape
    seq = tokens // batch
    tq = min(tq, seq)
    nq = seq // tq
    mem_len = kv.shape[0] // batch
    return pl.pallas_call(
        _cross_kernel,
        grid=(batch, nq),
        in_specs=[pl.BlockSpec((tq, d), lambda b, i: (b * nq + i, 0)),
                  pl.BlockSpec((1, d), lambda b, i: (0, 0)),
                  pl.BlockSpec(wq.shape, lambda b, i: (0, 0)),
                  pl.BlockSpec((mem_len, 2 * CA_DIM), lambda b, i: (b, 0)),
                  pl.BlockSpec(wo.shape, lambda b, i: (0, 0))],
        out_specs=pl.BlockSpec((tq, d), lambda b, i: (b * nq + i, 0)),
        out_shape=jax.ShapeDtypeStruct((tokens, d), F32),
        compiler_params=_cparams(("parallel", "parallel")),
        name="cross_block",
    )(x, norm_w.reshape(1, d), wq, kv, wo)


SEL_E1, SEL_E2, SEL_G1, SEL_G2 = 0, 1, 2, 3


def _route(logits, b_router):
    biased = logits + b_router
    lane = lax.broadcasted_iota(jnp.int32, logits.shape, 1)
    big = jnp.int32(LANES)

    def first_argmax(vals):
        mx = jnp.max(vals, axis=-1, keepdims=True)
        return jnp.min(jnp.where(vals == mx, lane, big), axis=-1, keepdims=True)

    def pick(vals, idx):
        return jnp.sum(jnp.where(lane == idx, vals, 0.0), axis=-1, keepdims=True)

    is_group = (lane >= MOE_EXPERTS) & (lane < MOE_EXPERTS + MOE_GROUPS)
    gl = jnp.where(is_group, logits, NEG_BIG)
    ge = jnp.exp(gl - jnp.max(gl, axis=-1, keepdims=True))
    gp = ge / jnp.sum(ge, axis=-1, keepdims=True)
    g_lane = first_argmax(jnp.where(is_group, biased, NEG_BIG))
    p_group = pick(gp, g_lane)
    lo = (g_lane - MOE_EXPERTS) * MOE_PER_GROUP
    in_group = (lane >= lo) & (lane < lo + MOE_PER_GROUP)
    eb = jnp.where(in_group, biased, NEG_BIG)
    i1 = first_argmax(eb)
    i2 = first_argmax(jnp.where(lane == i1, NEG_BIG, eb))
    l1, l2 = pick(logits, i1), pick(logits, i2)
    mx = jnp.maximum(l1, l2)
    e1, e2 = jnp.exp(l1 - mx), jnp.exp(l2 - mx)
    w1, w2 = e1 / (e1 + e2), e2 / (e1 + e2)
    return jnp.where(lane == SEL_E1, i1.astype(F32),
                     jnp.where(lane == SEL_E2, i2.astype(F32),
                               jnp.where(lane == SEL_G1, w1 * p_group,
                                         jnp.where(lane == SEL_G2, w2 * p_group, 0.0))))


def _router_kernel(x_ref, nw_ref, wr_ref, br_ref, h_ref, sel_ref):
    h = _rms(x_ref[...], nw_ref[...], NORM_EPS)
    h_ref[...] = h
    sel_ref[...] = _route(_dot_x3(h, wr_ref[...]), br_ref[...])


def moe_router(x, norm_w, w_router, b_router, *, tm=512):
    tokens, d = x.shape
    tm = min(tm, tokens)
    return pl.pallas_call(
        _router_kernel,
        grid=(tokens // tm,),
        in_specs=[pl.BlockSpec((tm, d), lambda i: (i, 0)),
                  pl.BlockSpec((1, d), lambda i: (0, 0)),
                  pl.BlockSpec((d, LANES), lambda i: (0, 0)),
                  pl.BlockSpec((1, LANES), lambda i: (0, 0))],
        out_specs=[pl.BlockSpec((tm, d), lambda i: (i, 0)),
                   pl.BlockSpec((tm, LANES), lambda i: (i, 0))],
        out_shape=[jax.ShapeDtypeStruct((tokens, d), F32),
                   jax.ShapeDtypeStruct((tokens, LANES), F32)],
        compiler_params=_cparams(("parallel",)),
        name="moe_router",
    )(x, norm_w.reshape(1, d), w_router, b_router)


def _moe_rank_kernel(sel_ref, ltri_ref, rank_ref, counts_ref, carry_ref):
    @pl.when(pl.program_id(0) == 0)
    def _():
        carry_ref[...] = jnp.zeros_like(carry_ref)

    sel = sel_ref[...]
    lane = lax.broadcasted_iota(jnp.int32, sel.shape, 1)
    lane_f = lane.astype(F32)
    oh1 = lane_f == sel[:, SEL_E1:SEL_E1 + 1]
    oh2 = lane_f == sel[:, SEL_E2:SEL_E2 + 1]
    f1, f2 = oh1.astype(F32), oh2.astype(F32)
    ltri = ltri_ref[...]
    before1 = _dot(ltri, f1.astype(BF16))
    before2 = _dot(ltri, f2.astype(BF16))
    c1 = jnp.sum(f1, axis=0, keepdims=True)
    c2 = jnp.sum(f2, axis=0, keepdims=True)
    carry = carry_ref[...]
    r1 = jnp.sum(jnp.where(oh1, before1 + carry, 0.0), axis=1, keepdims=True)
    r2 = jnp.sum(jnp.where(oh2, before2 + carry + c1, 0.0), axis=1, keepdims=True)
    rank_ref[...] = jnp.where(lane == SEL_E1, r1, jnp.where(lane == SEL_E2, r2, 0.0)).astype(jnp.int32)
    total = carry + c1 + c2
    carry_ref[...] = total
    counts_ref[...] = total.astype(jnp.int32)


def moe_rank(sel, *, tm=512):
    tokens = sel.shape[0]
    tm = min(tm, tokens)
    ltri = (jnp.arange(tm)[:, None] > jnp.arange(tm)[None, :]).astype(BF16)
    return pl.pallas_call(
        _moe_rank_kernel,
        grid=(tokens // tm,),
        in_specs=[pl.BlockSpec((tm, LANES), lambda i: (i, 0)),
                  pl.BlockSpec((tm, tm), lambda i: (0, 0))],
        out_specs=[pl.BlockSpec((tm, LANES), lambda i: (i, 0)),
                   pl.BlockSpec((1, LANES), lambda i: (0, 0))],
        out_shape=[jax.ShapeDtypeStruct((tokens, LANES), jnp.int32),
                   jax.ShapeDtypeStruct((1, LANES), jnp.int32)],
        scratch_shapes=[pltpu.VMEM((1, LANES), F32)],
        compiler_params=_cparams(("arbitrary",)),
        name="moe_rank",
    )(sel, ltri)


def _row_copy(src_ref, src_row, dst_ref, dst_row, sem):
    return pltpu.make_async_copy(src_ref.at[pl.ds(src_row, 1)], dst_ref.at[pl.ds(dst_row, 1)], sem)


def _moe_dispatch_kernel(dest_ref, tail_ref, h_ref, xs_ref, hbuf, zero_ref, lsem, ssem, zsem, *, tm):
    step = pl.program_id(0)
    n_steps = pl.num_programs(0)
    base = step * (2 * tm)
    slot = step % 3

    def load(tile, buf_slot):
        return pltpu.make_async_copy(h_ref.at[pl.ds(pl.multiple_of(tile * tm, tm), tm)], hbuf.at[buf_slot],
                                     lsem.at[buf_slot])

    def wait_scatter(buf_slot):
        for _ in range(2):
            pltpu.make_async_copy(hbuf.at[buf_slot], xs_ref.at[pl.ds(0, tm)], ssem.at[buf_slot]).wait()

    @pl.when(step == 0)
    def _():
        load(0, 0).start()

        @pl.when(n_steps > 1)
        def _():
            load(1, 1).start()

    @pl.when(step == 0)
    def _():
        zero_ref[...] = jnp.zeros_like(zero_ref)

        def fill(tail):
            return pltpu.make_async_copy(zero_ref, xs_ref.at[pl.ds(pl.multiple_of(tail, 8), zero_ref.shape[0])], zsem)

        def start(e, carry):
            @pl.when(tail_ref[e] >= 0)
            def _():
                fill(tail_ref[e]).start()
            return carry

        def wait(e, carry):
            @pl.when(tail_ref[e] >= 0)
            def _():
                fill(tail_ref[e]).wait()
            return carry

        lax.fori_loop(0, tail_ref.shape[0], start, 0)
        lax.fori_loop(0, tail_ref.shape[0], wait, 0)

    load(step, slot).wait()

    def issue(r, carry):
        for s in range(2):
            _row_copy(hbuf.at[slot], r, xs_ref, dest_ref[base + 2 * r + s], ssem.at[slot]).start()
        return carry

    lax.fori_loop(0, tm, issue, 0, unroll=DMA_ISSUE_UNROLL)

    @pl.when(step >= 1)
    def _():
        wait_scatter((step + 2) % 3)

    @pl.when(step + 2 < n_steps)
    def _():
        load(step + 2, (step + 2) % 3).start()

    @pl.when(step == n_steps - 1)
    def _():
        wait_scatter(slot)


def moe_dispatch(h, dest, tails, rows, row_tile, *, tm=256):
    tokens, d = h.shape
    tm = min(tm, tokens)
    grid_spec = pltpu.PrefetchScalarGridSpec(
        num_scalar_prefetch=2,
        grid=(tokens // tm,),
        in_specs=[pl.BlockSpec(memory_space=pl.ANY)],
        out_specs=pl.BlockSpec(memory_space=pl.ANY),
        scratch_shapes=[pltpu.VMEM((3, tm, d), h.dtype), pltpu.VMEM((row_tile, d), h.dtype),
                        pltpu.SemaphoreType.DMA((3,)), pltpu.SemaphoreType.DMA((3,)),
                        pltpu.SemaphoreType.DMA(())],
    )
    return pl.pallas_call(
        functools.partial(_moe_dispatch_kernel, tm=tm),
        grid_spec=grid_spec,
        out_shape=jax.ShapeDtypeStruct((rows, d), h.dtype),
        compiler_params=_cparams(("arbitrary",)),
        name="moe_dispatch",
    )(dest, tails, h)


def _moe_expert_kernel(te_ref, nused_ref, xs_ref, wg_ref, wu_ref, wd_ref, ys_ref, wgb, wub, wdb):
    r = pl.program_id(0)
    used = r < nused_ref[0]
    changed = (r == 0) | (te_ref[r] != te_ref[jnp.maximum(r - 1, 0)])

    @pl.when(used & changed)
    def _():
        wgb[...] = wg_ref[...].astype(BF16)
        wub[...] = wu_ref[...].astype(BF16)
        wdb[...] = wd_ref[...].astype(BF16)

    @pl.when(used)
    def _():
        x = xs_ref[...].astype(BF16)
        gate_pre = _dot(x, wgb[...])
        hid = (gate_pre * jax.nn.sigmoid(gate_pre)) * _dot(x, wub[...])
        ys_ref[...] = _dot(hid.astype(BF16), wdb[...])

    @pl.when(jnp.logical_not(used))
    def _():
        ys_ref[...] = jnp.zeros_like(ys_ref)


def moe_experts(xs, tile_expert, n_used, wg, wu, wd, layer, *, tm):
    rows, d = xs.shape
    de = wg.shape[-1]
    grid_spec = pltpu.PrefetchScalarGridSpec(
        num_scalar_prefetch=2,
        grid=(rows // tm,),
        in_specs=[pl.BlockSpec((tm, d), lambda r, te, nu: (jnp.minimum(r, nu[0] - 1), 0)),
                  pl.BlockSpec((None, None, d, de), lambda r, te, nu: (layer, te[r], 0, 0)),
                  pl.BlockSpec((None, None, d, de), lambda r, te, nu: (layer, te[r], 0, 0)),
                  pl.BlockSpec((None, None, de, d), lambda r, te, nu: (layer, te[r], 0, 0))],
        out_specs=pl.BlockSpec((tm, d), lambda r, te, nu: (r, 0)),
        scratch_shapes=[pltpu.VMEM((d, de), BF16), pltpu.VMEM((d, de), BF16), pltpu.VMEM((de, d), BF16)],
    )
    return pl.pallas_call(
        _moe_expert_kernel,
        grid_spec=grid_spec,
        out_shape=jax.ShapeDtypeStruct((rows, d), F32),
        compiler_params=_cparams(("arbitrary",)),
        name="moe_experts",
    )(tile_expert, n_used, xs, wg, wu, wd)


def _moe_combine_kernel(dest_ref, x_ref, sel_ref, nw_ref, ys_ref, o_ref, buf_ref, sem, *, tm, normalize):
    step = pl.program_id(0)
    slot = step % 2

    def gather(tile, buf_slot):
        base = tile * (2 * tm)

        def issue(r, carry):
            for s in range(2):
                _row_copy(ys_ref, dest_ref[base + 2 * r + s], buf_ref.at[buf_slot, s], r, sem.at[buf_slot]).start()
            return carry

        lax.fori_loop(0, tm, issue, 0, unroll=DMA_ISSUE_UNROLL)

    @pl.when(step == 0)
    def _():
        gather(0, 0)

    @pl.when(step + 1 < pl.num_programs(0))
    def _():
        gather(step + 1, 1 - slot)

    for s in range(2):
        pltpu.make_async_copy(ys_ref.at[pl.ds(0, tm)], buf_ref.at[slot, s], sem.at[slot]).wait()
    sel = sel_ref[...]
    out = x_ref[...] + sel[:, SEL_G1:SEL_G1 + 1] * buf_ref[slot, 0] + sel[:, SEL_G2:SEL_G2 + 1] * buf_ref[slot, 1]
    o_ref[...] = _rms(out, nw_ref[...], NORM_EPS) if normalize else out


def moe_combine(x, sel, ys, dest, final_norm=None, *, tm=256):
    tokens, d = x.shape
    tm = min(tm, tokens)
    normalize = final_norm is not None
    nw = (final_norm if normalize else jnp.ones((d,), F32)).reshape(1, d)
    grid_spec = pltpu.PrefetchScalarGridSpec(
        num_scalar_prefetch=1,
        grid=(tokens // tm,),
        in_specs=[pl.BlockSpec((tm, d), lambda i, *_: (i, 0)),
                  pl.BlockSpec((tm, LANES), lambda i, *_: (i, 0)),
                  pl.BlockSpec((1, d), lambda i, *_: (0, 0)),
                  pl.BlockSpec(memory_space=pl.ANY)],
        out_specs=pl.BlockSpec((tm, d), lambda i, *_: (i, 0)),
        scratch_shapes=[pltpu.VMEM((2, 2, tm, d), F32), pltpu.SemaphoreType.DMA((2,))],
    )
    return pl.pallas_call(
        functools.partial(_moe_combine_kernel, tm=tm, normalize=normalize),
        grid_spec=grid_spec,
        out_shape=jax.ShapeDtypeStruct((tokens, d), F32),
        compiler_params=_cparams(("arbitrary",)),
        name="moe_combine",
    )(dest, x, sel, nw, ys)


def moe_block(x, norm_w, w_router, b_router, wg, wu, wd, layer, final_norm=None, *, tm=MOE_ROW_TILE):
    tokens, d = x.shape
    n_exp = wg.shape[1]
    h, sel = moe_router(x, norm_w, w_router, b_router)
    rank, counts = moe_rank(sel)
    padded = (counts[0, :n_exp] + (tm - 1)) // tm * tm
    ends = jnp.cumsum(padded)
    starts = ends - padded
    experts = sel[:, SEL_E1:SEL_E2 + 1].astype(jnp.int32)
    start_of = jnp.sum(jnp.where(experts[..., None] == jnp.arange(n_exp, dtype=jnp.int32), starts, 0), axis=-1)
    dest = (start_of + rank[:, SEL_E1:SEL_E2 + 1]).reshape(-1)
    rows = 2 * tokens + n_exp * tm
    tile_start = jnp.arange(rows // tm, dtype=jnp.int32) * tm
    tile_expert = jnp.minimum(jnp.sum(tile_start[:, None] >= ends[None, :], axis=1), n_exp - 1).astype(jnp.int32)
    n_used = (ends[-1] // tm).astype(jnp.int32).reshape(1)
    tails = jnp.concatenate([jnp.where(padded > 0, ends - tm, -1),
                             jnp.where(tile_start >= ends[-1], tile_start, -1)]).astype(jnp.int32)
    xs = moe_dispatch(h, dest, tails, rows, tm)
    ys = moe_experts(xs, tile_expert, n_used, wg, wu, wd, layer, tm=tm)
    return moe_combine(x, sel, ys, dest, final_norm)


def _proj_weights(w_in_l, w_vres_l):
    d = w_in_l.shape[0]
    mla0 = RW_COLS + DF_COLS
    vres = jnp.zeros((d, RW_V_RANK), F32) if w_vres_l is None else w_vres_l
    part_f = [w_in_l[:, 3 * RW_DIM:RW_COLS],
              vres, jnp.zeros((d, LANES - RW_V_RANK), F32),
              w_in_l[:, mla0 + ML_Q_RANK + ML_KV_RANK:mla0 + ML_COLS], jnp.zeros((d, LANES - ML_ROPE), F32)]
    part_a = [w_in_l[:, :3 * RW_DIM],
              w_in_l[:, RW_COLS:RW_COLS + DF_COLS],
              w_in_l[:, mla0 + ML_Q_RANK:mla0 + ML_Q_RANK + ML_KV_RANK],
              jnp.zeros((d, OFF_MQ - OFF_MKV - ML_KV_RANK), F32),
              w_in_l[:, mla0:mla0 + ML_Q_RANK]]
    return jnp.concatenate(part_f, axis=1).astype(BF16), jnp.concatenate(part_a, axis=1).astype(BF16)


def _pad_rows(w, rows, at=0):
    out = jnp.zeros((rows, w.shape[1]), w.dtype)
    return lax.dynamic_update_slice(out, w, (at, 0))


def kernel(x, mem, positions, rel_bias, final_norm, norm_mix, w_in, w_in_vres, w_out, tm_mu, tm_mu_vres, tm_w0, tm_w2, tm_a0, tm_a2, tm_v0, tm_v2, tm_g2, tm_k_k, tm_k_a, tm_r_k, tm_ln_w, tm_ln_b, da_lq1, da_lk1, da_lq2, da_lk2, da_subln, mla_q_norm, mla_wq_b, mla_kv_norm, mla_wkv_b, norm_cross, norm_mem, ca_wq, ca_wkv, ca_wo, norm_ffn, moe_w_group, moe_b_group, moe_w_expert, moe_b_expert, moe_w_gate, moe_w_up, moe_w_down):
    batch, seq, d = x.shape
    tokens = batch * seq
    depth = norm_mix.shape[0]
    xf = x.reshape(tokens, d)
    memf = mem.reshape(-1, d)
    positions = positions.astype(jnp.int32)

    head_of_lane = jnp.arange(RW_DIM) // RW_HEAD_DIM
    seg = (head_of_lane[:, None] == jnp.arange(LANES)[None, :]).astype(BF16)
    seg_t = seg.T
    row = lambda v: v.reshape(1, -1)

    v_first = None
    for l in range(depth):
        w_f, w_a = _proj_weights(w_in[l], None if l == 0 else w_in_vres[l - 1])
        pa, proj = norm_matmul(xf, norm_mix[l], w_a, w_f, out_dtype=BF16, tm=1024, tn=PROJ_A_COLS // 3)

        mu = tm_mu[l]
        prm = dict(mu_r=row(mu[:RW_DIM]), mu_k=row(mu[RW_DIM:2 * RW_DIM]), mu_v=row(mu[2 * RW_DIM:3 * RW_DIM]),
                   mu_l=row(mu[3 * RW_DIM:]), w0=row(tm_w0[l]), a0=row(tm_a0[l]),
                   w2=_pad_rows(tm_w2[l], LANES, 0), a2=_pad_rows(tm_a2[l], LANES, RW_W_RANK),
                   g2=tm_g2[l].astype(BF16), k_k=row(tm_k_k[l]), k_a=row(tm_k_a[l]), r_k=row(tm_r_k[l]),
                   seg=seg, seg_t=seg_t)
        if l > 0:
            prm.update(mu_vr=jnp.pad(row(tm_mu_vres[l - 1]), ((0, 0), (0, LANES - RW_V_RANK))),
                       v0=row(tm_v0[l - 1]), v2=_pad_rows(tm_v2[l - 1], LANES, 0))
        r, lw, k, v, kap, beta, gate, bonus = rwkv_prep(pa, proj, batch, v_first, prm)
        if l == 0:
            v_first = v
        o = rwkv_scan(r, lw, k, v, kap, beta, batch)
        y_a = rwkv_post(o, bonus, gate, tm_ln_w[l], tm_ln_b[l], seg, seg_t)

        lambda_init = 0.8 - 0.6 * math.exp(-0.3 * l)
        lam = (jnp.exp(jnp.sum(da_lq1[l] * da_lk1[l])) - jnp.exp(jnp.sum(da_lq2[l] * da_lk2[l])) + lambda_init)
        y_b = diff_attention(pa, positions, rel_bias, lam, lambda_init, da_subln[l])

        wq = mla_wq_b[l].reshape(ML_Q_RANK, ML_HEADS, ML_NOPE + ML_ROPE)
        wq_pe = jnp.pad(wq[:, :, ML_NOPE:], ((0, 0), (0, 0), (0, LANES - ML_ROPE)))
        wq_all = jnp.concatenate([wq[:, :, :ML_NOPE].reshape(ML_Q_RANK, -1),
                                  wq_pe.reshape(ML_Q_RANK, -1)], axis=1).astype(BF16)
        wkv = mla_wkv_b[l].reshape(ML_KV_RANK, ML_HEADS, ML_NOPE + ML_V)
        wkv = jnp.concatenate([wkv[:, :, :ML_NOPE].reshape(ML_KV_RANK, -1),
                               wkv[:, :, ML_NOPE:].reshape(ML_KV_RANK, -1)], axis=1).astype(BF16)
        qf, kf, v_mla = mla_prep(pa, proj, positions, mla_q_norm[l], mla_kv_norm[l], wq_all, wkv)
        y_c = mla_attention(qf, kf, v_mla, batch)

        wo = w_out[l].astype(BF16)
        xf = matmul_res([y_a, y_b, y_c],
                        [wo[:RW_DIM], wo[RW_DIM:RW_DIM + DF_DIM], wo[RW_DIM + DF_DIM:]], xf)

        kv_mem = norm_matmul(memf, norm_mem[l], ca_wkv[l].astype(BF16), out_dtype=BF16)
        xf = cross_block(xf, batch, norm_cross[l], ca_wq[l].astype(BF16), kv_mem, ca_wo[l].astype(BF16))

        w_router = jnp.concatenate(
            [moe_w_expert[l], moe_w_group[l], jnp.zeros((d, LANES - MOE_EXPERTS - MOE_GROUPS), F32)], axis=1)
        b_router = jnp.concatenate(
            [moe_b_expert[l], moe_b_group[l], jnp.zeros((LANES - MOE_EXPERTS - MOE_GROUPS,), F32)]).reshape(1, LANES)
        xf = moe_block(xf, norm_ffn[l], w_router, b_router, moe_w_gate, moe_w_up, moe_w_down, l,
                       final_norm if l == depth - 1 else None)

    return xf.reshape(batch, seq, d)
```

```python
import functools
import math

import jax
import jax.numpy as jnp
from jax import lax
from jax.experimental import pallas as pl
from jax.experimental.pallas import tpu as pltpu

F32 = jnp.float32
BF16 = jnp.bfloat16

NORM_EPS = 1e-6
ROPE_THETA = 10000.0

RW_HEADS = 16
RW_HEAD_DIM = 64
RW_DIM = RW_HEADS * RW_HEAD_DIM
RW_W_RANK = 64
RW_A_RANK = 64
RW_G_RANK = 128
RW_V_RANK = 32
RW_LORA = RW_W_RANK + RW_A_RANK + RW_G_RANK
RW_LN_EPS = 64e-5
RW_COLS = 3 * RW_DIM + RW_LORA

DF_HEADS = 4
DF_HEAD_DIM = 64
DF_V_DIM = 2 * DF_HEAD_DIM
DF_QK = DF_HEADS * 2 * DF_HEAD_DIM
DF_DIM = DF_HEADS * DF_V_DIM
DF_COLS = 2 * DF_QK + DF_DIM
DF_SUBLN_EPS = 1e-5

ML_HEADS = 4
ML_Q_RANK = 384
ML_KV_RANK = 256
ML_NOPE = 128
ML_ROPE = 64
ML_V = 128
ML_DIM = ML_HEADS * ML_V
ML_COLS = ML_Q_RANK + ML_KV_RANK + ML_ROPE

REL_BUCKETS = 32
REL_MAX_DIST = 128

CA_HEADS = 4
CA_HEAD_DIM = 128
CA_DIM = CA_HEADS * CA_HEAD_DIM

MOE_GROUPS = 4
MOE_PER_GROUP = 8
MOE_EXPERTS = MOE_GROUPS * MOE_PER_GROUP

LANES = 128
SCAN_CHUNK = 64
SCAN_GROUP = 4
SCAN_BATCHES = 2
ATTN_TILE = 512
POST_ROW_BLOCKS = 4
DMA_ISSUE_UNROLL = 8
MOE_ROW_TILE = 256
VMEM_LIMIT = 56 * 1024 * 1024
NEG_BIG = -1e30

LOG2E = 1.4426950408889634

OFF_LORA = 0
OFF_VRES = OFF_LORA + RW_LORA
OFF_KPE = OFF_VRES + LANES
PROJ_F_COLS = OFF_KPE + LANES
OFF_R = 0
OFF_K = RW_DIM
OFF_V = 2 * RW_DIM
OFF_DQ = 3 * RW_DIM
OFF_DK = OFF_DQ + DF_QK
OFF_DV = OFF_DK + DF_QK
OFF_MKV = OFF_DV + DF_DIM
OFF_MQ = 13 * ML_Q_RANK
PROJ_A_COLS = OFF_MQ + ML_Q_RANK


def _cparams(sem, vmem=VMEM_LIMIT, flags=None):
    return pltpu.CompilerParams(dimension_semantics=sem, vmem_limit_bytes=vmem, flags=flags)


def _dot(a, b):
    return jnp.dot(a, b, preferred_element_type=F32)


def _dot_t(a, b):
    return lax.dot_general(a, b, (((1,), (1,)), ((), ())), preferred_element_type=F32)


def _split3(x):
    hi = x.astype(BF16)
    r1 = x - hi.astype(F32)
    mid = r1.astype(BF16)
    lo = (r1 - mid.astype(F32)).astype(BF16)
    return hi, mid, lo


def _dot_rhs01(x, ones_bf16):
    hi = x.astype(BF16)
    lo = (x - hi.astype(F32)).astype(BF16)
    return _dot(hi, ones_bf16) + _dot(lo, ones_bf16)


def _dot_x3(a, b):
    ah = a.astype(BF16)
    al = (a - ah.astype(F32)).astype(BF16)
    bh = b.astype(BF16)
    bl = (b - bh.astype(F32)).astype(BF16)
    return _dot(ah, bh) + _dot(ah, bl) + _dot(al, bh)


def _rms(x, w, eps):
    ms = jnp.mean(x * x, axis=-1, keepdims=True)
    return x * lax.rsqrt(ms + eps) * w


def _norm_matmul_kernel(*refs, eps, has_side):
    if has_side:
        x_ref, nw_ref, w_ref, ws_ref, o_ref, os_ref, xn_ref = refs
    else:
        x_ref, nw_ref, w_ref, o_ref, xn_ref = refs

    @pl.when(pl.program_id(1) == 0)
    def _():
        xn_ref[...] = _rms(x_ref[...], nw_ref[...], eps).astype(BF16)
        if has_side:
            os_ref[...] = _dot(xn_ref[...], ws_ref[...])

    o_ref[...] = _dot(xn_ref[...], w_ref[...]).astype(o_ref.dtype)


def norm_matmul(x, nw, w, w_side=None, *, out_dtype=F32, tm=512, tn=None, eps=NORM_EPS):
    m, d = x.shape
    n = w.shape[1]
    tm = min(tm, m)
    tn = n if tn is None else tn
    has_side = w_side is not None
    in_specs = [pl.BlockSpec((tm, d), lambda i, j: (i, 0)),
                pl.BlockSpec((1, d), lambda i, j: (0, 0)),
                pl.BlockSpec((d, tn), lambda i, j: (0, j))]
    out_specs = [pl.BlockSpec((tm, tn), lambda i, j: (i, j))]
    out_shape = [jax.ShapeDtypeStruct((m, n), out_dtype)]
    args = [x, nw.reshape(1, d), w]
    if has_side:
        ns = w_side.shape[1]
        in_specs.append(pl.BlockSpec((d, ns), lambda i, j: (0, 0)))
        out_specs.append(pl.BlockSpec((tm, ns), lambda i, j: (i, 0)))
        out_shape.append(jax.ShapeDtypeStruct((m, ns), F32))
        args.append(w_side)
    outs = pl.pallas_call(
        functools.partial(_norm_matmul_kernel, eps=eps, has_side=has_side),
        grid=(m // tm, n // tn),
        in_specs=in_specs,
        out_specs=out_specs,
        out_shape=out_shape,
        scratch_shapes=[pltpu.VMEM((tm, d), BF16)],
        compiler_params=_cparams(("parallel", "arbitrary")),
        name="norm_matmul",
    )(*args)
    return outs if has_side else outs[0]


def _matmul_res_kernel(*refs, n_a):
    a_refs, w_refs = refs[:n_a], refs[n_a:2 * n_a]
    res_ref, o_ref = refs[2 * n_a], refs[2 * n_a + 1]
    acc = res_ref[...]
    for a_ref, w_ref in zip(a_refs, w_refs):
        acc = acc + _dot(a_ref[...].astype(BF16), w_ref[...])
    o_ref[...] = acc


def matmul_res(a_list, w_list, res, *, tm=512, tn=2048):
    m, n = res.shape
    tm = min(tm, m)
    tn = min(tn, n)
    n_a = len(a_list)
    in_specs = ([pl.BlockSpec((tm, a.shape[1]), lambda i, j: (i, 0)) for a in a_list]
                + [pl.BlockSpec((w.shape[0], tn), lambda i, j: (0, j)) for w in w_list]
                + [pl.BlockSpec((tm, tn), lambda i, j: (i, j))])
    return pl.pallas_call(
        functools.partial(_matmul_res_kernel, n_a=n_a),
        grid=(m // tm, n // tn),
        in_specs=in_specs,
        out_specs=pl.BlockSpec((tm, tn), lambda i, j: (i, j)),
        out_shape=jax.ShapeDtypeStruct((m, n), F32),
        compiler_params=_cparams(("parallel", "arbitrary")),
        name="matmul_res",
    )(*a_list, *w_list, res)


def _softplus(z):
    return jnp.maximum(z, 0.0) + jnp.log(1.0 + jnp.exp(-jnp.abs(z)))


def _rwkv_prep_kernel(*refs, has_vres):
    if has_vres:
        (pr_ref, pk_ref, pv_ref, pl_ref, pvr_ref, vfirst_ref,
         mu_r, mu_k, mu_v, mu_l, mu_vr, w0, w2, a0, a2, g2, v0, v2,
         k_k, k_a, r_k, seg, seg_t,
         r_o, lw_o, k_o, v_o, kap_o, beta_o, g_o, bonus_o,
         last_r, last_k, last_v, last_l, last_vr) = refs
    else:
        (pr_ref, pk_ref, pv_ref, pl_ref,
         mu_r, mu_k, mu_v, mu_l, w0, w2, a0, a2, g2,
         k_k, k_a, r_k, seg, seg_t,
         r_o, lw_o, k_o, v_o, kap_o, beta_o, g_o, bonus_o,
         last_r, last_k, last_v, last_l) = refs
    t = pl.program_id(1)

    def shifted(p_ref, last_ref, mu_ref):
        p = p_ref[...].astype(F32)
        n = p.shape[0]
        carried = jnp.where(t == 0, 0.0, last_ref[0:1, :])
        row = lax.broadcasted_iota(jnp.int32, p.shape, 0)
        prev = jnp.where(row == 0, carried, pltpu.roll(p, 1, axis=0))
        last_ref[0:1, :] = p[n - 1:n, :]
        return p + mu_ref[...] * (prev - p)

    r = shifted(pr_ref, last_r, mu_r)
    k = shifted(pk_ref, last_k, mu_k)
    v = shifted(pv_ref, last_v, mu_v)
    lora = shifted(pl_ref, last_l, mu_l)
    wl = lora[:, :LANES]
    gl = lora[:, LANES:]

    lane = lax.broadcasted_iota(jnp.int32, wl.shape, 1)
    wl_t = jnp.where(lane < RW_W_RANK, jnp.tanh(wl), 0.0)
    al = jnp.where(lane >= RW_W_RANK, wl, 0.0)
    w_log = -_softplus(-(w0[...] + _dot_x3(wl_t, w2[...]))) - 0.5
    lw_o[...] = -jnp.exp(w_log)
    a = jax.nn.sigmoid(a0[...] + _dot_x3(al, a2[...]))
    g_o[...] = _dot(jax.nn.sigmoid(gl).astype(BF16), g2[...]).astype(g_o.dtype)

    segm, segm_t = seg[...], seg_t[...]

    def head_sum(x):
        return _dot_rhs01(_dot_rhs01(x, segm), segm_t)

    kk = k * k_k[...]
    kk = kk * lax.rsqrt(jnp.maximum(head_sum(kk * kk), 1e-24))
    k = k * (1.0 + (a - 1.0) * k_a[...])
    if has_vres:
        vr = shifted(pvr_ref, last_vr, mu_vr)
        mix = jax.nn.sigmoid(v0[...] + _dot_x3(vr, v2[...]))
        v = v + (vfirst_ref[...] - v) * mix
    r_o[...] = r.astype(r_o.dtype)
    k_o[...] = k.astype(k_o.dtype)
    v_o[...] = v.astype(v_o.dtype)
    kap_o[...] = kk.astype(kap_o.dtype)
    beta_o[...] = (kk * a).astype(beta_o.dtype)
    bonus_o[...] = (head_sum(r * k * r_k[...]) * v).astype(bonus_o.dtype)


def rwkv_prep(pa, proj, batch, vfirst, prm, *, tt=512):
    tokens = proj.shape[0]
    seq = tokens // batch
    tt = min(tt, seq)
    nt = seq // tt
    has_vres = vfirst is not None
    d = RW_DIM

    def rows(width, col):
        return pl.BlockSpec((tt, width), lambda b, t, col=col: (b * nt + t, col))

    def full(shape):
        return pl.BlockSpec(shape, lambda b, t: (0, 0))

    in_specs = [rows(d, OFF_R // d), rows(d, OFF_K // d), rows(d, OFF_V // d),
                rows(RW_LORA, OFF_LORA // RW_LORA)]
    args = [pa, pa, pa, proj]
    if has_vres:
        in_specs += [rows(LANES, OFF_VRES // LANES), rows(d, 0)]
        args += [proj, vfirst]
    names = ["mu_r", "mu_k", "mu_v", "mu_l"] + (["mu_vr"] if has_vres else []) + ["w0", "w2", "a0", "a2", "g2"]
    names += (["v0", "v2"] if has_vres else []) + ["k_k", "k_a", "r_k", "seg", "seg_t"]
    for nm in names:
        in_specs.append(full(prm[nm].shape))
        args.append(prm[nm])
    out_spec = pl.BlockSpec((tt, d), lambda b, t: (b * nt + t, 0))
    scratch = [pltpu.VMEM((8, d), F32)] * 3 + [pltpu.VMEM((8, RW_LORA), F32)]
    if has_vres:
        scratch.append(pltpu.VMEM((8, LANES), F32))
    return pl.pallas_call(
        functools.partial(_rwkv_prep_kernel, has_vres=has_vres),
        grid=(batch, nt),
        in_specs=in_specs,
        out_specs=[out_spec] * 8,
        out_shape=[jax.ShapeDtypeStruct((tokens, d), F32 if i == 1 else BF16) for i in range(8)],
        scratch_shapes=scratch,
        compiler_params=_cparams(("arbitrary", "arbitrary")),
        name="rwkv_prep",
    )(*args)


def _rwkv_scan_kernel(r_ref, lw_ref, k_ref, v_ref, kap_ref, beta_ref, tril_ref, bmask_ref,
                      o_ref, ht_ref):
    @pl.when(pl.program_id(1) == 0)
    def _():
        ht_ref[...] = jnp.zeros_like(ht_ref)

    n_batch, c, d = lw_ref.shape
    w = ht_ref.shape[1]
    g = w // RW_HEAD_DIM
    bmask = bmask_ref[...]
    bmask_b = bmask.astype(BF16)
    tril3 = tril_ref[...]
    t_idx = lax.broadcasted_iota(jnp.int32, (c, w), 0)
    s_idx = lax.broadcasted_iota(jnp.int32, (c, w), 1) % c
    strict = t_idx > s_idx
    incl = t_idx >= s_idx
    n_sq = int(math.log2(c))

    def stack(x):
        return jnp.concatenate([x.astype(BF16)] * g, axis=0) * bmask_b

    sls = [(bi, slice(None), slice(lo, lo + w)) for bi in range(n_batch) for lo in range(0, d, w)]
    groups = range(len(sls))
    lw = [lw_ref[sl] for sl in sls]
    cum = [_dot(tril3, jnp.concatenate(_split3(x), axis=0)) for x in lw]
    total = [x[c - 1:c, :] for x in cum]
    ar = [jnp.concatenate([-kap_ref[sls[gi]] * jnp.exp(cum[gi] - lw[gi]), r_ref[sls[gi]] * jnp.exp(cum[gi])],
                          axis=0).astype(BF16) for gi in groups]
    p_inv = [jnp.exp(-x) for x in cum]
    b_s = [stack(beta_ref[sls[gi]] * p_inv[gi]) for gi in groups]
    k_s = [stack(k_ref[sls[gi]] * p_inv[gi]) for gi in groups]
    v_n = [v_ref[sl] for sl in sls]
    v_s = [stack(x) for x in v_n]

    arb = [_dot_t(ar[gi], b_s[gi]) for gi in groups]
    ark = [_dot_t(ar[gi], k_s[gi]) for gi in groups]
    ab = [jnp.where(strict, m[:c], 0.0) for m in arb]
    rb = [jnp.where(incl, m[c:], 0.0).astype(BF16) for m in arb]
    akrk = [jnp.concatenate([jnp.where(strict, m[:c], 0.0), jnp.where(incl, m[c:], 0.0)], axis=0).astype(BF16)
            for m in ark]

    ht = [ht_ref[gi] for gi in groups]
    base = [_dot_t(ar[gi], ht[gi].astype(BF16)) + _dot(akrk[gi], v_s[gi]) for gi in groups]
    x = [m[:c] for m in base]
    lp = ab
    for i in range(n_sq):
        lpb = [m.astype(BF16) for m in lp]
        x = [x[gi] + _dot(lpb[gi], stack(x[gi])) for gi in groups]
        if i < n_sq - 1:
            lp = [_dot(lpb[gi], stack(lp[gi])) for gi in groups]
    for gi in groups:
        o_ref[sls[gi]] = base[gi][c:] + _dot(rb[gi], stack(x[gi]))

    for gi in groups:
        p_rem = jnp.exp(total[gi] - cum[gi])
        z = jnp.concatenate([beta_ref[sls[gi]] * p_rem, k_ref[sls[gi]] * p_rem], axis=0).astype(BF16)
        uv_t = jnp.concatenate([x[gi], v_n[gi].astype(F32)], axis=0).T.astype(BF16)
        ht_ref[gi] = ht[gi] * jnp.exp(total[gi]) + bmask * _dot(uv_t, z)


def rwkv_scan(r, lw, k, v, kap, beta, batch):
    tokens, d = r.shape
    seq = tokens // batch
    c = min(SCAN_CHUNK, seq)
    nc = seq // c
    gw = SCAN_GROUP * RW_HEAD_DIM
    rr = SCAN_GROUP * c
    assert c == RW_HEAD_DIM, "the stacking mask doubles as the head-block mask of the state"
    bb = math.gcd(batch, SCAN_BATCHES)
    tril = jnp.tile((jnp.arange(c)[:, None] >= jnp.arange(c)[None, :]).astype(BF16), (1, 3))
    bmask = (jnp.arange(rr)[:, None] // c == jnp.arange(gw)[None, :] // RW_HEAD_DIM).astype(F32)
    blk = pl.BlockSpec((bb, c, d), lambda b, i: (b, i, 0))
    as3d = lambda a: a.reshape(batch, seq, d)
    out = pl.pallas_call(
        _rwkv_scan_kernel,
        grid=(batch // bb, nc),
        in_specs=[blk] * 6 + [pl.BlockSpec((c, 3 * c), lambda b, i: (0, 0)),
                              pl.BlockSpec((rr, gw), lambda b, i: (0, 0))],
        out_specs=blk,
        out_shape=jax.ShapeDtypeStruct((batch, seq, d), F32),
        scratch_shapes=[pltpu.VMEM((bb * (d // gw), gw, gw), F32)],
        compiler_params=_cparams(("arbitrary", "arbitrary")),
        name="rwkv_scan",
    )(as3d(r), as3d(lw), as3d(k), as3d(v), as3d(kap), as3d(beta), tril, bmask)
    return out.reshape(tokens, d)


def _rwkv_post_kernel(o_ref, bonus_ref, g_ref, lnw_ref, lnb_ref, seg, seg_t, y_ref):
    segm, segm_t = seg[...], seg_t[...]

    def head_means(xs):
        sums = [_dot_rhs01(x, segm) for x in xs]
        return [_dot_rhs01(s, segm_t) * (1.0 / RW_HEAD_DIM) for s in sums]

    rows = o_ref.shape[0] // POST_ROW_BLOCKS
    blocks = [slice(i * rows, (i + 1) * rows) for i in range(POST_ROW_BLOCKS)]
    o = [o_ref[b, :] for b in blocks]
    dlt = [x - m for x, m in zip(o, head_means(o))]
    var = head_means([x * x for x in dlt])
    for b, x, v in zip(blocks, dlt, var):
        y = x * lax.rsqrt(v + RW_LN_EPS) * lnw_ref[...] + lnb_ref[...]
        y_ref[b, :] = ((y + bonus_ref[b, :]) * g_ref[b, :]).astype(y_ref.dtype)


def rwkv_post(o, bonus, g, ln_w, ln_b, seg, seg_t, *, tm=512):
    tokens, d = o.shape
    tm = min(tm, tokens)
    blk = pl.BlockSpec((tm, d), lambda i: (i, 0))
    vec = pl.BlockSpec((1, d), lambda i: (0, 0))
    return pl.pallas_call(
        _rwkv_post_kernel,
        grid=(tokens // tm,),
        in_specs=[blk, blk, blk, vec, vec,
                  pl.BlockSpec(seg.shape, lambda i: (0, 0)), pl.BlockSpec(seg_t.shape, lambda i: (0, 0))],
        out_specs=blk,
        out_shape=jax.ShapeDtypeStruct((tokens, d), BF16),
        compiler_params=_cparams(("parallel",)),
        name="rwkv_post",
    )(o, bonus, g, ln_w.reshape(1, d), ln_b.reshape(1, d), seg, seg_t)


def _t5_thresholds():
    max_exact = REL_BUCKETS // 2
    thr = list(range(1, max_exact))
    n = max_exact
    for bucket in range(max_exact, REL_BUCKETS):
        while True:
            large = max_exact + int(math.log(max(n, max_exact) / max_exact)
                                    / math.log(REL_MAX_DIST / max_exact) * (REL_BUCKETS - max_exact))
            if min(large, REL_BUCKETS - 1) >= bucket:
                break
            n += 1
        thr.append(n)
    return thr


T5_THRESHOLDS = _t5_thresholds()
T5_FAR = T5_THRESHOLDS[-1]


def _softmax_tiles(s_list, c_list, states, vt_list):
    stats = []
    for s_t, c, (m_old, l_old, _) in zip(s_list, c_list, states):
        m_new = jnp.maximum(m_old, jnp.max(s_t, axis=0, keepdims=True) + c)
        alpha = jnp.exp2(m_old - m_new)
        p_t = jnp.exp2(s_t - (m_new - c))
        stats.append((m_new, alpha, alpha * l_old + jnp.sum(p_t, axis=0, keepdims=True), p_t.astype(BF16)))
    return tuple((m_new, l_new, alpha * acc + _dot(vt, p_t))
                 for (m_new, alpha, l_new, p_t), (_, _, acc), vt in zip(stats, states, vt_list))


def _transpose_into(vt_ref, v_ref, chunk):
    seq = v_ref.shape[0]
    for c in range(seq // chunk):
        vt_ref[:, c * chunk:(c + 1) * chunk] = v_ref[c * chunk:(c + 1) * chunk, :].astype(F32).T.astype(BF16)


def _diff_attn_kernel(qfirst_ref, klast_ref, q_ref, k_ref, v_ref, qpos_ref, kpos_ref, subln_ref, table_ref, lam_ref,
                      o_ref, vt_ref, *, tq, tk, scale2, out_scale):
    b, i = pl.program_id(0), pl.program_id(1)
    nq = pl.num_programs(1)
    seq = k_ref.shape[0]
    nk = seq // tk
    w = DF_V_DIM

    @pl.when(i == 0)
    def _():
        _transpose_into(vt_ref, v_ref, tk)

    n_tiles = (i * tq + tq - 1) // tk + 1
    qf = qfirst_ref[b * nq + i]
    n_far = lax.while_loop(
        lambda j: (j * tk + tk - 1 <= i * tq) & (qf - klast_ref[b * nk + jnp.minimum(j, nk - 1)] >= T5_FAR),
        lambda j: j + 1, jnp.int32(0))

    dist = lax.broadcasted_iota(jnp.int32, (1, LANES), 1)
    qpos = qpos_ref[...]
    q_idx = i * tq + lax.broadcasted_iota(jnp.int32, (tk, tq), 1)
    k_off = lax.broadcasted_iota(jnp.int32, (tk, tq), 0)
    lane = lax.broadcasted_iota(jnp.int32, (tq, w), 1)

    bias_rows, c_far, qm = [], [], []
    for h in range(DF_HEADS):
        bias_vec = jnp.full((1, LANES), table_ref[h], F32)
        for bucket, thr in enumerate(T5_THRESHOLDS, start=1):
            bias_vec = jnp.where(dist >= thr, table_ref[bucket * DF_HEADS + h], bias_vec)
        bias_rows.append(jnp.broadcast_to(bias_vec * LOG2E, (tk, LANES)))
        c_far.append(table_ref[(REL_BUCKETS - 1) * DF_HEADS + h] * LOG2E)
        qh = q_ref[:, h * w:(h + 1) * w].astype(F32) * scale2
        qm.append([jnp.where((lane >= mi * DF_HEAD_DIM) & (lane < (mi + 1) * DF_HEAD_DIM), qh, 0.0).astype(BF16)
                   for mi in range(2)])

    def tiles(j, h):
        off = pl.multiple_of(j * tk, tk)
        return k_ref[pl.ds(off, tk), h * w:(h + 1) * w], vt_ref[h * w:(h + 1) * w, pl.ds(off, tk)], off

    chains = [(h, mi) for h in range(DF_HEADS) for mi in range(2)]

    def far_body(j, st):
        kv = [tiles(j, h) for h in range(DF_HEADS)]
        s = [_dot_t(kv[h][0], qm[h][mi]) for h, mi in chains]
        return _softmax_tiles(s, [c_far[h] for h, _ in chains], st, [kv[h][1] for h, _ in chains])

    def near_body(j, st):
        off = pl.multiple_of(j * tk, tk)
        n = jnp.clip(qpos - kpos_ref[pl.ds(off, tk), :], 0, LANES - 1)
        keep = q_idx >= off + k_off
        kv = [tiles(j, h) for h in range(DF_HEADS)]
        bias = [jnp.concatenate(
            [jnp.take_along_axis(bias_rows[h], n[:, cb * LANES:(cb + 1) * LANES], axis=1)
             for cb in range(tq // LANES)], axis=1) for h in range(DF_HEADS)]
        s = [jnp.where(keep, _dot_t(kv[h][0], qm[h][mi]) + bias[h], NEG_BIG) for h, mi in chains]
        return _softmax_tiles(s, [0.0] * len(chains), st, [kv[h][1] for h, _ in chains])

    init = tuple((jnp.full((1, tq), NEG_BIG, F32), jnp.zeros((1, tq), F32), jnp.zeros((w, tq), F32))
                 for _ in range(2 * DF_HEADS))
    st = lax.fori_loop(0, n_far, far_body, init)
    st = lax.fori_loop(n_far, n_tiles, near_body, st)
    for h in range(DF_HEADS):
        s0, s1 = st[2 * h], st[2 * h + 1]
        d_t = s0[2] / s0[1] - lam_ref[0] * (s1[2] / s1[1])
        ms = jnp.mean(d_t * d_t, axis=0, keepdims=True)
        y_t = d_t * lax.rsqrt(ms + DF_SUBLN_EPS) * (subln_ref[...] * out_scale)
        o_ref[:, h * w:(h + 1) * w] = y_t.T.astype(o_ref.dtype)


def diff_attention(pa, positions, rel_bias, lam, lambda_init, subln_w, *, tq=ATTN_TILE):
    batch, seq = positions.shape
    tokens = batch * seq
    tq = min(tq, seq)
    tk = tq
    nq, nk = seq // tq, seq // tk
    qfirst = positions[:, ::tq].reshape(-1)
    klast = positions[:, tk - 1::tk].reshape(-1)
    qpos = positions.reshape(batch, 1, seq)
    kpos = positions.reshape(batch, seq, 1)
    wd = DF_DIM
    grid_spec = pltpu.PrefetchScalarGridSpec(
        num_scalar_prefetch=2,
        grid=(batch, nq),
        in_specs=[pl.BlockSpec((tq, wd), lambda b, i, *_: (b * nq + i, OFF_DQ // wd)),
                  pl.BlockSpec((seq, wd), lambda b, i, *_: (b, OFF_DK // wd)),
                  pl.BlockSpec((seq, wd), lambda b, i, *_: (b, OFF_DV // wd)),
                  pl.BlockSpec((None, 1, tq), lambda b, i, *_: (b, 0, i)),
                  pl.BlockSpec((None, seq, 1), lambda b, i, *_: (b, 0, 0)),
                  pl.BlockSpec((DF_V_DIM, 1), lambda b, i, *_: (0, 0)),
                  pl.BlockSpec(memory_space=pltpu.SMEM),
                  pl.BlockSpec(memory_space=pltpu.SMEM)],
        out_specs=pl.BlockSpec((tq, wd), lambda b, i, *_: (b * nq + i, 0)),
        scratch_shapes=[pltpu.VMEM((wd, seq), BF16)],
    )
    return pl.pallas_call(
        functools.partial(_diff_attn_kernel, tq=tq, tk=tk, scale2=DF_HEAD_DIM ** -0.5 * LOG2E,
                          out_scale=1.0 - lambda_init),
        grid_spec=grid_spec,
        out_shape=jax.ShapeDtypeStruct((tokens, DF_DIM), BF16),
        compiler_params=_cparams(("arbitrary", "arbitrary")),
        name="diff_attention",
    )(qfirst, klast, pa, pa, pa, qpos, kpos, subln_w.reshape(DF_V_DIM, 1), rel_bias.reshape(-1), lam.reshape(1))


ML_QK_PAD = 2 * LANES


def _rope_block(x, cos, sin):
    half = ML_ROPE // 2
    lane = lax.broadcasted_iota(jnp.int32, x.shape, 1)
    rot = jnp.where(lane < half, -pltpu.roll(x, LANES - half, axis=1),
                    jnp.where(lane < ML_ROPE, pltpu.roll(x, half, axis=1), 0.0))
    return x * cos + rot * sin


def _mla_prep_kernel(mq_ref, mkv_ref, kpe_ref, pos_ref, qn_w, kvn_w, wq_ref, wkv_ref, freq_ref,
                     qf_o, kf_o, v_o, *, qscale):
    ang = pos_ref[...].astype(F32) * freq_ref[...]
    cos, sin = jnp.cos(ang), jnp.sin(ang)
    qc = _rms(mq_ref[...].astype(F32), qn_w[...], NORM_EPS).astype(BF16)
    q_all = _dot(qc, wq_ref[...]) * qscale
    kvc = _rms(mkv_ref[...].astype(F32), kvn_w[...], NORM_EPS).astype(BF16)
    kvb = _dot(kvc, wkv_ref[...])
    kpe = _rope_block(kpe_ref[...], cos, sin).astype(BF16)
    nope_w = ML_HEADS * ML_NOPE
    for h in range(ML_HEADS):
        lo = h * ML_QK_PAD
        qf_o[:, lo:lo + LANES] = q_all[:, h * LANES:(h + 1) * LANES].astype(BF16)
        qf_o[:, lo + LANES:lo + 2 * LANES] = _rope_block(
            q_all[:, nope_w + h * LANES:nope_w + (h + 1) * LANES], cos, sin).astype(BF16)
        kf_o[:, lo:lo + LANES] = kvb[:, h * LANES:(h + 1) * LANES].astype(BF16)
        kf_o[:, lo + LANES:lo + 2 * LANES] = kpe
    v_o[...] = kvb[:, nope_w:].astype(BF16)


def mla_prep(pa, pf, positions, q_norm, kv_norm, wq_all, wkv, *, tm=512):
    tokens = pa.shape[0]
    tm = min(tm, tokens)
    half = ML_ROPE // 2
    inv_freq = ROPE_THETA ** (-jnp.arange(half, dtype=F32) / half)
    freq = jnp.concatenate([inv_freq, inv_freq, jnp.zeros((LANES - ML_ROPE,), F32)]).reshape(1, LANES)

    def full(a):
        return pl.BlockSpec(a.shape, lambda i: (0, 0))

    qn_w = q_norm.reshape(1, -1)
    kvn_w = kv_norm.reshape(1, -1)
    wide = ML_HEADS * ML_QK_PAD
    return pl.pallas_call(
        functools.partial(_mla_prep_kernel, qscale=(ML_NOPE + ML_ROPE) ** -0.5 * LOG2E),
        grid=(tokens // tm,),
        in_specs=[pl.BlockSpec((tm, ML_Q_RANK), lambda i: (i, OFF_MQ // ML_Q_RANK)),
                  pl.BlockSpec((tm, ML_KV_RANK), lambda i: (i, OFF_MKV // ML_KV_RANK)),
                  pl.BlockSpec((tm, LANES), lambda i: (i, OFF_KPE // LANES)),
                  pl.BlockSpec((tm, 1), lambda i: (i, 0)),
                  full(qn_w), full(kvn_w), full(wq_all), full(wkv), full(freq)],
        out_specs=[pl.BlockSpec((tm, wide), lambda i: (i, 0)),
                   pl.BlockSpec((tm, wide), lambda i: (i, 0)),
                   pl.BlockSpec((tm, ML_DIM), lambda i: (i, 0))],
        out_shape=[jax.ShapeDtypeStruct((tokens, wide), BF16),
                   jax.ShapeDtypeStruct((tokens, wide), BF16),
                   jax.ShapeDtypeStruct((tokens, ML_DIM), BF16)],
        compiler_params=_cparams(("parallel",)),
        name="mla_prep",
    )(pa, pa, pf, positions.reshape(tokens, 1), qn_w, kvn_w, wq_all, wkv, freq)


def _mla_attn_kernel(q_ref, k_ref, v_ref, o_ref, vt_ref, *, tq, tk):
    i = pl.program_id(1)
    wq = ML_QK_PAD

    @pl.when(i == 0)
    def _():
        _transpose_into(vt_ref, v_ref, tk)

    n_tiles = (i * tq + tq - 1) // tk + 1
    n_full = (i * tq + 1) // tk
    q_idx = i * tq + lax.broadcasted_iota(jnp.int32, (tk, tq), 1)
    k_off = lax.broadcasted_iota(jnp.int32, (tk, tq), 0)
    qh = [q_ref[:, h * wq:(h + 1) * wq] for h in range(ML_HEADS)]

    def tiles(j, h):
        off = pl.multiple_of(j * tk, tk)
        return (k_ref[pl.ds(off, tk), h * wq:(h + 1) * wq],
                vt_ref[h * ML_V:(h + 1) * ML_V, pl.ds(off, tk)], off)

    heads = range(ML_HEADS)

    def full_body(j, st):
        kv = [tiles(j, h) for h in heads]
        s = [_dot_t(kv[h][0], qh[h]) for h in heads]
        return _softmax_tiles(s, [0.0] * ML_HEADS, st, [kv[h][1] for h in heads])

    def diag_body(j, st):
        kv = [tiles(j, h) for h in heads]
        keep = q_idx >= kv[0][2] + k_off
        s = [jnp.where(keep, _dot_t(kv[h][0], qh[h]), NEG_BIG) for h in heads]
        return _softmax_tiles(s, [0.0] * ML_HEADS, st, [kv[h][1] for h in heads])

    st = tuple((jnp.full((1, tq), NEG_BIG, F32), jnp.zeros((1, tq), F32), jnp.zeros((ML_V, tq), F32))
               for _ in range(ML_HEADS))
    st = lax.fori_loop(0, n_full, full_body, st)
    st = lax.fori_loop(n_full, n_tiles, diag_body, st)
    for h in range(ML_HEADS):
        o_ref[:, h * ML_V:(h + 1) * ML_V] = (st[h][2] / st[h][1]).T.astype(o_ref.dtype)


def mla_attention(qf, kf, v, batch, *, tq=ATTN_TILE):
    tokens = qf.shape[0]
    seq = tokens // batch
    tq = min(tq, seq)
    tk = tq
    nq = seq // tq
    wide = qf.shape[1]
    return pl.pallas_call(
        functools.partial(_mla_attn_kernel, tq=tq, tk=tk),
        grid=(batch, nq),
        in_specs=[pl.BlockSpec((tq, wide), lambda b, i: (b * nq + i, 0)),
                  pl.BlockSpec((seq, wide), lambda b, i: (b, 0)),
                  pl.BlockSpec((seq, ML_DIM), lambda b, i: (b, 0))],
        out_specs=pl.BlockSpec((tq, ML_DIM), lambda b, i: (b * nq + i, 0)),
        out_shape=jax.ShapeDtypeStruct((tokens, ML_DIM), BF16),
        scratch_shapes=[pltpu.VMEM((ML_DIM, seq), BF16)],
        compiler_params=_cparams(("arbitrary", "arbitrary")),
        name="mla_attention",
    )(qf, kf, v)


def _cross_kernel(x_ref, nw_ref, wq_ref, kv_ref, wo_ref, o_ref):
    x = x_ref[...]
    q = _dot(_rms(x, nw_ref[...], NORM_EPS).astype(BF16), wq_ref[...])
    kv = kv_ref[...]
    scale = CA_HEAD_DIM ** -0.5
    heads = [slice(hh * CA_HEAD_DIM, (hh + 1) * CA_HEAD_DIM) for hh in range(CA_HEADS)]
    s = [_dot_t(q[:, sl].astype(BF16), kv[:, sl]) * scale for sl in heads]
    p = [jnp.exp(x - jnp.max(x, axis=-1, keepdims=True)) for x in s]
    p = [(x / jnp.sum(x, axis=-1, keepdims=True)).astype(BF16) for x in p]
    outs = [_dot(x, kv[:, CA_DIM + sl.start:CA_DIM + sl.stop]) for x, sl in zip(p, heads)]
    o = jnp.concatenate(outs, axis=1).astype(BF16)
    o_ref[...] = x + _dot(o, wo_ref[...])


def cross_block(x, batch, norm_w, wq, kv, wo, *, tq=512):
    tokens, d = x.shape
    seq = tokens // batch
    tq = min(tq, seq)
    nq = seq // tq
    mem_len = kv.shape[0] // batch
    return pl.pallas_call(
        _cross_kernel,
        grid=(batch, nq),
        in_specs=[pl.BlockSpec((tq, d), lambda b, i: (b * nq + i, 0)),
                  pl.BlockSpec((1, d), lambda b, i: (0, 0)),
                  pl.BlockSpec(wq.shape, lambda b, i: (0, 0)),
                  pl.BlockSpec((mem_len, 2 * CA_DIM), lambda b, i: (b, 0)),
                  pl.BlockSpec(wo.shape, lambda b, i: (0, 0))],
        out_specs=pl.BlockSpec((tq, d), lambda b, i: (b * nq + i, 0)),
        out_shape=jax.ShapeDtypeStruct((tokens, d), F32),
        compiler_params=_cparams(("parallel", "parallel")),
        name="cross_block",
    )(x, norm_w.reshape(1, d), wq, kv, wo)


SEL_E1, SEL_E2, SEL_G1, SEL_G2 = 0, 1, 2, 3


def _route(logits, b_router):
    biased = logits + b_router
    lane = lax.broadcasted_iota(jnp.int32, logits.shape, 1)
    big = jnp.int32(LANES)

    def first_argmax(vals):
        mx = jnp.max(vals, axis=-1, keepdims=True)
        return jnp.min(jnp.where(vals == mx, lane, big), axis=-1, keepdims=True)

    def pick(vals, idx):
        return jnp.sum(jnp.where(lane == idx, vals, 0.0), axis=-1, keepdims=True)

    is_group = (lane >= MOE_EXPERTS) & (lane < MOE_EXPERTS + MOE_GROUPS)
    gl = jnp.where(is_group, logits, NEG_BIG)
    ge = jnp.exp(gl - jnp.max(gl, axis=-1, keepdims=True))
    gp = ge / jnp.sum(ge, axis=-1, keepdims=True)
    g_lane = first_argmax(jnp.where(is_group, biased, NEG_BIG))
    p_group = pick(gp, g_lane)
    lo = (g_lane - MOE_EXPERTS) * MOE_PER_GROUP
    in_group = (lane >= lo) & (lane < lo + MOE_PER_GROUP)
    eb = jnp.where(in_group, biased, NEG_BIG)
    i1 = first_argmax(eb)
    i2 = first_argmax(jnp.where(lane == i1, NEG_BIG, eb))
    l1, l2 = pick(logits, i1), pick(logits, i2)
    mx = jnp.maximum(l1, l2)
    e1, e2 = jnp.exp(l1 - mx), jnp.exp(l2 - mx)
    w1, w2 = e1 / (e1 + e2), e2 / (e1 + e2)
    return jnp.where(lane == SEL_E1, i1.astype(F32),
                     jnp.where(lane == SEL_E2, i2.astype(F32),
                               jnp.where(lane == SEL_G1, w1 * p_group,
                                         jnp.where(lane == SEL_G2, w2 * p_group, 0.0))))


def _router_kernel(x_ref, nw_ref, wr_ref, br_ref, h_ref, sel_ref):
    h = _rms(x_ref[...], nw_ref[...], NORM_EPS)
    h_ref[...] = h
    sel_ref[...] = _route(_dot_x3(h, wr_ref[...]), br_ref[...])


def moe_router(x, norm_w, w_router, b_router, *, tm=512):
    tokens, d = x.shape
    tm = min(tm, tokens)
    return pl.pallas_call(
        _router_kernel,
        grid=(tokens // tm,),
        in_specs=[pl.BlockSpec((tm, d), lambda i: (i, 0)),
                  pl.BlockSpec((1, d), lambda i: (0, 0)),
                  pl.BlockSpec((d, LANES), lambda i: (0, 0)),
                  pl.BlockSpec((1, LANES), lambda i: (0, 0))],
        out_specs=[pl.BlockSpec((tm, d), lambda i: (i, 0)),
                   pl.BlockSpec((tm, LANES), lambda i: (i, 0))],
        out_shape=[jax.ShapeDtypeStruct((tokens, d), F32),
                   jax.ShapeDtypeStruct((tokens, LANES), F32)],
        compiler_params=_cparams(("parallel",)),
        name="moe_router",
    )(x, norm_w.reshape(1, d), w_router, b_router)


def _moe_rank_kernel(sel_ref, ltri_ref, rank_ref, counts_ref, carry_ref):
    @pl.when(pl.program_id(0) == 0)
    def _():
        carry_ref[...] = jnp.zeros_like(carry_ref)

    sel = sel_ref[...]
    lane = lax.broadcasted_iota(jnp.int32, sel.shape, 1)
    lane_f = lane.astype(F32)
    oh1 = lane_f == sel[:, SEL_E1:SEL_E1 + 1]
    oh2 = lane_f == sel[:, SEL_E2:SEL_E2 + 1]
    f1, f2 = oh1.astype(F32), oh2.astype(F32)
    ltri = ltri_ref[...]
    before1 = _dot(ltri, f1.astype(BF16))
    before2 = _dot(ltri, f2.astype(BF16))
    c1 = jnp.sum(f1, axis=0, keepdims=True)
    c2 = jnp.sum(f2, axis=0, keepdims=True)
    carry = carry_ref[...]
    r1 = jnp.sum(jnp.where(oh1, before1 + carry, 0.0), axis=1, keepdims=True)
    r2 = jnp.sum(jnp.where(oh2, before2 + carry + c1, 0.0), axis=1, keepdims=True)
    rank_ref[...] = jnp.where(lane == SEL_E1, r1, jnp.where(lane == SEL_E2, r2, 0.0)).astype(jnp.int32)
    total = carry + c1 + c2
    carry_ref[...] = total
    counts_ref[...] = total.astype(jnp.int32)


def moe_rank(sel, *, tm=512):
    tokens = sel.shape[0]
    tm = min(tm, tokens)
    ltri = (jnp.arange(tm)[:, None] > jnp.arange(tm)[None, :]).astype(BF16)
    return pl.pallas_call(
        _moe_rank_kernel,
        grid=(tokens // tm,),
        in_specs=[pl.BlockSpec((tm, LANES), lambda i: (i, 0)),
                  pl.BlockSpec((tm, tm), lambda i: (0, 0))],
        out_specs=[pl.BlockSpec((tm, LANES), lambda i: (i, 0)),
                   pl.BlockSpec((1, LANES), lambda i: (0, 0))],
        out_shape=[jax.ShapeDtypeStruct((tokens, LANES), jnp.int32),
                   jax.ShapeDtypeStruct((1, LANES), jnp.int32)],
        scratch_shapes=[pltpu.VMEM((1, LANES), F32)],
        compiler_params=_cparams(("arbitrary",)),
        name="moe_rank",
    )(sel, ltri)


def _row_copy(src_ref, src_row, dst_ref, dst_row, sem):
    return pltpu.make_async_copy(src_ref.at[pl.ds(src_row, 1)], dst_ref.at[pl.ds(dst_row, 1)], sem)


def _moe_dispatch_kernel(dest_ref, tail_ref, h_ref, xs_ref, hbuf, zero_ref, lsem, ssem, zsem, *, tm):
    step = pl.program_id(0)
    n_steps = pl.num_programs(0)
    base = step * (2 * tm)
    slot = step % 3

    def load(tile, buf_slot):
        return pltpu.make_async_copy(h_ref.at[pl.ds(pl.multiple_of(tile * tm, tm), tm)], hbuf.at[buf_slot],
                                     lsem.at[buf_slot])

    def wait_scatter(buf_slot):
        for _ in range(2):
            pltpu.make_async_copy(hbuf.at[buf_slot], xs_ref.at[pl.ds(0, tm)], ssem.at[buf_slot]).wait()

    @pl.when(step == 0)
    def _():
        load(0, 0).start()

        @pl.when(n_steps > 1)
        def _():
            load(1, 1).start()

    @pl.when(step == 0)
    def _():
        zero_ref[...] = jnp.zeros_like(zero_ref)

        def fill(tail):
            return pltpu.make_async_copy(zero_ref, xs_ref.at[pl.ds(pl.multiple_of(tail, 8), zero_ref.shape[0])], zsem)

        def start(e, carry):
            @pl.when(tail_ref[e] >= 0)
            def _():
                fill(tail_ref[e]).start()
            return carry

        def wait(e, carry):
            @pl.when(tail_ref[e] >= 0)
            def _():
                fill(tail_ref[e]).wait()
            return carry

        lax.fori_loop(0, tail_ref.shape[0], start, 0)
        lax.fori_loop(0, tail_ref.shape[0], wait, 0)

    load(step, slot).wait()

    def issue(r, carry):
        for s in range(2):
            _row_copy(hbuf.at[slot], r, xs_ref, dest_ref[base + 2 * r + s], ssem.at[slot]).start()
        return carry

    lax.fori_loop(0, tm, issue, 0, unroll=DMA_ISSUE_UNROLL)

    @pl.when(step >= 1)
    def _():
        wait_scatter((step + 2) % 3)

    @pl.when(step + 2 < n_steps)
    def _():
        load(step + 2, (step + 2) % 3).start()

    @pl.when(step == n_steps - 1)
    def _():
        wait_scatter(slot)


def moe_dispatch(h, dest, tails, rows, row_tile, *, tm=256):
    tokens, d = h.shape
    tm = min(tm, tokens)
    grid_spec = pltpu.PrefetchScalarGridSpec(
        num_scalar_prefetch=2,
        grid=(tokens // tm,),
        in_specs=[pl.BlockSpec(memory_space=pl.ANY)],
        out_specs=pl.BlockSpec(memory_space=pl.ANY),
        scratch_shapes=[pltpu.VMEM((3, tm, d), h.dtype), pltpu.VMEM((row_tile, d), h.dtype),
                        pltpu.SemaphoreType.DMA((3,)), pltpu.SemaphoreType.DMA((3,)),
                        pltpu.SemaphoreType.DMA(())],
    )
    return pl.pallas_call(
        functools.partial(_moe_dispatch_kernel, tm=tm),
        grid_spec=grid_spec,
        out_shape=jax.ShapeDtypeStruct((rows, d), h.dtype),
        compiler_params=_cparams(("arbitrary",)),
        name="moe_dispatch",
    )(dest, tails, h)


def _moe_expert_kernel(te_ref, nused_ref, xs_ref, wg_ref, wu_ref, wd_ref, ys_ref, wgb, wub, wdb):
    r = pl.program_id(0)
    used = r < nused_ref[0]
    changed = (r == 0) | (te_ref[r] != te_ref[jnp.maximum(r - 1, 0)])

    @pl.when(used & changed)
    def _():
        wgb[...] = wg_ref[...].astype(BF16)
        wub[...] = wu_ref[...].astype(BF16)
        wdb[...] = wd_ref[...].astype(BF16)

    @pl.when(used)
    def _():
        x = xs_ref[...].astype(BF16)
        gate_pre = _dot(x, wgb[...])
        hid = (gate_pre * jax.nn.sigmoid(gate_pre)) * _dot(x, wub[...])
        ys_ref[...] = _dot(hid.astype(BF16), wdb[...])

    @pl.when(jnp.logical_not(used))
    def _():
        ys_ref[...] = jnp.zeros_like(ys_ref)


def moe_experts(xs, tile_expert, n_used, wg, wu, wd, layer, *, tm):
    rows, d = xs.shape
    de = wg.shape[-1]
    grid_spec = pltpu.PrefetchScalarGridSpec(
        num_scalar_prefetch=2,
        grid=(rows // tm,),
        in_specs=[pl.BlockSpec((tm, d), lambda r, te, nu: (jnp.minimum(r, nu[0] - 1), 0)),
                  pl.BlockSpec((None, None, d, de), lambda r, te, nu: (layer, te[r], 0, 0)),
                  pl.BlockSpec((None, None, d, de), lambda r, te, nu: (layer, te[r], 0, 0)),
                  pl.BlockSpec((None, None, de, d), lambda r, te, nu: (layer, te[r], 0, 0))],
        out_specs=pl.BlockSpec((tm, d), lambda r, te, nu: (r, 0)),
        scratch_shapes=[pltpu.VMEM((d, de), BF16), pltpu.VMEM((d, de), BF16), pltpu.VMEM((de, d), BF16)],
    )
    return pl.pallas_call(
        _moe_expert_kernel,
        grid_spec=grid_spec,
        out_shape=jax.ShapeDtypeStruct((rows, d), F32),
        compiler_params=_cparams(("arbitrary",)),
        name="moe_experts",
    )(tile_expert, n_used, xs, wg, wu, wd)


def _moe_combine_kernel(dest_ref, x_ref, sel_ref, nw_ref, ys_ref, o_ref, buf_ref, sem, *, tm, normalize):
    step = pl.program_id(0)
    slot = step % 2

    def gather(tile, buf_slot):
        base = tile * (2 * tm)

        def issue(r, carry):
            for s in range(2):
                _row_copy(ys_ref, dest_ref[base + 2 * r + s], buf_ref.at[buf_slot, s], r, sem.at[buf_slot]).start()
            return carry

        lax.fori_loop(0, tm, issue, 0, unroll=DMA_ISSUE_UNROLL)

    @pl.when(step == 0)
    def _():
        gather(0, 0)

    @pl.when(step + 1 < pl.num_programs(0))
    def _():
        gather(step + 1, 1 - slot)

    for s in range(2):
        pltpu.make_async_copy(ys_ref.at[pl.ds(0, tm)], buf_ref.at[slot, s], sem.at[slot]).wait()
    sel = sel_ref[...]
    out = x_ref[...] + sel[:, SEL_G1:SEL_G1 + 1] * buf_ref[slot, 0] + sel[:, SEL_G2:SEL_G2 + 1] * buf_ref[slot, 1]
    o_ref[...] = _rms(out, nw_ref[...], NORM_EPS) if normalize else out


def moe_combine(x, sel, ys, dest, final_norm=None, *, tm=256):
    tokens, d = x.shape
    tm = min(tm, tokens)
    normalize = final_norm is not None
    nw = (final_norm if normalize else jnp.ones((d,), F32)).reshape(1, d)
    grid_spec = pltpu.PrefetchScalarGridSpec(
        num_scalar_prefetch=1,
        grid=(tokens // tm,),
        in_specs=[pl.BlockSpec((tm, d), lambda i, *_: (i, 0)),
                  pl.BlockSpec((tm, LANES), lambda i, *_: (i, 0)),
                  pl.BlockSpec((1, d), lambda i, *_: (0, 0)),
                  pl.BlockSpec(memory_space=pl.ANY)],
        out_specs=pl.BlockSpec((tm, d), lambda i, *_: (i, 0)),
        scratch_shapes=[pltpu.VMEM((2, 2, tm, d), F32), pltpu.SemaphoreType.DMA((2,))],
    )
    return pl.pallas_call(
        functools.partial(_moe_combine_kernel, tm=tm, normalize=normalize),
        grid_spec=grid_spec,
        out_shape=jax.ShapeDtypeStruct((tokens, d), F32),
        compiler_params=_cparams(("arbitrary",)),
        name="moe_combine",
    )(dest, x, sel, nw, ys)


def moe_block(x, norm_w, w_router, b_router, wg, wu, wd, layer, final_norm=None, *, tm=MOE_ROW_TILE):
    tokens, d = x.shape
    n_exp = wg.shape[1]
    h, sel = moe_router(x, norm_w, w_router, b_router)
    rank, counts = moe_rank(sel)
    padded = (counts[0, :n_exp] + (tm - 1)) // tm * tm
    ends = jnp.cumsum(padded)
    starts = ends - padded
    experts = sel[:, SEL_E1:SEL_E2 + 1].astype(jnp.int32)
    start_of = jnp.sum(jnp.where(experts[..., None] == jnp.arange(n_exp, dtype=jnp.int32), starts, 0), axis=-1)
    dest = (start_of + rank[:, SEL_E1:SEL_E2 + 1]).reshape(-1)
    rows = 2 * tokens + n_exp * tm
    tile_start = jnp.arange(rows // tm, dtype=jnp.int32) * tm
    tile_expert = jnp.minimum(jnp.sum(tile_start[:, None] >= ends[None, :], axis=1), n_exp - 1).astype(jnp.int32)
    n_used = (ends[-1] // tm).astype(jnp.int32).reshape(1)
    tails = jnp.concatenate([jnp.where(padded > 0, ends - tm, -1),
                             jnp.where(tile_start >= ends[-1], tile_start, -1)]).astype(jnp.int32)
    xs = moe_dispatch(h, dest, tails, rows, tm)
    ys = moe_experts(xs, tile_expert, n_used, wg, wu, wd, layer, tm=tm)
    return moe_combine(x, sel, ys, dest, final_norm)


def _proj_weights(w_in_l, w_vres_l):
    d = w_in_l.shape[0]
    mla0 = RW_COLS + DF_COLS
    vres = jnp.zeros((d, RW_V_RANK), F32) if w_vres_l is None else w_vres_l
    part_f = [w_in_l[:, 3 * RW_DIM:RW_COLS],
              vres, jnp.zeros((d, LANES - RW_V_RANK), F32),
              w_in_l[:, mla0 + ML_Q_RANK + ML_KV_RANK:mla0 + ML_COLS], jnp.zeros((d, LANES - ML_ROPE), F32)]
    part_a = [w_in_l[:, :3 * RW_DIM],
              w_in_l[:, RW_COLS:RW_COLS + DF_COLS],
              w_in_l[:, mla0 + ML_Q_RANK:mla0 + ML_Q_RANK + ML_KV_RANK],
              jnp.zeros((d, OFF_MQ - OFF_MKV - ML_KV_RANK), F32),
              w_in_l[:, mla0:mla0 + ML_Q_RANK]]
    return jnp.concatenate(part_f, axis=1).astype(BF16), jnp.concatenate(part_a, axis=1).astype(BF16)


def _pad_rows(w, rows, at=0):
    out = jnp.zeros((rows, w.shape[1]), w.dtype)
    return lax.dynamic_update_slice(out, w, (at, 0))


def kernel(x, mem, positions, rel_bias, final_norm, norm_mix, w_in, w_in_vres, w_out, tm_mu, tm_mu_vres, tm_w0, tm_w2, tm_a0, tm_a2, tm_v0, tm_v2, tm_g2, tm_k_k, tm_k_a, tm_r_k, tm_ln_w, tm_ln_b, da_lq1, da_lk1, da_lq2, da_lk2, da_subln, mla_q_norm, mla_wq_b, mla_kv_norm, mla_wkv_b, norm_cross, norm_mem, ca_wq, ca_wkv, ca_wo, norm_ffn, moe_w_group, moe_b_group, moe_w_expert, moe_b_expert, moe_w_gate, moe_w_up, moe_w_down):
    batch, seq, d = x.shape
    tokens = batch * seq
    depth = norm_mix.shape[0]
    xf = x.reshape(tokens, d)
    memf = mem.reshape(-1, d)
    positions = positions.astype(jnp.int32)

    head_of_lane = jnp.arange(RW_DIM) // RW_HEAD_DIM
    seg = (head_of_lane[:, None] == jnp.arange(LANES)[None, :]).astype(BF16)
    seg_t = seg.T
    row = lambda v: v.reshape(1, -1)

    v_first = None
    for l in range(depth):
        w_f, w_a = _proj_weights(w_in[l], None if l == 0 else w_in_vres[l - 1])
        pa, proj = norm_matmul(xf, norm_mix[l], w_a, w_f, out_dtype=BF16, tm=1024, tn=PROJ_A_COLS // 3)

        mu = tm_mu[l]
        prm = dict(mu_r=row(mu[:RW_DIM]), mu_k=row(mu[RW_DIM:2 * RW_DIM]), mu_v=row(mu[2 * RW_DIM:3 * RW_DIM]),
                   mu_l=row(mu[3 * RW_DIM:]), w0=row(tm_w0[l]), a0=row(tm_a0[l]),
                   w2=_pad_rows(tm_w2[l], LANES, 0), a2=_pad_rows(tm_a2[l], LANES, RW_W_RANK),
                   g2=tm_g2[l].astype(BF16), k_k=row(tm_k_k[l]), k_a=row(tm_k_a[l]), r_k=row(tm_r_k[l]),
                   seg=seg, seg_t=seg_t)
        if l > 0:
            prm.update(mu_vr=jnp.pad(row(tm_mu_vres[l - 1]), ((0, 0), (0, LANES - RW_V_RANK))),
                       v0=row(tm_v0[l - 1]), v2=_pad_rows(tm_v2[l - 1], LANES, 0))
        r, lw, k, v, kap, beta, gate, bonus = rwkv_prep(pa, proj, batch, v_first, prm)
        if l == 0:
            v_first = v
        o = rwkv_scan(r, lw, k, v, kap, beta, batch)
        y_a = rwkv_post(o, bonus, gate, tm_ln_w[l], tm_ln_b[l], seg, seg_t)

        lambda_init = 0.8 - 0.6 * math.exp(-0.3 * l)
        lam = (jnp.exp(jnp.sum(da_lq1[l] * da_lk1[l])) - jnp.exp(jnp.sum(da_lq2[l] * da_lk2[l])) + lambda_init)
        y_b = diff_attention(pa, positions, rel_bias, lam, lambda_init, da_subln[l])

        wq = mla_wq_b[l].reshape(ML_Q_RANK, ML_HEADS, ML_NOPE + ML_ROPE)
        wq_pe = jnp.pad(wq[:, :, ML_NOPE:], ((0, 0), (0, 0), (0, LANES - ML_ROPE)))
        wq_all = jnp.concatenate([wq[:, :, :ML_NOPE].reshape(ML_Q_RANK, -1),
                                  wq_pe.reshape(ML_Q_RANK, -1)], axis=1).astype(BF16)
        wkv = mla_wkv_b[l].reshape(ML_KV_RANK, ML_HEADS, ML_NOPE + ML_V)
        wkv = jnp.concatenate([wkv[:, :, :ML_NOPE].reshape(ML_KV_RANK, -1),
                               wkv[:, :, ML_NOPE:].reshape(ML_KV_RANK, -1)], axis=1).astype(BF16)
        qf, kf, v_mla = mla_prep(pa, proj, positions, mla_q_norm[l], mla_kv_norm[l], wq_all, wkv)
        y_c = mla_attention(qf, kf, v_mla, batch)

        wo = w_out[l].astype(BF16)
        xf = matmul_res([y_a, y_b, y_c],
                        [wo[:RW_DIM], wo[RW_DIM:RW_DIM + DF_DIM], wo[RW_DIM + DF_DIM:]], xf)

        kv_mem = norm_matmul(memf, norm_mem[l], ca_wkv[l].astype(BF16), out_dtype=BF16)
        xf = cross_block(xf, batch, norm_cross[l], ca_wq[l].astype(BF16), kv_mem, ca_wo[l].astype(BF16))

        w_router = jnp.concatenate(
            [moe_w_expert[l], moe_w_group[l], jnp.zeros((d, LANES - MOE_EXPERTS - MOE_GROUPS), F32)], axis=1)
        b_router = jnp.concatenate(
            [moe_b_expert[l], moe_b_group[l], jnp.zeros((LANES - MOE_EXPERTS - MOE_GROUPS,), F32)]).reshape(1, LANES)
        xf = moe_block(xf, norm_ffn[l], w_router, b_router, moe_w_gate, moe_w_up, moe_w_down, l,
                       final_norm if l == depth - 1 else None)

    return xf.reshape(batch, seq, d)
```

```python
import functools
import math

import jax
import jax.numpy as jnp
from jax import lax
from jax.experimental import pallas as pl
from jax.experimental.pallas import tpu as pltpu

F32 = jnp.float32
BF16 = jnp.bfloat16

NORM_EPS = 1e-6
ROPE_THETA = 10000.0

RW_HEADS = 16
RW_HEAD_DIM = 64
RW_DIM = RW_HEADS * RW_HEAD_DIM
RW_W_RANK = 64
RW_A_RANK = 64
RW_G_RANK = 128
RW_V_RANK = 32
RW_LORA = RW_W_RANK + RW_A_RANK + RW_G_RANK
RW_LN_EPS = 64e-5
RW_COLS = 3 * RW_DIM + RW_LORA

DF_HEADS = 4
DF_HEAD_DIM = 64
DF_V_DIM = 2 * DF_HEAD_DIM
DF_QK = DF_HEADS * 2 * DF_HEAD_DIM
DF_DIM = DF_HEADS * DF_V_DIM
DF_COLS = 2 * DF_QK + DF_DIM
DF_SUBLN_EPS = 1e-5

ML_HEADS = 4
ML_Q_RANK = 384
ML_KV_RANK = 256
ML_NOPE = 128
ML_ROPE = 64
ML_V = 128
ML_DIM = ML_HEADS * ML_V
ML_COLS = ML_Q_RANK + ML_KV_RANK + ML_ROPE

REL_BUCKETS = 32
REL_MAX_DIST = 128

CA_HEADS = 4
CA_HEAD_DIM = 128
CA_DIM = CA_HEADS * CA_HEAD_DIM

MOE_GROUPS = 4
MOE_PER_GROUP = 8
MOE_EXPERTS = MOE_GROUPS * MOE_PER_GROUP

LANES = 128
SCAN_CHUNK = 64
SCAN_GROUP = 4
SCAN_BATCHES = 2
ATTN_TILE = 512
ONES_ROWS = 16
POST_ROW_BLOCKS = 4
DMA_ISSUE_UNROLL = 8
MOE_ROW_TILE = 256
VMEM_LIMIT = 56 * 1024 * 1024
NEG_BIG = -1e30

LOG2E = 1.4426950408889634

OFF_LORA = 0
OFF_VRES = OFF_LORA + RW_LORA
OFF_KPE = OFF_VRES + LANES
PROJ_F_COLS = OFF_KPE + LANES
OFF_R = 0
OFF_K = RW_DIM
OFF_V = 2 * RW_DIM
OFF_DQ = 3 * RW_DIM
OFF_DK = OFF_DQ + DF_QK
OFF_DV = OFF_DK + DF_QK
OFF_MKV = OFF_DV + DF_DIM
OFF_MQ = 13 * ML_Q_RANK
PROJ_A_COLS = OFF_MQ + ML_Q_RANK


def _cparams(sem, vmem=VMEM_LIMIT, flags=None):
    return pltpu.CompilerParams(dimension_semantics=sem, vmem_limit_bytes=vmem, flags=flags)


def _dot(a, b):
    return jnp.dot(a, b, preferred_element_type=F32)


def _dot_t(a, b):
    return lax.dot_general(a, b, (((1,), (1,)), ((), ())), preferred_element_type=F32)


def _split3(x):
    hi = x.astype(BF16)
    r1 = x - hi.astype(F32)
    mid = r1.astype(BF16)
    lo = (r1 - mid.astype(F32)).astype(BF16)
    return hi, mid, lo


def _dot_rhs01(x, ones_bf16):
    hi = x.astype(BF16)
    lo = (x - hi.astype(F32)).astype(BF16)
    return _dot(hi, ones_bf16) + _dot(lo, ones_bf16)


def _dot_x3(a, b):
    ah = a.astype(BF16)
    al = (a - ah.astype(F32)).astype(BF16)
    bh = b.astype(BF16)
    bl = (b - bh.astype(F32)).astype(BF16)
    return _dot(ah, bh) + _dot(ah, bl) + _dot(al, bh)


def _rms(x, w, eps):
    ms = jnp.mean(x * x, axis=-1, keepdims=True)
    return x * lax.rsqrt(ms + eps) * w


def _norm_matmul_kernel(*refs, eps, has_side):
    if has_side:
        x_ref, nw_ref, w_ref, ws_ref, o_ref, os_ref, xn_ref = refs
    else:
        x_ref, nw_ref, w_ref, o_ref, xn_ref = refs

    @pl.when(pl.program_id(1) == 0)
    def _():
        xn_ref[...] = _rms(x_ref[...], nw_ref[...], eps).astype(BF16)
        if has_side:
            os_ref[...] = _dot(xn_ref[...], ws_ref[...])

    o_ref[...] = _dot(xn_ref[...], w_ref[...]).astype(o_ref.dtype)


def norm_matmul(x, nw, w, w_side=None, *, out_dtype=F32, tm=512, tn=None, eps=NORM_EPS):
    m, d = x.shape
    n = w.shape[1]
    tm = min(tm, m)
    tn = n if tn is None else tn
    has_side = w_side is not None
    in_specs = [pl.BlockSpec((tm, d), lambda i, j: (i, 0)),
                pl.BlockSpec((1, d), lambda i, j: (0, 0)),
                pl.BlockSpec((d, tn), lambda i, j: (0, j))]
    out_specs = [pl.BlockSpec((tm, tn), lambda i, j: (i, j))]
    out_shape = [jax.ShapeDtypeStruct((m, n), out_dtype)]
    args = [x, nw.reshape(1, d), w]
    if has_side:
        ns = w_side.shape[1]
        in_specs.append(pl.BlockSpec((d, ns), lambda i, j: (0, 0)))
        out_specs.append(pl.BlockSpec((tm, ns), lambda i, j: (i, 0)))
        out_shape.append(jax.ShapeDtypeStruct((m, ns), F32))
        args.append(w_side)
    outs = pl.pallas_call(
        functools.partial(_norm_matmul_kernel, eps=eps, has_side=has_side),
        grid=(m // tm, n // tn),
        in_specs=in_specs,
        out_specs=out_specs,
        out_shape=out_shape,
        scratch_shapes=[pltpu.VMEM((tm, d), BF16)],
        compiler_params=_cparams(("parallel", "arbitrary")),
        name="norm_matmul",
    )(*args)
    return outs if has_side else outs[0]


def _matmul_res_kernel(*refs, n_a):
    a_refs, w_refs = refs[:n_a], refs[n_a:2 * n_a]
    res_ref, o_ref = refs[2 * n_a], refs[2 * n_a + 1]
    acc = res_ref[...]
    for a_ref, w_ref in zip(a_refs, w_refs):
        acc = acc + _dot(a_ref[...].astype(BF16), w_ref[...])
    o_ref[...] = acc


def matmul_res(a_list, w_list, res, *, tm=512, tn=2048):
    m, n = res.shape
    tm = min(tm, m)
    tn = min(tn, n)
    n_a = len(a_list)
    in_specs = ([pl.BlockSpec((tm, a.shape[1]), lambda i, j: (i, 0)) for a in a_list]
                + [pl.BlockSpec((w.shape[0], tn), lambda i, j: (0, j)) for w in w_list]
                + [pl.BlockSpec((tm, tn), lambda i, j: (i, j))])
    return pl.pallas_call(
        functools.partial(_matmul_res_kernel, n_a=n_a),
        grid=(m // tm, n // tn),
        in_specs=in_specs,
        out_specs=pl.BlockSpec((tm, tn), lambda i, j: (i, j)),
        out_shape=jax.ShapeDtypeStruct((m, n), F32),
        compiler_params=_cparams(("parallel", "arbitrary")),
        name="matmul_res",
    )(*a_list, *w_list, res)


def _softplus(z):
    return jnp.maximum(z, 0.0) + jnp.log(1.0 + jnp.exp(-jnp.abs(z)))


def _rwkv_prep_kernel(*refs, has_vres):
    if has_vres:
        (pr_ref, pk_ref, pv_ref, pl_ref, pvr_ref, vfirst_ref,
         mu_r, mu_k, mu_v, mu_l, mu_vr, w0, w2, a0, a2, g2, v0, v2,
         k_k, k_a, r_k, seg, seg_t,
         r_o, lw_o, k_o, v_o, kap_o, beta_o, g_o, bonus_o,
         last_r, last_k, last_v, last_l, last_vr) = refs
    else:
        (pr_ref, pk_ref, pv_ref, pl_ref,
         mu_r, mu_k, mu_v, mu_l, w0, w2, a0, a2, g2,
         k_k, k_a, r_k, seg, seg_t,
         r_o, lw_o, k_o, v_o, kap_o, beta_o, g_o, bonus_o,
         last_r, last_k, last_v, last_l) = refs
    t = pl.program_id(1)

    def shifted(p_ref, last_ref, mu_ref):
        p = p_ref[...].astype(F32)
        n = p.shape[0]
        carried = jnp.where(t == 0, 0.0, last_ref[0:1, :])
        row = lax.broadcasted_iota(jnp.int32, p.shape, 0)
        prev = jnp.where(row == 0, carried, pltpu.roll(p, 1, axis=0))
        last_ref[0:1, :] = p[n - 1:n, :]
        return p + mu_ref[...] * (prev - p)

    r = shifted(pr_ref, last_r, mu_r)
    k = shifted(pk_ref, last_k, mu_k)
    v = shifted(pv_ref, last_v, mu_v)
    lora = shifted(pl_ref, last_l, mu_l)
    wl = lora[:, :LANES]
    gl = lora[:, LANES:]

    lane = lax.broadcasted_iota(jnp.int32, wl.shape, 1)
    wl_t = jnp.where(lane < RW_W_RANK, jnp.tanh(wl), 0.0)
    al = jnp.where(lane >= RW_W_RANK, wl, 0.0)
    w_log = -_softplus(-(w0[...] + _dot_x3(wl_t, w2[...]))) - 0.5
    lw_o[...] = -jnp.exp(w_log)
    a = jax.nn.sigmoid(a0[...] + _dot_x3(al, a2[...]))
    g_o[...] = _dot(jax.nn.sigmoid(gl).astype(BF16), g2[...]).astype(g_o.dtype)

    segm, segm_t = seg[...], seg_t[...]

    def head_sum(x):
        return _dot_rhs01(_dot_rhs01(x, segm), segm_t)

    kk = k * k_k[...]
    kk = kk * lax.rsqrt(jnp.maximum(head_sum(kk * kk), 1e-24))
    k = k * (1.0 + (a - 1.0) * k_a[...])
    if has_vres:
        vr = shifted(pvr_ref, last_vr, mu_vr)
        mix = jax.nn.sigmoid(v0[...] + _dot_x3(vr, v2[...]))
        v = v + (vfirst_ref[...] - v) * mix
    r_o[...] = r.astype(r_o.dtype)
    k_o[...] = k.astype(k_o.dtype)
    v_o[...] = v.astype(v_o.dtype)
    kap_o[...] = kk.astype(kap_o.dtype)
    beta_o[...] = (kk * a).astype(beta_o.dtype)
    bonus_o[...] = (head_sum(r * k * r_k[...]) * v).astype(bonus_o.dtype)


def rwkv_prep(pa, proj, batch, vfirst, prm, *, tt=512):
    tokens = proj.shape[0]
    seq = tokens // batch
    tt = min(tt, seq)
    nt = seq // tt
    has_vres = vfirst is not None
    d = RW_DIM

    def rows(width, col):
        return pl.BlockSpec((tt, width), lambda b, t, col=col: (b * nt + t, col))

    def full(shape):
        return pl.BlockSpec(shape, lambda b, t: (0, 0))

    in_specs = [rows(d, OFF_R // d), rows(d, OFF_K // d), rows(d, OFF_V // d),
                rows(RW_LORA, OFF_LORA // RW_LORA)]
    args = [pa, pa, pa, proj]
    if has_vres:
        in_specs += [rows(LANES, OFF_VRES // LANES), rows(d, 0)]
        args += [proj, vfirst]
    names = ["mu_r", "mu_k", "mu_v", "mu_l"] + (["mu_vr"] if has_vres else []) + ["w0", "w2", "a0", "a2", "g2"]
    names += (["v0", "v2"] if has_vres else []) + ["k_k", "k_a", "r_k", "seg", "seg_t"]
    for nm in names:
        in_specs.append(full(prm[nm].shape))
        args.append(prm[nm])
    out_spec = pl.BlockSpec((tt, d), lambda b, t: (b * nt + t, 0))
    scratch = [pltpu.VMEM((8, d), F32)] * 3 + [pltpu.VMEM((8, RW_LORA), F32)]
    if has_vres:
        scratch.append(pltpu.VMEM((8, LANES), F32))
    return pl.pallas_call(
        functools.partial(_rwkv_prep_kernel, has_vres=has_vres),
        grid=(batch, nt),
        in_specs=in_specs,
        out_specs=[out_spec] * 8,
        out_shape=[jax.ShapeDtypeStruct((tokens, d), F32 if i == 1 else BF16) for i in range(8)],
        scratch_shapes=scratch,
        compiler_params=_cparams(("arbitrary", "arbitrary")),
        name="rwkv_prep",
    )(*args)


def _rwkv_scan_kernel(r_ref, lw_ref, k_ref, v_ref, kap_ref, beta_ref, tril_ref, bmask_ref,
                      o_ref, ht_ref):
    @pl.when(pl.program_id(1) == 0)
    def _():
        ht_ref[...] = jnp.zeros_like(ht_ref)

    n_batch, c, d = lw_ref.shape
    w = ht_ref.shape[1]
    g = w // RW_HEAD_DIM
    bmask = bmask_ref[...]
    bmask_b = bmask.astype(BF16)
    tril3 = tril_ref[...]
    t_idx = lax.broadcasted_iota(jnp.int32, (c, w), 0)
    s_idx = lax.broadcasted_iota(jnp.int32, (c, w), 1) % c
    strict = t_idx > s_idx
    incl = t_idx >= s_idx
    n_sq = int(math.log2(c))

    def stack(x):
        return jnp.concatenate([x.astype(BF16)] * g, axis=0) * bmask_b

    sls = [(bi, slice(None), slice(lo, lo + w)) for bi in range(n_batch) for lo in range(0, d, w)]
    groups = range(len(sls))
    lw = [lw_ref[sl] for sl in sls]
    cum = [_dot(tril3, jnp.concatenate(_split3(x), axis=0)) for x in lw]
    total = [x[c - 1:c, :] for x in cum]
    ar = [jnp.concatenate([-kap_ref[sls[gi]] * jnp.exp(cum[gi] - lw[gi]), r_ref[sls[gi]] * jnp.exp(cum[gi])],
                          axis=0).astype(BF16) for gi in groups]
    p_inv = [jnp.exp(-x) for x in cum]
    b_s = [stack(beta_ref[sls[gi]] * p_inv[gi]) for gi in groups]
    k_s = [stack(k_ref[sls[gi]] * p_inv[gi]) for gi in groups]
    v_n = [v_ref[sl] for sl in sls]
    v_s = [stack(x) for x in v_n]

    arb = [_dot_t(ar[gi], b_s[gi]) for gi in groups]
    ark = [_dot_t(ar[gi], k_s[gi]) for gi in groups]
    ab = [jnp.where(strict, m[:c], 0.0) for m in arb]
    rb = [jnp.where(incl, m[c:], 0.0).astype(BF16) for m in arb]
    akrk = [jnp.concatenate([jnp.where(strict, m[:c], 0.0), jnp.where(incl, m[c:], 0.0)], axis=0).astype(BF16)
            for m in ark]

    ht = [ht_ref[gi] for gi in groups]
    base = [_dot_t(ar[gi], ht[gi].astype(BF16)) + _dot(akrk[gi], v_s[gi]) for gi in groups]
    x = [m[:c] for m in base]
    lp = ab
    for i in range(n_sq):
        lpb = [m.astype(BF16) for m in lp]
        x = [x[gi] + _dot(lpb[gi], stack(x[gi])) for gi in groups]
        if i < n_sq - 1:
            lp = [_dot(lpb[gi], stack(lp[gi])) for gi in groups]
    for gi in groups:
        o_ref[sls[gi]] = base[gi][c:] + _dot(rb[gi], stack(x[gi]))

    for gi in groups:
        p_rem = jnp.exp(total[gi] - cum[gi])
        z = jnp.concatenate([beta_ref[sls[gi]] * p_rem, k_ref[sls[gi]] * p_rem], axis=0).astype(BF16)
        uv_t = jnp.concatenate([x[gi], v_n[gi].astype(F32)], axis=0).T.astype(BF16)
        ht_ref[gi] = ht[gi] * jnp.exp(total[gi]) + bmask * _dot(uv_t, z)


def rwkv_scan(r, lw, k, v, kap, beta, batch):
    tokens, d = r.shape
    seq = tokens // batch
    c = min(SCAN_CHUNK, seq)
    nc = seq // c
    gw = SCAN_GROUP * RW_HEAD_DIM
    rr = SCAN_GROUP * c
    assert c == RW_HEAD_DIM, "the stacking mask doubles as the head-block mask of the state"
    bb = math.gcd(batch, SCAN_BATCHES)
    tril = jnp.tile((jnp.arange(c)[:, None] >= jnp.arange(c)[None, :]).astype(BF16), (1, 3))
    bmask = (jnp.arange(rr)[:, None] // c == jnp.arange(gw)[None, :] // RW_HEAD_DIM).astype(F32)
    blk = pl.BlockSpec((bb, c, d), lambda b, i: (b, i, 0))
    as3d = lambda a: a.reshape(batch, seq, d)
    out = pl.pallas_call(
        _rwkv_scan_kernel,
        grid=(batch // bb, nc),
        in_specs=[blk] * 6 + [pl.BlockSpec((c, 3 * c), lambda b, i: (0, 0)),
                              pl.BlockSpec((rr, gw), lambda b, i: (0, 0))],
        out_specs=blk,
        out_shape=jax.ShapeDtypeStruct((batch, seq, d), F32),
        scratch_shapes=[pltpu.VMEM((bb * (d // gw), gw, gw), F32)],
        compiler_params=_cparams(("arbitrary", "arbitrary")),
        name="rwkv_scan",
    )(as3d(r), as3d(lw), as3d(k), as3d(v), as3d(kap), as3d(beta), tril, bmask)
    return out.reshape(tokens, d)


def _rwkv_post_kernel(o_ref, bonus_ref, g_ref, lnw_ref, lnb_ref, seg, seg_t, y_ref):
    segm, segm_t = seg[...], seg_t[...]

    def head_means(xs):
        sums = [_dot_rhs01(x, segm) for x in xs]
        return [_dot_rhs01(s, segm_t) * (1.0 / RW_HEAD_DIM) for s in sums]

    rows = o_ref.shape[0] // POST_ROW_BLOCKS
    blocks = [slice(i * rows, (i + 1) * rows) for i in range(POST_ROW_BLOCKS)]
    o = [o_ref[b, :] for b in blocks]
    dlt = [x - m for x, m in zip(o, head_means(o))]
    var = head_means([x * x for x in dlt])
    for b, x, v in zip(blocks, dlt, var):
        y = x * lax.rsqrt(v + RW_LN_EPS) * lnw_ref[...] + lnb_ref[...]
        y_ref[b, :] = ((y + bonus_ref[b, :]) * g_ref[b, :]).astype(y_ref.dtype)


def rwkv_post(o, bonus, g, ln_w, ln_b, seg, seg_t, *, tm=512):
    tokens, d = o.shape
    tm = min(tm, tokens)
    blk = pl.BlockSpec((tm, d), lambda i: (i, 0))
    vec = pl.BlockSpec((1, d), lambda i: (0, 0))
    return pl.pallas_call(
        _rwkv_post_kernel,
        grid=(tokens // tm,),
        in_specs=[blk, blk, blk, vec, vec,
                  pl.BlockSpec(seg.shape, lambda i: (0, 0)), pl.BlockSpec(seg_t.shape, lambda i: (0, 0))],
        out_specs=blk,
        out_shape=jax.ShapeDtypeStruct((tokens, d), BF16),
        compiler_params=_cparams(("parallel",)),
        name="rwkv_post",
    )(o, bonus, g, ln_w.reshape(1, d), ln_b.reshape(1, d), seg, seg_t)


def _t5_thresholds():
    max_exact = REL_BUCKETS // 2
    thr = list(range(1, max_exact))
    n = max_exact
    for bucket in range(max_exact, REL_BUCKETS):
        while True:
            large = max_exact + int(math.log(max(n, max_exact) / max_exact)
                                    / math.log(REL_MAX_DIST / max_exact) * (REL_BUCKETS - max_exact))
            if min(large, REL_BUCKETS - 1) >= bucket:
                break
            n += 1
        thr.append(n)
    return thr


T5_THRESHOLDS = _t5_thresholds()
T5_FAR = T5_THRESHOLDS[-1]


def _softmax_tiles(s_list, c_list, states, vt_list):
    stats = []
    for s_t, c, (m_old, _) in zip(s_list, c_list, states):
        m_new = jnp.maximum(m_old, jnp.max(s_t, axis=0, keepdims=True) + c)
        stats.append((m_new, jnp.exp2(m_old - m_new), jnp.exp2(s_t - (m_new - c)).astype(BF16)))
    return tuple((m_new, alpha * acc + _dot(vt, p_t))
                 for (m_new, alpha, p_t), (_, acc), vt in zip(stats, states, vt_list))


def _transpose_into(vt_ref, v_ref, chunk, heads, width):
    seq = v_ref.shape[0]
    blk = width + ONES_ROWS
    for c in range(seq // chunk):
        cols = slice(c * chunk, (c + 1) * chunk)
        vt = v_ref[cols, :].astype(F32).T.astype(BF16)
        for h in range(heads):
            vt_ref[h * blk:h * blk + width, cols] = vt[h * width:(h + 1) * width]
            vt_ref[h * blk + width:(h + 1) * blk, cols] = jnp.ones((ONES_ROWS, chunk), BF16)


def _diff_attn_kernel(qfirst_ref, klast_ref, q_ref, k_ref, v_ref, qpos_ref, kpos_ref, subln_ref, table_ref, lam_ref,
                      o_ref, vt_ref, *, tq, tk, scale2, out_scale):
    b, i = pl.program_id(0), pl.program_id(1)
    nq = pl.num_programs(1)
    seq = k_ref.shape[0]
    nk = seq // tk
    w = DF_V_DIM
    vblk = w + ONES_ROWS

    @pl.when(i == 0)
    def _():
        _transpose_into(vt_ref, v_ref, tk, DF_HEADS, w)

    n_tiles = (i * tq + tq - 1) // tk + 1
    qf = qfirst_ref[b * nq + i]
    n_far = lax.while_loop(
        lambda j: (j * tk + tk - 1 <= i * tq) & (qf - klast_ref[b * nk + jnp.minimum(j, nk - 1)] >= T5_FAR),
        lambda j: j + 1, jnp.int32(0))

    dist = lax.broadcasted_iota(jnp.int32, (1, LANES), 1)
    qpos = qpos_ref[...]
    q_idx = i * tq + lax.broadcasted_iota(jnp.int32, (tk, tq), 1)
    k_off = lax.broadcasted_iota(jnp.int32, (tk, tq), 0)
    lane = lax.broadcasted_iota(jnp.int32, (tq, w), 1)

    bias_rows, c_far, qm = [], [], []
    for h in range(DF_HEADS):
        bias_vec = jnp.full((1, LANES), table_ref[h], F32)
        for bucket, thr in enumerate(T5_THRESHOLDS, start=1):
            bias_vec = jnp.where(dist >= thr, table_ref[bucket * DF_HEADS + h], bias_vec)
        bias_rows.append(jnp.broadcast_to(bias_vec * LOG2E, (tk, LANES)))
        c_far.append(table_ref[(REL_BUCKETS - 1) * DF_HEADS + h] * LOG2E)
        qh = q_ref[:, h * w:(h + 1) * w].astype(F32) * scale2
        qm.append([jnp.where((lane >= mi * DF_HEAD_DIM) & (lane < (mi + 1) * DF_HEAD_DIM), qh, 0.0).astype(BF16)
                   for mi in range(2)])

    def tiles(j, h):
        off = pl.multiple_of(j * tk, tk)
        return k_ref[pl.ds(off, tk), h * w:(h + 1) * w], vt_ref[h * vblk:(h + 1) * vblk, pl.ds(off, tk)], off

    chains = [(h, mi) for h in range(DF_HEADS) for mi in range(2)]

    def far_body(j, st):
        kv = [tiles(j, h) for h in range(DF_HEADS)]
        s = [_dot_t(kv[h][0], qm[h][mi]) for h, mi in chains]
        return _softmax_tiles(s, [c_far[h] for h, _ in chains], st, [kv[h][1] for h, _ in chains])

    def near_body(j, st):
        off = pl.multiple_of(j * tk, tk)
        n = jnp.clip(qpos - kpos_ref[pl.ds(off, tk), :], 0, LANES - 1)
        keep = q_idx >= off + k_off
        kv = [tiles(j, h) for h in range(DF_HEADS)]
        bias = [jnp.concatenate(
            [jnp.take_along_axis(bias_rows[h], n[:, cb * LANES:(cb + 1) * LANES], axis=1)
             for cb in range(tq // LANES)], axis=1) for h in range(DF_HEADS)]
        s = [jnp.where(keep, _dot_t(kv[h][0], qm[h][mi]) + bias[h], NEG_BIG) for h, mi in chains]
        return _softmax_tiles(s, [0.0] * len(chains), st, [kv[h][1] for h, _ in chains])

    init = tuple((jnp.full((1, tq), NEG_BIG, F32), jnp.zeros((w + ONES_ROWS, tq), F32))
                 for _ in range(2 * DF_HEADS))
    st = lax.fori_loop(0, n_far, far_body, init)
    st = lax.fori_loop(n_far, n_tiles, near_body, st)
    for h in range(DF_HEADS):
        a0, a1 = st[2 * h][1], st[2 * h + 1][1]
        d_t = a0[:w] / a0[w:w + 1] - lam_ref[0] * (a1[:w] / a1[w:w + 1])
        ms = jnp.mean(d_t * d_t, axis=0, keepdims=True)
        y_t = d_t * lax.rsqrt(ms + DF_SUBLN_EPS) * (subln_ref[...] * out_scale)
        o_ref[:, h * w:(h + 1) * w] = y_t.T.astype(o_ref.dtype)


def diff_attention(pa, positions, rel_bias, lam, lambda_init, subln_w, *, tq=ATTN_TILE):
    batch, seq = positions.shape
    tokens = batch * seq
    tq = min(tq, seq)
    tk = tq
    nq, nk = seq // tq, seq // tk
    qfirst = positions[:, ::tq].reshape(-1)
    klast = positions[:, tk - 1::tk].reshape(-1)
    qpos = positions.reshape(batch, 1, seq)
    kpos = positions.reshape(batch, seq, 1)
    wd = DF_DIM
    grid_spec = pltpu.PrefetchScalarGridSpec(
        num_scalar_prefetch=2,
        grid=(batch, nq),
        in_specs=[pl.BlockSpec((tq, wd), lambda b, i, *_: (b * nq + i, OFF_DQ // wd)),
                  pl.BlockSpec((seq, wd), lambda b, i, *_: (b, OFF_DK // wd)),
                  pl.BlockSpec((seq, wd), lambda b, i, *_: (b, OFF_DV // wd)),
                  pl.BlockSpec((None, 1, tq), lambda b, i, *_: (b, 0, i)),
                  pl.BlockSpec((None, seq, 1), lambda b, i, *_: (b, 0, 0)),
                  pl.BlockSpec((DF_V_DIM, 1), lambda b, i, *_: (0, 0)),
                  pl.BlockSpec(memory_space=pltpu.SMEM),
                  pl.BlockSpec(memory_space=pltpu.SMEM)],
        out_specs=pl.BlockSpec((tq, wd), lambda b, i, *_: (b * nq + i, 0)),
        scratch_shapes=[pltpu.VMEM((DF_HEADS * (DF_V_DIM + ONES_ROWS), seq), BF16)],
    )
    return pl.pallas_call(
        functools.partial(_diff_attn_kernel, tq=tq, tk=tk, scale2=DF_HEAD_DIM ** -0.5 * LOG2E,
                          out_scale=1.0 - lambda_init),
        grid_spec=grid_spec,
        out_shape=jax.ShapeDtypeStruct((tokens, DF_DIM), BF16),
        compiler_params=_cparams(("arbitrary", "arbitrary")),
        name="diff_attention",
    )(qfirst, klast, pa, pa, pa, qpos, kpos, subln_w.reshape(DF_V_DIM, 1), rel_bias.reshape(-1), lam.reshape(1))


ML_QK_PAD = 2 * LANES


def _rope_block(x, cos, sin):
    half = ML_ROPE // 2
    lane = lax.broadcasted_iota(jnp.int32, x.shape, 1)
    rot = jnp.where(lane < half, -pltpu.roll(x, LANES - half, axis=1),
                    jnp.where(lane < ML_ROPE, pltpu.roll(x, half, axis=1), 0.0))
    return x * cos + rot * sin


def _mla_prep_kernel(mq_ref, mkv_ref, kpe_ref, pos_ref, qn_w, kvn_w, wq_ref, wkv_ref, freq_ref,
                     qf_o, kf_o, v_o, *, qscale):
    ang = pos_ref[...].astype(F32) * freq_ref[...]
    cos, sin = jnp.cos(ang), jnp.sin(ang)
    qc = _rms(mq_ref[...].astype(F32), qn_w[...], NORM_EPS).astype(BF16)
    q_all = _dot(qc, wq_ref[...]) * qscale
    kvc = _rms(mkv_ref[...].astype(F32), kvn_w[...], NORM_EPS).astype(BF16)
    kvb = _dot(kvc, wkv_ref[...])
    kpe = _rope_block(kpe_ref[...], cos, sin).astype(BF16)
    nope_w = ML_HEADS * ML_NOPE
    for h in range(ML_HEADS):
        lo = h * ML_QK_PAD
        qf_o[:, lo:lo + LANES] = q_all[:, h * LANES:(h + 1) * LANES].astype(BF16)
        qf_o[:, lo + LANES:lo + 2 * LANES] = _rope_block(
            q_all[:, nope_w + h * LANES:nope_w + (h + 1) * LANES], cos, sin).astype(BF16)
        kf_o[:, lo:lo + LANES] = kvb[:, h * LANES:(h + 1) * LANES].astype(BF16)
        kf_o[:, lo + LANES:lo + 2 * LANES] = kpe
    v_o[...] = kvb[:, nope_w:].astype(BF16)


def mla_prep(pa, pf, positions, q_norm, kv_norm, wq_all, wkv, *, tm=512):
    tokens = pa.shape[0]
    tm = min(tm, tokens)
    half = ML_ROPE // 2
    inv_freq = ROPE_THETA ** (-jnp.arange(half, dtype=F32) / half)
    freq = jnp.concatenate([inv_freq, inv_freq, jnp.zeros((LANES - ML_ROPE,), F32)]).reshape(1, LANES)

    def full(a):
        return pl.BlockSpec(a.shape, lambda i: (0, 0))

    qn_w = q_norm.reshape(1, -1)
    kvn_w = kv_norm.reshape(1, -1)
    wide = ML_HEADS * ML_QK_PAD
    return pl.pallas_call(
        functools.partial(_mla_prep_kernel, qscale=(ML_NOPE + ML_ROPE) ** -0.5 * LOG2E),
        grid=(tokens // tm,),
        in_specs=[pl.BlockSpec((tm, ML_Q_RANK), lambda i: (i, OFF_MQ // ML_Q_RANK)),
                  pl.BlockSpec((tm, ML_KV_RANK), lambda i: (i, OFF_MKV // ML_KV_RANK)),
                  pl.BlockSpec((tm, LANES), lambda i: (i, OFF_KPE // LANES)),
                  pl.BlockSpec((tm, 1), lambda i: (i, 0)),
                  full(qn_w), full(kvn_w), full(wq_all), full(wkv), full(freq)],
        out_specs=[pl.BlockSpec((tm, wide), lambda i: (i, 0)),
                   pl.BlockSpec((tm, wide), lambda i: (i, 0)),
                   pl.BlockSpec((tm, ML_DIM), lambda i: (i, 0))],
        out_shape=[jax.ShapeDtypeStruct((tokens, wide), BF16),
                   jax.ShapeDtypeStruct((tokens, wide), BF16),
                   jax.ShapeDtypeStruct((tokens, ML_DIM), BF16)],
        compiler_params=_cparams(("parallel",)),
        name="mla_prep",
    )(pa, pa, pf, positions.reshape(tokens, 1), qn_w, kvn_w, wq_all, wkv, freq)


def _mla_attn_kernel(q_ref, k_ref, v_ref, o_ref, vt_ref, *, tq, tk):
    i = pl.program_id(1)
    wq = ML_QK_PAD
    vblk = ML_V + ONES_ROWS

    @pl.when(i == 0)
    def _():
        _transpose_into(vt_ref, v_ref, tk, ML_HEADS, ML_V)

    n_tiles = (i * tq + tq - 1) // tk + 1
    n_full = (i * tq + 1) // tk
    q_idx = i * tq + lax.broadcasted_iota(jnp.int32, (tk, tq), 1)
    k_off = lax.broadcasted_iota(jnp.int32, (tk, tq), 0)
    qh = [q_ref[:, h * wq:(h + 1) * wq] for h in range(ML_HEADS)]

    def tiles(j, h):
        off = pl.multiple_of(j * tk, tk)
        return (k_ref[pl.ds(off, tk), h * wq:(h + 1) * wq],
                vt_ref[h * vblk:(h + 1) * vblk, pl.ds(off, tk)], off)

    heads = range(ML_HEADS)

    def full_body(j, st):
        kv = [tiles(j, h) for h in heads]
        s = [_dot_t(kv[h][0], qh[h]) for h in heads]
        return _softmax_tiles(s, [0.0] * ML_HEADS, st, [kv[h][1] for h in heads])

    def diag_body(j, st):
        kv = [tiles(j, h) for h in heads]
        keep = q_idx >= kv[0][2] + k_off
        s = [jnp.where(keep, _dot_t(kv[h][0], qh[h]), NEG_BIG) for h in heads]
        return _softmax_tiles(s, [0.0] * ML_HEADS, st, [kv[h][1] for h in heads])

    st = tuple((jnp.full((1, tq), NEG_BIG, F32), jnp.zeros((vblk, tq), F32)) for _ in range(ML_HEADS))
    st = lax.fori_loop(0, n_full, full_body, st)
    st = lax.fori_loop(n_full, n_tiles, diag_body, st)
    for h in range(ML_HEADS):
        acc = st[h][1]
        o_ref[:, h * ML_V:(h + 1) * ML_V] = (acc[:ML_V] / acc[ML_V:ML_V + 1]).T.astype(o_ref.dtype)


def mla_attention(qf, kf, v, batch, *, tq=ATTN_TILE):
    tokens = qf.shape[0]
    seq = tokens // batch
    tq = min(tq, seq)
    tk = tq
    nq = seq // tq
    wide = qf.shape[1]
    return pl.pallas_call(
        functools.partial(_mla_attn_kernel, tq=tq, tk=tk),
        grid=(batch, nq),
        in_specs=[pl.BlockSpec((tq, wide), lambda b, i: (b * nq + i, 0)),
                  pl.BlockSpec((seq, wide), lambda b, i: (b, 0)),
                  pl.BlockSpec((seq, ML_DIM), lambda b, i: (b, 0))],
        out_specs=pl.BlockSpec((tq, ML_DIM), lambda b, i: (b * nq + i, 0)),
        out_shape=jax.ShapeDtypeStruct((tokens, ML_DIM), BF16),
        scratch_shapes=[pltpu.VMEM((ML_HEADS * (ML_V + ONES_ROWS), seq), BF16)],
        compiler_params=_cparams(("arbitrary", "arbitrary")),
        name="mla_attention",
    )(qf, kf, v)


def _cross_kernel(x_ref, nw_ref, wq_ref, kv_ref, wo_ref, o_ref):
    x = x_ref[...]
    q = _dot(_rms(x, nw_ref[...], NORM_EPS).astype(BF16), wq_ref[...])
    kv = kv_ref[...]
    scale = CA_HEAD_DIM ** -0.5
    heads = [slice(hh * CA_HEAD_DIM, (hh + 1) * CA_HEAD_DIM) for hh in range(CA_HEADS)]
    s = [_dot_t(q[:, sl].astype(BF16), kv[:, sl]) * scale for sl in heads]
    p = [jnp.exp(x - jnp.max(x, axis=-1, keepdims=True)) for x in s]
    p = [(x / jnp.sum(x, axis=-1, keepdims=True)).astype(BF16) for x in p]
    outs = [_dot(x, kv[:, CA_DIM + sl.start:CA_DIM + sl.stop]) for x, sl in zip(p, heads)]
    o = jnp.concatenate(outs, axis=1).astype(BF16)
    o_ref[...] = x + _dot(o, wo_ref[...])


def cross_block(x, batch, norm_w, wq, kv, wo, *, tq=512):
    tokens, d = x.shape
    seq = tokens // batch
    tq = min(tq, seq)
    nq = seq // tq
    mem_len = kv.shape[0] // batch
    return pl.pallas_call(
        _cross_kernel,
        grid=(batch, nq),
        in_specs=[pl.BlockSpec((tq, d), lambda b, i: (b * nq + i, 0)),
                  pl.BlockSpec((1, d), lambda b, i: (0, 0)),
                  pl.BlockSpec(wq.shape, lambda b, i: (0, 0)),
                  pl.BlockSpec((mem_len, 2 * CA_DIM), lambda b, i: (b, 0)),
                  pl.BlockSpec(wo.shape, lambda b, i: (0, 0))],
        out_specs=pl.BlockSpec((tq, d), lambda b, i: (b * nq + i, 0)),
        out_shape=jax.ShapeDtypeStruct((tokens, d), F32),
        compiler_params=_cparams(("parallel", "parallel")),
        name="cross_block",
    )(x, norm_w.reshape(1, d), wq, kv, wo)


SEL_E1, SEL_E2, SEL_G1, SEL_G2 = 0, 1, 2, 3


def _route(logits, b_router):
    biased = logits + b_router
    lane = lax.broadcasted_iota(jnp.int32, logits.shape, 1)
    big = jnp.int32(LANES)

    def first_argmax(vals):
        mx = jnp.max(vals, axis=-1, keepdims=True)
        return jnp.min(jnp.where(vals == mx, lane, big), axis=-1, keepdims=True)

    def pick(vals, idx):
        return jnp.sum(jnp.where(lane == idx, vals, 0.0), axis=-1, keepdims=True)

    is_group = (lane >= MOE_EXPERTS) & (lane < MOE_EXPERTS + MOE_GROUPS)
    gl = jnp.where(is_group, logits, NEG_BIG)
    ge = jnp.exp(gl - jnp.max(gl, axis=-1, keepdims=True))
    gp = ge / jnp.sum(ge, axis=-1, keepdims=True)
    g_lane = first_argmax(jnp.where(is_group, biased, NEG_BIG))
    p_group = pick(gp, g_lane)
    lo = (g_lane - MOE_EXPERTS) * MOE_PER_GROUP
    in_group = (lane >= lo) & (lane < lo + MOE_PER_GROUP)
    eb = jnp.where(in_group, biased, NEG_BIG)
    i1 = first_argmax(eb)
    i2 = first_argmax(jnp.where(lane == i1, NEG_BIG, eb))
    l1, l2 = pick(logits, i1), pick(logits, i2)
    mx = jnp.maximum(l1, l2)
    e1, e2 = jnp.exp(l1 - mx), jnp.exp(l2 - mx)
    w1, w2 = e1 / (e1 + e2), e2 / (e1 + e2)
    return jnp.where(lane == SEL_E1, i1.astype(F32),
                     jnp.where(lane == SEL_E2, i2.astype(F32),
                               jnp.where(lane == SEL_G1, w1 * p_group,
                                         jnp.where(lane == SEL_G2, w2 * p_group, 0.0))))


def _router_kernel(x_ref, nw_ref, wr_ref, br_ref, h_ref, sel_ref):
    h = _rms(x_ref[...], nw_ref[...], NORM_EPS)
    h_ref[...] = h
    sel_ref[...] = _route(_dot_x3(h, wr_ref[...]), br_ref[...])


def moe_router(x, norm_w, w_router, b_router, *, tm=512):
    tokens, d = x.shape
    tm = min(tm, tokens)
    return pl.pallas_call(
        _router_kernel,
        grid=(tokens // tm,),
        in_specs=[pl.BlockSpec((tm, d), lambda i: (i, 0)),
                  pl.BlockSpec((1, d), lambda i: (0, 0)),
                  pl.BlockSpec((d, LANES), lambda i: (0, 0)),
                  pl.BlockSpec((1, LANES), lambda i: (0, 0))],
        out_specs=[pl.BlockSpec((tm, d), lambda i: (i, 0)),
                   pl.BlockSpec((tm, LANES), lambda i: (i, 0))],
        out_shape=[jax.ShapeDtypeStruct((tokens, d), F32),
                   jax.ShapeDtypeStruct((tokens, LANES), F32)],
        compiler_params=_cparams(("parallel",)),
        name="moe_router",
    )(x, norm_w.reshape(1, d), w_router, b_router)


def _moe_rank_kernel(sel_ref, ltri_ref, rank_ref, counts_ref, carry_ref):
    @pl.when(pl.program_id(0) == 0)
    def _():
        carry_ref[...] = jnp.zeros_like(carry_ref)

    sel = sel_ref[...]
    lane = lax.broadcasted_iota(jnp.int32, sel.shape, 1)
    lane_f = lane.astype(F32)
    oh1 = lane_f == sel[:, SEL_E1:SEL_E1 + 1]
    oh2 = lane_f == sel[:, SEL_E2:SEL_E2 + 1]
    f1, f2 = oh1.astype(F32), oh2.astype(F32)
    ltri = ltri_ref[...]
    before1 = _dot(ltri, f1.astype(BF16))
    before2 = _dot(ltri, f2.astype(BF16))
    c1 = jnp.sum(f1, axis=0, keepdims=True)
    c2 = jnp.sum(f2, axis=0, keepdims=True)
    carry = carry_ref[...]
    r1 = jnp.sum(jnp.where(oh1, before1 + carry, 0.0), axis=1, keepdims=True)
    r2 = jnp.sum(jnp.where(oh2, before2 + carry + c1, 0.0), axis=1, keepdims=True)
    rank_ref[...] = jnp.where(lane == SEL_E1, r1, jnp.where(lane == SEL_E2, r2, 0.0)).astype(jnp.int32)
    total = carry + c1 + c2
    carry_ref[...] = total
    counts_ref[...] = total.astype(jnp.int32)


def moe_rank(sel, *, tm=512):
    tokens = sel.shape[0]
    tm = min(tm, tokens)
    ltri = (jnp.arange(tm)[:, None] > jnp.arange(tm)[None, :]).astype(BF16)
    return pl.pallas_call(
        _moe_rank_kernel,
        grid=(tokens // tm,),
        in_specs=[pl.BlockSpec((tm, LANES), lambda i: (i, 0)),
                  pl.BlockSpec((tm, tm), lambda i: (0, 0))],
        out_specs=[pl.BlockSpec((tm, LANES), lambda i: (i, 0)),
                   pl.BlockSpec((1, LANES), lambda i: (0, 0))],
        out_shape=[jax.ShapeDtypeStruct((tokens, LANES), jnp.int32),
                   jax.ShapeDtypeStruct((1, LANES), jnp.int32)],
        scratch_shapes=[pltpu.VMEM((1, LANES), F32)],
        compiler_params=_cparams(("arbitrary",)),
        name="moe_rank",
    )(sel, ltri)


def _row_copy(src_ref, src_row, dst_ref, dst_row, sem):
    return pltpu.make_async_copy(src_ref.at[pl.ds(src_row, 1)], dst_ref.at[pl.ds(dst_row, 1)], sem)


def _moe_dispatch_kernel(dest_ref, tail_ref, h_ref, xs_ref, hbuf, zero_ref, lsem, ssem, zsem, *, tm):
    step = pl.program_id(0)
    n_steps = pl.num_programs(0)
    base = step * (2 * tm)
    slot = step % 3

    def load(tile, buf_slot):
        return pltpu.make_async_copy(h_ref.at[pl.ds(pl.multiple_of(tile * tm, tm), tm)], hbuf.at[buf_slot],
                                     lsem.at[buf_slot])

    def wait_scatter(buf_slot):
        for _ in range(2):
            pltpu.make_async_copy(hbuf.at[buf_slot], xs_ref.at[pl.ds(0, tm)], ssem.at[buf_slot]).wait()

    @pl.when(step == 0)
    def _():
        load(0, 0).start()

        @pl.when(n_steps > 1)
        def _():
            load(1, 1).start()

    @pl.when(step == 0)
    def _():
        zero_ref[...] = jnp.zeros_like(zero_ref)

        def fill(tail):
            return pltpu.make_async_copy(zero_ref, xs_ref.at[pl.ds(pl.multiple_of(tail, 8), zero_ref.shape[0])], zsem)

        def start(e, carry):
            @pl.when(tail_ref[e] >= 0)
            def _():
                fill(tail_ref[e]).start()
            return carry

        def wait(e, carry):
            @pl.when(tail_ref[e] >= 0)
            def _():
                fill(tail_ref[e]).wait()
            return carry

        lax.fori_loop(0, tail_ref.shape[0], start, 0)
        lax.fori_loop(0, tail_ref.shape[0], wait, 0)

    load(step, slot).wait()

    def issue(r, carry):
        for s in range(2):
            _row_copy(hbuf.at[slot], r, xs_ref, dest_ref[base + 2 * r + s], ssem.at[slot]).start()
        return carry

    lax.fori_loop(0, tm, issue, 0, unroll=DMA_ISSUE_UNROLL)

    @pl.when(step >= 1)
    def _():
        wait_scatter((step + 2) % 3)

    @pl.when(step + 2 < n_steps)
    def _():
        load(step + 2, (step + 2) % 3).start()

    @pl.when(step == n_steps - 1)
    def _():
        wait_scatter(slot)


def moe_dispatch(h, dest, tails, rows, row_tile, *, tm=256):
    tokens, d = h.shape
    tm = min(tm, tokens)
    grid_spec = pltpu.PrefetchScalarGridSpec(
        num_scalar_prefetch=2,
        grid=(tokens // tm,),
        in_specs=[pl.BlockSpec(memory_space=pl.ANY)],
        out_specs=pl.BlockSpec(memory_space=pl.ANY),
        scratch_shapes=[pltpu.VMEM((3, tm, d), h.dtype), pltpu.VMEM((row_tile, d), h.dtype),
                        pltpu.SemaphoreType.DMA((3,)), pltpu.SemaphoreType.DMA((3,)),
                        pltpu.SemaphoreType.DMA(())],
    )
    return pl.pallas_call(
        functools.partial(_moe_dispatch_kernel, tm=tm),
        grid_spec=grid_spec,
        out_shape=jax.ShapeDtypeStruct((rows, d), h.dtype),
        compiler_params=_cparams(("arbitrary",)),
        name="moe_dispatch",
    )(dest, tails, h)


def _moe_expert_kernel(te_ref, nused_ref, xs_ref, wg_ref, wu_ref, wd_ref, ys_ref, wgb, wub, wdb):
    r = pl.program_id(0)
    used = r < nused_ref[0]
    changed = (r == 0) | (te_ref[r] != te_ref[jnp.maximum(r - 1, 0)])

    @pl.when(used & changed)
    def _():
        wgb[...] = wg_ref[...].astype(BF16)
        wub[...] = wu_ref[...].astype(BF16)
        wdb[...] = wd_ref[...].astype(BF16)

    @pl.when(used)
    def _():
        x = xs_ref[...].astype(BF16)
        gate_pre = _dot(x, wgb[...])
        hid = (gate_pre * jax.nn.sigmoid(gate_pre)) * _dot(x, wub[...])
        ys_ref[...] = _dot(hid.astype(BF16), wdb[...])

    @pl.when(jnp.logical_not(used))
    def _():
        ys_ref[...] = jnp.zeros_like(ys_ref)


def moe_experts(xs, tile_expert, n_used, wg, wu, wd, layer, *, tm):
    rows, d = xs.shape
    de = wg.shape[-1]
    grid_spec = pltpu.PrefetchScalarGridSpec(
        num_scalar_prefetch=2,
        grid=(rows // tm,),
        in_specs=[pl.BlockSpec((tm, d), lambda r, te, nu: (jnp.minimum(r, nu[0] - 1), 0)),
                  pl.BlockSpec((None, None, d, de), lambda r, te, nu: (layer, te[r], 0, 0)),
                  pl.BlockSpec((None, None, d, de), lambda r, te, nu: (layer, te[r], 0, 0)),
                  pl.BlockSpec((None, None, de, d), lambda r, te, nu: (layer, te[r], 0, 0))],
        out_specs=pl.BlockSpec((tm, d), lambda r, te, nu: (r, 0)),
        scratch_shapes=[pltpu.VMEM((d, de), BF16), pltpu.VMEM((d, de), BF16), pltpu.VMEM((de, d), BF16)],
    )
    return pl.pallas_call(
        _moe_expert_kernel,
        grid_spec=grid_spec,
        out_shape=jax.ShapeDtypeStruct((rows, d), F32),
        compiler_params=_cparams(("arbitrary",)),
        name="moe_experts",
    )(tile_expert, n_used, xs, wg, wu, wd)


def _moe_combine_kernel(dest_ref, x_ref, sel_ref, nw_ref, ys_ref, o_ref, buf_ref, sem, *, tm, normalize):
    step = pl.program_id(0)
    slot = step % 2

    def gather(tile, buf_slot):
        base = tile * (2 * tm)

        def issue(r, carry):
            for s in range(2):
                _row_copy(ys_ref, dest_ref[base + 2 * r + s], buf_ref.at[buf_slot, s], r, sem.at[buf_slot]).start()
            return carry

        lax.fori_loop(0, tm, issue, 0, unroll=DMA_ISSUE_UNROLL)

    @pl.when(step == 0)
    def _():
        gather(0, 0)

    @pl.when(step + 1 < pl.num_programs(0))
    def _():
        gather(step + 1, 1 - slot)

    for s in range(2):
        pltpu.make_async_copy(ys_ref.at[pl.ds(0, tm)], buf_ref.at[slot, s], sem.at[slot]).wait()
    sel = sel_ref[...]
    out = x_ref[...] + sel[:, SEL_G1:SEL_G1 + 1] * buf_ref[slot, 0] + sel[:, SEL_G2:SEL_G2 + 1] * buf_ref[slot, 1]
    o_ref[...] = _rms(out, nw_ref[...], NORM_EPS) if normalize else out


def moe_combine(x, sel, ys, dest, final_norm=None, *, tm=256):
    tokens, d = x.shape
    tm = min(tm, tokens)
    normalize = final_norm is not None
    nw = (final_norm if normalize else jnp.ones((d,), F32)).reshape(1, d)
    grid_spec = pltpu.PrefetchScalarGridSpec(
        num_scalar_prefetch=1,
        grid=(tokens // tm,),
        in_specs=[pl.BlockSpec((tm, d), lambda i, *_: (i, 0)),
                  pl.BlockSpec((tm, LANES), lambda i, *_: (i, 0)),
                  pl.BlockSpec((1, d), lambda i, *_: (0, 0)),
                  pl.BlockSpec(memory_space=pl.ANY)],
        out_specs=pl.BlockSpec((tm, d), lambda i, *_: (i, 0)),
        scratch_shapes=[pltpu.VMEM((2, 2, tm, d), F32), pltpu.SemaphoreType.DMA((2,))],
    )
    return pl.pallas_call(
        functools.partial(_moe_combine_kernel, tm=tm, normalize=normalize),
        grid_spec=grid_spec,
        out_shape=jax.ShapeDtypeStruct((tokens, d), F32),
        compiler_params=_cparams(("arbitrary",)),
        name="moe_combine",
    )(dest, x, sel, nw, ys)


def moe_block(x, norm_w, w_router, b_router, wg, wu, wd, layer, final_norm=None, *, tm=MOE_ROW_TILE):
    tokens, d = x.shape
    n_exp = wg.shape[1]
    h, sel = moe_router(x, norm_w, w_router, b_router)
    rank, counts = moe_rank(sel)
    padded = (counts[0, :n_exp] + (tm - 1)) // tm * tm
    ends = jnp.cumsum(padded)
    starts = ends - padded
    experts = sel[:, SEL_E1:SEL_E2 + 1].astype(jnp.int32)
    start_of = jnp.sum(jnp.where(experts[..., None] == jnp.arange(n_exp, dtype=jnp.int32), starts, 0), axis=-1)
    dest = (start_of + rank[:, SEL_E1:SEL_E2 + 1]).reshape(-1)
    rows = 2 * tokens + n_exp * tm
    tile_start = jnp.arange(rows // tm, dtype=jnp.int32) * tm
    tile_expert = jnp.minimum(jnp.sum(tile_start[:, None] >= ends[None, :], axis=1), n_exp - 1).astype(jnp.int32)
    n_used = (ends[-1] // tm).astype(jnp.int32).reshape(1)
    tails = jnp.concatenate([jnp.where(padded > 0, ends - tm, -1),
                             jnp.where(tile_start >= ends[-1], tile_start, -1)]).astype(jnp.int32)
    xs = moe_dispatch(h, dest, tails, rows, tm)
    ys = moe_experts(xs, tile_expert, n_used, wg, wu, wd, layer, tm=tm)
    return moe_combine(x, sel, ys, dest, final_norm)


def _proj_weights(w_in_l, w_vres_l):
    d = w_in_l.shape[0]
    mla0 = RW_COLS + DF_COLS
    vres = jnp.zeros((d, RW_V_RANK), F32) if w_vres_l is None else w_vres_l
    part_f = [w_in_l[:, 3 * RW_DIM:RW_COLS],
              vres, jnp.zeros((d, LANES - RW_V_RANK), F32),
              w_in_l[:, mla0 + ML_Q_RANK + ML_KV_RANK:mla0 + ML_COLS], jnp.zeros((d, LANES - ML_ROPE), F32)]
    part_a = [w_in_l[:, :3 * RW_DIM],
              w_in_l[:, RW_COLS:RW_COLS + DF_COLS],
              w_in_l[:, mla0 + ML_Q_RANK:mla0 + ML_Q_RANK + ML_KV_RANK],
              jnp.zeros((d, OFF_MQ - OFF_MKV - ML_KV_RANK), F32),
              w_in_l[:, mla0:mla0 + ML_Q_RANK]]
    return jnp.concatenate(part_f, axis=1).astype(BF16), jnp.concatenate(part_a, axis=1).astype(BF16)


def _pad_rows(w, rows, at=0):
    out = jnp.zeros((rows, w.shape[1]), w.dtype)
    return lax.dynamic_update_slice(out, w, (at, 0))


def kernel(x, mem, positions, rel_bias, final_norm, norm_mix, w_in, w_in_vres, w_out, tm_mu, tm_mu_vres, tm_w0, tm_w2, tm_a0, tm_a2, tm_v0, tm_v2, tm_g2, tm_k_k, tm_k_a, tm_r_k, tm_ln_w, tm_ln_b, da_lq1, da_lk1, da_lq2, da_lk2, da_subln, mla_q_norm, mla_wq_b, mla_kv_norm, mla_wkv_b, norm_cross, norm_mem, ca_wq, ca_wkv, ca_wo, norm_ffn, moe_w_group, moe_b_group, moe_w_expert, moe_b_expert, moe_w_gate, moe_w_up, moe_w_down):
    batch, seq, d = x.shape
    tokens = batch * seq
    depth = norm_mix.shape[0]
    xf = x.reshape(tokens, d)
    memf = mem.reshape(-1, d)
    positions = positions.astype(jnp.int32)

    head_of_lane = jnp.arange(RW_DIM) // RW_HEAD_DIM
    seg = (head_of_lane[:, None] == jnp.arange(LANES)[None, :]).astype(BF16)
    seg_t = seg.T
    row = lambda v: v.reshape(1, -1)

    v_first = None
    for l in range(depth):
        w_f, w_a = _proj_weights(w_in[l], None if l == 0 else w_in_vres[l - 1])
        pa, proj = norm_matmul(xf, norm_mix[l], w_a, w_f, out_dtype=BF16, tm=1024, tn=PROJ_A_COLS // 3)

        mu = tm_mu[l]
        prm = dict(mu_r=row(mu[:RW_DIM]), mu_k=row(mu[RW_DIM:2 * RW_DIM]), mu_v=row(mu[2 * RW_DIM:3 * RW_DIM]),
                   mu_l=row(mu[3 * RW_DIM:]), w0=row(tm_w0[l]), a0=row(tm_a0[l]),
                   w2=_pad_rows(tm_w2[l], LANES, 0), a2=_pad_rows(tm_a2[l], LANES, RW_W_RANK),
                   g2=tm_g2[l].astype(BF16), k_k=row(tm_k_k[l]), k_a=row(tm_k_a[l]), r_k=row(tm_r_k[l]),
                   seg=seg, seg_t=seg_t)
        if l > 0:
            prm.update(mu_vr=jnp.pad(row(tm_mu_vres[l - 1]), ((0, 0), (0, LANES - RW_V_RANK))),
                       v0=row(tm_v0[l - 1]), v2=_pad_rows(tm_v2[l - 1], LANES, 0))
        r, lw, k, v, kap, beta, gate, bonus = rwkv_prep(pa, proj, batch, v_first, prm)
        if l == 0:
            v_first = v
        o = rwkv_scan(r, lw, k, v, kap, beta, batch)
        y_a = rwkv_post(o, bonus, gate, tm_ln_w[l], tm_ln_b[l], seg, seg_t)

        lambda_init = 0.8 - 0.6 * math.exp(-0.3 * l)
        lam = (jnp.exp(jnp.sum(da_lq1[l] * da_lk1[l])) - jnp.exp(jnp.sum(da_lq2[l] * da_lk2[l])) + lambda_init)
        y_b = diff_attention(pa, positions, rel_bias, lam, lambda_init, da_subln[l])

        wq = mla_wq_b[l].reshape(ML_Q_RANK, ML_HEADS, ML_NOPE + ML_ROPE)
        wq_pe = jnp.pad(wq[:, :, ML_NOPE:], ((0, 0), (0, 0), (0, LANES - ML_ROPE)))
        wq_all = jnp.concatenate([wq[:, :, :ML_NOPE].reshape(ML_Q_RANK, -1),
                                  wq_pe.reshape(ML_Q_RANK, -1)], axis=1).astype(BF16)
        wkv = mla_wkv_b[l].reshape(ML_KV_RANK, ML_HEADS, ML_NOPE + ML_V)
        wkv = jnp.concatenate([wkv[:, :, :ML_NOPE].reshape(ML_KV_RANK, -1),
                               wkv[:, :, ML_NOPE:].reshape(ML_KV_RANK, -1)], axis=1).astype(BF16)
        qf, kf, v_mla = mla_prep(pa, proj, positions, mla_q_norm[l], mla_kv_norm[l], wq_all, wkv)
        y_c = mla_attention(qf, kf, v_mla, batch)

        wo = w_out[l].astype(BF16)
        xf = matmul_res([y_a, y_b, y_c],
                        [wo[:RW_DIM], wo[RW_DIM:RW_DIM + DF_DIM], wo[RW_DIM + DF_DIM:]], xf)

        kv_mem = norm_matmul(memf, norm_mem[l], ca_wkv[l].astype(BF16), out_dtype=BF16)
        xf = cross_block(xf, batch, norm_cross[l], ca_wq[l].astype(BF16), kv_mem, ca_wo[l].astype(BF16))

        w_router = jnp.concatenate(
            [moe_w_expert[l], moe_w_group[l], jnp.zeros((d, LANES - MOE_EXPERTS - MOE_GROUPS), F32)], axis=1)
        b_router = jnp.concatenate(
            [moe_b_expert[l], moe_b_group[l], jnp.zeros((LANES - MOE_EXPERTS - MOE_GROUPS,), F32)]).reshape(1, LANES)
        xf = moe_block(xf, norm_ffn[l], w_router, b_router, moe_w_gate, moe_w_up, moe_w_down, l,
                       final_norm if l == depth - 1 else None)

    return xf.reshape(batch, seq, d)
```

```python
import functools
import math

import jax
import jax.numpy as jnp
from jax import lax
from jax.experimental import pallas as pl
from jax.experimental.pallas import tpu as pltpu

F32 = jnp.float32
BF16 = jnp.bfloat16

NORM_EPS = 1e-6
ROPE_THETA = 10000.0

RW_HEADS = 16
RW_HEAD_DIM = 64
RW_DIM = RW_HEADS * RW_HEAD_DIM
RW_W_RANK = 64
RW_A_RANK = 64
RW_G_RANK = 128
RW_V_RANK = 32
RW_LORA = RW_W_RANK + RW_A_RANK + RW_G_RANK
RW_LN_EPS = 64e-5
RW_COLS = 3 * RW_DIM + RW_LORA

DF_HEADS = 4
DF_HEAD_DIM = 64
DF_V_DIM = 2 * DF_HEAD_DIM
DF_QK = DF_HEADS * 2 * DF_HEAD_DIM
DF_DIM = DF_HEADS * DF_V_DIM
DF_COLS = 2 * DF_QK + DF_DIM
DF_SUBLN_EPS = 1e-5

ML_HEADS = 4
ML_Q_RANK = 384
ML_KV_RANK = 256
ML_NOPE = 128
ML_ROPE = 64
ML_V = 128
ML_DIM = ML_HEADS * ML_V
ML_COLS = ML_Q_RANK + ML_KV_RANK + ML_ROPE

REL_BUCKETS = 32
REL_MAX_DIST = 128

CA_HEADS = 4
CA_HEAD_DIM = 128
CA_DIM = CA_HEADS * CA_HEAD_DIM

MOE_GROUPS = 4
MOE_PER_GROUP = 8
MOE_EXPERTS = MOE_GROUPS * MOE_PER_GROUP

LANES = 128
SCAN_CHUNK = 64
SCAN_GROUP = 4
SCAN_BATCHES = 2
ATTN_TILE = 512
ONES_ROWS = 16
POST_ROW_BLOCKS = 4
DMA_ISSUE_UNROLL = 8
MOE_ROW_TILE = 256
VMEM_LIMIT = 56 * 1024 * 1024
NEG_BIG = -1e30

LOG2E = 1.4426950408889634

OFF_LORA = 0
OFF_VRES = OFF_LORA + RW_LORA
OFF_KPE = OFF_VRES + LANES
PROJ_F_COLS = OFF_KPE + LANES
OFF_R = 0
OFF_K = RW_DIM
OFF_V = 2 * RW_DIM
OFF_DQ = 3 * RW_DIM
OFF_DK = OFF_DQ + DF_QK
OFF_DV = OFF_DK + DF_QK
OFF_MKV = OFF_DV + DF_DIM
OFF_MQ = 13 * ML_Q_RANK
PROJ_A_COLS = OFF_MQ + ML_Q_RANK


def _cparams(sem, vmem=VMEM_LIMIT, flags=None):
    return pltpu.CompilerParams(dimension_semantics=sem, vmem_limit_bytes=vmem, flags=flags)


def _dot(a, b):
    return jnp.dot(a, b, preferred_element_type=F32)


def _dot_t(a, b):
    return lax.dot_general(a, b, (((1,), (1,)), ((), ())), preferred_element_type=F32)


def _split3(x):
    hi = x.astype(BF16)
    r1 = x - hi.astype(F32)
    mid = r1.astype(BF16)
    lo = (r1 - mid.astype(F32)).astype(BF16)
    return hi, mid, lo


def _dot_rhs01(x, ones_bf16):
    hi = x.astype(BF16)
    lo = (x - hi.astype(F32)).astype(BF16)
    return _dot(hi, ones_bf16) + _dot(lo, ones_bf16)


def _dot_x3(a, b):
    ah = a.astype(BF16)
    al = (a - ah.astype(F32)).astype(BF16)
    bh = b.astype(BF16)
    bl = (b - bh.astype(F32)).astype(BF16)
    return _dot(ah, bh) + _dot(ah, bl) + _dot(al, bh)


def _rms(x, w, eps):
    ms = jnp.mean(x * x, axis=-1, keepdims=True)
    return x * lax.rsqrt(ms + eps) * w


def _norm_matmul_kernel(*refs, eps, has_side):
    if has_side:
        x_ref, nw_ref, w_ref, ws_ref, o_ref, os_ref, xn_ref = refs
    else:
        x_ref, nw_ref, w_ref, o_ref, xn_ref = refs

    @pl.when(pl.program_id(1) == 0)
    def _():
        xn_ref[...] = _rms(x_ref[...], nw_ref[...], eps).astype(BF16)
        if has_side:
            os_ref[...] = _dot(xn_ref[...], ws_ref[...])

    o_ref[...] = _dot(xn_ref[...], w_ref[...]).astype(o_ref.dtype)


def norm_matmul(x, nw, w, w_side=None, *, out_dtype=F32, tm=512, tn=None, eps=NORM_EPS):
    m, d = x.shape
    n = w.shape[1]
    tm = min(tm, m)
    tn = n if tn is None else tn
    has_side = w_side is not None
    in_specs = [pl.BlockSpec((tm, d), lambda i, j: (i, 0)),
                pl.BlockSpec((1, d), lambda i, j: (0, 0)),
                pl.BlockSpec((d, tn), lambda i, j: (0, j))]
    out_specs = [pl.BlockSpec((tm, tn), lambda i, j: (i, j))]
    out_shape = [jax.ShapeDtypeStruct((m, n), out_dtype)]
    args = [x, nw.reshape(1, d), w]
    if has_side:
        ns = w_side.shape[1]
        in_specs.append(pl.BlockSpec((d, ns), lambda i, j: (0, 0)))
        out_specs.append(pl.BlockSpec((tm, ns), lambda i, j: (i, 0)))
        out_shape.append(jax.ShapeDtypeStruct((m, ns), F32))
        args.append(w_side)
    outs = pl.pallas_call(
        functools.partial(_norm_matmul_kernel, eps=eps, has_side=has_side),
        grid=(m // tm, n // tn),
        in_specs=in_specs,
        out_specs=out_specs,
        out_shape=out_shape,
        scratch_shapes=[pltpu.VMEM((tm, d), BF16)],
        compiler_params=_cparams(("parallel", "arbitrary")),
        name="norm_matmul",
    )(*args)
    return outs if has_side else outs[0]


def _matmul_res_kernel(*refs, n_a):
    a_refs, w_refs = refs[:n_a], refs[n_a:2 * n_a]
    res_ref, o_ref = refs[2 * n_a], refs[2 * n_a + 1]
    acc = res_ref[...]
    for a_ref, w_ref in zip(a_refs, w_refs):
        acc = acc + _dot(a_ref[...].astype(BF16), w_ref[...])
    o_ref[...] = acc


def matmul_res(a_list, w_list, res, *, tm=512, tn=2048):
    m, n = res.shape
    tm = min(tm, m)
    tn = min(tn, n)
    n_a = len(a_list)
    in_specs = ([pl.BlockSpec((tm, a.shape[1]), lambda i, j: (i, 0)) for a in a_list]
                + [pl.BlockSpec((w.shape[0], tn), lambda i, j: (0, j)) for w in w_list]
                + [pl.BlockSpec((tm, tn), lambda i, j: (i, j))])
    return pl.pallas_call(
        functools.partial(_matmul_res_kernel, n_a=n_a),
        grid=(m // tm, n // tn),
        in_specs=in_specs,
        out_specs=pl.BlockSpec((tm, tn), lambda i, j: (i, j)),
        out_shape=jax.ShapeDtypeStruct((m, n), F32),
        compiler_params=_cparams(("parallel", "arbitrary")),
        name="matmul_res",
    )(*a_list, *w_list, res)


def _softplus(z):
    return jnp.maximum(z, 0.0) + jnp.log(1.0 + jnp.exp(-jnp.abs(z)))


def _rwkv_prep_kernel(*refs, has_vres):
    if has_vres:
        (pr_ref, pk_ref, pv_ref, pl_ref, pvr_ref, vfirst_ref,
         mu_r, mu_k, mu_v, mu_l, mu_vr, w0, w2, a0, a2, g2, v0, v2,
         k_k, k_a, r_k, seg, seg_t,
         r_o, lw_o, k_o, v_o, kap_o, beta_o, g_o, bonus_o,
         last_r, last_k, last_v, last_l, last_vr) = refs
    else:
        (pr_ref, pk_ref, pv_ref, pl_ref,
         mu_r, mu_k, mu_v, mu_l, w0, w2, a0, a2, g2,
         k_k, k_a, r_k, seg, seg_t,
         r_o, lw_o, k_o, v_o, kap_o, beta_o, g_o, bonus_o,
         last_r, last_k, last_v, last_l) = refs
    t = pl.program_id(1)

    def shifted(p_ref, last_ref, mu_ref):
        p = p_ref[...].astype(F32)
        n = p.shape[0]
        carried = jnp.where(t == 0, 0.0, last_ref[0:1, :])
        row = lax.broadcasted_iota(jnp.int32, p.shape, 0)
        prev = jnp.where(row == 0, carried, pltpu.roll(p, 1, axis=0))
        last_ref[0:1, :] = p[n - 1:n, :]
        return p + mu_ref[...] * (prev - p)

    r = shifted(pr_ref, last_r, mu_r)
    k = shifted(pk_ref, last_k, mu_k)
    v = shifted(pv_ref, last_v, mu_v)
    lora = shifted(pl_ref, last_l, mu_l)
    wl = lora[:, :LANES]
    gl = lora[:, LANES:]

    lane = lax.broadcasted_iota(jnp.int32, wl.shape, 1)
    wl_t = jnp.where(lane < RW_W_RANK, jnp.tanh(wl), 0.0)
    al = jnp.where(lane >= RW_W_RANK, wl, 0.0)
    w_log = -_softplus(-(w0[...] + _dot_x3(wl_t, w2[...]))) - 0.5
    lw_o[...] = -jnp.exp(w_log)
    a = jax.nn.sigmoid(a0[...] + _dot_x3(al, a2[...]))
    g_o[...] = _dot(jax.nn.sigmoid(gl).astype(BF16), g2[...]).astype(g_o.dtype)

    segm, segm_t = seg[...], seg_t[...]

    def head_sum(x):
        return _dot_rhs01(_dot_rhs01(x, segm), segm_t)

    kk = k * k_k[...]
    kk = kk * lax.rsqrt(jnp.maximum(head_sum(kk * kk), 1e-24))
    k = k * (1.0 + (a - 1.0) * k_a[...])
    if has_vres:
        vr = shifted(pvr_ref, last_vr, mu_vr)
        mix = jax.nn.sigmoid(v0[...] + _dot_x3(vr, v2[...]))
        v = v + (vfirst_ref[...] - v) * mix
    r_o[...] = r.astype(r_o.dtype)
    k_o[...] = k.astype(k_o.dtype)
    v_o[...] = v.astype(v_o.dtype)
    kap_o[...] = kk.astype(kap_o.dtype)
    beta_o[...] = (kk * a).astype(beta_o.dtype)
    bonus_o[...] = (head_sum(r * k * r_k[...]) * v).astype(bonus_o.dtype)


def rwkv_prep(pa, proj, batch, vfirst, prm, *, tt=512):
    tokens = proj.shape[0]
    seq = tokens // batch
    tt = min(tt, seq)
    nt = seq // tt
    has_vres = vfirst is not None
    d = RW_DIM

    def rows(width, col):
        return pl.BlockSpec((tt, width), lambda b, t, col=col: (b * nt + t, col))

    def full(shape):
        return pl.BlockSpec(shape, lambda b, t: (0, 0))

    in_specs = [rows(d, OFF_R // d), rows(d, OFF_K // d), rows(d, OFF_V // d),
                rows(RW_LORA, OFF_LORA // RW_LORA)]
    args = [pa, pa, pa, proj]
    if has_vres:
        in_specs += [rows(LANES, OFF_VRES // LANES), rows(d, 0)]
        args += [proj, vfirst]
    names = ["mu_r", "mu_k", "mu_v", "mu_l"] + (["mu_vr"] if has_vres else []) + ["w0", "w2", "a0", "a2", "g2"]
    names += (["v0", "v2"] if has_vres else []) + ["k_k", "k_a", "r_k", "seg", "seg_t"]
    for nm in names:
        in_specs.append(full(prm[nm].shape))
        args.append(prm[nm])
    out_spec = pl.BlockSpec((tt, d), lambda b, t: (b * nt + t, 0))
    scratch = [pltpu.VMEM((8, d), F32)] * 3 + [pltpu.VMEM((8, RW_LORA), F32)]
    if has_vres:
        scratch.append(pltpu.VMEM((8, LANES), F32))
    return pl.pallas_call(
        functools.partial(_rwkv_prep_kernel, has_vres=has_vres),
        grid=(batch, nt),
        in_specs=in_specs,
        out_specs=[out_spec] * 8,
        out_shape=[jax.ShapeDtypeStruct((tokens, d), F32 if i == 1 else BF16) for i in range(8)],
        scratch_shapes=scratch,
        compiler_params=_cparams(("arbitrary", "arbitrary")),
        name="rwkv_prep",
    )(*args)


def _rwkv_scan_kernel(r_ref, lw_ref, k_ref, v_ref, kap_ref, beta_ref, tril_ref, bmask_ref,
                      o_ref, ht_ref):
    @pl.when(pl.program_id(1) == 0)
    def _():
        ht_ref[...] = jnp.zeros_like(ht_ref)

    n_batch, c, d = lw_ref.shape
    w = ht_ref.shape[1]
    g = w // RW_HEAD_DIM
    bmask = bmask_ref[...]
    bmask_b = bmask.astype(BF16)
    tril3 = tril_ref[...]
    t_idx = lax.broadcasted_iota(jnp.int32, (c, w), 0)
    s_idx = lax.broadcasted_iota(jnp.int32, (c, w), 1) % c
    strict = t_idx > s_idx
    incl = t_idx >= s_idx
    n_sq = int(math.log2(c))

    def stack(x):
        return jnp.concatenate([x.astype(BF16)] * g, axis=0) * bmask_b

    sls = [(bi, slice(None), slice(lo, lo + w)) for bi in range(n_batch) for lo in range(0, d, w)]
    groups = range(len(sls))
    lw = [lw_ref[sl] for sl in sls]
    cum = [_dot(tril3, jnp.concatenate(_split3(x), axis=0)) for x in lw]
    total = [x[c - 1:c, :] for x in cum]
    ar = [jnp.concatenate([-kap_ref[sls[gi]] * jnp.exp(cum[gi] - lw[gi]), r_ref[sls[gi]] * jnp.exp(cum[gi])],
                          axis=0).astype(BF16) for gi in groups]
    p_inv = [jnp.exp(-x) for x in cum]
    b_s = [stack(beta_ref[sls[gi]] * p_inv[gi]) for gi in groups]
    k_s = [stack(k_ref[sls[gi]] * p_inv[gi]) for gi in groups]
    v_n = [v_ref[sl] for sl in sls]
    v_s = [stack(x) for x in v_n]

    arb = [_dot_t(ar[gi], b_s[gi]) for gi in groups]
    ark = [_dot_t(ar[gi], k_s[gi]) for gi in groups]
    ab = [jnp.where(strict, m[:c], 0.0) for m in arb]
    rb = [jnp.where(incl, m[c:], 0.0).astype(BF16) for m in arb]
    akrk = [jnp.concatenate([jnp.where(strict, m[:c], 0.0), jnp.where(incl, m[c:], 0.0)], axis=0).astype(BF16)
            for m in ark]

    ht = [ht_ref[gi] for gi in groups]
    base = [_dot_t(ar[gi], ht[gi].astype(BF16)) + _dot(akrk[gi], v_s[gi]) for gi in groups]
    x = [m[:c] for m in base]
    lp = ab
    for i in range(n_sq):
        lpb = [m.astype(BF16) for m in lp]
        x = [x[gi] + _dot(lpb[gi], stack(x[gi])) for gi in groups]
        if i < n_sq - 1:
            lp = [_dot(lpb[gi], stack(lp[gi])) for gi in groups]
    for gi in groups:
        o_ref[sls[gi]] = base[gi][c:] + _dot(rb[gi], stack(x[gi]))

    for gi in groups:
        p_rem = jnp.exp(total[gi] - cum[gi])
        z = jnp.concatenate([beta_ref[sls[gi]] * p_rem, k_ref[sls[gi]] * p_rem], axis=0).astype(BF16)
        uv_t = jnp.concatenate([x[gi], v_n[gi].astype(F32)], axis=0).T.astype(BF16)
        ht_ref[gi] = ht[gi] * jnp.exp(total[gi]) + bmask * _dot(uv_t, z)


def rwkv_scan(r, lw, k, v, kap, beta, batch):
    tokens, d = r.shape
    seq = tokens // batch
    c = min(SCAN_CHUNK, seq)
    nc = seq // c
    gw = SCAN_GROUP * RW_HEAD_DIM
    rr = SCAN_GROUP * c
    assert c == RW_HEAD_DIM, "the stacking mask doubles as the head-block mask of the state"
    bb = math.gcd(batch, SCAN_BATCHES)
    tril = jnp.tile((jnp.arange(c)[:, None] >= jnp.arange(c)[None, :]).astype(BF16), (1, 3))
    bmask = (jnp.arange(rr)[:, None] // c == jnp.arange(gw)[None, :] // RW_HEAD_DIM).astype(F32)
    blk = pl.BlockSpec((bb, c, d), lambda b, i: (b, i, 0))
    as3d = lambda a: a.reshape(batch, seq, d)
    out = pl.pallas_call(
        _rwkv_scan_kernel,
        grid=(batch // bb, nc),
        in_specs=[blk] * 6 + [pl.BlockSpec((c, 3 * c), lambda b, i: (0, 0)),
                              pl.BlockSpec((rr, gw), lambda b, i: (0, 0))],
        out_specs=blk,
        out_shape=jax.ShapeDtypeStruct((batch, seq, d), F32),
        scratch_shapes=[pltpu.VMEM((bb * (d // gw), gw, gw), F32)],
        compiler_params=_cparams(("arbitrary", "arbitrary")),
        name="rwkv_scan",
    )(as3d(r), as3d(lw), as3d(k), as3d(v), as3d(kap), as3d(beta), tril, bmask)
    return out.reshape(tokens, d)


def _rwkv_post_kernel(o_ref, bonus_ref, g_ref, lnw_ref, lnb_ref, seg, seg_t, y_ref):
    segm, segm_t = seg[...], seg_t[...]

    def head_means(xs):
        sums = [_dot_rhs01(x, segm) for x in xs]
        return [_dot_rhs01(s, segm_t) * (1.0 / RW_HEAD_DIM) for s in sums]

    rows = o_ref.shape[0] // POST_ROW_BLOCKS
    blocks = [slice(i * rows, (i + 1) * rows) for i in range(POST_ROW_BLOCKS)]
    o = [o_ref[b, :] for b in blocks]
    dlt = [x - m for x, m in zip(o, head_means(o))]
    var = head_means([x * x for x in dlt])
    for b, x, v in zip(blocks, dlt, var):
        y = x * lax.rsqrt(v + RW_LN_EPS) * lnw_ref[...] + lnb_ref[...]
        y_ref[b, :] = ((y + bonus_ref[b, :]) * g_ref[b, :]).astype(y_ref.dtype)


def rwkv_post(o, bonus, g, ln_w, ln_b, seg, seg_t, *, tm=512):
    tokens, d = o.shape
    tm = min(tm, tokens)
    blk = pl.BlockSpec((tm, d), lambda i: (i, 0))
    vec = pl.BlockSpec((1, d), lambda i: (0, 0))
    return pl.pallas_call(
        _rwkv_post_kernel,
        grid=(tokens // tm,),
        in_specs=[blk, blk, blk, vec, vec,
                  pl.BlockSpec(seg.shape, lambda i: (0, 0)), pl.BlockSpec(seg_t.shape, lambda i: (0, 0))],
        out_specs=blk,
        out_shape=jax.ShapeDtypeStruct((tokens, d), BF16),
        compiler_params=_cparams(("parallel",)),
        name="rwkv_post",
    )(o, bonus, g, ln_w.reshape(1, d), ln_b.reshape(1, d), seg, seg_t)


def _t5_thresholds():
    max_exact = REL_BUCKETS // 2
    thr = list(range(1, max_exact))
    n = max_exact
    for bucket in range(max_exact, REL_BUCKETS):
        while True:
            large = max_exact + int(math.log(max(n, max_exact) / max_exact)
                                    / math.log(REL_MAX_DIST / max_exact) * (REL_BUCKETS - max_exact))
            if min(large, REL_BUCKETS - 1) >= bucket:
                break
            n += 1
        thr.append(n)
    return thr


T5_THRESHOLDS = _t5_thresholds()
T5_FAR = T5_THRESHOLDS[-1]


def _softmax_tiles(s_list, c_list, states, vt_list):
    stats = []
    for s_t, c, (m_old, _) in zip(s_list, c_list, states):
        m_new = jnp.maximum(m_old, jnp.max(s_t, axis=0, keepdims=True) + c)
        stats.append((m_new, jnp.exp2(m_old - m_new), jnp.exp2(s_t - (m_new - c)).astype(BF16)))
    return tuple((m_new, alpha * acc + _dot(vt, p_t))
                 for (m_new, alpha, p_t), (_, acc), vt in zip(stats, states, vt_list))


def _transpose_into(vt_ref, v_ref, chunk, heads, width):
    seq = v_ref.shape[0]
    blk = width + ONES_ROWS
    for c in range(seq // chunk):
        cols = slice(c * chunk, (c + 1) * chunk)
        vt = v_ref[cols, :].astype(F32).T.astype(BF16)
        for h in range(heads):
            vt_ref[h * blk:h * blk + width, cols] = vt[h * width:(h + 1) * width]
            vt_ref[h * blk + width:(h + 1) * blk, cols] = jnp.ones((ONES_ROWS, chunk), BF16)


def _diff_attn_kernel(qfirst_ref, klast_ref, q_ref, k_ref, v_ref, qpos_ref, kpos_ref, subln_ref, table_ref, lam_ref,
                      o_ref, vt_ref, *, tq, tk, scale2, out_scale):
    b, i = pl.program_id(0), pl.program_id(1)
    nq = pl.num_programs(1)
    seq = k_ref.shape[0]
    nk = seq // tk
    w = DF_V_DIM
    vblk = w + ONES_ROWS

    @pl.when(i == 0)
    def _():
        _transpose_into(vt_ref, v_ref, tk, DF_HEADS, w)

    n_tiles = (i * tq + tq - 1) // tk + 1
    qf = qfirst_ref[b * nq + i]
    n_far = lax.while_loop(
        lambda j: (j * tk + tk - 1 <= i * tq) & (qf - klast_ref[b * nk + jnp.minimum(j, nk - 1)] >= T5_FAR),
        lambda j: j + 1, jnp.int32(0))

    dist = lax.broadcasted_iota(jnp.int32, (1, LANES), 1)
    qpos = qpos_ref[...]
    q_idx = i * tq + lax.broadcasted_iota(jnp.int32, (tk, tq), 1)
    k_off = lax.broadcasted_iota(jnp.int32, (tk, tq), 0)
    lane = lax.broadcasted_iota(jnp.int32, (tq, w), 1)

    bias_rows, c_far, qm = [], [], []
    for h in range(DF_HEADS):
        bias_vec = jnp.full((1, LANES), table_ref[h], F32)
        for bucket, thr in enumerate(T5_THRESHOLDS, start=1):
            bias_vec = jnp.where(dist >= thr, table_ref[bucket * DF_HEADS + h], bias_vec)
        bias_rows.append(jnp.broadcast_to(bias_vec * LOG2E, (tk, LANES)))
        c_far.append(table_ref[(REL_BUCKETS - 1) * DF_HEADS + h] * LOG2E)
        qh = q_ref[:, h * w:(h + 1) * w].astype(F32) * scale2
        qm.append([jnp.where((lane >= mi * DF_HEAD_DIM) & (lane < (mi + 1) * DF_HEAD_DIM), qh, 0.0).astype(BF16)
                   for mi in range(2)])

    def tiles(j, h):
        off = pl.multiple_of(j * tk, tk)
        return k_ref[pl.ds(off, tk), h * w:(h + 1) * w], vt_ref[h * vblk:(h + 1) * vblk, pl.ds(off, tk)], off

    chains = [(h, mi) for h in range(DF_HEADS) for mi in range(2)]

    def far_body(j, st):
        kv = [tiles(j, h) for h in range(DF_HEADS)]
        s = [_dot_t(kv[h][0], qm[h][mi]) for h, mi in chains]
        return _softmax_tiles(s, [c_far[h] for h, _ in chains], st, [kv[h][1] for h, _ in chains])

    def near_body(j, st):
        off = pl.multiple_of(j * tk, tk)
        n = jnp.clip(qpos - kpos_ref[pl.ds(off, tk), :], 0, LANES - 1)
        keep = q_idx >= off + k_off
        kv = [tiles(j, h) for h in range(DF_HEADS)]
        bias = [jnp.concatenate(
            [jnp.take_along_axis(bias_rows[h], n[:, cb * LANES:(cb + 1) * LANES], axis=1)
             for cb in range(tq // LANES)], axis=1) for h in range(DF_HEADS)]
        s = [jnp.where(keep, _dot_t(kv[h][0], qm[h][mi]) + bias[h], NEG_BIG) for h, mi in chains]
        return _softmax_tiles(s, [0.0] * len(chains), st, [kv[h][1] for h, _ in chains])

    init = tuple((jnp.full((1, tq), NEG_BIG, F32), jnp.zeros((w + ONES_ROWS, tq), F32))
                 for _ in range(2 * DF_HEADS))
    st = lax.fori_loop(0, n_far, far_body, init)
    st = lax.fori_loop(n_far, n_tiles, near_body, st)
    for h in range(DF_HEADS):
        a0, a1 = st[2 * h][1], st[2 * h + 1][1]
        d_t = a0[:w] / a0[w:w + 1] - lam_ref[0] * (a1[:w] / a1[w:w + 1])
        ms = jnp.mean(d_t * d_t, axis=0, keepdims=True)
        y_t = d_t * lax.rsqrt(ms + DF_SUBLN_EPS) * (subln_ref[...] * out_scale)
        o_ref[:, h * w:(h + 1) * w] = y_t.T.astype(o_ref.dtype)


def diff_attention(pa, positions, rel_bias, lam, lambda_init, subln_w, *, tq=ATTN_TILE):
    batch, seq = positions.shape
    tokens = batch * seq
    tq = min(tq, seq)
    tk = tq
    nq, nk = seq // tq, seq // tk
    qfirst = positions[:, ::tq].reshape(-1)
    klast = positions[:, tk - 1::tk].reshape(-1)
    qpos = positions.reshape(batch, 1, seq)
    kpos = positions.reshape(batch, seq, 1)
    wd = DF_DIM
    grid_spec = pltpu.PrefetchScalarGridSpec(
        num_scalar_prefetch=2,
        grid=(batch, nq),
        in_specs=[pl.BlockSpec((tq, wd), lambda b, i, *_: (b * nq + i, OFF_DQ // wd)),
                  pl.BlockSpec((seq, wd), lambda b, i, *_: (b, OFF_DK // wd)),
                  pl.BlockSpec((seq, wd), lambda b, i, *_: (b, OFF_DV // wd)),
                  pl.BlockSpec((None, 1, tq), lambda b, i, *_: (b, 0, i)),
                  pl.BlockSpec((None, seq, 1), lambda b, i, *_: (b, 0, 0)),
                  pl.BlockSpec((DF_V_DIM, 1), lambda b, i, *_: (0, 0)),
                  pl.BlockSpec(memory_space=pltpu.SMEM),
                  pl.BlockSpec(memory_space=pltpu.SMEM)],
        out_specs=pl.BlockSpec((tq, wd), lambda b, i, *_: (b * nq + i, 0)),
        scratch_shapes=[pltpu.VMEM((DF_HEADS * (DF_V_DIM + ONES_ROWS), seq), BF16)],
    )
    return pl.pallas_call(
        functools.partial(_diff_attn_kernel, tq=tq, tk=tk, scale2=DF_HEAD_DIM ** -0.5 * LOG2E,
                          out_scale=1.0 - lambda_init),
        grid_spec=grid_spec,
        out_shape=jax.ShapeDtypeStruct((tokens, DF_DIM), BF16),
        compiler_params=_cparams(("arbitrary", "arbitrary")),
        name="diff_attention",
    )(qfirst, klast, pa, pa, pa, qpos, kpos, subln_w.reshape(DF_V_DIM, 1), rel_bias.reshape(-1), lam.reshape(1))


ML_QK_PAD = 2 * LANES


def _rope_block(x, cos, sin):
    half = ML_ROPE // 2
    lane = lax.broadcasted_iota(jnp.int32, x.shape, 1)
    rot = jnp.where(lane < half, -pltpu.roll(x, LANES - half, axis=1),
                    jnp.where(lane < ML_ROPE, pltpu.roll(x, half, axis=1), 0.0))
    return x * cos + rot * sin


def _mla_prep_kernel(*refs, qscale, make_rope):
    if make_rope:
        (mq_ref, mkv_ref, kpe_ref, pos_ref, freq_ref, qn_w, kvn_w, wq_ref, wkv_ref,
         qf_o, kf_o, v_o, cos_o, sin_o) = refs
        ang = pos_ref[...].astype(F32) * freq_ref[...]
        cos, sin = jnp.cos(ang), jnp.sin(ang)
        cos_o[...] = cos
        sin_o[...] = sin
    else:
        mq_ref, mkv_ref, kpe_ref, cos_ref, sin_ref, qn_w, kvn_w, wq_ref, wkv_ref, qf_o, kf_o, v_o = refs
        cos, sin = cos_ref[...], sin_ref[...]
    qc = _rms(mq_ref[...].astype(F32), qn_w[...], NORM_EPS).astype(BF16)
    q_all = _dot(qc, wq_ref[...]) * qscale
    kvc = _rms(mkv_ref[...].astype(F32), kvn_w[...], NORM_EPS).astype(BF16)
    kvb = _dot(kvc, wkv_ref[...])
    kpe = _rope_block(kpe_ref[...], cos, sin).astype(BF16)
    nope_w = ML_HEADS * ML_NOPE
    for h in range(ML_HEADS):
        lo = h * ML_QK_PAD
        qf_o[:, lo:lo + LANES] = q_all[:, h * LANES:(h + 1) * LANES].astype(BF16)
        qf_o[:, lo + LANES:lo + 2 * LANES] = _rope_block(
            q_all[:, nope_w + h * LANES:nope_w + (h + 1) * LANES], cos, sin).astype(BF16)
        kf_o[:, lo:lo + LANES] = kvb[:, h * LANES:(h + 1) * LANES].astype(BF16)
        kf_o[:, lo + LANES:lo + 2 * LANES] = kpe
    v_o[...] = kvb[:, nope_w:].astype(BF16)


def mla_prep(pa, pf, positions, rope, q_norm, kv_norm, wq_all, wkv, *, tm=512):
    tokens = pa.shape[0]
    tm = min(tm, tokens)
    make_rope = rope is None

    def full(a):
        return pl.BlockSpec(a.shape, lambda i: (0, 0))

    row_blk = pl.BlockSpec((tm, LANES), lambda i: (i, 0))
    if make_rope:
        half = ML_ROPE // 2
        inv_freq = ROPE_THETA ** (-jnp.arange(half, dtype=F32) / half)
        freq = jnp.concatenate([inv_freq, inv_freq, jnp.zeros((LANES - ML_ROPE,), F32)]).reshape(1, LANES)
        rope_args = [positions.reshape(tokens, 1), freq]
        rope_specs = [pl.BlockSpec((tm, 1), lambda i: (i, 0)), full(freq)]
    else:
        rope_args = list(rope)
        rope_specs = [row_blk, row_blk]
    qn_w = q_norm.reshape(1, -1)
    kvn_w = kv_norm.reshape(1, -1)
    wide = ML_HEADS * ML_QK_PAD
    out_specs = [pl.BlockSpec((tm, wide), lambda i: (i, 0)),
                 pl.BlockSpec((tm, wide), lambda i: (i, 0)),
                 pl.BlockSpec((tm, ML_DIM), lambda i: (i, 0))]
    out_shape = [jax.ShapeDtypeStruct((tokens, wide), BF16),
                 jax.ShapeDtypeStruct((tokens, wide), BF16),
                 jax.ShapeDtypeStruct((tokens, ML_DIM), BF16)]
    if make_rope:
        out_specs += [row_blk, row_blk]
        out_shape += [jax.ShapeDtypeStruct((tokens, LANES), F32)] * 2
    outs = pl.pallas_call(
        functools.partial(_mla_prep_kernel, qscale=(ML_NOPE + ML_ROPE) ** -0.5 * LOG2E, make_rope=make_rope),
        grid=(tokens // tm,),
        in_specs=[pl.BlockSpec((tm, ML_Q_RANK), lambda i: (i, OFF_MQ // ML_Q_RANK)),
                  pl.BlockSpec((tm, ML_KV_RANK), lambda i: (i, OFF_MKV // ML_KV_RANK)),
                  pl.BlockSpec((tm, LANES), lambda i: (i, OFF_KPE // LANES))] + rope_specs
                 + [full(qn_w), full(kvn_w), full(wq_all), full(wkv)],
        out_specs=out_specs,
        out_shape=out_shape,
        compiler_params=_cparams(("parallel",)),
        name="mla_prep",
    )(pa, pa, pf, *rope_args, qn_w, kvn_w, wq_all, wkv)
    if make_rope:
        return outs[0], outs[1], outs[2], (outs[3], outs[4])
    return outs[0], outs[1], outs[2], rope


def _mla_attn_kernel(q_ref, k_ref, v_ref, o_ref, vt_ref, *, tq, tk):
    i = pl.program_id(1)
    wq = ML_QK_PAD
    vblk = ML_V + ONES_ROWS

    @pl.when(i == 0)
    def _():
        _transpose_into(vt_ref, v_ref, tk, ML_HEADS, ML_V)

    n_tiles = (i * tq + tq - 1) // tk + 1
    n_full = (i * tq + 1) // tk
    q_idx = i * tq + lax.broadcasted_iota(jnp.int32, (tk, tq), 1)
    k_off = lax.broadcasted_iota(jnp.int32, (tk, tq), 0)
    qh = [q_ref[:, h * wq:(h + 1) * wq] for h in range(ML_HEADS)]

    def tiles(j, h):
        off = pl.multiple_of(j * tk, tk)
        return (k_ref[pl.ds(off, tk), h * wq:(h + 1) * wq],
                vt_ref[h * vblk:(h + 1) * vblk, pl.ds(off, tk)], off)

    heads = range(ML_HEADS)

    def full_body(j, st):
        kv = [tiles(j, h) for h in heads]
        s = [_dot_t(kv[h][0], qh[h]) for h in heads]
        return _softmax_tiles(s, [0.0] * ML_HEADS, st, [kv[h][1] for h in heads])

    def diag_body(j, st):
        kv = [tiles(j, h) for h in heads]
        keep = q_idx >= kv[0][2] + k_off
        s = [jnp.where(keep, _dot_t(kv[h][0], qh[h]), NEG_BIG) for h in heads]
        return _softmax_tiles(s, [0.0] * ML_HEADS, st, [kv[h][1] for h in heads])

    st = tuple((jnp.full((1, tq), NEG_BIG, F32), jnp.zeros((vblk, tq), F32)) for _ in range(ML_HEADS))
    st = lax.fori_loop(0, n_full, full_body, st)
    st = lax.fori_loop(n_full, n_tiles, diag_body, st)
    for h in range(ML_HEADS):
        acc = st[h][1]
        o_ref[:, h * ML_V:(h + 1) * ML_V] = (acc[:ML_V] / acc[ML_V:ML_V + 1]).T.astype(o_ref.dtype)


def mla_attention(qf, kf, v, batch, *, tq=ATTN_TILE):
    tokens = qf.shape[0]
    seq = tokens // batch
    tq = min(tq, seq)
    tk = tq
    nq = seq // tq
    wide = qf.shape[1]
    return pl.pallas_call(
        functools.partial(_mla_attn_kernel, tq=tq, tk=tk),
        grid=(batch, nq),
        in_specs=[pl.BlockSpec((tq, wide), lambda b, i: (b * nq + i, 0)),
                  pl.BlockSpec((seq, wide), lambda b, i: (b, 0)),
                  pl.BlockSpec((seq, ML_DIM), lambda b, i: (b, 0))],
        out_specs=pl.BlockSpec((tq, ML_DIM), lambda b, i: (b * nq + i, 0)),
        out_shape=jax.ShapeDtypeStruct((tokens, ML_DIM), BF16),
        scratch_shapes=[pltpu.VMEM((ML_HEADS * (ML_V + ONES_ROWS), seq), BF16)],
        compiler_params=_cparams(("arbitrary", "arbitrary")),
        name="mla_attention",
    )(qf, kf, v)


def _cross_kernel(x_ref, nw_ref, wq_ref, kv_ref, wo_ref, o_ref):
    x = x_ref[...]
    q = _dot(_rms(x, nw_ref[...], NORM_EPS).astype(BF16), wq_ref[...])
    kv = kv_ref[...]
    scale = CA_HEAD_DIM ** -0.5
    heads = [slice(hh * CA_HEAD_DIM, (hh + 1) * CA_HEAD_DIM) for hh in range(CA_HEADS)]
    s = [_dot_t(q[:, sl].astype(BF16), kv[:, sl]) * scale for sl in heads]
    p = [jnp.exp(x - jnp.max(x, axis=-1, keepdims=True)) for x in s]
    p = [(x / jnp.sum(x, axis=-1, keepdims=True)).astype(BF16) for x in p]
    outs = [_dot(x, kv[:, CA_DIM + sl.start:CA_DIM + sl.stop]) for x, sl in zip(p, heads)]
    o = jnp.concatenate(outs, axis=1).astype(BF16)
    o_ref[...] = x + _dot(o, wo_ref[...])


def cross_block(x, batch, norm_w, wq, kv, wo, *, tq=512):
    tokens, d = x.shape
    seq = tokens // batch
    tq = min(tq, seq)
    nq = seq // tq
    mem_len = kv.shape[0] // batch
    return pl.pallas_call(
        _cross_kernel,
        grid=(batch, nq),
        in_specs=[pl.BlockSpec((tq, d), lambda b, i: (b * nq + i, 0)),
                  pl.BlockSpec((1, d), lambda b, i: (0, 0)),
                  pl.BlockSpec(wq.shape, lambda b, i: (0, 0)),
                  pl.BlockSpec((mem_len, 2 * CA_DIM), lambda b, i: (b, 0)),
                  pl.BlockSpec(wo.shape, lambda b, i: (0, 0))],
        out_specs=pl.BlockSpec((tq, d), lambda b, i: (b * nq + i, 0)),
        out_shape=jax.ShapeDtypeStruct((tokens, d), F32),
        compiler_params=_cparams(("parallel", "parallel")),
        name="cross_block",
    )(x, norm_w.reshape(1, d), wq, kv, wo)


SEL_E1, SEL_E2, SEL_G1, SEL_G2 = 0, 1, 2, 3


def _route(logits, b_router):
    biased = logits + b_router
    lane = lax.broadcasted_iota(jnp.int32, logits.shape, 1)
    big = jnp.int32(LANES)

    def first_argmax(vals):
        mx = jnp.max(vals, axis=-1, keepdims=True)
        return jnp.min(jnp.where(vals == mx, lane, big), axis=-1, keepdims=True)

    def pick(vals, idx):
        return jnp.sum(jnp.where(lane == idx, vals, 0.0), axis=-1, keepdims=True)

    is_group = (lane >= MOE_EXPERTS) & (lane < MOE_EXPERTS + MOE_GROUPS)
    gl = jnp.where(is_group, logits, NEG_BIG)
    ge = jnp.exp(gl - jnp.max(gl, axis=-1, keepdims=True))
    gp = ge / jnp.sum(ge, axis=-1, keepdims=True)
    g_lane = first_argmax(jnp.where(is_group, biased, NEG_BIG))
    p_group = pick(gp, g_lane)
    lo = (g_lane - MOE_EXPERTS) * MOE_PER_GROUP
    in_group = (lane >= lo) & (lane < lo + MOE_PER_GROUP)
    eb = jnp.where(in_group, biased, NEG_BIG)
    i1 = first_argmax(eb)
    i2 = first_argmax(jnp.where(lane == i1, NEG_BIG, eb))
    l1, l2 = pick(logits, i1), pick(logits, i2)
    mx = jnp.maximum(l1, l2)
    e1, e2 = jnp.exp(l1 - mx), jnp.exp(l2 - mx)
    w1, w2 = e1 / (e1 + e2), e2 / (e1 + e2)
    return jnp.where(lane == SEL_E1, i1.astype(F32),
                     jnp.where(lane == SEL_E2, i2.astype(F32),
                               jnp.where(lane == SEL_G1, w1 * p_group,
                                         jnp.where(lane == SEL_G2, w2 * p_group, 0.0))))


def _router_kernel(x_ref, nw_ref, wr_ref, br_ref, h_ref, sel_ref):
    h = _rms(x_ref[...], nw_ref[...], NORM_EPS)
    h_ref[...] = h
    sel_ref[...] = _route(_dot_x3(h, wr_ref[...]), br_ref[...])


def moe_router(x, norm_w, w_router, b_router, *, tm=512):
    tokens, d = x.shape
    tm = min(tm, tokens)
    return pl.pallas_call(
        _router_kernel,
        grid=(tokens // tm,),
        in_specs=[pl.BlockSpec((tm, d), lambda i: (i, 0)),
                  pl.BlockSpec((1, d), lambda i: (0, 0)),
                  pl.BlockSpec((d, LANES), lambda i: (0, 0)),
                  pl.BlockSpec((1, LANES), lambda i: (0, 0))],
        out_specs=[pl.BlockSpec((tm, d), lambda i: (i, 0)),
                   pl.BlockSpec((tm, LANES), lambda i: (i, 0))],
        out_shape=[jax.ShapeDtypeStruct((tokens, d), F32),
                   jax.ShapeDtypeStruct((tokens, LANES), F32)],
        compiler_params=_cparams(("parallel",)),
        name="moe_router",
    )(x, norm_w.reshape(1, d), w_router, b_router)


def _moe_rank_kernel(sel_ref, ltri_ref, rank_ref, counts_ref, carry_ref):
    @pl.when(pl.program_id(0) == 0)
    def _():
        carry_ref[...] = jnp.zeros_like(carry_ref)

    sel = sel_ref[...]
    lane = lax.broadcasted_iota(jnp.int32, sel.shape, 1)
    lane_f = lane.astype(F32)
    oh1 = lane_f == sel[:, SEL_E1:SEL_E1 + 1]
    oh2 = lane_f == sel[:, SEL_E2:SEL_E2 + 1]
    f1, f2 = oh1.astype(F32), oh2.astype(F32)
    ltri = ltri_ref[...]
    before1 = _dot(ltri, f1.astype(BF16))
    before2 = _dot(ltri, f2.astype(BF16))
    c1 = jnp.sum(f1, axis=0, keepdims=True)
    c2 = jnp.sum(f2, axis=0, keepdims=True)
    carry = carry_ref[...]
    r1 = jnp.sum(jnp.where(oh1, before1 + carry, 0.0), axis=1, keepdims=True)
    r2 = jnp.sum(jnp.where(oh2, before2 + carry + c1, 0.0), axis=1, keepdims=True)
    rank_ref[...] = jnp.where(lane == SEL_E1, r1, jnp.where(lane == SEL_E2, r2, 0.0)).astype(jnp.int32)
    total = carry + c1 + c2
    carry_ref[...] = total
    counts_ref[...] = total.astype(jnp.int32)


def moe_rank(sel, *, tm=512):
    tokens = sel.shape[0]
    tm = min(tm, tokens)
    ltri = (jnp.arange(tm)[:, None] > jnp.arange(tm)[None, :]).astype(BF16)
    return pl.pallas_call(
        _moe_rank_kernel,
        grid=(tokens // tm,),
        in_specs=[pl.BlockSpec((tm, LANES), lambda i: (i, 0)),
                  pl.BlockSpec((tm, tm), lambda i: (0, 0))],
        out_specs=[pl.BlockSpec((tm, LANES), lambda i: (i, 0)),
                   pl.BlockSpec((1, LANES), lambda i: (0, 0))],
        out_shape=[jax.ShapeDtypeStruct((tokens, LANES), jnp.int32),
                   jax.ShapeDtypeStruct((1, LANES), jnp.int32)],
        scratch_shapes=[pltpu.VMEM((1, LANES), F32)],
        compiler_params=_cparams(("arbitrary",)),
        name="moe_rank",
    )(sel, ltri)


def _row_copy(src_ref, src_row, dst_ref, dst_row, sem):
    return pltpu.make_async_copy(src_ref.at[pl.ds(src_row, 1)], dst_ref.at[pl.ds(dst_row, 1)], sem)


def _moe_dispatch_kernel(dest_ref, tail_ref, h_ref, xs_ref, hbuf, zero_ref, lsem, ssem, zsem, *, tm):
    step = pl.program_id(0)
    n_steps = pl.num_programs(0)
    base = step * (2 * tm)
    slot = step % 3

    def load(tile, buf_slot):
        return pltpu.make_async_copy(h_ref.at[pl.ds(pl.multiple_of(tile * tm, tm), tm)], hbuf.at[buf_slot],
                                     lsem.at[buf_slot])

    def wait_scatter(buf_slot):
        for _ in range(2):
            pltpu.make_async_copy(hbuf.at[buf_slot], xs_ref.at[pl.ds(0, tm)], ssem.at[buf_slot]).wait()

    @pl.when(step == 0)
    def _():
        load(0, 0).start()

        @pl.when(n_steps > 1)
        def _():
            load(1, 1).start()

    @pl.when(step == 0)
    def _():
        zero_ref[...] = jnp.zeros_like(zero_ref)

        def fill(tail):
            return pltpu.make_async_copy(zero_ref, xs_ref.at[pl.ds(pl.multiple_of(tail, 8), zero_ref.shape[0])], zsem)

        def start(e, carry):
            @pl.when(tail_ref[e] >= 0)
            def _():
                fill(tail_ref[e]).start()
            return carry

        def wait(e, carry):
            @pl.when(tail_ref[e] >= 0)
            def _():
                fill(tail_ref[e]).wait()
            return carry

        lax.fori_loop(0, tail_ref.shape[0], start, 0)
        lax.fori_loop(0, tail_ref.shape[0], wait, 0)

    load(step, slot).wait()

    def issue(r, carry):
        for s in range(2):
            _row_copy(hbuf.at[slot], r, xs_ref, dest_ref[base + 2 * r + s], ssem.at[slot]).start()
        return carry

    lax.fori_loop(0, tm, issue, 0, unroll=DMA_ISSUE_UNROLL)

    @pl.when(step >= 1)
    def _():
        wait_scatter((step + 2) % 3)

    @pl.when(step + 2 < n_steps)
    def _():
        load(step + 2, (step + 2) % 3).start()

    @pl.when(step == n_steps - 1)
    def _():
        wait_scatter(slot)


def moe_dispatch(h, dest, tails, rows, row_tile, *, tm=256):
    tokens, d = h.shape
    tm = min(tm, tokens)
    grid_spec = pltpu.PrefetchScalarGridSpec(
        num_scalar_prefetch=2,
        grid=(tokens // tm,),
        in_specs=[pl.BlockSpec(memory_space=pl.ANY)],
        out_specs=pl.BlockSpec(memory_space=pl.ANY),
        scratch_shapes=[pltpu.VMEM((3, tm, d), h.dtype), pltpu.VMEM((row_tile, d), h.dtype),
                        pltpu.SemaphoreType.DMA((3,)), pltpu.SemaphoreType.DMA((3,)),
                        pltpu.SemaphoreType.DMA(())],
    )
    return pl.pallas_call(
        functools.partial(_moe_dispatch_kernel, tm=tm),
        grid_spec=grid_spec,
        out_shape=jax.ShapeDtypeStruct((rows, d), h.dtype),
        compiler_params=_cparams(("arbitrary",)),
        name="moe_dispatch",
    )(dest, tails, h)


def _moe_expert_kernel(te_ref, nused_ref, xs_ref, wg_ref, wu_ref, wd_ref, ys_ref, wgb, wub, wdb):
    r = pl.program_id(0)
    used = r < nused_ref[0]
    changed = (r == 0) | (te_ref[r] != te_ref[jnp.maximum(r - 1, 0)])

    @pl.when(used & changed)
    def _():
        wgb[...] = wg_ref[...].astype(BF16)
        wub[...] = wu_ref[...].astype(BF16)
        wdb[...] = wd_ref[...].astype(BF16)

    @pl.when(used)
    def _():
        x = xs_ref[...].astype(BF16)
        gate_pre = _dot(x, wgb[...])
        hid = (gate_pre * jax.nn.sigmoid(gate_pre)) * _dot(x, wub[...])
        ys_ref[...] = _dot(hid.astype(BF16), wdb[...])

    @pl.when(jnp.logical_not(used))
    def _():
        ys_ref[...] = jnp.zeros_like(ys_ref)


def moe_experts(xs, tile_expert, n_used, wg, wu, wd, layer, *, tm):
    rows, d = xs.shape
    de = wg.shape[-1]
    grid_spec = pltpu.PrefetchScalarGridSpec(
        num_scalar_prefetch=2,
        grid=(rows // tm,),
        in_specs=[pl.BlockSpec((tm, d), lambda r, te, nu: (jnp.minimum(r, nu[0] - 1), 0)),
                  pl.BlockSpec((None, None, d, de), lambda r, te, nu: (layer, te[r], 0, 0)),
                  pl.BlockSpec((None, None, d, de), lambda r, te, nu: (layer, te[r], 0, 0)),
                  pl.BlockSpec((None, None, de, d), lambda r, te, nu: (layer, te[r], 0, 0))],
        out_specs=pl.BlockSpec((tm, d), lambda r, te, nu: (r, 0)),
        scratch_shapes=[pltpu.VMEM((d, de), BF16), pltpu.VMEM((d, de), BF16), pltpu.VMEM((de, d), BF16)],
    )
    return pl.pallas_call(
        _moe_expert_kernel,
        grid_spec=grid_spec,
        out_shape=jax.ShapeDtypeStruct((rows, d), F32),
        compiler_params=_cparams(("arbitrary",)),
        name="moe_experts",
    )(tile_expert, n_used, xs, wg, wu, wd)


def _moe_combine_kernel(dest_ref, x_ref, sel_ref, nw_ref, ys_ref, o_ref, buf_ref, sem, *, tm, normalize):
    step = pl.program_id(0)
    slot = step % 2

    def gather(tile, buf_slot):
        base = tile * (2 * tm)

        def issue(r, carry):
            for s in range(2):
                _row_copy(ys_ref, dest_ref[base + 2 * r + s], buf_ref.at[buf_slot, s], r, sem.at[buf_slot]).start()
            return carry

        lax.fori_loop(0, tm, issue, 0, unroll=DMA_ISSUE_UNROLL)

    @pl.when(step == 0)
    def _():
        gather(0, 0)

    @pl.when(step + 1 < pl.num_programs(0))
    def _():
        gather(step + 1, 1 - slot)

    for s in range(2):
        pltpu.make_async_copy(ys_ref.at[pl.ds(0, tm)], buf_ref.at[slot, s], sem.at[slot]).wait()
    sel = sel_ref[...]
    out = x_ref[...] + sel[:, SEL_G1:SEL_G1 + 1] * buf_ref[slot, 0] + sel[:, SEL_G2:SEL_G2 + 1] * buf_ref[slot, 1]
    o_ref[...] = _rms(out, nw_ref[...], NORM_EPS) if normalize else out


def moe_combine(x, sel, ys, dest, final_norm=None, *, tm=256):
    tokens, d = x.shape
    tm = min(tm, tokens)
    normalize = final_norm is not None
    nw = (final_norm if normalize else jnp.ones((d,), F32)).reshape(1, d)
    grid_spec = pltpu.PrefetchScalarGridSpec(
        num_scalar_prefetch=1,
        grid=(tokens // tm,),
        in_specs=[pl.BlockSpec((tm, d), lambda i, *_: (i, 0)),
                  pl.BlockSpec((tm, LANES), lambda i, *_: (i, 0)),
                  pl.BlockSpec((1, d), lambda i, *_: (0, 0)),
                  pl.BlockSpec(memory_space=pl.ANY)],
        out_specs=pl.BlockSpec((tm, d), lambda i, *_: (i, 0)),
        scratch_shapes=[pltpu.VMEM((2, 2, tm, d), F32), pltpu.SemaphoreType.DMA((2,))],
    )
    return pl.pallas_call(
        functools.partial(_moe_combine_kernel, tm=tm, normalize=normalize),
        grid_spec=grid_spec,
        out_shape=jax.ShapeDtypeStruct((tokens, d), F32),
        compiler_params=_cparams(("arbitrary",)),
        name="moe_combine",
    )(dest, x, sel, nw, ys)


def moe_block(x, norm_w, w_router, b_router, wg, wu, wd, layer, final_norm=None, *, tm=MOE_ROW_TILE):
    tokens, d = x.shape
    n_exp = wg.shape[1]
    h, sel = moe_router(x, norm_w, w_router, b_router)
    rank, counts = moe_rank(sel)
    padded = (counts[0, :n_exp] + (tm - 1)) // tm * tm
    ends = jnp.cumsum(padded)
    starts = ends - padded
    experts = sel[:, SEL_E1:SEL_E2 + 1].astype(jnp.int32)
    start_of = jnp.sum(jnp.where(experts[..., None] == jnp.arange(n_exp, dtype=jnp.int32), starts, 0), axis=-1)
    dest = (start_of + rank[:, SEL_E1:SEL_E2 + 1]).reshape(-1)
    rows = 2 * tokens + n_exp * tm
    tile_start = jnp.arange(rows // tm, dtype=jnp.int32) * tm
    tile_expert = jnp.minimum(jnp.sum(tile_start[:, None] >= ends[None, :], axis=1), n_exp - 1).astype(jnp.int32)
    n_used = (ends[-1] // tm).astype(jnp.int32).reshape(1)
    tails = jnp.concatenate([jnp.where(padded > 0, ends - tm, -1),
                             jnp.where(tile_start >= ends[-1], tile_start, -1)]).astype(jnp.int32)
    xs = moe_dispatch(h, dest, tails, rows, tm)
    ys = moe_experts(xs, tile_expert, n_used, wg, wu, wd, layer, tm=tm)
    return moe_combine(x, sel, ys, dest, final_norm)


def _proj_weights(w_in_l, w_vres_l):
    d = w_in_l.shape[0]
    mla0 = RW_COLS + DF_COLS
    vres = jnp.zeros((d, RW_V_RANK), F32) if w_vres_l is None else w_vres_l
    part_f = [w_in_l[:, 3 * RW_DIM:RW_COLS],
              vres, jnp.zeros((d, LANES - RW_V_RANK), F32),
              w_in_l[:, mla0 + ML_Q_RANK + ML_KV_RANK:mla0 + ML_COLS], jnp.zeros((d, LANES - ML_ROPE), F32)]
    part_a = [w_in_l[:, :3 * RW_DIM],
              w_in_l[:, RW_COLS:RW_COLS + DF_COLS],
              w_in_l[:, mla0 + ML_Q_RANK:mla0 + ML_Q_RANK + ML_KV_RANK],
              jnp.zeros((d, OFF_MQ - OFF_MKV - ML_KV_RANK), F32),
              w_in_l[:, mla0:mla0 + ML_Q_RANK]]
    return jnp.concatenate(part_f, axis=1).astype(BF16), jnp.concatenate(part_a, axis=1).astype(BF16)


def _pad_rows(w, rows, at=0):
    out = jnp.zeros((rows, w.shape[1]), w.dtype)
    return lax.dynamic_update_slice(out, w, (at, 0))


def kernel(x, mem, positions, rel_bias, final_norm, norm_mix, w_in, w_in_vres, w_out, tm_mu, tm_mu_vres, tm_w0, tm_w2, tm_a0, tm_a2, tm_v0, tm_v2, tm_g2, tm_k_k, tm_k_a, tm_r_k, tm_ln_w, tm_ln_b, da_lq1, da_lk1, da_lq2, da_lk2, da_subln, mla_q_norm, mla_wq_b, mla_kv_norm, mla_wkv_b, norm_cross, norm_mem, ca_wq, ca_wkv, ca_wo, norm_ffn, moe_w_group, moe_b_group, moe_w_expert, moe_b_expert, moe_w_gate, moe_w_up, moe_w_down):
    batch, seq, d = x.shape
    tokens = batch * seq
    depth = norm_mix.shape[0]
    xf = x.reshape(tokens, d)
    memf = mem.reshape(-1, d)
    positions = positions.astype(jnp.int32)

    head_of_lane = jnp.arange(RW_DIM) // RW_HEAD_DIM
    seg = (head_of_lane[:, None] == jnp.arange(LANES)[None, :]).astype(BF16)
    seg_t = seg.T
    row = lambda v: v.reshape(1, -1)

    v_first = None
    rope = None
    for l in range(depth):
        w_f, w_a = _proj_weights(w_in[l], None if l == 0 else w_in_vres[l - 1])
        pa, proj = norm_matmul(xf, norm_mix[l], w_a, w_f, out_dtype=BF16, tm=1024, tn=PROJ_A_COLS // 3)

        mu = tm_mu[l]
        prm = dict(mu_r=row(mu[:RW_DIM]), mu_k=row(mu[RW_DIM:2 * RW_DIM]), mu_v=row(mu[2 * RW_DIM:3 * RW_DIM]),
                   mu_l=row(mu[3 * RW_DIM:]), w0=row(tm_w0[l]), a0=row(tm_a0[l]),
                   w2=_pad_rows(tm_w2[l], LANES, 0), a2=_pad_rows(tm_a2[l], LANES, RW_W_RANK),
                   g2=tm_g2[l].astype(BF16), k_k=row(tm_k_k[l]), k_a=row(tm_k_a[l]), r_k=row(tm_r_k[l]),
                   seg=seg, seg_t=seg_t)
        if l > 0:
            prm.update(mu_vr=jnp.pad(row(tm_mu_vres[l - 1]), ((0, 0), (0, LANES - RW_V_RANK))),
                       v0=row(tm_v0[l - 1]), v2=_pad_rows(tm_v2[l - 1], LANES, 0))
        r, lw, k, v, kap, beta, gate, bonus = rwkv_prep(pa, proj, batch, v_first, prm)
        if l == 0:
            v_first = v
        o = rwkv_scan(r, lw, k, v, kap, beta, batch)
        y_a = rwkv_post(o, bonus, gate, tm_ln_w[l], tm_ln_b[l], seg, seg_t)

        lambda_init = 0.8 - 0.6 * math.exp(-0.3 * l)
        lam = (jnp.exp(jnp.sum(da_lq1[l] * da_lk1[l])) - jnp.exp(jnp.sum(da_lq2[l] * da_lk2[l])) + lambda_init)
        y_b = diff_attention(pa, positions, rel_bias, lam, lambda_init, da_subln[l])

        wq = mla_wq_b[l].reshape(ML_Q_RANK, ML_HEADS, ML_NOPE + ML_ROPE)
        wq_pe = jnp.pad(wq[:, :, ML_NOPE:], ((0, 0), (0, 0), (0, LANES - ML_ROPE)))
        wq_all = jnp.concatenate([wq[:, :, :ML_NOPE].reshape(ML_Q_RANK, -1),
                                  wq_pe.reshape(ML_Q_RANK, -1)], axis=1).astype(BF16)
        wkv = mla_wkv_b[l].reshape(ML_KV_RANK, ML_HEADS, ML_NOPE + ML_V)
        wkv = jnp.concatenate([wkv[:, :, :ML_NOPE].reshape(ML_KV_RANK, -1),
                               wkv[:, :, ML_NOPE:].reshape(ML_KV_RANK, -1)], axis=1).astype(BF16)
        qf, kf, v_mla, rope = mla_prep(pa, proj, positions, rope, mla_q_norm[l], mla_kv_norm[l], wq_all, wkv)
        y_c = mla_attention(qf, kf, v_mla, batch)

        wo = w_out[l].astype(BF16)
        xf = matmul_res([y_a, y_b, y_c],
                        [wo[:RW_DIM], wo[RW_DIM:RW_DIM + DF_DIM], wo[RW_DIM + DF_DIM:]], xf)

        kv_mem = norm_matmul(memf, norm_mem[l], ca_wkv[l].astype(BF16), out_dtype=BF16)
        xf = cross_block(xf, batch, norm_cross[l], ca_wq[l].astype(BF16), kv_mem, ca_wo[l].astype(BF16))

        w_router = jnp.concatenate(
            [moe_w_expert[l], moe_w_group[l], jnp.zeros((d, LANES - MOE_EXPERTS - MOE_GROUPS), F32)], axis=1)
        b_router = jnp.concatenate(
            [moe_b_expert[l], moe_b_group[l], jnp.zeros((LANES - MOE_EXPERTS - MOE_GROUPS,), F32)]).reshape(1, LANES)
        xf = moe_block(xf, norm_ffn[l], w_router, b_router, moe_w_gate, moe_w_up, moe_w_down, l,
                       final_norm if l == depth - 1 else None)

    return xf.reshape(batch, seq, d)
```

```python
import functools
import math

import jax
import jax.numpy as jnp
from jax import lax
from jax.experimental import pallas as pl
from jax.experimental.pallas import tpu as pltpu

F32 = jnp.float32
BF16 = jnp.bfloat16

NORM_EPS = 1e-6
ROPE_THETA = 10000.0

RW_HEADS = 16
RW_HEAD_DIM = 64
RW_DIM = RW_HEADS * RW_HEAD_DIM
RW_W_RANK = 64
RW_A_RANK = 64
RW_G_RANK = 128
RW_V_RANK = 32
RW_LORA = RW_W_RANK + RW_A_RANK + RW_G_RANK
RW_LN_EPS = 64e-5
RW_COLS = 3 * RW_DIM + RW_LORA

DF_HEADS = 4
DF_HEAD_DIM = 64
DF_V_DIM = 2 * DF_HEAD_DIM
DF_QK = DF_HEADS * 2 * DF_HEAD_DIM
DF_DIM = DF_HEADS * DF_V_DIM
DF_COLS = 2 * DF_QK + DF_DIM
DF_SUBLN_EPS = 1e-5

ML_HEADS = 4
ML_Q_RANK = 384
ML_KV_RANK = 256
ML_NOPE = 128
ML_ROPE = 64
ML_V = 128
ML_DIM = ML_HEADS * ML_V
ML_COLS = ML_Q_RANK + ML_KV_RANK + ML_ROPE

REL_BUCKETS = 32
REL_MAX_DIST = 128

CA_HEADS = 4
CA_HEAD_DIM = 128
CA_DIM = CA_HEADS * CA_HEAD_DIM

MOE_GROUPS = 4
MOE_PER_GROUP = 8
MOE_EXPERTS = MOE_GROUPS * MOE_PER_GROUP

LANES = 128
SCAN_CHUNK = 64
SCAN_GROUP = 4
SCAN_BATCHES = 2
ATTN_TILE = 512
ONES_ROWS = 16
POST_ROW_BLOCKS = 4
DMA_ISSUE_UNROLL = 8
MOE_ROW_TILE = 256
VMEM_LIMIT = 56 * 1024 * 1024
NEG_BIG = -1e30

LOG2E = 1.4426950408889634

OFF_LORA = 0
OFF_VRES = OFF_LORA + RW_LORA
OFF_KPE = OFF_VRES + LANES
PROJ_F_COLS = OFF_KPE + LANES
OFF_R = 0
OFF_K = RW_DIM
OFF_V = 2 * RW_DIM
OFF_DQ = 3 * RW_DIM
OFF_DK = OFF_DQ + DF_QK
OFF_DV = OFF_DK + DF_QK
OFF_MKV = OFF_DV + DF_DIM
OFF_MQ = 13 * ML_Q_RANK
PROJ_A_COLS = OFF_MQ + ML_Q_RANK


def _cparams(sem, vmem=VMEM_LIMIT, flags=None):
    return pltpu.CompilerParams(dimension_semantics=sem, vmem_limit_bytes=vmem, flags=flags)


def _dot(a, b):
    return jnp.dot(a, b, preferred_element_type=F32)


def _dot_t(a, b):
    return lax.dot_general(a, b, (((1,), (1,)), ((), ())), preferred_element_type=F32)


def _split3(x):
    hi = x.astype(BF16)
    r1 = x - hi.astype(F32)
    mid = r1.astype(BF16)
    lo = (r1 - mid.astype(F32)).astype(BF16)
    return hi, mid, lo


def _dot_rhs01(x, ones_bf16):
    hi = x.astype(BF16)
    lo = (x - hi.astype(F32)).astype(BF16)
    return _dot(hi, ones_bf16) + _dot(lo, ones_bf16)


def _dot_x3(a, b):
    ah = a.astype(BF16)
    al = (a - ah.astype(F32)).astype(BF16)
    bh = b.astype(BF16)
    bl = (b - bh.astype(F32)).astype(BF16)
    return _dot(ah, bh) + _dot(ah, bl) + _dot(al, bh)


def _rms(x, w, eps):
    ms = jnp.mean(x * x, axis=-1, keepdims=True)
    return x * lax.rsqrt(ms + eps) * w


def _norm_matmul_kernel(*refs, eps, has_side):
    if has_side:
        x_ref, nw_ref, w_ref, ws_ref, o_ref, os_ref, xn_ref = refs
    else:
        x_ref, nw_ref, w_ref, o_ref, xn_ref = refs

    @pl.when(pl.program_id(1) == 0)
    def _():
        xn_ref[...] = _rms(x_ref[...], nw_ref[...], eps).astype(BF16)
        if has_side:
            os_ref[...] = _dot(xn_ref[...], ws_ref[...])

    o_ref[...] = _dot(xn_ref[...], w_ref[...]).astype(o_ref.dtype)


def norm_matmul(x, nw, w, w_side=None, *, out_dtype=F32, tm=512, tn=None, eps=NORM_EPS):
    m, d = x.shape
    n = w.shape[1]
    tm = min(tm, m)
    tn = n if tn is None else tn
    has_side = w_side is not None
    in_specs = [pl.BlockSpec((tm, d), lambda i, j: (i, 0)),
                pl.BlockSpec((1, d), lambda i, j: (0, 0)),
                pl.BlockSpec((d, tn), lambda i, j: (0, j))]
    out_specs = [pl.BlockSpec((tm, tn), lambda i, j: (i, j))]
    out_shape = [jax.ShapeDtypeStruct((m, n), out_dtype)]
    args = [x, nw.reshape(1, d), w]
    if has_side:
        ns = w_side.shape[1]
        in_specs.append(pl.BlockSpec((d, ns), lambda i, j: (0, 0)))
        out_specs.append(pl.BlockSpec((tm, ns), lambda i, j: (i, 0)))
        out_shape.append(jax.ShapeDtypeStruct((m, ns), F32))
        args.append(w_side)
    outs = pl.pallas_call(
        functools.partial(_norm_matmul_kernel, eps=eps, has_side=has_side),
        grid=(m // tm, n // tn),
        in_specs=in_specs,
        out_specs=out_specs,
        out_shape=out_shape,
        scratch_shapes=[pltpu.VMEM((tm, d), BF16)],
        compiler_params=_cparams(("parallel", "arbitrary")),
        name="norm_matmul",
    )(*args)
    return outs if has_side else outs[0]


def _softplus(z):
    return jnp.maximum(z, 0.0) + jnp.log(1.0 + jnp.exp(-jnp.abs(z)))


def _rwkv_prep_kernel(*refs, has_vres):
    if has_vres:
        (pr_ref, pk_ref, pv_ref, pl_ref, pvr_ref, vfirst_ref,
         mu_r, mu_k, mu_v, mu_l, mu_vr, w0, w2, a0, a2, g2, v0, v2,
         k_k, k_a, r_k, seg, seg_t,
         r_o, lw_o, k_o, v_o, kap_o, beta_o, g_o, bonus_o,
         last_r, last_k, last_v, last_l, last_vr) = refs
    else:
        (pr_ref, pk_ref, pv_ref, pl_ref,
         mu_r, mu_k, mu_v, mu_l, w0, w2, a0, a2, g2,
         k_k, k_a, r_k, seg, seg_t,
         r_o, lw_o, k_o, v_o, kap_o, beta_o, g_o, bonus_o,
         last_r, last_k, last_v, last_l) = refs
    t = pl.program_id(1)

    def shifted(p_ref, last_ref, mu_ref):
        p = p_ref[...].astype(F32)
        n = p.shape[0]
        carried = jnp.where(t == 0, 0.0, last_ref[0:1, :])
        row = lax.broadcasted_iota(jnp.int32, p.shape, 0)
        prev = jnp.where(row == 0, carried, pltpu.roll(p, 1, axis=0))
        last_ref[0:1, :] = p[n - 1:n, :]
        return p + mu_ref[...] * (prev - p)

    r = shifted(pr_ref, last_r, mu_r)
    k = shifted(pk_ref, last_k, mu_k)
    v = shifted(pv_ref, last_v, mu_v)
    lora = shifted(pl_ref, last_l, mu_l)
    wl = lora[:, :LANES]
    gl = lora[:, LANES:]

    lane = lax.broadcasted_iota(jnp.int32, wl.shape, 1)
    wl_t = jnp.where(lane < RW_W_RANK, jnp.tanh(wl), 0.0)
    al = jnp.where(lane >= RW_W_RANK, wl, 0.0)
    w_log = -_softplus(-(w0[...] + _dot_x3(wl_t, w2[...]))) - 0.5
    lw_o[...] = -jnp.exp(w_log)
    a = jax.nn.sigmoid(a0[...] + _dot_x3(al, a2[...]))
    g_o[...] = _dot(jax.nn.sigmoid(gl).astype(BF16), g2[...]).astype(g_o.dtype)

    segm, segm_t = seg[...], seg_t[...]

    def head_sum(x):
        return _dot_rhs01(_dot_rhs01(x, segm), segm_t)

    kk = k * k_k[...]
    kk = kk * lax.rsqrt(jnp.maximum(head_sum(kk * kk), 1e-24))
    k = k * (1.0 + (a - 1.0) * k_a[...])
    if has_vres:
        vr = shifted(pvr_ref, last_vr, mu_vr)
        mix = jax.nn.sigmoid(v0[...] + _dot_x3(vr, v2[...]))
        v = v + (vfirst_ref[...] - v) * mix
    r_o[...] = r.astype(r_o.dtype)
    k_o[...] = k.astype(k_o.dtype)
    v_o[...] = v.astype(v_o.dtype)
    kap_o[...] = kk.astype(kap_o.dtype)
    beta_o[...] = (kk * a).astype(beta_o.dtype)
    bonus_o[...] = (head_sum(r * k * r_k[...]) * v).astype(bonus_o.dtype)


def rwkv_prep(pa, proj, batch, vfirst, prm, *, tt=512):
    tokens = proj.shape[0]
    seq = tokens // batch
    tt = min(tt, seq)
    nt = seq // tt
    has_vres = vfirst is not None
    d = RW_DIM

    def rows(width, col):
        return pl.BlockSpec((tt, width), lambda b, t, col=col: (b * nt + t, col))

    def full(shape):
        return pl.BlockSpec(shape, lambda b, t: (0, 0))

    in_specs = [rows(d, OFF_R // d), rows(d, OFF_K // d), rows(d, OFF_V // d),
                rows(RW_LORA, OFF_LORA // RW_LORA)]
    args = [pa, pa, pa, proj]
    if has_vres:
        in_specs += [rows(LANES, OFF_VRES // LANES), rows(d, 0)]
        args += [proj, vfirst]
    names = ["mu_r", "mu_k", "mu_v", "mu_l"] + (["mu_vr"] if has_vres else []) + ["w0", "w2", "a0", "a2", "g2"]
    names += (["v0", "v2"] if has_vres else []) + ["k_k", "k_a", "r_k", "seg", "seg_t"]
    for nm in names:
        in_specs.append(full(prm[nm].shape))
        args.append(prm[nm])
    out_spec = pl.BlockSpec((tt, d), lambda b, t: (b * nt + t, 0))
    scratch = [pltpu.VMEM((8, d), F32)] * 3 + [pltpu.VMEM((8, RW_LORA), F32)]
    if has_vres:
        scratch.append(pltpu.VMEM((8, LANES), F32))
    return pl.pallas_call(
        functools.partial(_rwkv_prep_kernel, has_vres=has_vres),
        grid=(batch, nt),
        in_specs=in_specs,
        out_specs=[out_spec] * 8,
        out_shape=[jax.ShapeDtypeStruct((tokens, d), F32 if i == 1 else BF16) for i in range(8)],
        scratch_shapes=scratch,
        compiler_params=_cparams(("arbitrary", "arbitrary")),
        name="rwkv_prep",
    )(*args)


def _rwkv_scan_kernel(r_ref, lw_ref, k_ref, v_ref, kap_ref, beta_ref, tril_ref, bmask_ref,
                      o_ref, ht_ref):
    @pl.when(pl.program_id(1) == 0)
    def _():
        ht_ref[...] = jnp.zeros_like(ht_ref)

    n_batch, c, d = lw_ref.shape
    w = ht_ref.shape[1]
    g = w // RW_HEAD_DIM
    bmask = bmask_ref[...]
    bmask_b = bmask.astype(BF16)
    tril3 = tril_ref[...]
    t_idx = lax.broadcasted_iota(jnp.int32, (c, w), 0)
    s_idx = lax.broadcasted_iota(jnp.int32, (c, w), 1) % c
    strict = t_idx > s_idx
    incl = t_idx >= s_idx
    n_sq = int(math.log2(c))

    def stack(x):
        return jnp.concatenate([x.astype(BF16)] * g, axis=0) * bmask_b

    sls = [(bi, slice(None), slice(lo, lo + w)) for bi in range(n_batch) for lo in range(0, d, w)]
    groups = range(len(sls))
    lw = [lw_ref[sl] for sl in sls]
    cum = [_dot(tril3, jnp.concatenate(_split3(x), axis=0)) for x in lw]
    total = [x[c - 1:c, :] for x in cum]
    ar = [jnp.concatenate([-kap_ref[sls[gi]] * jnp.exp(cum[gi] - lw[gi]), r_ref[sls[gi]] * jnp.exp(cum[gi])],
                          axis=0).astype(BF16) for gi in groups]
    p_inv = [jnp.exp(-x) for x in cum]
    b_s = [stack(beta_ref[sls[gi]] * p_inv[gi]) for gi in groups]
    k_s = [stack(k_ref[sls[gi]] * p_inv[gi]) for gi in groups]
    v_n = [v_ref[sl] for sl in sls]
    v_s = [stack(x) for x in v_n]

    arb = [_dot_t(ar[gi], b_s[gi]) for gi in groups]
    ark = [_dot_t(ar[gi], k_s[gi]) for gi in groups]
    ab = [jnp.where(strict, m[:c], 0.0) for m in arb]
    rb = [jnp.where(incl, m[c:], 0.0).astype(BF16) for m in arb]
    akrk = [jnp.concatenate([jnp.where(strict, m[:c], 0.0), jnp.where(incl, m[c:], 0.0)], axis=0).astype(BF16)
            for m in ark]

    ht = [ht_ref[gi] for gi in groups]
    base = [_dot_t(ar[gi], ht[gi].astype(BF16)) + _dot(akrk[gi], v_s[gi]) for gi in groups]
    x = [m[:c] for m in base]
    lp = ab
    for i in range(n_sq):
        lpb = [m.astype(BF16) for m in lp]
        x = [x[gi] + _dot(lpb[gi], stack(x[gi])) for gi in groups]
        if i < n_sq - 1:
            lp = [_dot(lpb[gi], stack(lp[gi])) for gi in groups]
    for gi in groups:
        o_ref[sls[gi]] = base[gi][c:] + _dot(rb[gi], stack(x[gi]))

    for gi in groups:
        p_rem = jnp.exp(total[gi] - cum[gi])
        z = jnp.concatenate([beta_ref[sls[gi]] * p_rem, k_ref[sls[gi]] * p_rem], axis=0).astype(BF16)
        uv_t = jnp.concatenate([x[gi], v_n[gi].astype(F32)], axis=0).T.astype(BF16)
        ht_ref[gi] = ht[gi] * jnp.exp(total[gi]) + bmask * _dot(uv_t, z)


def rwkv_scan(r, lw, k, v, kap, beta, batch):
    tokens, d = r.shape
    seq = tokens // batch
    c = min(SCAN_CHUNK, seq)
    nc = seq // c
    gw = SCAN_GROUP * RW_HEAD_DIM
    rr = SCAN_GROUP * c
    assert c == RW_HEAD_DIM, "the stacking mask doubles as the head-block mask of the state"
    bb = math.gcd(batch, SCAN_BATCHES)
    tril = jnp.tile((jnp.arange(c)[:, None] >= jnp.arange(c)[None, :]).astype(BF16), (1, 3))
    bmask = (jnp.arange(rr)[:, None] // c == jnp.arange(gw)[None, :] // RW_HEAD_DIM).astype(F32)
    blk = pl.BlockSpec((bb, c, d), lambda b, i: (b, i, 0))
    as3d = lambda a: a.reshape(batch, seq, d)
    out = pl.pallas_call(
        _rwkv_scan_kernel,
        grid=(batch // bb, nc),
        in_specs=[blk] * 6 + [pl.BlockSpec((c, 3 * c), lambda b, i: (0, 0)),
                              pl.BlockSpec((rr, gw), lambda b, i: (0, 0))],
        out_specs=blk,
        out_shape=jax.ShapeDtypeStruct((batch, seq, d), F32),
        scratch_shapes=[pltpu.VMEM((bb * (d // gw), gw, gw), F32)],
        compiler_params=_cparams(("arbitrary", "arbitrary")),
        name="rwkv_scan",
    )(as3d(r), as3d(lw), as3d(k), as3d(v), as3d(kap), as3d(beta), tril, bmask)
    return out.reshape(tokens, d)


def _rwkv_post_kernel(o_ref, bonus_ref, g_ref, lnw_ref, lnb_ref, seg, seg_t, y_ref):
    segm, segm_t = seg[...], seg_t[...]

    def head_means(xs):
        sums = [_dot_rhs01(x, segm) for x in xs]
        return [_dot_rhs01(s, segm_t) * (1.0 / RW_HEAD_DIM) for s in sums]

    rows = o_ref.shape[0] // POST_ROW_BLOCKS
    blocks = [slice(i * rows, (i + 1) * rows) for i in range(POST_ROW_BLOCKS)]
    o = [o_ref[b, :] for b in blocks]
    dlt = [x - m for x, m in zip(o, head_means(o))]
    var = head_means([x * x for x in dlt])
    for b, x, v in zip(blocks, dlt, var):
        y = x * lax.rsqrt(v + RW_LN_EPS) * lnw_ref[...] + lnb_ref[...]
        y_ref[b, :] = ((y + bonus_ref[b, :]) * g_ref[b, :]).astype(y_ref.dtype)


def rwkv_post(o, bonus, g, ln_w, ln_b, seg, seg_t, *, tm=512):
    tokens, d = o.shape
    tm = min(tm, tokens)
    blk = pl.BlockSpec((tm, d), lambda i: (i, 0))
    vec = pl.BlockSpec((1, d), lambda i: (0, 0))
    return pl.pallas_call(
        _rwkv_post_kernel,
        grid=(tokens // tm,),
        in_specs=[blk, blk, blk, vec, vec,
                  pl.BlockSpec(seg.shape, lambda i: (0, 0)), pl.BlockSpec(seg_t.shape, lambda i: (0, 0))],
        out_specs=blk,
        out_shape=jax.ShapeDtypeStruct((tokens, d), BF16),
        compiler_params=_cparams(("parallel",)),
        name="rwkv_post",
    )(o, bonus, g, ln_w.reshape(1, d), ln_b.reshape(1, d), seg, seg_t)


def _t5_thresholds():
    max_exact = REL_BUCKETS // 2
    thr = list(range(1, max_exact))
    n = max_exact
    for bucket in range(max_exact, REL_BUCKETS):
        while True:
            large = max_exact + int(math.log(max(n, max_exact) / max_exact)
                                    / math.log(REL_MAX_DIST / max_exact) * (REL_BUCKETS - max_exact))
            if min(large, REL_BUCKETS - 1) >= bucket:
                break
            n += 1
        thr.append(n)
    return thr


T5_THRESHOLDS = _t5_thresholds()
T5_FAR = T5_THRESHOLDS[-1]


def _softmax_tiles(s_list, c_list, states, vt_list):
    stats = []
    for s_t, c, (m_old, _) in zip(s_list, c_list, states):
        m_new = jnp.maximum(m_old, jnp.max(s_t, axis=0, keepdims=True) + c)
        stats.append((m_new, jnp.exp2(m_old - m_new), jnp.exp2(s_t - (m_new - c)).astype(BF16)))
    return tuple((m_new, alpha * acc + _dot(vt, p_t))
                 for (m_new, alpha, p_t), (_, acc), vt in zip(stats, states, vt_list))


def _transpose_into(vt_ref, v_ref, chunk, heads, width):
    seq = v_ref.shape[0]
    blk = width + ONES_ROWS
    for c in range(seq // chunk):
        cols = slice(c * chunk, (c + 1) * chunk)
        vt = v_ref[cols, :].astype(F32).T.astype(BF16)
        for h in range(heads):
            vt_ref[h * blk:h * blk + width, cols] = vt[h * width:(h + 1) * width]
            vt_ref[h * blk + width:(h + 1) * blk, cols] = jnp.ones((ONES_ROWS, chunk), BF16)


def _diff_attn_kernel(qfirst_ref, klast_ref, q_ref, k_ref, v_ref, qpos_ref, kpos_ref, subln_ref, table_ref, lam_ref,
                      o_ref, vt_ref, *, tq, tk, scale2, out_scale):
    b, i = pl.program_id(0), pl.program_id(1)
    nq = pl.num_programs(1)
    seq = k_ref.shape[0]
    nk = seq // tk
    w = DF_V_DIM
    vblk = w + ONES_ROWS

    @pl.when(i == 0)
    def _():
        _transpose_into(vt_ref, v_ref, tk, DF_HEADS, w)

    n_tiles = (i * tq + tq - 1) // tk + 1
    qf = qfirst_ref[b * nq + i]
    n_far = lax.while_loop(
        lambda j: (j * tk + tk - 1 <= i * tq) & (qf - klast_ref[b * nk + jnp.minimum(j, nk - 1)] >= T5_FAR),
        lambda j: j + 1, jnp.int32(0))

    dist = lax.broadcasted_iota(jnp.int32, (1, LANES), 1)
    qpos = qpos_ref[...]
    q_idx = i * tq + lax.broadcasted_iota(jnp.int32, (tk, tq), 1)
    k_off = lax.broadcasted_iota(jnp.int32, (tk, tq), 0)
    lane = lax.broadcasted_iota(jnp.int32, (tq, w), 1)

    bias_rows, c_far, qm = [], [], []
    for h in range(DF_HEADS):
        bias_vec = jnp.full((1, LANES), table_ref[h], F32)
        for bucket, thr in enumerate(T5_THRESHOLDS, start=1):
            bias_vec = jnp.where(dist >= thr, table_ref[bucket * DF_HEADS + h], bias_vec)
        bias_rows.append(jnp.broadcast_to(bias_vec * LOG2E, (tk, LANES)))
        c_far.append(table_ref[(REL_BUCKETS - 1) * DF_HEADS + h] * LOG2E)
        qh = q_ref[:, h * w:(h + 1) * w].astype(F32) * scale2
        qm.append([jnp.where((lane >= mi * DF_HEAD_DIM) & (lane < (mi + 1) * DF_HEAD_DIM), qh, 0.0).astype(BF16)
                   for mi in range(2)])

    def tiles(j, h):
        off = pl.multiple_of(j * tk, tk)
        return k_ref[pl.ds(off, tk), h * w:(h + 1) * w], vt_ref[h * vblk:(h + 1) * vblk, pl.ds(off, tk)], off

    chains = [(h, mi) for h in range(DF_HEADS) for mi in range(2)]

    def far_body(j, st):
        kv = [tiles(j, h) for h in range(DF_HEADS)]
        s = [_dot_t(kv[h][0], qm[h][mi]) for h, mi in chains]
        return _softmax_tiles(s, [c_far[h] for h, _ in chains], st, [kv[h][1] for h, _ in chains])

    def near_body(j, st):
        off = pl.multiple_of(j * tk, tk)
        n = jnp.clip(qpos - kpos_ref[pl.ds(off, tk), :], 0, LANES - 1)
        keep = q_idx >= off + k_off
        kv = [tiles(j, h) for h in range(DF_HEADS)]
        bias = [jnp.concatenate(
            [jnp.take_along_axis(bias_rows[h], n[:, cb * LANES:(cb + 1) * LANES], axis=1)
             for cb in range(tq // LANES)], axis=1) for h in range(DF_HEADS)]
        s = [jnp.where(keep, _dot_t(kv[h][0], qm[h][mi]) + bias[h], NEG_BIG) for h, mi in chains]
        return _softmax_tiles(s, [0.0] * len(chains), st, [kv[h][1] for h, _ in chains])

    init = tuple((jnp.full((1, tq), NEG_BIG, F32), jnp.zeros((w + ONES_ROWS, tq), F32))
                 for _ in range(2 * DF_HEADS))
    st = lax.fori_loop(0, n_far, far_body, init)
    st = lax.fori_loop(n_far, n_tiles, near_body, st)
    for h in range(DF_HEADS):
        a0, a1 = st[2 * h][1], st[2 * h + 1][1]
        d_t = a0[:w] / a0[w:w + 1] - lam_ref[0] * (a1[:w] / a1[w:w + 1])
        ms = jnp.mean(d_t * d_t, axis=0, keepdims=True)
        y_t = d_t * lax.rsqrt(ms + DF_SUBLN_EPS) * (subln_ref[...] * out_scale)
        o_ref[:, h * w:(h + 1) * w] = y_t.T.astype(o_ref.dtype)


def diff_attention(pa, positions, rel_bias, lam, lambda_init, subln_w, *, tq=ATTN_TILE):
    batch, seq = positions.shape
    tokens = batch * seq
    tq = min(tq, seq)
    tk = tq
    nq, nk = seq // tq, seq // tk
    qfirst = positions[:, ::tq].reshape(-1)
    klast = positions[:, tk - 1::tk].reshape(-1)
    qpos = positions.reshape(batch, 1, seq)
    kpos = positions.reshape(batch, seq, 1)
    wd = DF_DIM
    grid_spec = pltpu.PrefetchScalarGridSpec(
        num_scalar_prefetch=2,
        grid=(batch, nq),
        in_specs=[pl.BlockSpec((tq, wd), lambda b, i, *_: (b * nq + i, OFF_DQ // wd)),
                  pl.BlockSpec((seq, wd), lambda b, i, *_: (b, OFF_DK // wd)),
                  pl.BlockSpec((seq, wd), lambda b, i, *_: (b, OFF_DV // wd)),
                  pl.BlockSpec((None, 1, tq), lambda b, i, *_: (b, 0, i)),
                  pl.BlockSpec((None, seq, 1), lambda b, i, *_: (b, 0, 0)),
                  pl.BlockSpec((DF_V_DIM, 1), lambda b, i, *_: (0, 0)),
                  pl.BlockSpec(memory_space=pltpu.SMEM),
                  pl.BlockSpec(memory_space=pltpu.SMEM)],
        out_specs=pl.BlockSpec((tq, wd), lambda b, i, *_: (b * nq + i, 0)),
        scratch_shapes=[pltpu.VMEM((DF_HEADS * (DF_V_DIM + ONES_ROWS), seq), BF16)],
    )
    return pl.pallas_call(
        functools.partial(_diff_attn_kernel, tq=tq, tk=tk, scale2=DF_HEAD_DIM ** -0.5 * LOG2E,
                          out_scale=1.0 - lambda_init),
        grid_spec=grid_spec,
        out_shape=jax.ShapeDtypeStruct((tokens, DF_DIM), BF16),
        compiler_params=_cparams(("arbitrary", "arbitrary")),
        name="diff_attention",
    )(qfirst, klast, pa, pa, pa, qpos, kpos, subln_w.reshape(DF_V_DIM, 1), rel_bias.reshape(-1), lam.reshape(1))


ML_QK_PAD = 2 * LANES


def _rope_block(x, cos, sin):
    half = ML_ROPE // 2
    lane = lax.broadcasted_iota(jnp.int32, x.shape, 1)
    rot = jnp.where(lane < half, -pltpu.roll(x, LANES - half, axis=1),
                    jnp.where(lane < ML_ROPE, pltpu.roll(x, half, axis=1), 0.0))
    return x * cos + rot * sin


def _mla_prep_kernel(mq_ref, mkv_ref, kpe_ref, pos_ref, qn_w, kvn_w, wq_ref, wkv_ref, freq_ref,
                     qf_o, kf_o, v_o, *, qscale):
    ang = pos_ref[...].astype(F32) * freq_ref[...]
    cos, sin = jnp.cos(ang), jnp.sin(ang)
    qc = _rms(mq_ref[...].astype(F32), qn_w[...], NORM_EPS).astype(BF16)
    q_all = _dot(qc, wq_ref[...]) * qscale
    kvc = _rms(mkv_ref[...].astype(F32), kvn_w[...], NORM_EPS).astype(BF16)
    kvb = _dot(kvc, wkv_ref[...])
    kpe = _rope_block(kpe_ref[...], cos, sin).astype(BF16)
    nope_w = ML_HEADS * ML_NOPE
    for h in range(ML_HEADS):
        lo = h * ML_QK_PAD
        qf_o[:, lo:lo + LANES] = q_all[:, h * LANES:(h + 1) * LANES].astype(BF16)
        qf_o[:, lo + LANES:lo + 2 * LANES] = _rope_block(
            q_all[:, nope_w + h * LANES:nope_w + (h + 1) * LANES], cos, sin).astype(BF16)
        kf_o[:, lo:lo + LANES] = kvb[:, h * LANES:(h + 1) * LANES].astype(BF16)
        kf_o[:, lo + LANES:lo + 2 * LANES] = kpe
    v_o[...] = kvb[:, nope_w:].astype(BF16)


def mla_prep(pa, pf, positions, q_norm, kv_norm, wq_all, wkv, *, tm=512):
    tokens = pa.shape[0]
    tm = min(tm, tokens)
    half = ML_ROPE // 2
    inv_freq = ROPE_THETA ** (-jnp.arange(half, dtype=F32) / half)
    freq = jnp.concatenate([inv_freq, inv_freq, jnp.zeros((LANES - ML_ROPE,), F32)]).reshape(1, LANES)

    def full(a):
        return pl.BlockSpec(a.shape, lambda i: (0, 0))

    qn_w = q_norm.reshape(1, -1)
    kvn_w = kv_norm.reshape(1, -1)
    wide = ML_HEADS * ML_QK_PAD
    return pl.pallas_call(
        functools.partial(_mla_prep_kernel, qscale=(ML_NOPE + ML_ROPE) ** -0.5 * LOG2E),
        grid=(tokens // tm,),
        in_specs=[pl.BlockSpec((tm, ML_Q_RANK), lambda i: (i, OFF_MQ // ML_Q_RANK)),
                  pl.BlockSpec((tm, ML_KV_RANK), lambda i: (i, OFF_MKV // ML_KV_RANK)),
                  pl.BlockSpec((tm, LANES), lambda i: (i, OFF_KPE // LANES)),
                  pl.BlockSpec((tm, 1), lambda i: (i, 0)),
                  full(qn_w), full(kvn_w), full(wq_all), full(wkv), full(freq)],
        out_specs=[pl.BlockSpec((tm, wide), lambda i: (i, 0)),
                   pl.BlockSpec((tm, wide), lambda i: (i, 0)),
                   pl.BlockSpec((tm, ML_DIM), lambda i: (i, 0))],
        out_shape=[jax.ShapeDtypeStruct((tokens, wide), BF16),
                   jax.ShapeDtypeStruct((tokens, wide), BF16),
                   jax.ShapeDtypeStruct((tokens, ML_DIM), BF16)],
        compiler_params=_cparams(("parallel",)),
        name="mla_prep",
    )(pa, pa, pf, positions.reshape(tokens, 1), qn_w, kvn_w, wq_all, wkv, freq)


def _mla_attn_kernel(q_ref, k_ref, v_ref, o_ref, vt_ref, *, tq, tk):
    i = pl.program_id(1)
    wq = ML_QK_PAD
    vblk = ML_V + ONES_ROWS

    @pl.when(i == 0)
    def _():
        _transpose_into(vt_ref, v_ref, tk, ML_HEADS, ML_V)

    n_tiles = (i * tq + tq - 1) // tk + 1
    n_full = (i * tq + 1) // tk
    q_idx = i * tq + lax.broadcasted_iota(jnp.int32, (tk, tq), 1)
    k_off = lax.broadcasted_iota(jnp.int32, (tk, tq), 0)
    qh = [q_ref[:, h * wq:(h + 1) * wq] for h in range(ML_HEADS)]

    def tiles(j, h):
        off = pl.multiple_of(j * tk, tk)
        return (k_ref[pl.ds(off, tk), h * wq:(h + 1) * wq],
                vt_ref[h * vblk:(h + 1) * vblk, pl.ds(off, tk)], off)

    heads = range(ML_HEADS)

    def full_body(j, st):
        kv = [tiles(j, h) for h in heads]
        s = [_dot_t(kv[h][0], qh[h]) for h in heads]
        return _softmax_tiles(s, [0.0] * ML_HEADS, st, [kv[h][1] for h in heads])

    def diag_body(j, st):
        kv = [tiles(j, h) for h in heads]
        keep = q_idx >= kv[0][2] + k_off
        s = [jnp.where(keep, _dot_t(kv[h][0], qh[h]), NEG_BIG) for h in heads]
        return _softmax_tiles(s, [0.0] * ML_HEADS, st, [kv[h][1] for h in heads])

    st = tuple((jnp.full((1, tq), NEG_BIG, F32), jnp.zeros((vblk, tq), F32)) for _ in range(ML_HEADS))
    st = lax.fori_loop(0, n_full, full_body, st)
    st = lax.fori_loop(n_full, n_tiles, diag_body, st)
    for h in range(ML_HEADS):
        acc = st[h][1]
        o_ref[:, h * ML_V:(h + 1) * ML_V] = (acc[:ML_V] / acc[ML_V:ML_V + 1]).T.astype(o_ref.dtype)


def mla_attention(qf, kf, v, batch, *, tq=ATTN_TILE):
    tokens = qf.shape[0]
    seq = tokens // batch
    tq = min(tq, seq)
    tk = tq
    nq = seq // tq
    wide = qf.shape[1]
    return pl.pallas_call(
        functools.partial(_mla_attn_kernel, tq=tq, tk=tk),
        grid=(batch, nq),
        in_specs=[pl.BlockSpec((tq, wide), lambda b, i: (b * nq + i, 0)),
                  pl.BlockSpec((seq, wide), lambda b, i: (b, 0)),
                  pl.BlockSpec((seq, ML_DIM), lambda b, i: (b, 0))],
        out_specs=pl.BlockSpec((tq, ML_DIM), lambda b, i: (b * nq + i, 0)),
        out_shape=jax.ShapeDtypeStruct((tokens, ML_DIM), BF16),
        scratch_shapes=[pltpu.VMEM((ML_HEADS * (ML_V + ONES_ROWS), seq), BF16)],
        compiler_params=_cparams(("arbitrary", "arbitrary")),
        name="mla_attention",
    )(qf, kf, v)


def _cross_kernel(*refs, n_mix):
    y_refs, w_refs = refs[:n_mix], refs[n_mix:2 * n_mix]
    x_ref, nw_ref, wq_ref, kv_ref, wo_ref, o_ref = refs[2 * n_mix:]
    x = x_ref[...]
    for y_ref, w_ref in zip(y_refs, w_refs):
        x = x + _dot(y_ref[...], w_ref[...])
    q = _dot(_rms(x, nw_ref[...], NORM_EPS).astype(BF16), wq_ref[...])
    kv = kv_ref[...]
    scale = CA_HEAD_DIM ** -0.5
    heads = [slice(hh * CA_HEAD_DIM, (hh + 1) * CA_HEAD_DIM) for hh in range(CA_HEADS)]
    s = [_dot_t(q[:, sl].astype(BF16), kv[:, sl]) * scale for sl in heads]
    p = [jnp.exp(x - jnp.max(x, axis=-1, keepdims=True)) for x in s]
    p = [(x / jnp.sum(x, axis=-1, keepdims=True)).astype(BF16) for x in p]
    outs = [_dot(x, kv[:, CA_DIM + sl.start:CA_DIM + sl.stop]) for x, sl in zip(p, heads)]
    o = jnp.concatenate(outs, axis=1).astype(BF16)
    o_ref[...] = x + _dot(o, wo_ref[...])


def mix_cross_block(y_list, w_list, x, batch, norm_w, wq, kv, wo, *, tq=512):
    tokens, d = x.shape
    seq = tokens // batch
    tq = min(tq, seq)
    nq = seq // tq
    mem_len = kv.shape[0] // batch
    rows = lambda width: pl.BlockSpec((tq, width), lambda b, i: (b * nq + i, 0))
    whole = lambda a: pl.BlockSpec(a.shape, lambda b, i: (0, 0))
    return pl.pallas_call(
        functools.partial(_cross_kernel, n_mix=len(y_list)),
        grid=(batch, nq),
        in_specs=[rows(y.shape[1]) for y in y_list] + [whole(w) for w in w_list]
                 + [rows(d), pl.BlockSpec((1, d), lambda b, i: (0, 0)), whole(wq),
                    pl.BlockSpec((mem_len, 2 * CA_DIM), lambda b, i: (b, 0)), whole(wo)],
        out_specs=rows(d),
        out_shape=jax.ShapeDtypeStruct((tokens, d), F32),
        compiler_params=_cparams(("parallel", "parallel")),
        name="mix_cross_block",
    )(*y_list, *w_list, x, norm_w.reshape(1, d), wq, kv, wo)


SEL_E1, SEL_E2, SEL_G1, SEL_G2 = 0, 1, 2, 3


def _route(logits, b_router):
    biased = logits + b_router
    lane = lax.broadcasted_iota(jnp.int32, logits.shape, 1)
    big = jnp.int32(LANES)

    def first_argmax(vals):
        mx = jnp.max(vals, axis=-1, keepdims=True)
        return jnp.min(jnp.where(vals == mx, lane, big), axis=-1, keepdims=True)

    def pick(vals, idx):
        return jnp.sum(jnp.where(lane == idx, vals, 0.0), axis=-1, keepdims=True)

    is_group = (lane >= MOE_EXPERTS) & (lane < MOE_EXPERTS + MOE_GROUPS)
    gl = jnp.where(is_group, logits, NEG_BIG)
    ge = jnp.exp(gl - jnp.max(gl, axis=-1, keepdims=True))
    gp = ge / jnp.sum(ge, axis=-1, keepdims=True)
    g_lane = first_argmax(jnp.where(is_group, biased, NEG_BIG))
    p_group = pick(gp, g_lane)
    lo = (g_lane - MOE_EXPERTS) * MOE_PER_GROUP
    in_group = (lane >= lo) & (lane < lo + MOE_PER_GROUP)
    eb = jnp.where(in_group, biased, NEG_BIG)
    i1 = first_argmax(eb)
    i2 = first_argmax(jnp.where(lane == i1, NEG_BIG, eb))
    l1, l2 = pick(logits, i1), pick(logits, i2)
    mx = jnp.maximum(l1, l2)
    e1, e2 = jnp.exp(l1 - mx), jnp.exp(l2 - mx)
    w1, w2 = e1 / (e1 + e2), e2 / (e1 + e2)
    return jnp.where(lane == SEL_E1, i1.astype(F32),
                     jnp.where(lane == SEL_E2, i2.astype(F32),
                               jnp.where(lane == SEL_G1, w1 * p_group,
                                         jnp.where(lane == SEL_G2, w2 * p_group, 0.0))))


def _router_kernel(x_ref, nw_ref, wr_ref, br_ref, h_ref, sel_ref):
    h = _rms(x_ref[...], nw_ref[...], NORM_EPS)
    h_ref[...] = h
    sel_ref[...] = _route(_dot_x3(h, wr_ref[...]), br_ref[...])


def moe_router(x, norm_w, w_router, b_router, *, tm=512):
    tokens, d = x.shape
    tm = min(tm, tokens)
    return pl.pallas_call(
        _router_kernel,
        grid=(tokens // tm,),
        in_specs=[pl.BlockSpec((tm, d), lambda i: (i, 0)),
                  pl.BlockSpec((1, d), lambda i: (0, 0)),
                  pl.BlockSpec((d, LANES), lambda i: (0, 0)),
                  pl.BlockSpec((1, LANES), lambda i: (0, 0))],
        out_specs=[pl.BlockSpec((tm, d), lambda i: (i, 0)),
                   pl.BlockSpec((tm, LANES), lambda i: (i, 0))],
        out_shape=[jax.ShapeDtypeStruct((tokens, d), F32),
                   jax.ShapeDtypeStruct((tokens, LANES), F32)],
        compiler_params=_cparams(("parallel",)),
        name="moe_router",
    )(x, norm_w.reshape(1, d), w_router, b_router)


def _moe_rank_kernel(sel_ref, ltri_ref, rank_ref, counts_ref, carry_ref):
    @pl.when(pl.program_id(0) == 0)
    def _():
        carry_ref[...] = jnp.zeros_like(carry_ref)

    sel = sel_ref[...]
    lane = lax.broadcasted_iota(jnp.int32, sel.shape, 1)
    lane_f = lane.astype(F32)
    oh1 = lane_f == sel[:, SEL_E1:SEL_E1 + 1]
    oh2 = lane_f == sel[:, SEL_E2:SEL_E2 + 1]
    f1, f2 = oh1.astype(F32), oh2.astype(F32)
    ltri = ltri_ref[...]
    before1 = _dot(ltri, f1.astype(BF16))
    before2 = _dot(ltri, f2.astype(BF16))
    c1 = jnp.sum(f1, axis=0, keepdims=True)
    c2 = jnp.sum(f2, axis=0, keepdims=True)
    carry = carry_ref[...]
    r1 = jnp.sum(jnp.where(oh1, before1 + carry, 0.0), axis=1, keepdims=True)
    r2 = jnp.sum(jnp.where(oh2, before2 + carry + c1, 0.0), axis=1, keepdims=True)
    rank_ref[...] = jnp.where(lane == SEL_E1, r1, jnp.where(lane == SEL_E2, r2, 0.0)).astype(jnp.int32)
    total = carry + c1 + c2
    carry_ref[...] = total
    counts_ref[...] = total.astype(jnp.int32)


def moe_rank(sel, *, tm=512):
    tokens = sel.shape[0]
    tm = min(tm, tokens)
    ltri = (jnp.arange(tm)[:, None] > jnp.arange(tm)[None, :]).astype(BF16)
    return pl.pallas_call(
        _moe_rank_kernel,
        grid=(tokens // tm,),
        in_specs=[pl.BlockSpec((tm, LANES), lambda i: (i, 0)),
                  pl.BlockSpec((tm, tm), lambda i: (0, 0))],
        out_specs=[pl.BlockSpec((tm, LANES), lambda i: (i, 0)),
                   pl.BlockSpec((1, LANES), lambda i: (0, 0))],
        out_shape=[jax.ShapeDtypeStruct((tokens, LANES), jnp.int32),
                   jax.ShapeDtypeStruct((1, LANES), jnp.int32)],
        scratch_shapes=[pltpu.VMEM((1, LANES), F32)],
        compiler_params=_cparams(("arbitrary",)),
        name="moe_rank",
    )(sel, ltri)


def _row_copy(src_ref, src_row, dst_ref, dst_row, sem):
    return pltpu.make_async_copy(src_ref.at[pl.ds(src_row, 1)], dst_ref.at[pl.ds(dst_row, 1)], sem)


def _moe_dispatch_kernel(dest_ref, tail_ref, h_ref, xs_ref, hbuf, zero_ref, lsem, ssem, zsem, *, tm):
    step = pl.program_id(0)
    n_steps = pl.num_programs(0)
    base = step * (2 * tm)
    slot = step % 3

    def load(tile, buf_slot):
        return pltpu.make_async_copy(h_ref.at[pl.ds(pl.multiple_of(tile * tm, tm), tm)], hbuf.at[buf_slot],
                                     lsem.at[buf_slot])

    def wait_scatter(buf_slot):
        for _ in range(2):
            pltpu.make_async_copy(hbuf.at[buf_slot], xs_ref.at[pl.ds(0, tm)], ssem.at[buf_slot]).wait()

    @pl.when(step == 0)
    def _():
        load(0, 0).start()

        @pl.when(n_steps > 1)
        def _():
            load(1, 1).start()

    @pl.when(step == 0)
    def _():
        zero_ref[...] = jnp.zeros_like(zero_ref)

        def fill(tail):
            return pltpu.make_async_copy(zero_ref, xs_ref.at[pl.ds(pl.multiple_of(tail, 8), zero_ref.shape[0])], zsem)

        def start(e, carry):
            @pl.when(tail_ref[e] >= 0)
            def _():
                fill(tail_ref[e]).start()
            return carry

        def wait(e, carry):
            @pl.when(tail_ref[e] >= 0)
            def _():
                fill(tail_ref[e]).wait()
            return carry

        lax.fori_loop(0, tail_ref.shape[0], start, 0)
        lax.fori_loop(0, tail_ref.shape[0], wait, 0)

    load(step, slot).wait()

    def issue(r, carry):
        for s in range(2):
            _row_copy(hbuf.at[slot], r, xs_ref, dest_ref[base + 2 * r + s], ssem.at[slot]).start()
        return carry

    lax.fori_loop(0, tm, issue, 0, unroll=DMA_ISSUE_UNROLL)

    @pl.when(step >= 1)
    def _():
        wait_scatter((step + 2) % 3)

    @pl.when(step + 2 < n_steps)
    def _():
        load(step + 2, (step + 2) % 3).start()

    @pl.when(step == n_steps - 1)
    def _():
        wait_scatter(slot)


def moe_dispatch(h, dest, tails, rows, row_tile, *, tm=256):
    tokens, d = h.shape
    tm = min(tm, tokens)
    grid_spec = pltpu.PrefetchScalarGridSpec(
        num_scalar_prefetch=2,
        grid=(tokens // tm,),
        in_specs=[pl.BlockSpec(memory_space=pl.ANY)],
        out_specs=pl.BlockSpec(memory_space=pl.ANY),
        scratch_shapes=[pltpu.VMEM((3, tm, d), h.dtype), pltpu.VMEM((row_tile, d), h.dtype),
                        pltpu.SemaphoreType.DMA((3,)), pltpu.SemaphoreType.DMA((3,)),
                        pltpu.SemaphoreType.DMA(())],
    )
    return pl.pallas_call(
        functools.partial(_moe_dispatch_kernel, tm=tm),
        grid_spec=grid_spec,
        out_shape=jax.ShapeDtypeStruct((rows, d), h.dtype),
        compiler_params=_cparams(("arbitrary",)),
        name="moe_dispatch",
    )(dest, tails, h)


def _moe_expert_kernel(te_ref, nused_ref, xs_ref, wg_ref, wu_ref, wd_ref, ys_ref, wgb, wub, wdb):
    r = pl.program_id(0)
    used = r < nused_ref[0]
    changed = (r == 0) | (te_ref[r] != te_ref[jnp.maximum(r - 1, 0)])

    @pl.when(used & changed)
    def _():
        wgb[...] = wg_ref[...].astype(BF16)
        wub[...] = wu_ref[...].astype(BF16)
        wdb[...] = wd_ref[...].astype(BF16)

    @pl.when(used)
    def _():
        x = xs_ref[...].astype(BF16)
        gate_pre = _dot(x, wgb[...])
        hid = (gate_pre * jax.nn.sigmoid(gate_pre)) * _dot(x, wub[...])
        ys_ref[...] = _dot(hid.astype(BF16), wdb[...])

    @pl.when(jnp.logical_not(used))
    def _():
        ys_ref[...] = jnp.zeros_like(ys_ref)


def moe_experts(xs, tile_expert, n_used, wg, wu, wd, layer, *, tm):
    rows, d = xs.shape
    de = wg.shape[-1]
    grid_spec = pltpu.PrefetchScalarGridSpec(
        num_scalar_prefetch=2,
        grid=(rows // tm,),
        in_specs=[pl.BlockSpec((tm, d), lambda r, te, nu: (jnp.minimum(r, nu[0] - 1), 0)),
                  pl.BlockSpec((None, None, d, de), lambda r, te, nu: (layer, te[r], 0, 0)),
                  pl.BlockSpec((None, None, d, de), lambda r, te, nu: (layer, te[r], 0, 0)),
                  pl.BlockSpec((None, None, de, d), lambda r, te, nu: (layer, te[r], 0, 0))],
        out_specs=pl.BlockSpec((tm, d), lambda r, te, nu: (r, 0)),
        scratch_shapes=[pltpu.VMEM((d, de), BF16), pltpu.VMEM((d, de), BF16), pltpu.VMEM((de, d), BF16)],
    )
    return pl.pallas_call(
        _moe_expert_kernel,
        grid_spec=grid_spec,
        out_shape=jax.ShapeDtypeStruct((rows, d), F32),
        compiler_params=_cparams(("arbitrary",)),
        name="moe_experts",
    )(tile_expert, n_used, xs, wg, wu, wd)


def _moe_combine_kernel(dest_ref, x_ref, sel_ref, nw_ref, ys_ref, o_ref, buf_ref, sem, *, tm, normalize):
    step = pl.program_id(0)
    slot = step % 2

    def gather(tile, buf_slot):
        base = tile * (2 * tm)

        def issue(r, carry):
            for s in range(2):
                _row_copy(ys_ref, dest_ref[base + 2 * r + s], buf_ref.at[buf_slot, s], r, sem.at[buf_slot]).start()
            return carry

        lax.fori_loop(0, tm, issue, 0, unroll=DMA_ISSUE_UNROLL)

    @pl.when(step == 0)
    def _():
        gather(0, 0)

    @pl.when(step + 1 < pl.num_programs(0))
    def _():
        gather(step + 1, 1 - slot)

    for s in range(2):
        pltpu.make_async_copy(ys_ref.at[pl.ds(0, tm)], buf_ref.at[slot, s], sem.at[slot]).wait()
    sel = sel_ref[...]
    out = x_ref[...] + sel[:, SEL_G1:SEL_G1 + 1] * buf_ref[slot, 0] + sel[:, SEL_G2:SEL_G2 + 1] * buf_ref[slot, 1]
    o_ref[...] = _rms(out, nw_ref[...], NORM_EPS) if normalize else out


def moe_combine(x, sel, ys, dest, final_norm=None, *, tm=256):
    tokens, d = x.shape
    tm = min(tm, tokens)
    normalize = final_norm is not None
    nw = (final_norm if normalize else jnp.ones((d,), F32)).reshape(1, d)
    grid_spec = pltpu.PrefetchScalarGridSpec(
        num_scalar_prefetch=1,
        grid=(tokens // tm,),
        in_specs=[pl.BlockSpec((tm, d), lambda i, *_: (i, 0)),
                  pl.BlockSpec((tm, LANES), lambda i, *_: (i, 0)),
                  pl.BlockSpec((1, d), lambda i, *_: (0, 0)),
                  pl.BlockSpec(memory_space=pl.ANY)],
        out_specs=pl.BlockSpec((tm, d), lambda i, *_: (i, 0)),
        scratch_shapes=[pltpu.VMEM((2, 2, tm, d), F32), pltpu.SemaphoreType.DMA((2,))],
    )
    return pl.pallas_call(
        functools.partial(_moe_combine_kernel, tm=tm, normalize=normalize),
        grid_spec=grid_spec,
        out_shape=jax.ShapeDtypeStruct((tokens, d), F32),
        compiler_params=_cparams(("arbitrary",)),
        name="moe_combine",
    )(dest, x, sel, nw, ys)


def moe_block(x, norm_w, w_router, b_router, wg, wu, wd, layer, final_norm=None, *, tm=MOE_ROW_TILE):
    tokens, d = x.shape
    n_exp = wg.shape[1]
    h, sel = moe_router(x, norm_w, w_router, b_router)
    rank, counts = moe_rank(sel)
    padded = (counts[0, :n_exp] + (tm - 1)) // tm * tm
    ends = jnp.cumsum(padded)
    starts = ends - padded
    experts = sel[:, SEL_E1:SEL_E2 + 1].astype(jnp.int32)
    start_of = jnp.sum(jnp.where(experts[..., None] == jnp.arange(n_exp, dtype=jnp.int32), starts, 0), axis=-1)
    dest = (start_of + rank[:, SEL_E1:SEL_E2 + 1]).reshape(-1)
    rows = 2 * tokens + n_exp * tm
    tile_start = jnp.arange(rows // tm, dtype=jnp.int32) * tm
    tile_expert = jnp.minimum(jnp.sum(tile_start[:, None] >= ends[None, :], axis=1), n_exp - 1).astype(jnp.int32)
    n_used = (ends[-1] // tm).astype(jnp.int32).reshape(1)
    tails = jnp.concatenate([jnp.where(padded > 0, ends - tm, -1),
                             jnp.where(tile_start >= ends[-1], tile_start, -1)]).astype(jnp.int32)
    xs = moe_dispatch(h, dest, tails, rows, tm)
    ys = moe_experts(xs, tile_expert, n_used, wg, wu, wd, layer, tm=tm)
    return moe_combine(x, sel, ys, dest, final_norm)


def _proj_weights(w_in_l, w_vres_l):
    d = w_in_l.shape[0]
    mla0 = RW_COLS + DF_COLS
    vres = jnp.zeros((d, RW_V_RANK), F32) if w_vres_l is None else w_vres_l
    part_f = [w_in_l[:, 3 * RW_DIM:RW_COLS],
              vres, jnp.zeros((d, LANES - RW_V_RANK), F32),
              w_in_l[:, mla0 + ML_Q_RANK + ML_KV_RANK:mla0 + ML_COLS], jnp.zeros((d, LANES - ML_ROPE), F32)]
    part_a = [w_in_l[:, :3 * RW_DIM],
              w_in_l[:, RW_COLS:RW_COLS + DF_COLS],
              w_in_l[:, mla0 + ML_Q_RANK:mla0 + ML_Q_RANK + ML_KV_RANK],
              jnp.zeros((d, OFF_MQ - OFF_MKV - ML_KV_RANK), F32),
              w_in_l[:, mla0:mla0 + ML_Q_RANK]]
    return jnp.concatenate(part_f, axis=1).astype(BF16), jnp.concatenate(part_a, axis=1).astype(BF16)


def _pad_rows(w, rows, at=0):
    out = jnp.zeros((rows, w.shape[1]), w.dtype)
    return lax.dynamic_update_slice(out, w, (at, 0))


def kernel(x, mem, positions, rel_bias, final_norm, norm_mix, w_in, w_in_vres, w_out, tm_mu, tm_mu_vres, tm_w0, tm_w2, tm_a0, tm_a2, tm_v0, tm_v2, tm_g2, tm_k_k, tm_k_a, tm_r_k, tm_ln_w, tm_ln_b, da_lq1, da_lk1, da_lq2, da_lk2, da_subln, mla_q_norm, mla_wq_b, mla_kv_norm, mla_wkv_b, norm_cross, norm_mem, ca_wq, ca_wkv, ca_wo, norm_ffn, moe_w_group, moe_b_group, moe_w_expert, moe_b_expert, moe_w_gate, moe_w_up, moe_w_down):
    batch, seq, d = x.shape
    tokens = batch * seq
    depth = norm_mix.shape[0]
    xf = x.reshape(tokens, d)
    memf = mem.reshape(-1, d)
    positions = positions.astype(jnp.int32)

    head_of_lane = jnp.arange(RW_DIM) // RW_HEAD_DIM
    seg = (head_of_lane[:, None] == jnp.arange(LANES)[None, :]).astype(BF16)
    seg_t = seg.T
    row = lambda v: v.reshape(1, -1)

    v_first = None
    for l in range(depth):
        w_f, w_a = _proj_weights(w_in[l], None if l == 0 else w_in_vres[l - 1])
        pa, proj = norm_matmul(xf, norm_mix[l], w_a, w_f, out_dtype=BF16, tm=1024, tn=PROJ_A_COLS // 3)

        mu = tm_mu[l]
        prm = dict(mu_r=row(mu[:RW_DIM]), mu_k=row(mu[RW_DIM:2 * RW_DIM]), mu_v=row(mu[2 * RW_DIM:3 * RW_DIM]),
                   mu_l=row(mu[3 * RW_DIM:]), w0=row(tm_w0[l]), a0=row(tm_a0[l]),
                   w2=_pad_rows(tm_w2[l], LANES, 0), a2=_pad_rows(tm_a2[l], LANES, RW_W_RANK),
                   g2=tm_g2[l].astype(BF16), k_k=row(tm_k_k[l]), k_a=row(tm_k_a[l]), r_k=row(tm_r_k[l]),
                   seg=seg, seg_t=seg_t)
        if l > 0:
            prm.update(mu_vr=jnp.pad(row(tm_mu_vres[l - 1]), ((0, 0), (0, LANES - RW_V_RANK))),
                       v0=row(tm_v0[l - 1]), v2=_pad_rows(tm_v2[l - 1], LANES, 0))
        r, lw, k, v, kap, beta, gate, bonus = rwkv_prep(pa, proj, batch, v_first, prm)
        if l == 0:
            v_first = v
        o = rwkv_scan(r, lw, k, v, kap, beta, batch)
        y_a = rwkv_post(o, bonus, gate, tm_ln_w[l], tm_ln_b[l], seg, seg_t)

        lambda_init = 0.8 - 0.6 * math.exp(-0.3 * l)
        lam = (jnp.exp(jnp.sum(da_lq1[l] * da_lk1[l])) - jnp.exp(jnp.sum(da_lq2[l] * da_lk2[l])) + lambda_init)
        y_b = diff_attention(pa, positions, rel_bias, lam, lambda_init, da_subln[l])

        wq = mla_wq_b[l].reshape(ML_Q_RANK, ML_HEADS, ML_NOPE + ML_ROPE)
        wq_pe = jnp.pad(wq[:, :, ML_NOPE:], ((0, 0), (0, 0), (0, LANES - ML_ROPE)))
        wq_all = jnp.concatenate([wq[:, :, :ML_NOPE].reshape(ML_Q_RANK, -1),
                                  wq_pe.reshape(ML_Q_RANK, -1)], axis=1).astype(BF16)
        wkv = mla_wkv_b[l].reshape(ML_KV_RANK, ML_HEADS, ML_NOPE + ML_V)
        wkv = jnp.concatenate([wkv[:, :, :ML_NOPE].reshape(ML_KV_RANK, -1),
                               wkv[:, :, ML_NOPE:].reshape(ML_KV_RANK, -1)], axis=1).astype(BF16)
        qf, kf, v_mla = mla_prep(pa, proj, positions, mla_q_norm[l], mla_kv_norm[l], wq_all, wkv)
        y_c = mla_attention(qf, kf, v_mla, batch)

        wo = w_out[l].astype(BF16)
        kv_mem = norm_matmul(memf, norm_mem[l], ca_wkv[l].astype(BF16), out_dtype=BF16)
        xf = mix_cross_block([y_a, y_b, y_c], [wo[:RW_DIM], wo[RW_DIM:RW_DIM + DF_DIM], wo[RW_DIM + DF_DIM:]],
                             xf, batch, norm_cross[l], ca_wq[l].astype(BF16), kv_mem, ca_wo[l].astype(BF16))

        w_router = jnp.concatenate(
            [moe_w_expert[l], moe_w_group[l], jnp.zeros((d, LANES - MOE_EXPERTS - MOE_GROUPS), F32)], axis=1)
        b_router = jnp.concatenate(
            [moe_b_expert[l], moe_b_group[l], jnp.zeros((LANES - MOE_EXPERTS - MOE_GROUPS,), F32)]).reshape(1, LANES)
        xf = moe_block(xf, norm_ffn[l], w_router, b_router, moe_w_gate, moe_w_up, moe_w_down, l,
                       final_norm if l == depth - 1 else None)

    return xf.reshape(batch, seq, d)
```

```python
import functools
import math

import jax
import jax.numpy as jnp
from jax import lax
from jax.experimental import pallas as pl
from jax.experimental.pallas import tpu as pltpu

F32 = jnp.float32
BF16 = jnp.bfloat16

NORM_EPS = 1e-6
ROPE_THETA = 10000.0

RW_HEADS = 16
RW_HEAD_DIM = 64
RW_DIM = RW_HEADS * RW_HEAD_DIM
RW_W_RANK = 64
RW_A_RANK = 64
RW_G_RANK = 128
RW_V_RANK = 32
RW_LORA = RW_W_RANK + RW_A_RANK + RW_G_RANK
RW_LN_EPS = 64e-5
RW_COLS = 3 * RW_DIM + RW_LORA

DF_HEADS = 4
DF_HEAD_DIM = 64
DF_V_DIM = 2 * DF_HEAD_DIM
DF_QK = DF_HEADS * 2 * DF_HEAD_DIM
DF_DIM = DF_HEADS * DF_V_DIM
DF_COLS = 2 * DF_QK + DF_DIM
DF_SUBLN_EPS = 1e-5

ML_HEADS = 4
ML_Q_RANK = 384
ML_KV_RANK = 256
ML_NOPE = 128
ML_ROPE = 64
ML_V = 128
ML_DIM = ML_HEADS * ML_V
ML_COLS = ML_Q_RANK + ML_KV_RANK + ML_ROPE

REL_BUCKETS = 32
REL_MAX_DIST = 128

CA_HEADS = 4
CA_HEAD_DIM = 128
CA_DIM = CA_HEADS * CA_HEAD_DIM

MOE_GROUPS = 4
MOE_PER_GROUP = 8
MOE_EXPERTS = MOE_GROUPS * MOE_PER_GROUP

LANES = 128
SCAN_CHUNK = 64
SCAN_GROUP = 4
SCAN_BATCHES = 2
ATTN_TILE = 512
ONES_ROWS = 16
POST_ROW_BLOCKS = 4
DMA_ISSUE_UNROLL = 8
MOE_ROW_TILE = 256
VMEM_LIMIT = 56 * 1024 * 1024
NEG_BIG = -1e30

LOG2E = 1.4426950408889634

OFF_LORA = 0
OFF_VRES = OFF_LORA + RW_LORA
OFF_KPE = OFF_VRES + LANES
PROJ_F_COLS = OFF_KPE + LANES
OFF_R = 0
OFF_K = RW_DIM
OFF_V = 2 * RW_DIM
OFF_DQ = 3 * RW_DIM
OFF_DK = OFF_DQ + DF_QK
OFF_DV = OFF_DK + DF_QK
OFF_MKV = OFF_DV + DF_DIM
OFF_MQ = 13 * ML_Q_RANK
PROJ_A_COLS = OFF_MQ + ML_Q_RANK


def _cparams(sem, vmem=VMEM_LIMIT, flags=None):
    return pltpu.CompilerParams(dimension_semantics=sem, vmem_limit_bytes=vmem, flags=flags)


def _dot(a, b):
    return jnp.dot(a, b, preferred_element_type=F32)


def _dot_t(a, b):
    return lax.dot_general(a, b, (((1,), (1,)), ((), ())), preferred_element_type=F32)


def _split3(x):
    hi = x.astype(BF16)
    r1 = x - hi.astype(F32)
    mid = r1.astype(BF16)
    lo = (r1 - mid.astype(F32)).astype(BF16)
    return hi, mid, lo


def _dot_rhs01(x, ones_bf16):
    hi = x.astype(BF16)
    lo = (x - hi.astype(F32)).astype(BF16)
    return _dot(hi, ones_bf16) + _dot(lo, ones_bf16)


def _dot_x3(a, b):
    ah = a.astype(BF16)
    al = (a - ah.astype(F32)).astype(BF16)
    bh = b.astype(BF16)
    bl = (b - bh.astype(F32)).astype(BF16)
    return _dot(ah, bh) + _dot(ah, bl) + _dot(al, bh)


def _rms(x, w, eps):
    ms = jnp.mean(x * x, axis=-1, keepdims=True)
    return x * lax.rsqrt(ms + eps) * w


def _norm_matmul_kernel(*refs, eps, has_side):
    if has_side:
        x_ref, nw_ref, w_ref, ws_ref, o_ref, os_ref, xn_ref = refs
    else:
        x_ref, nw_ref, w_ref, o_ref, xn_ref = refs

    @pl.when(pl.program_id(1) == 0)
    def _():
        xn_ref[...] = _rms(x_ref[...], nw_ref[...], eps).astype(BF16)
        if has_side:
            os_ref[...] = _dot(xn_ref[...], ws_ref[...])

    o_ref[...] = _dot(xn_ref[...], w_ref[...]).astype(o_ref.dtype)


def norm_matmul(x, nw, w, w_side=None, *, out_dtype=F32, tm=512, tn=None, eps=NORM_EPS):
    m, d = x.shape
    n = w.shape[1]
    tm = min(tm, m)
    tn = n if tn is None else tn
    has_side = w_side is not None
    in_specs = [pl.BlockSpec((tm, d), lambda i, j: (i, 0)),
                pl.BlockSpec((1, d), lambda i, j: (0, 0)),
                pl.BlockSpec((d, tn), lambda i, j: (0, j))]
    out_specs = [pl.BlockSpec((tm, tn), lambda i, j: (i, j))]
    out_shape = [jax.ShapeDtypeStruct((m, n), out_dtype)]
    args = [x, nw.reshape(1, d), w]
    if has_side:
        ns = w_side.shape[1]
        in_specs.append(pl.BlockSpec((d, ns), lambda i, j: (0, 0)))
        out_specs.append(pl.BlockSpec((tm, ns), lambda i, j: (i, 0)))
        out_shape.append(jax.ShapeDtypeStruct((m, ns), F32))
        args.append(w_side)
    outs = pl.pallas_call(
        functools.partial(_norm_matmul_kernel, eps=eps, has_side=has_side),
        grid=(m // tm, n // tn),
        in_specs=in_specs,
        out_specs=out_specs,
        out_shape=out_shape,
        scratch_shapes=[pltpu.VMEM((tm, d), BF16)],
        compiler_params=_cparams(("parallel", "arbitrary")),
        name="norm_matmul",
    )(*args)
    return outs if has_side else outs[0]


def _softplus(z):
    return jnp.maximum(z, 0.0) + jnp.log(1.0 + jnp.exp(-jnp.abs(z)))


def _rwkv_prep_kernel(*refs, has_vres):
    if has_vres:
        (pr_ref, pk_ref, pv_ref, pl_ref, pvr_ref, vfirst_ref,
         mu_r, mu_k, mu_v, mu_l, mu_vr, w0, w2, a0, a2, g2, v0, v2,
         k_k, k_a, r_k, seg, seg_t,
         r_o, lw_o, k_o, v_o, kap_o, beta_o, g_o, bonus_o,
         last_r, last_k, last_v, last_l, last_vr) = refs
    else:
        (pr_ref, pk_ref, pv_ref, pl_ref,
         mu_r, mu_k, mu_v, mu_l, w0, w2, a0, a2, g2,
         k_k, k_a, r_k, seg, seg_t,
         r_o, lw_o, k_o, v_o, kap_o, beta_o, g_o, bonus_o,
         last_r, last_k, last_v, last_l) = refs
    t = pl.program_id(1)

    def shifted(p_ref, last_ref, mu_ref):
        p = p_ref[...].astype(F32)
        n = p.shape[0]
        carried = jnp.where(t == 0, 0.0, last_ref[0:1, :])
        row = lax.broadcasted_iota(jnp.int32, p.shape, 0)
        prev = jnp.where(row == 0, carried, pltpu.roll(p, 1, axis=0))
        last_ref[0:1, :] = p[n - 1:n, :]
        return p + mu_ref[...] * (prev - p)

    r = shifted(pr_ref, last_r, mu_r)
    k = shifted(pk_ref, last_k, mu_k)
    v = shifted(pv_ref, last_v, mu_v)
    lora = shifted(pl_ref, last_l, mu_l)
    wl = lora[:, :LANES]
    gl = lora[:, LANES:]

    lane = lax.broadcasted_iota(jnp.int32, wl.shape, 1)
    wl_t = jnp.where(lane < RW_W_RANK, jnp.tanh(wl), 0.0)
    al = jnp.where(lane >= RW_W_RANK, wl, 0.0)
    w_log = -_softplus(-(w0[...] + _dot_x3(wl_t, w2[...]))) - 0.5
    lw_o[...] = -jnp.exp(w_log)
    a = jax.nn.sigmoid(a0[...] + _dot_x3(al, a2[...]))
    g_o[...] = _dot(jax.nn.sigmoid(gl).astype(BF16), g2[...]).astype(g_o.dtype)

    segm, segm_t = seg[...], seg_t[...]

    def head_sum(x):
        return _dot_rhs01(_dot_rhs01(x, segm), segm_t)

    kk = k * k_k[...]
    kk = kk * lax.rsqrt(jnp.maximum(head_sum(kk * kk), 1e-24))
    k = k * (1.0 + (a - 1.0) * k_a[...])
    if has_vres:
        vr = shifted(pvr_ref, last_vr, mu_vr)
        mix = jax.nn.sigmoid(v0[...] + _dot_x3(vr, v2[...]))
        v = v + (vfirst_ref[...] - v) * mix
    r_o[...] = r.astype(r_o.dtype)
    k_o[...] = k.astype(k_o.dtype)
    v_o[...] = v.astype(v_o.dtype)
    kap_o[...] = kk.astype(kap_o.dtype)
    beta_o[...] = (kk * a).astype(beta_o.dtype)
    bonus_o[...] = (head_sum(r * k * r_k[...]) * v).astype(bonus_o.dtype)


def rwkv_prep(pa, proj, batch, vfirst, prm, *, tt=512):
    tokens = proj.shape[0]
    seq = tokens // batch
    tt = min(tt, seq)
    nt = seq // tt
    has_vres = vfirst is not None
    d = RW_DIM

    def rows(width, col):
        return pl.BlockSpec((tt, width), lambda b, t, col=col: (b * nt + t, col))

    def full(shape):
        return pl.BlockSpec(shape, lambda b, t: (0, 0))

    in_specs = [rows(d, OFF_R // d), rows(d, OFF_K // d), rows(d, OFF_V // d),
                rows(RW_LORA, OFF_LORA // RW_LORA)]
    args = [pa, pa, pa, proj]
    if has_vres:
        in_specs += [rows(LANES, OFF_VRES // LANES), rows(d, 0)]
        args += [proj, vfirst]
    names = ["mu_r", "mu_k", "mu_v", "mu_l"] + (["mu_vr"] if has_vres else []) + ["w0", "w2", "a0", "a2", "g2"]
    names += (["v0", "v2"] if has_vres else []) + ["k_k", "k_a", "r_k", "seg", "seg_t"]
    for nm in names:
        in_specs.append(full(prm[nm].shape))
        args.append(prm[nm])
    out_spec = pl.BlockSpec((tt, d), lambda b, t: (b * nt + t, 0))
    scratch = [pltpu.VMEM((8, d), F32)] * 3 + [pltpu.VMEM((8, RW_LORA), F32)]
    if has_vres:
        scratch.append(pltpu.VMEM((8, LANES), F32))
    return pl.pallas_call(
        functools.partial(_rwkv_prep_kernel, has_vres=has_vres),
        grid=(batch, nt),
        in_specs=in_specs,
        out_specs=[out_spec] * 8,
        out_shape=[jax.ShapeDtypeStruct((tokens, d), F32 if i == 1 else BF16) for i in range(8)],
        scratch_shapes=scratch,
        compiler_params=_cparams(("arbitrary", "arbitrary")),
        name="rwkv_prep",
    )(*args)


def _rwkv_scan_kernel(r_ref, lw_ref, k_ref, v_ref, kap_ref, beta_ref, tril_ref, bmask_ref,
                      o_ref, ht_ref):
    @pl.when(pl.program_id(1) == 0)
    def _():
        ht_ref[...] = jnp.zeros_like(ht_ref)

    n_batch, c, d = lw_ref.shape
    w = ht_ref.shape[1]
    g = w // RW_HEAD_DIM
    bmask = bmask_ref[...]
    bmask_b = bmask.astype(BF16)
    tril3 = tril_ref[...]
    t_idx = lax.broadcasted_iota(jnp.int32, (c, w), 0)
    s_idx = lax.broadcasted_iota(jnp.int32, (c, w), 1) % c
    strict = t_idx > s_idx
    incl = t_idx >= s_idx
    n_sq = int(math.log2(c))

    def stack(x):
        return jnp.concatenate([x.astype(BF16)] * g, axis=0) * bmask_b

    sls = [(bi, slice(None), slice(lo, lo + w)) for bi in range(n_batch) for lo in range(0, d, w)]
    groups = range(len(sls))
    lw = [lw_ref[sl] for sl in sls]
    cum = [_dot(tril3, jnp.concatenate(_split3(x), axis=0)) for x in lw]
    total = [x[c - 1:c, :] for x in cum]
    ar = [jnp.concatenate([-kap_ref[sls[gi]] * jnp.exp(cum[gi] - lw[gi]), r_ref[sls[gi]] * jnp.exp(cum[gi])],
                          axis=0).astype(BF16) for gi in groups]
    p_inv = [jnp.exp(-x) for x in cum]
    b_s = [stack(beta_ref[sls[gi]] * p_inv[gi]) for gi in groups]
    k_s = [stack(k_ref[sls[gi]] * p_inv[gi]) for gi in groups]
    v_n = [v_ref[sl] for sl in sls]
    v_s = [stack(x) for x in v_n]

    arb = [_dot_t(ar[gi], b_s[gi]) for gi in groups]
    ark = [_dot_t(ar[gi], k_s[gi]) for gi in groups]
    ab = [jnp.where(strict, m[:c], 0.0) for m in arb]
    rb = [jnp.where(incl, m[c:], 0.0).astype(BF16) for m in arb]
    akrk = [jnp.concatenate([jnp.where(strict, m[:c], 0.0), jnp.where(incl, m[c:], 0.0)], axis=0).astype(BF16)
            for m in ark]

    ht = [ht_ref[gi] for gi in groups]
    base = [_dot_t(ar[gi], ht[gi].astype(BF16)) + _dot(akrk[gi], v_s[gi]) for gi in groups]
    x = [m[:c] for m in base]
    lp = ab
    for i in range(n_sq):
        lpb = [m.astype(BF16) for m in lp]
        x = [x[gi] + _dot(lpb[gi], stack(x[gi])) for gi in groups]
        if i < n_sq - 1:
            lp = [_dot(lpb[gi], stack(lp[gi])) for gi in groups]
    for gi in groups:
        o_ref[sls[gi]] = base[gi][c:] + _dot(rb[gi], stack(x[gi]))

    for gi in groups:
        p_rem = jnp.exp(total[gi] - cum[gi])
        z = jnp.concatenate([beta_ref[sls[gi]] * p_rem, k_ref[sls[gi]] * p_rem], axis=0).astype(BF16)
        uv_t = jnp.concatenate([x[gi], v_n[gi].astype(F32)], axis=0).T.astype(BF16)
        ht_ref[gi] = ht[gi] * jnp.exp(total[gi]) + bmask * _dot(uv_t, z)


def rwkv_scan(r, lw, k, v, kap, beta, batch):
    tokens, d = r.shape
    seq = tokens // batch
    c = min(SCAN_CHUNK, seq)
    nc = seq // c
    gw = SCAN_GROUP * RW_HEAD_DIM
    rr = SCAN_GROUP * c
    assert c == RW_HEAD_DIM, "the stacking mask doubles as the head-block mask of the state"
    bb = math.gcd(batch, SCAN_BATCHES)
    tril = jnp.tile((jnp.arange(c)[:, None] >= jnp.arange(c)[None, :]).astype(BF16), (1, 3))
    bmask = (jnp.arange(rr)[:, None] // c == jnp.arange(gw)[None, :] // RW_HEAD_DIM).astype(F32)
    blk = pl.BlockSpec((bb, c, d), lambda b, i: (b, i, 0))
    as3d = lambda a: a.reshape(batch, seq, d)
    out = pl.pallas_call(
        _rwkv_scan_kernel,
        grid=(batch // bb, nc),
        in_specs=[blk] * 6 + [pl.BlockSpec((c, 3 * c), lambda b, i: (0, 0)),
                              pl.BlockSpec((rr, gw), lambda b, i: (0, 0))],
        out_specs=blk,
        out_shape=jax.ShapeDtypeStruct((batch, seq, d), F32),
        scratch_shapes=[pltpu.VMEM((bb * (d // gw), gw, gw), F32)],
        compiler_params=_cparams(("arbitrary", "arbitrary")),
        name="rwkv_scan",
    )(as3d(r), as3d(lw), as3d(k), as3d(v), as3d(kap), as3d(beta), tril, bmask)
    return out.reshape(tokens, d)


def _rwkv_post_kernel(o_ref, bonus_ref, g_ref, lnw_ref, lnb_ref, seg, seg_t, y_ref):
    segm, segm_t = seg[...], seg_t[...]

    def head_means(xs):
        sums = [_dot_rhs01(x, segm) for x in xs]
        return [_dot_rhs01(s, segm_t) * (1.0 / RW_HEAD_DIM) for s in sums]

    rows = o_ref.shape[0] // POST_ROW_BLOCKS
    blocks = [slice(i * rows, (i + 1) * rows) for i in range(POST_ROW_BLOCKS)]
    o = [o_ref[b, :] for b in blocks]
    dlt = [x - m for x, m in zip(o, head_means(o))]
    var = head_means([x * x for x in dlt])
    for b, x, v in zip(blocks, dlt, var):
        y = x * lax.rsqrt(v + RW_LN_EPS) * lnw_ref[...] + lnb_ref[...]
        y_ref[b, :] = ((y + bonus_ref[b, :]) * g_ref[b, :]).astype(y_ref.dtype)


def _t5_thresholds():
    max_exact = REL_BUCKETS // 2
    thr = list(range(1, max_exact))
    n = max_exact
    for bucket in range(max_exact, REL_BUCKETS):
        while True:
            large = max_exact + int(math.log(max(n, max_exact) / max_exact)
                                    / math.log(REL_MAX_DIST / max_exact) * (REL_BUCKETS - max_exact))
            if min(large, REL_BUCKETS - 1) >= bucket:
                break
            n += 1
        thr.append(n)
    return thr


T5_THRESHOLDS = _t5_thresholds()
T5_FAR = T5_THRESHOLDS[-1]


def _softmax_tiles(s_list, c_list, states, vt_list):
    stats = []
    for s_t, c, (m_old, _) in zip(s_list, c_list, states):
        m_new = jnp.maximum(m_old, jnp.max(s_t, axis=0, keepdims=True) + c)
        stats.append((m_new, jnp.exp2(m_old - m_new), jnp.exp2(s_t - (m_new - c)).astype(BF16)))
    return tuple((m_new, alpha * acc + _dot(vt, p_t))
                 for (m_new, alpha, p_t), (_, acc), vt in zip(stats, states, vt_list))


def _transpose_into(vt_ref, v_ref, chunk, heads, width):
    seq = v_ref.shape[0]
    blk = width + ONES_ROWS
    for c in range(seq // chunk):
        cols = slice(c * chunk, (c + 1) * chunk)
        vt = v_ref[cols, :].astype(F32).T.astype(BF16)
        for h in range(heads):
            vt_ref[h * blk:h * blk + width, cols] = vt[h * width:(h + 1) * width]
            vt_ref[h * blk + width:(h + 1) * blk, cols] = jnp.ones((ONES_ROWS, chunk), BF16)


def _diff_attn_kernel(qfirst_ref, klast_ref, q_ref, k_ref, v_ref, qpos_ref, kpos_ref, subln_ref, table_ref, lam_ref,
                      o_ref, vt_ref, *, tq, tk, scale2, out_scale):
    b, i = pl.program_id(0), pl.program_id(1)
    nq = pl.num_programs(1)
    seq = k_ref.shape[0]
    nk = seq // tk
    w = DF_V_DIM
    vblk = w + ONES_ROWS

    @pl.when(i == 0)
    def _():
        _transpose_into(vt_ref, v_ref, tk, DF_HEADS, w)

    n_tiles = (i * tq + tq - 1) // tk + 1
    qf = qfirst_ref[b * nq + i]
    n_far = lax.while_loop(
        lambda j: (j * tk + tk - 1 <= i * tq) & (qf - klast_ref[b * nk + jnp.minimum(j, nk - 1)] >= T5_FAR),
        lambda j: j + 1, jnp.int32(0))

    dist = lax.broadcasted_iota(jnp.int32, (1, LANES), 1)
    qpos = qpos_ref[...]
    q_idx = i * tq + lax.broadcasted_iota(jnp.int32, (tk, tq), 1)
    k_off = lax.broadcasted_iota(jnp.int32, (tk, tq), 0)
    lane = lax.broadcasted_iota(jnp.int32, (tq, w), 1)

    bias_rows, c_far, qm = [], [], []
    for h in range(DF_HEADS):
        bias_vec = jnp.full((1, LANES), table_ref[h], F32)
        for bucket, thr in enumerate(T5_THRESHOLDS, start=1):
            bias_vec = jnp.where(dist >= thr, table_ref[bucket * DF_HEADS + h], bias_vec)
        bias_rows.append(jnp.broadcast_to(bias_vec * LOG2E, (tk, LANES)))
        c_far.append(table_ref[(REL_BUCKETS - 1) * DF_HEADS + h] * LOG2E)
        qh = q_ref[:, h * w:(h + 1) * w].astype(F32) * scale2
        qm.append([jnp.where((lane >= mi * DF_HEAD_DIM) & (lane < (mi + 1) * DF_HEAD_DIM), qh, 0.0).astype(BF16)
                   for mi in range(2)])

    def tiles(j, h):
        off = pl.multiple_of(j * tk, tk)
        return k_ref[pl.ds(off, tk), h * w:(h + 1) * w], vt_ref[h * vblk:(h + 1) * vblk, pl.ds(off, tk)], off

    chains = [(h, mi) for h in range(DF_HEADS) for mi in range(2)]

    def far_body(j, st):
        kv = [tiles(j, h) for h in range(DF_HEADS)]
        s = [_dot_t(kv[h][0], qm[h][mi]) for h, mi in chains]
        return _softmax_tiles(s, [c_far[h] for h, _ in chains], st, [kv[h][1] for h, _ in chains])

    def near_body(j, st):
        off = pl.multiple_of(j * tk, tk)
        n = jnp.clip(qpos - kpos_ref[pl.ds(off, tk), :], 0, LANES - 1)
        keep = q_idx >= off + k_off
        kv = [tiles(j, h) for h in range(DF_HEADS)]
        bias = [jnp.concatenate(
            [jnp.take_along_axis(bias_rows[h], n[:, cb * LANES:(cb + 1) * LANES], axis=1)
             for cb in range(tq // LANES)], axis=1) for h in range(DF_HEADS)]
        s = [jnp.where(keep, _dot_t(kv[h][0], qm[h][mi]) + bias[h], NEG_BIG) for h, mi in chains]
        return _softmax_tiles(s, [0.0] * len(chains), st, [kv[h][1] for h, _ in chains])

    init = tuple((jnp.full((1, tq), NEG_BIG, F32), jnp.zeros((w + ONES_ROWS, tq), F32))
                 for _ in range(2 * DF_HEADS))
    st = lax.fori_loop(0, n_far, far_body, init)
    st = lax.fori_loop(n_far, n_tiles, near_body, st)
    for h in range(DF_HEADS):
        a0, a1 = st[2 * h][1], st[2 * h + 1][1]
        d_t = a0[:w] / a0[w:w + 1] - lam_ref[0] * (a1[:w] / a1[w:w + 1])
        ms = jnp.mean(d_t * d_t, axis=0, keepdims=True)
        y_t = d_t * lax.rsqrt(ms + DF_SUBLN_EPS) * (subln_ref[...] * out_scale)
        o_ref[:, h * w:(h + 1) * w] = y_t.T.astype(o_ref.dtype)


def diff_attention(pa, positions, rel_bias, lam, lambda_init, subln_w, *, tq=ATTN_TILE):
    batch, seq = positions.shape
    tokens = batch * seq
    tq = min(tq, seq)
    tk = tq
    nq, nk = seq // tq, seq // tk
    qfirst = positions[:, ::tq].reshape(-1)
    klast = positions[:, tk - 1::tk].reshape(-1)
    qpos = positions.reshape(batch, 1, seq)
    kpos = positions.reshape(batch, seq, 1)
    wd = DF_DIM
    grid_spec = pltpu.PrefetchScalarGridSpec(
        num_scalar_prefetch=2,
        grid=(batch, nq),
        in_specs=[pl.BlockSpec((tq, wd), lambda b, i, *_: (b * nq + i, OFF_DQ // wd)),
                  pl.BlockSpec((seq, wd), lambda b, i, *_: (b, OFF_DK // wd)),
                  pl.BlockSpec((seq, wd), lambda b, i, *_: (b, OFF_DV // wd)),
                  pl.BlockSpec((None, 1, tq), lambda b, i, *_: (b, 0, i)),
                  pl.BlockSpec((None, seq, 1), lambda b, i, *_: (b, 0, 0)),
                  pl.BlockSpec((DF_V_DIM, 1), lambda b, i, *_: (0, 0)),
                  pl.BlockSpec(memory_space=pltpu.SMEM),
                  pl.BlockSpec(memory_space=pltpu.SMEM)],
        out_specs=pl.BlockSpec((tq, wd), lambda b, i, *_: (b * nq + i, 0)),
        scratch_shapes=[pltpu.VMEM((DF_HEADS * (DF_V_DIM + ONES_ROWS), seq), BF16)],
    )
    return pl.pallas_call(
        functools.partial(_diff_attn_kernel, tq=tq, tk=tk, scale2=DF_HEAD_DIM ** -0.5 * LOG2E,
                          out_scale=1.0 - lambda_init),
        grid_spec=grid_spec,
        out_shape=jax.ShapeDtypeStruct((tokens, DF_DIM), BF16),
        compiler_params=_cparams(("arbitrary", "arbitrary")),
        name="diff_attention",
    )(qfirst, klast, pa, pa, pa, qpos, kpos, subln_w.reshape(DF_V_DIM, 1), rel_bias.reshape(-1), lam.reshape(1))


ML_QK_PAD = 2 * LANES


def _rope_block(x, cos, sin):
    half = ML_ROPE // 2
    lane = lax.broadcasted_iota(jnp.int32, x.shape, 1)
    rot = jnp.where(lane < half, -pltpu.roll(x, LANES - half, axis=1),
                    jnp.where(lane < ML_ROPE, pltpu.roll(x, half, axis=1), 0.0))
    return x * cos + rot * sin


def _mla_prep_kernel(mq_ref, mkv_ref, kpe_ref, pos_ref, qn_w, kvn_w, wq_ref, wkv_ref, freq_ref,
                     qf_o, kf_o, v_o, *, qscale):
    ang = pos_ref[...].astype(F32) * freq_ref[...]
    cos, sin = jnp.cos(ang), jnp.sin(ang)
    qc = _rms(mq_ref[...].astype(F32), qn_w[...], NORM_EPS).astype(BF16)
    q_all = _dot(qc, wq_ref[...]) * qscale
    kvc = _rms(mkv_ref[...].astype(F32), kvn_w[...], NORM_EPS).astype(BF16)
    kvb = _dot(kvc, wkv_ref[...])
    kpe = _rope_block(kpe_ref[...], cos, sin).astype(BF16)
    nope_w = ML_HEADS * ML_NOPE
    for h in range(ML_HEADS):
        lo = h * ML_QK_PAD
        qf_o[:, lo:lo + LANES] = q_all[:, h * LANES:(h + 1) * LANES].astype(BF16)
        qf_o[:, lo + LANES:lo + 2 * LANES] = _rope_block(
            q_all[:, nope_w + h * LANES:nope_w + (h + 1) * LANES], cos, sin).astype(BF16)
        kf_o[:, lo:lo + LANES] = kvb[:, h * LANES:(h + 1) * LANES].astype(BF16)
        kf_o[:, lo + LANES:lo + 2 * LANES] = kpe
    v_o[...] = kvb[:, nope_w:].astype(BF16)


def mla_prep(pa, pf, positions, q_norm, kv_norm, wq_all, wkv, *, tm=512):
    tokens = pa.shape[0]
    tm = min(tm, tokens)
    half = ML_ROPE // 2
    inv_freq = ROPE_THETA ** (-jnp.arange(half, dtype=F32) / half)
    freq = jnp.concatenate([inv_freq, inv_freq, jnp.zeros((LANES - ML_ROPE,), F32)]).reshape(1, LANES)

    def full(a):
        return pl.BlockSpec(a.shape, lambda i: (0, 0))

    qn_w = q_norm.reshape(1, -1)
    kvn_w = kv_norm.reshape(1, -1)
    wide = ML_HEADS * ML_QK_PAD
    return pl.pallas_call(
        functools.partial(_mla_prep_kernel, qscale=(ML_NOPE + ML_ROPE) ** -0.5 * LOG2E),
        grid=(tokens // tm,),
        in_specs=[pl.BlockSpec((tm, ML_Q_RANK), lambda i: (i, OFF_MQ // ML_Q_RANK)),
                  pl.BlockSpec((tm, ML_KV_RANK), lambda i: (i, OFF_MKV // ML_KV_RANK)),
                  pl.BlockSpec((tm, LANES), lambda i: (i, OFF_KPE // LANES)),
                  pl.BlockSpec((tm, 1), lambda i: (i, 0)),
                  full(qn_w), full(kvn_w), full(wq_all), full(wkv), full(freq)],
        out_specs=[pl.BlockSpec((tm, wide), lambda i: (i, 0)),
                   pl.BlockSpec((tm, wide), lambda i: (i, 0)),
                   pl.BlockSpec((tm, ML_DIM), lambda i: (i, 0))],
        out_shape=[jax.ShapeDtypeStruct((tokens, wide), BF16),
                   jax.ShapeDtypeStruct((tokens, wide), BF16),
                   jax.ShapeDtypeStruct((tokens, ML_DIM), BF16)],
        compiler_params=_cparams(("parallel",)),
        name="mla_prep",
    )(pa, pa, pf, positions.reshape(tokens, 1), qn_w, kvn_w, wq_all, wkv, freq)


def _mla_attn_kernel(q_ref, k_ref, v_ref, o_ref, vt_ref, *, tq, tk):
    i = pl.program_id(1)
    wq = ML_QK_PAD
    vblk = ML_V + ONES_ROWS

    @pl.when(i == 0)
    def _():
        _transpose_into(vt_ref, v_ref, tk, ML_HEADS, ML_V)

    n_tiles = (i * tq + tq - 1) // tk + 1
    n_full = (i * tq + 1) // tk
    q_idx = i * tq + lax.broadcasted_iota(jnp.int32, (tk, tq), 1)
    k_off = lax.broadcasted_iota(jnp.int32, (tk, tq), 0)
    qh = [q_ref[:, h * wq:(h + 1) * wq] for h in range(ML_HEADS)]

    def tiles(j, h):
        off = pl.multiple_of(j * tk, tk)
        return (k_ref[pl.ds(off, tk), h * wq:(h + 1) * wq],
                vt_ref[h * vblk:(h + 1) * vblk, pl.ds(off, tk)], off)

    heads = range(ML_HEADS)

    def full_body(j, st):
        kv = [tiles(j, h) for h in heads]
        s = [_dot_t(kv[h][0], qh[h]) for h in heads]
        return _softmax_tiles(s, [0.0] * ML_HEADS, st, [kv[h][1] for h in heads])

    def diag_body(j, st):
        kv = [tiles(j, h) for h in heads]
        keep = q_idx >= kv[0][2] + k_off
        s = [jnp.where(keep, _dot_t(kv[h][0], qh[h]), NEG_BIG) for h in heads]
        return _softmax_tiles(s, [0.0] * ML_HEADS, st, [kv[h][1] for h in heads])

    st = tuple((jnp.full((1, tq), NEG_BIG, F32), jnp.zeros((vblk, tq), F32)) for _ in range(ML_HEADS))
    st = lax.fori_loop(0, n_full, full_body, st)
    st = lax.fori_loop(n_full, n_tiles, diag_body, st)
    for h in range(ML_HEADS):
        acc = st[h][1]
        o_ref[:, h * ML_V:(h + 1) * ML_V] = (acc[:ML_V] / acc[ML_V:ML_V + 1]).T.astype(o_ref.dtype)


def mla_attention(qf, kf, v, batch, *, tq=ATTN_TILE):
    tokens = qf.shape[0]
    seq = tokens // batch
    tq = min(tq, seq)
    tk = tq
    nq = seq // tq
    wide = qf.shape[1]
    return pl.pallas_call(
        functools.partial(_mla_attn_kernel, tq=tq, tk=tk),
        grid=(batch, nq),
        in_specs=[pl.BlockSpec((tq, wide), lambda b, i: (b * nq + i, 0)),
                  pl.BlockSpec((seq, wide), lambda b, i: (b, 0)),
                  pl.BlockSpec((seq, ML_DIM), lambda b, i: (b, 0))],
        out_specs=pl.BlockSpec((tq, ML_DIM), lambda b, i: (b * nq + i, 0)),
        out_shape=jax.ShapeDtypeStruct((tokens, ML_DIM), BF16),
        scratch_shapes=[pltpu.VMEM((ML_HEADS * (ML_V + ONES_ROWS), seq), BF16)],
        compiler_params=_cparams(("arbitrary", "arbitrary")),
        name="mla_attention",
    )(qf, kf, v)


def _cross_kernel(*refs, n_mix):
    post_refs, refs = refs[:7], refs[7:]
    y_refs, w_refs = refs[:n_mix - 1], refs[n_mix - 1:2 * n_mix - 1]
    x_ref, nw_ref, wq_ref, kv_ref, wo_ref, o_ref, ya_buf = refs[2 * n_mix - 1:]
    _rwkv_post_kernel(*post_refs, ya_buf)
    x = x_ref[...]
    for y_ref, w_ref in zip((ya_buf,) + tuple(y_refs), w_refs):
        x = x + _dot(y_ref[...], w_ref[...])
    q = _dot(_rms(x, nw_ref[...], NORM_EPS).astype(BF16), wq_ref[...])
    kv = kv_ref[...]
    scale = CA_HEAD_DIM ** -0.5
    heads = [slice(hh * CA_HEAD_DIM, (hh + 1) * CA_HEAD_DIM) for hh in range(CA_HEADS)]
    s = [_dot_t(q[:, sl].astype(BF16), kv[:, sl]) * scale for sl in heads]
    p = [jnp.exp(x - jnp.max(x, axis=-1, keepdims=True)) for x in s]
    p = [(x / jnp.sum(x, axis=-1, keepdims=True)).astype(BF16) for x in p]
    outs = [_dot(x, kv[:, CA_DIM + sl.start:CA_DIM + sl.stop]) for x, sl in zip(p, heads)]
    o = jnp.concatenate(outs, axis=1).astype(BF16)
    o_ref[...] = x + _dot(o, wo_ref[...])


def mix_cross_block(rwkv_raw, y_list, w_list, x, batch, norm_w, wq, kv, wo, *, tq=512):
    tokens, d = x.shape
    seq = tokens // batch
    tq = min(tq, seq)
    nq = seq // tq
    mem_len = kv.shape[0] // batch
    o, bonus, gate, ln_w, ln_b, seg, seg_t = rwkv_raw
    dr = o.shape[1]
    rows = lambda width: pl.BlockSpec((tq, width), lambda b, i: (b * nq + i, 0))
    whole = lambda a: pl.BlockSpec(a.shape, lambda b, i: (0, 0))
    vec = lambda width: pl.BlockSpec((1, width), lambda b, i: (0, 0))
    return pl.pallas_call(
        functools.partial(_cross_kernel, n_mix=len(w_list)),
        grid=(batch, nq),
        in_specs=[rows(dr), rows(dr), rows(dr), vec(dr), vec(dr), whole(seg), whole(seg_t)]
                 + [rows(y.shape[1]) for y in y_list] + [whole(w) for w in w_list]
                 + [rows(d), vec(d), whole(wq),
                    pl.BlockSpec((mem_len, 2 * CA_DIM), lambda b, i: (b, 0)), whole(wo)],
        out_specs=rows(d),
        out_shape=jax.ShapeDtypeStruct((tokens, d), F32),
        scratch_shapes=[pltpu.VMEM((tq, dr), BF16)],
        compiler_params=_cparams(("parallel", "parallel")),
        name="mix_cross_block",
    )(o, bonus, gate, ln_w.reshape(1, dr), ln_b.reshape(1, dr), seg, seg_t,
      *y_list, *w_list, x, norm_w.reshape(1, d), wq, kv, wo)


SEL_E1, SEL_E2, SEL_G1, SEL_G2 = 0, 1, 2, 3


def _route(logits, b_router):
    biased = logits + b_router
    lane = lax.broadcasted_iota(jnp.int32, logits.shape, 1)
    big = jnp.int32(LANES)

    def first_argmax(vals):
        mx = jnp.max(vals, axis=-1, keepdims=True)
        return jnp.min(jnp.where(vals == mx, lane, big), axis=-1, keepdims=True)

    def pick(vals, idx):
        return jnp.sum(jnp.where(lane == idx, vals, 0.0), axis=-1, keepdims=True)

    is_group = (lane >= MOE_EXPERTS) & (lane < MOE_EXPERTS + MOE_GROUPS)
    gl = jnp.where(is_group, logits, NEG_BIG)
    ge = jnp.exp(gl - jnp.max(gl, axis=-1, keepdims=True))
    gp = ge / jnp.sum(ge, axis=-1, keepdims=True)
    g_lane = first_argmax(jnp.where(is_group, biased, NEG_BIG))
    p_group = pick(gp, g_lane)
    lo = (g_lane - MOE_EXPERTS) * MOE_PER_GROUP
    in_group = (lane >= lo) & (lane < lo + MOE_PER_GROUP)
    eb = jnp.where(in_group, biased, NEG_BIG)
    i1 = first_argmax(eb)
    i2 = first_argmax(jnp.where(lane == i1, NEG_BIG, eb))
    l1, l2 = pick(logits, i1), pick(logits, i2)
    mx = jnp.maximum(l1, l2)
    e1, e2 = jnp.exp(l1 - mx), jnp.exp(l2 - mx)
    w1, w2 = e1 / (e1 + e2), e2 / (e1 + e2)
    return jnp.where(lane == SEL_E1, i1.astype(F32),
                     jnp.where(lane == SEL_E2, i2.astype(F32),
                               jnp.where(lane == SEL_G1, w1 * p_group,
                                         jnp.where(lane == SEL_G2, w2 * p_group, 0.0))))


def _router_kernel(x_ref, nw_ref, wr_ref, br_ref, h_ref, sel_ref):
    h = _rms(x_ref[...], nw_ref[...], NORM_EPS)
    h_ref[...] = h
    sel_ref[...] = _route(_dot_x3(h, wr_ref[...]), br_ref[...])


def moe_router(x, norm_w, w_router, b_router, *, tm=512):
    tokens, d = x.shape
    tm = min(tm, tokens)
    return pl.pallas_call(
        _router_kernel,
        grid=(tokens // tm,),
        in_specs=[pl.BlockSpec((tm, d), lambda i: (i, 0)),
                  pl.BlockSpec((1, d), lambda i: (0, 0)),
                  pl.BlockSpec((d, LANES), lambda i: (0, 0)),
                  pl.BlockSpec((1, LANES), lambda i: (0, 0))],
        out_specs=[pl.BlockSpec((tm, d), lambda i: (i, 0)),
                   pl.BlockSpec((tm, LANES), lambda i: (i, 0))],
        out_shape=[jax.ShapeDtypeStruct((tokens, d), F32),
                   jax.ShapeDtypeStruct((tokens, LANES), F32)],
        compiler_params=_cparams(("parallel",)),
        name="moe_router",
    )(x, norm_w.reshape(1, d), w_router, b_router)


def _moe_rank_kernel(sel_ref, ltri_ref, rank_ref, counts_ref, carry_ref):
    @pl.when(pl.program_id(0) == 0)
    def _():
        carry_ref[...] = jnp.zeros_like(carry_ref)

    sel = sel_ref[...]
    lane = lax.broadcasted_iota(jnp.int32, sel.shape, 1)
    lane_f = lane.astype(F32)
    oh1 = lane_f == sel[:, SEL_E1:SEL_E1 + 1]
    oh2 = lane_f == sel[:, SEL_E2:SEL_E2 + 1]
    f1, f2 = oh1.astype(F32), oh2.astype(F32)
    ltri = ltri_ref[...]
    before1 = _dot(ltri, f1.astype(BF16))
    before2 = _dot(ltri, f2.astype(BF16))
    c1 = jnp.sum(f1, axis=0, keepdims=True)
    c2 = jnp.sum(f2, axis=0, keepdims=True)
    carry = carry_ref[...]
    r1 = jnp.sum(jnp.where(oh1, before1 + carry, 0.0), axis=1, keepdims=True)
    r2 = jnp.sum(jnp.where(oh2, before2 + carry + c1, 0.0), axis=1, keepdims=True)
    rank_ref[...] = jnp.where(lane == SEL_E1, r1, jnp.where(lane == SEL_E2, r2, 0.0)).astype(jnp.int32)
    total = carry + c1 + c2
    carry_ref[...] = total
    counts_ref[...] = total.astype(jnp.int32)


def moe_rank(sel, *, tm=512):
    tokens = sel.shape[0]
    tm = min(tm, tokens)
    ltri = (jnp.arange(tm)[:, None] > jnp.arange(tm)[None, :]).astype(BF16)
    return pl.pallas_call(
        _moe_rank_kernel,
        grid=(tokens // tm,),
        in_specs=[pl.BlockSpec((tm, LANES), lambda i: (i, 0)),
                  pl.BlockSpec((tm, tm), lambda i: (0, 0))],
        out_specs=[pl.BlockSpec((tm, LANES), lambda i: (i, 0)),
                   pl.BlockSpec((1, LANES), lambda i: (0, 0))],
        out_shape=[jax.ShapeDtypeStruct((tokens, LANES), jnp.int32),
                   jax.ShapeDtypeStruct((1, LANES), jnp.int32)],
        scratch_shapes=[pltpu.VMEM((1, LANES), F32)],
        compiler_params=_cparams(("arbitrary",)),
        name="moe_rank",
    )(sel, ltri)


def _row_copy(src_ref, src_row, dst_ref, dst_row, sem):
    return pltpu.make_async_copy(src_ref.at[pl.ds(src_row, 1)], dst_ref.at[pl.ds(dst_row, 1)], sem)


def _moe_dispatch_kernel(dest_ref, tail_ref, h_ref, xs_ref, hbuf, zero_ref, lsem, ssem, zsem, *, tm):
    step = pl.program_id(0)
    n_steps = pl.num_programs(0)
    base = step * (2 * tm)
    slot = step % 3

    def load(tile, buf_slot):
        return pltpu.make_async_copy(h_ref.at[pl.ds(pl.multiple_of(tile * tm, tm), tm)], hbuf.at[buf_slot],
                                     lsem.at[buf_slot])

    def wait_scatter(buf_slot):
        for _ in range(2):
            pltpu.make_async_copy(hbuf.at[buf_slot], xs_ref.at[pl.ds(0, tm)], ssem.at[buf_slot]).wait()

    @pl.when(step == 0)
    def _():
        load(0, 0).start()

        @pl.when(n_steps > 1)
        def _():
            load(1, 1).start()

    @pl.when(step == 0)
    def _():
        zero_ref[...] = jnp.zeros_like(zero_ref)

        def fill(tail):
            return pltpu.make_async_copy(zero_ref, xs_ref.at[pl.ds(pl.multiple_of(tail, 8), zero_ref.shape[0])], zsem)

        def start(e, carry):
            @pl.when(tail_ref[e] >= 0)
            def _():
                fill(tail_ref[e]).start()
            return carry

        def wait(e, carry):
            @pl.when(tail_ref[e] >= 0)
            def _():
                fill(tail_ref[e]).wait()
            return carry

        lax.fori_loop(0, tail_ref.shape[0], start, 0)
        lax.fori_loop(0, tail_ref.shape[0], wait, 0)

    load(step, slot).wait()

    def issue(r, carry):
        for s in range(2):
            _row_copy(hbuf.at[slot], r, xs_ref, dest_ref[base + 2 * r + s], ssem.at[slot]).start()
        return carry

    lax.fori_loop(0, tm, issue, 0, unroll=DMA_ISSUE_UNROLL)

    @pl.when(step >= 1)
    def _():
        wait_scatter((step + 2) % 3)

    @pl.when(step + 2 < n_steps)
    def _():
        load(step + 2, (step + 2) % 3).start()

    @pl.when(step == n_steps - 1)
    def _():
        wait_scatter(slot)


def moe_dispatch(h, dest, tails, rows, row_tile, *, tm=256):
    tokens, d = h.shape
    tm = min(tm, tokens)
    grid_spec = pltpu.PrefetchScalarGridSpec(
        num_scalar_prefetch=2,
        grid=(tokens // tm,),
        in_specs=[pl.BlockSpec(memory_space=pl.ANY)],
        out_specs=pl.BlockSpec(memory_space=pl.ANY),
        scratch_shapes=[pltpu.VMEM((3, tm, d), h.dtype), pltpu.VMEM((row_tile, d), h.dtype),
                        pltpu.SemaphoreType.DMA((3,)), pltpu.SemaphoreType.DMA((3,)),
                        pltpu.SemaphoreType.DMA(())],
    )
    return pl.pallas_call(
        functools.partial(_moe_dispatch_kernel, tm=tm),
        grid_spec=grid_spec,
        out_shape=jax.ShapeDtypeStruct((rows, d), h.dtype),
        compiler_params=_cparams(("arbitrary",)),
        name="moe_dispatch",
    )(dest, tails, h)


def _moe_expert_kernel(te_ref, nused_ref, xs_ref, wg_ref, wu_ref, wd_ref, ys_ref, wgb, wub, wdb):
    r = pl.program_id(0)
    used = r < nused_ref[0]
    changed = (r == 0) | (te_ref[r] != te_ref[jnp.maximum(r - 1, 0)])

    @pl.when(used & changed)
    def _():
        wgb[...] = wg_ref[...].astype(BF16)
        wub[...] = wu_ref[...].astype(BF16)
        wdb[...] = wd_ref[...].astype(BF16)

    @pl.when(used)
    def _():
        x = xs_ref[...].astype(BF16)
        gate_pre = _dot(x, wgb[...])
        hid = (gate_pre * jax.nn.sigmoid(gate_pre)) * _dot(x, wub[...])
        ys_ref[...] = _dot(hid.astype(BF16), wdb[...])

    @pl.when(jnp.logical_not(used))
    def _():
        ys_ref[...] = jnp.zeros_like(ys_ref)


def moe_experts(xs, tile_expert, n_used, wg, wu, wd, layer, *, tm):
    rows, d = xs.shape
    de = wg.shape[-1]
    grid_spec = pltpu.PrefetchScalarGridSpec(
        num_scalar_prefetch=2,
        grid=(rows // tm,),
        in_specs=[pl.BlockSpec((tm, d), lambda r, te, nu: (jnp.minimum(r, nu[0] - 1), 0)),
                  pl.BlockSpec((None, None, d, de), lambda r, te, nu: (layer, te[r], 0, 0)),
                  pl.BlockSpec((None, None, d, de), lambda r, te, nu: (layer, te[r], 0, 0)),
                  pl.BlockSpec((None, None, de, d), lambda r, te, nu: (layer, te[r], 0, 0))],
        out_specs=pl.BlockSpec((tm, d), lambda r, te, nu: (r, 0)),
        scratch_shapes=[pltpu.VMEM((d, de), BF16), pltpu.VMEM((d, de), BF16), pltpu.VMEM((de, d), BF16)],
    )
    return pl.pallas_call(
        _moe_expert_kernel,
        grid_spec=grid_spec,
        out_shape=jax.ShapeDtypeStruct((rows, d), F32),
        compiler_params=_cparams(("arbitrary",)),
        name="moe_experts",
    )(tile_expert, n_used, xs, wg, wu, wd)


def _moe_combine_kernel(dest_ref, x_ref, sel_ref, nw_ref, ys_ref, o_ref, buf_ref, sem, *, tm, normalize):
    step = pl.program_id(0)
    slot = step % 2

    def gather(tile, buf_slot):
        base = tile * (2 * tm)

        def issue(r, carry):
            for s in range(2):
                _row_copy(ys_ref, dest_ref[base + 2 * r + s], buf_ref.at[buf_slot, s], r, sem.at[buf_slot]).start()
            return carry

        lax.fori_loop(0, tm, issue, 0, unroll=DMA_ISSUE_UNROLL)

    @pl.when(step == 0)
    def _():
        gather(0, 0)

    @pl.when(step + 1 < pl.num_programs(0))
    def _():
        gather(step + 1, 1 - slot)

    for s in range(2):
        pltpu.make_async_copy(ys_ref.at[pl.ds(0, tm)], buf_ref.at[slot, s], sem.at[slot]).wait()
    sel = sel_ref[...]
    out = x_ref[...] + sel[:, SEL_G1:SEL_G1 + 1] * buf_ref[slot, 0] + sel[:, SEL_G2:SEL_G2 + 1] * buf_ref[slot, 1]
    o_ref[...] = _rms(out, nw_ref[...], NORM_EPS) if normalize else out


def moe_combine(x, sel, ys, dest, final_norm=None, *, tm=256):
    tokens, d = x.shape
    tm = min(tm, tokens)
    normalize = final_norm is not None
    nw = (final_norm if normalize else jnp.ones((d,), F32)).reshape(1, d)
    grid_spec = pltpu.PrefetchScalarGridSpec(
        num_scalar_prefetch=1,
        grid=(tokens // tm,),
        in_specs=[pl.BlockSpec((tm, d), lambda i, *_: (i, 0)),
                  pl.BlockSpec((tm, LANES), lambda i, *_: (i, 0)),
                  pl.BlockSpec((1, d), lambda i, *_: (0, 0)),
                  pl.BlockSpec(memory_space=pl.ANY)],
        out_specs=pl.BlockSpec((tm, d), lambda i, *_: (i, 0)),
        scratch_shapes=[pltpu.VMEM((2, 2, tm, d), F32), pltpu.SemaphoreType.DMA((2,))],
    )
    return pl.pallas_call(
        functools.partial(_moe_combine_kernel, tm=tm, normalize=normalize),
        grid_spec=grid_spec,
        out_shape=jax.ShapeDtypeStruct((tokens, d), F32),
        compiler_params=_cparams(("arbitrary",)),
        name="moe_combine",
    )(dest, x, sel, nw, ys)


def moe_block(x, norm_w, w_router, b_router, wg, wu, wd, layer, final_norm=None, *, tm=MOE_ROW_TILE):
    tokens, d = x.shape
    n_exp = wg.shape[1]
    h, sel = moe_router(x, norm_w, w_router, b_router)
    rank, counts = moe_rank(sel)
    padded = (counts[0, :n_exp] + (tm - 1)) // tm * tm
    ends = jnp.cumsum(padded)
    starts = ends - padded
    experts = sel[:, SEL_E1:SEL_E2 + 1].astype(jnp.int32)
    start_of = jnp.sum(jnp.where(experts[..., None] == jnp.arange(n_exp, dtype=jnp.int32), starts, 0), axis=-1)
    dest = (start_of + rank[:, SEL_E1:SEL_E2 + 1]).reshape(-1)
    rows = 2 * tokens + n_exp * tm
    tile_start = jnp.arange(rows // tm, dtype=jnp.int32) * tm
    tile_expert = jnp.minimum(jnp.sum(tile_start[:, None] >= ends[None, :], axis=1), n_exp - 1).astype(jnp.int32)
    n_used = (ends[-1] // tm).astype(jnp.int32).reshape(1)
    tails = jnp.concatenate([jnp.where(padded > 0, ends - tm, -1),
                             jnp.where(tile_start >= ends[-1], tile_start, -1)]).astype(jnp.int32)
    xs = moe_dispatch(h, dest, tails, rows, tm)
    ys = moe_experts(xs, tile_expert, n_used, wg, wu, wd, layer, tm=tm)
    return moe_combine(x, sel, ys, dest, final_norm)


def _proj_weights(w_in_l, w_vres_l):
    d = w_in_l.shape[0]
    mla0 = RW_COLS + DF_COLS
    vres = jnp.zeros((d, RW_V_RANK), F32) if w_vres_l is None else w_vres_l
    part_f = [w_in_l[:, 3 * RW_DIM:RW_COLS],
              vres, jnp.zeros((d, LANES - RW_V_RANK), F32),
              w_in_l[:, mla0 + ML_Q_RANK + ML_KV_RANK:mla0 + ML_COLS], jnp.zeros((d, LANES - ML_ROPE), F32)]
    part_a = [w_in_l[:, :3 * RW_DIM],
              w_in_l[:, RW_COLS:RW_COLS + DF_COLS],
              w_in_l[:, mla0 + ML_Q_RANK:mla0 + ML_Q_RANK + ML_KV_RANK],
              jnp.zeros((d, OFF_MQ - OFF_MKV - ML_KV_RANK), F32),
              w_in_l[:, mla0:mla0 + ML_Q_RANK]]
    return jnp.concatenate(part_f, axis=1).astype(BF16), jnp.concatenate(part_a, axis=1).astype(BF16)


def _pad_rows(w, rows, at=0):
    out = jnp.zeros((rows, w.shape[1]), w.dtype)
    return lax.dynamic_update_slice(out, w, (at, 0))


def kernel(x, mem, positions, rel_bias, final_norm, norm_mix, w_in, w_in_vres, w_out, tm_mu, tm_mu_vres, tm_w0, tm_w2, tm_a0, tm_a2, tm_v0, tm_v2, tm_g2, tm_k_k, tm_k_a, tm_r_k, tm_ln_w, tm_ln_b, da_lq1, da_lk1, da_lq2, da_lk2, da_subln, mla_q_norm, mla_wq_b, mla_kv_norm, mla_wkv_b, norm_cross, norm_mem, ca_wq, ca_wkv, ca_wo, norm_ffn, moe_w_group, moe_b_group, moe_w_expert, moe_b_expert, moe_w_gate, moe_w_up, moe_w_down):
    batch, seq, d = x.shape
    tokens = batch * seq
    depth = norm_mix.shape[0]
    xf = x.reshape(tokens, d)
    memf = mem.reshape(-1, d)
    positions = positions.astype(jnp.int32)

    head_of_lane = jnp.arange(RW_DIM) // RW_HEAD_DIM
    seg = (head_of_lane[:, None] == jnp.arange(LANES)[None, :]).astype(BF16)
    seg_t = seg.T
    row = lambda v: v.reshape(1, -1)

    v_first = None
    for l in range(depth):
        w_f, w_a = _proj_weights(w_in[l], None if l == 0 else w_in_vres[l - 1])
        pa, proj = norm_matmul(xf, norm_mix[l], w_a, w_f, out_dtype=BF16, tm=1024, tn=PROJ_A_COLS // 3)

        mu = tm_mu[l]
        prm = dict(mu_r=row(mu[:RW_DIM]), mu_k=row(mu[RW_DIM:2 * RW_DIM]), mu_v=row(mu[2 * RW_DIM:3 * RW_DIM]),
                   mu_l=row(mu[3 * RW_DIM:]), w0=row(tm_w0[l]), a0=row(tm_a0[l]),
                   w2=_pad_rows(tm_w2[l], LANES, 0), a2=_pad_rows(tm_a2[l], LANES, RW_W_RANK),
                   g2=tm_g2[l].astype(BF16), k_k=row(tm_k_k[l]), k_a=row(tm_k_a[l]), r_k=row(tm_r_k[l]),
                   seg=seg, seg_t=seg_t)
        if l > 0:
            prm.update(mu_vr=jnp.pad(row(tm_mu_vres[l - 1]), ((0, 0), (0, LANES - RW_V_RANK))),
                       v0=row(tm_v0[l - 1]), v2=_pad_rows(tm_v2[l - 1], LANES, 0))
        r, lw, k, v, kap, beta, gate, bonus = rwkv_prep(pa, proj, batch, v_first, prm)
        if l == 0:
            v_first = v
        o = rwkv_scan(r, lw, k, v, kap, beta, batch)
        rwkv_raw = (o, bonus, gate, tm_ln_w[l], tm_ln_b[l], seg, seg_t)

        lambda_init = 0.8 - 0.6 * math.exp(-0.3 * l)
        lam = (jnp.exp(jnp.sum(da_lq1[l] * da_lk1[l])) - jnp.exp(jnp.sum(da_lq2[l] * da_lk2[l])) + lambda_init)
        y_b = diff_attention(pa, positions, rel_bias, lam, lambda_init, da_subln[l])

        wq = mla_wq_b[l].reshape(ML_Q_RANK, ML_HEADS, ML_NOPE + ML_ROPE)
        wq_pe = jnp.pad(wq[:, :, ML_NOPE:], ((0, 0), (0, 0), (0, LANES - ML_ROPE)))
        wq_all = jnp.concatenate([wq[:, :, :ML_NOPE].reshape(ML_Q_RANK, -1),
                                  wq_pe.reshape(ML_Q_RANK, -1)], axis=1).astype(BF16)
        wkv = mla_wkv_b[l].reshape(ML_KV_RANK, ML_HEADS, ML_NOPE + ML_V)
        wkv = jnp.concatenate([wkv[:, :, :ML_NOPE].reshape(ML_KV_RANK, -1),
                               wkv[:, :, ML_NOPE:].reshape(ML_KV_RANK, -1)], axis=1).astype(BF16)
        qf, kf, v_mla = mla_prep(pa, proj, positions, mla_q_norm[l], mla_kv_norm[l], wq_all, wkv)
        y_c = mla_attention(qf, kf, v_mla, batch)

        wo = w_out[l].astype(BF16)
        kv_mem = norm_matmul(memf, norm_mem[l], ca_wkv[l].astype(BF16), out_dtype=BF16)
        xf = mix_cross_block(rwkv_raw, [y_b, y_c], [wo[:RW_DIM], wo[RW_DIM:RW_DIM + DF_DIM], wo[RW_DIM + DF_DIM:]],
                             xf, batch, norm_cross[l], ca_wq[l].astype(BF16), kv_mem, ca_wo[l].astype(BF16))

        w_router = jnp.concatenate(
            [moe_w_expert[l], moe_w_group[l], jnp.zeros((d, LANES - MOE_EXPERTS - MOE_GROUPS), F32)], axis=1)
        b_router = jnp.concatenate(
            [moe_b_expert[l], moe_b_group[l], jnp.zeros((LANES - MOE_EXPERTS - MOE_GROUPS,), F32)]).reshape(1, LANES)
        xf = moe_block(xf, norm_ffn[l], w_router, b_router, moe_w_gate, moe_w_up, moe_w_down, l,
                       final_norm if l == depth - 1 else None)

    return xf.reshape(batch, seq, d)
```

```python
import functools
import math

import jax
import jax.numpy as jnp
from jax import lax
from jax.experimental import pallas as pl
from jax.experimental.pallas import tpu as pltpu

F32 = jnp.float32
BF16 = jnp.bfloat16

NORM_EPS = 1e-6
ROPE_THETA = 10000.0

RW_HEADS = 16
RW_HEAD_DIM = 64
RW_DIM = RW_HEADS * RW_HEAD_DIM
RW_W_RANK = 64
RW_A_RANK = 64
RW_G_RANK = 128
RW_V_RANK = 32
RW_LORA = RW_W_RANK + RW_A_RANK + RW_G_RANK
RW_LN_EPS = 64e-5
RW_COLS = 3 * RW_DIM + RW_LORA

DF_HEADS = 4
DF_HEAD_DIM = 64
DF_V_DIM = 2 * DF_HEAD_DIM
DF_QK = DF_HEADS * 2 * DF_HEAD_DIM
DF_DIM = DF_HEADS * DF_V_DIM
DF_COLS = 2 * DF_QK + DF_DIM
DF_SUBLN_EPS = 1e-5

ML_HEADS = 4
ML_Q_RANK = 384
ML_KV_RANK = 256
ML_NOPE = 128
ML_ROPE = 64
ML_V = 128
ML_DIM = ML_HEADS * ML_V
ML_COLS = ML_Q_RANK + ML_KV_RANK + ML_ROPE

REL_BUCKETS = 32
REL_MAX_DIST = 128

CA_HEADS = 4
CA_HEAD_DIM = 128
CA_DIM = CA_HEADS * CA_HEAD_DIM

MOE_GROUPS = 4
MOE_PER_GROUP = 8
MOE_EXPERTS = MOE_GROUPS * MOE_PER_GROUP

LANES = 128
SCAN_CHUNK = 64
SCAN_GROUP = 4
SCAN_BATCHES = 2
ATTN_TILE = 512
ONES_ROWS = 16
POST_ROW_BLOCKS = 4
DMA_ISSUE_UNROLL = 8
MOE_ROW_TILE = 256
VMEM_LIMIT = 56 * 1024 * 1024
NEG_BIG = -1e30

LOG2E = 1.4426950408889634

OFF_LORA = 0
OFF_VRES = OFF_LORA + RW_LORA
OFF_KPE = OFF_VRES + LANES
PROJ_F_COLS = OFF_KPE + LANES
OFF_R = 0
OFF_K = RW_DIM
OFF_V = 2 * RW_DIM
OFF_DQ = 3 * RW_DIM
OFF_DK = OFF_DQ + DF_QK
OFF_DV = OFF_DK + DF_QK
OFF_MKV = OFF_DV + DF_DIM
OFF_MQ = 13 * ML_Q_RANK
PROJ_A_COLS = OFF_MQ + ML_Q_RANK


def _cparams(sem, vmem=VMEM_LIMIT, flags=None):
    return pltpu.CompilerParams(dimension_semantics=sem, vmem_limit_bytes=vmem, flags=flags)


def _dot(a, b):
    return jnp.dot(a, b, preferred_element_type=F32)


def _dot_t(a, b):
    return lax.dot_general(a, b, (((1,), (1,)), ((), ())), preferred_element_type=F32)


def _split3(x):
    hi = x.astype(BF16)
    r1 = x - hi.astype(F32)
    mid = r1.astype(BF16)
    lo = (r1 - mid.astype(F32)).astype(BF16)
    return hi, mid, lo


def _dot_rhs01(x, ones_bf16):
    hi = x.astype(BF16)
    lo = (x - hi.astype(F32)).astype(BF16)
    return _dot(hi, ones_bf16) + _dot(lo, ones_bf16)


def _dot_x3(a, b):
    ah = a.astype(BF16)
    al = (a - ah.astype(F32)).astype(BF16)
    bh = b.astype(BF16)
    bl = (b - bh.astype(F32)).astype(BF16)
    return _dot(ah, bh) + _dot(ah, bl) + _dot(al, bh)


def _rms(x, w, eps):
    ms = jnp.mean(x * x, axis=-1, keepdims=True)
    return x * lax.rsqrt(ms + eps) * w


def _norm_matmul_kernel(*refs, eps, has_side):
    if has_side:
        x_ref, nw_ref, w_ref, ws_ref, o_ref, os_ref, xn_ref = refs
    else:
        x_ref, nw_ref, w_ref, o_ref, xn_ref = refs

    @pl.when(pl.program_id(1) == 0)
    def _():
        xn_ref[...] = _rms(x_ref[...], nw_ref[...], eps).astype(BF16)
        if has_side:
            os_ref[...] = _dot(xn_ref[...], ws_ref[...])

    o_ref[...] = _dot(xn_ref[...], w_ref[...]).astype(o_ref.dtype)


def norm_matmul(x, nw, w, w_side=None, *, out_dtype=F32, tm=512, tn=None, eps=NORM_EPS):
    m, d = x.shape
    n = w.shape[1]
    tm = min(tm, m)
    tn = n if tn is None else tn
    has_side = w_side is not None
    in_specs = [pl.BlockSpec((tm, d), lambda i, j: (i, 0)),
                pl.BlockSpec((1, d), lambda i, j: (0, 0)),
                pl.BlockSpec((d, tn), lambda i, j: (0, j))]
    out_specs = [pl.BlockSpec((tm, tn), lambda i, j: (i, j))]
    out_shape = [jax.ShapeDtypeStruct((m, n), out_dtype)]
    args = [x, nw.reshape(1, d), w]
    if has_side:
        ns = w_side.shape[1]
        in_specs.append(pl.BlockSpec((d, ns), lambda i, j: (0, 0)))
        out_specs.append(pl.BlockSpec((tm, ns), lambda i, j: (i, 0)))
        out_shape.append(jax.ShapeDtypeStruct((m, ns), F32))
        args.append(w_side)
    outs = pl.pallas_call(
        functools.partial(_norm_matmul_kernel, eps=eps, has_side=has_side),
        grid=(m // tm, n // tn),
        in_specs=in_specs,
        out_specs=out_specs,
        out_shape=out_shape,
        scratch_shapes=[pltpu.VMEM((tm, d), BF16)],
        compiler_params=_cparams(("parallel", "arbitrary")),
        name="norm_matmul",
    )(*args)
    return outs if has_side else outs[0]


def _softplus(z):
    return jnp.maximum(z, 0.0) + jnp.log(1.0 + jnp.exp(-jnp.abs(z)))


def _rwkv_prep_kernel(*refs, has_vres):
    if has_vres:
        (pr_ref, pk_ref, pv_ref, pl_ref, pvr_ref, vfirst_ref,
         mu_r, mu_k, mu_v, mu_l, mu_vr, w0, w2, a0, a2, g2, v0, v2,
         k_k, k_a, r_k, seg, seg_t,
         r_o, lw_o, k_o, v_o, kap_o, beta_o, g_o, bonus_o,
         last_r, last_k, last_v, last_l, last_vr) = refs
    else:
        (pr_ref, pk_ref, pv_ref, pl_ref,
         mu_r, mu_k, mu_v, mu_l, w0, w2, a0, a2, g2,
         k_k, k_a, r_k, seg, seg_t,
         r_o, lw_o, k_o, v_o, kap_o, beta_o, g_o, bonus_o,
         last_r, last_k, last_v, last_l) = refs
    t = pl.program_id(1)

    def shifted(p_ref, last_ref, mu_ref):
        p = p_ref[...].astype(F32)
        n = p.shape[0]
        carried = jnp.where(t == 0, 0.0, last_ref[0:1, :])
        row = lax.broadcasted_iota(jnp.int32, p.shape, 0)
        prev = jnp.where(row == 0, carried, pltpu.roll(p, 1, axis=0))
        last_ref[0:1, :] = p[n - 1:n, :]
        return p + mu_ref[...] * (prev - p)

    r = shifted(pr_ref, last_r, mu_r)
    k = shifted(pk_ref, last_k, mu_k)
    v = shifted(pv_ref, last_v, mu_v)
    lora = shifted(pl_ref, last_l, mu_l)
    wl = lora[:, :LANES]
    gl = lora[:, LANES:]

    lane = lax.broadcasted_iota(jnp.int32, wl.shape, 1)
    wl_t = jnp.where(lane < RW_W_RANK, jnp.tanh(wl), 0.0)
    al = jnp.where(lane >= RW_W_RANK, wl, 0.0)
    w_log = -_softplus(-(w0[...] + _dot_x3(wl_t, w2[...]))) - 0.5
    lw_o[...] = -jnp.exp(w_log)
    a = jax.nn.sigmoid(a0[...] + _dot_x3(al, a2[...]))
    g_o[...] = _dot(jax.nn.sigmoid(gl).astype(BF16), g2[...]).astype(g_o.dtype)

    segm, segm_t = seg[...], seg_t[...]

    def head_sum(x):
        return _dot_rhs01(_dot_rhs01(x, segm), segm_t)

    kk = k * k_k[...]
    kk = kk * lax.rsqrt(jnp.maximum(head_sum(kk * kk), 1e-24))
    k = k * (1.0 + (a - 1.0) * k_a[...])
    if has_vres:
        vr = shifted(pvr_ref, last_vr, mu_vr)
        mix = jax.nn.sigmoid(v0[...] + _dot_x3(vr, v2[...]))
        v = v + (vfirst_ref[...] - v) * mix
    r_o[...] = r.astype(r_o.dtype)
    k_o[...] = k.astype(k_o.dtype)
    v_o[...] = v.astype(v_o.dtype)
    kap_o[...] = kk.astype(kap_o.dtype)
    beta_o[...] = (kk * a).astype(beta_o.dtype)
    bonus_o[...] = (head_sum(r * k * r_k[...]) * v).astype(bonus_o.dtype)


def rwkv_prep(pa, proj, batch, vfirst, prm, *, tt=512):
    tokens = proj.shape[0]
    seq = tokens // batch
    tt = min(tt, seq)
    nt = seq // tt
    has_vres = vfirst is not None
    d = RW_DIM

    def rows(width, col):
        return pl.BlockSpec((tt, width), lambda b, t, col=col: (b * nt + t, col))

    def full(shape):
        return pl.BlockSpec(shape, lambda b, t: (0, 0))

    in_specs = [rows(d, OFF_R // d), rows(d, OFF_K // d), rows(d, OFF_V // d),
                rows(RW_LORA, OFF_LORA // RW_LORA)]
    args = [pa, pa, pa, proj]
    if has_vres:
        in_specs += [rows(LANES, OFF_VRES // LANES), rows(d, 0)]
        args += [proj, vfirst]
    names = ["mu_r", "mu_k", "mu_v", "mu_l"] + (["mu_vr"] if has_vres else []) + ["w0", "w2", "a0", "a2", "g2"]
    names += (["v0", "v2"] if has_vres else []) + ["k_k", "k_a", "r_k", "seg", "seg_t"]
    for nm in names:
        in_specs.append(full(prm[nm].shape))
        args.append(prm[nm])
    out_spec = pl.BlockSpec((tt, d), lambda b, t: (b * nt + t, 0))
    scratch = [pltpu.VMEM((8, d), F32)] * 3 + [pltpu.VMEM((8, RW_LORA), F32)]
    if has_vres:
        scratch.append(pltpu.VMEM((8, LANES), F32))
    return pl.pallas_call(
        functools.partial(_rwkv_prep_kernel, has_vres=has_vres),
        grid=(batch, nt),
        in_specs=in_specs,
        out_specs=[out_spec] * 8,
        out_shape=[jax.ShapeDtypeStruct((tokens, d), F32 if i == 1 else BF16) for i in range(8)],
        scratch_shapes=scratch,
        compiler_params=_cparams(("arbitrary", "arbitrary")),
        name="rwkv_prep",
    )(*args)


def _rwkv_scan_kernel(r_ref, lw_ref, k_ref, v_ref, kap_ref, beta_ref, tril_ref, bmask_ref,
                      o_ref, ht_ref):
    @pl.when(pl.program_id(1) == 0)
    def _():
        ht_ref[...] = jnp.zeros_like(ht_ref)

    n_batch, c, d = lw_ref.shape
    w = ht_ref.shape[1]
    g = w // RW_HEAD_DIM
    bmask = bmask_ref[...]
    bmask_b = bmask.astype(BF16)
    tril3 = tril_ref[...]
    t_idx = lax.broadcasted_iota(jnp.int32, (c, w), 0)
    s_idx = lax.broadcasted_iota(jnp.int32, (c, w), 1) % c
    strict = t_idx > s_idx
    incl = t_idx >= s_idx
    n_sq = int(math.log2(c))

    def stack(x):
        return jnp.concatenate([x.astype(BF16)] * g, axis=0) * bmask_b

    sls = [(bi, slice(None), slice(lo, lo + w)) for bi in range(n_batch) for lo in range(0, d, w)]
    groups = range(len(sls))
    lw = [lw_ref[sl] for sl in sls]
    cum = [_dot(tril3, jnp.concatenate(_split3(x), axis=0)) for x in lw]
    total = [x[c - 1:c, :] for x in cum]
    ar = [jnp.concatenate([-kap_ref[sls[gi]] * jnp.exp(cum[gi] - lw[gi]), r_ref[sls[gi]] * jnp.exp(cum[gi])],
                          axis=0).astype(BF16) for gi in groups]
    p_inv = [jnp.exp(-x) for x in cum]
    b_s = [stack(beta_ref[sls[gi]] * p_inv[gi]) for gi in groups]
    k_s = [stack(k_ref[sls[gi]] * p_inv[gi]) for gi in groups]
    v_n = [v_ref[sl] for sl in sls]
    v_s = [stack(x) for x in v_n]

    arb = [_dot_t(ar[gi], b_s[gi]) for gi in groups]
    ark = [_dot_t(ar[gi], k_s[gi]) for gi in groups]
    ab = [jnp.where(strict, m[:c], 0.0) for m in arb]
    rb = [jnp.where(incl, m[c:], 0.0).astype(BF16) for m in arb]
    akrk = [jnp.concatenate([jnp.where(strict, m[:c], 0.0), jnp.where(incl, m[c:], 0.0)], axis=0).astype(BF16)
            for m in ark]

    ht = [ht_ref[gi] for gi in groups]
    base = [_dot_t(ar[gi], ht[gi].astype(BF16)) + _dot(akrk[gi], v_s[gi]) for gi in groups]
    x = [m[:c] for m in base]
    lp = ab
    for i in range(n_sq):
        lpb = [m.astype(BF16) for m in lp]
        x = [x[gi] + _dot(lpb[gi], stack(x[gi])) for gi in groups]
        if i < n_sq - 1:
            lp = [_dot(lpb[gi], stack(lp[gi])) for gi in groups]
    for gi in groups:
        o_ref[sls[gi]] = base[gi][c:] + _dot(rb[gi], stack(x[gi]))

    for gi in groups:
        p_rem = jnp.exp(total[gi] - cum[gi])
        z = jnp.concatenate([beta_ref[sls[gi]] * p_rem, k_ref[sls[gi]] * p_rem], axis=0).astype(BF16)
        uv_t = jnp.concatenate([x[gi], v_n[gi].astype(F32)], axis=0).T.astype(BF16)
        ht_ref[gi] = ht[gi] * jnp.exp(total[gi]) + bmask * _dot(uv_t, z)


def rwkv_scan(r, lw, k, v, kap, beta, batch):
    tokens, d = r.shape
    seq = tokens // batch
    c = min(SCAN_CHUNK, seq)
    nc = seq // c
    gw = SCAN_GROUP * RW_HEAD_DIM
    rr = SCAN_GROUP * c
    assert c == RW_HEAD_DIM, "the stacking mask doubles as the head-block mask of the state"
    bb = math.gcd(batch, SCAN_BATCHES)
    tril = jnp.tile((jnp.arange(c)[:, None] >= jnp.arange(c)[None, :]).astype(BF16), (1, 3))
    bmask = (jnp.arange(rr)[:, None] // c == jnp.arange(gw)[None, :] // RW_HEAD_DIM).astype(F32)
    blk = pl.BlockSpec((bb, c, d), lambda b, i: (b, i, 0))
    as3d = lambda a: a.reshape(batch, seq, d)
    out = pl.pallas_call(
        _rwkv_scan_kernel,
        grid=(batch // bb, nc),
        in_specs=[blk] * 6 + [pl.BlockSpec((c, 3 * c), lambda b, i: (0, 0)),
                              pl.BlockSpec((rr, gw), lambda b, i: (0, 0))],
        out_specs=blk,
        out_shape=jax.ShapeDtypeStruct((batch, seq, d), F32),
        scratch_shapes=[pltpu.VMEM((bb * (d // gw), gw, gw), F32)],
        compiler_params=_cparams(("arbitrary", "arbitrary")),
        name="rwkv_scan",
    )(as3d(r), as3d(lw), as3d(k), as3d(v), as3d(kap), as3d(beta), tril, bmask)
    return out.reshape(tokens, d)


def _rwkv_post_kernel(o_ref, bonus_ref, g_ref, lnw_ref, lnb_ref, seg, seg_t, y_ref):
    segm, segm_t = seg[...], seg_t[...]

    def head_means(xs):
        sums = [_dot_rhs01(x, segm) for x in xs]
        return [_dot_rhs01(s, segm_t) * (1.0 / RW_HEAD_DIM) for s in sums]

    rows = o_ref.shape[0] // POST_ROW_BLOCKS
    blocks = [slice(i * rows, (i + 1) * rows) for i in range(POST_ROW_BLOCKS)]
    o = [o_ref[b, :] for b in blocks]
    dlt = [x - m for x, m in zip(o, head_means(o))]
    var = head_means([x * x for x in dlt])
    for b, x, v in zip(blocks, dlt, var):
        y = x * lax.rsqrt(v + RW_LN_EPS) * lnw_ref[...] + lnb_ref[...]
        y_ref[b, :] = ((y + bonus_ref[b, :]) * g_ref[b, :]).astype(y_ref.dtype)


def _t5_thresholds():
    max_exact = REL_BUCKETS // 2
    thr = list(range(1, max_exact))
    n = max_exact
    for bucket in range(max_exact, REL_BUCKETS):
        while True:
            large = max_exact + int(math.log(max(n, max_exact) / max_exact)
                                    / math.log(REL_MAX_DIST / max_exact) * (REL_BUCKETS - max_exact))
            if min(large, REL_BUCKETS - 1) >= bucket:
                break
            n += 1
        thr.append(n)
    return thr


T5_THRESHOLDS = _t5_thresholds()
T5_FAR = T5_THRESHOLDS[-1]


def _softmax_tiles(s_list, c_list, states, vt_list):
    stats = []
    for s_t, c, (m_old, _) in zip(s_list, c_list, states):
        m_new = jnp.maximum(m_old, jnp.max(s_t, axis=0, keepdims=True) + c)
        stats.append((m_new, jnp.exp2(m_old - m_new), jnp.exp2(s_t - (m_new - c)).astype(BF16)))
    return tuple((m_new, alpha * acc + _dot(vt, p_t))
                 for (m_new, alpha, p_t), (_, acc), vt in zip(stats, states, vt_list))


def _transpose_into(vt_ref, v_ref, chunk, heads, width):
    seq = v_ref.shape[0]
    blk = width + ONES_ROWS
    for c in range(seq // chunk):
        cols = slice(c * chunk, (c + 1) * chunk)
        vt = v_ref[cols, :].astype(F32).T.astype(BF16)
        for h in range(heads):
            vt_ref[h * blk:h * blk + width, cols] = vt[h * width:(h + 1) * width]
            vt_ref[h * blk + width:(h + 1) * blk, cols] = jnp.ones((ONES_ROWS, chunk), BF16)


def _diff_attn_kernel(qfirst_ref, klast_ref, q_ref, k_ref, v_ref, qpos_ref, kpos_ref, subln_ref, table_ref, lam_ref,
                      o_ref, vt_ref, *, tq, tk, scale2, out_scale):
    b, i = pl.program_id(0), pl.program_id(1)
    nq = pl.num_programs(1)
    seq = k_ref.shape[0]
    nk = seq // tk
    w = DF_V_DIM
    vblk = w + ONES_ROWS

    @pl.when(i == 0)
    def _():
        _transpose_into(vt_ref, v_ref, tk, DF_HEADS, w)

    n_tiles = (i * tq + tq - 1) // tk + 1
    qf = qfirst_ref[b * nq + i]
    n_far = lax.while_loop(
        lambda j: (j * tk + tk - 1 <= i * tq) & (qf - klast_ref[b * nk + jnp.minimum(j, nk - 1)] >= T5_FAR),
        lambda j: j + 1, jnp.int32(0))

    dist = lax.broadcasted_iota(jnp.int32, (1, LANES), 1)
    qpos = qpos_ref[...]
    q_idx = i * tq + lax.broadcasted_iota(jnp.int32, (tk, tq), 1)
    k_off = lax.broadcasted_iota(jnp.int32, (tk, tq), 0)
    lane = lax.broadcasted_iota(jnp.int32, (tq, w), 1)

    bias_rows, c_far, qm = [], [], []
    for h in range(DF_HEADS):
        bias_vec = jnp.full((1, LANES), table_ref[h], F32)
        for bucket, thr in enumerate(T5_THRESHOLDS, start=1):
            bias_vec = jnp.where(dist >= thr, table_ref[bucket * DF_HEADS + h], bias_vec)
        bias_rows.append(jnp.broadcast_to(bias_vec * LOG2E, (tk, LANES)))
        c_far.append(table_ref[(REL_BUCKETS - 1) * DF_HEADS + h] * LOG2E)
        qh = q_ref[:, h * w:(h + 1) * w].astype(F32) * scale2
        qm.append([jnp.where((lane >= mi * DF_HEAD_DIM) & (lane < (mi + 1) * DF_HEAD_DIM), qh, 0.0).astype(BF16)
                   for mi in range(2)])

    def tiles(j, h):
        off = pl.multiple_of(j * tk, tk)
        return k_ref[pl.ds(off, tk), h * w:(h + 1) * w], vt_ref[h * vblk:(h + 1) * vblk, pl.ds(off, tk)], off

    chains = [(h, mi) for h in range(DF_HEADS) for mi in range(2)]

    def far_body(j, st):
        kv = [tiles(j, h) for h in range(DF_HEADS)]
        s = [_dot_t(kv[h][0], qm[h][mi]) for h, mi in chains]
        return _softmax_tiles(s, [c_far[h] for h, _ in chains], st, [kv[h][1] for h, _ in chains])

    def near_body(j, st):
        off = pl.multiple_of(j * tk, tk)
        n = jnp.clip(qpos - kpos_ref[pl.ds(off, tk), :], 0, LANES - 1)
        keep = q_idx >= off + k_off
        kv = [tiles(j, h) for h in range(DF_HEADS)]
        bias = [jnp.concatenate(
            [jnp.take_along_axis(bias_rows[h], n[:, cb * LANES:(cb + 1) * LANES], axis=1)
             for cb in range(tq // LANES)], axis=1) for h in range(DF_HEADS)]
        s = [jnp.where(keep, _dot_t(kv[h][0], qm[h][mi]) + bias[h], NEG_BIG) for h, mi in chains]
        return _softmax_tiles(s, [0.0] * len(chains), st, [kv[h][1] for h, _ in chains])

    init = tuple((jnp.full((1, tq), NEG_BIG, F32), jnp.zeros((w + ONES_ROWS, tq), F32))
                 for _ in range(2 * DF_HEADS))
    st = lax.fori_loop(0, n_far, far_body, init)
    st = lax.fori_loop(n_far, n_tiles, near_body, st)
    for h in range(DF_HEADS):
        a0, a1 = st[2 * h][1], st[2 * h + 1][1]
        d_t = a0[:w] / a0[w:w + 1] - lam_ref[0] * (a1[:w] / a1[w:w + 1])
        ms = jnp.mean(d_t * d_t, axis=0, keepdims=True)
        y_t = d_t * lax.rsqrt(ms + DF_SUBLN_EPS) * (subln_ref[...] * out_scale)
        o_ref[:, h * w:(h + 1) * w] = y_t.T.astype(o_ref.dtype)


def diff_attention(pa, positions, rel_bias, lam, lambda_init, subln_w, *, tq=ATTN_TILE):
    batch, seq = positions.shape
    tokens = batch * seq
    tq = min(tq, seq)
    tk = tq
    nq, nk = seq // tq, seq // tk
    qfirst = positions[:, ::tq].reshape(-1)
    klast = positions[:, tk - 1::tk].reshape(-1)
    qpos = positions.reshape(batch, 1, seq)
    kpos = positions.reshape(batch, seq, 1)
    wd = DF_DIM
    grid_spec = pltpu.PrefetchScalarGridSpec(
        num_scalar_prefetch=2,
        grid=(batch, nq),
        in_specs=[pl.BlockSpec((tq, wd), lambda b, i, *_: (b * nq + i, OFF_DQ // wd)),
                  pl.BlockSpec((seq, wd), lambda b, i, *_: (b, OFF_DK // wd)),
                  pl.BlockSpec((seq, wd), lambda b, i, *_: (b, OFF_DV // wd)),
                  pl.BlockSpec((None, 1, tq), lambda b, i, *_: (b, 0, i)),
                  pl.BlockSpec((None, seq, 1), lambda b, i, *_: (b, 0, 0)),
                  pl.BlockSpec((DF_V_DIM, 1), lambda b, i, *_: (0, 0)),
                  pl.BlockSpec(memory_space=pltpu.SMEM),
                  pl.BlockSpec(memory_space=pltpu.SMEM)],
        out_specs=pl.BlockSpec((tq, wd), lambda b, i, *_: (b * nq + i, 0)),
        scratch_shapes=[pltpu.VMEM((DF_HEADS * (DF_V_DIM + ONES_ROWS), seq), BF16)],
    )
    return pl.pallas_call(
        functools.partial(_diff_attn_kernel, tq=tq, tk=tk, scale2=DF_HEAD_DIM ** -0.5 * LOG2E,
                          out_scale=1.0 - lambda_init),
        grid_spec=grid_spec,
        out_shape=jax.ShapeDtypeStruct((tokens, DF_DIM), BF16),
        compiler_params=_cparams(("arbitrary", "arbitrary")),
        name="diff_attention",
    )(qfirst, klast, pa, pa, pa, qpos, kpos, subln_w.reshape(DF_V_DIM, 1), rel_bias.reshape(-1), lam.reshape(1))


ML_QK_PAD = 2 * LANES


def _rope_block(x, cos, sin):
    half = ML_ROPE // 2
    lane = lax.broadcasted_iota(jnp.int32, x.shape, 1)
    rot = jnp.where(lane < half, -pltpu.roll(x, LANES - half, axis=1),
                    jnp.where(lane < ML_ROPE, pltpu.roll(x, half, axis=1), 0.0))
    return x * cos + rot * sin


def _mla_prep_kernel(mq_ref, mkv_ref, kpe_ref, pos_ref, qn_w, kvn_w, wq_ref, wkv_ref, freq_ref,
                     qf_o, kf_o, v_o, *, qscale):
    ang = pos_ref[...].astype(F32) * freq_ref[...]
    cos, sin = jnp.cos(ang), jnp.sin(ang)
    qc = _rms(mq_ref[...].astype(F32), qn_w[...], NORM_EPS).astype(BF16)
    q_all = _dot(qc, wq_ref[...]) * qscale
    kvc = _rms(mkv_ref[...].astype(F32), kvn_w[...], NORM_EPS).astype(BF16)
    kvb = _dot(kvc, wkv_ref[...])
    kpe = _rope_block(kpe_ref[...], cos, sin).astype(BF16)
    nope_w = ML_HEADS * ML_NOPE
    for h in range(ML_HEADS):
        lo = h * ML_QK_PAD
        qf_o[:, lo:lo + LANES] = q_all[:, h * LANES:(h + 1) * LANES].astype(BF16)
        qf_o[:, lo + LANES:lo + 2 * LANES] = _rope_block(
            q_all[:, nope_w + h * LANES:nope_w + (h + 1) * LANES], cos, sin).astype(BF16)
        kf_o[:, lo:lo + LANES] = kvb[:, h * LANES:(h + 1) * LANES].astype(BF16)
        kf_o[:, lo + LANES:lo + 2 * LANES] = kpe
    v_o[...] = kvb[:, nope_w:].astype(BF16)


def mla_prep(pa, pf, positions, q_norm, kv_norm, wq_all, wkv, *, tm=512):
    tokens = pa.shape[0]
    tm = min(tm, tokens)
    half = ML_ROPE // 2
    inv_freq = ROPE_THETA ** (-jnp.arange(half, dtype=F32) / half)
    freq = jnp.concatenate([inv_freq, inv_freq, jnp.zeros((LANES - ML_ROPE,), F32)]).reshape(1, LANES)

    def full(a):
        return pl.BlockSpec(a.shape, lambda i: (0, 0))

    qn_w = q_norm.reshape(1, -1)
    kvn_w = kv_norm.reshape(1, -1)
    wide = ML_HEADS * ML_QK_PAD
    return pl.pallas_call(
        functools.partial(_mla_prep_kernel, qscale=(ML_NOPE + ML_ROPE) ** -0.5 * LOG2E),
        grid=(tokens // tm,),
        in_specs=[pl.BlockSpec((tm, ML_Q_RANK), lambda i: (i, OFF_MQ // ML_Q_RANK)),
                  pl.BlockSpec((tm, ML_KV_RANK), lambda i: (i, OFF_MKV // ML_KV_RANK)),
                  pl.BlockSpec((tm, LANES), lambda i: (i, OFF_KPE // LANES)),
                  pl.BlockSpec((tm, 1), lambda i: (i, 0)),
                  full(qn_w), full(kvn_w), full(wq_all), full(wkv), full(freq)],
        out_specs=[pl.BlockSpec((tm, wide), lambda i: (i, 0)),
                   pl.BlockSpec((tm, wide), lambda i: (i, 0)),
                   pl.BlockSpec((tm, ML_DIM), lambda i: (i, 0))],
        out_shape=[jax.ShapeDtypeStruct((tokens, wide), BF16),
                   jax.ShapeDtypeStruct((tokens, wide), BF16),
                   jax.ShapeDtypeStruct((tokens, ML_DIM), BF16)],
        compiler_params=_cparams(("parallel",)),
        name="mla_prep",
    )(pa, pa, pf, positions.reshape(tokens, 1), qn_w, kvn_w, wq_all, wkv, freq)


def _mla_attn_kernel(q_ref, k_ref, v_ref, o_ref, vt_ref, *, tq, tk):
    i = pl.program_id(1)
    wq = ML_QK_PAD
    vblk = ML_V + ONES_ROWS

    @pl.when(i == 0)
    def _():
        _transpose_into(vt_ref, v_ref, tk, ML_HEADS, ML_V)

    n_tiles = (i * tq + tq - 1) // tk + 1
    n_full = (i * tq + 1) // tk
    q_idx = i * tq + lax.broadcasted_iota(jnp.int32, (tk, tq), 1)
    k_off = lax.broadcasted_iota(jnp.int32, (tk, tq), 0)
    qh = [q_ref[:, h * wq:(h + 1) * wq] for h in range(ML_HEADS)]

    def tiles(j, h):
        off = pl.multiple_of(j * tk, tk)
        return (k_ref[pl.ds(off, tk), h * wq:(h + 1) * wq],
                vt_ref[h * vblk:(h + 1) * vblk, pl.ds(off, tk)], off)

    heads = range(ML_HEADS)

    def full_body(j, st):
        kv = [tiles(j, h) for h in heads]
        s = [_dot_t(kv[h][0], qh[h]) for h in heads]
        return _softmax_tiles(s, [0.0] * ML_HEADS, st, [kv[h][1] for h in heads])

    def diag_body(j, st):
        kv = [tiles(j, h) for h in heads]
        keep = q_idx >= kv[0][2] + k_off
        s = [jnp.where(keep, _dot_t(kv[h][0], qh[h]), NEG_BIG) for h in heads]
        return _softmax_tiles(s, [0.0] * ML_HEADS, st, [kv[h][1] for h in heads])

    st = tuple((jnp.full((1, tq), NEG_BIG, F32), jnp.zeros((vblk, tq), F32)) for _ in range(ML_HEADS))
    st = lax.fori_loop(0, n_full, full_body, st)
    st = lax.fori_loop(n_full, n_tiles, diag_body, st)
    for h in range(ML_HEADS):
        acc = st[h][1]
        o_ref[:, h * ML_V:(h + 1) * ML_V] = (acc[:ML_V] / acc[ML_V:ML_V + 1]).T.astype(o_ref.dtype)


def mla_attention(qf, kf, v, batch, *, tq=ATTN_TILE):
    tokens = qf.shape[0]
    seq = tokens // batch
    tq = min(tq, seq)
    tk = tq
    nq = seq // tq
    wide = qf.shape[1]
    return pl.pallas_call(
        functools.partial(_mla_attn_kernel, tq=tq, tk=tk),
        grid=(batch, nq),
        in_specs=[pl.BlockSpec((tq, wide), lambda b, i: (b * nq + i, 0)),
                  pl.BlockSpec((seq, wide), lambda b, i: (b, 0)),
                  pl.BlockSpec((seq, ML_DIM), lambda b, i: (b, 0))],
        out_specs=pl.BlockSpec((tq, ML_DIM), lambda b, i: (b * nq + i, 0)),
        out_shape=jax.ShapeDtypeStruct((tokens, ML_DIM), BF16),
        scratch_shapes=[pltpu.VMEM((ML_HEADS * (ML_V + ONES_ROWS), seq), BF16)],
        compiler_params=_cparams(("arbitrary", "arbitrary")),
        name="mla_attention",
    )(qf, kf, v)


def _cross_kernel(*refs, n_mix):
    post_refs, refs = refs[:7], refs[7:]
    y_refs, w_refs = refs[:n_mix - 1], refs[n_mix - 1:2 * n_mix - 1]
    x_ref, nw_ref, wq_ref, kv_ref, wo_ref, o_ref, ya_buf = refs[2 * n_mix - 1:]
    _rwkv_post_kernel(*post_refs, ya_buf)
    x = x_ref[...]
    for y_ref, w_ref in zip((ya_buf,) + tuple(y_refs), w_refs):
        x = x + _dot(y_ref[...], w_ref[...])
    q = _dot(_rms(x, nw_ref[...], NORM_EPS).astype(BF16), wq_ref[...])
    kv = kv_ref[...]
    scale = CA_HEAD_DIM ** -0.5
    heads = [slice(hh * CA_HEAD_DIM, (hh + 1) * CA_HEAD_DIM) for hh in range(CA_HEADS)]
    s = [_dot_t(q[:, sl].astype(BF16), kv[:, sl]) * scale for sl in heads]
    p = [jnp.exp(x - jnp.max(x, axis=-1, keepdims=True)) for x in s]
    p = [(x / jnp.sum(x, axis=-1, keepdims=True)).astype(BF16) for x in p]
    outs = [_dot(x, kv[:, CA_DIM + sl.start:CA_DIM + sl.stop]) for x, sl in zip(p, heads)]
    o = jnp.concatenate(outs, axis=1).astype(BF16)
    o_ref[...] = x + _dot(o, wo_ref[...])


def mix_cross_block(rwkv_raw, y_list, w_list, x, batch, norm_w, wq, kv, wo, *, tq=512):
    tokens, d = x.shape
    seq = tokens // batch
    tq = min(tq, seq)
    nq = seq // tq
    mem_len = kv.shape[0] // batch
    o, bonus, gate, ln_w, ln_b, seg, seg_t = rwkv_raw
    dr = o.shape[1]
    rows = lambda width: pl.BlockSpec((tq, width), lambda b, i: (b * nq + i, 0))
    whole = lambda a: pl.BlockSpec(a.shape, lambda b, i: (0, 0))
    vec = lambda width: pl.BlockSpec((1, width), lambda b, i: (0, 0))
    return pl.pallas_call(
        functools.partial(_cross_kernel, n_mix=len(w_list)),
        grid=(batch, nq),
        in_specs=[rows(dr), rows(dr), rows(dr), vec(dr), vec(dr), whole(seg), whole(seg_t)]
                 + [rows(y.shape[1]) for y in y_list] + [whole(w) for w in w_list]
                 + [rows(d), vec(d), whole(wq),
                    pl.BlockSpec((mem_len, 2 * CA_DIM), lambda b, i: (b, 0)), whole(wo)],
        out_specs=rows(d),
        out_shape=jax.ShapeDtypeStruct((tokens, d), F32),
        scratch_shapes=[pltpu.VMEM((tq, dr), BF16)],
        compiler_params=_cparams(("parallel", "parallel")),
        name="mix_cross_block",
    )(o, bonus, gate, ln_w.reshape(1, dr), ln_b.reshape(1, dr), seg, seg_t,
      *y_list, *w_list, x, norm_w.reshape(1, d), wq, kv, wo)


SEL_E1, SEL_E2, SEL_G1, SEL_G2 = 0, 1, 2, 3


def _route(logits, b_router):
    biased = logits + b_router
    lane = lax.broadcasted_iota(jnp.int32, logits.shape, 1)
    big = jnp.int32(LANES)

    def first_argmax(vals):
        mx = jnp.max(vals, axis=-1, keepdims=True)
        return jnp.min(jnp.where(vals == mx, lane, big), axis=-1, keepdims=True)

    def pick(vals, idx):
        return jnp.sum(jnp.where(lane == idx, vals, 0.0), axis=-1, keepdims=True)

    is_group = (lane >= MOE_EXPERTS) & (lane < MOE_EXPERTS + MOE_GROUPS)
    gl = jnp.where(is_group, logits, NEG_BIG)
    ge = jnp.exp(gl - jnp.max(gl, axis=-1, keepdims=True))
    gp = ge / jnp.sum(ge, axis=-1, keepdims=True)
    g_lane = first_argmax(jnp.where(is_group, biased, NEG_BIG))
    p_group = pick(gp, g_lane)
    lo = (g_lane - MOE_EXPERTS) * MOE_PER_GROUP
    in_group = (lane >= lo) & (lane < lo + MOE_PER_GROUP)
    eb = jnp.where(in_group, biased, NEG_BIG)
    i1 = first_argmax(eb)
    i2 = first_argmax(jnp.where(lane == i1, NEG_BIG, eb))
    l1, l2 = pick(logits, i1), pick(logits, i2)
    mx = jnp.maximum(l1, l2)
    e1, e2 = jnp.exp(l1 - mx), jnp.exp(l2 - mx)
    w1, w2 = e1 / (e1 + e2), e2 / (e1 + e2)
    return jnp.where(lane == SEL_E1, i1.astype(F32),
                     jnp.where(lane == SEL_E2, i2.astype(F32),
                               jnp.where(lane == SEL_G1, w1 * p_group,
                                         jnp.where(lane == SEL_G2, w2 * p_group, 0.0))))


def _router_kernel(x_ref, nw_ref, wr_ref, br_ref, h_ref, sel_ref):
    h = _rms(x_ref[...], nw_ref[...], NORM_EPS)
    h_ref[...] = h
    sel_ref[...] = _route(_dot_x3(h, wr_ref[...]), br_ref[...])


def moe_router(x, norm_w, w_router, b_router, *, tm=512):
    tokens, d = x.shape
    tm = min(tm, tokens)
    return pl.pallas_call(
        _router_kernel,
        grid=(tokens // tm,),
        in_specs=[pl.BlockSpec((tm, d), lambda i: (i, 0)),
                  pl.BlockSpec((1, d), lambda i: (0, 0)),
                  pl.BlockSpec((d, LANES), lambda i: (0, 0)),
                  pl.BlockSpec((1, LANES), lambda i: (0, 0))],
        out_specs=[pl.BlockSpec((tm, d), lambda i: (i, 0)),
                   pl.BlockSpec((tm, LANES), lambda i: (i, 0))],
        out_shape=[jax.ShapeDtypeStruct((tokens, d), F32),
                   jax.ShapeDtypeStruct((tokens, LANES), F32)],
        compiler_params=_cparams(("parallel",)),
        name="moe_router",
    )(x, norm_w.reshape(1, d), w_router, b_router)


def _moe_rank_kernel(sel_ref, ltri_ref, rank_ref, counts_ref, carry_ref):
    @pl.when(pl.program_id(0) == 0)
    def _():
        carry_ref[...] = jnp.zeros_like(carry_ref)

    sel = sel_ref[...]
    lane = lax.broadcasted_iota(jnp.int32, sel.shape, 1)
    lane_f = lane.astype(F32)
    oh1 = lane_f == sel[:, SEL_E1:SEL_E1 + 1]
    oh2 = lane_f == sel[:, SEL_E2:SEL_E2 + 1]
    f1, f2 = oh1.astype(F32), oh2.astype(F32)
    ltri = ltri_ref[...]
    before1 = _dot(ltri, f1.astype(BF16))
    before2 = _dot(ltri, f2.astype(BF16))
    c1 = jnp.sum(f1, axis=0, keepdims=True)
    c2 = jnp.sum(f2, axis=0, keepdims=True)
    carry = carry_ref[...]
    r1 = jnp.sum(jnp.where(oh1, before1 + carry, 0.0), axis=1, keepdims=True)
    r2 = jnp.sum(jnp.where(oh2, before2 + carry + c1, 0.0), axis=1, keepdims=True)
    rank_ref[...] = jnp.where(lane == SEL_E1, r1, jnp.where(lane == SEL_E2, r2, 0.0)).astype(jnp.int32)
    total = carry + c1 + c2
    carry_ref[...] = total
    counts_ref[...] = total.astype(jnp.int32)


def moe_rank(sel, *, tm=512):
    tokens = sel.shape[0]
    tm = min(tm, tokens)
    ltri = (jnp.arange(tm)[:, None] > jnp.arange(tm)[None, :]).astype(BF16)
    return pl.pallas_call(
        _moe_rank_kernel,
        grid=(tokens // tm,),
        in_specs=[pl.BlockSpec((tm, LANES), lambda i: (i, 0)),
                  pl.BlockSpec((tm, tm), lambda i: (0, 0))],
        out_specs=[pl.BlockSpec((tm, LANES), lambda i: (i, 0)),
                   pl.BlockSpec((1, LANES), lambda i: (0, 0))],
        out_shape=[jax.ShapeDtypeStruct((tokens, LANES), jnp.int32),
                   jax.ShapeDtypeStruct((1, LANES), jnp.int32)],
        scratch_shapes=[pltpu.VMEM((1, LANES), F32)],
        compiler_params=_cparams(("arbitrary",)),
        name="moe_rank",
    )(sel, ltri)


def _row_copy(src_ref, src_row, dst_ref, dst_row, sem):
    return pltpu.make_async_copy(src_ref.at[pl.ds(src_row, 1)], dst_ref.at[pl.ds(dst_row, 1)], sem)


def _moe_dispatch_kernel(dest_ref, tail_ref, h_ref, xs_ref, hbuf, zero_ref, lsem, ssem, zsem, *, tm):
    step = pl.program_id(0)
    n_steps = pl.num_programs(0)
    base = step * (2 * tm)
    slot = step % 3

    def load(tile, buf_slot):
        return pltpu.make_async_copy(h_ref.at[pl.ds(pl.multiple_of(tile * tm, tm), tm)], hbuf.at[buf_slot],
                                     lsem.at[buf_slot])

    def wait_scatter(buf_slot):
        for _ in range(2):
            pltpu.make_async_copy(hbuf.at[buf_slot], xs_ref.at[pl.ds(0, tm)], ssem.at[buf_slot]).wait()

    @pl.when(step == 0)
    def _():
        load(0, 0).start()

        @pl.when(n_steps > 1)
        def _():
            load(1, 1).start()

    @pl.when(step == 0)
    def _():
        zero_ref[...] = jnp.zeros_like(zero_ref)

        def fill(tail):
            return pltpu.make_async_copy(zero_ref, xs_ref.at[pl.ds(pl.multiple_of(tail, 8), zero_ref.shape[0])], zsem)

        def start(e, carry):
            @pl.when(tail_ref[e] >= 0)
            def _():
                fill(tail_ref[e]).start()
            return carry

        def wait(e, carry):
            @pl.when(tail_ref[e] >= 0)
            def _():
                fill(tail_ref[e]).wait()
            return carry

        lax.fori_loop(0, tail_ref.shape[0], start, 0)
        lax.fori_loop(0, tail_ref.shape[0], wait, 0)

    load(step, slot).wait()

    def issue(r, carry):
        for s in range(2):
            _row_copy(hbuf.at[slot], r, xs_ref, dest_ref[base + 2 * r + s], ssem.at[slot]).start()
        return carry

    lax.fori_loop(0, tm, issue, 0, unroll=DMA_ISSUE_UNROLL)

    @pl.when(step >= 1)
    def _():
        wait_scatter((step + 2) % 3)

    @pl.when(step + 2 < n_steps)
    def _():
        load(step + 2, (step + 2) % 3).start()

    @pl.when(step == n_steps - 1)
    def _():
        wait_scatter(slot)


def moe_dispatch(h, dest, tails, rows, row_tile, *, tm=512):
    tokens, d = h.shape
    tm = min(tm, tokens)
    grid_spec = pltpu.PrefetchScalarGridSpec(
        num_scalar_prefetch=2,
        grid=(tokens // tm,),
        in_specs=[pl.BlockSpec(memory_space=pl.ANY)],
        out_specs=pl.BlockSpec(memory_space=pl.ANY),
        scratch_shapes=[pltpu.VMEM((3, tm, d), h.dtype), pltpu.VMEM((row_tile, d), h.dtype),
                        pltpu.SemaphoreType.DMA((3,)), pltpu.SemaphoreType.DMA((3,)),
                        pltpu.SemaphoreType.DMA(())],
    )
    return pl.pallas_call(
        functools.partial(_moe_dispatch_kernel, tm=tm),
        grid_spec=grid_spec,
        out_shape=jax.ShapeDtypeStruct((rows, d), h.dtype),
        compiler_params=_cparams(("arbitrary",)),
        name="moe_dispatch",
    )(dest, tails, h)


def _moe_expert_kernel(te_ref, nused_ref, xs_ref, wg_ref, wu_ref, wd_ref, ys_ref, wgb, wub, wdb):
    r = pl.program_id(0)
    used = r < nused_ref[0]
    changed = (r == 0) | (te_ref[r] != te_ref[jnp.maximum(r - 1, 0)])

    @pl.when(used & changed)
    def _():
        wgb[...] = wg_ref[...].astype(BF16)
        wub[...] = wu_ref[...].astype(BF16)
        wdb[...] = wd_ref[...].astype(BF16)

    @pl.when(used)
    def _():
        x = xs_ref[...].astype(BF16)
        gate_pre = _dot(x, wgb[...])
        hid = (gate_pre * jax.nn.sigmoid(gate_pre)) * _dot(x, wub[...])
        ys_ref[...] = _dot(hid.astype(BF16), wdb[...])

    @pl.when(jnp.logical_not(used))
    def _():
        ys_ref[...] = jnp.zeros_like(ys_ref)


def moe_experts(xs, tile_expert, n_used, wg, wu, wd, layer, *, tm):
    rows, d = xs.shape
    de = wg.shape[-1]
    grid_spec = pltpu.PrefetchScalarGridSpec(
        num_scalar_prefetch=2,
        grid=(rows // tm,),
        in_specs=[pl.BlockSpec((tm, d), lambda r, te, nu: (jnp.minimum(r, nu[0] - 1), 0)),
                  pl.BlockSpec((None, None, d, de), lambda r, te, nu: (layer, te[r], 0, 0)),
                  pl.BlockSpec((None, None, d, de), lambda r, te, nu: (layer, te[r], 0, 0)),
                  pl.BlockSpec((None, None, de, d), lambda r, te, nu: (layer, te[r], 0, 0))],
        out_specs=pl.BlockSpec((tm, d), lambda r, te, nu: (r, 0)),
        scratch_shapes=[pltpu.VMEM((d, de), BF16), pltpu.VMEM((d, de), BF16), pltpu.VMEM((de, d), BF16)],
    )
    return pl.pallas_call(
        _moe_expert_kernel,
        grid_spec=grid_spec,
        out_shape=jax.ShapeDtypeStruct((rows, d), F32),
        compiler_params=_cparams(("arbitrary",)),
        name="moe_experts",
    )(tile_expert, n_used, xs, wg, wu, wd)


def _moe_combine_kernel(dest_ref, x_ref, sel_ref, nw_ref, ys_ref, o_ref, buf_ref, sem, *, tm, normalize):
    step = pl.program_id(0)
    slot = step % 2

    def gather(tile, buf_slot):
        base = tile * (2 * tm)

        def issue(r, carry):
            for s in range(2):
                _row_copy(ys_ref, dest_ref[base + 2 * r + s], buf_ref.at[buf_slot, s], r, sem.at[buf_slot]).start()
            return carry

        lax.fori_loop(0, tm, issue, 0, unroll=DMA_ISSUE_UNROLL)

    @pl.when(step == 0)
    def _():
        gather(0, 0)

    @pl.when(step + 1 < pl.num_programs(0))
    def _():
        gather(step + 1, 1 - slot)

    for s in range(2):
        pltpu.make_async_copy(ys_ref.at[pl.ds(0, tm)], buf_ref.at[slot, s], sem.at[slot]).wait()
    sel = sel_ref[...]
    out = x_ref[...] + sel[:, SEL_G1:SEL_G1 + 1] * buf_ref[slot, 0] + sel[:, SEL_G2:SEL_G2 + 1] * buf_ref[slot, 1]
    o_ref[...] = _rms(out, nw_ref[...], NORM_EPS) if normalize else out


def moe_combine(x, sel, ys, dest, final_norm=None, *, tm=512):
    tokens, d = x.shape
    tm = min(tm, tokens)
    normalize = final_norm is not None
    nw = (final_norm if normalize else jnp.ones((d,), F32)).reshape(1, d)
    grid_spec = pltpu.PrefetchScalarGridSpec(
        num_scalar_prefetch=1,
        grid=(tokens // tm,),
        in_specs=[pl.BlockSpec((tm, d), lambda i, *_: (i, 0)),
                  pl.BlockSpec((tm, LANES), lambda i, *_: (i, 0)),
                  pl.BlockSpec((1, d), lambda i, *_: (0, 0)),
                  pl.BlockSpec(memory_space=pl.ANY)],
        out_specs=pl.BlockSpec((tm, d), lambda i, *_: (i, 0)),
        scratch_shapes=[pltpu.VMEM((2, 2, tm, d), F32), pltpu.SemaphoreType.DMA((2,))],
    )
    return pl.pallas_call(
        functools.partial(_moe_combine_kernel, tm=tm, normalize=normalize),
        grid_spec=grid_spec,
        out_shape=jax.ShapeDtypeStruct((tokens, d), F32),
        compiler_params=_cparams(("arbitrary",)),
        name="moe_combine",
    )(dest, x, sel, nw, ys)


def moe_block(x, norm_w, w_router, b_router, wg, wu, wd, layer, final_norm=None, *, tm=MOE_ROW_TILE):
    tokens, d = x.shape
    n_exp = wg.shape[1]
    h, sel = moe_router(x, norm_w, w_router, b_router)
    rank, counts = moe_rank(sel)
    padded = (counts[0, :n_exp] + (tm - 1)) // tm * tm
    ends = jnp.cumsum(padded)
    starts = ends - padded
    experts = sel[:, SEL_E1:SEL_E2 + 1].astype(jnp.int32)
    start_of = jnp.sum(jnp.where(experts[..., None] == jnp.arange(n_exp, dtype=jnp.int32), starts, 0), axis=-1)
    dest = (start_of + rank[:, SEL_E1:SEL_E2 + 1]).reshape(-1)
    rows = 2 * tokens + n_exp * tm
    tile_start = jnp.arange(rows // tm, dtype=jnp.int32) * tm
    tile_expert = jnp.minimum(jnp.sum(tile_start[:, None] >= ends[None, :], axis=1), n_exp - 1).astype(jnp.int32)
    n_used = (ends[-1] // tm).astype(jnp.int32).reshape(1)
    tails = jnp.concatenate([jnp.where(padded > 0, ends - tm, -1),
                             jnp.where(tile_start >= ends[-1], tile_start, -1)]).astype(jnp.int32)
    xs = moe_dispatch(h, dest, tails, rows, tm)
    ys = moe_experts(xs, tile_expert, n_used, wg, wu, wd, layer, tm=tm)
    return moe_combine(x, sel, ys, dest, final_norm)


def _proj_weights(w_in_l, w_vres_l):
    d = w_in_l.shape[0]
    mla0 = RW_COLS + DF_COLS
    vres = jnp.zeros((d, RW_V_RANK), F32) if w_vres_l is None else w_vres_l
    part_f = [w_in_l[:, 3 * RW_DIM:RW_COLS],
              vres, jnp.zeros((d, LANES - RW_V_RANK), F32),
              w_in_l[:, mla0 + ML_Q_RANK + ML_KV_RANK:mla0 + ML_COLS], jnp.zeros((d, LANES - ML_ROPE), F32)]
    part_a = [w_in_l[:, :3 * RW_DIM],
              w_in_l[:, RW_COLS:RW_COLS + DF_COLS],
              w_in_l[:, mla0 + ML_Q_RANK:mla0 + ML_Q_RANK + ML_KV_RANK],
              jnp.zeros((d, OFF_MQ - OFF_MKV - ML_KV_RANK), F32),
              w_in_l[:, mla0:mla0 + ML_Q_RANK]]
    return jnp.concatenate(part_f, axis=1).astype(BF16), jnp.concatenate(part_a, axis=1).astype(BF16)


def _pad_rows(w, rows, at=0):
    out = jnp.zeros((rows, w.shape[1]), w.dtype)
    return lax.dynamic_update_slice(out, w, (at, 0))


def kernel(x, mem, positions, rel_bias, final_norm, norm_mix, w_in, w_in_vres, w_out, tm_mu, tm_mu_vres, tm_w0, tm_w2, tm_a0, tm_a2, tm_v0, tm_v2, tm_g2, tm_k_k, tm_k_a, tm_r_k, tm_ln_w, tm_ln_b, da_lq1, da_lk1, da_lq2, da_lk2, da_subln, mla_q_norm, mla_wq_b, mla_kv_norm, mla_wkv_b, norm_cross, norm_mem, ca_wq, ca_wkv, ca_wo, norm_ffn, moe_w_group, moe_b_group, moe_w_expert, moe_b_expert, moe_w_gate, moe_w_up, moe_w_down):
    batch, seq, d = x.shape
    tokens = batch * seq
    depth = norm_mix.shape[0]
    xf = x.reshape(tokens, d)
    memf = mem.reshape(-1, d)
    positions = positions.astype(jnp.int32)

    head_of_lane = jnp.arange(RW_DIM) // RW_HEAD_DIM
    seg = (head_of_lane[:, None] == jnp.arange(LANES)[None, :]).astype(BF16)
    seg_t = seg.T
    row = lambda v: v.reshape(1, -1)

    v_first = None
    for l in range(depth):
        w_f, w_a = _proj_weights(w_in[l], None if l == 0 else w_in_vres[l - 1])
        pa, proj = norm_matmul(xf, norm_mix[l], w_a, w_f, out_dtype=BF16, tm=1024, tn=PROJ_A_COLS // 3)

        mu = tm_mu[l]
        prm = dict(mu_r=row(mu[:RW_DIM]), mu_k=row(mu[RW_DIM:2 * RW_DIM]), mu_v=row(mu[2 * RW_DIM:3 * RW_DIM]),
                   mu_l=row(mu[3 * RW_DIM:]), w0=row(tm_w0[l]), a0=row(tm_a0[l]),
                   w2=_pad_rows(tm_w2[l], LANES, 0), a2=_pad_rows(tm_a2[l], LANES, RW_W_RANK),
                   g2=tm_g2[l].astype(BF16), k_k=row(tm_k_k[l]), k_a=row(tm_k_a[l]), r_k=row(tm_r_k[l]),
                   seg=seg, seg_t=seg_t)
        if l > 0:
            prm.update(mu_vr=jnp.pad(row(tm_mu_vres[l - 1]), ((0, 0), (0, LANES - RW_V_RANK))),
                       v0=row(tm_v0[l - 1]), v2=_pad_rows(tm_v2[l - 1], LANES, 0))
        r, lw, k, v, kap, beta, gate, bonus = rwkv_prep(pa, proj, batch, v_first, prm)
        if l == 0:
            v_first = v
        o = rwkv_scan(r, lw, k, v, kap, beta, batch)
        rwkv_raw = (o, bonus, gate, tm_ln_w[l], tm_ln_b[l], seg, seg_t)

        lambda_init = 0.8 - 0.6 * math.exp(-0.3 * l)
        lam = (jnp.exp(jnp.sum(da_lq1[l] * da_lk1[l])) - jnp.exp(jnp.sum(da_lq2[l] * da_lk2[l])) + lambda_init)
        y_b = diff_attention(pa, positions, rel_bias, lam, lambda_init, da_subln[l])

        wq = mla_wq_b[l].reshape(ML_Q_RANK, ML_HEADS, ML_NOPE + ML_ROPE)
        wq_pe = jnp.pad(wq[:, :, ML_NOPE:], ((0, 0), (0, 0), (0, LANES - ML_ROPE)))
        wq_all = jnp.concatenate([wq[:, :, :ML_NOPE].reshape(ML_Q_RANK, -1),
                                  wq_pe.reshape(ML_Q_RANK, -1)], axis=1).astype(BF16)
        wkv = mla_wkv_b[l].reshape(ML_KV_RANK, ML_HEADS, ML_NOPE + ML_V)
        wkv = jnp.concatenate([wkv[:, :, :ML_NOPE].reshape(ML_KV_RANK, -1),
                               wkv[:, :, ML_NOPE:].reshape(ML_KV_RANK, -1)], axis=1).astype(BF16)
        qf, kf, v_mla = mla_prep(pa, proj, positions, mla_q_norm[l], mla_kv_norm[l], wq_all, wkv)
        y_c = mla_attention(qf, kf, v_mla, batch)

        wo = w_out[l].astype(BF16)
        kv_mem = norm_matmul(memf, norm_mem[l], ca_wkv[l].astype(BF16), out_dtype=BF16)
        xf = mix_cross_block(rwkv_raw, [y_b, y_c], [wo[:RW_DIM], wo[RW_DIM:RW_DIM + DF_DIM], wo[RW_DIM + DF_DIM:]],
                             xf, batch, norm_cross[l], ca_wq[l].astype(BF16), kv_mem, ca_wo[l].astype(BF16))

        w_router = jnp.concatenate(
            [moe_w_expert[l], moe_w_group[l], jnp.zeros((d, LANES - MOE_EXPERTS - MOE_GROUPS), F32)], axis=1)
        b_router = jnp.concatenate(
            [moe_b_expert[l], moe_b_group[l], jnp.zeros((LANES - MOE_EXPERTS - MOE_GROUPS,), F32)]).reshape(1, LANES)
        xf = moe_block(xf, norm_ffn[l], w_router, b_router, moe_w_gate, moe_w_up, moe_w_down, l,
                       final_norm if l == depth - 1 else None)

    return xf.reshape(batch, seq, d)
```

```python
import functools
import math

import jax
import jax.numpy as jnp
from jax import lax
from jax.experimental import pallas as pl
from jax.experimental.pallas import tpu as pltpu

F32 = jnp.float32
BF16 = jnp.bfloat16

NORM_EPS = 1e-6
ROPE_THETA = 10000.0

RW_HEADS = 16
RW_HEAD_DIM = 64
RW_DIM = RW_HEADS * RW_HEAD_DIM
RW_W_RANK = 64
RW_A_RANK = 64
RW_G_RANK = 128
RW_V_RANK = 32
RW_LORA = RW_W_RANK + RW_A_RANK + RW_G_RANK
RW_LN_EPS = 64e-5
RW_COLS = 3 * RW_DIM + RW_LORA

DF_HEADS = 4
DF_HEAD_DIM = 64
DF_V_DIM = 2 * DF_HEAD_DIM
DF_QK = DF_HEADS * 2 * DF_HEAD_DIM
DF_DIM = DF_HEADS * DF_V_DIM
DF_COLS = 2 * DF_QK + DF_DIM
DF_SUBLN_EPS = 1e-5

ML_HEADS = 4
ML_Q_RANK = 384
ML_KV_RANK = 256
ML_NOPE = 128
ML_ROPE = 64
ML_V = 128
ML_DIM = ML_HEADS * ML_V
ML_COLS = ML_Q_RANK + ML_KV_RANK + ML_ROPE

REL_BUCKETS = 32
REL_MAX_DIST = 128

CA_HEADS = 4
CA_HEAD_DIM = 128
CA_DIM = CA_HEADS * CA_HEAD_DIM

MOE_GROUPS = 4
MOE_PER_GROUP = 8
MOE_EXPERTS = MOE_GROUPS * MOE_PER_GROUP

LANES = 128
SCAN_CHUNK = 64
SCAN_GROUP = 4
SCAN_BATCHES = 4
ATTN_TILE = 512
ONES_ROWS = 16
POST_ROW_BLOCKS = 4
DMA_ISSUE_UNROLL = 8
MOE_ROW_TILE = 256
VMEM_LIMIT = 56 * 1024 * 1024
NEG_BIG = -1e30

LOG2E = 1.4426950408889634

OFF_LORA = 0
OFF_VRES = OFF_LORA + RW_LORA
OFF_KPE = OFF_VRES + LANES
PROJ_F_COLS = OFF_KPE + LANES
OFF_R = 0
OFF_K = RW_DIM
OFF_V = 2 * RW_DIM
OFF_DQ = 3 * RW_DIM
OFF_DK = OFF_DQ + DF_QK
OFF_DV = OFF_DK + DF_QK
OFF_MKV = OFF_DV + DF_DIM
OFF_MQ = 13 * ML_Q_RANK
PROJ_A_COLS = OFF_MQ + ML_Q_RANK


def _cparams(sem, vmem=VMEM_LIMIT, flags=None):
    return pltpu.CompilerParams(dimension_semantics=sem, vmem_limit_bytes=vmem, flags=flags)


def _dot(a, b):
    return jnp.dot(a, b, preferred_element_type=F32)


def _dot_t(a, b):
    return lax.dot_general(a, b, (((1,), (1,)), ((), ())), preferred_element_type=F32)


def _split3(x):
    hi = x.astype(BF16)
    r1 = x - hi.astype(F32)
    mid = r1.astype(BF16)
    lo = (r1 - mid.astype(F32)).astype(BF16)
    return hi, mid, lo


def _dot_rhs01(x, ones_bf16):
    hi = x.astype(BF16)
    lo = (x - hi.astype(F32)).astype(BF16)
    return _dot(hi, ones_bf16) + _dot(lo, ones_bf16)


def _dot_x3(a, b):
    ah = a.astype(BF16)
    al = (a - ah.astype(F32)).astype(BF16)
    bh = b.astype(BF16)
    bl = (b - bh.astype(F32)).astype(BF16)
    return _dot(ah, bh) + _dot(ah, bl) + _dot(al, bh)


def _rms(x, w, eps):
    ms = jnp.mean(x * x, axis=-1, keepdims=True)
    return x * lax.rsqrt(ms + eps) * w


def _norm_matmul_kernel(*refs, eps, has_side):
    if has_side:
        x_ref, nw_ref, w_ref, ws_ref, o_ref, os_ref, xn_ref = refs
    else:
        x_ref, nw_ref, w_ref, o_ref, xn_ref = refs

    @pl.when(pl.program_id(1) == 0)
    def _():
        xn_ref[...] = _rms(x_ref[...], nw_ref[...], eps).astype(BF16)
        if has_side:
            os_ref[...] = _dot(xn_ref[...], ws_ref[...])

    o_ref[...] = _dot(xn_ref[...], w_ref[...]).astype(o_ref.dtype)


def norm_matmul(x, nw, w, w_side=None, *, out_dtype=F32, tm=512, tn=None, eps=NORM_EPS):
    m, d = x.shape
    n = w.shape[1]
    tm = min(tm, m)
    tn = n if tn is None else tn
    has_side = w_side is not None
    in_specs = [pl.BlockSpec((tm, d), lambda i, j: (i, 0)),
                pl.BlockSpec((1, d), lambda i, j: (0, 0)),
                pl.BlockSpec((d, tn), lambda i, j: (0, j))]
    out_specs = [pl.BlockSpec((tm, tn), lambda i, j: (i, j))]
    out_shape = [jax.ShapeDtypeStruct((m, n), out_dtype)]
    args = [x, nw.reshape(1, d), w]
    if has_side:
        ns = w_side.shape[1]
        in_specs.append(pl.BlockSpec((d, ns), lambda i, j: (0, 0)))
        out_specs.append(pl.BlockSpec((tm, ns), lambda i, j: (i, 0)))
        out_shape.append(jax.ShapeDtypeStruct((m, ns), F32))
        args.append(w_side)
    outs = pl.pallas_call(
        functools.partial(_norm_matmul_kernel, eps=eps, has_side=has_side),
        grid=(m // tm, n // tn),
        in_specs=in_specs,
        out_specs=out_specs,
        out_shape=out_shape,
        scratch_shapes=[pltpu.VMEM((tm, d), BF16)],
        compiler_params=_cparams(("parallel", "arbitrary")),
        name="norm_matmul",
    )(*args)
    return outs if has_side else outs[0]


def _softplus(z):
    return jnp.maximum(z, 0.0) + jnp.log(1.0 + jnp.exp(-jnp.abs(z)))


def _rwkv_prep_kernel(*refs, has_vres):
    if has_vres:
        (pr_ref, pk_ref, pv_ref, pl_ref, pvr_ref, vfirst_ref,
         mu_r, mu_k, mu_v, mu_l, mu_vr, w0, w2, a0, a2, g2, v0, v2,
         k_k, k_a, r_k, seg, seg_t,
         r_o, lw_o, k_o, v_o, kap_o, beta_o, g_o, bonus_o,
         last_r, last_k, last_v, last_l, last_vr) = refs
    else:
        (pr_ref, pk_ref, pv_ref, pl_ref,
         mu_r, mu_k, mu_v, mu_l, w0, w2, a0, a2, g2,
         k_k, k_a, r_k, seg, seg_t,
         r_o, lw_o, k_o, v_o, kap_o, beta_o, g_o, bonus_o,
         last_r, last_k, last_v, last_l) = refs
    t = pl.program_id(1)

    def shifted(p_ref, last_ref, mu_ref):
        p = p_ref[...].astype(F32)
        n = p.shape[0]
        carried = jnp.where(t == 0, 0.0, last_ref[0:1, :])
        row = lax.broadcasted_iota(jnp.int32, p.shape, 0)
        prev = jnp.where(row == 0, carried, pltpu.roll(p, 1, axis=0))
        last_ref[0:1, :] = p[n - 1:n, :]
        return p + mu_ref[...] * (prev - p)

    r = shifted(pr_ref, last_r, mu_r)
    k = shifted(pk_ref, last_k, mu_k)
    v = shifted(pv_ref, last_v, mu_v)
    lora = shifted(pl_ref, last_l, mu_l)
    wl = lora[:, :LANES]
    gl = lora[:, LANES:]

    lane = lax.broadcasted_iota(jnp.int32, wl.shape, 1)
    wl_t = jnp.where(lane < RW_W_RANK, jnp.tanh(wl), 0.0)
    al = jnp.where(lane >= RW_W_RANK, wl, 0.0)
    w_log = -_softplus(-(w0[...] + _dot_x3(wl_t, w2[...]))) - 0.5
    lw_o[...] = -jnp.exp(w_log)
    a = jax.nn.sigmoid(a0[...] + _dot_x3(al, a2[...]))
    g_o[...] = _dot(jax.nn.sigmoid(gl).astype(BF16), g2[...]).astype(g_o.dtype)

    segm, segm_t = seg[...], seg_t[...]

    def head_sum(x):
        return _dot_rhs01(_dot_rhs01(x, segm), segm_t)

    kk = k * k_k[...]
    kk = kk * lax.rsqrt(jnp.maximum(head_sum(kk * kk), 1e-24))
    k = k * (1.0 + (a - 1.0) * k_a[...])
    if has_vres:
        vr = shifted(pvr_ref, last_vr, mu_vr)
        mix = jax.nn.sigmoid(v0[...] + _dot_x3(vr, v2[...]))
        v = v + (vfirst_ref[...] - v) * mix
    r_o[...] = r.astype(r_o.dtype)
    k_o[...] = k.astype(k_o.dtype)
    v_o[...] = v.astype(v_o.dtype)
    kap_o[...] = kk.astype(kap_o.dtype)
    beta_o[...] = (kk * a).astype(beta_o.dtype)
    bonus_o[...] = (head_sum(r * k * r_k[...]) * v).astype(bonus_o.dtype)


def rwkv_prep(pa, proj, batch, vfirst, prm, *, tt=512):
    tokens = proj.shape[0]
    seq = tokens // batch
    tt = min(tt, seq)
    nt = seq // tt
    has_vres = vfirst is not None
    d = RW_DIM

    def rows(width, col):
        return pl.BlockSpec((tt, width), lambda b, t, col=col: (b * nt + t, col))

    def full(shape):
        return pl.BlockSpec(shape, lambda b, t: (0, 0))

    in_specs = [rows(d, OFF_R // d), rows(d, OFF_K // d), rows(d, OFF_V // d),
                rows(RW_LORA, OFF_LORA // RW_LORA)]
    args = [pa, pa, pa, proj]
    if has_vres:
        in_specs += [rows(LANES, OFF_VRES // LANES), rows(d, 0)]
        args += [proj, vfirst]
    names = ["mu_r", "mu_k", "mu_v", "mu_l"] + (["mu_vr"] if has_vres else []) + ["w0", "w2", "a0", "a2", "g2"]
    names += (["v0", "v2"] if has_vres else []) + ["k_k", "k_a", "r_k", "seg", "seg_t"]
    for nm in names:
        in_specs.append(full(prm[nm].shape))
        args.append(prm[nm])
    out_spec = pl.BlockSpec((tt, d), lambda b, t: (b * nt + t, 0))
    scratch = [pltpu.VMEM((8, d), F32)] * 3 + [pltpu.VMEM((8, RW_LORA), F32)]
    if has_vres:
        scratch.append(pltpu.VMEM((8, LANES), F32))
    return pl.pallas_call(
        functools.partial(_rwkv_prep_kernel, has_vres=has_vres),
        grid=(batch, nt),
        in_specs=in_specs,
        out_specs=[out_spec] * 8,
        out_shape=[jax.ShapeDtypeStruct((tokens, d), F32 if i == 1 else BF16) for i in range(8)],
        scratch_shapes=scratch,
        compiler_params=_cparams(("arbitrary", "arbitrary")),
        name="rwkv_prep",
    )(*args)


def _rwkv_scan_kernel(r_ref, lw_ref, k_ref, v_ref, kap_ref, beta_ref, tril_ref, bmask_ref,
                      o_ref, ht_ref):
    @pl.when(pl.program_id(1) == 0)
    def _():
        ht_ref[...] = jnp.zeros_like(ht_ref)

    n_batch, c, d = lw_ref.shape
    w = ht_ref.shape[1]
    g = w // RW_HEAD_DIM
    bmask = bmask_ref[...]
    bmask_b = bmask.astype(BF16)
    tril3 = tril_ref[...]
    t_idx = lax.broadcasted_iota(jnp.int32, (c, w), 0)
    s_idx = lax.broadcasted_iota(jnp.int32, (c, w), 1) % c
    strict = t_idx > s_idx
    incl = t_idx >= s_idx
    n_sq = int(math.log2(c))

    def stack(x):
        return jnp.concatenate([x.astype(BF16)] * g, axis=0) * bmask_b

    sls = [(bi, slice(None), slice(lo, lo + w)) for bi in range(n_batch) for lo in range(0, d, w)]
    groups = range(len(sls))
    lw = [lw_ref[sl] for sl in sls]
    cum = [_dot(tril3, jnp.concatenate(_split3(x), axis=0)) for x in lw]
    total = [x[c - 1:c, :] for x in cum]
    ar = [jnp.concatenate([-kap_ref[sls[gi]] * jnp.exp(cum[gi] - lw[gi]), r_ref[sls[gi]] * jnp.exp(cum[gi])],
                          axis=0).astype(BF16) for gi in groups]
    p_inv = [jnp.exp(-x) for x in cum]
    b_s = [stack(beta_ref[sls[gi]] * p_inv[gi]) for gi in groups]
    k_s = [stack(k_ref[sls[gi]] * p_inv[gi]) for gi in groups]
    v_n = [v_ref[sl] for sl in sls]
    v_s = [stack(x) for x in v_n]

    arb = [_dot_t(ar[gi], b_s[gi]) for gi in groups]
    ark = [_dot_t(ar[gi], k_s[gi]) for gi in groups]
    ab = [jnp.where(strict, m[:c], 0.0) for m in arb]
    rb = [jnp.where(incl, m[c:], 0.0).astype(BF16) for m in arb]
    akrk = [jnp.concatenate([jnp.where(strict, m[:c], 0.0), jnp.where(incl, m[c:], 0.0)], axis=0).astype(BF16)
            for m in ark]

    ht = [ht_ref[gi] for gi in groups]
    base = [_dot_t(ar[gi], ht[gi].astype(BF16)) + _dot(akrk[gi], v_s[gi]) for gi in groups]
    x = [m[:c] for m in base]
    lp = ab
    for i in range(n_sq):
        lpb = [m.astype(BF16) for m in lp]
        x = [x[gi] + _dot(lpb[gi], stack(x[gi])) for gi in groups]
        if i < n_sq - 1:
            lp = [_dot(lpb[gi], stack(lp[gi])) for gi in groups]
    for gi in groups:
        o_ref[sls[gi]] = base[gi][c:] + _dot(rb[gi], stack(x[gi]))

    for gi in groups:
        p_rem = jnp.exp(total[gi] - cum[gi])
        z = jnp.concatenate([beta_ref[sls[gi]] * p_rem, k_ref[sls[gi]] * p_rem], axis=0).astype(BF16)
        uv_t = jnp.concatenate([x[gi], v_n[gi].astype(F32)], axis=0).T.astype(BF16)
        ht_ref[gi] = ht[gi] * jnp.exp(total[gi]) + bmask * _dot(uv_t, z)


def rwkv_scan(r, lw, k, v, kap, beta, batch):
    tokens, d = r.shape
    seq = tokens // batch
    c = min(SCAN_CHUNK, seq)
    nc = seq // c
    gw = SCAN_GROUP * RW_HEAD_DIM
    rr = SCAN_GROUP * c
    assert c == RW_HEAD_DIM, "the stacking mask doubles as the head-block mask of the state"
    bb = math.gcd(batch, SCAN_BATCHES)
    tril = jnp.tile((jnp.arange(c)[:, None] >= jnp.arange(c)[None, :]).astype(BF16), (1, 3))
    bmask = (jnp.arange(rr)[:, None] // c == jnp.arange(gw)[None, :] // RW_HEAD_DIM).astype(F32)
    blk = pl.BlockSpec((bb, c, d), lambda b, i: (b, i, 0))
    as3d = lambda a: a.reshape(batch, seq, d)
    out = pl.pallas_call(
        _rwkv_scan_kernel,
        grid=(batch // bb, nc),
        in_specs=[blk] * 6 + [pl.BlockSpec((c, 3 * c), lambda b, i: (0, 0)),
                              pl.BlockSpec((rr, gw), lambda b, i: (0, 0))],
        out_specs=blk,
        out_shape=jax.ShapeDtypeStruct((batch, seq, d), F32),
        scratch_shapes=[pltpu.VMEM((bb * (d // gw), gw, gw), F32)],
        compiler_params=_cparams(("arbitrary", "arbitrary")),
        name="rwkv_scan",
    )(as3d(r), as3d(lw), as3d(k), as3d(v), as3d(kap), as3d(beta), tril, bmask)
    return out.reshape(tokens, d)


def _rwkv_post_kernel(o_ref, bonus_ref, g_ref, lnw_ref, lnb_ref, seg, seg_t, y_ref):
    segm, segm_t = seg[...], seg_t[...]

    def head_means(xs):
        sums = [_dot_rhs01(x, segm) for x in xs]
        return [_dot_rhs01(s, segm_t) * (1.0 / RW_HEAD_DIM) for s in sums]

    rows = o_ref.shape[0] // POST_ROW_BLOCKS
    blocks = [slice(i * rows, (i + 1) * rows) for i in range(POST_ROW_BLOCKS)]
    o = [o_ref[b, :] for b in blocks]
    dlt = [x - m for x, m in zip(o, head_means(o))]
    var = head_means([x * x for x in dlt])
    for b, x, v in zip(blocks, dlt, var):
        y = x * lax.rsqrt(v + RW_LN_EPS) * lnw_ref[...] + lnb_ref[...]
        y_ref[b, :] = ((y + bonus_ref[b, :]) * g_ref[b, :]).astype(y_ref.dtype)


def _t5_thresholds():
    max_exact = REL_BUCKETS // 2
    thr = list(range(1, max_exact))
    n = max_exact
    for bucket in range(max_exact, REL_BUCKETS):
        while True:
            large = max_exact + int(math.log(max(n, max_exact) / max_exact)
                                    / math.log(REL_MAX_DIST / max_exact) * (REL_BUCKETS - max_exact))
            if min(large, REL_BUCKETS - 1) >= bucket:
                break
            n += 1
        thr.append(n)
    return thr


T5_THRESHOLDS = _t5_thresholds()
T5_FAR = T5_THRESHOLDS[-1]


def _softmax_tiles(s_list, c_list, states, vt_list):
    stats = []
    for s_t, c, (m_old, _) in zip(s_list, c_list, states):
        m_new = jnp.maximum(m_old, jnp.max(s_t, axis=0, keepdims=True) + c)
        stats.append((m_new, jnp.exp2(m_old - m_new), jnp.exp2(s_t - (m_new - c)).astype(BF16)))
    return tuple((m_new, alpha * acc + _dot(vt, p_t))
                 for (m_new, alpha, p_t), (_, acc), vt in zip(stats, states, vt_list))


def _transpose_into(vt_ref, v_ref, chunk, heads, width):
    seq = v_ref.shape[0]
    blk = width + ONES_ROWS
    for c in range(seq // chunk):
        cols = slice(c * chunk, (c + 1) * chunk)
        vt = v_ref[cols, :].astype(F32).T.astype(BF16)
        for h in range(heads):
            vt_ref[h * blk:h * blk + width, cols] = vt[h * width:(h + 1) * width]
            vt_ref[h * blk + width:(h + 1) * blk, cols] = jnp.ones((ONES_ROWS, chunk), BF16)


def _diff_attn_kernel(qfirst_ref, klast_ref, q_ref, k_ref, v_ref, qpos_ref, kpos_ref, subln_ref, table_ref, lam_ref,
                      o_ref, vt_ref, *, tq, tk, scale2, out_scale):
    b, i = pl.program_id(0), pl.program_id(1)
    nq = pl.num_programs(1)
    seq = k_ref.shape[0]
    nk = seq // tk
    w = DF_V_DIM
    vblk = w + ONES_ROWS

    @pl.when(i == 0)
    def _():
        _transpose_into(vt_ref, v_ref, tk, DF_HEADS, w)

    n_tiles = (i * tq + tq - 1) // tk + 1
    qf = qfirst_ref[b * nq + i]
    n_far = lax.while_loop(
        lambda j: (j * tk + tk - 1 <= i * tq) & (qf - klast_ref[b * nk + jnp.minimum(j, nk - 1)] >= T5_FAR),
        lambda j: j + 1, jnp.int32(0))

    dist = lax.broadcasted_iota(jnp.int32, (1, LANES), 1)
    qpos = qpos_ref[...]
    q_idx = i * tq + lax.broadcasted_iota(jnp.int32, (tk, tq), 1)
    k_off = lax.broadcasted_iota(jnp.int32, (tk, tq), 0)
    lane = lax.broadcasted_iota(jnp.int32, (tq, w), 1)

    bias_rows, c_far, qm = [], [], []
    for h in range(DF_HEADS):
        bias_vec = jnp.full((1, LANES), table_ref[h], F32)
        for bucket, thr in enumerate(T5_THRESHOLDS, start=1):
            bias_vec = jnp.where(dist >= thr, table_ref[bucket * DF_HEADS + h], bias_vec)
        bias_rows.append(jnp.broadcast_to(bias_vec * LOG2E, (tk, LANES)))
        c_far.append(table_ref[(REL_BUCKETS - 1) * DF_HEADS + h] * LOG2E)
        qh = q_ref[:, h * w:(h + 1) * w].astype(F32) * scale2
        qm.append([jnp.where((lane >= mi * DF_HEAD_DIM) & (lane < (mi + 1) * DF_HEAD_DIM), qh, 0.0).astype(BF16)
                   for mi in range(2)])

    def tiles(j, h):
        off = pl.multiple_of(j * tk, tk)
        return k_ref[pl.ds(off, tk), h * w:(h + 1) * w], vt_ref[h * vblk:(h + 1) * vblk, pl.ds(off, tk)], off

    chains = [(h, mi) for h in range(DF_HEADS) for mi in range(2)]

    def far_body(j, st):
        kv = [tiles(j, h) for h in range(DF_HEADS)]
        s = [_dot_t(kv[h][0], qm[h][mi]) for h, mi in chains]
        return _softmax_tiles(s, [c_far[h] for h, _ in chains], st, [kv[h][1] for h, _ in chains])

    def near_body(j, st):
        off = pl.multiple_of(j * tk, tk)
        n = jnp.clip(qpos - kpos_ref[pl.ds(off, tk), :], 0, LANES - 1)
        keep = q_idx >= off + k_off
        kv = [tiles(j, h) for h in range(DF_HEADS)]
        bias = [jnp.concatenate(
            [jnp.take_along_axis(bias_rows[h], n[:, cb * LANES:(cb + 1) * LANES], axis=1)
             for cb in range(tq // LANES)], axis=1) for h in range(DF_HEADS)]
        s = [jnp.where(keep, _dot_t(kv[h][0], qm[h][mi]) + bias[h], NEG_BIG) for h, mi in chains]
        return _softmax_tiles(s, [0.0] * len(chains), st, [kv[h][1] for h, _ in chains])

    init = tuple((jnp.full((1, tq), NEG_BIG, F32), jnp.zeros((w + ONES_ROWS, tq), F32))
                 for _ in range(2 * DF_HEADS))
    st = lax.fori_loop(0, n_far, far_body, init)
    st = lax.fori_loop(n_far, n_tiles, near_body, st)
    for h in range(DF_HEADS):
        a0, a1 = st[2 * h][1], st[2 * h + 1][1]
        d_t = a0[:w] / a0[w:w + 1] - lam_ref[0] * (a1[:w] / a1[w:w + 1])
        ms = jnp.mean(d_t * d_t, axis=0, keepdims=True)
        y_t = d_t * lax.rsqrt(ms + DF_SUBLN_EPS) * (subln_ref[...] * out_scale)
        o_ref[:, h * w:(h + 1) * w] = y_t.T.astype(o_ref.dtype)


def diff_attention(pa, positions, rel_bias, lam, lambda_init, subln_w, *, tq=ATTN_TILE):
    batch, seq = positions.shape
    tokens = batch * seq
    tq = min(tq, seq)
    tk = tq
    nq, nk = seq // tq, seq // tk
    qfirst = positions[:, ::tq].reshape(-1)
    klast = positions[:, tk - 1::tk].reshape(-1)
    qpos = positions.reshape(batch, 1, seq)
    kpos = positions.reshape(batch, seq, 1)
    wd = DF_DIM
    grid_spec = pltpu.PrefetchScalarGridSpec(
        num_scalar_prefetch=2,
        grid=(batch, nq),
        in_specs=[pl.BlockSpec((tq, wd), lambda b, i, *_: (b * nq + i, OFF_DQ // wd)),
                  pl.BlockSpec((seq, wd), lambda b, i, *_: (b, OFF_DK // wd)),
                  pl.BlockSpec((seq, wd), lambda b, i, *_: (b, OFF_DV // wd)),
                  pl.BlockSpec((None, 1, tq), lambda b, i, *_: (b, 0, i)),
                  pl.BlockSpec((None, seq, 1), lambda b, i, *_: (b, 0, 0)),
                  pl.BlockSpec((DF_V_DIM, 1), lambda b, i, *_: (0, 0)),
                  pl.BlockSpec(memory_space=pltpu.SMEM),
                  pl.BlockSpec(memory_space=pltpu.SMEM)],
        out_specs=pl.BlockSpec((tq, wd), lambda b, i, *_: (b * nq + i, 0)),
        scratch_shapes=[pltpu.VMEM((DF_HEADS * (DF_V_DIM + ONES_ROWS), seq), BF16)],
    )
    return pl.pallas_call(
        functools.partial(_diff_attn_kernel, tq=tq, tk=tk, scale2=DF_HEAD_DIM ** -0.5 * LOG2E,
                          out_scale=1.0 - lambda_init),
        grid_spec=grid_spec,
        out_shape=jax.ShapeDtypeStruct((tokens, DF_DIM), BF16),
        compiler_params=_cparams(("arbitrary", "arbitrary")),
        name="diff_attention",
    )(qfirst, klast, pa, pa, pa, qpos, kpos, subln_w.reshape(DF_V_DIM, 1), rel_bias.reshape(-1), lam.reshape(1))


ML_QK_PAD = 2 * LANES


def _rope_block(x, cos, sin):
    half = ML_ROPE // 2
    lane = lax.broadcasted_iota(jnp.int32, x.shape, 1)
    rot = jnp.where(lane < half, -pltpu.roll(x, LANES - half, axis=1),
                    jnp.where(lane < ML_ROPE, pltpu.roll(x, half, axis=1), 0.0))
    return x * cos + rot * sin


def _mla_prep_kernel(mq_ref, mkv_ref, kpe_ref, pos_ref, qn_w, kvn_w, wq_ref, wkv_ref, freq_ref,
                     qf_o, kf_o, v_o, *, qscale):
    ang = pos_ref[...].astype(F32) * freq_ref[...]
    cos, sin = jnp.cos(ang), jnp.sin(ang)
    qc = _rms(mq_ref[...].astype(F32), qn_w[...], NORM_EPS).astype(BF16)
    q_all = _dot(qc, wq_ref[...]) * qscale
    kvc = _rms(mkv_ref[...].astype(F32), kvn_w[...], NORM_EPS).astype(BF16)
    kvb = _dot(kvc, wkv_ref[...])
    kpe = _rope_block(kpe_ref[...], cos, sin).astype(BF16)
    nope_w = ML_HEADS * ML_NOPE
    for h in range(ML_HEADS):
        lo = h * ML_QK_PAD
        qf_o[:, lo:lo + LANES] = q_all[:, h * LANES:(h + 1) * LANES].astype(BF16)
        qf_o[:, lo + LANES:lo + 2 * LANES] = _rope_block(
            q_all[:, nope_w + h * LANES:nope_w + (h + 1) * LANES], cos, sin).astype(BF16)
        kf_o[:, lo:lo + LANES] = kvb[:, h * LANES:(h + 1) * LANES].astype(BF16)
        kf_o[:, lo + LANES:lo + 2 * LANES] = kpe
    v_o[...] = kvb[:, nope_w:].astype(BF16)


def mla_prep(pa, pf, positions, q_norm, kv_norm, wq_all, wkv, *, tm=512):
    tokens = pa.shape[0]
    tm = min(tm, tokens)
    half = ML_ROPE // 2
    inv_freq = ROPE_THETA ** (-jnp.arange(half, dtype=F32) / half)
    freq = jnp.concatenate([inv_freq, inv_freq, jnp.zeros((LANES - ML_ROPE,), F32)]).reshape(1, LANES)

    def full(a):
        return pl.BlockSpec(a.shape, lambda i: (0, 0))

    qn_w = q_norm.reshape(1, -1)
    kvn_w = kv_norm.reshape(1, -1)
    wide = ML_HEADS * ML_QK_PAD
    return pl.pallas_call(
        functools.partial(_mla_prep_kernel, qscale=(ML_NOPE + ML_ROPE) ** -0.5 * LOG2E),
        grid=(tokens // tm,),
        in_specs=[pl.BlockSpec((tm, ML_Q_RANK), lambda i: (i, OFF_MQ // ML_Q_RANK)),
                  pl.BlockSpec((tm, ML_KV_RANK), lambda i: (i, OFF_MKV // ML_KV_RANK)),
                  pl.BlockSpec((tm, LANES), lambda i: (i, OFF_KPE // LANES)),
                  pl.BlockSpec((tm, 1), lambda i: (i, 0)),
                  full(qn_w), full(kvn_w), full(wq_all), full(wkv), full(freq)],
        out_specs=[pl.BlockSpec((tm, wide), lambda i: (i, 0)),
                   pl.BlockSpec((tm, wide), lambda i: (i, 0)),
                   pl.BlockSpec((tm, ML_DIM), lambda i: (i, 0))],
        out_shape=[jax.ShapeDtypeStruct((tokens, wide), BF16),
                   jax.ShapeDtypeStruct((tokens, wide), BF16),
                   jax.ShapeDtypeStruct((tokens, ML_DIM), BF16)],
        compiler_params=_cparams(("parallel",)),
        name="mla_prep",
    )(pa, pa, pf, positions.reshape(tokens, 1), qn_w, kvn_w, wq_all, wkv, freq)


def _mla_attn_kernel(q_ref, k_ref, v_ref, o_ref, vt_ref, *, tq, tk):
    i = pl.program_id(1)
    wq = ML_QK_PAD
    vblk = ML_V + ONES_ROWS

    @pl.when(i == 0)
    def _():
        _transpose_into(vt_ref, v_ref, tk, ML_HEADS, ML_V)

    n_tiles = (i * tq + tq - 1) // tk + 1
    n_full = (i * tq + 1) // tk
    q_idx = i * tq + lax.broadcasted_iota(jnp.int32, (tk, tq), 1)
    k_off = lax.broadcasted_iota(jnp.int32, (tk, tq), 0)
    qh = [q_ref[:, h * wq:(h + 1) * wq] for h in range(ML_HEADS)]

    def tiles(j, h):
        off = pl.multiple_of(j * tk, tk)
        return (k_ref[pl.ds(off, tk), h * wq:(h + 1) * wq],
                vt_ref[h * vblk:(h + 1) * vblk, pl.ds(off, tk)], off)

    heads = range(ML_HEADS)

    def full_body(j, st):
        kv = [tiles(j, h) for h in heads]
        s = [_dot_t(kv[h][0], qh[h]) for h in heads]
        return _softmax_tiles(s, [0.0] * ML_HEADS, st, [kv[h][1] for h in heads])

    def diag_body(j, st):
        kv = [tiles(j, h) for h in heads]
        keep = q_idx >= kv[0][2] + k_off
        s = [jnp.where(keep, _dot_t(kv[h][0], qh[h]), NEG_BIG) for h in heads]
        return _softmax_tiles(s, [0.0] * ML_HEADS, st, [kv[h][1] for h in heads])

    st = tuple((jnp.full((1, tq), NEG_BIG, F32), jnp.zeros((vblk, tq), F32)) for _ in range(ML_HEADS))
    st = lax.fori_loop(0, n_full, full_body, st)
    st = lax.fori_loop(n_full, n_tiles, diag_body, st)
    for h in range(ML_HEADS):
        acc = st[h][1]
        o_ref[:, h * ML_V:(h + 1) * ML_V] = (acc[:ML_V] / acc[ML_V:ML_V + 1]).T.astype(o_ref.dtype)


def mla_attention(qf, kf, v, batch, *, tq=ATTN_TILE):
    tokens = qf.shape[0]
    seq = tokens // batch
    tq = min(tq, seq)
    tk = tq
    nq = seq // tq
    wide = qf.shape[1]
    return pl.pallas_call(
        functools.partial(_mla_attn_kernel, tq=tq, tk=tk),
        grid=(batch, nq),
        in_specs=[pl.BlockSpec((tq, wide), lambda b, i: (b * nq + i, 0)),
                  pl.BlockSpec((seq, wide), lambda b, i: (b, 0)),
                  pl.BlockSpec((seq, ML_DIM), lambda b, i: (b, 0))],
        out_specs=pl.BlockSpec((tq, ML_DIM), lambda b, i: (b * nq + i, 0)),
        out_shape=jax.ShapeDtypeStruct((tokens, ML_DIM), BF16),
        scratch_shapes=[pltpu.VMEM((ML_HEADS * (ML_V + ONES_ROWS), seq), BF16)],
        compiler_params=_cparams(("arbitrary", "arbitrary")),
        name="mla_attention",
    )(qf, kf, v)


def _cross_kernel(*refs, n_mix):
    post_refs, refs = refs[:7], refs[7:]
    y_refs, w_refs = refs[:n_mix - 1], refs[n_mix - 1:2 * n_mix - 1]
    x_ref, nw_ref, wq_ref, kv_ref, wo_ref, o_ref, ya_buf = refs[2 * n_mix - 1:]
    _rwkv_post_kernel(*post_refs, ya_buf)
    x = x_ref[...]
    for y_ref, w_ref in zip((ya_buf,) + tuple(y_refs), w_refs):
        x = x + _dot(y_ref[...], w_ref[...])
    q = _dot(_rms(x, nw_ref[...], NORM_EPS).astype(BF16), wq_ref[...])
    kv = kv_ref[...]
    scale = CA_HEAD_DIM ** -0.5
    heads = [slice(hh * CA_HEAD_DIM, (hh + 1) * CA_HEAD_DIM) for hh in range(CA_HEADS)]
    s = [_dot_t(q[:, sl].astype(BF16), kv[:, sl]) * scale for sl in heads]
    p = [jnp.exp(x - jnp.max(x, axis=-1, keepdims=True)) for x in s]
    p = [(x / jnp.sum(x, axis=-1, keepdims=True)).astype(BF16) for x in p]
    outs = [_dot(x, kv[:, CA_DIM + sl.start:CA_DIM + sl.stop]) for x, sl in zip(p, heads)]
    o = jnp.concatenate(outs, axis=1).astype(BF16)
    o_ref[...] = x + _dot(o, wo_ref[...])


def mix_cross_block(rwkv_raw, y_list, w_list, x, batch, norm_w, wq, kv, wo, *, tq=512):
    tokens, d = x.shape
    seq = tokens // batch
    tq = min(tq, seq)
    nq = seq // tq
    mem_len = kv.shape[0] // batch
    o, bonus, gate, ln_w, ln_b, seg, seg_t = rwkv_raw
    dr = o.shape[1]
    rows = lambda width: pl.BlockSpec((tq, width), lambda b, i: (b * nq + i, 0))
    whole = lambda a: pl.BlockSpec(a.shape, lambda b, i: (0, 0))
    vec = lambda width: pl.BlockSpec((1, width), lambda b, i: (0, 0))
    return pl.pallas_call(
        functools.partial(_cross_kernel, n_mix=len(w_list)),
        grid=(batch, nq),
        in_specs=[rows(dr), rows(dr), rows(dr), vec(dr), vec(dr), whole(seg), whole(seg_t)]
                 + [rows(y.shape[1]) for y in y_list] + [whole(w) for w in w_list]
                 + [rows(d), vec(d), whole(wq),
                    pl.BlockSpec((mem_len, 2 * CA_DIM), lambda b, i: (b, 0)), whole(wo)],
        out_specs=rows(d),
        out_shape=jax.ShapeDtypeStruct((tokens, d), F32),
        scratch_shapes=[pltpu.VMEM((tq, dr), BF16)],
        compiler_params=_cparams(("parallel", "parallel")),
        name="mix_cross_block",
    )(o, bonus, gate, ln_w.reshape(1, dr), ln_b.reshape(1, dr), seg, seg_t,
      *y_list, *w_list, x, norm_w.reshape(1, d), wq, kv, wo)


SEL_E1, SEL_E2, SEL_G1, SEL_G2 = 0, 1, 2, 3


def _route(logits, b_router):
    biased = logits + b_router
    lane = lax.broadcasted_iota(jnp.int32, logits.shape, 1)
    big = jnp.int32(LANES)

    def first_argmax(vals):
        mx = jnp.max(vals, axis=-1, keepdims=True)
        return jnp.min(jnp.where(vals == mx, lane, big), axis=-1, keepdims=True)

    def pick(vals, idx):
        return jnp.sum(jnp.where(lane == idx, vals, 0.0), axis=-1, keepdims=True)

    is_group = (lane >= MOE_EXPERTS) & (lane < MOE_EXPERTS + MOE_GROUPS)
    gl = jnp.where(is_group, logits, NEG_BIG)
    ge = jnp.exp(gl - jnp.max(gl, axis=-1, keepdims=True))
    gp = ge / jnp.sum(ge, axis=-1, keepdims=True)
    g_lane = first_argmax(jnp.where(is_group, biased, NEG_BIG))
    p_group = pick(gp, g_lane)
    lo = (g_lane - MOE_EXPERTS) * MOE_PER_GROUP
    in_group = (lane >= lo) & (lane < lo + MOE_PER_GROUP)
    eb = jnp.where(in_group, biased, NEG_BIG)
    i1 = first_argmax(eb)
    i2 = first_argmax(jnp.where(lane == i1, NEG_BIG, eb))
    l1, l2 = pick(logits, i1), pick(logits, i2)
    mx = jnp.maximum(l1, l2)
    e1, e2 = jnp.exp(l1 - mx), jnp.exp(l2 - mx)
    w1, w2 = e1 / (e1 + e2), e2 / (e1 + e2)
    return jnp.where(lane == SEL_E1, i1.astype(F32),
                     jnp.where(lane == SEL_E2, i2.astype(F32),
                               jnp.where(lane == SEL_G1, w1 * p_group,
                                         jnp.where(lane == SEL_G2, w2 * p_group, 0.0))))


def _router_kernel(x_ref, nw_ref, wr_ref, br_ref, h_ref, sel_ref):
    h = _rms(x_ref[...], nw_ref[...], NORM_EPS)
    h_ref[...] = h
    sel_ref[...] = _route(_dot_x3(h, wr_ref[...]), br_ref[...])


def moe_router(x, norm_w, w_router, b_router, *, tm=512):
    tokens, d = x.shape
    tm = min(tm, tokens)
    return pl.pallas_call(
        _router_kernel,
        grid=(tokens // tm,),
        in_specs=[pl.BlockSpec((tm, d), lambda i: (i, 0)),
                  pl.BlockSpec((1, d), lambda i: (0, 0)),
                  pl.BlockSpec((d, LANES), lambda i: (0, 0)),
                  pl.BlockSpec((1, LANES), lambda i: (0, 0))],
        out_specs=[pl.BlockSpec((tm, d), lambda i: (i, 0)),
                   pl.BlockSpec((tm, LANES), lambda i: (i, 0))],
        out_shape=[jax.ShapeDtypeStruct((tokens, d), F32),
                   jax.ShapeDtypeStruct((tokens, LANES), F32)],
        compiler_params=_cparams(("parallel",)),
        name="moe_router",
    )(x, norm_w.reshape(1, d), w_router, b_router)


def _moe_rank_kernel(sel_ref, ltri_ref, rank_ref, counts_ref, carry_ref):
    @pl.when(pl.program_id(0) == 0)
    def _():
        carry_ref[...] = jnp.zeros_like(carry_ref)

    sel = sel_ref[...]
    lane = lax.broadcasted_iota(jnp.int32, sel.shape, 1)
    lane_f = lane.astype(F32)
    oh1 = lane_f == sel[:, SEL_E1:SEL_E1 + 1]
    oh2 = lane_f == sel[:, SEL_E2:SEL_E2 + 1]
    f1, f2 = oh1.astype(F32), oh2.astype(F32)
    ltri = ltri_ref[...]
    before1 = _dot(ltri, f1.astype(BF16))
    before2 = _dot(ltri, f2.astype(BF16))
    c1 = jnp.sum(f1, axis=0, keepdims=True)
    c2 = jnp.sum(f2, axis=0, keepdims=True)
    carry = carry_ref[...]
    r1 = jnp.sum(jnp.where(oh1, before1 + carry, 0.0), axis=1, keepdims=True)
    r2 = jnp.sum(jnp.where(oh2, before2 + carry + c1, 0.0), axis=1, keepdims=True)
    rank_ref[...] = jnp.where(lane == SEL_E1, r1, jnp.where(lane == SEL_E2, r2, 0.0)).astype(jnp.int32)
    total = carry + c1 + c2
    carry_ref[...] = total
    counts_ref[...] = total.astype(jnp.int32)


def moe_rank(sel, *, tm=512):
    tokens = sel.shape[0]
    tm = min(tm, tokens)
    ltri = (jnp.arange(tm)[:, None] > jnp.arange(tm)[None, :]).astype(BF16)
    return pl.pallas_call(
        _moe_rank_kernel,
        grid=(tokens // tm,),
        in_specs=[pl.BlockSpec((tm, LANES), lambda i: (i, 0)),
                  pl.BlockSpec((tm, tm), lambda i: (0, 0))],
        out_specs=[pl.BlockSpec((tm, LANES), lambda i: (i, 0)),
                   pl.BlockSpec((1, LANES), lambda i: (0, 0))],
        out_shape=[jax.ShapeDtypeStruct((tokens, LANES), jnp.int32),
                   jax.ShapeDtypeStruct((1, LANES), jnp.int32)],
        scratch_shapes=[pltpu.VMEM((1, LANES), F32)],
        compiler_params=_cparams(("arbitrary",)),
        name="moe_rank",
    )(sel, ltri)


def _row_copy(src_ref, src_row, dst_ref, dst_row, sem):
    return pltpu.make_async_copy(src_ref.at[pl.ds(src_row, 1)], dst_ref.at[pl.ds(dst_row, 1)], sem)


def _moe_dispatch_kernel(dest_ref, tail_ref, h_ref, xs_ref, hbuf, zero_ref, lsem, ssem, zsem, *, tm):
    step = pl.program_id(0)
    n_steps = pl.num_programs(0)
    base = step * (2 * tm)
    slot = step % 3

    def load(tile, buf_slot):
        return pltpu.make_async_copy(h_ref.at[pl.ds(pl.multiple_of(tile * tm, tm), tm)], hbuf.at[buf_slot],
                                     lsem.at[buf_slot])

    def wait_scatter(buf_slot):
        for _ in range(2):
            pltpu.make_async_copy(hbuf.at[buf_slot], xs_ref.at[pl.ds(0, tm)], ssem.at[buf_slot]).wait()

    @pl.when(step == 0)
    def _():
        load(0, 0).start()

        @pl.when(n_steps > 1)
        def _():
            load(1, 1).start()

    @pl.when(step == 0)
    def _():
        zero_ref[...] = jnp.zeros_like(zero_ref)

        def fill(tail):
            return pltpu.make_async_copy(zero_ref, xs_ref.at[pl.ds(pl.multiple_of(tail, 8), zero_ref.shape[0])], zsem)

        def start(e, carry):
            @pl.when(tail_ref[e] >= 0)
            def _():
                fill(tail_ref[e]).start()
            return carry

        def wait(e, carry):
            @pl.when(tail_ref[e] >= 0)
            def _():
                fill(tail_ref[e]).wait()
            return carry

        lax.fori_loop(0, tail_ref.shape[0], start, 0)
        lax.fori_loop(0, tail_ref.shape[0], wait, 0)

    load(step, slot).wait()

    def issue(r, carry):
        for s in range(2):
            _row_copy(hbuf.at[slot], r, xs_ref, dest_ref[base + 2 * r + s], ssem.at[slot]).start()
        return carry

    lax.fori_loop(0, tm, issue, 0, unroll=DMA_ISSUE_UNROLL)

    @pl.when(step >= 1)
    def _():
        wait_scatter((step + 2) % 3)

    @pl.when(step + 2 < n_steps)
    def _():
        load(step + 2, (step + 2) % 3).start()

    @pl.when(step == n_steps - 1)
    def _():
        wait_scatter(slot)


def moe_dispatch(h, dest, tails, rows, row_tile, *, tm=256):
    tokens, d = h.shape
    tm = min(tm, tokens)
    grid_spec = pltpu.PrefetchScalarGridSpec(
        num_scalar_prefetch=2,
        grid=(tokens // tm,),
        in_specs=[pl.BlockSpec(memory_space=pl.ANY)],
        out_specs=pl.BlockSpec(memory_space=pl.ANY),
        scratch_shapes=[pltpu.VMEM((3, tm, d), h.dtype), pltpu.VMEM((row_tile, d), h.dtype),
                        pltpu.SemaphoreType.DMA((3,)), pltpu.SemaphoreType.DMA((3,)),
                        pltpu.SemaphoreType.DMA(())],
    )
    return pl.pallas_call(
        functools.partial(_moe_dispatch_kernel, tm=tm),
        grid_spec=grid_spec,
        out_shape=jax.ShapeDtypeStruct((rows, d), h.dtype),
        compiler_params=_cparams(("arbitrary",)),
        name="moe_dispatch",
    )(dest, tails, h)


def _moe_expert_kernel(te_ref, nused_ref, xs_ref, wg_ref, wu_ref, wd_ref, ys_ref, wgb, wub, wdb):
    r = pl.program_id(0)
    used = r < nused_ref[0]
    changed = (r == 0) | (te_ref[r] != te_ref[jnp.maximum(r - 1, 0)])

    @pl.when(used & changed)
    def _():
        wgb[...] = wg_ref[...].astype(BF16)
        wub[...] = wu_ref[...].astype(BF16)
        wdb[...] = wd_ref[...].astype(BF16)

    @pl.when(used)
    def _():
        x = xs_ref[...].astype(BF16)
        gate_pre = _dot(x, wgb[...])
        hid = (gate_pre * jax.nn.sigmoid(gate_pre)) * _dot(x, wub[...])
        ys_ref[...] = _dot(hid.astype(BF16), wdb[...])

    @pl.when(jnp.logical_not(used))
    def _():
        ys_ref[...] = jnp.zeros_like(ys_ref)


def moe_experts(xs, tile_expert, n_used, wg, wu, wd, layer, *, tm):
    rows, d = xs.shape
    de = wg.shape[-1]
    grid_spec = pltpu.PrefetchScalarGridSpec(
        num_scalar_prefetch=2,
        grid=(rows // tm,),
        in_specs=[pl.BlockSpec((tm, d), lambda r, te, nu: (jnp.minimum(r, nu[0] - 1), 0)),
                  pl.BlockSpec((None, None, d, de), lambda r, te, nu: (layer, te[r], 0, 0)),
                  pl.BlockSpec((None, None, d, de), lambda r, te, nu: (layer, te[r], 0, 0)),
                  pl.BlockSpec((None, None, de, d), lambda r, te, nu: (layer, te[r], 0, 0))],
        out_specs=pl.BlockSpec((tm, d), lambda r, te, nu: (r, 0)),
        scratch_shapes=[pltpu.VMEM((d, de), BF16), pltpu.VMEM((d, de), BF16), pltpu.VMEM((de, d), BF16)],
    )
    return pl.pallas_call(
        _moe_expert_kernel,
        grid_spec=grid_spec,
        out_shape=jax.ShapeDtypeStruct((rows, d), F32),
        compiler_params=_cparams(("arbitrary",)),
        name="moe_experts",
    )(tile_expert, n_used, xs, wg, wu, wd)


def _moe_combine_kernel(dest_ref, x_ref, sel_ref, nw_ref, ys_ref, o_ref, buf_ref, sem, *, tm, normalize):
    step = pl.program_id(0)
    slot = step % 2

    def gather(tile, buf_slot):
        base = tile * (2 * tm)

        def issue(r, carry):
            for s in range(2):
                _row_copy(ys_ref, dest_ref[base + 2 * r + s], buf_ref.at[buf_slot, s], r, sem.at[buf_slot]).start()
            return carry

        lax.fori_loop(0, tm, issue, 0, unroll=DMA_ISSUE_UNROLL)

    @pl.when(step == 0)
    def _():
        gather(0, 0)

    @pl.when(step + 1 < pl.num_programs(0))
    def _():
        gather(step + 1, 1 - slot)

    for s in range(2):
        pltpu.make_async_copy(ys_ref.at[pl.ds(0, tm)], buf_ref.at[slot, s], sem.at[slot]).wait()
    sel = sel_ref[...]
    out = x_ref[...] + sel[:, SEL_G1:SEL_G1 + 1] * buf_ref[slot, 0] + sel[:, SEL_G2:SEL_G2 + 1] * buf_ref[slot, 1]
    o_ref[...] = _rms(out, nw_ref[...], NORM_EPS) if normalize else out


def moe_combine(x, sel, ys, dest, final_norm=None, *, tm=256):
    tokens, d = x.shape
    tm = min(tm, tokens)
    normalize = final_norm is not None
    nw = (final_norm if normalize else jnp.ones((d,), F32)).reshape(1, d)
    grid_spec = pltpu.PrefetchScalarGridSpec(
        num_scalar_prefetch=1,
        grid=(tokens // tm,),
        in_specs=[pl.BlockSpec((tm, d), lambda i, *_: (i, 0)),
                  pl.BlockSpec((tm, LANES), lambda i, *_: (i, 0)),
                  pl.BlockSpec((1, d), lambda i, *_: (0, 0)),
                  pl.BlockSpec(memory_space=pl.ANY)],
        out_specs=pl.BlockSpec((tm, d), lambda i, *_: (i, 0)),
        scratch_shapes=[pltpu.VMEM((2, 2, tm, d), F32), pltpu.SemaphoreType.DMA((2,))],
    )
    return pl.pallas_call(
        functools.partial(_moe_combine_kernel, tm=tm, normalize=normalize),
        grid_spec=grid_spec,
        out_shape=jax.ShapeDtypeStruct((tokens, d), F32),
        compiler_params=_cparams(("arbitrary",)),
        name="moe_combine",
    )(dest, x, sel, nw, ys)


def moe_block(x, norm_w, w_router, b_router, wg, wu, wd, layer, final_norm=None, *, tm=MOE_ROW_TILE):
    tokens, d = x.shape
    n_exp = wg.shape[1]
    h, sel = moe_router(x, norm_w, w_router, b_router)
    rank, counts = moe_rank(sel)
    padded = (counts[0, :n_exp] + (tm - 1)) // tm * tm
    ends = jnp.cumsum(padded)
    starts = ends - padded
    experts = sel[:, SEL_E1:SEL_E2 + 1].astype(jnp.int32)
    start_of = jnp.sum(jnp.where(experts[..., None] == jnp.arange(n_exp, dtype=jnp.int32), starts, 0), axis=-1)
    dest = (start_of + rank[:, SEL_E1:SEL_E2 + 1]).reshape(-1)
    rows = 2 * tokens + n_exp * tm
    tile_start = jnp.arange(rows // tm, dtype=jnp.int32) * tm
    tile_expert = jnp.minimum(jnp.sum(tile_start[:, None] >= ends[None, :], axis=1), n_exp - 1).astype(jnp.int32)
    n_used = (ends[-1] // tm).astype(jnp.int32).reshape(1)
    tails = jnp.concatenate([jnp.where(padded > 0, ends - tm, -1),
                             jnp.where(tile_start >= ends[-1], tile_start, -1)]).astype(jnp.int32)
    xs = moe_dispatch(h, dest, tails, rows, tm)
    ys = moe_experts(xs, tile_expert, n_used, wg, wu, wd, layer, tm=tm)
    return moe_combine(x, sel, ys, dest, final_norm)


def _proj_weights(w_in_l, w_vres_l):
    d = w_in_l.shape[0]
    mla0 = RW_COLS + DF_COLS
    vres = jnp.zeros((d, RW_V_RANK), F32) if w_vres_l is None else w_vres_l
    part_f = [w_in_l[:, 3 * RW_DIM:RW_COLS],
              vres, jnp.zeros((d, LANES - RW_V_RANK), F32),
              w_in_l[:, mla0 + ML_Q_RANK + ML_KV_RANK:mla0 + ML_COLS], jnp.zeros((d, LANES - ML_ROPE), F32)]
    part_a = [w_in_l[:, :3 * RW_DIM],
              w_in_l[:, RW_COLS:RW_COLS + DF_COLS],
              w_in_l[:, mla0 + ML_Q_RANK:mla0 + ML_Q_RANK + ML_KV_RANK],
              jnp.zeros((d, OFF_MQ - OFF_MKV - ML_KV_RANK), F32),
              w_in_l[:, mla0:mla0 + ML_Q_RANK]]
    return jnp.concatenate(part_f, axis=1).astype(BF16), jnp.concatenate(part_a, axis=1).astype(BF16)


def _pad_rows(w, rows, at=0):
    out = jnp.zeros((rows, w.shape[1]), w.dtype)
    return lax.dynamic_update_slice(out, w, (at, 0))


def kernel(x, mem, positions, rel_bias, final_norm, norm_mix, w_in, w_in_vres, w_out, tm_mu, tm_mu_vres, tm_w0, tm_w2, tm_a0, tm_a2, tm_v0, tm_v2, tm_g2, tm_k_k, tm_k_a, tm_r_k, tm_ln_w, tm_ln_b, da_lq1, da_lk1, da_lq2, da_lk2, da_subln, mla_q_norm, mla_wq_b, mla_kv_norm, mla_wkv_b, norm_cross, norm_mem, ca_wq, ca_wkv, ca_wo, norm_ffn, moe_w_group, moe_b_group, moe_w_expert, moe_b_expert, moe_w_gate, moe_w_up, moe_w_down):
    batch, seq, d = x.shape
    tokens = batch * seq
    depth = norm_mix.shape[0]
    xf = x.reshape(tokens, d)
    memf = mem.reshape(-1, d)
    positions = positions.astype(jnp.int32)

    head_of_lane = jnp.arange(RW_DIM) // RW_HEAD_DIM
    seg = (head_of_lane[:, None] == jnp.arange(LANES)[None, :]).astype(BF16)
    seg_t = seg.T
    row = lambda v: v.reshape(1, -1)

    v_first = None
    for l in range(depth):
        w_f, w_a = _proj_weights(w_in[l], None if l == 0 else w_in_vres[l - 1])
        pa, proj = norm_matmul(xf, norm_mix[l], w_a, w_f, out_dtype=BF16, tm=1024, tn=PROJ_A_COLS // 3)

        mu = tm_mu[l]
        prm = dict(mu_r=row(mu[:RW_DIM]), mu_k=row(mu[RW_DIM:2 * RW_DIM]), mu_v=row(mu[2 * RW_DIM:3 * RW_DIM]),
                   mu_l=row(mu[3 * RW_DIM:]), w0=row(tm_w0[l]), a0=row(tm_a0[l]),
                   w2=_pad_rows(tm_w2[l], LANES, 0), a2=_pad_rows(tm_a2[l], LANES, RW_W_RANK),
                   g2=tm_g2[l].astype(BF16), k_k=row(tm_k_k[l]), k_a=row(tm_k_a[l]), r_k=row(tm_r_k[l]),
                   seg=seg, seg_t=seg_t)
        if l > 0:
            prm.update(mu_vr=jnp.pad(row(tm_mu_vres[l - 1]), ((0, 0), (0, LANES - RW_V_RANK))),
                       v0=row(tm_v0[l - 1]), v2=_pad_rows(tm_v2[l - 1], LANES, 0))
        r, lw, k, v, kap, beta, gate, bonus = rwkv_prep(pa, proj, batch, v_first, prm)
        if l == 0:
            v_first = v
        o = rwkv_scan(r, lw, k, v, kap, beta, batch)
        rwkv_raw = (o, bonus, gate, tm_ln_w[l], tm_ln_b[l], seg, seg_t)

        lambda_init = 0.8 - 0.6 * math.exp(-0.3 * l)
        lam = (jnp.exp(jnp.sum(da_lq1[l] * da_lk1[l])) - jnp.exp(jnp.sum(da_lq2[l] * da_lk2[l])) + lambda_init)
        y_b = diff_attention(pa, positions, rel_bias, lam, lambda_init, da_subln[l])

        wq = mla_wq_b[l].reshape(ML_Q_RANK, ML_HEADS, ML_NOPE + ML_ROPE)
        wq_pe = jnp.pad(wq[:, :, ML_NOPE:], ((0, 0), (0, 0), (0, LANES - ML_ROPE)))
        wq_all = jnp.concatenate([wq[:, :, :ML_NOPE].reshape(ML_Q_RANK, -1),
                                  wq_pe.reshape(ML_Q_RANK, -1)], axis=1).astype(BF16)
        wkv = mla_wkv_b[l].reshape(ML_KV_RANK, ML_HEADS, ML_NOPE + ML_V)
        wkv = jnp.concatenate([wkv[:, :, :ML_NOPE].reshape(ML_KV_RANK, -1),
                               wkv[:, :, ML_NOPE:].reshape(ML_KV_RANK, -1)], axis=1).astype(BF16)
        qf, kf, v_mla = mla_prep(pa, proj, positions, mla_q_norm[l], mla_kv_norm[l], wq_all, wkv)
        y_c = mla_attention(qf, kf, v_mla, batch)

        wo = w_out[l].astype(BF16)
        kv_mem = norm_matmul(memf, norm_mem[l], ca_wkv[l].astype(BF16), out_dtype=BF16)
        xf = mix_cross_block(rwkv_raw, [y_b, y_c], [wo[:RW_DIM], wo[RW_DIM:RW_DIM + DF_DIM], wo[RW_DIM + DF_DIM:]],
                             xf, batch, norm_cross[l], ca_wq[l].astype(BF16), kv_mem, ca_wo[l].astype(BF16))

        w_router = jnp.concatenate(
            [moe_w_expert[l], moe_w_group[l], jnp.zeros((d, LANES - MOE_EXPERTS - MOE_GROUPS), F32)], axis=1)
        b_router = jnp.concatenate(
            [moe_b_expert[l], moe_b_group[l], jnp.zeros((LANES - MOE_EXPERTS - MOE_GROUPS,), F32)]).reshape(1, LANES)
        xf = moe_block(xf, norm_ffn[l], w_router, b_router, moe_w_gate, moe_w_up, moe_w_down, l,
                       final_norm if l == depth - 1 else None)

    return xf.reshape(batch, seq, d)
```

```python
import functools
import math

import jax
import jax.numpy as jnp
from jax import lax
from jax.experimental import pallas as pl
from jax.experimental.pallas import tpu as pltpu

F32 = jnp.float32
BF16 = jnp.bfloat16

NORM_EPS = 1e-6
ROPE_THETA = 10000.0

RW_HEADS = 16
RW_HEAD_DIM = 64
RW_DIM = RW_HEADS * RW_HEAD_DIM
RW_W_RANK = 64
RW_A_RANK = 64
RW_G_RANK = 128
RW_V_RANK = 32
RW_LORA = RW_W_RANK + RW_A_RANK + RW_G_RANK
RW_LN_EPS = 64e-5
RW_COLS = 3 * RW_DIM + RW_LORA

DF_HEADS = 4
DF_HEAD_DIM = 64
DF_V_DIM = 2 * DF_HEAD_DIM
DF_QK = DF_HEADS * 2 * DF_HEAD_DIM
DF_DIM = DF_HEADS * DF_V_DIM
DF_COLS = 2 * DF_QK + DF_DIM
DF_SUBLN_EPS = 1e-5

ML_HEADS = 4
ML_Q_RANK = 384
ML_KV_RANK = 256
ML_NOPE = 128
ML_ROPE = 64
ML_V = 128
ML_DIM = ML_HEADS * ML_V
ML_COLS = ML_Q_RANK + ML_KV_RANK + ML_ROPE

REL_BUCKETS = 32
REL_MAX_DIST = 128

CA_HEADS = 4
CA_HEAD_DIM = 128
CA_DIM = CA_HEADS * CA_HEAD_DIM

MOE_GROUPS = 4
MOE_PER_GROUP = 8
MOE_EXPERTS = MOE_GROUPS * MOE_PER_GROUP

LANES = 128
SCAN_CHUNK = 64
SCAN_GROUP = 4
SCAN_BATCHES = 4
ATTN_TILE = 512
ONES_ROWS = 16
POST_ROW_BLOCKS = 4
DMA_ISSUE_UNROLL = 8
MOE_ROW_TILE = 256
VMEM_LIMIT = 56 * 1024 * 1024
NEG_BIG = -1e30

LOG2E = 1.4426950408889634

OFF_LORA = 0
OFF_VRES = OFF_LORA + RW_LORA
OFF_KPE = OFF_VRES + LANES
PROJ_F_COLS = OFF_KPE + LANES
OFF_R = 0
OFF_K = RW_DIM
OFF_V = 2 * RW_DIM
OFF_DQ = 3 * RW_DIM
OFF_DK = OFF_DQ + DF_QK
OFF_DV = OFF_DK + DF_QK
OFF_MKV = OFF_DV + DF_DIM
OFF_MQ = 13 * ML_Q_RANK
PROJ_A_COLS = OFF_MQ + ML_Q_RANK


def _cparams(sem, vmem=VMEM_LIMIT, flags=None):
    return pltpu.CompilerParams(dimension_semantics=sem, vmem_limit_bytes=vmem, flags=flags)


def _dot(a, b):
    return jnp.dot(a, b, preferred_element_type=F32)


def _dot_t(a, b):
    return lax.dot_general(a, b, (((1,), (1,)), ((), ())), preferred_element_type=F32)


def _split3(x):
    hi = x.astype(BF16)
    r1 = x - hi.astype(F32)
    mid = r1.astype(BF16)
    lo = (r1 - mid.astype(F32)).astype(BF16)
    return hi, mid, lo


def _dot_rhs01(x, ones_bf16):
    hi = x.astype(BF16)
    lo = (x - hi.astype(F32)).astype(BF16)
    return _dot(hi, ones_bf16) + _dot(lo, ones_bf16)


def _dot_x3(a, b):
    ah = a.astype(BF16)
    al = (a - ah.astype(F32)).astype(BF16)
    bh = b.astype(BF16)
    bl = (b - bh.astype(F32)).astype(BF16)
    return _dot(ah, bh) + _dot(ah, bl) + _dot(al, bh)


def _rms(x, w, eps):
    ms = jnp.mean(x * x, axis=-1, keepdims=True)
    return x * lax.rsqrt(ms + eps) * w


def _norm_matmul_kernel(*refs, eps, has_side):
    if has_side:
        x_ref, nw_ref, w_ref, ws_ref, o_ref, os_ref, xn_ref = refs
    else:
        x_ref, nw_ref, w_ref, o_ref, xn_ref = refs

    @pl.when(pl.program_id(1) == 0)
    def _():
        xn_ref[...] = _rms(x_ref[...], nw_ref[...], eps).astype(BF16)
        if has_side:
            os_ref[...] = _dot(xn_ref[...], ws_ref[...])

    o_ref[...] = _dot(xn_ref[...], w_ref[...]).astype(o_ref.dtype)


def norm_matmul(x, nw, w, w_side=None, *, out_dtype=F32, tm=512, tn=None, eps=NORM_EPS):
    m, d = x.shape
    n = w.shape[1]
    tm = min(tm, m)
    tn = n if tn is None else tn
    has_side = w_side is not None
    in_specs = [pl.BlockSpec((tm, d), lambda i, j: (i, 0)),
                pl.BlockSpec((1, d), lambda i, j: (0, 0)),
                pl.BlockSpec((d, tn), lambda i, j: (0, j))]
    out_specs = [pl.BlockSpec((tm, tn), lambda i, j: (i, j))]
    out_shape = [jax.ShapeDtypeStruct((m, n), out_dtype)]
    args = [x, nw.reshape(1, d), w]
    if has_side:
        ns = w_side.shape[1]
        in_specs.append(pl.BlockSpec((d, ns), lambda i, j: (0, 0)))
        out_specs.append(pl.BlockSpec((tm, ns), lambda i, j: (i, 0)))
        out_shape.append(jax.ShapeDtypeStruct((m, ns), F32))
        args.append(w_side)
    outs = pl.pallas_call(
        functools.partial(_norm_matmul_kernel, eps=eps, has_side=has_side),
        grid=(m // tm, n // tn),
        in_specs=in_specs,
        out_specs=out_specs,
        out_shape=out_shape,
        scratch_shapes=[pltpu.VMEM((tm, d), BF16)],
        compiler_params=_cparams(("parallel", "arbitrary")),
        name="norm_matmul",
    )(*args)
    return outs if has_side else outs[0]


def _softplus(z):
    return jnp.maximum(z, 0.0) + jnp.log(1.0 + jnp.exp(-jnp.abs(z)))


def _rwkv_prep_kernel(*refs, has_vres):
    if has_vres:
        (pr_ref, pk_ref, pv_ref, pl_ref, pvr_ref, vfirst_ref,
         mu_r, mu_k, mu_v, mu_l, mu_vr, w0, w2, a0, a2, g2, v0, v2,
         k_k, k_a, r_k, seg, seg_t,
         r_o, lw_o, k_o, v_o, kap_o, beta_o, g_o, bonus_o,
         last_r, last_k, last_v, last_l, last_vr) = refs
    else:
        (pr_ref, pk_ref, pv_ref, pl_ref,
         mu_r, mu_k, mu_v, mu_l, w0, w2, a0, a2, g2,
         k_k, k_a, r_k, seg, seg_t,
         r_o, lw_o, k_o, v_o, kap_o, beta_o, g_o, bonus_o,
         last_r, last_k, last_v, last_l) = refs
    t = pl.program_id(1)

    def shifted(p_ref, last_ref, mu_ref):
        p = p_ref[...].astype(F32)
        n = p.shape[0]
        carried = jnp.where(t == 0, 0.0, last_ref[0:1, :])
        row = lax.broadcasted_iota(jnp.int32, p.shape, 0)
        prev = jnp.where(row == 0, carried, pltpu.roll(p, 1, axis=0))
        last_ref[0:1, :] = p[n - 1:n, :]
        return p + mu_ref[...] * (prev - p)

    r = shifted(pr_ref, last_r, mu_r)
    k = shifted(pk_ref, last_k, mu_k)
    v = shifted(pv_ref, last_v, mu_v)
    lora = shifted(pl_ref, last_l, mu_l)
    wl = lora[:, :LANES]
    gl = lora[:, LANES:]

    lane = lax.broadcasted_iota(jnp.int32, wl.shape, 1)
    wl_t = jnp.where(lane < RW_W_RANK, jnp.tanh(wl), 0.0)
    al = jnp.where(lane >= RW_W_RANK, wl, 0.0)
    w_log = -_softplus(-(w0[...] + _dot_x3(wl_t, w2[...]))) - 0.5
    lw_o[...] = -jnp.exp(w_log)
    a = jax.nn.sigmoid(a0[...] + _dot_x3(al, a2[...]))
    g_o[...] = _dot(jax.nn.sigmoid(gl).astype(BF16), g2[...]).astype(g_o.dtype)

    segm, segm_t = seg[...], seg_t[...]

    def head_sum(x):
        return _dot_rhs01(_dot_rhs01(x, segm), segm_t)

    kk = k * k_k[...]
    kk = kk * lax.rsqrt(jnp.maximum(head_sum(kk * kk), 1e-24))
    k = k * (1.0 + (a - 1.0) * k_a[...])
    if has_vres:
        vr = shifted(pvr_ref, last_vr, mu_vr)
        mix = jax.nn.sigmoid(v0[...] + _dot_x3(vr, v2[...]))
        v = v + (vfirst_ref[...] - v) * mix
    r_o[...] = r.astype(r_o.dtype)
    k_o[...] = k.astype(k_o.dtype)
    v_o[...] = v.astype(v_o.dtype)
    kap_o[...] = kk.astype(kap_o.dtype)
    beta_o[...] = (kk * a).astype(beta_o.dtype)
    bonus_o[...] = (head_sum(r * k * r_k[...]) * v).astype(bonus_o.dtype)


def rwkv_prep(pa, proj, batch, vfirst, prm, *, tt=512):
    tokens = proj.shape[0]
    seq = tokens // batch
    tt = min(tt, seq)
    nt = seq // tt
    has_vres = vfirst is not None
    d = RW_DIM

    def rows(width, col):
        return pl.BlockSpec((tt, width), lambda b, t, col=col: (b * nt + t, col))

    def full(shape):
        return pl.BlockSpec(shape, lambda b, t: (0, 0))

    in_specs = [rows(d, OFF_R // d), rows(d, OFF_K // d), rows(d, OFF_V // d),
                rows(RW_LORA, OFF_LORA // RW_LORA)]
    args = [pa, pa, pa, proj]
    if has_vres:
        in_specs += [rows(LANES, OFF_VRES // LANES), rows(d, 0)]
        args += [proj, vfirst]
    names = ["mu_r", "mu_k", "mu_v", "mu_l"] + (["mu_vr"] if has_vres else []) + ["w0", "w2", "a0", "a2", "g2"]
    names += (["v0", "v2"] if has_vres else []) + ["k_k", "k_a", "r_k", "seg", "seg_t"]
    for nm in names:
        in_specs.append(full(prm[nm].shape))
        args.append(prm[nm])
    out_spec = pl.BlockSpec((tt, d), lambda b, t: (b * nt + t, 0))
    scratch = [pltpu.VMEM((8, d), F32)] * 3 + [pltpu.VMEM((8, RW_LORA), F32)]
    if has_vres:
        scratch.append(pltpu.VMEM((8, LANES), F32))
    return pl.pallas_call(
        functools.partial(_rwkv_prep_kernel, has_vres=has_vres),
        grid=(batch, nt),
        in_specs=in_specs,
        out_specs=[out_spec] * 8,
        out_shape=[jax.ShapeDtypeStruct((tokens, d), F32 if i == 1 else BF16) for i in range(8)],
        scratch_shapes=scratch,
        compiler_params=_cparams(("arbitrary", "arbitrary")),
        name="rwkv_prep",
    )(*args)


def _rwkv_scan_kernel(r_ref, lw_ref, k_ref, v_ref, kap_ref, beta_ref, tril_ref, bmask_ref,
                      o_ref, ht_ref):
    @pl.when(pl.program_id(1) == 0)
    def _():
        ht_ref[...] = jnp.zeros_like(ht_ref)

    n_batch, c, d = lw_ref.shape
    w = ht_ref.shape[1]
    g = w // RW_HEAD_DIM
    bmask = bmask_ref[...]
    bmask_b = bmask.astype(BF16)
    tril3 = tril_ref[...]
    t_idx = lax.broadcasted_iota(jnp.int32, (c, w), 0)
    s_idx = lax.broadcasted_iota(jnp.int32, (c, w), 1) % c
    strict = t_idx > s_idx
    incl = t_idx >= s_idx
    n_sq = int(math.log2(c))

    def stack(x):
        return jnp.concatenate([x.astype(BF16)] * g, axis=0) * bmask_b

    sls = [(bi, slice(None), slice(lo, lo + w)) for bi in range(n_batch) for lo in range(0, d, w)]
    groups = range(len(sls))
    lw = [lw_ref[sl] for sl in sls]
    cum = [_dot(tril3, jnp.concatenate(_split3(x), axis=0)) for x in lw]
    total = [x[c - 1:c, :] for x in cum]
    ar = [jnp.concatenate([-kap_ref[sls[gi]] * jnp.exp(cum[gi] - lw[gi]), r_ref[sls[gi]] * jnp.exp(cum[gi])],
                          axis=0).astype(BF16) for gi in groups]
    p_inv = [jnp.exp(-x) for x in cum]
    b_s = [stack(beta_ref[sls[gi]] * p_inv[gi]) for gi in groups]
    k_s = [stack(k_ref[sls[gi]] * p_inv[gi]) for gi in groups]
    v_n = [v_ref[sl] for sl in sls]
    v_s = [stack(x) for x in v_n]

    arb = [_dot_t(ar[gi], b_s[gi]) for gi in groups]
    ark = [_dot_t(ar[gi], k_s[gi]) for gi in groups]
    ab = [jnp.where(strict, m[:c], 0.0) for m in arb]
    rb = [jnp.where(incl, m[c:], 0.0).astype(BF16) for m in arb]
    akrk = [jnp.concatenate([jnp.where(strict, m[:c], 0.0), jnp.where(incl, m[c:], 0.0)], axis=0).astype(BF16)
            for m in ark]

    ht = [ht_ref[gi] for gi in groups]
    base = [_dot_t(ar[gi], ht[gi].astype(BF16)) + _dot(akrk[gi], v_s[gi]) for gi in groups]
    x = [m[:c] for m in base]
    lp = ab
    for i in range(n_sq):
        lpb = [m.astype(BF16) for m in lp]
        x = [x[gi] + _dot(lpb[gi], stack(x[gi])) for gi in groups]
        if i < n_sq - 1:
            lp = [_dot(lpb[gi], stack(lp[gi])) for gi in groups]
    for gi in groups:
        o_ref[sls[gi]] = base[gi][c:] + _dot(rb[gi], stack(x[gi]))

    for gi in groups:
        p_rem = jnp.exp(total[gi] - cum[gi])
        z = jnp.concatenate([beta_ref[sls[gi]] * p_rem, k_ref[sls[gi]] * p_rem], axis=0).astype(BF16)
        uv_t = jnp.concatenate([x[gi], v_n[gi].astype(F32)], axis=0).T.astype(BF16)
        ht_ref[gi] = ht[gi] * jnp.exp(total[gi]) + bmask * _dot(uv_t, z)


def rwkv_scan(r, lw, k, v, kap, beta, batch):
    tokens, d = r.shape
    seq = tokens // batch
    c = min(SCAN_CHUNK, seq)
    nc = seq // c
    gw = SCAN_GROUP * RW_HEAD_DIM
    rr = SCAN_GROUP * c
    assert c == RW_HEAD_DIM, "the stacking mask doubles as the head-block mask of the state"
    bb = math.gcd(batch, SCAN_BATCHES)
    tril = jnp.tile((jnp.arange(c)[:, None] >= jnp.arange(c)[None, :]).astype(BF16), (1, 3))
    bmask = (jnp.arange(rr)[:, None] // c == jnp.arange(gw)[None, :] // RW_HEAD_DIM).astype(F32)
    blk = pl.BlockSpec((bb, c, d), lambda b, i: (b, i, 0))
    as3d = lambda a: a.reshape(batch, seq, d)
    out = pl.pallas_call(
        _rwkv_scan_kernel,
        grid=(batch // bb, nc),
        in_specs=[blk] * 6 + [pl.BlockSpec((c, 3 * c), lambda b, i: (0, 0)),
                              pl.BlockSpec((rr, gw), lambda b, i: (0, 0))],
        out_specs=blk,
        out_shape=jax.ShapeDtypeStruct((batch, seq, d), F32),
        scratch_shapes=[pltpu.VMEM((bb * (d // gw), gw, gw), F32)],
        compiler_params=_cparams(("arbitrary", "arbitrary")),
        name="rwkv_scan",
    )(as3d(r), as3d(lw), as3d(k), as3d(v), as3d(kap), as3d(beta), tril, bmask)
    return out.reshape(tokens, d)


def _rwkv_post_kernel(o_ref, bonus_ref, g_ref, lnw_ref, lnb_ref, seg, seg_t, y_ref):
    segm, segm_t = seg[...], seg_t[...]

    def head_means(xs):
        sums = [_dot_rhs01(x, segm) for x in xs]
        return [_dot_rhs01(s, segm_t) * (1.0 / RW_HEAD_DIM) for s in sums]

    rows = o_ref.shape[0] // POST_ROW_BLOCKS
    blocks = [slice(i * rows, (i + 1) * rows) for i in range(POST_ROW_BLOCKS)]
    o = [o_ref[b, :] for b in blocks]
    dlt = [x - m for x, m in zip(o, head_means(o))]
    var = head_means([x * x for x in dlt])
    for b, x, v in zip(blocks, dlt, var):
        y = x * lax.rsqrt(v + RW_LN_EPS) * lnw_ref[...] + lnb_ref[...]
        y_ref[b, :] = ((y + bonus_ref[b, :]) * g_ref[b, :]).astype(y_ref.dtype)


def _t5_thresholds():
    max_exact = REL_BUCKETS // 2
    thr = list(range(1, max_exact))
    n = max_exact
    for bucket in range(max_exact, REL_BUCKETS):
        while True:
            large = max_exact + int(math.log(max(n, max_exact) / max_exact)
                                    / math.log(REL_MAX_DIST / max_exact) * (REL_BUCKETS - max_exact))
            if min(large, REL_BUCKETS - 1) >= bucket:
                break
            n += 1
        thr.append(n)
    return thr


T5_THRESHOLDS = _t5_thresholds()
T5_FAR = T5_THRESHOLDS[-1]


def _softmax_tiles(s_list, c_list, states, vt_list):
    stats = []
    for s_t, c, (m_old, _) in zip(s_list, c_list, states):
        m_new = jnp.maximum(m_old, jnp.max(s_t, axis=0, keepdims=True) + c)
        stats.append((m_new, jnp.exp2(m_old - m_new), jnp.exp2(s_t - (m_new - c)).astype(BF16)))
    return tuple((m_new, alpha * acc + _dot(vt, p_t))
                 for (m_new, alpha, p_t), (_, acc), vt in zip(stats, states, vt_list))


def _transpose_into(vt_ref, v_ref, chunk, heads, width):
    seq = v_ref.shape[0]
    blk = width + ONES_ROWS
    for c in range(seq // chunk):
        cols = slice(c * chunk, (c + 1) * chunk)
        vt = v_ref[cols, :].astype(F32).T.astype(BF16)
        for h in range(heads):
            vt_ref[h * blk:h * blk + width, cols] = vt[h * width:(h + 1) * width]
            vt_ref[h * blk + width:(h + 1) * blk, cols] = jnp.ones((ONES_ROWS, chunk), BF16)


def _diff_attn_kernel(qfirst_ref, klast_ref, q_ref, k_ref, v_ref, qpos_ref, kpos_ref, subln_ref, table_ref, lam_ref,
                      o_ref, vt_ref, *, tq, tk, scale2, out_scale):
    b, i = pl.program_id(0), pl.program_id(1)
    nq = pl.num_programs(1)
    seq = k_ref.shape[0]
    nk = seq // tk
    w = DF_V_DIM
    vblk = w + ONES_ROWS

    @pl.when(i == 0)
    def _():
        _transpose_into(vt_ref, v_ref, tk, DF_HEADS, w)

    n_tiles = (i * tq + tq - 1) // tk + 1
    qf = qfirst_ref[b * nq + i]
    n_far = lax.while_loop(
        lambda j: (j * tk + tk - 1 <= i * tq) & (qf - klast_ref[b * nk + jnp.minimum(j, nk - 1)] >= T5_FAR),
        lambda j: j + 1, jnp.int32(0))

    dist = lax.broadcasted_iota(jnp.int32, (1, LANES), 1)
    qpos = qpos_ref[...]
    q_idx = i * tq + lax.broadcasted_iota(jnp.int32, (tk, tq), 1)
    k_off = lax.broadcasted_iota(jnp.int32, (tk, tq), 0)
    lane = lax.broadcasted_iota(jnp.int32, (tq, w), 1)

    bias_rows, c_far, qm = [], [], []
    for h in range(DF_HEADS):
        bias_vec = jnp.full((1, LANES), table_ref[h], F32)
        for bucket, thr in enumerate(T5_THRESHOLDS, start=1):
            bias_vec = jnp.where(dist >= thr, table_ref[bucket * DF_HEADS + h], bias_vec)
        bias_rows.append(jnp.broadcast_to(bias_vec * LOG2E, (tk, LANES)))
        c_far.append(table_ref[(REL_BUCKETS - 1) * DF_HEADS + h] * LOG2E)
        qh = q_ref[:, h * w:(h + 1) * w].astype(F32) * scale2
        qm.append([jnp.where((lane >= mi * DF_HEAD_DIM) & (lane < (mi + 1) * DF_HEAD_DIM), qh, 0.0).astype(BF16)
                   for mi in range(2)])

    def tiles(j, h):
        off = pl.multiple_of(j * tk, tk)
        return k_ref[pl.ds(off, tk), h * w:(h + 1) * w], vt_ref[h * vblk:(h + 1) * vblk, pl.ds(off, tk)], off

    chains = [(h, mi) for h in range(DF_HEADS) for mi in range(2)]

    def far_body(j, st):
        kv = [tiles(j, h) for h in range(DF_HEADS)]
        s = [_dot_t(kv[h][0], qm[h][mi]) for h, mi in chains]
        return _softmax_tiles(s, [c_far[h] for h, _ in chains], st, [kv[h][1] for h, _ in chains])

    def near_body(j, st):
        off = pl.multiple_of(j * tk, tk)
        n = jnp.clip(qpos - kpos_ref[pl.ds(off, tk), :], 0, LANES - 1)
        keep = q_idx >= off + k_off
        kv = [tiles(j, h) for h in range(DF_HEADS)]
        bias = [jnp.concatenate(
            [jnp.take_along_axis(bias_rows[h], n[:, cb * LANES:(cb + 1) * LANES], axis=1)
             for cb in range(tq // LANES)], axis=1) for h in range(DF_HEADS)]
        s = [jnp.where(keep, _dot_t(kv[h][0], qm[h][mi]) + bias[h], NEG_BIG) for h, mi in chains]
        return _softmax_tiles(s, [0.0] * len(chains), st, [kv[h][1] for h, _ in chains])

    init = tuple((jnp.full((1, tq), NEG_BIG, F32), jnp.zeros((w + ONES_ROWS, tq), F32))
                 for _ in range(2 * DF_HEADS))
    st = lax.fori_loop(0, n_far, far_body, init)
    st = lax.fori_loop(n_far, n_tiles, near_body, st)
    for h in range(DF_HEADS):
        a0, a1 = st[2 * h][1], st[2 * h + 1][1]
        d_t = a0[:w] / a0[w:w + 1] - lam_ref[0] * (a1[:w] / a1[w:w + 1])
        ms = jnp.mean(d_t * d_t, axis=0, keepdims=True)
        y_t = d_t * lax.rsqrt(ms + DF_SUBLN_EPS) * (subln_ref[...] * out_scale)
        o_ref[:, h * w:(h + 1) * w] = y_t.T.astype(o_ref.dtype)


def diff_attention(pa, positions, rel_bias, lam, lambda_init, subln_w, *, tq=ATTN_TILE):
    batch, seq = positions.shape
    tokens = batch * seq
    tq = min(tq, seq)
    tk = tq
    nq, nk = seq // tq, seq // tk
    qfirst = positions[:, ::tq].reshape(-1)
    klast = positions[:, tk - 1::tk].reshape(-1)
    qpos = positions.reshape(batch, 1, seq)
    kpos = positions.reshape(batch, seq, 1)
    wd = DF_DIM
    grid_spec = pltpu.PrefetchScalarGridSpec(
        num_scalar_prefetch=2,
        grid=(batch, nq),
        in_specs=[pl.BlockSpec((tq, wd), lambda b, i, *_: (b * nq + i, OFF_DQ // wd)),
                  pl.BlockSpec((seq, wd), lambda b, i, *_: (b, OFF_DK // wd)),
                  pl.BlockSpec((seq, wd), lambda b, i, *_: (b, OFF_DV // wd)),
                  pl.BlockSpec((None, 1, tq), lambda b, i, *_: (b, 0, i)),
                  pl.BlockSpec((None, seq, 1), lambda b, i, *_: (b, 0, 0)),
                  pl.BlockSpec((DF_V_DIM, 1), lambda b, i, *_: (0, 0)),
                  pl.BlockSpec(memory_space=pltpu.SMEM),
                  pl.BlockSpec(memory_space=pltpu.SMEM)],
        out_specs=pl.BlockSpec((tq, wd), lambda b, i, *_: (b * nq + i, 0)),
        scratch_shapes=[pltpu.VMEM((DF_HEADS * (DF_V_DIM + ONES_ROWS), seq), BF16)],
    )
    return pl.pallas_call(
        functools.partial(_diff_attn_kernel, tq=tq, tk=tk, scale2=DF_HEAD_DIM ** -0.5 * LOG2E,
                          out_scale=1.0 - lambda_init),
        grid_spec=grid_spec,
        out_shape=jax.ShapeDtypeStruct((tokens, DF_DIM), BF16),
        compiler_params=_cparams(("arbitrary", "arbitrary")),
        name="diff_attention",
    )(qfirst, klast, pa, pa, pa, qpos, kpos, subln_w.reshape(DF_V_DIM, 1), rel_bias.reshape(-1), lam.reshape(1))


ML_QK_PAD = 2 * LANES


def _rope_block(x, cos, sin):
    half = ML_ROPE // 2
    lane = lax.broadcasted_iota(jnp.int32, x.shape, 1)
    rot = jnp.where(lane < half, -pltpu.roll(x, LANES - half, axis=1),
                    jnp.where(lane < ML_ROPE, pltpu.roll(x, half, axis=1), 0.0))
    return x * cos + rot * sin


def _mla_prep_kernel(mq_ref, mkv_ref, kpe_ref, pos_ref, qn_w, kvn_w, wq_ref, wkv_ref, freq_ref,
                     qf_o, kf_o, v_o, *, qscale):
    ang = pos_ref[...].astype(F32) * freq_ref[...]
    cos, sin = jnp.cos(ang), jnp.sin(ang)
    qc = _rms(mq_ref[...].astype(F32), qn_w[...], NORM_EPS).astype(BF16)
    q_all = _dot(qc, wq_ref[...]) * qscale
    kvc = _rms(mkv_ref[...].astype(F32), kvn_w[...], NORM_EPS).astype(BF16)
    kvb = _dot(kvc, wkv_ref[...])
    kpe = _rope_block(kpe_ref[...], cos, sin).astype(BF16)
    nope_w = ML_HEADS * ML_NOPE
    for h in range(ML_HEADS):
        lo = h * ML_QK_PAD
        qf_o[:, lo:lo + LANES] = q_all[:, h * LANES:(h + 1) * LANES].astype(BF16)
        qf_o[:, lo + LANES:lo + 2 * LANES] = _rope_block(
            q_all[:, nope_w + h * LANES:nope_w + (h + 1) * LANES], cos, sin).astype(BF16)
        kf_o[:, lo:lo + LANES] = kvb[:, h * LANES:(h + 1) * LANES].astype(BF16)
        kf_o[:, lo + LANES:lo + 2 * LANES] = kpe
    v_o[...] = kvb[:, nope_w:].astype(BF16)


def mla_prep(pa, pf, positions, q_norm, kv_norm, wq_all, wkv, *, tm=512):
    tokens = pa.shape[0]
    tm = min(tm, tokens)
    half = ML_ROPE // 2
    inv_freq = ROPE_THETA ** (-jnp.arange(half, dtype=F32) / half)
    freq = jnp.concatenate([inv_freq, inv_freq, jnp.zeros((LANES - ML_ROPE,), F32)]).reshape(1, LANES)

    def full(a):
        return pl.BlockSpec(a.shape, lambda i: (0, 0))

    qn_w = q_norm.reshape(1, -1)
    kvn_w = kv_norm.reshape(1, -1)
    wide = ML_HEADS * ML_QK_PAD
    return pl.pallas_call(
        functools.partial(_mla_prep_kernel, qscale=(ML_NOPE + ML_ROPE) ** -0.5 * LOG2E),
        grid=(tokens // tm,),
        in_specs=[pl.BlockSpec((tm, ML_Q_RANK), lambda i: (i, OFF_MQ // ML_Q_RANK)),
                  pl.BlockSpec((tm, ML_KV_RANK), lambda i: (i, OFF_MKV // ML_KV_RANK)),
                  pl.BlockSpec((tm, LANES), lambda i: (i, OFF_KPE // LANES)),
                  pl.BlockSpec((tm, 1), lambda i: (i, 0)),
                  full(qn_w), full(kvn_w), full(wq_all), full(wkv), full(freq)],
        out_specs=[pl.BlockSpec((tm, wide), lambda i: (i, 0)),
                   pl.BlockSpec((tm, wide), lambda i: (i, 0)),
                   pl.BlockSpec((tm, ML_DIM), lambda i: (i, 0))],
        out_shape=[jax.ShapeDtypeStruct((tokens, wide), BF16),
                   jax.ShapeDtypeStruct((tokens, wide), BF16),
                   jax.ShapeDtypeStruct((tokens, ML_DIM), BF16)],
        compiler_params=_cparams(("parallel",)),
        name="mla_prep",
    )(pa, pa, pf, positions.reshape(tokens, 1), qn_w, kvn_w, wq_all, wkv, freq)


def _mla_attn_kernel(q_ref, k_ref, v_ref, o_ref, vt_ref, *, tq, tk):
    i = pl.program_id(1)
    wq = ML_QK_PAD
    vblk = ML_V + ONES_ROWS

    @pl.when(i == 0)
    def _():
        _transpose_into(vt_ref, v_ref, tk, ML_HEADS, ML_V)

    n_tiles = (i * tq + tq - 1) // tk + 1
    n_full = (i * tq + 1) // tk
    q_idx = i * tq + lax.broadcasted_iota(jnp.int32, (tk, tq), 1)
    k_off = lax.broadcasted_iota(jnp.int32, (tk, tq), 0)
    qh = [q_ref[:, h * wq:(h + 1) * wq] for h in range(ML_HEADS)]

    def tiles(j, h):
        off = pl.multiple_of(j * tk, tk)
        return (k_ref[pl.ds(off, tk), h * wq:(h + 1) * wq],
                vt_ref[h * vblk:(h + 1) * vblk, pl.ds(off, tk)], off)

    heads = range(ML_HEADS)

    def full_body(j, st):
        kv = [tiles(j, h) for h in heads]
        s = [_dot_t(kv[h][0], qh[h]) for h in heads]
        return _softmax_tiles(s, [0.0] * ML_HEADS, st, [kv[h][1] for h in heads])

    def diag_body(j, st):
        kv = [tiles(j, h) for h in heads]
        keep = q_idx >= kv[0][2] + k_off
        s = [jnp.where(keep, _dot_t(kv[h][0], qh[h]), NEG_BIG) for h in heads]
        return _softmax_tiles(s, [0.0] * ML_HEADS, st, [kv[h][1] for h in heads])

    st = tuple((jnp.full((1, tq), NEG_BIG, F32), jnp.zeros((vblk, tq), F32)) for _ in range(ML_HEADS))
    st = lax.fori_loop(0, n_full, full_body, st)
    st = lax.fori_loop(n_full, n_tiles, diag_body, st)
    for h in range(ML_HEADS):
        acc = st[h][1]
        o_ref[:, h * ML_V:(h + 1) * ML_V] = (acc[:ML_V] / acc[ML_V:ML_V + 1]).T.astype(o_ref.dtype)


def mla_attention(qf, kf, v, batch, *, tq=ATTN_TILE):
    tokens = qf.shape[0]
    seq = tokens // batch
    tq = min(tq, seq)
    tk = tq
    nq = seq // tq
    wide = qf.shape[1]
    return pl.pallas_call(
        functools.partial(_mla_attn_kernel, tq=tq, tk=tk),
        grid=(batch, nq),
        in_specs=[pl.BlockSpec((tq, wide), lambda b, i: (b * nq + i, 0)),
                  pl.BlockSpec((seq, wide), lambda b, i: (b, 0)),
                  pl.BlockSpec((seq, ML_DIM), lambda b, i: (b, 0))],
        out_specs=pl.BlockSpec((tq, ML_DIM), lambda b, i: (b * nq + i, 0)),
        out_shape=jax.ShapeDtypeStruct((tokens, ML_DIM), BF16),
        scratch_shapes=[pltpu.VMEM((ML_HEADS * (ML_V + ONES_ROWS), seq), BF16)],
        compiler_params=_cparams(("arbitrary", "arbitrary")),
        name="mla_attention",
    )(qf, kf, v)


def _cross_kernel(*refs, n_mix):
    post_refs, refs = refs[:7], refs[7:]
    y_refs, w_refs = refs[:n_mix - 1], refs[n_mix - 1:2 * n_mix - 1]
    x_ref, nw_ref, wq_ref, kv_ref, wo_ref, o_ref, ya_buf = refs[2 * n_mix - 1:]
    _rwkv_post_kernel(*post_refs, ya_buf)
    x = x_ref[...]
    for y_ref, w_ref in zip((ya_buf,) + tuple(y_refs), w_refs):
        x = x + _dot(y_ref[...], w_ref[...])
    q = _dot(_rms(x, nw_ref[...], NORM_EPS).astype(BF16), wq_ref[...])
    kv = kv_ref[...]
    scale = CA_HEAD_DIM ** -0.5
    heads = [slice(hh * CA_HEAD_DIM, (hh + 1) * CA_HEAD_DIM) for hh in range(CA_HEADS)]
    s = [_dot_t(q[:, sl].astype(BF16), kv[:, sl]) * scale for sl in heads]
    p = [jnp.exp(x - jnp.max(x, axis=-1, keepdims=True)) for x in s]
    p = [(x / jnp.sum(x, axis=-1, keepdims=True)).astype(BF16) for x in p]
    outs = [_dot(x, kv[:, CA_DIM + sl.start:CA_DIM + sl.stop]) for x, sl in zip(p, heads)]
    o = jnp.concatenate(outs, axis=1).astype(BF16)
    o_ref[...] = x + _dot(o, wo_ref[...])


def mix_cross_block(rwkv_raw, y_list, w_list, x, batch, norm_w, wq, kv, wo, *, tq=512):
    tokens, d = x.shape
    seq = tokens // batch
    tq = min(tq, seq)
    nq = seq // tq
    mem_len = kv.shape[0] // batch
    o, bonus, gate, ln_w, ln_b, seg, seg_t = rwkv_raw
    dr = o.shape[1]
    rows = lambda width: pl.BlockSpec((tq, width), lambda b, i: (b * nq + i, 0))
    whole = lambda a: pl.BlockSpec(a.shape, lambda b, i: (0, 0))
    vec = lambda width: pl.BlockSpec((1, width), lambda b, i: (0, 0))
    return pl.pallas_call(
        functools.partial(_cross_kernel, n_mix=len(w_list)),
        grid=(batch, nq),
        in_specs=[rows(dr), rows(dr), rows(dr), vec(dr), vec(dr), whole(seg), whole(seg_t)]
                 + [rows(y.shape[1]) for y in y_list] + [whole(w) for w in w_list]
                 + [rows(d), vec(d), whole(wq),
                    pl.BlockSpec((mem_len, 2 * CA_DIM), lambda b, i: (b, 0)), whole(wo)],
        out_specs=rows(d),
        out_shape=jax.ShapeDtypeStruct((tokens, d), F32),
        scratch_shapes=[pltpu.VMEM((tq, dr), BF16)],
        compiler_params=_cparams(("parallel", "parallel")),
        name="mix_cross_block",
    )(o, bonus, gate, ln_w.reshape(1, dr), ln_b.reshape(1, dr), seg, seg_t,
      *y_list, *w_list, x, norm_w.reshape(1, d), wq, kv, wo)


SEL_E1, SEL_E2, SEL_G1, SEL_G2 = 0, 1, 2, 3


def _route(logits, b_router):
    biased = logits + b_router
    lane = lax.broadcasted_iota(jnp.int32, logits.shape, 1)
    big = jnp.int32(LANES)

    def first_argmax(vals):
        mx = jnp.max(vals, axis=-1, keepdims=True)
        return jnp.min(jnp.where(vals == mx, lane, big), axis=-1, keepdims=True)

    def pick(vals, idx):
        return jnp.sum(jnp.where(lane == idx, vals, 0.0), axis=-1, keepdims=True)

    is_group = (lane >= MOE_EXPERTS) & (lane < MOE_EXPERTS + MOE_GROUPS)
    gl = jnp.where(is_group, logits, NEG_BIG)
    ge = jnp.exp(gl - jnp.max(gl, axis=-1, keepdims=True))
    gp = ge / jnp.sum(ge, axis=-1, keepdims=True)
    g_lane = first_argmax(jnp.where(is_group, biased, NEG_BIG))
    p_group = pick(gp, g_lane)
    lo = (g_lane - MOE_EXPERTS) * MOE_PER_GROUP
    in_group = (lane >= lo) & (lane < lo + MOE_PER_GROUP)
    eb = jnp.where(in_group, biased, NEG_BIG)
    i1 = first_argmax(eb)
    i2 = first_argmax(jnp.where(lane == i1, NEG_BIG, eb))
    l1, l2 = pick(logits, i1), pick(logits, i2)
    mx = jnp.maximum(l1, l2)
    e1, e2 = jnp.exp(l1 - mx), jnp.exp(l2 - mx)
    w1, w2 = e1 / (e1 + e2), e2 / (e1 + e2)
    return jnp.where(lane == SEL_E1, i1.astype(F32),
                     jnp.where(lane == SEL_E2, i2.astype(F32),
                               jnp.where(lane == SEL_G1, w1 * p_group,
                                         jnp.where(lane == SEL_G2, w2 * p_group, 0.0))))


def _router_kernel(x_ref, nw_ref, wr_ref, br_ref, ltri_ref, h_ref, sel_ref, rank_ref, counts_ref, carry_ref):
    h = _rms(x_ref[...], nw_ref[...], NORM_EPS)
    h_ref[...] = h
    sel_ref[...] = _route(_dot_x3(h, wr_ref[...]), br_ref[...])
    _moe_rank_kernel(sel_ref, ltri_ref, rank_ref, counts_ref, carry_ref)


def moe_router(x, norm_w, w_router, b_router, *, tm=512):
    tokens, d = x.shape
    tm = min(tm, tokens)
    ltri = (jnp.arange(tm)[:, None] > jnp.arange(tm)[None, :]).astype(BF16)
    row = lambda width: pl.BlockSpec((tm, width), lambda i: (i, 0))
    return pl.pallas_call(
        _router_kernel,
        grid=(tokens // tm,),
        in_specs=[row(d),
                  pl.BlockSpec((1, d), lambda i: (0, 0)),
                  pl.BlockSpec((d, LANES), lambda i: (0, 0)),
                  pl.BlockSpec((1, LANES), lambda i: (0, 0)),
                  pl.BlockSpec((tm, tm), lambda i: (0, 0))],
        out_specs=[row(d), row(LANES), row(LANES), pl.BlockSpec((1, LANES), lambda i: (0, 0))],
        out_shape=[jax.ShapeDtypeStruct((tokens, d), F32),
                   jax.ShapeDtypeStruct((tokens, LANES), F32),
                   jax.ShapeDtypeStruct((tokens, LANES), jnp.int32),
                   jax.ShapeDtypeStruct((1, LANES), jnp.int32)],
        scratch_shapes=[pltpu.VMEM((1, LANES), F32)],
        compiler_params=_cparams(("arbitrary",)),
        name="moe_router",
    )(x, norm_w.reshape(1, d), w_router, b_router, ltri)


def _moe_rank_kernel(sel_ref, ltri_ref, rank_ref, counts_ref, carry_ref):
    @pl.when(pl.program_id(0) == 0)
    def _():
        carry_ref[...] = jnp.zeros_like(carry_ref)

    sel = sel_ref[...]
    lane = lax.broadcasted_iota(jnp.int32, sel.shape, 1)
    lane_f = lane.astype(F32)
    oh1 = lane_f == sel[:, SEL_E1:SEL_E1 + 1]
    oh2 = lane_f == sel[:, SEL_E2:SEL_E2 + 1]
    f1, f2 = oh1.astype(F32), oh2.astype(F32)
    ltri = ltri_ref[...]
    before1 = _dot(ltri, f1.astype(BF16))
    before2 = _dot(ltri, f2.astype(BF16))
    c1 = jnp.sum(f1, axis=0, keepdims=True)
    c2 = jnp.sum(f2, axis=0, keepdims=True)
    carry = carry_ref[...]
    r1 = jnp.sum(jnp.where(oh1, before1 + carry, 0.0), axis=1, keepdims=True)
    r2 = jnp.sum(jnp.where(oh2, before2 + carry + c1, 0.0), axis=1, keepdims=True)
    rank_ref[...] = jnp.where(lane == SEL_E1, r1, jnp.where(lane == SEL_E2, r2, 0.0)).astype(jnp.int32)
    total = carry + c1 + c2
    carry_ref[...] = total
    counts_ref[...] = total.astype(jnp.int32)


def _row_copy(src_ref, src_row, dst_ref, dst_row, sem):
    return pltpu.make_async_copy(src_ref.at[pl.ds(src_row, 1)], dst_ref.at[pl.ds(dst_row, 1)], sem)


def _moe_dispatch_kernel(dest_ref, tail_ref, h_ref, xs_ref, hbuf, zero_ref, lsem, ssem, zsem, *, tm):
    step = pl.program_id(0)
    n_steps = pl.num_programs(0)
    base = step * (2 * tm)
    slot = step % 3

    def load(tile, buf_slot):
        return pltpu.make_async_copy(h_ref.at[pl.ds(pl.multiple_of(tile * tm, tm), tm)], hbuf.at[buf_slot],
                                     lsem.at[buf_slot])

    def wait_scatter(buf_slot):
        for _ in range(2):
            pltpu.make_async_copy(hbuf.at[buf_slot], xs_ref.at[pl.ds(0, tm)], ssem.at[buf_slot]).wait()

    @pl.when(step == 0)
    def _():
        load(0, 0).start()

        @pl.when(n_steps > 1)
        def _():
            load(1, 1).start()

    @pl.when(step == 0)
    def _():
        zero_ref[...] = jnp.zeros_like(zero_ref)

        def fill(tail):
            return pltpu.make_async_copy(zero_ref, xs_ref.at[pl.ds(pl.multiple_of(tail, 8), zero_ref.shape[0])], zsem)

        def start(e, carry):
            @pl.when(tail_ref[e] >= 0)
            def _():
                fill(tail_ref[e]).start()
            return carry

        def wait(e, carry):
            @pl.when(tail_ref[e] >= 0)
            def _():
                fill(tail_ref[e]).wait()
            return carry

        lax.fori_loop(0, tail_ref.shape[0], start, 0)
        lax.fori_loop(0, tail_ref.shape[0], wait, 0)

    load(step, slot).wait()

    def issue(r, carry):
        for s in range(2):
            _row_copy(hbuf.at[slot], r, xs_ref, dest_ref[base + 2 * r + s], ssem.at[slot]).start()
        return carry

    lax.fori_loop(0, tm, issue, 0, unroll=DMA_ISSUE_UNROLL)

    @pl.when(step >= 1)
    def _():
        wait_scatter((step + 2) % 3)

    @pl.when(step + 2 < n_steps)
    def _():
        load(step + 2, (step + 2) % 3).start()

    @pl.when(step == n_steps - 1)
    def _():
        wait_scatter(slot)


def moe_dispatch(h, dest, tails, rows, row_tile, *, tm=256):
    tokens, d = h.shape
    tm = min(tm, tokens)
    grid_spec = pltpu.PrefetchScalarGridSpec(
        num_scalar_prefetch=2,
        grid=(tokens // tm,),
        in_specs=[pl.BlockSpec(memory_space=pl.ANY)],
        out_specs=pl.BlockSpec(memory_space=pl.ANY),
        scratch_shapes=[pltpu.VMEM((3, tm, d), h.dtype), pltpu.VMEM((row_tile, d), h.dtype),
                        pltpu.SemaphoreType.DMA((3,)), pltpu.SemaphoreType.DMA((3,)),
                        pltpu.SemaphoreType.DMA(())],
    )
    return pl.pallas_call(
        functools.partial(_moe_dispatch_kernel, tm=tm),
        grid_spec=grid_spec,
        out_shape=jax.ShapeDtypeStruct((rows, d), h.dtype),
        compiler_params=_cparams(("arbitrary",)),
        name="moe_dispatch",
    )(dest, tails, h)


def _moe_expert_kernel(te_ref, nused_ref, xs_ref, wg_ref, wu_ref, wd_ref, ys_ref, wgb, wub, wdb):
    r = pl.program_id(0)
    used = r < nused_ref[0]
    changed = (r == 0) | (te_ref[r] != te_ref[jnp.maximum(r - 1, 0)])

    @pl.when(used & changed)
    def _():
        wgb[...] = wg_ref[...].astype(BF16)
        wub[...] = wu_ref[...].astype(BF16)
        wdb[...] = wd_ref[...].astype(BF16)

    @pl.when(used)
    def _():
        x = xs_ref[...].astype(BF16)
        gate_pre = _dot(x, wgb[...])
        hid = (gate_pre * jax.nn.sigmoid(gate_pre)) * _dot(x, wub[...])
        ys_ref[...] = _dot(hid.astype(BF16), wdb[...])

    @pl.when(jnp.logical_not(used))
    def _():
        ys_ref[...] = jnp.zeros_like(ys_ref)


def moe_experts(xs, tile_expert, n_used, wg, wu, wd, layer, *, tm):
    rows, d = xs.shape
    de = wg.shape[-1]
    grid_spec = pltpu.PrefetchScalarGridSpec(
        num_scalar_prefetch=2,
        grid=(rows // tm,),
        in_specs=[pl.BlockSpec((tm, d), lambda r, te, nu: (jnp.minimum(r, nu[0] - 1), 0)),
                  pl.BlockSpec((None, None, d, de), lambda r, te, nu: (layer, te[r], 0, 0)),
                  pl.BlockSpec((None, None, d, de), lambda r, te, nu: (layer, te[r], 0, 0)),
                  pl.BlockSpec((None, None, de, d), lambda r, te, nu: (layer, te[r], 0, 0))],
        out_specs=pl.BlockSpec((tm, d), lambda r, te, nu: (r, 0)),
        scratch_shapes=[pltpu.VMEM((d, de), BF16), pltpu.VMEM((d, de), BF16), pltpu.VMEM((de, d), BF16)],
    )
    return pl.pallas_call(
        _moe_expert_kernel,
        grid_spec=grid_spec,
        out_shape=jax.ShapeDtypeStruct((rows, d), F32),
        compiler_params=_cparams(("arbitrary",)),
        name="moe_experts",
    )(tile_expert, n_used, xs, wg, wu, wd)


def _moe_combine_kernel(dest_ref, x_ref, sel_ref, nw_ref, ys_ref, o_ref, buf_ref, sem, *, tm, normalize):
    step = pl.program_id(0)
    slot = step % 2

    def gather(tile, buf_slot):
        base = tile * (2 * tm)

        def issue(r, carry):
            for s in range(2):
                _row_copy(ys_ref, dest_ref[base + 2 * r + s], buf_ref.at[buf_slot, s], r, sem.at[buf_slot]).start()
            return carry

        lax.fori_loop(0, tm, issue, 0, unroll=DMA_ISSUE_UNROLL)

    @pl.when(step == 0)
    def _():
        gather(0, 0)

    @pl.when(step + 1 < pl.num_programs(0))
    def _():
        gather(step + 1, 1 - slot)

    for s in range(2):
        pltpu.make_async_copy(ys_ref.at[pl.ds(0, tm)], buf_ref.at[slot, s], sem.at[slot]).wait()
    sel = sel_ref[...]
    out = x_ref[...] + sel[:, SEL_G1:SEL_G1 + 1] * buf_ref[slot, 0] + sel[:, SEL_G2:SEL_G2 + 1] * buf_ref[slot, 1]
    o_ref[...] = _rms(out, nw_ref[...], NORM_EPS) if normalize else out


def moe_combine(x, sel, ys, dest, final_norm=None, *, tm=256):
    tokens, d = x.shape
    tm = min(tm, tokens)
    normalize = final_norm is not None
    nw = (final_norm if normalize else jnp.ones((d,), F32)).reshape(1, d)
    grid_spec = pltpu.PrefetchScalarGridSpec(
        num_scalar_prefetch=1,
        grid=(tokens // tm,),
        in_specs=[pl.BlockSpec((tm, d), lambda i, *_: (i, 0)),
                  pl.BlockSpec((tm, LANES), lambda i, *_: (i, 0)),
                  pl.BlockSpec((1, d), lambda i, *_: (0, 0)),
                  pl.BlockSpec(memory_space=pl.ANY)],
        out_specs=pl.BlockSpec((tm, d), lambda i, *_: (i, 0)),
        scratch_shapes=[pltpu.VMEM((2, 2, tm, d), F32), pltpu.SemaphoreType.DMA((2,))],
    )
    return pl.pallas_call(
        functools.partial(_moe_combine_kernel, tm=tm, normalize=normalize),
        grid_spec=grid_spec,
        out_shape=jax.ShapeDtypeStruct((tokens, d), F32),
        compiler_params=_cparams(("arbitrary",)),
        name="moe_combine",
    )(dest, x, sel, nw, ys)


def moe_block(x, norm_w, w_router, b_router, wg, wu, wd, layer, final_norm=None, *, tm=MOE_ROW_TILE):
    tokens, d = x.shape
    n_exp = wg.shape[1]
    h, sel, rank, counts = moe_router(x, norm_w, w_router, b_router)
    padded = (counts[0, :n_exp] + (tm - 1)) // tm * tm
    ends = jnp.cumsum(padded)
    starts = ends - padded
    experts = sel[:, SEL_E1:SEL_E2 + 1].astype(jnp.int32)
    start_of = jnp.sum(jnp.where(experts[..., None] == jnp.arange(n_exp, dtype=jnp.int32), starts, 0), axis=-1)
    dest = (start_of + rank[:, SEL_E1:SEL_E2 + 1]).reshape(-1)
    rows = 2 * tokens + n_exp * tm
    tile_start = jnp.arange(rows // tm, dtype=jnp.int32) * tm
    tile_expert = jnp.minimum(jnp.sum(tile_start[:, None] >= ends[None, :], axis=1), n_exp - 1).astype(jnp.int32)
    n_used = (ends[-1] // tm).astype(jnp.int32).reshape(1)
    tails = jnp.concatenate([jnp.where(padded > 0, ends - tm, -1),
                             jnp.where(tile_start >= ends[-1], tile_start, -1)]).astype(jnp.int32)
    xs = moe_dispatch(h, dest, tails, rows, tm)
    ys = moe_experts(xs, tile_expert, n_used, wg, wu, wd, layer, tm=tm)
    return moe_combine(x, sel, ys, dest, final_norm)


def _proj_weights(w_in_l, w_vres_l):
    d = w_in_l.shape[0]
    mla0 = RW_COLS + DF_COLS
    vres = jnp.zeros((d, RW_V_RANK), F32) if w_vres_l is None else w_vres_l
    part_f = [w_in_l[:, 3 * RW_DIM:RW_COLS],
              vres, jnp.zeros((d, LANES - RW_V_RANK), F32),
              w_in_l[:, mla0 + ML_Q_RANK + ML_KV_RANK:mla0 + ML_COLS], jnp.zeros((d, LANES - ML_ROPE), F32)]
    part_a = [w_in_l[:, :3 * RW_DIM],
              w_in_l[:, RW_COLS:RW_COLS + DF_COLS],
              w_in_l[:, mla0 + ML_Q_RANK:mla0 + ML_Q_RANK + ML_KV_RANK],
              jnp.zeros((d, OFF_MQ - OFF_MKV - ML_KV_RANK), F32),
              w_in_l[:, mla0:mla0 + ML_Q_RANK]]
    return jnp.concatenate(part_f, axis=1).astype(BF16), jnp.concatenate(part_a, axis=1).astype(BF16)


def _pad_rows(w, rows, at=0):
    out = jnp.zeros((rows, w.shape[1]), w.dtype)
    return lax.dynamic_update_slice(out, w, (at, 0))


def kernel(x, mem, positions, rel_bias, final_norm, norm_mix, w_in, w_in_vres, w_out, tm_mu, tm_mu_vres, tm_w0, tm_w2, tm_a0, tm_a2, tm_v0, tm_v2, tm_g2, tm_k_k, tm_k_a, tm_r_k, tm_ln_w, tm_ln_b, da_lq1, da_lk1, da_lq2, da_lk2, da_subln, mla_q_norm, mla_wq_b, mla_kv_norm, mla_wkv_b, norm_cross, norm_mem, ca_wq, ca_wkv, ca_wo, norm_ffn, moe_w_group, moe_b_group, moe_w_expert, moe_b_expert, moe_w_gate, moe_w_up, moe_w_down):
    batch, seq, d = x.shape
    tokens = batch * seq
    depth = norm_mix.shape[0]
    xf = x.reshape(tokens, d)
    memf = mem.reshape(-1, d)
    positions = positions.astype(jnp.int32)

    head_of_lane = jnp.arange(RW_DIM) // RW_HEAD_DIM
    seg = (head_of_lane[:, None] == jnp.arange(LANES)[None, :]).astype(BF16)
    seg_t = seg.T
    row = lambda v: v.reshape(1, -1)

    v_first = None
    for l in range(depth):
        w_f, w_a = _proj_weights(w_in[l], None if l == 0 else w_in_vres[l - 1])
        pa, proj = norm_matmul(xf, norm_mix[l], w_a, w_f, out_dtype=BF16, tm=1024, tn=PROJ_A_COLS // 3)

        mu = tm_mu[l]
        prm = dict(mu_r=row(mu[:RW_DIM]), mu_k=row(mu[RW_DIM:2 * RW_DIM]), mu_v=row(mu[2 * RW_DIM:3 * RW_DIM]),
                   mu_l=row(mu[3 * RW_DIM:]), w0=row(tm_w0[l]), a0=row(tm_a0[l]),
                   w2=_pad_rows(tm_w2[l], LANES, 0), a2=_pad_rows(tm_a2[l], LANES, RW_W_RANK),
                   g2=tm_g2[l].astype(BF16), k_k=row(tm_k_k[l]), k_a=row(tm_k_a[l]), r_k=row(tm_r_k[l]),
                   seg=seg, seg_t=seg_t)
        if l > 0:
            prm.update(mu_vr=jnp.pad(row(tm_mu_vres[l - 1]), ((0, 0), (0, LANES - RW_V_RANK))),
                       v0=row(tm_v0[l - 1]), v2=_pad_rows(tm_v2[l - 1], LANES, 0))
        r, lw, k, v, kap, beta, gate, bonus = rwkv_prep(pa, proj, batch, v_first, prm)
        if l == 0:
            v_first = v
        o = rwkv_scan(r, lw, k, v, kap, beta, batch)
        rwkv_raw = (o, bonus, gate, tm_ln_w[l], tm_ln_b[l], seg, seg_t)

        lambda_init = 0.8 - 0.6 * math.exp(-0.3 * l)
        lam = (jnp.exp(jnp.sum(da_lq1[l] * da_lk1[l])) - jnp.exp(jnp.sum(da_lq2[l] * da_lk2[l])) + lambda_init)
        y_b = diff_attention(pa, positions, rel_bias, lam, lambda_init, da_subln[l])

        wq = mla_wq_b[l].reshape(ML_Q_RANK, ML_HEADS, ML_NOPE + ML_ROPE)
        wq_pe = jnp.pad(wq[:, :, ML_NOPE:], ((0, 0), (0, 0), (0, LANES - ML_ROPE)))
        wq_all = jnp.concatenate([wq[:, :, :ML_NOPE].reshape(ML_Q_RANK, -1),
                                  wq_pe.reshape(ML_Q_RANK, -1)], axis=1).astype(BF16)
        wkv = mla_wkv_b[l].reshape(ML_KV_RANK, ML_HEADS, ML_NOPE + ML_V)
        wkv = jnp.concatenate([wkv[:, :, :ML_NOPE].reshape(ML_KV_RANK, -1),
                               wkv[:, :, ML_NOPE:].reshape(ML_KV_RANK, -1)], axis=1).astype(BF16)
        qf, kf, v_mla = mla_prep(pa, proj, positions, mla_q_norm[l], mla_kv_norm[l], wq_all, wkv)
        y_c = mla_attention(qf, kf, v_mla, batch)

        wo = w_out[l].astype(BF16)
        kv_mem = norm_matmul(memf, norm_mem[l], ca_wkv[l].astype(BF16), out_dtype=BF16)
        xf = mix_cross_block(rwkv_raw, [y_b, y_c], [wo[:RW_DIM], wo[RW_DIM:RW_DIM + DF_DIM], wo[RW_DIM + DF_DIM:]],
                             xf, batch, norm_cross[l], ca_wq[l].astype(BF16), kv_mem, ca_wo[l].astype(BF16))

        w_router = jnp.concatenate(
            [moe_w_expert[l], moe_w_group[l], jnp.zeros((d, LANES - MOE_EXPERTS - MOE_GROUPS), F32)], axis=1)
        b_router = jnp.concatenate(
            [moe_b_expert[l], moe_b_group[l], jnp.zeros((LANES - MOE_EXPERTS - MOE_GROUPS,), F32)]).reshape(1, LANES)
        xf = moe_block(xf, norm_ffn[l], w_router, b_router, moe_w_gate, moe_w_up, moe_w_down, l,
                       final_norm if l == depth - 1 else None)

    return xf.reshape(batch, seq, d)
```
